```python
import math
import jax
import jax.numpy as jnp
from jax import lax
import numpy as np


D_MODEL = 1024
BATCH = 4
SEQ = 8192
DEPTH = 1

NORM_EPS = 1e-6
NEG_INF = -1e30
Q_BLOCK = 128

DA_HEADS = 8
DA_DIM = 64
DA_VDIM = 2 * DA_DIM
DA_SUBLN_EPS = 1e-5

NSA_HEADS = 8
NSA_GROUPS = 2
NSA_HPG = NSA_HEADS // NSA_GROUPS
NSA_DIM = 64
CMP_BLOCK = 32
CMP_STRIDE = 16
CMP_HIDDEN = 128
SEL_BLOCK = 64
SEL_TOPK = 16
WINDOW = 512
FORCE_BONUS = 1e4

N_BRANCH = 2
N_ALIBI_HEADS = DA_HEADS + NSA_HEADS

N_EXPERTS = 64
MOE_TOPK = 8
N_EXPERT_GROUPS = 8
TOPK_EXPERT_GROUPS = 4
EXPERT_HIDDEN = 256
SHARED_HIDDEN = 256
ROUTED_SCALE = 2.5
MOE_BLOCK = 128

IN_SIZES = (
    DA_HEADS * 2 * DA_DIM,
    DA_HEADS * 2 * DA_DIM,
    DA_HEADS * DA_VDIM,
    NSA_HEADS * NSA_DIM,
    NSA_GROUPS * NSA_DIM,
    NSA_GROUPS * NSA_DIM,
    NSA_GROUPS * NSA_DIM,
    NSA_GROUPS * NSA_DIM,
    NSA_GROUPS * NSA_DIM,
    NSA_GROUPS * NSA_DIM,
    NSA_HEADS * 3,
    N_BRANCH * D_MODEL,
)
IN_COLS = 6424

kernel_name = 'hybrid_diffattn_nsa_moe_block'


def _lambda_init(layer):
    return 0.8 - 0.6 * math.exp(-0.3 * layer)


def _alibi_slopes():
    i = jnp.arange(1, N_ALIBI_HEADS + 1, dtype=jnp.float32)
    return 2.0 ** (-8.0 * i / N_ALIBI_HEADS)


def _rms(x, eps):
    xf = x.astype(jnp.float32)
    return xf * lax.rsqrt(jnp.mean(xf * xf, axis=-1, keepdims=True) + eps)


def _modulate(x, w, shift, scale):
    y = _rms(x, NORM_EPS) * w.astype(jnp.float32)
    return (y * (1.0 + scale[:, None, :]) + shift[:, None, :]).astype(x.dtype)


def _swiglu(x, w_gate, w_up, w_down):
    return (jax.nn.silu(x @ w_gate) * (x @ w_up)) @ w_down


def diff_attention(q, k, v, lam, slopes):
    B, H, _, S, _ = q.shape
    key_pos = jnp.arange(S)
    m = slopes[None, :, None, None, None]

    def one_block(i):
        q0 = i * Q_BLOCK
        qb = lax.dynamic_slice_in_dim(q, q0, Q_BLOCK, axis=3)
        dist = (q0 + jnp.arange(Q_BLOCK))[:, None] - key_pos[None, :]
        s = jnp.einsum('bhcqd,bhckd->bhcqk', qb, k) - m * dist.astype(jnp.float32)
        p = jax.nn.softmax(jnp.where(dist >= 0, s, NEG_INF), axis=-1)
        a = p[:, :, 0] - lam * p[:, :, 1]
        return jnp.einsum('bhqk,bhkd->bhqd', a, v)

    o = lax.map(one_block, jnp.arange(S // Q_BLOCK))
    return o.transpose(1, 2, 0, 3, 4).reshape(B, H, S, -1)


def compress_blocks(k, pe, w1, w2):
    B, G, S, Dh = k.shape
    n_cmp = (S - CMP_BLOCK) // CMP_STRIDE + 1
    idx = jnp.arange(n_cmp)[:, None] * CMP_STRIDE + jnp.arange(CMP_BLOCK)[None, :]
    blk = k[:, :, idx, :] + pe.astype(jnp.float32)
    flat = blk.reshape(B, G, n_cmp, CMP_BLOCK * Dh)
    return jax.nn.gelu(flat @ w1.astype(jnp.float32)) @ w2.astype(jnp.float32)


def _cmp_to_sel(n_cmp, n_sel):
    c0 = jnp.arange(n_cmp)[:, None] * CMP_STRIDE
    s0 = jnp.arange(n_sel)[None, :] * SEL_BLOCK
    ov = jnp.minimum(c0 + CMP_BLOCK, s0 + SEL_BLOCK) - jnp.maximum(c0, s0)
    return jnp.clip(ov, 0).astype(jnp.float32) / CMP_BLOCK


def nsa_attention(q, kc, vc, ks, vs, kw, vw, gates, slopes):
    B, G, Hg, S, Dh = q.shape
    n_cmp = kc.shape[2]
    n_sel = S // SEL_BLOCK
    n_top = min(SEL_TOPK, n_sel)
    cmp_pos = jnp.arange(n_cmp) * CMP_STRIDE + CMP_BLOCK - 1
    cmp_to_sel = _cmp_to_sel(n_cmp, n_sel)
    ks_blk = ks.reshape(B, G, n_sel, SEL_BLOCK, Dh)
    vs_blk = vs.reshape(B, G, n_sel, SEL_BLOCK, Dh)
    pad = ((0, 0), (0, 0), (WINDOW, 0), (0, 0))
    kw_pad = jnp.pad(kw, pad)
    vw_pad = jnp.pad(vw, pad)
    b_ix = jnp.arange(B)[:, None, None, None]
    g_ix = jnp.arange(G)[None, :, None, None]
    blk_ids = jnp.arange(n_sel)
    m = slopes[None, :, :, None, None]

    def one_block(i):
        q0 = i * Q_BLOCK
        t = q0 + jnp.arange(Q_BLOCK)
        qb = lax.dynamic_slice_in_dim(q, q0, Q_BLOCK, axis=3)
        gb = lax.dynamic_slice_in_dim(gates, q0, Q_BLOCK, axis=3)
        d_c = t[:, None] - cmp_pos[None, :]
        ok_c = d_c >= 0
        s_c = jnp.einsum('bghqd,bgcd->bghqc', qb, kc) - m * d_c.astype(jnp.float32)
        p_c = jax.nn.softmax(jnp.where(ok_c, s_c, NEG_INF), axis=-1) * ok_c
        o_c = jnp.einsum('bghqc,bgcd->bghqd', p_c, vc)
        imp = jnp.einsum('bghqc,cn->bgqn', p_c, cmp_to_sel)
        cur = t // SEL_BLOCK
        forced = (blk_ids[None, :] == 0) | (blk_ids[None, :] == cur[:, None]) | (blk_ids[None, :] == cur[:, None] - 1)
        causal_blk = blk_ids[None, :] * SEL_BLOCK <= t[:, None]
        sel_score = jnp.where(causal_blk, imp + FORCE_BONUS * forced, NEG_INF)
        _, sel = lax.top_k(sel_score, n_top)
        k_sel = ks_blk[b_ix, g_ix, sel]
        v_sel = vs_blk[b_ix, g_ix, sel]
        tok = sel[..., None] * SEL_BLOCK + jnp.arange(SEL_BLOCK)
        d_s = (t[:, None, None] - tok)[:, :, None]
        s_s = jnp.einsum('bghqd,bgqnld->bghqnl', qb, k_sel) - m[..., None] * d_s.astype(jnp.float32)
        s_s = jnp.where(d_s >= 0, s_s, NEG_INF).reshape(B, G, Hg, Q_BLOCK, n_top * SEL_BLOCK)
        p_s = jax.nn.softmax(s_s, axis=-1).reshape(B, G, Hg, Q_BLOCK, n_top, SEL_BLOCK)
        o_s = jnp.einsum('bghqnl,bgqnld->bghqd', p_s, v_sel)
        k_w = lax.dynamic_slice_in_dim(kw_pad, q0, Q_BLOCK + WINDOW, axis=2)
        v_w = lax.dynamic_slice_in_dim(vw_pad, q0, Q_BLOCK + WINDOW, axis=2)
        key_pos = q0 - WINDOW + jnp.arange(Q_BLOCK + WINDOW)
        d_w = t[:, None] - key_pos[None, :]
        ok_w = (d_w >= 0) & (d_w < WINDOW) & (key_pos[None, :] >= 0)
        s_w = jnp.einsum('bghqd,bgkd->bghqk', qb, k_w) - m * d_w.astype(jnp.float32)
        p_w = jax.nn.softmax(jnp.where(ok_w, s_w, NEG_INF), axis=-1)
        o_w = jnp.einsum('bghqk,bgkd->bghqd', p_w, v_w)
        return gb[..., 0:1] * o_c + gb[..., 1:2] * o_s + gb[..., 2:3] * o_w

    o = lax.map(one_block, jnp.arange(S // Q_BLOCK))
    return o.transpose(1, 0, 4, 2, 3, 5).reshape(B, S, G * Hg * Dh)


def hybrid_mixer(h, w_in, lq1, lk1, lq2, lk2, subln_w, ck_pe, ck_w1, ck_w2, cv_pe, cv_w1, cv_w2,
                 w_da_out, w_nsa_out, w_o, lam_init):
    B, S, D = h.shape
    f32 = jnp.float32
    proj = h @ w_in
    offsets = np.cumsum(IN_SIZES)[:-1].tolist()
    (da_q, da_k, da_v, nsa_q, c_k, c_v, s_k, s_v, w_k, w_v, nsa_g, merge_g) = jnp.split(proj, offsets, axis=-1)
    slopes = _alibi_slopes()
    slopes_a = slopes[0::2]
    slopes_b = slopes[1::2]

    qa = da_q.reshape(B, S, DA_HEADS, 2, DA_DIM).transpose(0, 2, 3, 1, 4).astype(f32) * (DA_DIM ** -0.5)
    ka = da_k.reshape(B, S, DA_HEADS, 2, DA_DIM).transpose(0, 2, 3, 1, 4).astype(f32)
    va = da_v.reshape(B, S, DA_HEADS, DA_VDIM).transpose(0, 2, 1, 3).astype(f32)
    lam = (jnp.exp(jnp.sum(lq1.astype(f32) * lk1.astype(f32)))
           - jnp.exp(jnp.sum(lq2.astype(f32) * lk2.astype(f32))) + lam_init)
    oa = diff_attention(qa, ka, va, lam, slopes_a)
    oa = _rms(oa, DA_SUBLN_EPS) * subln_w.astype(f32) * (1.0 - lam_init)
    oa = oa.transpose(0, 2, 1, 3).reshape(B, S, DA_HEADS * DA_VDIM).astype(h.dtype)

    def kv_groups(t):
        return t.reshape(B, S, NSA_GROUPS, NSA_DIM).transpose(0, 2, 1, 3).astype(f32)
    qb = nsa_q.reshape(B, S, NSA_GROUPS, NSA_HPG, NSA_DIM).transpose(0, 2, 3, 1, 4).astype(f32) * (NSA_DIM ** -0.5)
    kc = compress_blocks(kv_groups(c_k), ck_pe, ck_w1, ck_w2)
    vc = compress_blocks(kv_groups(c_v), cv_pe, cv_w1, cv_w2)
    gates = jax.nn.sigmoid(nsa_g.astype(f32)).reshape(B, S, NSA_GROUPS, NSA_HPG, 3).transpose(0, 2, 3, 1, 4)
    ob = nsa_attention(qb, kc, vc, kv_groups(s_k), kv_groups(s_v), kv_groups(w_k), kv_groups(w_v),
                       gates, slopes_b.reshape(NSA_GROUPS, NSA_HPG)).astype(h.dtype)

    ya = oa @ w_da_out
    yb = ob @ w_nsa_out
    g = jax.nn.sigmoid(merge_g.astype(f32)).reshape(B, S, N_BRANCH, D)
    merged = (g[:, :, 0] * ya + g[:, :, 1] * yb).astype(h.dtype)
    return merged @ w_o


def _routed_experts(xf, top_e, top_w, w_gate, w_up, w_down):
    N, D = xf.shape
    K = top_e.shape[1]
    E = w_gate.shape[0]
    flat_e = top_e.reshape(-1)
    order = jnp.argsort(flat_e)
    sorted_e = flat_e[order]
    counts = jnp.bincount(flat_e, length=E)
    padded = ((counts + MOE_BLOCK - 1) // MOE_BLOCK) * MOE_BLOCK
    start = jnp.cumsum(counts) - counts
    pend = jnp.cumsum(padded)
    pstart = pend - padded
    dest = pstart[sorted_e] + jnp.arange(N * K) - start[sorted_e]
    n_rows = ((N * K + E * (MOE_BLOCK - 1) + MOE_BLOCK - 1) // MOE_BLOCK) * MOE_BLOCK
    n_blk = n_rows // MOE_BLOCK
    row_tok = jnp.full((n_rows,), N, jnp.int32).at[dest].set((order // K).astype(jnp.int32))
    row_w = jnp.zeros((n_rows,), jnp.float32).at[dest].set(top_w.reshape(-1)[order])
    blk_e = jnp.clip(jnp.searchsorted(pend, jnp.arange(n_blk) * MOE_BLOCK, side='right'), 0, E - 1)
    x_pad = jnp.concatenate([xf, jnp.zeros((1, D), xf.dtype)], axis=0)

    def one_group(args):
        tok, e = args
        xb = x_pad[tok]
        return _swiglu(xb, w_gate[e], w_up[e], w_down[e])

    y = lax.map(one_group, (row_tok.reshape(n_blk, MOE_BLOCK), blk_e))
    y = y.reshape(n_rows, D).astype(jnp.float32) * row_w[:, None]
    return jax.ops.segment_sum(y, row_tok, num_segments=N + 1)[:N]


def moe_ffn(h, router_w, router_b, w_gate, w_up, w_down, sw_gate, sw_up, sw_down):
    B, S, D = h.shape
    N = B * S
    xf = h.reshape(N, D)
    scores = jax.nn.sigmoid((xf @ router_w).astype(jnp.float32))
    biased = scores + router_b.astype(jnp.float32)
    grp = biased.reshape(N, N_EXPERT_GROUPS, N_EXPERTS // N_EXPERT_GROUPS)
    grp_score = jnp.sum(lax.top_k(grp, 2)[0], axis=-1)
    _, top_g = lax.top_k(grp_score, TOPK_EXPERT_GROUPS)
    g_mask = jnp.sum(jax.nn.one_hot(top_g, N_EXPERT_GROUPS, dtype=jnp.float32), axis=1) > 0
    e_mask = jnp.repeat(g_mask, N_EXPERTS // N_EXPERT_GROUPS, axis=1)
    _, top_e = lax.top_k(jnp.where(e_mask, biased, NEG_INF), MOE_TOPK)
    top_w = jnp.take_along_axis(scores, top_e, axis=1)
    top_w = top_w / jnp.sum(top_w, axis=-1, keepdims=True) * ROUTED_SCALE
    routed = _routed_experts(xf, top_e, top_w, w_gate, w_up, w_down)
    shared = _swiglu(xf, sw_gate, sw_up, sw_down).astype(jnp.float32)
    return (routed + shared).astype(h.dtype).reshape(B, S, D)


def setup_inputs(seed: int = 0) -> dict:
    key = jax.random.key(seed)
    ks = jax.random.split(key, 30)
    f32 = jnp.float32

    def nrm(k, shape, scale):
        return jax.random.normal(k, shape, f32) * scale

    D = D_MODEL
    L = DEPTH
    return {
        'x': nrm(ks[0], (BATCH, SEQ, D), 1.0),
        'c': nrm(ks[1], (BATCH, D), 1.0),
        'ada_w': nrm(ks[2], (L, D, 6 * D), 0.5 * D ** -0.5),
        'ada_b': nrm(ks[3], (L, 6 * D), 0.02),
        'norm1_w': 1.0 + nrm(ks[4], (L, D), 0.02),
        'w_in': nrm(ks[5], (L, D, IN_COLS), D ** -0.5),
        'da_lq1': nrm(ks[6], (L, DA_DIM), 0.1),
        'da_lk1': nrm(ks[7], (L, DA_DIM), 0.1),
        'da_lq2': nrm(ks[8], (L, DA_DIM), 0.1),
        'da_lk2': nrm(ks[9], (L, DA_DIM), 0.1),
        'da_subln_w': 1.0 + nrm(ks[10], (L, DA_VDIM), 0.02),
        'cmp_k_pe': nrm(ks[11], (L, CMP_BLOCK, NSA_DIM), 0.02),
        'cmp_k_w1': nrm(ks[12], (L, CMP_BLOCK * NSA_DIM, CMP_HIDDEN), (CMP_BLOCK * NSA_DIM) ** -0.5),
        'cmp_k_w2': nrm(ks[13], (L, CMP_HIDDEN, NSA_DIM), CMP_HIDDEN ** -0.5),
        'cmp_v_pe': nrm(ks[14], (L, CMP_BLOCK, NSA_DIM), 0.02),
        'cmp_v_w1': nrm(ks[15], (L, CMP_BLOCK * NSA_DIM, CMP_HIDDEN), (CMP_BLOCK * NSA_DIM) ** -0.5),
        'cmp_v_w2': nrm(ks[16], (L, CMP_HIDDEN, NSA_DIM), CMP_HIDDEN ** -0.5),
        'w_da_out': nrm(ks[17], (L, DA_HEADS * DA_VDIM, D), (DA_HEADS * DA_VDIM) ** -0.5),
        'w_nsa_out': nrm(ks[18], (L, NSA_HEADS * NSA_DIM, D), (NSA_HEADS * NSA_DIM) ** -0.5),
        'w_o': nrm(ks[19], (L, D, D), D ** -0.5),
        'norm2_w': 1.0 + nrm(ks[20], (L, D), 0.02),
        'router_w': nrm(ks[21], (L, D, N_EXPERTS), D ** -0.5),
        'router_b': nrm(ks[22], (L, N_EXPERTS), 0.01),
        'exp_w_gate': nrm(ks[23], (L, N_EXPERTS, D, EXPERT_HIDDEN), D ** -0.5),
        'exp_w_up': nrm(ks[24], (L, N_EXPERTS, D, EXPERT_HIDDEN), D ** -0.5),
        'exp_w_down': nrm(ks[25], (L, N_EXPERTS, EXPERT_HIDDEN, D), EXPERT_HIDDEN ** -0.5),
        'sh_w_gate': nrm(ks[26], (L, D, SHARED_HIDDEN), D ** -0.5),
        'sh_w_up': nrm(ks[27], (L, D, SHARED_HIDDEN), D ** -0.5),
        'sh_w_down': nrm(ks[28], (L, SHARED_HIDDEN, D), SHARED_HIDDEN ** -0.5),
        'final_norm_w': 1.0 + nrm(ks[29], (D,), 0.02),
    }


def reference(x, c, ada_w, ada_b, norm1_w, w_in, da_lq1, da_lk1, da_lq2, da_lk2, da_subln_w,
              cmp_k_pe, cmp_k_w1, cmp_k_w2, cmp_v_pe, cmp_v_w1, cmp_v_w2, w_da_out, w_nsa_out, w_o,
              norm2_w, router_w, router_b, exp_w_gate, exp_w_up, exp_w_down,
              sh_w_gate, sh_w_up, sh_w_down, final_norm_w):
    f32 = jnp.float32
    c_act = jax.nn.silu(c.astype(f32))
    for l in range(DEPTH):
        mod = c_act @ ada_w[l].astype(f32) + ada_b[l].astype(f32)
        sh1, sc1, g1, sh2, sc2, g2 = jnp.split(mod, 6, axis=-1)
        h = _modulate(x, norm1_w[l], sh1, sc1)
        mix = hybrid_mixer(h, w_in[l], da_lq1[l], da_lk1[l], da_lq2[l], da_lk2[l], da_subln_w[l],
                           cmp_k_pe[l], cmp_k_w1[l], cmp_k_w2[l], cmp_v_pe[l], cmp_v_w1[l], cmp_v_w2[l],
                           w_da_out[l], w_nsa_out[l], w_o[l], _lambda_init(l))
        x = (x + g1[:, None, :] * mix).astype(x.dtype)
        h = _modulate(x, norm2_w[l], sh2, sc2)
        ffn = moe_ffn(h, router_w[l], router_b[l], exp_w_gate[l], exp_w_up[l], exp_w_down[l],
                      sh_w_gate[l], sh_w_up[l], sh_w_down[l])
        x = (x + g2[:, None, :] * ffn).astype(x.dtype)
    return (_rms(x, NORM_EPS) * final_norm_w.astype(f32)).astype(x.dtype)
```

```python
import functools
import math

import numpy as np
import jax
import jax.numpy as jnp
from jax import lax
from jax.experimental import pallas as pl
from jax.experimental.pallas import tpu as pltpu

F32 = jnp.float32
BF16 = jnp.bfloat16
I32 = jnp.int32
HIGHEST = lax.Precision.HIGHEST

NORM_EPS = 1e-6
NEG = -1e30

DA_HEADS = 8
DA_DIM = 64
DA_VDIM = 128
DA_SUBLN_EPS = 1e-5

NSA_HEADS = 8
NSA_GROUPS = 2
NSA_HPG = 4
NSA_DIM = 64
CMP_BLOCK = 32
CMP_STRIDE = 16
SEL_BLOCK = 64
SEL_TOPK = 16
WINDOW = 512
FORCE_BONUS = 1e4

N_EXPERTS = 64
MOE_TOPK = 8
N_EXPERT_GROUPS = 8
TOPK_EXPERT_GROUPS = 4
ROUTED_SCALE = 2.5

VMEM_LIMIT_V7X = 56 * 1024 * 1024

COL_DAQ, COL_DAK, COL_DAV = 0, 1024, 2048
COL_MERGE = 3072
COL_NSAQ = 5120
COL_KV6 = 5632
COL_NSAG = 6400
SLAB_COLS = 6528


def _cparams(sem, vmem=VMEM_LIMIT_V7X):
    return pltpu.CompilerParams(dimension_semantics=sem, vmem_limit_bytes=vmem)


def _ada_kernel(c_ref, w_ref, b_ref, o_ref):
    c = c_ref[...]
    ca = c * jax.nn.sigmoid(c)
    o_ref[...] = jnp.dot(ca, w_ref[...], preferred_element_type=F32, precision=HIGHEST) + b_ref[...]


def _ada(c, w, b):
    B, D = c.shape
    n_out = w.shape[1]
    rows = 8
    cp = jnp.zeros((rows, D), F32).at[:B].set(c)
    tn = 1024
    out = pl.pallas_call(
        _ada_kernel,
        grid=(n_out // tn,),
        in_specs=[pl.BlockSpec((rows, D), lambda j: (0, 0)),
                  pl.BlockSpec((D, tn), lambda j: (0, j)),
                  pl.BlockSpec((1, tn), lambda j: (0, j))],
        out_specs=pl.BlockSpec((rows, tn), lambda j: (0, j)),
        out_shape=jax.ShapeDtypeStruct((rows, n_out), F32),
        compiler_params=_cparams(("arbitrary",)),
        name="ada",
    )(cp, w, b.reshape(1, n_out))
    return out[:B]


def _modulate(x, nw, sh, sc):
    r = lax.rsqrt(jnp.mean(x * x, axis=-1, keepdims=True) + NORM_EPS)
    return (x * r) * nw * (1.0 + sc) + sh


def _inproj_kernel(x_ref, nw_ref, sh_ref, sc_ref, w_ref, o_ref, h_ref):
    @pl.when(pl.program_id(1) == 0)
    def _():
        h_ref[...] = _modulate(x_ref[...], nw_ref[...], sh_ref[...], sc_ref[...]).astype(BF16)

    o_ref[...] = jnp.dot(h_ref[...], w_ref[...], preferred_element_type=F32).astype(BF16)


def _inproj(x2, nw, sh, sc, w_slab, S):
    N, D = x2.shape
    tm = 512
    tn = SLAB_COLS // 3
    per_b = S // tm
    return pl.pallas_call(
        _inproj_kernel,
        grid=(N // tm, SLAB_COLS // tn),
        in_specs=[pl.BlockSpec((tm, D), lambda i, j: (i, 0)),
                  pl.BlockSpec((1, D), lambda i, j: (0, 0)),
                  pl.BlockSpec((None, 1, D), lambda i, j: (i // per_b, 0, 0)),
                  pl.BlockSpec((None, 1, D), lambda i, j: (i // per_b, 0, 0)),
                  pl.BlockSpec((D, tn), lambda i, j: (0, j))],
        out_specs=pl.BlockSpec((tm, tn), lambda i, j: (i, j)),
        out_shape=jax.ShapeDtypeStruct((N, SLAB_COLS), BF16),
        scratch_shapes=[pltpu.VMEM((tm, D), BF16)],
        compiler_params=_cparams(("parallel", "arbitrary")),
        name="inproj",
    )(x2, nw.reshape(1, D), sh.reshape(-1, 1, D), sc.reshape(-1, 1, D), w_slab)


def _cmp_kernel(r_ref, pe_ref, w1a_ref, w1b_ref, w2_ref, o_ref):
    r = r_ref[...]
    n_rows = r.shape[0]
    a = jnp.dot(r, w1a_ref[...], preferred_element_type=F32)
    b = jnp.dot(r, w1b_ref[...], preferred_element_type=F32)
    pe = pe_ref[...]
    half = pe.shape[1] // 2
    bias = (jnp.dot(pe[:, :half], w1a_ref[...].astype(F32), preferred_element_type=F32, precision=HIGHEST)
            + jnp.dot(pe[:, half:], w1b_ref[...].astype(F32), preferred_element_type=F32, precision=HIGHEST))
    hid = a + pltpu.roll(b, n_rows - 1, 0) + bias
    act = 0.5 * hid * (1.0 + jnp.tanh(0.7978845608028654 * (hid + 0.044715 * hid * hid * hid)))
    o_ref[...] = jnp.dot(act, w2_ref[...], preferred_element_type=F32, precision=HIGHEST)


def _compress(rows, pe, w1, w2):
    BG, n_rows, width = rows.shape
    hidden = w1.shape[1]
    w1a = w1[:width].astype(BF16)
    w1b = w1[width:].astype(BF16)
    return pl.pallas_call(
        _cmp_kernel,
        grid=(BG,),
        in_specs=[pl.BlockSpec((None, n_rows, width), lambda i: (i, 0, 0)),
                  pl.BlockSpec((1, 2 * width), lambda i: (0, 0)),
                  pl.BlockSpec((width, hidden), lambda i: (0, 0)),
                  pl.BlockSpec((width, hidden), lambda i: (0, 0)),
                  pl.BlockSpec((hidden, NSA_DIM), lambda i: (0, 0))],
        out_specs=pl.BlockSpec((None, n_rows, NSA_DIM), lambda i: (i, 0, 0)),
        out_shape=jax.ShapeDtypeStruct((BG, n_rows, NSA_DIM), F32),
        compiler_params=_cparams(("parallel",)),
        name="cmp",
    )(rows, pe.reshape(1, -1).astype(F32), w1a, w1b, w2.astype(F32))


def _da_kernel(slopes_ref, qT_ref, k_ref, vT_ref, lq1_ref, lk1_ref, lq2_ref, lk2_ref, subw_ref,
               oT_ref, acc_ref, *, tq, tk, lam_init):
    h = pl.program_id(1)
    qi = pl.program_id(2)
    slope = slopes_ref[h]
    qT = qT_ref[...]
    row = lax.broadcasted_iota(I32, qT.shape, 0)
    zero = jnp.zeros_like(qT)
    qTs = (jnp.where(row < DA_DIM, qT, zero), jnp.where(row >= DA_DIM, qT, zero))
    key_off = lax.broadcasted_iota(I32, (tk, tq), 0)
    t_loc = lax.broadcasted_iota(I32, (tk, tq), 1)
    base = slope * key_off.astype(F32)
    acc_ref[...] = jnp.zeros_like(acc_ref)

    def step(kt, carry, masked):
        k0 = kt * tk
        k_t = k_ref[pl.ds(pl.multiple_of(k0, tk), tk), :]
        vT_t = vT_ref[kt]
        off = slope * k0.astype(F32)
        out = []
        for c in range(2):
            m_old, l_old = carry[2 * c], carry[2 * c + 1]
            s = jnp.dot(k_t, qTs[c], preferred_element_type=F32) + base
            if masked:
                s = jnp.where(key_off <= t_loc, s, NEG)
            m_new = jnp.maximum(m_old, jnp.max(s, axis=0, keepdims=True) + off)
            p = jnp.exp(s - (m_new - off))
            alpha = jnp.exp(m_old - m_new)
            l_new = alpha * l_old + jnp.sum(p, axis=0, keepdims=True)
            acc_ref[c] = alpha * acc_ref[c] + jnp.dot(vT_t, p.astype(BF16), preferred_element_type=F32)
            out += [m_new, l_new]
        return tuple(out)

    neg = jnp.full((1, tq), NEG, F32)
    zer = jnp.zeros((1, tq), F32)
    carry = lax.fori_loop(0, qi, lambda kt, cr: step(kt, cr, False), (neg, zer, neg, zer))
    _, l1, _, l2 = step(qi, carry, True)

    lam = (jnp.exp(jnp.sum(lq1_ref[...] * lk1_ref[...], axis=-1, keepdims=True))
           - jnp.exp(jnp.sum(lq2_ref[...] * lk2_ref[...], axis=-1, keepdims=True)) + lam_init)
    o = acc_ref[0] * (1.0 / l1) - lam * (acc_ref[1] * (1.0 / l2))
    r = lax.rsqrt(jnp.mean(o * o, axis=0, keepdims=True) + DA_SUBLN_EPS)
    oT_ref[...] = (o * r * subw_ref[...] * (1.0 - lam_init)).astype(BF16)


def _diff_attention(slopes, qT, slab3, vT, lq1, lk1, lq2, lk2, subw, lam_init, tile):
    B, _, S = qT.shape
    nk = S // tile
    kcol = COL_DAK // 128
    vec = lambda a: a.reshape(1, DA_DIM).astype(F32)
    grid_spec = pltpu.PrefetchScalarGridSpec(
        num_scalar_prefetch=1,
        grid=(B, DA_HEADS, S // tile),
        in_specs=[pl.BlockSpec((None, 128, tile), lambda b, h, i, s: (b, h, i)),
                  pl.BlockSpec((None, S, 128), lambda b, h, i, s: (b, 0, kcol + h)),
                  pl.BlockSpec((None, None, nk, 128, tile), lambda b, h, i, s: (b, h, 0, 0, 0)),
                  pl.BlockSpec((1, DA_DIM), lambda b, h, i, s: (0, 0)),
                  pl.BlockSpec((1, DA_DIM), lambda b, h, i, s: (0, 0)),
                  pl.BlockSpec((1, DA_DIM), lambda b, h, i, s: (0, 0)),
                  pl.BlockSpec((1, DA_DIM), lambda b, h, i, s: (0, 0)),
                  pl.BlockSpec((DA_VDIM, 1), lambda b, h, i, s: (0, 0))],
        out_specs=pl.BlockSpec((None, 128, tile), lambda b, h, i, s: (b, h, i)),
        scratch_shapes=[pltpu.VMEM((2, DA_VDIM, tile), F32)],
    )
    return pl.pallas_call(
        functools.partial(_da_kernel, tq=tile, tk=tile, lam_init=lam_init),
        grid_spec=grid_spec,
        out_shape=jax.ShapeDtypeStruct((B, DA_HEADS * DA_VDIM, S), BF16),
        compiler_params=_cparams(("parallel", "parallel", "arbitrary")),
        name="diffattn",
    )(slopes, qT, slab3, vT, vec(lq1), vec(lk1), vec(lq2), vec(lk2), subw.reshape(DA_VDIM, 1).astype(F32))


def _nsa_kernel(slopes_ref, qT_ref, kc_ref, vcT_ref, ks_ref, vsT_ref, kw_ref, vwT_ref, gT_ref, mT_ref,
                oT_ref, sel_ref, acc_ref, res_ref, *, t, n_top):
    g = pl.program_id(1)
    qi = pl.program_id(2)
    q0 = qi * t
    n_cp = kc_ref.shape[0]
    n_sel = mT_ref.shape[0]
    slopes = [slopes_ref[g * NSA_HPG + hh] for hh in range(NSA_HPG)]
    qTs = [qT_ref[hh] for hh in range(NSA_HPG)]
    gates = jax.nn.sigmoid(gT_ref[...].astype(F32))

    c_idx = lax.broadcasted_iota(I32, (n_cp, t), 0)
    t_c = q0 + lax.broadcasted_iota(I32, (n_cp, t), 1)
    d_c = t_c - (c_idx * CMP_STRIDE + (CMP_BLOCK - 1))
    ok_c = d_c >= 0
    d_cf = d_c.astype(F32)
    kc = kc_ref[...]
    vcT = vcT_ref[...]
    p_sum = jnp.zeros((n_cp, t), F32)
    for hh in range(NSA_HPG):
        s = jnp.dot(kc, qTs[hh], preferred_element_type=F32) - slopes[hh] * d_cf
        s = jnp.where(ok_c, s, NEG)
        e = jnp.exp(s - jnp.max(s, axis=0, keepdims=True))
        inv = 1.0 / jnp.sum(e, axis=0, keepdims=True)
        p = jnp.where(ok_c, e * inv, 0.0)
        p_sum = p_sum + p
        o_c = jnp.dot(vcT, p.astype(BF16), preferred_element_type=F32)
        res_ref[hh] = gates[3 * hh:3 * hh + 1, :] * o_c

    p_hi = p_sum.astype(BF16)
    p_lo = (p_sum - p_hi.astype(F32)).astype(BF16)
    mT = mT_ref[...]
    imp = jnp.dot(mT, p_hi, preferred_element_type=F32) + jnp.dot(mT, p_lo, preferred_element_type=F32)
    blk = lax.broadcasted_iota(I32, (n_sel, t), 0)
    t_s = q0 + lax.broadcasted_iota(I32, (n_sel, t), 1)
    cur = jnp.right_shift(t_s, int(math.log2(SEL_BLOCK)))
    forced = (blk == 0) | (blk == cur) | (blk == cur - 1)
    score = jnp.where(blk * SEL_BLOCK <= t_s, imp + jnp.where(forced, FORCE_BONUS, 0.0), NEG)
    blk_f = blk.astype(F32)
    sel = jnp.zeros((n_sel, t), F32)
    for _ in range(n_top):
        _, idx = _first_max(score, blk_f)
        hit = blk_f == idx
        sel = jnp.where(hit, 1.0, sel)
        score = jnp.where(hit, -jnp.inf, score)
    sel_ref[...] = sel

    key_off = lax.broadcasted_iota(I32, (t, t), 0)
    t_loc = lax.broadcasted_iota(I32, (t, t), 1)
    key_f = key_off.astype(F32)
    rel = t_loc - key_off
    neg = jnp.full((1, t), NEG, F32)
    zer = jnp.zeros((1, t), F32)
    init = (neg, zer) * NSA_HPG

    def attend(kt, carry, k_ref, vT_ref, mask):
        k0 = kt * t
        k_t = k_ref[pl.ds(pl.multiple_of(k0, t), t), :]
        vT_t = vT_ref[kt]
        k0f = k0.astype(F32)
        out = []
        for hh in range(NSA_HPG):
            m_old, l_old = carry[2 * hh], carry[2 * hh + 1]
            off = slopes[hh] * k0f
            s = jnp.dot(k_t, qTs[hh], preferred_element_type=F32) + slopes[hh] * key_f
            s = jnp.where(mask, s, NEG)
            m_new = jnp.maximum(m_old, jnp.max(s, axis=0, keepdims=True) + off)
            p = jnp.exp(s - (m_new - off))
            alpha = jnp.exp(m_old - m_new)
            l_new = alpha * l_old + jnp.sum(p, axis=0, keepdims=True)
            acc_ref[hh] = alpha * acc_ref[hh] + jnp.dot(vT_t, p.astype(BF16), preferred_element_type=F32)
            out += [m_new, l_new]
        return tuple(out)

    def sel_mask(kt):
        r0 = sel_ref[pl.ds(2 * kt, 1), :]
        r1 = sel_ref[pl.ds(2 * kt + 1, 1), :]
        return jnp.where(key_off < SEL_BLOCK, r0, r1) > 0.5

    acc_ref[...] = jnp.zeros_like(acc_ref)
    carry = lax.fori_loop(0, qi, lambda kt, cr: attend(kt, cr, ks_ref, vsT_ref, sel_mask(kt)), init)
    carry = attend(qi, carry, ks_ref, vsT_ref, sel_mask(qi) & (rel >= 0))
    for hh in range(NSA_HPG):
        res_ref[hh] = res_ref[hh] + gates[3 * hh + 1:3 * hh + 2, :] * (acc_ref[hh] * (1.0 / carry[2 * hh + 1]))

    acc_ref[...] = jnp.zeros_like(acc_ref)
    n_back = WINDOW // t

    def win_step(kt, cr):
        d = rel + (q0 - kt * t)
        return attend(kt, cr, kw_ref, vwT_ref, (d >= 0) & (d < WINDOW))

    carry = lax.fori_loop(jnp.maximum(qi - n_back, 0), qi + 1, win_step, init)
    for hh in range(NSA_HPG):
        o = res_ref[hh] + gates[3 * hh + 2:3 * hh + 3, :] * (acc_ref[hh] * (1.0 / carry[2 * hh + 1]))
        oT_ref[hh] = o.astype(BF16)


def _nsa_attention(slopes, qT, kc, vcT, ks, vsT, kw, vwT, gT, mT, tile):
    B, G, _, _, S = qT.shape
    n_cp = kc.shape[2]
    n_sel = S // SEL_BLOCK
    nk = S // tile
    n_top = min(SEL_TOPK, n_sel)
    grid_spec = pltpu.PrefetchScalarGridSpec(
        num_scalar_prefetch=1,
        grid=(B, G, S // tile),
        in_specs=[pl.BlockSpec((None, None, NSA_HPG, NSA_DIM, tile), lambda b, g, i, s: (b, g, 0, 0, i)),
                  pl.BlockSpec((None, None, n_cp, NSA_DIM), lambda b, g, i, s: (b, g, 0, 0)),
                  pl.BlockSpec((None, None, NSA_DIM, n_cp), lambda b, g, i, s: (b, g, 0, 0)),
                  pl.BlockSpec((None, None, S, NSA_DIM), lambda b, g, i, s: (b, g, 0, 0)),
                  pl.BlockSpec((None, None, nk, NSA_DIM, tile), lambda b, g, i, s: (b, g, 0, 0, 0)),
                  pl.BlockSpec((None, None, S, NSA_DIM), lambda b, g, i, s: (b, g, 0, 0)),
                  pl.BlockSpec((None, None, nk, NSA_DIM, tile), lambda b, g, i, s: (b, g, 0, 0, 0)),
                  pl.BlockSpec((None, None, 3 * NSA_HPG, tile), lambda b, g, i, s: (b, g, 0, i)),
                  pl.BlockSpec((n_sel, n_cp), lambda b, g, i, s: (0, 0))],
        out_specs=pl.BlockSpec((None, None, NSA_HPG, NSA_DIM, tile), lambda b, g, i, s: (b, g, 0, 0, i)),
        scratch_shapes=[pltpu.VMEM((n_sel, tile), F32),
                        pltpu.VMEM((NSA_HPG, NSA_DIM, tile), F32),
                        pltpu.VMEM((NSA_HPG, NSA_DIM, tile), F32)],
    )
    return pl.pallas_call(
        functools.partial(_nsa_kernel, t=tile, n_top=n_top),
        grid_spec=grid_spec,
        out_shape=jax.ShapeDtypeStruct((B, G, NSA_HPG, NSA_DIM, S), BF16),
        compiler_params=_cparams(("parallel", "parallel", "arbitrary")),
        name="nsa",
    )(slopes, qT, kc, vcT, ks, vsT, kw, vwT, gT, mT)


def _merge_kernel(oa_ref, ob_ref, ga_ref, gb_ref, x_ref, g1_ref, nw_ref, sh_ref, sc_ref,
                  wa_ref, wb_ref, wo_ref, wr_ref, x1_ref, h2_ref, lg_ref):
    ya = jnp.dot(oa_ref[...], wa_ref[...], preferred_element_type=F32)
    yb = jnp.dot(ob_ref[...], wb_ref[...], preferred_element_type=F32)
    merged = (jax.nn.sigmoid(ga_ref[...].astype(F32)) * ya
              + jax.nn.sigmoid(gb_ref[...].astype(F32)) * yb).astype(BF16)
    mix = jnp.dot(merged, wo_ref[...], preferred_element_type=F32)
    x1 = x_ref[...] + g1_ref[...] * mix
    x1_ref[...] = x1
    h2 = _modulate(x1, nw_ref[...], sh_ref[...], sc_ref[...])
    h2_ref[...] = h2
    lg_ref[...] = jnp.dot(h2, wr_ref[...], preferred_element_type=F32, precision=HIGHEST)


def _merge(oa, ob, slab, x2, g1, nw, sh, sc, wa, wb, wo, wr, S):
    N, D = x2.shape
    tm = 256
    per_b = S // tm
    mcol = COL_MERGE // D
    row = lambda i: (i, 0)
    full = lambda i: (0, 0)
    perb = lambda i: (i // per_b, 0, 0)
    n_lg = wr.shape[1]
    return pl.pallas_call(
        _merge_kernel,
        grid=(N // tm,),
        in_specs=[pl.BlockSpec((tm, oa.shape[1]), row),
                  pl.BlockSpec((tm, ob.shape[1]), row),
                  pl.BlockSpec((tm, D), lambda i: (i, mcol)),
                  pl.BlockSpec((tm, D), lambda i: (i, mcol + 1)),
                  pl.BlockSpec((tm, D), row),
                  pl.BlockSpec((None, 1, D), perb),
                  pl.BlockSpec((1, D), full),
                  pl.BlockSpec((None, 1, D), perb),
                  pl.BlockSpec((None, 1, D), perb),
                  pl.BlockSpec(wa.shape, full),
                  pl.BlockSpec(wb.shape, full),
                  pl.BlockSpec(wo.shape, full),
                  pl.BlockSpec(wr.shape, full)],
        out_specs=[pl.BlockSpec((tm, D), row), pl.BlockSpec((tm, D), row), pl.BlockSpec((tm, n_lg), row)],
        out_shape=[jax.ShapeDtypeStruct((N, D), F32), jax.ShapeDtypeStruct((N, D), F32),
                   jax.ShapeDtypeStruct((N, n_lg), F32)],
        compiler_params=_cparams(("parallel",)),
        name="merge",
    )(oa, ob, slab, slab, x2, g1.reshape(-1, 1, D), nw.reshape(1, D), sh.reshape(-1, 1, D),
      sc.reshape(-1, 1, D), wa, wb, wo, wr)


def _first_max(v, idx):
    mx = jnp.max(v, axis=0, keepdims=True)
    first = jnp.min(jnp.where(v == mx, idx, float(v.shape[0])), axis=0, keepdims=True)
    return mx, first


def _route_kernel(lg_ref, b_ref, e_ref, w_ref):
    lg = lg_ref[...]
    tn = lg.shape[1]
    per_g = N_EXPERTS // N_EXPERT_GROUPS
    scores = jax.nn.sigmoid(lg)
    biased = scores + b_ref[...]
    gi = lax.broadcasted_iota(I32, (N_EXPERT_GROUPS, tn), 0).astype(F32)
    gscore = jnp.zeros((N_EXPERT_GROUPS, tn), F32)
    for g in range(N_EXPERT_GROUPS):
        grp = biased[g * per_g:(g + 1) * per_g, :]
        m1, f1 = _first_max(grp, gi)
        m2 = jnp.max(jnp.where(gi == f1, -jnp.inf, grp), axis=0, keepdims=True)
        gscore = jnp.where(gi == g, m1 + m2, gscore)
    gsel = jnp.zeros((N_EXPERT_GROUPS, tn), F32)
    for _ in range(TOPK_EXPERT_GROUPS):
        _, f = _first_max(gscore, gi)
        hit = gi == f
        gsel = jnp.where(hit, 1.0, gsel)
        gscore = jnp.where(hit, -jnp.inf, gscore)
    ei = lax.broadcasted_iota(I32, (N_EXPERTS, tn), 0).astype(F32)
    emask = jnp.zeros((N_EXPERTS, tn), F32)
    for g in range(N_EXPERT_GROUPS):
        in_g = (ei >= g * per_g) & (ei < (g + 1) * per_g)
        emask = jnp.where(in_g, gsel[g:g + 1, :], emask)
    masked = jnp.where(emask > 0.5, biased, NEG)
    e_out = jnp.zeros((MOE_TOPK, tn), F32)
    w_out = jnp.zeros((MOE_TOPK, tn), F32)
    for r in range(MOE_TOPK):
        _, f = _first_max(masked, ei)
        hit = ei == f
        wv = jnp.sum(jnp.where(hit, scores, 0.0), axis=0, keepdims=True)
        e_out = jnp.where(gi == r, f, e_out)
        w_out = jnp.where(gi == r, wv, w_out)
        masked = jnp.where(hit, -jnp.inf, masked)
    e_ref[...] = e_out.astype(I32)
    w_ref[...] = w_out / jnp.sum(w_out, axis=0, keepdims=True) * ROUTED_SCALE


def _route(lgT, rb):
    E, N = lgT.shape
    tn = 512
    return pl.pallas_call(
        _route_kernel,
        grid=(N // tn,),
        in_specs=[pl.BlockSpec((E, tn), lambda i: (0, i)), pl.BlockSpec((E, 1), lambda i: (0, 0))],
        out_specs=[pl.BlockSpec((MOE_TOPK, tn), lambda i: (0, i)), pl.BlockSpec((MOE_TOPK, tn), lambda i: (0, i))],
        out_shape=[jax.ShapeDtypeStruct((MOE_TOPK, N), I32), jax.ShapeDtypeStruct((MOE_TOPK, N), F32)],
        compiler_params=_cparams(("parallel",)),
        name="route",
    )(lgT, rb.reshape(E, 1).astype(F32))


def _slots_kernel(e_ref, tri_ref, lt_ref, dest_ref, cnt_ref, carry_ref, *, rb):
    phase = pl.program_id(0)
    i = pl.program_id(1)
    e = e_ref[...]
    tn = e.shape[1]
    ei = lax.broadcasted_iota(I32, (N_EXPERTS, tn), 0)

    @pl.when((phase == 0) & (i == 0))
    def _():
        carry_ref[...] = jnp.zeros_like(carry_ref)

    @pl.when(phase == 0)
    def _():
        tot = jnp.zeros((N_EXPERTS, 1), F32)
        for k in range(MOE_TOPK):
            oh = jnp.where(ei == e[k:k + 1, :], 1.0, 0.0)
            tot = tot + jnp.sum(oh, axis=1, keepdims=True)
        carry_ref[...] = carry_ref[...] + tot
        dest_ref[...] = jnp.zeros_like(dest_ref)

    @pl.when((phase == 1) & (i == 0))
    def _():
        cnt = carry_ref[...]
        cnt_ref[...] = cnt
        padded = jnp.broadcast_to(jnp.ceil(cnt / rb) * rb, (N_EXPERTS, 128))
        first = jnp.dot(lt_ref[...], padded, preferred_element_type=F32, precision=HIGHEST)
        carry_ref[...] = first[:, :1]

    @pl.when(phase == 1)
    def _():
        carry = carry_ref[...]
        ki = lax.broadcasted_iota(I32, (MOE_TOPK, tn), 0)
        dest = jnp.zeros((MOE_TOPK, tn), F32)
        for k in range(MOE_TOPK):
            hit = ei == e[k:k + 1, :]
            oh = jnp.where(hit, 1.0, 0.0)
            before = jnp.dot(oh.astype(BF16), tri_ref[...], preferred_element_type=F32)
            row = jnp.sum(jnp.where(hit, before + carry, 0.0), axis=0, keepdims=True)
            dest = jnp.where(ki == k, row, dest)
            carry = carry + jnp.sum(oh, axis=1, keepdims=True)
        carry_ref[...] = carry
        dest_ref[...] = dest.astype(I32)


def _slots(top_eT, rb):
    K, N = top_eT.shape
    tn = 512
    idx = np.arange(tn)
    tri = jnp.asarray((idx[:, None] < idx[None, :]).astype(np.float32), BF16)
    e_idx = np.arange(N_EXPERTS)
    lt = jnp.asarray((e_idx[None, :] < e_idx[:, None]).astype(np.float32), F32)
    return pl.pallas_call(
        functools.partial(_slots_kernel, rb=rb),
        grid=(2, N // tn),
        in_specs=[pl.BlockSpec((K, tn), lambda p, i: (0, i)),
                  pl.BlockSpec((tn, tn), lambda p, i: (0, 0)),
                  pl.BlockSpec((N_EXPERTS, N_EXPERTS), lambda p, i: (0, 0))],
        out_specs=[pl.BlockSpec((K, tn), lambda p, i: (0, i * p)),
                   pl.BlockSpec((N_EXPERTS, 1), lambda p, i: (0, 0))],
        out_shape=[jax.ShapeDtypeStruct((K, N), I32), jax.ShapeDtypeStruct((N_EXPERTS, 1), F32)],
        scratch_shapes=[pltpu.VMEM((N_EXPERTS, 1), F32)],
        compiler_params=_cparams(("arbitrary", "arbitrary")),
        name="slots",
    )(top_eT, tri, lt)


def _dispatch_kernel(dest_ref, h_ref, xs_in_ref, xs_ref, sem, *, tt):
    del xs_in_ref
    i = pl.program_id(0)

    def body(n, c):
        src = h_ref.at[pl.ds(i * tt + n, 1), :]
        for k in range(MOE_TOPK):
            d = dest_ref[0, 0, k * tt + n]
            pltpu.make_async_copy(src, xs_ref.at[pl.ds(d, 1), :], sem).start()
        return c

    lax.fori_loop(0, tt, body, 0)
    pltpu.make_async_copy(h_ref.at[pl.ds(0, MOE_TOPK * tt), :], xs_ref.at[pl.ds(0, MOE_TOPK * tt), :], sem).wait()


def _dispatch(dest_tiles, h2, n_rows, tt):
    N, D = h2.shape
    xs0 = jnp.zeros((n_rows, D), h2.dtype)
    return pl.pallas_call(
        functools.partial(_dispatch_kernel, tt=tt),
        grid=(N // tt,),
        in_specs=[pl.BlockSpec((1, 1, MOE_TOPK * tt), lambda i: (i, 0, 0), memory_space=pltpu.SMEM),
                  pl.BlockSpec(memory_space=pl.ANY),
                  pl.BlockSpec(memory_space=pl.ANY)],
        out_specs=pl.BlockSpec(memory_space=pl.ANY),
        out_shape=jax.ShapeDtypeStruct((n_rows, D), h2.dtype),
        scratch_shapes=[pltpu.SemaphoreType.DMA(())],
        input_output_aliases={2: 0},
        compiler_params=_cparams(("arbitrary",)),
        name="dispatch",
    )(dest_tiles, h2, xs0)


def _experts_kernel(be_ref, x_ref, wgu_ref, wd_ref, y_ref):
    del be_ref
    x = x_ref[...].astype(BF16)
    gu = jnp.dot(x, wgu_ref[...], preferred_element_type=F32)
    hdim = gu.shape[1] // 2
    g = gu[:, :hdim]
    act = (g * jax.nn.sigmoid(g) * gu[:, hdim:]).astype(BF16)
    y_ref[...] = jnp.dot(act, wd_ref[...], preferred_element_type=F32)


def _experts(blk_e, xs, wgu, wd, rb):
    n_rows, D = xs.shape
    grid_spec = pltpu.PrefetchScalarGridSpec(
        num_scalar_prefetch=1,
        grid=(n_rows // rb,),
        in_specs=[pl.BlockSpec((rb, D), lambda i, be: (i, 0)),
                  pl.BlockSpec((None,) + wgu.shape[1:], lambda i, be: (be[i], 0, 0)),
                  pl.BlockSpec((None,) + wd.shape[1:], lambda i, be: (be[i], 0, 0))],
        out_specs=pl.BlockSpec((rb, D), lambda i, be: (i, 0)),
    )
    return pl.pallas_call(
        _experts_kernel,
        grid_spec=grid_spec,
        out_shape=jax.ShapeDtypeStruct((n_rows, D), F32),
        compiler_params=_cparams(("arbitrary",)),
        name="experts",
    )(blk_e, xs, wgu, wd)


def _combine_kernel(dest_ref, ys_ref, w_ref, h_ref, x1_ref, g2_ref, fw_ref, sgu_ref, sd_ref, o_ref, buf_ref, sem, *, tt):
    def body(n, c):
        for k in range(MOE_TOPK):
            d = dest_ref[0, 0, k * tt + n]
            pltpu.make_async_copy(ys_ref.at[pl.ds(d, 1), :], buf_ref.at[k, pl.ds(n, 1), :], sem).start()
        return c

    lax.fori_loop(0, tt, body, 0)

    h = h_ref[...].astype(BF16)
    gu = jnp.dot(h, sgu_ref[...], preferred_element_type=F32)
    hdim = gu.shape[1] // 2
    g = gu[:, :hdim]
    ffn = jnp.dot((g * jax.nn.sigmoid(g) * gu[:, hdim:]).astype(BF16), sd_ref[...], preferred_element_type=F32)

    for k in range(MOE_TOPK):
        pltpu.make_async_copy(ys_ref.at[pl.ds(0, tt), :], buf_ref.at[k], sem).wait()
    w = w_ref[...]
    for k in range(MOE_TOPK):
        ffn = ffn + w[:, k:k + 1] * buf_ref[k]
    x2 = x1_ref[...] + g2_ref[...] * ffn
    r = lax.rsqrt(jnp.mean(x2 * x2, axis=-1, keepdims=True) + NORM_EPS)
    o_ref[...] = x2 * r * fw_ref[...]


def _combine(dest_tiles, ys, top_w, h2, x1, g2, fw, sgu, sd, S, tt):
    N, D = h2.shape
    per_b = S // tt
    row = lambda i: (i, 0)
    full = lambda i: (0, 0)
    return pl.pallas_call(
        functools.partial(_combine_kernel, tt=tt),
        grid=(N // tt,),
        in_specs=[pl.BlockSpec((1, 1, MOE_TOPK * tt), lambda i: (i, 0, 0), memory_space=pltpu.SMEM),
                  pl.BlockSpec(memory_space=pl.ANY),
                  pl.BlockSpec((tt, MOE_TOPK), row),
                  pl.BlockSpec((tt, D), row),
                  pl.BlockSpec((tt, D), row),
                  pl.BlockSpec((None, 1, D), lambda i: (i // per_b, 0, 0)),
                  pl.BlockSpec((1, D), full),
                  pl.BlockSpec(sgu.shape, full),
                  pl.BlockSpec(sd.shape, full)],
        out_specs=pl.BlockSpec((tt, D), row),
        out_shape=jax.ShapeDtypeStruct((N, D), F32),
        scratch_shapes=[pltpu.VMEM((MOE_TOPK, tt, D), F32), pltpu.SemaphoreType.DMA(())],
        compiler_params=_cparams(("arbitrary",)),
        name="combine",
    )(dest_tiles, ys, top_w, h2, x1, g2.reshape(-1, 1, D), fw.reshape(1, D), sgu, sd)


def _cmp_to_sel_T(n_cp, n_sel):
    c0 = np.arange(n_cp)[None, :] * CMP_STRIDE
    s0 = np.arange(n_sel)[:, None] * SEL_BLOCK
    ov = np.minimum(c0 + CMP_BLOCK, s0 + SEL_BLOCK) - np.maximum(c0, s0)
    m = np.clip(ov, 0, None).astype(np.float32) / CMP_BLOCK
    m[:, n_cp - 1] = 0.0
    return jnp.asarray(m, BF16)


def _slab_weights(w_in):
    D = w_in.shape[0]
    sizes = [1024, 1024, 1024, 512, 128, 128, 128, 128, 128, 128, 24, 2048]
    offs = np.concatenate([[0], np.cumsum(sizes)])
    part = lambda i: w_in[:, offs[i]:offs[i + 1]]
    pieces = [part(0), part(1), part(2), part(11), part(3)] + [part(i) for i in range(4, 10)] + [part(10)]
    w = jnp.concatenate(pieces, axis=1)
    return jnp.pad(w, ((0, 0), (0, SLAB_COLS - w.shape[1]))).astype(BF16)


def _layer(x, c, ada_w, ada_b, norm1_w, w_in, da_lq1, da_lk1, da_lq2, da_lk2, da_subln_w,
           cmp_k_pe, cmp_k_w1, cmp_k_w2, cmp_v_pe, cmp_v_w1, cmp_v_w2, w_da_out, w_nsa_out, w_o,
           norm2_w, router_w, router_b, exp_w_gate, exp_w_up, exp_w_down,
           sh_w_gate, sh_w_up, sh_w_down, final_norm_w, lam_init):
    B, S, D = x.shape
    N = B * S
    G, HPG = NSA_GROUPS, NSA_HPG
    x2 = x.reshape(N, D)

    mod = _ada(c, ada_w, ada_b)
    sh1, sc1, g1, sh2, sc2, g2 = jnp.split(mod, 6, axis=-1)

    slab = _inproj(x2, norm1_w, sh1, sc1, _slab_weights(w_in), S)
    slab3 = slab.reshape(B, S, SLAB_COLS)

    i_all = np.arange(1, DA_HEADS + NSA_HEADS + 1, dtype=np.float32)
    slopes = (2.0 ** (-8.0 * i_all / (DA_HEADS + NSA_HEADS))).astype(np.float32)
    slopes_a = jnp.asarray(slopes[0::2])
    slopes_b = jnp.asarray(slopes[1::2])

    ta = 256
    qT = (slab3[:, :, COL_DAQ:COL_DAQ + 1024] * jnp.asarray(DA_DIM ** -0.5, BF16)).transpose(0, 2, 1)
    vT = slab3[:, :, COL_DAV:COL_DAV + 1024].reshape(B, S // ta, ta, DA_HEADS, DA_VDIM).transpose(0, 3, 1, 4, 2)
    oaT = _diff_attention(slopes_a, qT, slab3, vT, da_lq1, da_lk1, da_lq2, da_lk2, da_subln_w, lam_init, ta)
    oa = oaT.transpose(0, 2, 1).reshape(N, DA_HEADS * DA_VDIM)

    tb = 128
    n_cp = S // CMP_STRIDE

    def kv_groups(j):
        c0 = COL_KV6 + 128 * j
        return slab3[:, :, c0:c0 + 128].reshape(B, S, G, NSA_DIM).transpose(0, 2, 1, 3)

    def kv_tiles_T(a):
        return a.reshape(B, G, S // tb, tb, NSA_DIM).transpose(0, 1, 2, 4, 3)

    ck_rows = kv_groups(0).reshape(B * G, n_cp, CMP_STRIDE * NSA_DIM)
    cv_rows = kv_groups(1).reshape(B * G, n_cp, CMP_STRIDE * NSA_DIM)
    kc = _compress(ck_rows, cmp_k_pe, cmp_k_w1, cmp_k_w2).reshape(B, G, n_cp, NSA_DIM).astype(BF16)
    vc = _compress(cv_rows, cmp_v_pe, cmp_v_w1, cmp_v_w2).reshape(B, G, n_cp, NSA_DIM).astype(BF16)
    nq = (slab3[:, :, COL_NSAQ:COL_NSAQ + 512] * jnp.asarray(NSA_DIM ** -0.5, BF16))
    nqT = nq.reshape(B, S, G, HPG, NSA_DIM).transpose(0, 2, 3, 4, 1)
    gT = slab3[:, :, COL_NSAG:COL_NSAG + 3 * NSA_HEADS].reshape(B, S, G, 3 * HPG).transpose(0, 2, 3, 1).astype(F32)
    obT = _nsa_attention(slopes_b, nqT, kc, vc.transpose(0, 1, 3, 2), kv_groups(2), kv_tiles_T(kv_groups(3)),
                         kv_groups(4), kv_tiles_T(kv_groups(5)), gT, _cmp_to_sel_T(n_cp, S // SEL_BLOCK), tb)
    ob = obT.transpose(0, 4, 1, 2, 3).reshape(N, NSA_HEADS * NSA_DIM)

    x1, h2, logits = _merge(oa, ob, slab, x2, g1, norm2_w, sh2, sc2, w_da_out.astype(BF16),
                            w_nsa_out.astype(BF16), w_o.astype(BF16), router_w.astype(F32), S)

    rb = 256
    tt = 128
    top_eT, top_wT = _route(logits.T, router_b)
    destT, counts = _slots(top_eT, rb)
    n_rows = ((N * MOE_TOPK + N_EXPERTS * (rb - 1) + rb - 1) // rb) * rb
    padded = (jnp.ceil(counts[:, 0] / rb) * rb).astype(I32)
    pend = jnp.cumsum(padded)
    blk_e = jnp.clip(jnp.searchsorted(pend, jnp.arange(n_rows // rb, dtype=I32) * rb, side='right'),
                     0, N_EXPERTS - 1).astype(I32)
    dest_tiles = destT.reshape(MOE_TOPK, N // tt, tt).transpose(1, 0, 2).reshape(N // tt, 1, MOE_TOPK * tt)
    xs = _dispatch(dest_tiles, h2, n_rows, tt)
    wgu = jnp.concatenate([exp_w_gate, exp_w_up], axis=-1).astype(BF16)
    ys = _experts(blk_e, xs, wgu, exp_w_down.astype(BF16), rb)
    sgu = jnp.concatenate([sh_w_gate, sh_w_up], axis=-1).astype(BF16)
    out = _combine(dest_tiles, ys, top_wT.T, h2, x1, g2, final_norm_w, sgu, sh_w_down.astype(BF16), S, tt)
    return out.reshape(B, S, D)


def kernel(x, c, ada_w, ada_b, norm1_w, w_in, da_lq1, da_lk1, da_lq2, da_lk2, da_subln_w, cmp_k_pe, cmp_k_w1, cmp_k_w2, cmp_v_pe, cmp_v_w1, cmp_v_w2, w_da_out, w_nsa_out, w_o, norm2_w, router_w, router_b, exp_w_gate, exp_w_up, exp_w_down, sh_w_gate, sh_w_up, sh_w_down, final_norm_w):
    depth = ada_w.shape[0]
    assert depth == 1, "one decoder layer"
    lam_init = 0.8 - 0.6 * math.exp(-0.3 * 0)
    return _layer(x, c, ada_w[0], ada_b[0], norm1_w[0], w_in[0], da_lq1[0], da_lk1[0], da_lq2[0], da_lk2[0],
                  da_subln_w[0], cmp_k_pe[0], cmp_k_w1[0], cmp_k_w2[0], cmp_v_pe[0], cmp_v_w1[0], cmp_v_w2[0],
                  w_da_out[0], w_nsa_out[0], w_o[0], norm2_w[0], router_w[0], router_b[0],
                  exp_w_gate[0], exp_w_up[0], exp_w_down[0], sh_w_gate[0], sh_w_up[0], sh_w_down[0],
                  final_norm_w, lam_init)
```

```python
import functools
import math

import numpy as np
import jax
import jax.numpy as jnp
from jax import lax
from jax.experimental import pallas as pl
from jax.experimental.pallas import tpu as pltpu

F32 = jnp.float32
BF16 = jnp.bfloat16
I32 = jnp.int32
HIGHEST = lax.Precision.HIGHEST

NORM_EPS = 1e-6
NEG = -1e30

DA_HEADS = 8
DA_DIM = 64
DA_VDIM = 128
DA_SUBLN_EPS = 1e-5

NSA_HEADS = 8
NSA_GROUPS = 2
NSA_HPG = 4
NSA_DIM = 64
CMP_BLOCK = 32
CMP_STRIDE = 16
SEL_BLOCK = 64
SEL_TOPK = 16
WINDOW = 512
FORCE_BONUS = 1e4

N_EXPERTS = 64
MOE_TOPK = 8
N_EXPERT_GROUPS = 8
TOPK_EXPERT_GROUPS = 4
ROUTED_SCALE = 2.5

VMEM_LIMIT_V7X = 56 * 1024 * 1024

COL_DAQ, COL_DAK, COL_DAV = 0, 1024, 2048
COL_MERGE = 3072
COL_NSAQ = 5120
COL_KV6 = 5632
COL_NSAG = 6400
SLAB_COLS = 6528


def _cparams(sem, vmem=VMEM_LIMIT_V7X):
    return pltpu.CompilerParams(dimension_semantics=sem, vmem_limit_bytes=vmem)


def _ada_kernel(c_ref, w_ref, b_ref, o_ref):
    c = c_ref[...]
    ca = c * jax.nn.sigmoid(c)
    o_ref[...] = jnp.dot(ca, w_ref[...], preferred_element_type=F32, precision=HIGHEST) + b_ref[...]


def _ada(c, w, b):
    B, D = c.shape
    n_out = w.shape[1]
    rows = 8
    cp = jnp.zeros((rows, D), F32).at[:B].set(c)
    tn = 1024
    out = pl.pallas_call(
        _ada_kernel,
        grid=(n_out // tn,),
        in_specs=[pl.BlockSpec((rows, D), lambda j: (0, 0)),
                  pl.BlockSpec((D, tn), lambda j: (0, j)),
                  pl.BlockSpec((1, tn), lambda j: (0, j))],
        out_specs=pl.BlockSpec((rows, tn), lambda j: (0, j)),
        out_shape=jax.ShapeDtypeStruct((rows, n_out), F32),
        compiler_params=_cparams(("arbitrary",)),
        name="ada",
    )(cp, w, b.reshape(1, n_out))
    return out[:B]


def _modulate(x, nw, sh, sc):
    r = lax.rsqrt(jnp.mean(x * x, axis=-1, keepdims=True) + NORM_EPS)
    return (x * r) * nw * (1.0 + sc) + sh


def _inproj_kernel(x_ref, nw_ref, sh_ref, sc_ref, w_ref, o_ref, h_ref):
    @pl.when(pl.program_id(1) == 0)
    def _():
        h_ref[...] = _modulate(x_ref[...], nw_ref[...], sh_ref[...], sc_ref[...]).astype(BF16)

    o_ref[...] = jnp.dot(h_ref[...], w_ref[...], preferred_element_type=F32).astype(BF16)


def _inproj(x2, nw, sh, sc, w_slab, S):
    N, D = x2.shape
    tm = 512
    tn = SLAB_COLS // 3
    per_b = S // tm
    return pl.pallas_call(
        _inproj_kernel,
        grid=(N // tm, SLAB_COLS // tn),
        in_specs=[pl.BlockSpec((tm, D), lambda i, j: (i, 0)),
                  pl.BlockSpec((1, D), lambda i, j: (0, 0)),
                  pl.BlockSpec((None, 1, D), lambda i, j: (i // per_b, 0, 0)),
                  pl.BlockSpec((None, 1, D), lambda i, j: (i // per_b, 0, 0)),
                  pl.BlockSpec((D, tn), lambda i, j: (0, j))],
        out_specs=pl.BlockSpec((tm, tn), lambda i, j: (i, j)),
        out_shape=jax.ShapeDtypeStruct((N, SLAB_COLS), BF16),
        scratch_shapes=[pltpu.VMEM((tm, D), BF16)],
        compiler_params=_cparams(("parallel", "arbitrary")),
        name="inproj",
    )(x2, nw.reshape(1, D), sh.reshape(-1, 1, D), sc.reshape(-1, 1, D), w_slab)


def _cmp_kernel(r_ref, pe_ref, w1a_ref, w1b_ref, w2_ref, o_ref):
    r = r_ref[...]
    n_rows = r.shape[0]
    a = jnp.dot(r, w1a_ref[...], preferred_element_type=F32)
    b = jnp.dot(r, w1b_ref[...], preferred_element_type=F32)
    pe = pe_ref[...]
    half = pe.shape[1] // 2
    bias = (jnp.dot(pe[:, :half], w1a_ref[...].astype(F32), preferred_element_type=F32, precision=HIGHEST)
            + jnp.dot(pe[:, half:], w1b_ref[...].astype(F32), preferred_element_type=F32, precision=HIGHEST))
    hid = a + pltpu.roll(b, n_rows - 1, 0) + bias
    act = 0.5 * hid * (1.0 + jnp.tanh(0.7978845608028654 * (hid + 0.044715 * hid * hid * hid)))
    o_ref[...] = jnp.dot(act, w2_ref[...], preferred_element_type=F32, precision=HIGHEST)


def _compress(rows, pe, w1, w2):
    BG, n_rows, width = rows.shape
    hidden = w1.shape[1]
    w1a = w1[:width].astype(BF16)
    w1b = w1[width:].astype(BF16)
    return pl.pallas_call(
        _cmp_kernel,
        grid=(BG,),
        in_specs=[pl.BlockSpec((None, n_rows, width), lambda i: (i, 0, 0)),
                  pl.BlockSpec((1, 2 * width), lambda i: (0, 0)),
                  pl.BlockSpec((width, hidden), lambda i: (0, 0)),
                  pl.BlockSpec((width, hidden), lambda i: (0, 0)),
                  pl.BlockSpec((hidden, NSA_DIM), lambda i: (0, 0))],
        out_specs=pl.BlockSpec((None, n_rows, NSA_DIM), lambda i: (i, 0, 0)),
        out_shape=jax.ShapeDtypeStruct((BG, n_rows, NSA_DIM), F32),
        compiler_params=_cparams(("parallel",)),
        name="cmp",
    )(rows, pe.reshape(1, -1).astype(F32), w1a, w1b, w2.astype(F32))


def _da_kernel(slopes_ref, qT_ref, k_ref, vT_ref, lq1_ref, lk1_ref, lq2_ref, lk2_ref, subw_ref,
               oT_ref, acc_ref, base_ref, *, tq, tk, lam_init):
    h = pl.program_id(1)
    qi = pl.program_id(2)
    slope = slopes_ref[h]
    q0 = qi * tq
    w = 2 * tq
    qT = qT_ref[...]
    row = lax.broadcasted_iota(I32, qT.shape, 0)
    zero = jnp.zeros_like(qT)
    q_both = jnp.concatenate([jnp.where(row < DA_DIM, qT, zero), jnp.where(row >= DA_DIM, qT, zero)], axis=1)
    key_off = lax.broadcasted_iota(I32, (tk, w), 0)
    base_ref[...] = slope * key_off.astype(F32)
    acc_ref[...] = jnp.zeros_like(acc_ref)

    def step(kt, carry, masked):
        m_old, l_old = carry
        k0 = kt * tk
        k_t = k_ref[pl.ds(pl.multiple_of(k0, tk), tk), :]
        off = slope * k0.astype(F32)
        s = jnp.dot(k_t, q_both, preferred_element_type=F32) + base_ref[...]
        if masked:
            lane = lax.broadcasted_iota(I32, (tk, w), 1)
            t_loc = jnp.where(lane >= tq, lane - tq, lane)
            s = jnp.where(key_off - t_loc <= q0 - k0, s, NEG)
        m_new = jnp.maximum(m_old, jnp.max(s, axis=0, keepdims=True) + off)
        p = jnp.exp(s - (m_new - off))
        alpha = jnp.exp(m_old - m_new)
        l_new = alpha * l_old + jnp.sum(p, axis=0, keepdims=True)
        acc_ref[...] = alpha * acc_ref[...] + jnp.dot(vT_ref[kt], p.astype(BF16), preferred_element_type=F32)
        return m_new, l_new

    init = (jnp.full((1, w), NEG, F32), jnp.zeros((1, w), F32))
    n_full = q0 // tk
    carry = lax.fori_loop(0, n_full, lambda kt, cr: step(kt, cr, False), init)
    _, l = step(n_full, carry, True)

    lam = (jnp.exp(jnp.sum(lq1_ref[...] * lk1_ref[...], axis=-1, keepdims=True))
           - jnp.exp(jnp.sum(lq2_ref[...] * lk2_ref[...], axis=-1, keepdims=True)) + lam_init)
    on = acc_ref[...] * (1.0 / l)
    o = on[:, :tq] - lam * on[:, tq:]
    r = lax.rsqrt(jnp.mean(o * o, axis=0, keepdims=True) + DA_SUBLN_EPS)
    oT_ref[...] = (o * r * subw_ref[...] * (1.0 - lam_init)).astype(BF16)


def _diff_attention(slopes, qT, slab3, vT, lq1, lk1, lq2, lk2, subw, lam_init, tq, tk):
    B, _, S = qT.shape
    nk = S // tk
    assert tk % tq == 0 and S % tk == 0
    tile = tq
    kcol = COL_DAK // 128
    vec = lambda a: a.reshape(1, DA_DIM).astype(F32)
    grid_spec = pltpu.PrefetchScalarGridSpec(
        num_scalar_prefetch=1,
        grid=(B, DA_HEADS, S // tile),
        in_specs=[pl.BlockSpec((None, 128, tile), lambda b, h, i, s: (b, h, i)),
                  pl.BlockSpec((None, S, 128), lambda b, h, i, s: (b, 0, kcol + h)),
                  pl.BlockSpec((None, None, nk, 128, tk), lambda b, h, i, s: (b, h, 0, 0, 0)),
                  pl.BlockSpec((1, DA_DIM), lambda b, h, i, s: (0, 0)),
                  pl.BlockSpec((1, DA_DIM), lambda b, h, i, s: (0, 0)),
                  pl.BlockSpec((1, DA_DIM), lambda b, h, i, s: (0, 0)),
                  pl.BlockSpec((1, DA_DIM), lambda b, h, i, s: (0, 0)),
                  pl.BlockSpec((DA_VDIM, 1), lambda b, h, i, s: (0, 0))],
        out_specs=pl.BlockSpec((None, 128, tile), lambda b, h, i, s: (b, h, i)),
        scratch_shapes=[pltpu.VMEM((DA_VDIM, 2 * tq), F32), pltpu.VMEM((tk, 2 * tq), F32)],
    )
    return pl.pallas_call(
        functools.partial(_da_kernel, tq=tq, tk=tk, lam_init=lam_init),
        grid_spec=grid_spec,
        out_shape=jax.ShapeDtypeStruct((B, DA_HEADS * DA_VDIM, S), BF16),
        compiler_params=_cparams(("parallel", "parallel", "arbitrary")),
        name="diffattn",
    )(slopes, qT, slab3, vT, vec(lq1), vec(lk1), vec(lq2), vec(lk2), subw.reshape(DA_VDIM, 1).astype(F32))


def _nsa_kernel(slopes_ref, qT_ref, kc_ref, vcT_ref, ks_ref, vsT_ref, kw_ref, vwT_ref, gT_ref, mT_ref, eT_ref,
                oT_ref, selb_ref, acc_ref, res_ref, base_ref, *, t, tk, tkw, n_top):
    g = pl.program_id(1)
    qi = pl.program_id(2)
    q0 = qi * t
    w = NSA_HPG * t
    n_cp = kc_ref.shape[0]
    n_sel = mT_ref.shape[0]
    lane1 = lax.broadcasted_iota(I32, (1, w), 1)
    slope_row = jnp.zeros((1, w), F32)
    for hh in range(NSA_HPG):
        slope_row = jnp.where((lane1 >= hh * t) & (lane1 < (hh + 1) * t), slopes_ref[g * NSA_HPG + hh], slope_row)
    q_all = qT_ref[...]
    gates = jax.nn.sigmoid(gT_ref[...])
    key_off = lax.broadcasted_iota(I32, (tk, w), 0)
    t_loc = jnp.bitwise_and(lax.broadcasted_iota(I32, (tk, w), 1), t - 1)
    base_ref[...] = slope_row * key_off.astype(F32)

    c_idx = lax.broadcasted_iota(I32, (n_cp, w), 0)
    t_c = q0 + jnp.bitwise_and(lax.broadcasted_iota(I32, (n_cp, w), 1), t - 1)
    d_c = t_c - (c_idx * CMP_STRIDE + (CMP_BLOCK - 1))
    ok_c = d_c >= 0
    s = jnp.dot(kc_ref[...], q_all, preferred_element_type=F32) - slope_row * d_c.astype(F32)
    s = jnp.where(ok_c, s, NEG)
    e = jnp.exp(s - jnp.max(s, axis=0, keepdims=True))
    inv = 1.0 / jnp.sum(e, axis=0, keepdims=True)
    p = jnp.where(ok_c, e * inv, 0.0)
    res_ref[...] = gates[0:1, :] * jnp.dot(vcT_ref[...], p.astype(BF16), preferred_element_type=F32)
    p_sum = p[:, 0:t]
    for hh in range(1, NSA_HPG):
        p_sum = p_sum + p[:, hh * t:(hh + 1) * t]

    p_hi = p_sum.astype(BF16)
    p_lo = (p_sum - p_hi.astype(F32)).astype(BF16)
    mT = mT_ref[...]
    imp = jnp.dot(mT, p_hi, preferred_element_type=F32) + jnp.dot(mT, p_lo, preferred_element_type=F32)
    blk = lax.broadcasted_iota(I32, (n_sel, t), 0)
    t_s = q0 + lax.broadcasted_iota(I32, (n_sel, t), 1)
    cur = jnp.right_shift(t_s, int(math.log2(SEL_BLOCK)))
    forced = (blk == 0) | (blk == cur) | (blk == cur - 1)
    score = jnp.where(blk * SEL_BLOCK <= t_s, imp + jnp.where(forced, FORCE_BONUS, 0.0), NEG)
    blk_f = blk.astype(F32)
    sel = jnp.zeros((n_sel, t), F32)
    for _ in range(n_top):
        _, idx = _first_max(score, blk_f)
        hit = blk_f == idx
        sel = jnp.where(hit, 1.0, sel)
        score = jnp.where(hit, -jnp.inf, score)
    selb = jnp.where(sel > 0.5, 0.0, NEG)
    selb_ref[...] = jnp.concatenate([selb] * NSA_HPG, axis=1).astype(BF16)

    init = (jnp.full((1, w), NEG, F32), jnp.zeros((1, w), F32))

    def attend(kt, carry, k_ref, vT_ref, rows, bias, mask):
        m_old, l_old = carry
        k0 = kt * rows
        k_t = k_ref[pl.ds(pl.multiple_of(k0, rows), rows), :]
        off = slope_row * k0.astype(F32)
        s = jnp.dot(k_t, q_all, preferred_element_type=F32) + base_ref[0:rows, :]
        if bias is not None:
            s = s + bias(k0)
        if mask is not None:
            s = jnp.where(mask(k0), s, NEG)
        m_new = jnp.maximum(m_old, jnp.max(s, axis=0, keepdims=True) + off)
        p = jnp.exp(s - (m_new - off))
        alpha = jnp.exp(m_old - m_new)
        l_new = alpha * l_old + jnp.sum(p, axis=0, keepdims=True)
        acc_ref[...] = alpha * acc_ref[...] + jnp.dot(vT_ref[kt], p.astype(BF16), preferred_element_type=F32)
        return m_new, l_new

    def sel_bias(k0):
        return jnp.dot(eT_ref[pl.ds(pl.multiple_of(k0, tk), tk), :], selb_ref[...], preferred_element_type=F32)

    def causal(k0):
        return key_off - t_loc <= q0 - k0

    acc_ref[...] = jnp.zeros_like(acc_ref)
    n_full = q0 // tk
    carry = lax.fori_loop(0, n_full, lambda kt, cr: attend(kt, cr, ks_ref, vsT_ref, tk, sel_bias, None), init)
    _, l = attend(n_full, carry, ks_ref, vsT_ref, tk, sel_bias, causal)
    res_ref[...] = res_ref[...] + gates[1:2, :] * (acc_ref[...] * (1.0 / l))

    acc_ref[...] = jnp.zeros_like(acc_ref)

    def in_window(k0):
        key_w = lax.broadcasted_iota(I32, (tkw, w), 0)
        t_w = jnp.bitwise_and(lax.broadcasted_iota(I32, (tkw, w), 1), t - 1)
        d = (t_w - key_w) + (q0 - k0)
        return (d >= 0) & (d < WINDOW)

    first = jnp.maximum(q0 - (WINDOW - 1), 0) // tkw
    last = (q0 + t - 1) // tkw
    _, l = lax.fori_loop(first, last + 1, lambda kt, cr: attend(kt, cr, kw_ref, vwT_ref, tkw, None, in_window), init)
    oT_ref[...] = (res_ref[...] + gates[2:3, :] * (acc_ref[...] * (1.0 / l))).astype(BF16)


def _nsa_attention(slopes, qT, kc, vcT, ks, vsT, kw, vwT, gT, mT, eT, t, tk, tkw):
    B, G, nq, _, w = qT.shape
    S = nq * t
    n_cp = kc.shape[2]
    n_sel = S // SEL_BLOCK
    n_top = min(SEL_TOPK, n_sel)
    assert tk % t == 0 and tkw <= tk and tkw % t == 0 and S % tk == 0 and t & (t - 1) == 0
    fixed = lambda b, g, i, s: (b, g, 0, 0)
    fixed5 = lambda b, g, i, s: (b, g, 0, 0, 0)
    tile5 = lambda b, g, i, s: (b, g, i, 0, 0)
    grid_spec = pltpu.PrefetchScalarGridSpec(
        num_scalar_prefetch=1,
        grid=(B, G, nq),
        in_specs=[pl.BlockSpec((None, None, None, NSA_DIM, w), tile5),
                  pl.BlockSpec((None, None, n_cp, NSA_DIM), fixed),
                  pl.BlockSpec((None, None, NSA_DIM, n_cp), fixed),
                  pl.BlockSpec((None, None, S, NSA_DIM), fixed),
                  pl.BlockSpec((None, None, S // tk, NSA_DIM, tk), fixed5),
                  pl.BlockSpec((None, None, S, NSA_DIM), fixed),
                  pl.BlockSpec((None, None, S // tkw, NSA_DIM, tkw), fixed5),
                  pl.BlockSpec((None, None, None, 3, w), tile5),
                  pl.BlockSpec((n_sel, n_cp), lambda b, g, i, s: (0, 0)),
                  pl.BlockSpec((S, n_sel), lambda b, g, i, s: (0, 0))],
        out_specs=pl.BlockSpec((None, None, None, NSA_DIM, w), tile5),
        scratch_shapes=[pltpu.VMEM((n_sel, w), BF16),
                        pltpu.VMEM((NSA_DIM, w), F32),
                        pltpu.VMEM((NSA_DIM, w), F32),
                        pltpu.VMEM((tk, w), F32)],
    )
    return pl.pallas_call(
        functools.partial(_nsa_kernel, t=t, tk=tk, tkw=tkw, n_top=n_top),
        grid_spec=grid_spec,
        out_shape=jax.ShapeDtypeStruct((B, G, nq, NSA_DIM, w), BF16),
        compiler_params=_cparams(("parallel", "parallel", "arbitrary")),
        name="nsa",
    )(slopes, qT, kc, vcT, ks, vsT, kw, vwT, gT, mT, eT)


def _merge_kernel(oa_ref, ob_ref, ga_ref, gb_ref, x_ref, g1_ref, nw_ref, sh_ref, sc_ref,
                  wa_ref, wb_ref, wo_ref, wr_ref, x1_ref, h2_ref, lg_ref):
    ya = jnp.dot(oa_ref[...], wa_ref[...], preferred_element_type=F32)
    yb = jnp.dot(ob_ref[...], wb_ref[...], preferred_element_type=F32)
    merged = (jax.nn.sigmoid(ga_ref[...].astype(F32)) * ya
              + jax.nn.sigmoid(gb_ref[...].astype(F32)) * yb).astype(BF16)
    mix = jnp.dot(merged, wo_ref[...], preferred_element_type=F32)
    x1 = x_ref[...] + g1_ref[...] * mix
    x1_ref[...] = x1
    h2 = _modulate(x1, nw_ref[...], sh_ref[...], sc_ref[...])
    h2_ref[...] = h2
    lg_ref[...] = jnp.dot(h2, wr_ref[...], preferred_element_type=F32, precision=HIGHEST)


def _merge(oa, ob, slab, x2, g1, nw, sh, sc, wa, wb, wo, wr, S):
    N, D = x2.shape
    tm = 256
    per_b = S // tm
    mcol = COL_MERGE // D
    row = lambda i: (i, 0)
    full = lambda i: (0, 0)
    perb = lambda i: (i // per_b, 0, 0)
    n_lg = wr.shape[1]
    return pl.pallas_call(
        _merge_kernel,
        grid=(N // tm,),
        in_specs=[pl.BlockSpec((tm, oa.shape[1]), row),
                  pl.BlockSpec((tm, ob.shape[1]), row),
                  pl.BlockSpec((tm, D), lambda i: (i, mcol)),
                  pl.BlockSpec((tm, D), lambda i: (i, mcol + 1)),
                  pl.BlockSpec((tm, D), row),
                  pl.BlockSpec((None, 1, D), perb),
                  pl.BlockSpec((1, D), full),
                  pl.BlockSpec((None, 1, D), perb),
                  pl.BlockSpec((None, 1, D), perb),
                  pl.BlockSpec(wa.shape, full),
                  pl.BlockSpec(wb.shape, full),
                  pl.BlockSpec(wo.shape, full),
                  pl.BlockSpec(wr.shape, full)],
        out_specs=[pl.BlockSpec((tm, D), row), pl.BlockSpec((tm, D), row), pl.BlockSpec((tm, n_lg), row)],
        out_shape=[jax.ShapeDtypeStruct((N, D), F32), jax.ShapeDtypeStruct((N, D), F32),
                   jax.ShapeDtypeStruct((N, n_lg), F32)],
        compiler_params=_cparams(("parallel",)),
        name="merge",
    )(oa, ob, slab, slab, x2, g1.reshape(-1, 1, D), nw.reshape(1, D), sh.reshape(-1, 1, D),
      sc.reshape(-1, 1, D), wa, wb, wo, wr)


def _first_max(v, idx):
    mx = jnp.max(v, axis=0, keepdims=True)
    first = jnp.min(jnp.where(v == mx, idx, float(v.shape[0])), axis=0, keepdims=True)
    return mx, first


def _route_kernel(lg_ref, b_ref, e_ref, w_ref):
    lg = lg_ref[...]
    tn = lg.shape[1]
    per_g = N_EXPERTS // N_EXPERT_GROUPS
    scores = jax.nn.sigmoid(lg)
    biased = scores + b_ref[...]
    gi = lax.broadcasted_iota(I32, (N_EXPERT_GROUPS, tn), 0).astype(F32)
    gscore = jnp.zeros((N_EXPERT_GROUPS, tn), F32)
    for g in range(N_EXPERT_GROUPS):
        grp = biased[g * per_g:(g + 1) * per_g, :]
        m1, f1 = _first_max(grp, gi)
        m2 = jnp.max(jnp.where(gi == f1, -jnp.inf, grp), axis=0, keepdims=True)
        gscore = jnp.where(gi == g, m1 + m2, gscore)
    gsel = jnp.zeros((N_EXPERT_GROUPS, tn), F32)
    for _ in range(TOPK_EXPERT_GROUPS):
        _, f = _first_max(gscore, gi)
        hit = gi == f
        gsel = jnp.where(hit, 1.0, gsel)
        gscore = jnp.where(hit, -jnp.inf, gscore)
    ei = lax.broadcasted_iota(I32, (N_EXPERTS, tn), 0).astype(F32)
    emask = jnp.zeros((N_EXPERTS, tn), F32)
    for g in range(N_EXPERT_GROUPS):
        in_g = (ei >= g * per_g) & (ei < (g + 1) * per_g)
        emask = jnp.where(in_g, gsel[g:g + 1, :], emask)
    masked = jnp.where(emask > 0.5, biased, NEG)
    e_out = jnp.zeros((MOE_TOPK, tn), F32)
    w_out = jnp.zeros((MOE_TOPK, tn), F32)
    for r in range(MOE_TOPK):
        _, f = _first_max(masked, ei)
        hit = ei == f
        wv = jnp.sum(jnp.where(hit, scores, 0.0), axis=0, keepdims=True)
        e_out = jnp.where(gi == r, f, e_out)
        w_out = jnp.where(gi == r, wv, w_out)
        masked = jnp.where(hit, -jnp.inf, masked)
    e_ref[...] = e_out.astype(I32)
    w_ref[...] = w_out / jnp.sum(w_out, axis=0, keepdims=True) * ROUTED_SCALE


def _route(lgT, rb):
    E, N = lgT.shape
    tn = 512
    return pl.pallas_call(
        _route_kernel,
        grid=(N // tn,),
        in_specs=[pl.BlockSpec((E, tn), lambda i: (0, i)), pl.BlockSpec((E, 1), lambda i: (0, 0))],
        out_specs=[pl.BlockSpec((MOE_TOPK, tn), lambda i: (0, i)), pl.BlockSpec((MOE_TOPK, tn), lambda i: (0, i))],
        out_shape=[jax.ShapeDtypeStruct((MOE_TOPK, N), I32), jax.ShapeDtypeStruct((MOE_TOPK, N), F32)],
        compiler_params=_cparams(("parallel",)),
        name="route",
    )(lgT, rb.reshape(E, 1).astype(F32))


def _slots_kernel(e_ref, tri_ref, lt_ref, dest_ref, cnt_ref, carry_ref, *, rb):
    phase = pl.program_id(0)
    i = pl.program_id(1)
    e = e_ref[...]
    tn = e.shape[1]
    ei = lax.broadcasted_iota(I32, (N_EXPERTS, tn), 0)

    @pl.when((phase == 0) & (i == 0))
    def _():
        carry_ref[...] = jnp.zeros_like(carry_ref)

    @pl.when(phase == 0)
    def _():
        tot = jnp.zeros((N_EXPERTS, 1), F32)
        for k in range(MOE_TOPK):
            oh = jnp.where(ei == e[k:k + 1, :], 1.0, 0.0)
            tot = tot + jnp.sum(oh, axis=1, keepdims=True)
        carry_ref[...] = carry_ref[...] + tot
        dest_ref[...] = jnp.zeros_like(dest_ref)

    @pl.when((phase == 1) & (i == 0))
    def _():
        cnt = carry_ref[...]
        cnt_ref[...] = cnt
        padded = jnp.broadcast_to(jnp.ceil(cnt / rb) * rb, (N_EXPERTS, 128))
        first = jnp.dot(lt_ref[...], padded, preferred_element_type=F32, precision=HIGHEST)
        carry_ref[...] = first[:, :1]

    @pl.when(phase == 1)
    def _():
        carry = carry_ref[...]
        ki = lax.broadcasted_iota(I32, (MOE_TOPK, tn), 0)
        dest = jnp.zeros((MOE_TOPK, tn), F32)
        for k in range(MOE_TOPK):
            hit = ei == e[k:k + 1, :]
            oh = jnp.where(hit, 1.0, 0.0)
            before = jnp.dot(oh.astype(BF16), tri_ref[...], preferred_element_type=F32)
            row = jnp.sum(jnp.where(hit, before + carry, 0.0), axis=0, keepdims=True)
            dest = jnp.where(ki == k, row, dest)
            carry = carry + jnp.sum(oh, axis=1, keepdims=True)
        carry_ref[...] = carry
        dest_ref[...] = dest.astype(I32)


def _slots(top_eT, rb):
    K, N = top_eT.shape
    tn = 512
    idx = np.arange(tn)
    tri = jnp.asarray((idx[:, None] < idx[None, :]).astype(np.float32), BF16)
    e_idx = np.arange(N_EXPERTS)
    lt = jnp.asarray((e_idx[None, :] < e_idx[:, None]).astype(np.float32), F32)
    return pl.pallas_call(
        functools.partial(_slots_kernel, rb=rb),
        grid=(2, N // tn),
        in_specs=[pl.BlockSpec((K, tn), lambda p, i: (0, i)),
                  pl.BlockSpec((tn, tn), lambda p, i: (0, 0)),
                  pl.BlockSpec((N_EXPERTS, N_EXPERTS), lambda p, i: (0, 0))],
        out_specs=[pl.BlockSpec((K, tn), lambda p, i: (0, i * p)),
                   pl.BlockSpec((N_EXPERTS, 1), lambda p, i: (0, 0))],
        out_shape=[jax.ShapeDtypeStruct((K, N), I32), jax.ShapeDtypeStruct((N_EXPERTS, 1), F32)],
        scratch_shapes=[pltpu.VMEM((N_EXPERTS, 1), F32)],
        compiler_params=_cparams(("arbitrary", "arbitrary")),
        name="slots",
    )(top_eT, tri, lt)


def _dispatch_kernel(dest_ref, h_ref, xs_in_ref, xs_ref, sem, *, tt):
    del xs_in_ref

    def body(n, c):
        src = h_ref.at[pl.ds(n, 1), :]
        for k in range(MOE_TOPK):
            d = dest_ref[0, 0, k * tt + n]
            pltpu.make_async_copy(src, xs_ref.at[pl.ds(d, 1), :], sem).start()
        return c

    lax.fori_loop(0, tt, body, 0)
    for k in range(MOE_TOPK):
        pltpu.make_async_copy(h_ref, xs_ref.at[pl.ds(0, tt), :], sem).wait()


def _dispatch(dest_tiles, h2, n_rows, tt):
    N, D = h2.shape
    xs0 = jnp.zeros((n_rows, D), h2.dtype)
    return pl.pallas_call(
        functools.partial(_dispatch_kernel, tt=tt),
        grid=(N // tt,),
        in_specs=[pl.BlockSpec((1, 1, MOE_TOPK * tt), lambda i: (i, 0, 0), memory_space=pltpu.SMEM),
                  pl.BlockSpec((tt, D), lambda i: (i, 0)),
                  pl.BlockSpec(memory_space=pl.ANY)],
        out_specs=pl.BlockSpec(memory_space=pl.ANY),
        out_shape=jax.ShapeDtypeStruct((n_rows, D), h2.dtype),
        scratch_shapes=[pltpu.SemaphoreType.DMA(())],
        input_output_aliases={2: 0},
        compiler_params=_cparams(("arbitrary",)),
        name="dispatch",
    )(dest_tiles, h2, xs0)


def _experts_kernel(be_ref, x_ref, wgu_ref, wd_ref, y_ref):
    del be_ref
    x = x_ref[...].astype(BF16)
    gu = jnp.dot(x, wgu_ref[...], preferred_element_type=F32)
    hdim = gu.shape[1] // 2
    g = gu[:, :hdim]
    act = (g * jax.nn.sigmoid(g) * gu[:, hdim:]).astype(BF16)
    y_ref[...] = jnp.dot(act, wd_ref[...], preferred_element_type=F32)


def _experts(blk_e, xs, wgu, wd, rb):
    n_rows, D = xs.shape
    grid_spec = pltpu.PrefetchScalarGridSpec(
        num_scalar_prefetch=1,
        grid=(n_rows // rb,),
        in_specs=[pl.BlockSpec((rb, D), lambda i, be: (i, 0)),
                  pl.BlockSpec((None,) + wgu.shape[1:], lambda i, be: (be[i], 0, 0)),
                  pl.BlockSpec((None,) + wd.shape[1:], lambda i, be: (be[i], 0, 0))],
        out_specs=pl.BlockSpec((rb, D), lambda i, be: (i, 0)),
    )
    return pl.pallas_call(
        _experts_kernel,
        grid_spec=grid_spec,
        out_shape=jax.ShapeDtypeStruct((n_rows, D), F32),
        compiler_params=_cparams(("arbitrary",)),
        name="experts",
    )(blk_e, xs, wgu, wd)


def _combine_kernel(dest_ref, ys_ref, w_ref, h_ref, x1_ref, g2_ref, fw_ref, sgu_ref, sd_ref, o_ref, buf_ref, sem, *, tt):
    def body(n, c):
        for k in range(MOE_TOPK):
            d = dest_ref[0, 0, k * tt + n]
            pltpu.make_async_copy(ys_ref.at[pl.ds(d, 1), :], buf_ref.at[k, pl.ds(n, 1), :], sem).start()
        return c

    lax.fori_loop(0, tt, body, 0)

    h = h_ref[...].astype(BF16)
    gu = jnp.dot(h, sgu_ref[...], preferred_element_type=F32)
    hdim = gu.shape[1] // 2
    g = gu[:, :hdim]
    ffn = jnp.dot((g * jax.nn.sigmoid(g) * gu[:, hdim:]).astype(BF16), sd_ref[...], preferred_element_type=F32)

    for k in range(MOE_TOPK):
        pltpu.make_async_copy(ys_ref.at[pl.ds(0, tt), :], buf_ref.at[k], sem).wait()
    w = w_ref[...]
    for k in range(MOE_TOPK):
        ffn = ffn + w[:, k:k + 1] * buf_ref[k]
    x2 = x1_ref[...] + g2_ref[...] * ffn
    r = lax.rsqrt(jnp.mean(x2 * x2, axis=-1, keepdims=True) + NORM_EPS)
    o_ref[...] = x2 * r * fw_ref[...]


def _combine(dest_tiles, ys, top_w, h2, x1, g2, fw, sgu, sd, S, tt):
    N, D = h2.shape
    per_b = S // tt
    row = lambda i: (i, 0)
    full = lambda i: (0, 0)
    return pl.pallas_call(
        functools.partial(_combine_kernel, tt=tt),
        grid=(N // tt,),
        in_specs=[pl.BlockSpec((1, 1, MOE_TOPK * tt), lambda i: (i, 0, 0), memory_space=pltpu.SMEM),
                  pl.BlockSpec(memory_space=pl.ANY),
                  pl.BlockSpec((tt, MOE_TOPK), row),
                  pl.BlockSpec((tt, D), row),
                  pl.BlockSpec((tt, D), row),
                  pl.BlockSpec((None, 1, D), lambda i: (i // per_b, 0, 0)),
                  pl.BlockSpec((1, D), full),
                  pl.BlockSpec(sgu.shape, full),
                  pl.BlockSpec(sd.shape, full)],
        out_specs=pl.BlockSpec((tt, D), row),
        out_shape=jax.ShapeDtypeStruct((N, D), F32),
        scratch_shapes=[pltpu.VMEM((MOE_TOPK, tt, D), F32), pltpu.SemaphoreType.DMA(())],
        compiler_params=_cparams(("arbitrary",)),
        name="combine",
    )(dest_tiles, ys, top_w, h2, x1, g2.reshape(-1, 1, D), fw.reshape(1, D), sgu, sd)


def _cmp_to_sel_T(n_cp, n_sel):
    c0 = np.arange(n_cp)[None, :] * CMP_STRIDE
    s0 = np.arange(n_sel)[:, None] * SEL_BLOCK
    ov = np.minimum(c0 + CMP_BLOCK, s0 + SEL_BLOCK) - np.maximum(c0, s0)
    m = np.clip(ov, 0, None).astype(np.float32) / CMP_BLOCK
    m[:, n_cp - 1] = 0.0
    return jnp.asarray(m, BF16)


def _slab_weights(w_in):
    D = w_in.shape[0]
    sizes = [1024, 1024, 1024, 512, 128, 128, 128, 128, 128, 128, 24, 2048]
    offs = np.concatenate([[0], np.cumsum(sizes)])
    part = lambda i: w_in[:, offs[i]:offs[i + 1]]
    pieces = [part(0), part(1), part(2), part(11), part(3)] + [part(i) for i in range(4, 10)] + [part(10)]
    w = jnp.concatenate(pieces, axis=1)
    return jnp.pad(w, ((0, 0), (0, SLAB_COLS - w.shape[1]))).astype(BF16)


def _layer(x, c, ada_w, ada_b, norm1_w, w_in, da_lq1, da_lk1, da_lq2, da_lk2, da_subln_w,
           cmp_k_pe, cmp_k_w1, cmp_k_w2, cmp_v_pe, cmp_v_w1, cmp_v_w2, w_da_out, w_nsa_out, w_o,
           norm2_w, router_w, router_b, exp_w_gate, exp_w_up, exp_w_down,
           sh_w_gate, sh_w_up, sh_w_down, final_norm_w, lam_init):
    B, S, D = x.shape
    N = B * S
    G, HPG = NSA_GROUPS, NSA_HPG
    x2 = x.reshape(N, D)

    mod = _ada(c, ada_w, ada_b)
    sh1, sc1, g1, sh2, sc2, g2 = jnp.split(mod, 6, axis=-1)

    slab = _inproj(x2, norm1_w, sh1, sc1, _slab_weights(w_in), S)
    slab3 = slab.reshape(B, S, SLAB_COLS)

    i_all = np.arange(1, DA_HEADS + NSA_HEADS + 1, dtype=np.float32)
    slopes = (2.0 ** (-8.0 * i_all / (DA_HEADS + NSA_HEADS))).astype(np.float32)
    slopes_a = jnp.asarray(slopes[0::2])
    slopes_b = jnp.asarray(slopes[1::2])

    ta_q, ta_k = 256, min(512, S)
    qT = (slab3[:, :, COL_DAQ:COL_DAQ + 1024] * jnp.asarray(DA_DIM ** -0.5, BF16)).transpose(0, 2, 1)
    vT = slab3[:, :, COL_DAV:COL_DAV + 1024].reshape(B, S // ta_k, ta_k, DA_HEADS, DA_VDIM).transpose(0, 3, 1, 4, 2)
    oaT = _diff_attention(slopes_a, qT, slab3, vT, da_lq1, da_lk1, da_lq2, da_lk2, da_subln_w, lam_init,
                          ta_q, ta_k)
    oa = oaT.transpose(0, 2, 1).reshape(N, DA_HEADS * DA_VDIM)

    tb, tb_k, tb_w = 128, 256, 128
    n_cp = S // CMP_STRIDE
    n_sel = S // SEL_BLOCK

    def kv_groups(j):
        c0 = COL_KV6 + 128 * j
        return slab3[:, :, c0:c0 + 128].reshape(B, S, G, NSA_DIM).transpose(0, 2, 1, 3)

    def kv_tiles_T(a, rows):
        return a.reshape(B, G, S // rows, rows, NSA_DIM).transpose(0, 1, 2, 4, 3)

    ck_rows = kv_groups(0).reshape(B * G, n_cp, CMP_STRIDE * NSA_DIM)
    cv_rows = kv_groups(1).reshape(B * G, n_cp, CMP_STRIDE * NSA_DIM)
    kc = _compress(ck_rows, cmp_k_pe, cmp_k_w1, cmp_k_w2).reshape(B, G, n_cp, NSA_DIM).astype(BF16)
    vc = _compress(cv_rows, cmp_v_pe, cmp_v_w1, cmp_v_w2).reshape(B, G, n_cp, NSA_DIM).astype(BF16)
    nq = (slab3[:, :, COL_NSAQ:COL_NSAQ + 512] * jnp.asarray(NSA_DIM ** -0.5, BF16))
    nqT = nq.reshape(B, S // tb, tb, G, HPG, NSA_DIM).transpose(0, 3, 1, 5, 4, 2).reshape(B, G, S // tb, NSA_DIM, HPG * tb)
    gT = (slab3[:, :, COL_NSAG:COL_NSAG + 3 * NSA_HEADS].astype(F32).reshape(B, S // tb, tb, G, HPG, 3)
          .transpose(0, 3, 1, 5, 4, 2).reshape(B, G, S // tb, 3, HPG * tb))
    eT = jnp.asarray((np.arange(S)[:, None] // SEL_BLOCK == np.arange(n_sel)[None, :]).astype(np.float32), BF16)
    obT = _nsa_attention(slopes_b, nqT, kc, vc.transpose(0, 1, 3, 2), kv_groups(2), kv_tiles_T(kv_groups(3), tb_k),
                         kv_groups(4), kv_tiles_T(kv_groups(5), tb_w), gT, _cmp_to_sel_T(n_cp, n_sel), eT,
                         tb, min(tb_k, S), tb_w)
    ob = (obT.reshape(B, G, S // tb, NSA_DIM, HPG, tb).transpose(0, 2, 5, 1, 4, 3)
          .reshape(N, NSA_HEADS * NSA_DIM))

    x1, h2, logits = _merge(oa, ob, slab, x2, g1, norm2_w, sh2, sc2, w_da_out.astype(BF16),
                            w_nsa_out.astype(BF16), w_o.astype(BF16), router_w.astype(F32), S)

    rb = 256
    tt = 128
    top_eT, top_wT = _route(logits.T, router_b)
    destT, counts = _slots(top_eT, rb)
    n_rows = ((N * MOE_TOPK + N_EXPERTS * (rb - 1) + rb - 1) // rb) * rb
    padded = (jnp.ceil(counts[:, 0] / rb) * rb).astype(I32)
    pend = jnp.cumsum(padded)
    blk_start = jnp.arange(n_rows // rb, dtype=I32) * rb
    blk_e = jnp.minimum(jnp.sum((pend[None, :] <= blk_start[:, None]).astype(I32), axis=1), N_EXPERTS - 1)
    dest_tiles = destT.reshape(MOE_TOPK, N // tt, tt).transpose(1, 0, 2).reshape(N // tt, 1, MOE_TOPK * tt)
    xs = _dispatch(dest_tiles, h2, n_rows, tt)
    wgu = jnp.concatenate([exp_w_gate, exp_w_up], axis=-1).astype(BF16)
    ys = _experts(blk_e, xs, wgu, exp_w_down.astype(BF16), rb)
    sgu = jnp.concatenate([sh_w_gate, sh_w_up], axis=-1).astype(BF16)
    out = _combine(dest_tiles, ys, top_wT.T, h2, x1, g2, final_norm_w, sgu, sh_w_down.astype(BF16), S, tt)
    return out.reshape(B, S, D)


def kernel(x, c, ada_w, ada_b, norm1_w, w_in, da_lq1, da_lk1, da_lq2, da_lk2, da_subln_w, cmp_k_pe, cmp_k_w1, cmp_k_w2, cmp_v_pe, cmp_v_w1, cmp_v_w2, w_da_out, w_nsa_out, w_o, norm2_w, router_w, router_b, exp_w_gate, exp_w_up, exp_w_down, sh_w_gate, sh_w_up, sh_w_down, final_norm_w):
    depth = ada_w.shape[0]
    assert depth == 1, "one decoder layer"
    lam_init = 0.8 - 0.6 * math.exp(-0.3 * 0)
    return _layer(x, c, ada_w[0], ada_b[0], norm1_w[0], w_in[0], da_lq1[0], da_lk1[0], da_lq2[0], da_lk2[0],
                  da_subln_w[0], cmp_k_pe[0], cmp_k_w1[0], cmp_k_w2[0], cmp_v_pe[0], cmp_v_w1[0], cmp_v_w2[0],
                  w_da_out[0], w_nsa_out[0], w_o[0], norm2_w[0], router_w[0], router_b[0],
                  exp_w_gate[0], exp_w_up[0], exp_w_down[0], sh_w_gate[0], sh_w_up[0], sh_w_down[0],
                  final_norm_w, lam_init)
```

```python
import functools
import math

import numpy as np
import jax
import jax.numpy as jnp
from jax import lax
from jax.experimental import pallas as pl
from jax.experimental.pallas import tpu as pltpu

F32 = jnp.float32
BF16 = jnp.bfloat16
I32 = jnp.int32
HIGHEST = lax.Precision.HIGHEST

NORM_EPS = 1e-6
NEG = -1e30

DA_HEADS = 8
DA_DIM = 64
DA_VDIM = 128
DA_SUBLN_EPS = 1e-5

NSA_HEADS = 8
NSA_GROUPS = 2
NSA_HPG = 4
NSA_DIM = 64
CMP_BLOCK = 32
CMP_STRIDE = 16
SEL_BLOCK = 64
SEL_TOPK = 16
WINDOW = 512
FORCE_BONUS = 1e4
SEL_COLS = 128

N_EXPERTS = 64
MOE_TOPK = 8
N_EXPERT_GROUPS = 8
TOPK_EXPERT_GROUPS = 4
ROUTED_SCALE = 2.5

VMEM_LIMIT_V7X = 56 * 1024 * 1024

COL_DAQ, COL_DAK, COL_DAV = 0, 1024, 2048
COL_MERGE = 3072
COL_NSAQ = 5120
COL_KV6 = 5632
COL_NSAG = 6400
SLAB_COLS = 6528


def _cparams(sem, vmem=VMEM_LIMIT_V7X):
    return pltpu.CompilerParams(dimension_semantics=sem, vmem_limit_bytes=vmem)


def _ada_kernel(c_ref, w_ref, b_ref, o_ref):
    c = c_ref[...]
    ca = c * jax.nn.sigmoid(c)
    o_ref[...] = jnp.dot(ca, w_ref[...], preferred_element_type=F32, precision=HIGHEST) + b_ref[...]


def _ada(c, w, b):
    B, D = c.shape
    n_out = w.shape[1]
    rows = 8
    cp = jnp.zeros((rows, D), F32).at[:B].set(c)
    tn = 1024
    out = pl.pallas_call(
        _ada_kernel,
        grid=(n_out // tn,),
        in_specs=[pl.BlockSpec((rows, D), lambda j: (0, 0)),
                  pl.BlockSpec((D, tn), lambda j: (0, j)),
                  pl.BlockSpec((1, tn), lambda j: (0, j))],
        out_specs=pl.BlockSpec((rows, tn), lambda j: (0, j)),
        out_shape=jax.ShapeDtypeStruct((rows, n_out), F32),
        compiler_params=_cparams(("arbitrary",)),
        name="ada",
    )(cp, w, b.reshape(1, n_out))
    return out[:B]


def _modulate(x, nw, sh, sc):
    r = lax.rsqrt(jnp.mean(x * x, axis=-1, keepdims=True) + NORM_EPS)
    return (x * r) * nw * (1.0 + sc) + sh


def _inproj_kernel(x_ref, nw_ref, sh_ref, sc_ref, w_ref, o_ref, h_ref):
    @pl.when(pl.program_id(1) == 0)
    def _():
        h_ref[...] = _modulate(x_ref[...], nw_ref[...], sh_ref[...], sc_ref[...]).astype(BF16)

    o_ref[...] = jnp.dot(h_ref[...], w_ref[...], preferred_element_type=F32).astype(BF16)


def _inproj(x2, nw, sh, sc, w_slab, S):
    N, D = x2.shape
    tm = 512
    tn = SLAB_COLS // 3
    per_b = S // tm
    return pl.pallas_call(
        _inproj_kernel,
        grid=(N // tm, SLAB_COLS // tn),
        in_specs=[pl.BlockSpec((tm, D), lambda i, j: (i, 0)),
                  pl.BlockSpec((1, D), lambda i, j: (0, 0)),
                  pl.BlockSpec((None, 1, D), lambda i, j: (i // per_b, 0, 0)),
                  pl.BlockSpec((None, 1, D), lambda i, j: (i // per_b, 0, 0)),
                  pl.BlockSpec((D, tn), lambda i, j: (0, j))],
        out_specs=pl.BlockSpec((tm, tn), lambda i, j: (i, j)),
        out_shape=jax.ShapeDtypeStruct((N, SLAB_COLS), BF16),
        scratch_shapes=[pltpu.VMEM((tm, D), BF16)],
        compiler_params=_cparams(("parallel", "arbitrary")),
        name="inproj",
    )(x2, nw.reshape(1, D), sh.reshape(-1, 1, D), sc.reshape(-1, 1, D), w_slab)


def _cmp_kernel(r_ref, pe_ref, w1a_ref, w1b_ref, w2_ref, o_ref):
    r = r_ref[...]
    n_rows = r.shape[0]
    a = jnp.dot(r, w1a_ref[...], preferred_element_type=F32)
    b = jnp.dot(r, w1b_ref[...], preferred_element_type=F32)
    pe = pe_ref[...]
    half = pe.shape[1] // 2
    bias = (jnp.dot(pe[:, :half], w1a_ref[...].astype(F32), preferred_element_type=F32, precision=HIGHEST)
            + jnp.dot(pe[:, half:], w1b_ref[...].astype(F32), preferred_element_type=F32, precision=HIGHEST))
    hid = a + pltpu.roll(b, n_rows - 1, 0) + bias
    act = 0.5 * hid * (1.0 + jnp.tanh(0.7978845608028654 * (hid + 0.044715 * hid * hid * hid)))
    o_ref[...] = jnp.dot(act, w2_ref[...], preferred_element_type=F32, precision=HIGHEST)


def _compress(rows, pe, w1, w2):
    BG, n_rows, width = rows.shape
    hidden = w1.shape[1]
    w1a = w1[:width].astype(BF16)
    w1b = w1[width:].astype(BF16)
    return pl.pallas_call(
        _cmp_kernel,
        grid=(BG,),
        in_specs=[pl.BlockSpec((None, n_rows, width), lambda i: (i, 0, 0)),
                  pl.BlockSpec((1, 2 * width), lambda i: (0, 0)),
                  pl.BlockSpec((width, hidden), lambda i: (0, 0)),
                  pl.BlockSpec((width, hidden), lambda i: (0, 0)),
                  pl.BlockSpec((hidden, NSA_DIM), lambda i: (0, 0))],
        out_specs=pl.BlockSpec((None, n_rows, NSA_DIM), lambda i: (i, 0, 0)),
        out_shape=jax.ShapeDtypeStruct((BG, n_rows, NSA_DIM), F32),
        compiler_params=_cparams(("parallel",)),
        name="cmp",
    )(rows, pe.reshape(1, -1).astype(F32), w1a, w1b, w2.astype(F32))


def _da_kernel(slopes_ref, qT_ref, k_ref, vT_ref, lq1_ref, lk1_ref, lq2_ref, lk2_ref, subw_ref,
               oT_ref, acc_ref, base_ref, sa_ref, sb_ref, pa_ref, pb_ref, *, tq, tk, lam_init):
    h = pl.program_id(1)
    qi = pl.program_id(2)
    slope = slopes_ref[h]
    q0 = qi * tq
    w = 2 * tq
    nk = k_ref.shape[0] // tk
    qT = qT_ref[...]
    row = lax.broadcasted_iota(I32, qT.shape, 0)
    zero = jnp.zeros_like(qT)
    q_both = jnp.concatenate([jnp.where(row < DA_DIM, qT, zero), jnp.where(row >= DA_DIM, qT, zero)], axis=1)
    key_off = lax.broadcasted_iota(I32, (tk, w), 0)
    base_ref[...] = slope * key_off.astype(F32)

    def scores(kt, s_ref):
        k_t = k_ref[pl.ds(pl.multiple_of(kt * tk, tk), tk), :]
        s = jnp.dot(k_t, q_both, preferred_element_type=F32) + base_ref[...]
        s_ref[...] = s
        return jnp.max(s, axis=0, keepdims=True)

    def probs(s_ref, p_ref, mx, off, m_old, l_old):
        m_new = jnp.maximum(m_old, mx + off)
        p = jnp.exp(s_ref[...] - (m_new - off))
        alpha = jnp.exp(m_old - m_new)
        p_ref[...] = p.astype(BF16)
        return m_new, alpha * l_old + jnp.sum(p, axis=0, keepdims=True), alpha

    def accumulate(kt, p_ref, alpha):
        acc_ref[...] = alpha * acc_ref[...] + jnp.dot(vT_ref[kt], p_ref[...], preferred_element_type=F32)

    n_full = q0 // tk
    mx_a = scores(0, sa_ref)

    kd = n_full * tk
    s = jnp.dot(k_ref[pl.ds(pl.multiple_of(kd, tk), tk), :], q_both, preferred_element_type=F32) + base_ref[...]
    lane = lax.broadcasted_iota(I32, (tk, w), 1)
    t_loc = jnp.where(lane >= tq, lane - tq, lane)
    s = jnp.where(key_off - t_loc <= q0 - kd, s, NEG)
    off_d = slope * kd.astype(F32)
    m = jnp.max(s, axis=0, keepdims=True) + off_d
    p = jnp.exp(s - (m - off_d))
    l = jnp.sum(p, axis=0, keepdims=True)
    acc_ref[...] = jnp.dot(vT_ref[n_full], p.astype(BF16), preferred_element_type=F32)
    pb_ref[...] = jnp.zeros_like(pb_ref)

    def pair(j, carry):
        m, l, mx_a, alpha_b = carry
        ta = 2 * j
        tb = ta + 1
        off_a = slope * (ta * tk).astype(F32)
        off_b = jnp.where(tb < n_full, slope * (tb * tk).astype(F32), NEG)
        accumulate(jnp.maximum(ta - 1, 0), pb_ref, alpha_b)
        m, l, alpha_a = probs(sa_ref, pa_ref, mx_a, off_a, m, l)
        mx_b = scores(jnp.minimum(tb, nk - 1), sb_ref)
        accumulate(ta, pa_ref, alpha_a)
        m, l, alpha_b = probs(sb_ref, pb_ref, mx_b, off_b, m, l)
        mx_a = scores(jnp.minimum(ta + 2, nk - 1), sa_ref)
        return m, l, mx_a, alpha_b

    n_pairs = (n_full + 1) // 2
    m, l, _, alpha_b = lax.fori_loop(0, n_pairs, pair, (m, l, mx_a, jnp.ones((1, w), F32)))
    accumulate(jnp.minimum(jnp.maximum(2 * n_pairs - 1, 0), nk - 1), pb_ref, alpha_b)

    lam = (jnp.exp(jnp.sum(lq1_ref[...] * lk1_ref[...], axis=-1, keepdims=True))
           - jnp.exp(jnp.sum(lq2_ref[...] * lk2_ref[...], axis=-1, keepdims=True)) + lam_init)
    on = acc_ref[...] * (1.0 / l)
    o = on[:, :tq] - lam * on[:, tq:]
    r = lax.rsqrt(jnp.mean(o * o, axis=0, keepdims=True) + DA_SUBLN_EPS)
    oT_ref[...] = (o * r * subw_ref[...] * (1.0 - lam_init)).astype(BF16)


def _diff_attention(slopes, qT, slab3, vT, lq1, lk1, lq2, lk2, subw, lam_init, tq, tk):
    B, _, S = qT.shape
    nk = S // tk
    assert tk % tq == 0 and S % tk == 0
    tile = tq
    kcol = COL_DAK // 128
    vec = lambda a: a.reshape(1, DA_DIM).astype(F32)
    grid_spec = pltpu.PrefetchScalarGridSpec(
        num_scalar_prefetch=1,
        grid=(B, DA_HEADS, S // tile),
        in_specs=[pl.BlockSpec((None, 128, tile), lambda b, h, i, s: (b, h, i)),
                  pl.BlockSpec((None, S, 128), lambda b, h, i, s: (b, 0, kcol + h)),
                  pl.BlockSpec((None, None, nk, 128, tk), lambda b, h, i, s: (b, h, 0, 0, 0)),
                  pl.BlockSpec((1, DA_DIM), lambda b, h, i, s: (0, 0)),
                  pl.BlockSpec((1, DA_DIM), lambda b, h, i, s: (0, 0)),
                  pl.BlockSpec((1, DA_DIM), lambda b, h, i, s: (0, 0)),
                  pl.BlockSpec((1, DA_DIM), lambda b, h, i, s: (0, 0)),
                  pl.BlockSpec((DA_VDIM, 1), lambda b, h, i, s: (0, 0))],
        out_specs=pl.BlockSpec((None, 128, tile), lambda b, h, i, s: (b, h, i)),
        scratch_shapes=[pltpu.VMEM((DA_VDIM, 2 * tq), F32), pltpu.VMEM((tk, 2 * tq), F32),
                        pltpu.VMEM((tk, 2 * tq), F32), pltpu.VMEM((tk, 2 * tq), F32),
                        pltpu.VMEM((tk, 2 * tq), BF16), pltpu.VMEM((tk, 2 * tq), BF16)],
    )
    return pl.pallas_call(
        functools.partial(_da_kernel, tq=tq, tk=tk, lam_init=lam_init),
        grid_spec=grid_spec,
        out_shape=jax.ShapeDtypeStruct((B, DA_HEADS * DA_VDIM, S), BF16),
        compiler_params=_cparams(("parallel", "parallel", "arbitrary")),
        name="diffattn",
    )(slopes, qT, slab3, vT, vec(lq1), vec(lk1), vec(lq2), vec(lk2), subw.reshape(DA_VDIM, 1).astype(F32))


def _nsa_kernel(slopes_ref, qT_ref, kc_ref, vcT_ref, ksa_ref, vsT_ref, kwa_ref, vwT_ref, wb_ref, gT_ref, mT_ref,
                oT_ref, acc_ref, res_ref, base_ref, qa_ref, sa_ref, sb_ref, pa_ref, pb_ref, *, t, tk, n_top):
    g = pl.program_id(1)
    qi = pl.program_id(2)
    q0 = qi * t
    w = NSA_HPG * t
    n_cp = kc_ref.shape[0]
    n_sel = mT_ref.shape[0]
    lane1 = lax.broadcasted_iota(I32, (1, w), 1)
    slope_row = jnp.zeros((1, w), F32)
    for hh in range(NSA_HPG):
        slope_row = jnp.where((lane1 >= hh * t) & (lane1 < (hh + 1) * t), slopes_ref[g * NSA_HPG + hh], slope_row)
    q_all = qT_ref[...]
    gates = jax.nn.sigmoid(gT_ref[...])
    key_off = lax.broadcasted_iota(I32, (tk, w), 0)
    t_loc = jnp.bitwise_and(lax.broadcasted_iota(I32, (tk, w), 1), t - 1)
    base_ref[...] = slope_row * key_off.astype(F32)

    c_idx = lax.broadcasted_iota(I32, (n_cp, w), 0)
    t_c = q0 + jnp.bitwise_and(lax.broadcasted_iota(I32, (n_cp, w), 1), t - 1)
    d_c = t_c - (c_idx * CMP_STRIDE + (CMP_BLOCK - 1))
    ok_c = d_c >= 0
    s = jnp.dot(kc_ref[...], q_all, preferred_element_type=F32) - slope_row * d_c.astype(F32)
    s = jnp.where(ok_c, s, NEG)
    e = jnp.exp(s - jnp.max(s, axis=0, keepdims=True))
    inv = 1.0 / jnp.sum(e, axis=0, keepdims=True)
    p = jnp.where(ok_c, e * inv, 0.0)
    res_ref[...] = gates[0:1, :] * jnp.dot(vcT_ref[...], p.astype(BF16), preferred_element_type=F32)
    p_sum = p[:, 0:t]
    for hh in range(1, NSA_HPG):
        p_sum = p_sum + p[:, hh * t:(hh + 1) * t]

    p_hi = p_sum.astype(BF16)
    p_lo = (p_sum - p_hi.astype(F32)).astype(BF16)
    mT = mT_ref[...]
    imp = jnp.dot(mT, p_hi, preferred_element_type=F32) + jnp.dot(mT, p_lo, preferred_element_type=F32)
    blk = lax.broadcasted_iota(I32, (n_sel, t), 0)
    t_s = q0 + lax.broadcasted_iota(I32, (n_sel, t), 1)
    cur = jnp.right_shift(t_s, int(math.log2(SEL_BLOCK)))
    forced = (blk == 0) | (blk == cur) | (blk == cur - 1)
    score = jnp.where(blk * SEL_BLOCK <= t_s, imp + jnp.where(forced, FORCE_BONUS, 0.0), NEG)
    blk_f = blk.astype(F32)
    sel = jnp.zeros((n_sel, t), F32)
    for _ in range(n_top):
        _, idx = _first_max(score, blk_f)
        hit = blk_f == idx
        sel = jnp.where(hit, 1.0, sel)
        score = jnp.where(hit, -jnp.inf, score)
    selb = jnp.where(sel > 0.5, 0.0, NEG)
    row64 = lax.broadcasted_iota(I32, (NSA_DIM, w), 0)
    qa_ref[0:NSA_DIM, :] = q_all
    qa_ref[NSA_DIM:2 * NSA_DIM, :] = jnp.where(row64 == 0, NEG, 0.0).astype(BF16)
    qa_ref[2 * NSA_DIM:2 * NSA_DIM + n_sel, :] = jnp.concatenate([selb] * NSA_HPG, axis=1).astype(BF16)
    if n_sel < SEL_COLS:
        qa_ref[2 * NSA_DIM + n_sel:, :] = jnp.zeros((SEL_COLS - n_sel, w), BF16)

    n_wt = (WINDOW + t) // t
    kw_t = kwa_ref[pl.ds(pl.multiple_of(q0, t), WINDOW + t), :]
    s = jnp.dot(kw_t, qa_ref[0:2 * NSA_DIM, :], preferred_element_type=F32) + wb_ref[...]
    p = jnp.exp(s - jnp.max(s, axis=0, keepdims=True))
    inv = 1.0 / jnp.sum(p, axis=0, keepdims=True)
    p = p.astype(BF16)
    o_w = jnp.dot(vwT_ref[qi], p[0:t, :], preferred_element_type=F32)
    for c in range(1, n_wt):
        o_w = o_w + jnp.dot(vwT_ref[qi + c], p[c * t:(c + 1) * t, :], preferred_element_type=F32)
    res_ref[...] = res_ref[...] + gates[2:3, :] * (o_w * inv)

    nk = vsT_ref.shape[0]

    def scores(kt, s_ref):
        k_t = ksa_ref[pl.ds(pl.multiple_of(kt * tk, tk), tk), :]
        s = jnp.dot(k_t, qa_ref[...], preferred_element_type=F32) + base_ref[...]
        s_ref[...] = s
        return jnp.max(s, axis=0, keepdims=True)

    def probs(s_ref, p_ref, mx, off, m_old, l_old):
        m_new = jnp.maximum(m_old, mx + off)
        p = jnp.exp(s_ref[...] - (m_new - off))
        alpha = jnp.exp(m_old - m_new)
        p_ref[...] = p.astype(BF16)
        return m_new, alpha * l_old + jnp.sum(p, axis=0, keepdims=True), alpha

    def accumulate(kt, p_ref, alpha):
        acc_ref[...] = alpha * acc_ref[...] + jnp.dot(vsT_ref[kt], p_ref[...], preferred_element_type=F32)

    n_full = q0 // tk
    mx_a = scores(0, sa_ref)

    kd = n_full * tk
    s = jnp.dot(ksa_ref[pl.ds(pl.multiple_of(kd, tk), tk), :], qa_ref[...], preferred_element_type=F32) + base_ref[...]
    s = jnp.where(key_off - t_loc <= q0 - kd, s, NEG)
    off_d = slope_row * kd.astype(F32)
    m = jnp.max(s, axis=0, keepdims=True) + off_d
    p = jnp.exp(s - (m - off_d))
    l = jnp.sum(p, axis=0, keepdims=True)
    acc_ref[...] = jnp.dot(vsT_ref[n_full], p.astype(BF16), preferred_element_type=F32)
    pb_ref[...] = jnp.zeros_like(pb_ref)

    def pair(j, carry):
        m, l, mx_a, alpha_b = carry
        ta = 2 * j
        tb = ta + 1
        off_a = slope_row * (ta * tk).astype(F32)
        off_b = jnp.where(tb < n_full, slope_row * (tb * tk).astype(F32), NEG)
        accumulate(jnp.maximum(ta - 1, 0), pb_ref, alpha_b)
        m, l, alpha_a = probs(sa_ref, pa_ref, mx_a, off_a, m, l)
        mx_b = scores(jnp.minimum(tb, nk - 1), sb_ref)
        accumulate(ta, pa_ref, alpha_a)
        m, l, alpha_b = probs(sb_ref, pb_ref, mx_b, off_b, m, l)
        mx_a = scores(jnp.minimum(ta + 2, nk - 1), sa_ref)
        return m, l, mx_a, alpha_b

    n_pairs = (n_full + 1) // 2
    m, l, _, alpha_b = lax.fori_loop(0, n_pairs, pair, (m, l, mx_a, jnp.ones((1, w), F32)))
    accumulate(jnp.minimum(jnp.maximum(2 * n_pairs - 1, 0), nk - 1), pb_ref, alpha_b)
    oT_ref[...] = (res_ref[...] + gates[1:2, :] * (acc_ref[...] * (1.0 / l))).astype(BF16)


def _nsa_attention(slopes, qT, kc, vcT, ksa, vsT, kwa, vwT, wb, gT, mT, t, tk):
    B, G, nq, _, w = qT.shape
    S = nq * t
    n_cp = kc.shape[2]
    n_sel = S // SEL_BLOCK
    n_top = min(SEL_TOPK, n_sel)
    assert tk % t == 0 and S % tk == 0 and t & (t - 1) == 0 and WINDOW % t == 0 and n_sel <= SEL_COLS
    fixed = lambda b, g, i, s: (b, g, 0, 0)
    fixed5 = lambda b, g, i, s: (b, g, 0, 0, 0)
    tile5 = lambda b, g, i, s: (b, g, i, 0, 0)
    grid_spec = pltpu.PrefetchScalarGridSpec(
        num_scalar_prefetch=1,
        grid=(B, G, nq),
        in_specs=[pl.BlockSpec((None, None, None, NSA_DIM, w), tile5),
                  pl.BlockSpec((None, None, n_cp, NSA_DIM), fixed),
                  pl.BlockSpec((None, None, NSA_DIM, n_cp), fixed),
                  pl.BlockSpec((None, None, S, ksa.shape[-1]), fixed),
                  pl.BlockSpec((None, None, S // tk, NSA_DIM, tk), fixed5),
                  pl.BlockSpec((None, None, S + WINDOW, 2 * NSA_DIM), fixed),
                  pl.BlockSpec((None, None, (S + WINDOW) // t, NSA_DIM, t), fixed5),
                  pl.BlockSpec((None, WINDOW + t, w), lambda b, g, i, s: (g, 0, 0)),
                  pl.BlockSpec((None, None, None, 3, w), tile5),
                  pl.BlockSpec((n_sel, n_cp), lambda b, g, i, s: (0, 0))],
        out_specs=pl.BlockSpec((None, None, None, NSA_DIM, w), tile5),
        scratch_shapes=[pltpu.VMEM((NSA_DIM, w), F32),
                        pltpu.VMEM((NSA_DIM, w), F32),
                        pltpu.VMEM((tk, w), F32),
                        pltpu.VMEM((2 * NSA_DIM + SEL_COLS, w), BF16),
                        pltpu.VMEM((tk, w), F32), pltpu.VMEM((tk, w), F32),
                        pltpu.VMEM((tk, w), BF16), pltpu.VMEM((tk, w), BF16)],
    )
    return pl.pallas_call(
        functools.partial(_nsa_kernel, t=t, tk=tk, n_top=n_top),
        grid_spec=grid_spec,
        out_shape=jax.ShapeDtypeStruct((B, G, nq, NSA_DIM, w), BF16),
        compiler_params=_cparams(("parallel", "parallel", "arbitrary")),
        name="nsa",
    )(slopes, qT, kc, vcT, ksa, vsT, kwa, vwT, wb, gT, mT)


def _merge_kernel(oa_ref, ob_ref, ga_ref, gb_ref, x_ref, g1_ref, nw_ref, sh_ref, sc_ref,
                  wa_ref, wb_ref, wo_ref, wr_ref, x1_ref, h2_ref, lg_ref):
    ya = jnp.dot(oa_ref[...], wa_ref[...], preferred_element_type=F32)
    yb = jnp.dot(ob_ref[...], wb_ref[...], preferred_element_type=F32)
    merged = (jax.nn.sigmoid(ga_ref[...].astype(F32)) * ya
              + jax.nn.sigmoid(gb_ref[...].astype(F32)) * yb).astype(BF16)
    mix = jnp.dot(merged, wo_ref[...], preferred_element_type=F32)
    x1 = x_ref[...] + g1_ref[...] * mix
    x1_ref[...] = x1
    h2 = _modulate(x1, nw_ref[...], sh_ref[...], sc_ref[...])
    h2_ref[...] = h2
    lg_ref[...] = jnp.dot(h2, wr_ref[...], preferred_element_type=F32, precision=HIGHEST)


def _merge(oa, ob, slab, x2, g1, nw, sh, sc, wa, wb, wo, wr, S):
    N, D = x2.shape
    tm = 256
    per_b = S // tm
    mcol = COL_MERGE // D
    row = lambda i: (i, 0)
    full = lambda i: (0, 0)
    perb = lambda i: (i // per_b, 0, 0)
    n_lg = wr.shape[1]
    return pl.pallas_call(
        _merge_kernel,
        grid=(N // tm,),
        in_specs=[pl.BlockSpec((tm, oa.shape[1]), row),
                  pl.BlockSpec((tm, ob.shape[1]), row),
                  pl.BlockSpec((tm, D), lambda i: (i, mcol)),
                  pl.BlockSpec((tm, D), lambda i: (i, mcol + 1)),
                  pl.BlockSpec((tm, D), row),
                  pl.BlockSpec((None, 1, D), perb),
                  pl.BlockSpec((1, D), full),
                  pl.BlockSpec((None, 1, D), perb),
                  pl.BlockSpec((None, 1, D), perb),
                  pl.BlockSpec(wa.shape, full),
                  pl.BlockSpec(wb.shape, full),
                  pl.BlockSpec(wo.shape, full),
                  pl.BlockSpec(wr.shape, full)],
        out_specs=[pl.BlockSpec((tm, D), row), pl.BlockSpec((tm, D), row), pl.BlockSpec((tm, n_lg), row)],
        out_shape=[jax.ShapeDtypeStruct((N, D), F32), jax.ShapeDtypeStruct((N, D), F32),
                   jax.ShapeDtypeStruct((N, n_lg), F32)],
        compiler_params=_cparams(("parallel",)),
        name="merge",
    )(oa, ob, slab, slab, x2, g1.reshape(-1, 1, D), nw.reshape(1, D), sh.reshape(-1, 1, D),
      sc.reshape(-1, 1, D), wa, wb, wo, wr)


def _first_max(v, idx):
    mx = jnp.max(v, axis=0, keepdims=True)
    first = jnp.min(jnp.where(v == mx, idx, float(v.shape[0])), axis=0, keepdims=True)
    return mx, first


def _route_kernel(lg_ref, b_ref, e_ref, w_ref):
    lg = lg_ref[...]
    tn = lg.shape[1]
    per_g = N_EXPERTS // N_EXPERT_GROUPS
    scores = jax.nn.sigmoid(lg)
    biased = scores + b_ref[...]
    gi = lax.broadcasted_iota(I32, (N_EXPERT_GROUPS, tn), 0).astype(F32)
    gscore = jnp.zeros((N_EXPERT_GROUPS, tn), F32)
    for g in range(N_EXPERT_GROUPS):
        grp = biased[g * per_g:(g + 1) * per_g, :]
        m1, f1 = _first_max(grp, gi)
        m2 = jnp.max(jnp.where(gi == f1, -jnp.inf, grp), axis=0, keepdims=True)
        gscore = jnp.where(gi == g, m1 + m2, gscore)
    gsel = jnp.zeros((N_EXPERT_GROUPS, tn), F32)
    for _ in range(TOPK_EXPERT_GROUPS):
        _, f = _first_max(gscore, gi)
        hit = gi == f
        gsel = jnp.where(hit, 1.0, gsel)
        gscore = jnp.where(hit, -jnp.inf, gscore)
    ei = lax.broadcasted_iota(I32, (N_EXPERTS, tn), 0).astype(F32)
    emask = jnp.zeros((N_EXPERTS, tn), F32)
    for g in range(N_EXPERT_GROUPS):
        in_g = (ei >= g * per_g) & (ei < (g + 1) * per_g)
        emask = jnp.where(in_g, gsel[g:g + 1, :], emask)
    masked = jnp.where(emask > 0.5, biased, NEG)
    e_out = jnp.zeros((MOE_TOPK, tn), F32)
    w_out = jnp.zeros((MOE_TOPK, tn), F32)
    for r in range(MOE_TOPK):
        _, f = _first_max(masked, ei)
        hit = ei == f
        wv = jnp.sum(jnp.where(hit, scores, 0.0), axis=0, keepdims=True)
        e_out = jnp.where(gi == r, f, e_out)
        w_out = jnp.where(gi == r, wv, w_out)
        masked = jnp.where(hit, -jnp.inf, masked)
    e_ref[...] = e_out.astype(I32)
    w_ref[...] = w_out / jnp.sum(w_out, axis=0, keepdims=True) * ROUTED_SCALE


def _route(lgT, rb):
    E, N = lgT.shape
    tn = 512
    return pl.pallas_call(
        _route_kernel,
        grid=(N // tn,),
        in_specs=[pl.BlockSpec((E, tn), lambda i: (0, i)), pl.BlockSpec((E, 1), lambda i: (0, 0))],
        out_specs=[pl.BlockSpec((MOE_TOPK, tn), lambda i: (0, i)), pl.BlockSpec((MOE_TOPK, tn), lambda i: (0, i))],
        out_shape=[jax.ShapeDtypeStruct((MOE_TOPK, N), I32), jax.ShapeDtypeStruct((MOE_TOPK, N), F32)],
        compiler_params=_cparams(("parallel",)),
        name="route",
    )(lgT, rb.reshape(E, 1).astype(F32))


def _slots_kernel(e_ref, tri_ref, lt_ref, dest_ref, cnt_ref, carry_ref, *, rb):
    phase = pl.program_id(0)
    i = pl.program_id(1)
    e = e_ref[...]
    tn = e.shape[1]
    ei = lax.broadcasted_iota(I32, (N_EXPERTS, tn), 0)

    @pl.when((phase == 0) & (i == 0))
    def _():
        carry_ref[...] = jnp.zeros_like(carry_ref)

    @pl.when(phase == 0)
    def _():
        tot = jnp.zeros((N_EXPERTS, 1), F32)
        for k in range(MOE_TOPK):
            oh = jnp.where(ei == e[k:k + 1, :], 1.0, 0.0)
            tot = tot + jnp.sum(oh, axis=1, keepdims=True)
        carry_ref[...] = carry_ref[...] + tot
        dest_ref[...] = jnp.zeros_like(dest_ref)

    @pl.when((phase == 1) & (i == 0))
    def _():
        cnt = carry_ref[...]
        cnt_ref[...] = cnt
        padded = jnp.broadcast_to(jnp.ceil(cnt / rb) * rb, (N_EXPERTS, 128))
        first = jnp.dot(lt_ref[...], padded, preferred_element_type=F32, precision=HIGHEST)
        carry_ref[...] = first[:, :1]

    @pl.when(phase == 1)
    def _():
        carry = carry_ref[...]
        ki = lax.broadcasted_iota(I32, (MOE_TOPK, tn), 0)
        dest = jnp.zeros((MOE_TOPK, tn), F32)
        for k in range(MOE_TOPK):
            hit = ei == e[k:k + 1, :]
            oh = jnp.where(hit, 1.0, 0.0)
            before = jnp.dot(oh.astype(BF16), tri_ref[...], preferred_element_type=F32)
            row = jnp.sum(jnp.where(hit, before + carry, 0.0), axis=0, keepdims=True)
            dest = jnp.where(ki == k, row, dest)
            carry = carry + jnp.sum(oh, axis=1, keepdims=True)
        carry_ref[...] = carry
        dest_ref[...] = dest.astype(I32)


def _slots(top_eT, rb):
    K, N = top_eT.shape
    tn = 512
    idx = np.arange(tn)
    tri = jnp.asarray((idx[:, None] < idx[None, :]).astype(np.float32), BF16)
    e_idx = np.arange(N_EXPERTS)
    lt = jnp.asarray((e_idx[None, :] < e_idx[:, None]).astype(np.float32), F32)
    return pl.pallas_call(
        functools.partial(_slots_kernel, rb=rb),
        grid=(2, N // tn),
        in_specs=[pl.BlockSpec((K, tn), lambda p, i: (0, i)),
                  pl.BlockSpec((tn, tn), lambda p, i: (0, 0)),
                  pl.BlockSpec((N_EXPERTS, N_EXPERTS), lambda p, i: (0, 0))],
        out_specs=[pl.BlockSpec((K, tn), lambda p, i: (0, i * p)),
                   pl.BlockSpec((N_EXPERTS, 1), lambda p, i: (0, 0))],
        out_shape=[jax.ShapeDtypeStruct((K, N), I32), jax.ShapeDtypeStruct((N_EXPERTS, 1), F32)],
        scratch_shapes=[pltpu.VMEM((N_EXPERTS, 1), F32)],
        compiler_params=_cparams(("arbitrary", "arbitrary")),
        name="slots",
    )(top_eT, tri, lt)


def _dispatch_kernel(dest_ref, h_ref, xs_in_ref, xs_ref, sem, *, tt):
    del xs_in_ref

    def body(n, c):
        src = h_ref.at[pl.ds(n, 1), :]
        for k in range(MOE_TOPK):
            d = dest_ref[0, 0, k * tt + n]
            pltpu.make_async_copy(src, xs_ref.at[pl.ds(d, 1), :], sem).start()
        return c

    lax.fori_loop(0, tt, body, 0)
    for k in range(MOE_TOPK):
        pltpu.make_async_copy(h_ref, xs_ref.at[pl.ds(0, tt), :], sem).wait()


def _dispatch(dest_tiles, h2, n_rows, tt):
    N, D = h2.shape
    xs0 = jnp.zeros((n_rows, D), h2.dtype)
    return pl.pallas_call(
        functools.partial(_dispatch_kernel, tt=tt),
        grid=(N // tt,),
        in_specs=[pl.BlockSpec((1, 1, MOE_TOPK * tt), lambda i: (i, 0, 0), memory_space=pltpu.SMEM),
                  pl.BlockSpec((tt, D), lambda i: (i, 0)),
                  pl.BlockSpec(memory_space=pl.ANY)],
        out_specs=pl.BlockSpec(memory_space=pl.ANY),
        out_shape=jax.ShapeDtypeStruct((n_rows, D), h2.dtype),
        scratch_shapes=[pltpu.SemaphoreType.DMA(())],
        input_output_aliases={2: 0},
        compiler_params=_cparams(("arbitrary",)),
        name="dispatch",
    )(dest_tiles, h2, xs0)


def _experts_kernel(be_ref, x_ref, wgu_ref, wd_ref, y_ref):
    del be_ref
    x = x_ref[...].astype(BF16)
    gu = jnp.dot(x, wgu_ref[...], preferred_element_type=F32)
    hdim = gu.shape[1] // 2
    g = gu[:, :hdim]
    act = (g * jax.nn.sigmoid(g) * gu[:, hdim:]).astype(BF16)
    y_ref[...] = jnp.dot(act, wd_ref[...], preferred_element_type=F32)


def _experts(blk_e, xs, wgu, wd, rb):
    n_rows, D = xs.shape
    grid_spec = pltpu.PrefetchScalarGridSpec(
        num_scalar_prefetch=1,
        grid=(n_rows // rb,),
        in_specs=[pl.BlockSpec((rb, D), lambda i, be: (i, 0)),
                  pl.BlockSpec((None,) + wgu.shape[1:], lambda i, be: (be[i], 0, 0)),
                  pl.BlockSpec((None,) + wd.shape[1:], lambda i, be: (be[i], 0, 0))],
        out_specs=pl.BlockSpec((rb, D), lambda i, be: (i, 0)),
    )
    return pl.pallas_call(
        _experts_kernel,
        grid_spec=grid_spec,
        out_shape=jax.ShapeDtypeStruct((n_rows, D), F32),
        compiler_params=_cparams(("arbitrary",)),
        name="experts",
    )(blk_e, xs, wgu, wd)


def _combine_kernel(dest_ref, ys_ref, w_ref, h_ref, x1_ref, g2_ref, fw_ref, sgu_ref, sd_ref, o_ref, buf_ref, sem, *, tt):
    def body(n, c):
        for k in range(MOE_TOPK):
            d = dest_ref[0, 0, k * tt + n]
            pltpu.make_async_copy(ys_ref.at[pl.ds(d, 1), :], buf_ref.at[k, pl.ds(n, 1), :], sem).start()
        return c

    lax.fori_loop(0, tt, body, 0)

    h = h_ref[...].astype(BF16)
    gu = jnp.dot(h, sgu_ref[...], preferred_element_type=F32)
    hdim = gu.shape[1] // 2
    g = gu[:, :hdim]
    ffn = jnp.dot((g * jax.nn.sigmoid(g) * gu[:, hdim:]).astype(BF16), sd_ref[...], preferred_element_type=F32)

    for k in range(MOE_TOPK):
        pltpu.make_async_copy(ys_ref.at[pl.ds(0, tt), :], buf_ref.at[k], sem).wait()
    w = w_ref[...]
    for k in range(MOE_TOPK):
        ffn = ffn + w[:, k:k + 1] * buf_ref[k]
    x2 = x1_ref[...] + g2_ref[...] * ffn
    r = lax.rsqrt(jnp.mean(x2 * x2, axis=-1, keepdims=True) + NORM_EPS)
    o_ref[...] = x2 * r * fw_ref[...]


def _combine(dest_tiles, ys, top_w, h2, x1, g2, fw, sgu, sd, S, tt):
    N, D = h2.shape
    per_b = S // tt
    row = lambda i: (i, 0)
    full = lambda i: (0, 0)
    return pl.pallas_call(
        functools.partial(_combine_kernel, tt=tt),
        grid=(N // tt,),
        in_specs=[pl.BlockSpec((1, 1, MOE_TOPK * tt), lambda i: (i, 0, 0), memory_space=pltpu.SMEM),
                  pl.BlockSpec(memory_space=pl.ANY),
                  pl.BlockSpec((tt, MOE_TOPK), row),
                  pl.BlockSpec((tt, D), row),
                  pl.BlockSpec((tt, D), row),
                  pl.BlockSpec((None, 1, D), lambda i: (i // per_b, 0, 0)),
                  pl.BlockSpec((1, D), full),
                  pl.BlockSpec(sgu.shape, full),
                  pl.BlockSpec(sd.shape, full)],
        out_specs=pl.BlockSpec((tt, D), row),
        out_shape=jax.ShapeDtypeStruct((N, D), F32),
        scratch_shapes=[pltpu.VMEM((MOE_TOPK, tt, D), F32), pltpu.SemaphoreType.DMA(())],
        compiler_params=_cparams(("arbitrary",)),
        name="combine",
    )(dest_tiles, ys, top_w, h2, x1, g2.reshape(-1, 1, D), fw.reshape(1, D), sgu, sd)


def _cmp_to_sel_T(n_cp, n_sel):
    c0 = np.arange(n_cp)[None, :] * CMP_STRIDE
    s0 = np.arange(n_sel)[:, None] * SEL_BLOCK
    ov = np.minimum(c0 + CMP_BLOCK, s0 + SEL_BLOCK) - np.maximum(c0, s0)
    m = np.clip(ov, 0, None).astype(np.float32) / CMP_BLOCK
    m[:, n_cp - 1] = 0.0
    return jnp.asarray(m, BF16)


def _slab_weights(w_in):
    D = w_in.shape[0]
    sizes = [1024, 1024, 1024, 512, 128, 128, 128, 128, 128, 128, 24, 2048]
    offs = np.concatenate([[0], np.cumsum(sizes)])
    part = lambda i: w_in[:, offs[i]:offs[i + 1]]
    pieces = [part(0), part(1), part(2), part(11), part(3)] + [part(i) for i in range(4, 10)] + [part(10)]
    w = jnp.concatenate(pieces, axis=1)
    return jnp.pad(w, ((0, 0), (0, SLAB_COLS - w.shape[1]))).astype(BF16)


def _layer(x, c, ada_w, ada_b, norm1_w, w_in, da_lq1, da_lk1, da_lq2, da_lk2, da_subln_w,
           cmp_k_pe, cmp_k_w1, cmp_k_w2, cmp_v_pe, cmp_v_w1, cmp_v_w2, w_da_out, w_nsa_out, w_o,
           norm2_w, router_w, router_b, exp_w_gate, exp_w_up, exp_w_down,
           sh_w_gate, sh_w_up, sh_w_down, final_norm_w, lam_init):
    B, S, D = x.shape
    N = B * S
    G, HPG = NSA_GROUPS, NSA_HPG
    x2 = x.reshape(N, D)

    mod = _ada(c, ada_w, ada_b)
    sh1, sc1, g1, sh2, sc2, g2 = jnp.split(mod, 6, axis=-1)

    slab = _inproj(x2, norm1_w, sh1, sc1, _slab_weights(w_in), S)
    slab3 = slab.reshape(B, S, SLAB_COLS)

    i_all = np.arange(1, DA_HEADS + NSA_HEADS + 1, dtype=np.float32)
    slopes = (2.0 ** (-8.0 * i_all / (DA_HEADS + NSA_HEADS))).astype(np.float32)
    slopes_a = jnp.asarray(slopes[0::2])
    slopes_b = jnp.asarray(slopes[1::2])

    ta_q, ta_k = 256, min(512, S)
    qT = (slab3[:, :, COL_DAQ:COL_DAQ + 1024] * jnp.asarray(DA_DIM ** -0.5, BF16)).transpose(0, 2, 1)
    vT = slab3[:, :, COL_DAV:COL_DAV + 1024].reshape(B, S // ta_k, ta_k, DA_HEADS, DA_VDIM).transpose(0, 3, 1, 4, 2)
    oaT = _diff_attention(slopes_a, qT, slab3, vT, da_lq1, da_lk1, da_lq2, da_lk2, da_subln_w, lam_init,
                          ta_q, ta_k)
    oa = oaT.transpose(0, 2, 1).reshape(N, DA_HEADS * DA_VDIM)

    tb, tb_k = 128, min(256, S)
    n_cp = S // CMP_STRIDE
    n_sel = S // SEL_BLOCK

    def kv_groups(j):
        c0 = COL_KV6 + 128 * j
        return slab3[:, :, c0:c0 + 128].reshape(B, S, G, NSA_DIM).transpose(0, 2, 1, 3)

    def kv_tiles_T(a, rows):
        return a.reshape(B, G, S // rows, rows, NSA_DIM).transpose(0, 1, 2, 4, 3)

    ck_rows = kv_groups(0).reshape(B * G, n_cp, CMP_STRIDE * NSA_DIM)
    cv_rows = kv_groups(1).reshape(B * G, n_cp, CMP_STRIDE * NSA_DIM)
    kc = _compress(ck_rows, cmp_k_pe, cmp_k_w1, cmp_k_w2).reshape(B, G, n_cp, NSA_DIM).astype(BF16)
    vc = _compress(cv_rows, cmp_v_pe, cmp_v_w1, cmp_v_w2).reshape(B, G, n_cp, NSA_DIM).astype(BF16)
    nq = (slab3[:, :, COL_NSAQ:COL_NSAQ + 512] * jnp.asarray(NSA_DIM ** -0.5, BF16))
    nqT = nq.reshape(B, S // tb, tb, G, HPG, NSA_DIM).transpose(0, 3, 1, 5, 4, 2).reshape(B, G, S // tb, NSA_DIM, HPG * tb)
    gT = (slab3[:, :, COL_NSAG:COL_NSAG + 3 * NSA_HEADS].astype(F32).reshape(B, S // tb, tb, G, HPG, 3)
          .transpose(0, 3, 1, 5, 4, 2).reshape(B, G, S // tb, 3, HPG * tb))
    onehot = jnp.asarray((np.arange(S)[:, None] // SEL_BLOCK == np.arange(SEL_COLS)[None, :]).astype(np.float32), BF16)
    zeros64 = jnp.zeros((B, G, S, NSA_DIM), BF16)
    ksa = jnp.concatenate([kv_groups(2), zeros64, jnp.broadcast_to(onehot, (B, G, S, SEL_COLS))], axis=-1)
    pad_rows = jnp.zeros((B, G, WINDOW, 2 * NSA_DIM), BF16).at[..., NSA_DIM].set(1.0)
    kwa = jnp.concatenate([pad_rows, jnp.concatenate([kv_groups(4), zeros64], axis=-1)], axis=2)
    vw_pad = jnp.pad(kv_groups(5), ((0, 0), (0, 0), (WINDOW, 0), (0, 0)))
    vwT = vw_pad.reshape(B, G, (S + WINDOW) // tb, tb, NSA_DIM).transpose(0, 1, 2, 4, 3)
    r_w = np.arange(WINDOW + tb)[:, None]
    lane_w = np.arange(HPG * tb)[None, :]
    d_w = (WINDOW + lane_w % tb - r_w).astype(np.float32)
    slope_w = slopes[1::2].reshape(G, 1, HPG)[:, :, lane_w[0] // tb]
    wb = jnp.asarray(np.where((d_w >= 0) & (d_w < WINDOW), -slope_w * d_w[None], np.float32(NEG)).astype(np.float32))
    obT = _nsa_attention(slopes_b, nqT, kc, vc.transpose(0, 1, 3, 2), ksa, kv_tiles_T(kv_groups(3), tb_k),
                         kwa, vwT, wb, gT, _cmp_to_sel_T(n_cp, n_sel), tb, tb_k)
    ob = (obT.reshape(B, G, S // tb, NSA_DIM, HPG, tb).transpose(0, 2, 5, 1, 4, 3)
          .reshape(N, NSA_HEADS * NSA_DIM))

    x1, h2, logits = _merge(oa, ob, slab, x2, g1, norm2_w, sh2, sc2, w_da_out.astype(BF16),
                            w_nsa_out.astype(BF16), w_o.astype(BF16), router_w.astype(F32), S)

    rb = 256
    tt = 128
    top_eT, top_wT = _route(logits.T, router_b)
    destT, counts = _slots(top_eT, rb)
    n_rows = ((N * MOE_TOPK + N_EXPERTS * (rb - 1) + rb - 1) // rb) * rb
    padded = (jnp.ceil(counts[:, 0] / rb) * rb).astype(I32)
    pend = jnp.cumsum(padded)
    blk_start = jnp.arange(n_rows // rb, dtype=I32) * rb
    blk_e = jnp.minimum(jnp.sum((pend[None, :] <= blk_start[:, None]).astype(I32), axis=1), N_EXPERTS - 1)
    dest_tiles = destT.reshape(MOE_TOPK, N // tt, tt).transpose(1, 0, 2).reshape(N // tt, 1, MOE_TOPK * tt)
    xs = _dispatch(dest_tiles, h2, n_rows, tt)
    wgu = jnp.concatenate([exp_w_gate, exp_w_up], axis=-1).astype(BF16)
    ys = _experts(blk_e, xs, wgu, exp_w_down.astype(BF16), rb)
    sgu = jnp.concatenate([sh_w_gate, sh_w_up], axis=-1).astype(BF16)
    out = _combine(dest_tiles, ys, top_wT.T, h2, x1, g2, final_norm_w, sgu, sh_w_down.astype(BF16), S, tt)
    return out.reshape(B, S, D)


def kernel(x, c, ada_w, ada_b, norm1_w, w_in, da_lq1, da_lk1, da_lq2, da_lk2, da_subln_w, cmp_k_pe, cmp_k_w1, cmp_k_w2, cmp_v_pe, cmp_v_w1, cmp_v_w2, w_da_out, w_nsa_out, w_o, norm2_w, router_w, router_b, exp_w_gate, exp_w_up, exp_w_down, sh_w_gate, sh_w_up, sh_w_down, final_norm_w):
    depth = ada_w.shape[0]
    assert depth == 1, "one decoder layer"
    lam_init = 0.8 - 0.6 * math.exp(-0.3 * 0)
    return _layer(x, c, ada_w[0], ada_b[0], norm1_w[0], w_in[0], da_lq1[0], da_lk1[0], da_lq2[0], da_lk2[0],
                  da_subln_w[0], cmp_k_pe[0], cmp_k_w1[0], cmp_k_w2[0], cmp_v_pe[0], cmp_v_w1[0], cmp_v_w2[0],
                  w_da_out[0], w_nsa_out[0], w_o[0], norm2_w[0], router_w[0], router_b[0],
                  exp_w_gate[0], exp_w_up[0], exp_w_down[0], sh_w_gate[0], sh_w_up[0], sh_w_down[0],
                  final_norm_w, lam_init)
```

```python
import functools
import math

import numpy as np
import jax
import jax.numpy as jnp
from jax import lax
from jax.experimental import pallas as pl
from jax.experimental.pallas import tpu as pltpu

F32 = jnp.float32
BF16 = jnp.bfloat16
I32 = jnp.int32
U32 = jnp.uint32
HIGHEST = lax.Precision.HIGHEST

NORM_EPS = 1e-6
NEG = -1e30

DA_HEADS = 8
DA_DIM = 64
DA_VDIM = 128
DA_SUBLN_EPS = 1e-5

NSA_HEADS = 8
NSA_GROUPS = 2
NSA_HPG = 4
NSA_DIM = 64
CMP_BLOCK = 32
CMP_STRIDE = 16
SEL_BLOCK = 64
SEL_TOPK = 16
WINDOW = 512
FORCE_BONUS = 1e4
SEL_COLS = 128

N_EXPERTS = 64
MOE_TOPK = 8
N_EXPERT_GROUPS = 8
TOPK_EXPERT_GROUPS = 4
ROUTED_SCALE = 2.5

VMEM_LIMIT_V7X = 56 * 1024 * 1024

COL_DAQ, COL_DAK, COL_DAV = 0, 1024, 2048
COL_MERGE = 3072
COL_NSAQ = 5120
COL_KV6 = 5632
COL_NSAG = 6400
SLAB_COLS = 6528


POS_SPLIT = 16
POS_ROWS = 16
ONES_ROWS = 16


def _pos_features(rows):
    r = np.arange(rows)
    f = np.zeros((rows, 128), np.float32)
    f[:, 0:3] = (r // POS_SPLIT)[:, None]
    f[:, 3:6] = (r % POS_SPLIT)[:, None]
    return jnp.asarray(f, BF16)


def _slope_rows(slopes):
    s = np.asarray(slopes, np.float32)
    bf = lambda x: x.astype(BF16).astype(np.float32)
    p1 = bf(s)
    p2 = bf(s - p1)
    p3 = bf(s - p1 - p2)
    out = np.zeros((s.shape[0], POS_ROWS, 128), np.float32)
    for i, piece in enumerate((p1, p2, p3)):
        out[:, i, :] = POS_SPLIT * piece[:, None]
        out[:, 3 + i, :] = piece[:, None]
    return jnp.asarray(out)


def _with_ones_row(vT):
    shape = vT.shape[:-2] + (ONES_ROWS, vT.shape[-1])
    extra = jnp.zeros(shape, vT.dtype).at[..., 0, :].set(1.0)
    return jnp.concatenate([vT, extra], axis=-2)


def _cparams(sem, vmem=VMEM_LIMIT_V7X):
    return pltpu.CompilerParams(dimension_semantics=sem, vmem_limit_bytes=vmem)


def _ada_kernel(c_ref, w_ref, b_ref, o_ref):
    c = c_ref[...]
    ca = c * jax.nn.sigmoid(c)
    o_ref[...] = jnp.dot(ca, w_ref[...], preferred_element_type=F32, precision=HIGHEST) + b_ref[...]


def _ada(c, w, b):
    B, D = c.shape
    n_out = w.shape[1]
    rows = 8
    cp = jnp.zeros((rows, D), F32).at[:B].set(c)
    tn = 1024
    out = pl.pallas_call(
        _ada_kernel,
        grid=(n_out // tn,),
        in_specs=[pl.BlockSpec((rows, D), lambda j: (0, 0)),
                  pl.BlockSpec((D, tn), lambda j: (0, j)),
                  pl.BlockSpec((1, tn), lambda j: (0, j))],
        out_specs=pl.BlockSpec((rows, tn), lambda j: (0, j)),
        out_shape=jax.ShapeDtypeStruct((rows, n_out), F32),
        compiler_params=_cparams(("arbitrary",)),
        name="ada",
    )(cp, w, b.reshape(1, n_out))
    return out[:B]


def _modulate(x, nw, sh, sc):
    r = lax.rsqrt(jnp.mean(x * x, axis=-1, keepdims=True) + NORM_EPS)
    return (x * r) * nw * (1.0 + sc) + sh


def _inproj_kernel(x_ref, nw_ref, sh_ref, sc_ref, w_ref, o_ref, h_ref):
    @pl.when(pl.program_id(1) == 0)
    def _():
        h_ref[...] = _modulate(x_ref[...], nw_ref[...], sh_ref[...], sc_ref[...]).astype(BF16)

    o_ref[...] = jnp.dot(h_ref[...], w_ref[...], preferred_element_type=F32).astype(BF16)


def _inproj(x2, nw, sh, sc, w_slab, S):
    N, D = x2.shape
    tm = 512
    tn = SLAB_COLS // 3
    per_b = S // tm
    return pl.pallas_call(
        _inproj_kernel,
        grid=(N // tm, SLAB_COLS // tn),
        in_specs=[pl.BlockSpec((tm, D), lambda i, j: (i, 0)),
                  pl.BlockSpec((1, D), lambda i, j: (0, 0)),
                  pl.BlockSpec((None, 1, D), lambda i, j: (i // per_b, 0, 0)),
                  pl.BlockSpec((None, 1, D), lambda i, j: (i // per_b, 0, 0)),
                  pl.BlockSpec((D, tn), lambda i, j: (0, j))],
        out_specs=pl.BlockSpec((tm, tn), lambda i, j: (i, j)),
        out_shape=jax.ShapeDtypeStruct((N, SLAB_COLS), BF16),
        scratch_shapes=[pltpu.VMEM((tm, D), BF16)],
        compiler_params=_cparams(("parallel", "arbitrary")),
        name="inproj",
    )(x2, nw.reshape(1, D), sh.reshape(-1, 1, D), sc.reshape(-1, 1, D), w_slab)


def _cmp_kernel(r_ref, pe_ref, w1a_ref, w1b_ref, w2_ref, o_ref):
    r = r_ref[...]
    n_rows = r.shape[0]
    a = jnp.dot(r, w1a_ref[...], preferred_element_type=F32)
    b = jnp.dot(r, w1b_ref[...], preferred_element_type=F32)
    pe = pe_ref[...]
    half = pe.shape[1] // 2
    bias = (jnp.dot(pe[:, :half], w1a_ref[...].astype(F32), preferred_element_type=F32, precision=HIGHEST)
            + jnp.dot(pe[:, half:], w1b_ref[...].astype(F32), preferred_element_type=F32, precision=HIGHEST))
    hid = a + pltpu.roll(b, n_rows - 1, 0) + bias
    act = 0.5 * hid * (1.0 + jnp.tanh(0.7978845608028654 * (hid + 0.044715 * hid * hid * hid)))
    o_ref[...] = jnp.dot(act, w2_ref[...], preferred_element_type=F32, precision=HIGHEST)


def _compress(rows, pe, w1, w2):
    BG, n_rows, width = rows.shape
    hidden = w1.shape[1]
    w1a = w1[:width].astype(BF16)
    w1b = w1[width:].astype(BF16)
    return pl.pallas_call(
        _cmp_kernel,
        grid=(BG,),
        in_specs=[pl.BlockSpec((None, n_rows, width), lambda i: (i, 0, 0)),
                  pl.BlockSpec((1, 2 * width), lambda i: (0, 0)),
                  pl.BlockSpec((width, hidden), lambda i: (0, 0)),
                  pl.BlockSpec((width, hidden), lambda i: (0, 0)),
                  pl.BlockSpec((hidden, NSA_DIM), lambda i: (0, 0))],
        out_specs=pl.BlockSpec((None, n_rows, NSA_DIM), lambda i: (i, 0, 0)),
        out_shape=jax.ShapeDtypeStruct((BG, n_rows, NSA_DIM), F32),
        compiler_params=_cparams(("parallel",)),
        name="cmp",
    )(rows, pe.reshape(1, -1).astype(F32), w1a, w1b, w2.astype(F32))


def _da_kernel(slopes_ref, qT_ref, k_ref, vT_ref, pos_ref, srow_ref, lq1_ref, lk1_ref, lq2_ref, lk2_ref, subw_ref,
               oT_ref, acc_ref, qa_ref, sa_ref, sb_ref, pa_ref, pb_ref, *, tq, tk, lam_init):
    h = pl.program_id(1)
    qi = pl.program_id(2)
    slope = slopes_ref[h]
    q0 = qi * tq
    w = 2 * tq
    nk = k_ref.shape[0] // tk
    qT = qT_ref[...]
    row = lax.broadcasted_iota(I32, qT.shape, 0)
    zero = jnp.zeros_like(qT)
    qa_ref[0:2 * DA_DIM, :] = jnp.concatenate(
        [jnp.where(row < DA_DIM, qT, zero), jnp.where(row >= DA_DIM, qT, zero)], axis=1)
    qa_ref[2 * DA_DIM:2 * DA_DIM + POS_ROWS, :] = jnp.concatenate([srow_ref[...]] * (w // 128), axis=1).astype(BF16)
    qa_ref[2 * DA_DIM + POS_ROWS:, :] = jnp.zeros((2 * DA_DIM - POS_ROWS, w), BF16)
    key_off = lax.broadcasted_iota(I32, (tk, w), 0)

    def raw_scores(kt):
        k_t = k_ref[pl.ds(pl.multiple_of(kt * tk, tk), tk), :]
        return jnp.dot(jnp.concatenate([k_t, pos_ref[...]], axis=1), qa_ref[...], preferred_element_type=F32)

    def scores(kt, s_ref):
        s = raw_scores(kt)
        s_ref[...] = s
        return jnp.max(s, axis=0, keepdims=True)

    def probs(s_ref, p_ref, mx, off, m_old):
        m_new = jnp.maximum(m_old, mx + off)
        p_ref[...] = jnp.exp(s_ref[...] - (m_new - off)).astype(BF16)
        return m_new, jnp.exp(m_old - m_new)

    def accumulate(kt, p_ref, alpha):
        acc_ref[...] = alpha * acc_ref[...] + jnp.dot(vT_ref[kt], p_ref[...], preferred_element_type=F32)

    n_full = q0 // tk
    mx_a = scores(0, sa_ref)

    kd = n_full * tk
    lane = lax.broadcasted_iota(I32, (tk, w), 1)
    t_loc = jnp.where(lane >= tq, lane - tq, lane)
    s = jnp.where(key_off - t_loc <= q0 - kd, raw_scores(n_full), NEG)
    off_d = slope * kd.astype(F32)
    m = jnp.max(s, axis=0, keepdims=True) + off_d
    acc_ref[...] = jnp.dot(vT_ref[n_full], jnp.exp(s - (m - off_d)).astype(BF16), preferred_element_type=F32)
    pb_ref[...] = jnp.zeros_like(pb_ref)

    def pair(j, carry):
        m, mx_a, alpha_b = carry
        ta = 2 * j
        tb = ta + 1
        off_a = slope * (ta * tk).astype(F32)
        off_b = jnp.where(tb < n_full, slope * (tb * tk).astype(F32), NEG)
        accumulate(jnp.maximum(ta - 1, 0), pb_ref, alpha_b)
        m, alpha_a = probs(sa_ref, pa_ref, mx_a, off_a, m)
        mx_b = scores(jnp.minimum(tb, nk - 1), sb_ref)
        accumulate(ta, pa_ref, alpha_a)
        m, alpha_b = probs(sb_ref, pb_ref, mx_b, off_b, m)
        mx_a = scores(jnp.minimum(ta + 2, nk - 1), sa_ref)
        return m, mx_a, alpha_b

    n_pairs = (n_full + 1) // 2
    _, _, alpha_b = lax.fori_loop(0, n_pairs, pair, (m, mx_a, jnp.ones((1, w), F32)))
    accumulate(jnp.minimum(jnp.maximum(2 * n_pairs - 1, 0), nk - 1), pb_ref, alpha_b)

    lam = (jnp.exp(jnp.sum(lq1_ref[...] * lk1_ref[...], axis=-1, keepdims=True))
           - jnp.exp(jnp.sum(lq2_ref[...] * lk2_ref[...], axis=-1, keepdims=True)) + lam_init)
    on = acc_ref[0:DA_VDIM, :] * (1.0 / acc_ref[DA_VDIM:DA_VDIM + 1, :])
    o = on[:, :tq] - lam * on[:, tq:]
    r = lax.rsqrt(jnp.mean(o * o, axis=0, keepdims=True) + DA_SUBLN_EPS)
    oT_ref[...] = (o * r * subw_ref[...] * (1.0 - lam_init)).astype(BF16)


def _diff_attention(slopes, qT, slab3, vT, srows, lq1, lk1, lq2, lk2, subw, lam_init, tq, tk):
    B, _, S = qT.shape
    nk = S // tk
    assert tk % tq == 0 and S % tk == 0
    tile = tq
    kcol = COL_DAK // 128
    v_rows = vT.shape[3]
    vec = lambda a: a.reshape(1, DA_DIM).astype(F32)
    grid_spec = pltpu.PrefetchScalarGridSpec(
        num_scalar_prefetch=1,
        grid=(B, DA_HEADS, S // tile),
        in_specs=[pl.BlockSpec((None, 128, tile), lambda b, h, i, s: (b, h, i)),
                  pl.BlockSpec((None, S, 128), lambda b, h, i, s: (b, 0, kcol + h)),
                  pl.BlockSpec((None, None, nk, v_rows, tk), lambda b, h, i, s: (b, h, 0, 0, 0)),
                  pl.BlockSpec((tk, 128), lambda b, h, i, s: (0, 0)),
                  pl.BlockSpec((None, POS_ROWS, 128), lambda b, h, i, s: (h, 0, 0)),
                  pl.BlockSpec((1, DA_DIM), lambda b, h, i, s: (0, 0)),
                  pl.BlockSpec((1, DA_DIM), lambda b, h, i, s: (0, 0)),
                  pl.BlockSpec((1, DA_DIM), lambda b, h, i, s: (0, 0)),
                  pl.BlockSpec((1, DA_DIM), lambda b, h, i, s: (0, 0)),
                  pl.BlockSpec((DA_VDIM, 1), lambda b, h, i, s: (0, 0))],
        out_specs=pl.BlockSpec((None, 128, tile), lambda b, h, i, s: (b, h, i)),
        scratch_shapes=[pltpu.VMEM((v_rows, 2 * tq), F32), pltpu.VMEM((4 * DA_DIM, 2 * tq), BF16),
                        pltpu.VMEM((tk, 2 * tq), F32), pltpu.VMEM((tk, 2 * tq), F32),
                        pltpu.VMEM((tk, 2 * tq), BF16), pltpu.VMEM((tk, 2 * tq), BF16)],
    )
    return pl.pallas_call(
        functools.partial(_da_kernel, tq=tq, tk=tk, lam_init=lam_init),
        grid_spec=grid_spec,
        out_shape=jax.ShapeDtypeStruct((B, DA_HEADS * DA_VDIM, S), BF16),
        compiler_params=_cparams(("parallel", "parallel", "arbitrary")),
        name="diffattn",
    )(slopes, qT, slab3, vT, _pos_features(tk), srows, vec(lq1), vec(lk1), vec(lq2), vec(lk2),
      subw.reshape(DA_VDIM, 1).astype(F32))


def _nsa_kernel(slopes_ref, qT_ref, kc_ref, vcT_ref, ksa_ref, vsT_ref, kwa_ref, vwT_ref, wb_ref, gT_ref, mT_ref,
                oT_ref, acc_ref, res_ref, base_ref, qa_ref, sa_ref, sb_ref, pa_ref, pb_ref, *, t, tk, n_top):
    g = pl.program_id(1)
    qi = pl.program_id(2)
    q0 = qi * t
    w = NSA_HPG * t
    n_cp = kc_ref.shape[0]
    n_sel = mT_ref.shape[0]
    lane1 = lax.broadcasted_iota(I32, (1, w), 1)
    slope_row = jnp.zeros((1, w), F32)
    for hh in range(NSA_HPG):
        slope_row = jnp.where((lane1 >= hh * t) & (lane1 < (hh + 1) * t), slopes_ref[g * NSA_HPG + hh], slope_row)
    q_all = qT_ref[...]
    gates = jax.nn.sigmoid(gT_ref[...])
    key_off = lax.broadcasted_iota(I32, (tk, w), 0)
    t_loc = jnp.bitwise_and(lax.broadcasted_iota(I32, (tk, w), 1), t - 1)
    base_ref[...] = slope_row * key_off.astype(F32)

    c_idx = lax.broadcasted_iota(I32, (n_cp, w), 0)
    t_c = q0 + jnp.bitwise_and(lax.broadcasted_iota(I32, (n_cp, w), 1), t - 1)
    d_c = t_c - (c_idx * CMP_STRIDE + (CMP_BLOCK - 1))
    ok_c = d_c >= 0
    s = jnp.dot(kc_ref[...], q_all, preferred_element_type=F32) - slope_row * d_c.astype(F32)
    s = jnp.where(ok_c, s, NEG)
    e = jnp.exp(s - jnp.max(s, axis=0, keepdims=True))
    inv = 1.0 / jnp.sum(e, axis=0, keepdims=True)
    p = jnp.where(ok_c, e * inv, 0.0)
    res_ref[...] = gates[0:1, :] * jnp.dot(vcT_ref[...], p.astype(BF16), preferred_element_type=F32)
    p_sum = p[:, 0:t]
    for hh in range(1, NSA_HPG):
        p_sum = p_sum + p[:, hh * t:(hh + 1) * t]

    p_hi = p_sum.astype(BF16)
    p_lo = (p_sum - p_hi.astype(F32)).astype(BF16)
    mT = mT_ref[...]
    imp = jnp.dot(mT, p_hi, preferred_element_type=F32) + jnp.dot(mT, p_lo, preferred_element_type=F32)
    blk = lax.broadcasted_iota(I32, (n_sel, t), 0)
    t_s = q0 + lax.broadcasted_iota(I32, (n_sel, t), 1)
    cur = jnp.right_shift(t_s, int(math.log2(SEL_BLOCK)))
    forced = (blk == 0) | (blk == cur) | (blk == cur - 1)
    score = jnp.where(blk * SEL_BLOCK <= t_s, imp + jnp.where(forced, FORCE_BONUS, 0.0), NEG)
    blk_f = blk.astype(F32)
    sel = jnp.zeros((n_sel, t), F32)
    for _ in range(n_top):
        _, idx = _first_max(score, blk_f)
        hit = blk_f == idx
        sel = jnp.where(hit, 1.0, sel)
        score = jnp.where(hit, -jnp.inf, score)
    selb = jnp.where(sel > 0.5, 0.0, NEG)
    row64 = lax.broadcasted_iota(I32, (NSA_DIM, w), 0)
    qa_ref[0:NSA_DIM, :] = q_all
    qa_ref[NSA_DIM:2 * NSA_DIM, :] = jnp.where(row64 == 0, NEG, 0.0).astype(BF16)
    qa_ref[2 * NSA_DIM:2 * NSA_DIM + n_sel, :] = jnp.concatenate([selb] * NSA_HPG, axis=1).astype(BF16)
    if n_sel < SEL_COLS:
        qa_ref[2 * NSA_DIM + n_sel:, :] = jnp.zeros((SEL_COLS - n_sel, w), BF16)

    n_wt = (WINDOW + t) // t
    kw_t = kwa_ref[pl.ds(pl.multiple_of(q0, t), WINDOW + t), :]
    s = jnp.dot(kw_t, qa_ref[0:2 * NSA_DIM, :], preferred_element_type=F32) + wb_ref[...]
    p = jnp.exp(s - jnp.max(s, axis=0, keepdims=True))
    inv = 1.0 / jnp.sum(p, axis=0, keepdims=True)
    p = p.astype(BF16)
    o_w = jnp.dot(vwT_ref[qi], p[0:t, :], preferred_element_type=F32)
    for c in range(1, n_wt):
        o_w = o_w + jnp.dot(vwT_ref[qi + c], p[c * t:(c + 1) * t, :], preferred_element_type=F32)
    res_ref[...] = res_ref[...] + gates[2:3, :] * (o_w * inv)

    nk = vsT_ref.shape[0]

    def scores(kt, s_ref):
        k_t = ksa_ref[pl.ds(pl.multiple_of(kt * tk, tk), tk), :]
        s = jnp.dot(k_t, qa_ref[...], preferred_element_type=F32) + base_ref[...]
        s_ref[...] = s
        return jnp.max(s, axis=0, keepdims=True)

    def probs(s_ref, p_ref, mx, off, m_old, l_old):
        m_new = jnp.maximum(m_old, mx + off)
        p = jnp.exp(s_ref[...] - (m_new - off))
        alpha = jnp.exp(m_old - m_new)
        p_ref[...] = p.astype(BF16)
        return m_new, alpha * l_old + jnp.sum(p, axis=0, keepdims=True), alpha

    def accumulate(kt, p_ref, alpha):
        acc_ref[...] = alpha * acc_ref[...] + jnp.dot(vsT_ref[kt], p_ref[...], preferred_element_type=F32)

    n_full = q0 // tk
    mx_a = scores(0, sa_ref)

    kd = n_full * tk
    s = jnp.dot(ksa_ref[pl.ds(pl.multiple_of(kd, tk), tk), :], qa_ref[...], preferred_element_type=F32) + base_ref[...]
    s = jnp.where(key_off - t_loc <= q0 - kd, s, NEG)
    off_d = slope_row * kd.astype(F32)
    m = jnp.max(s, axis=0, keepdims=True) + off_d
    p = jnp.exp(s - (m - off_d))
    l = jnp.sum(p, axis=0, keepdims=True)
    acc_ref[...] = jnp.dot(vsT_ref[n_full], p.astype(BF16), preferred_element_type=F32)
    pb_ref[...] = jnp.zeros_like(pb_ref)

    def pair(j, carry):
        m, l, mx_a, alpha_b = carry
        ta = 2 * j
        tb = ta + 1
        off_a = slope_row * (ta * tk).astype(F32)
        off_b = jnp.where(tb < n_full, slope_row * (tb * tk).astype(F32), NEG)
        accumulate(jnp.maximum(ta - 1, 0), pb_ref, alpha_b)
        m, l, alpha_a = probs(sa_ref, pa_ref, mx_a, off_a, m, l)
        mx_b = scores(jnp.minimum(tb, nk - 1), sb_ref)
        accumulate(ta, pa_ref, alpha_a)
        m, l, alpha_b = probs(sb_ref, pb_ref, mx_b, off_b, m, l)
        mx_a = scores(jnp.minimum(ta + 2, nk - 1), sa_ref)
        return m, l, mx_a, alpha_b

    n_pairs = (n_full + 1) // 2
    m, l, _, alpha_b = lax.fori_loop(0, n_pairs, pair, (m, l, mx_a, jnp.ones((1, w), F32)))
    accumulate(jnp.minimum(jnp.maximum(2 * n_pairs - 1, 0), nk - 1), pb_ref, alpha_b)
    oT_ref[...] = (res_ref[...] + gates[1:2, :] * (acc_ref[...] * (1.0 / l))).astype(BF16)


def _nsa_attention(slopes, qT, kc, vcT, ksa, vsT, kwa, vwT, wb, gT, mT, t, tk):
    B, G, nq, _, w = qT.shape
    S = nq * t
    n_cp = kc.shape[2]
    n_sel = S // SEL_BLOCK
    n_top = min(SEL_TOPK, n_sel)
    assert tk % t == 0 and S % tk == 0 and t & (t - 1) == 0 and WINDOW % t == 0 and n_sel <= SEL_COLS
    fixed = lambda b, g, i, s: (b, g, 0, 0)
    fixed5 = lambda b, g, i, s: (b, g, 0, 0, 0)
    tile5 = lambda b, g, i, s: (b, g, i, 0, 0)
    grid_spec = pltpu.PrefetchScalarGridSpec(
        num_scalar_prefetch=1,
        grid=(B, G, nq),
        in_specs=[pl.BlockSpec((None, None, None, NSA_DIM, w), tile5),
                  pl.BlockSpec((None, None, n_cp, NSA_DIM), fixed),
                  pl.BlockSpec((None, None, NSA_DIM, n_cp), fixed),
                  pl.BlockSpec((None, None, S, ksa.shape[-1]), fixed),
                  pl.BlockSpec((None, None, S // tk, NSA_DIM, tk), fixed5),
                  pl.BlockSpec((None, None, S + WINDOW, 2 * NSA_DIM), fixed),
                  pl.BlockSpec((None, None, (S + WINDOW) // t, NSA_DIM, t), fixed5),
                  pl.BlockSpec((None, WINDOW + t, w), lambda b, g, i, s: (g, 0, 0)),
                  pl.BlockSpec((None, None, None, 3, w), tile5),
                  pl.BlockSpec((n_sel, n_cp), lambda b, g, i, s: (0, 0))],
        out_specs=pl.BlockSpec((None, None, None, NSA_DIM, w), tile5),
        scratch_shapes=[pltpu.VMEM((NSA_DIM, w), F32),
                        pltpu.VMEM((NSA_DIM, w), F32),
                        pltpu.VMEM((tk, w), F32),
                        pltpu.VMEM((2 * NSA_DIM + SEL_COLS, w), BF16),
                        pltpu.VMEM((tk, w), F32), pltpu.VMEM((tk, w), F32),
                        pltpu.VMEM((tk, w), BF16), pltpu.VMEM((tk, w), BF16)],
    )
    return pl.pallas_call(
        functools.partial(_nsa_kernel, t=t, tk=tk, n_top=n_top),
        grid_spec=grid_spec,
        out_shape=jax.ShapeDtypeStruct((B, G, nq, NSA_DIM, w), BF16),
        compiler_params=_cparams(("parallel", "parallel", "arbitrary")),
        name="nsa",
    )(slopes, qT, kc, vcT, ksa, vsT, kwa, vwT, wb, gT, mT)


def _pack_halves(v):
    c = v.shape[1] // 2
    lo = lax.bitcast_convert_type(v[:, :c].astype(BF16).astype(F32), U32)
    hi = lax.bitcast_convert_type(v[:, c:].astype(BF16).astype(F32), U32)
    return hi | (lo >> 16)


def _unpack_halves(u):
    lo = lax.bitcast_convert_type(u << 16, F32)
    hi = lax.bitcast_convert_type(u & jnp.uint32(0xFFFF0000), F32)
    return lo, hi


def _merge_kernel(oa_ref, ob_ref, ga_ref, gb_ref, x_ref, g1_ref, nw_ref, sh_ref, sc_ref,
                  wa_ref, wb_ref, wo_ref, wr_ref, x1_ref, h2_ref, lg_ref):
    ya = jnp.dot(oa_ref[...], wa_ref[...], preferred_element_type=F32)
    yb = jnp.dot(ob_ref[...], wb_ref[...], preferred_element_type=F32)
    merged = (jax.nn.sigmoid(ga_ref[...].astype(F32)) * ya
              + jax.nn.sigmoid(gb_ref[...].astype(F32)) * yb).astype(BF16)
    mix = jnp.dot(merged, wo_ref[...], preferred_element_type=F32)
    x1 = x_ref[...] + g1_ref[...] * mix
    x1_ref[...] = x1
    h2 = _modulate(x1, nw_ref[...], sh_ref[...], sc_ref[...])
    h2_ref[...] = _pack_halves(h2)
    lg_ref[...] = jnp.dot(h2, wr_ref[...], preferred_element_type=F32, precision=HIGHEST)


def _merge(oa, ob, slab, x2, g1, nw, sh, sc, wa, wb, wo, wr, S):
    N, D = x2.shape
    tm = 256
    per_b = S // tm
    mcol = COL_MERGE // D
    row = lambda i: (i, 0)
    full = lambda i: (0, 0)
    perb = lambda i: (i // per_b, 0, 0)
    n_lg = wr.shape[1]
    return pl.pallas_call(
        _merge_kernel,
        grid=(N // tm,),
        in_specs=[pl.BlockSpec((tm, oa.shape[1]), row),
                  pl.BlockSpec((tm, ob.shape[1]), row),
                  pl.BlockSpec((tm, D), lambda i: (i, mcol)),
                  pl.BlockSpec((tm, D), lambda i: (i, mcol + 1)),
                  pl.BlockSpec((tm, D), row),
                  pl.BlockSpec((None, 1, D), perb),
                  pl.BlockSpec((1, D), full),
                  pl.BlockSpec((None, 1, D), perb),
                  pl.BlockSpec((None, 1, D), perb),
                  pl.BlockSpec(wa.shape, full),
                  pl.BlockSpec(wb.shape, full),
                  pl.BlockSpec(wo.shape, full),
                  pl.BlockSpec(wr.shape, full)],
        out_specs=[pl.BlockSpec((tm, D), row), pl.BlockSpec((tm, D // 2), row), pl.BlockSpec((tm, n_lg), row)],
        out_shape=[jax.ShapeDtypeStruct((N, D), F32), jax.ShapeDtypeStruct((N, D // 2), U32),
                   jax.ShapeDtypeStruct((N, n_lg), F32)],
        compiler_params=_cparams(("parallel",)),
        name="merge",
    )(oa, ob, slab, slab, x2, g1.reshape(-1, 1, D), nw.reshape(1, D), sh.reshape(-1, 1, D),
      sc.reshape(-1, 1, D), wa, wb, wo, wr)


def _first_max(v, idx):
    mx = jnp.max(v, axis=0, keepdims=True)
    first = jnp.min(jnp.where(v == mx, idx, float(v.shape[0])), axis=0, keepdims=True)
    return mx, first


def _route_kernel(lg_ref, b_ref, e_ref, w_ref):
    lg = lg_ref[...]
    tn = lg.shape[1]
    per_g = N_EXPERTS // N_EXPERT_GROUPS
    scores = jax.nn.sigmoid(lg)
    biased = scores + b_ref[...]
    gi = lax.broadcasted_iota(I32, (N_EXPERT_GROUPS, tn), 0).astype(F32)
    gscore = jnp.zeros((N_EXPERT_GROUPS, tn), F32)
    for g in range(N_EXPERT_GROUPS):
        grp = biased[g * per_g:(g + 1) * per_g, :]
        m1, f1 = _first_max(grp, gi)
        m2 = jnp.max(jnp.where(gi == f1, -jnp.inf, grp), axis=0, keepdims=True)
        gscore = jnp.where(gi == g, m1 + m2, gscore)
    gsel = jnp.zeros((N_EXPERT_GROUPS, tn), F32)
    for _ in range(TOPK_EXPERT_GROUPS):
        _, f = _first_max(gscore, gi)
        hit = gi == f
        gsel = jnp.where(hit, 1.0, gsel)
        gscore = jnp.where(hit, -jnp.inf, gscore)
    ei = lax.broadcasted_iota(I32, (N_EXPERTS, tn), 0).astype(F32)
    emask = jnp.zeros((N_EXPERTS, tn), F32)
    for g in range(N_EXPERT_GROUPS):
        in_g = (ei >= g * per_g) & (ei < (g + 1) * per_g)
        emask = jnp.where(in_g, gsel[g:g + 1, :], emask)
    masked = jnp.where(emask > 0.5, biased, NEG)
    e_out = jnp.zeros((MOE_TOPK, tn), F32)
    w_out = jnp.zeros((MOE_TOPK, tn), F32)
    for r in range(MOE_TOPK):
        _, f = _first_max(masked, ei)
        hit = ei == f
        wv = jnp.sum(jnp.where(hit, scores, 0.0), axis=0, keepdims=True)
        e_out = jnp.where(gi == r, f, e_out)
        w_out = jnp.where(gi == r, wv, w_out)
        masked = jnp.where(hit, -jnp.inf, masked)
    e_ref[...] = e_out.astype(I32)
    w_ref[...] = w_out / jnp.sum(w_out, axis=0, keepdims=True) * ROUTED_SCALE


def _route(lgT, rb):
    E, N = lgT.shape
    tn = 512
    return pl.pallas_call(
        _route_kernel,
        grid=(N // tn,),
        in_specs=[pl.BlockSpec((E, tn), lambda i: (0, i)), pl.BlockSpec((E, 1), lambda i: (0, 0))],
        out_specs=[pl.BlockSpec((MOE_TOPK, tn), lambda i: (0, i)), pl.BlockSpec((MOE_TOPK, tn), lambda i: (0, i))],
        out_shape=[jax.ShapeDtypeStruct((MOE_TOPK, N), I32), jax.ShapeDtypeStruct((MOE_TOPK, N), F32)],
        compiler_params=_cparams(("parallel",)),
        name="route",
    )(lgT, rb.reshape(E, 1).astype(F32))


def _slots_kernel(e_ref, tri_ref, lt_ref, dest_ref, cnt_ref, carry_ref, *, rb):
    phase = pl.program_id(0)
    i = pl.program_id(1)
    e = e_ref[...]
    tn = e.shape[1]
    ei = lax.broadcasted_iota(I32, (N_EXPERTS, tn), 0)

    @pl.when((phase == 0) & (i == 0))
    def _():
        carry_ref[...] = jnp.zeros_like(carry_ref)

    @pl.when(phase == 0)
    def _():
        tot = jnp.zeros((N_EXPERTS, 1), F32)
        for k in range(MOE_TOPK):
            oh = jnp.where(ei == e[k:k + 1, :], 1.0, 0.0)
            tot = tot + jnp.sum(oh, axis=1, keepdims=True)
        carry_ref[...] = carry_ref[...] + tot
        dest_ref[...] = jnp.zeros_like(dest_ref)

    @pl.when((phase == 1) & (i == 0))
    def _():
        cnt = carry_ref[...]
        cnt_ref[...] = cnt
        padded = jnp.broadcast_to(jnp.ceil(cnt / rb) * rb, (N_EXPERTS, 128))
        first = jnp.dot(lt_ref[...], padded, preferred_element_type=F32, precision=HIGHEST)
        carry_ref[...] = first[:, :1]

    @pl.when(phase == 1)
    def _():
        carry = carry_ref[...]
        ki = lax.broadcasted_iota(I32, (MOE_TOPK, tn), 0)
        dest = jnp.zeros((MOE_TOPK, tn), F32)
        for k in range(MOE_TOPK):
            hit = ei == e[k:k + 1, :]
            oh = jnp.where(hit, 1.0, 0.0)
            before = jnp.dot(oh.astype(BF16), tri_ref[...], preferred_element_type=F32)
            row = jnp.sum(jnp.where(hit, before + carry, 0.0), axis=0, keepdims=True)
            dest = jnp.where(ki == k, row, dest)
            carry = carry + jnp.sum(oh, axis=1, keepdims=True)
        carry_ref[...] = carry
        dest_ref[...] = dest.astype(I32)


def _slots(top_eT, rb):
    K, N = top_eT.shape
    tn = 512
    idx = np.arange(tn)
    tri = jnp.asarray((idx[:, None] < idx[None, :]).astype(np.float32), BF16)
    e_idx = np.arange(N_EXPERTS)
    lt = jnp.asarray((e_idx[None, :] < e_idx[:, None]).astype(np.float32), F32)
    return pl.pallas_call(
        functools.partial(_slots_kernel, rb=rb),
        grid=(2, N // tn),
        in_specs=[pl.BlockSpec((K, tn), lambda p, i: (0, i)),
                  pl.BlockSpec((tn, tn), lambda p, i: (0, 0)),
                  pl.BlockSpec((N_EXPERTS, N_EXPERTS), lambda p, i: (0, 0))],
        out_specs=[pl.BlockSpec((K, tn), lambda p, i: (0, i * p)),
                   pl.BlockSpec((N_EXPERTS, 1), lambda p, i: (0, 0))],
        out_shape=[jax.ShapeDtypeStruct((K, N), I32), jax.ShapeDtypeStruct((N_EXPERTS, 1), F32)],
        scratch_shapes=[pltpu.VMEM((N_EXPERTS, 1), F32)],
        compiler_params=_cparams(("arbitrary", "arbitrary")),
        name="slots",
    )(top_eT, tri, lt)


def _dispatch_kernel(dest_ref, h_ref, xs_in_ref, xs_ref, sem, *, tt):
    del xs_in_ref

    def body(n, c):
        src = h_ref.at[pl.ds(n, 1), :]
        for k in range(MOE_TOPK):
            d = dest_ref[0, 0, k * tt + n]
            pltpu.make_async_copy(src, xs_ref.at[pl.ds(d, 1), :], sem).start()
        return c

    lax.fori_loop(0, tt, body, 0)
    for k in range(MOE_TOPK):
        pltpu.make_async_copy(h_ref, xs_ref.at[pl.ds(0, tt), :], sem).wait()


def _dispatch(dest_tiles, h2, n_rows, tt):
    N, D = h2.shape
    xs0 = jnp.zeros((n_rows, D), h2.dtype)
    return pl.pallas_call(
        functools.partial(_dispatch_kernel, tt=tt),
        grid=(N // tt,),
        in_specs=[pl.BlockSpec((1, 1, MOE_TOPK * tt), lambda i: (i, 0, 0), memory_space=pltpu.SMEM),
                  pl.BlockSpec((tt, D), lambda i: (i, 0)),
                  pl.BlockSpec(memory_space=pl.ANY)],
        out_specs=pl.BlockSpec(memory_space=pl.ANY),
        out_shape=jax.ShapeDtypeStruct((n_rows, D), h2.dtype),
        scratch_shapes=[pltpu.SemaphoreType.DMA(())],
        input_output_aliases={2: 0},
        compiler_params=_cparams(("arbitrary",)),
        name="dispatch",
    )(dest_tiles, h2, xs0)


def _experts_kernel(be_ref, x_ref, wgu_ref, wd_ref, y_ref):
    del be_ref
    lo, hi = _unpack_halves(x_ref[...])
    x = jnp.concatenate([lo.astype(BF16), hi.astype(BF16)], axis=1)
    gu = jnp.dot(x, wgu_ref[...], preferred_element_type=F32)
    hdim = gu.shape[1] // 2
    g = gu[:, :hdim]
    act = (g * jax.nn.sigmoid(g) * gu[:, hdim:]).astype(BF16)
    y_ref[...] = _pack_halves(jnp.dot(act, wd_ref[...], preferred_element_type=F32))


def _experts(blk_e, xs, wgu, wd, rb):
    n_rows, D = xs.shape
    grid_spec = pltpu.PrefetchScalarGridSpec(
        num_scalar_prefetch=1,
        grid=(n_rows // rb,),
        in_specs=[pl.BlockSpec((rb, D), lambda i, be: (i, 0)),
                  pl.BlockSpec((None,) + wgu.shape[1:], lambda i, be: (be[i], 0, 0)),
                  pl.BlockSpec((None,) + wd.shape[1:], lambda i, be: (be[i], 0, 0))],
        out_specs=pl.BlockSpec((rb, D), lambda i, be: (i, 0)),
    )
    return pl.pallas_call(
        _experts_kernel,
        grid_spec=grid_spec,
        out_shape=jax.ShapeDtypeStruct((n_rows, D), U32),
        compiler_params=_cparams(("arbitrary",)),
        name="experts",
    )(blk_e, xs, wgu, wd)


def _combine_kernel(dest_ref, ys_ref, w_ref, h_ref, x1_ref, g2_ref, fw_ref, sgu_ref, sd_ref, o_ref, buf_ref, sem, *, tt):
    def body(n, c):
        for k in range(MOE_TOPK):
            d = dest_ref[0, 0, k * tt + n]
            pltpu.make_async_copy(ys_ref.at[pl.ds(d, 1), :], buf_ref.at[k, pl.ds(n, 1), :], sem).start()
        return c

    lax.fori_loop(0, tt, body, 0)

    h_lo, h_hi = _unpack_halves(h_ref[...])
    h = jnp.concatenate([h_lo.astype(BF16), h_hi.astype(BF16)], axis=1)
    gu = jnp.dot(h, sgu_ref[...], preferred_element_type=F32)
    hdim = gu.shape[1] // 2
    g = gu[:, :hdim]
    ffn = jnp.dot((g * jax.nn.sigmoid(g) * gu[:, hdim:]).astype(BF16), sd_ref[...], preferred_element_type=F32)

    for k in range(MOE_TOPK):
        pltpu.make_async_copy(ys_ref.at[pl.ds(0, tt), :], buf_ref.at[k], sem).wait()
    w = w_ref[...]
    r_lo = jnp.zeros(h_lo.shape, F32)
    r_hi = jnp.zeros(h_hi.shape, F32)
    for k in range(MOE_TOPK):
        y_lo, y_hi = _unpack_halves(buf_ref[k])
        r_lo = r_lo + w[:, k:k + 1] * y_lo
        r_hi = r_hi + w[:, k:k + 1] * y_hi
    ffn = ffn + jnp.concatenate([r_lo, r_hi], axis=1)
    x2 = x1_ref[...] + g2_ref[...] * ffn
    r = lax.rsqrt(jnp.mean(x2 * x2, axis=-1, keepdims=True) + NORM_EPS)
    o_ref[...] = x2 * r * fw_ref[...]


def _combine(dest_tiles, ys, top_w, h2p, x1, g2, fw, sgu, sd, S, tt):
    N, D = x1.shape
    per_b = S // tt
    row = lambda i: (i, 0)
    full = lambda i: (0, 0)
    return pl.pallas_call(
        functools.partial(_combine_kernel, tt=tt),
        grid=(N // tt,),
        in_specs=[pl.BlockSpec((1, 1, MOE_TOPK * tt), lambda i: (i, 0, 0), memory_space=pltpu.SMEM),
                  pl.BlockSpec(memory_space=pl.ANY),
                  pl.BlockSpec((tt, MOE_TOPK), row),
                  pl.BlockSpec((tt, D // 2), row),
                  pl.BlockSpec((tt, D), row),
                  pl.BlockSpec((None, 1, D), lambda i: (i // per_b, 0, 0)),
                  pl.BlockSpec((1, D), full),
                  pl.BlockSpec(sgu.shape, full),
                  pl.BlockSpec(sd.shape, full)],
        out_specs=pl.BlockSpec((tt, D), row),
        out_shape=jax.ShapeDtypeStruct((N, D), F32),
        scratch_shapes=[pltpu.VMEM((MOE_TOPK, tt, D // 2), U32), pltpu.SemaphoreType.DMA(())],
        compiler_params=_cparams(("arbitrary",)),
        name="combine",
    )(dest_tiles, ys, top_w, h2p, x1, g2.reshape(-1, 1, D), fw.reshape(1, D), sgu, sd)


def _cmp_to_sel_T(n_cp, n_sel):
    c0 = np.arange(n_cp)[None, :] * CMP_STRIDE
    s0 = np.arange(n_sel)[:, None] * SEL_BLOCK
    ov = np.minimum(c0 + CMP_BLOCK, s0 + SEL_BLOCK) - np.maximum(c0, s0)
    m = np.clip(ov, 0, None).astype(np.float32) / CMP_BLOCK
    m[:, n_cp - 1] = 0.0
    return jnp.asarray(m, BF16)


def _slab_weights(w_in):
    D = w_in.shape[0]
    sizes = [1024, 1024, 1024, 512, 128, 128, 128, 128, 128, 128, 24, 2048]
    offs = np.concatenate([[0], np.cumsum(sizes)])
    part = lambda i: w_in[:, offs[i]:offs[i + 1]]
    pieces = [part(0), part(1), part(2), part(11), part(3)] + [part(i) for i in range(4, 10)] + [part(10)]
    w = jnp.concatenate(pieces, axis=1)
    return jnp.pad(w, ((0, 0), (0, SLAB_COLS - w.shape[1]))).astype(BF16)


def _layer(x, c, ada_w, ada_b, norm1_w, w_in, da_lq1, da_lk1, da_lq2, da_lk2, da_subln_w,
           cmp_k_pe, cmp_k_w1, cmp_k_w2, cmp_v_pe, cmp_v_w1, cmp_v_w2, w_da_out, w_nsa_out, w_o,
           norm2_w, router_w, router_b, exp_w_gate, exp_w_up, exp_w_down,
           sh_w_gate, sh_w_up, sh_w_down, final_norm_w, lam_init):
    B, S, D = x.shape
    N = B * S
    G, HPG = NSA_GROUPS, NSA_HPG
    x2 = x.reshape(N, D)

    mod = _ada(c, ada_w, ada_b)
    sh1, sc1, g1, sh2, sc2, g2 = jnp.split(mod, 6, axis=-1)

    slab = _inproj(x2, norm1_w, sh1, sc1, _slab_weights(w_in), S)
    slab3 = slab.reshape(B, S, SLAB_COLS)

    i_all = np.arange(1, DA_HEADS + NSA_HEADS + 1, dtype=np.float32)
    slopes = (2.0 ** (-8.0 * i_all / (DA_HEADS + NSA_HEADS))).astype(np.float32)
    slopes_a = jnp.asarray(slopes[0::2])
    slopes_b = jnp.asarray(slopes[1::2])

    ta_q, ta_k = 256, min(512, S)
    qT = (slab3[:, :, COL_DAQ:COL_DAQ + 1024] * jnp.asarray(DA_DIM ** -0.5, BF16)).transpose(0, 2, 1)
    vT = slab3[:, :, COL_DAV:COL_DAV + 1024].reshape(B, S // ta_k, ta_k, DA_HEADS, DA_VDIM).transpose(0, 3, 1, 4, 2)
    oaT = _diff_attention(slopes_a, qT, slab3, _with_ones_row(vT), _slope_rows(slopes[0::2]), da_lq1, da_lk1,
                          da_lq2, da_lk2, da_subln_w, lam_init, ta_q, ta_k)
    oa = oaT.transpose(0, 2, 1).reshape(N, DA_HEADS * DA_VDIM)

    tb, tb_k = 128, min(256, S)
    n_cp = S // CMP_STRIDE
    n_sel = S // SEL_BLOCK

    def kv_groups(j):
        c0 = COL_KV6 + 128 * j
        return slab3[:, :, c0:c0 + 128].reshape(B, S, G, NSA_DIM).transpose(0, 2, 1, 3)

    def kv_tiles_T(a, rows):
        return a.reshape(B, G, S // rows, rows, NSA_DIM).transpose(0, 1, 2, 4, 3)

    ck_rows = kv_groups(0).reshape(B * G, n_cp, CMP_STRIDE * NSA_DIM)
    cv_rows = kv_groups(1).reshape(B * G, n_cp, CMP_STRIDE * NSA_DIM)
    kc = _compress(ck_rows, cmp_k_pe, cmp_k_w1, cmp_k_w2).reshape(B, G, n_cp, NSA_DIM).astype(BF16)
    vc = _compress(cv_rows, cmp_v_pe, cmp_v_w1, cmp_v_w2).reshape(B, G, n_cp, NSA_DIM).astype(BF16)
    nq = (slab3[:, :, COL_NSAQ:COL_NSAQ + 512] * jnp.asarray(NSA_DIM ** -0.5, BF16))
    nqT = nq.reshape(B, S // tb, tb, G, HPG, NSA_DIM).transpose(0, 3, 1, 5, 4, 2).reshape(B, G, S // tb, NSA_DIM, HPG * tb)
    gT = (slab3[:, :, COL_NSAG:COL_NSAG + 3 * NSA_HEADS].astype(F32).reshape(B, S // tb, tb, G, HPG, 3)
          .transpose(0, 3, 1, 5, 4, 2).reshape(B, G, S // tb, 3, HPG * tb))
    onehot = jnp.asarray((np.arange(S)[:, None] // SEL_BLOCK == np.arange(SEL_COLS)[None, :]).astype(np.float32), BF16)
    zeros64 = jnp.zeros((B, G, S, NSA_DIM), BF16)
    ksa = jnp.concatenate([kv_groups(2), zeros64, jnp.broadcast_to(onehot, (B, G, S, SEL_COLS))], axis=-1)
    pad_rows = jnp.zeros((B, G, WINDOW, 2 * NSA_DIM), BF16).at[..., NSA_DIM].set(1.0)
    kwa = jnp.concatenate([pad_rows, jnp.concatenate([kv_groups(4), zeros64], axis=-1)], axis=2)
    vw_pad = jnp.pad(kv_groups(5), ((0, 0), (0, 0), (WINDOW, 0), (0, 0)))
    vwT = vw_pad.reshape(B, G, (S + WINDOW) // tb, tb, NSA_DIM).transpose(0, 1, 2, 4, 3)
    r_w = np.arange(WINDOW + tb)[:, None]
    lane_w = np.arange(HPG * tb)[None, :]
    d_w = (WINDOW + lane_w % tb - r_w).astype(np.float32)
    slope_w = slopes[1::2].reshape(G, 1, HPG)[:, :, lane_w[0] // tb]
    wb = jnp.asarray(np.where((d_w >= 0) & (d_w < WINDOW), -slope_w * d_w[None], np.float32(NEG)).astype(np.float32))
    obT = _nsa_attention(slopes_b, nqT, kc, vc.transpose(0, 1, 3, 2), ksa, kv_tiles_T(kv_groups(3), tb_k),
                         kwa, vwT, wb, gT, _cmp_to_sel_T(n_cp, n_sel), tb, tb_k)
    ob = (obT.reshape(B, G, S // tb, NSA_DIM, HPG, tb).transpose(0, 2, 5, 1, 4, 3)
          .reshape(N, NSA_HEADS * NSA_DIM))

    x1, h2, logits = _merge(oa, ob, slab, x2, g1, norm2_w, sh2, sc2, w_da_out.astype(BF16),
                            w_nsa_out.astype(BF16), w_o.astype(BF16), router_w.astype(F32), S)

    rb = 256
    tt = 128
    top_eT, top_wT = _route(logits.T, router_b)
    destT, counts = _slots(top_eT, rb)
    n_rows = ((N * MOE_TOPK + N_EXPERTS * (rb - 1) + rb - 1) // rb) * rb
    padded = (jnp.ceil(counts[:, 0] / rb) * rb).astype(I32)
    pend = jnp.cumsum(padded)
    blk_start = jnp.arange(n_rows // rb, dtype=I32) * rb
    blk_e = jnp.minimum(jnp.sum((pend[None, :] <= blk_start[:, None]).astype(I32), axis=1), N_EXPERTS - 1)
    dest_tiles = destT.reshape(MOE_TOPK, N // tt, tt).transpose(1, 0, 2).reshape(N // tt, 1, MOE_TOPK * tt)
    xs = _dispatch(dest_tiles, h2, n_rows, tt)
    wgu = jnp.concatenate([exp_w_gate, exp_w_up], axis=-1).astype(BF16)
    ys = _experts(blk_e, xs, wgu, exp_w_down.astype(BF16), rb)
    sgu = jnp.concatenate([sh_w_gate, sh_w_up], axis=-1).astype(BF16)
    out = _combine(dest_tiles, ys, top_wT.T, h2, x1, g2, final_norm_w, sgu, sh_w_down.astype(BF16), S, tt)
    return out.reshape(B, S, D)


def kernel(x, c, ada_w, ada_b, norm1_w, w_in, da_lq1, da_lk1, da_lq2, da_lk2, da_subln_w, cmp_k_pe, cmp_k_w1, cmp_k_w2, cmp_v_pe, cmp_v_w1, cmp_v_w2, w_da_out, w_nsa_out, w_o, norm2_w, router_w, router_b, exp_w_gate, exp_w_up, exp_w_down, sh_w_gate, sh_w_up, sh_w_down, final_norm_w):
    depth = ada_w.shape[0]
    assert depth == 1, "one decoder layer"
    lam_init = 0.8 - 0.6 * math.exp(-0.3 * 0)
    return _layer(x, c, ada_w[0], ada_b[0], norm1_w[0], w_in[0], da_lq1[0], da_lk1[0], da_lq2[0], da_lk2[0],
                  da_subln_w[0], cmp_k_pe[0], cmp_k_w1[0], cmp_k_w2[0], cmp_v_pe[0], cmp_v_w1[0], cmp_v_w2[0],
                  w_da_out[0], w_nsa_out[0], w_o[0], norm2_w[0], router_w[0], router_b[0],
                  exp_w_gate[0], exp_w_up[0], exp_w_down[0], sh_w_gate[0], sh_w_up[0], sh_w_down[0],
                  final_norm_w, lam_init)
```

```python
import functools
import math

import numpy as np
import jax
import jax.numpy as jnp
from jax import lax
from jax.experimental import pallas as pl
from jax.experimental.pallas import tpu as pltpu

F32 = jnp.float32
BF16 = jnp.bfloat16
I32 = jnp.int32
U32 = jnp.uint32
HIGHEST = lax.Precision.HIGHEST

NORM_EPS = 1e-6
NEG = -1e30

DA_HEADS = 8
DA_DIM = 64
DA_VDIM = 128
DA_SUBLN_EPS = 1e-5

NSA_HEADS = 8
NSA_GROUPS = 2
NSA_HPG = 4
NSA_DIM = 64
CMP_BLOCK = 32
CMP_STRIDE = 16
SEL_BLOCK = 64
SEL_TOPK = 16
WINDOW = 512
FORCE_BONUS = 1e4
SEL_COLS = 128

N_EXPERTS = 64
MOE_TOPK = 8
N_EXPERT_GROUPS = 8
TOPK_EXPERT_GROUPS = 4
ROUTED_SCALE = 2.5

VMEM_LIMIT_V7X = 56 * 1024 * 1024

COL_DAQ, COL_DAK, COL_DAV = 0, 1024, 2048
COL_MERGE = 3072
COL_NSAQ = 5120
COL_KV6 = 5632
COL_NSAG = 6400
SLAB_COLS = 6528


POS_SPLIT = 16
POS_ROWS = 16
ONES_ROWS = 16


def _pos_features(rows):
    r = np.arange(rows)
    f = np.zeros((rows, 128), np.float32)
    f[:, 0:3] = (r // POS_SPLIT)[:, None]
    f[:, 3:6] = (r % POS_SPLIT)[:, None]
    return jnp.asarray(f, BF16)


def _slope_rows(slopes):
    s = np.asarray(slopes, np.float32)
    bf = lambda x: x.astype(BF16).astype(np.float32)
    p1 = bf(s)
    p2 = bf(s - p1)
    p3 = bf(s - p1 - p2)
    out = np.zeros((s.shape[0], POS_ROWS, 128), np.float32)
    for i, piece in enumerate((p1, p2, p3)):
        out[:, i, :] = POS_SPLIT * piece[:, None]
        out[:, 3 + i, :] = piece[:, None]
    return jnp.asarray(out)


def _with_ones_row(vT):
    shape = vT.shape[:-2] + (ONES_ROWS, vT.shape[-1])
    extra = jnp.zeros(shape, vT.dtype).at[..., 0, :].set(1.0)
    return jnp.concatenate([vT, extra], axis=-2)


def _cparams(sem, vmem=VMEM_LIMIT_V7X):
    return pltpu.CompilerParams(dimension_semantics=sem, vmem_limit_bytes=vmem)


def _ada_kernel(c_ref, w_ref, b_ref, o_ref):
    c = c_ref[...]
    ca = c * jax.nn.sigmoid(c)
    o_ref[...] = jnp.dot(ca, w_ref[...], preferred_element_type=F32, precision=HIGHEST) + b_ref[...]


def _ada(c, w, b):
    B, D = c.shape
    n_out = w.shape[1]
    rows = 8
    cp = jnp.zeros((rows, D), F32).at[:B].set(c)
    tn = 1024
    out = pl.pallas_call(
        _ada_kernel,
        grid=(n_out // tn,),
        in_specs=[pl.BlockSpec((rows, D), lambda j: (0, 0)),
                  pl.BlockSpec((D, tn), lambda j: (0, j)),
                  pl.BlockSpec((1, tn), lambda j: (0, j))],
        out_specs=pl.BlockSpec((rows, tn), lambda j: (0, j)),
        out_shape=jax.ShapeDtypeStruct((rows, n_out), F32),
        compiler_params=_cparams(("arbitrary",)),
        name="ada",
    )(cp, w, b.reshape(1, n_out))
    return out[:B]


def _modulate(x, nw, sh, sc):
    r = lax.rsqrt(jnp.mean(x * x, axis=-1, keepdims=True) + NORM_EPS)
    return (x * r) * nw * (1.0 + sc) + sh


def _inproj_kernel(x_ref, nw_ref, sh_ref, sc_ref, w_ref, o_ref, h_ref):
    @pl.when(pl.program_id(1) == 0)
    def _():
        h_ref[...] = _modulate(x_ref[...], nw_ref[...], sh_ref[...], sc_ref[...]).astype(BF16)

    o_ref[...] = jnp.dot(h_ref[...], w_ref[...], preferred_element_type=F32).astype(BF16)


def _inproj(x2, nw, sh, sc, w_slab, S):
    N, D = x2.shape
    tm = 512
    tn = SLAB_COLS // 3
    per_b = S // tm
    return pl.pallas_call(
        _inproj_kernel,
        grid=(N // tm, SLAB_COLS // tn),
        in_specs=[pl.BlockSpec((tm, D), lambda i, j: (i, 0)),
                  pl.BlockSpec((1, D), lambda i, j: (0, 0)),
                  pl.BlockSpec((None, 1, D), lambda i, j: (i // per_b, 0, 0)),
                  pl.BlockSpec((None, 1, D), lambda i, j: (i // per_b, 0, 0)),
                  pl.BlockSpec((D, tn), lambda i, j: (0, j))],
        out_specs=pl.BlockSpec((tm, tn), lambda i, j: (i, j)),
        out_shape=jax.ShapeDtypeStruct((N, SLAB_COLS), BF16),
        scratch_shapes=[pltpu.VMEM((tm, D), BF16)],
        compiler_params=_cparams(("parallel", "arbitrary")),
        name="inproj",
    )(x2, nw.reshape(1, D), sh.reshape(-1, 1, D), sc.reshape(-1, 1, D), w_slab)


def _cmp_kernel(r_ref, pe_ref, w1a_ref, w1b_ref, w2_ref, o_ref):
    r = r_ref[...]
    n_rows = r.shape[0]
    a = jnp.dot(r, w1a_ref[...], preferred_element_type=F32)
    b = jnp.dot(r, w1b_ref[...], preferred_element_type=F32)
    pe = pe_ref[...]
    half = pe.shape[1] // 2
    bias = (jnp.dot(pe[:, :half], w1a_ref[...].astype(F32), preferred_element_type=F32, precision=HIGHEST)
            + jnp.dot(pe[:, half:], w1b_ref[...].astype(F32), preferred_element_type=F32, precision=HIGHEST))
    hid = a + pltpu.roll(b, n_rows - 1, 0) + bias
    act = 0.5 * hid * (1.0 + jnp.tanh(0.7978845608028654 * (hid + 0.044715 * hid * hid * hid)))
    o_ref[...] = jnp.dot(act, w2_ref[...], preferred_element_type=F32, precision=HIGHEST)


def _compress(rows, pe, w1, w2):
    BG, n_rows, width = rows.shape
    hidden = w1.shape[1]
    w1a = w1[:width].astype(BF16)
    w1b = w1[width:].astype(BF16)
    return pl.pallas_call(
        _cmp_kernel,
        grid=(BG,),
        in_specs=[pl.BlockSpec((None, n_rows, width), lambda i: (i, 0, 0)),
                  pl.BlockSpec((1, 2 * width), lambda i: (0, 0)),
                  pl.BlockSpec((width, hidden), lambda i: (0, 0)),
                  pl.BlockSpec((width, hidden), lambda i: (0, 0)),
                  pl.BlockSpec((hidden, NSA_DIM), lambda i: (0, 0))],
        out_specs=pl.BlockSpec((None, n_rows, NSA_DIM), lambda i: (i, 0, 0)),
        out_shape=jax.ShapeDtypeStruct((BG, n_rows, NSA_DIM), F32),
        compiler_params=_cparams(("parallel",)),
        name="cmp",
    )(rows, pe.reshape(1, -1).astype(F32), w1a, w1b, w2.astype(F32))


def _da_kernel(slopes_ref, qT_ref, k_ref, vT_ref, pos_ref, srow_ref, lq1_ref, lk1_ref, lq2_ref, lk2_ref, subw_ref,
               oT_ref, acc_ref, qa_ref, sa_ref, sb_ref, pa_ref, pb_ref, *, tq, tk, lam_init):
    h = pl.program_id(1)
    qi = pl.program_id(2)
    slope = slopes_ref[h]
    q0 = qi * tq
    w = 2 * tq
    nk = k_ref.shape[0] // tk
    qT = qT_ref[...]
    row = lax.broadcasted_iota(I32, qT.shape, 0)
    zero = jnp.zeros_like(qT)
    qa_ref[0:2 * DA_DIM, :] = jnp.concatenate(
        [jnp.where(row < DA_DIM, qT, zero), jnp.where(row >= DA_DIM, qT, zero)], axis=1)
    qa_ref[2 * DA_DIM:2 * DA_DIM + POS_ROWS, :] = jnp.concatenate([srow_ref[...]] * (w // 128), axis=1).astype(BF16)
    qa_ref[2 * DA_DIM + POS_ROWS:, :] = jnp.zeros((2 * DA_DIM - POS_ROWS, w), BF16)
    key_off = lax.broadcasted_iota(I32, (tk, w), 0)

    def raw_scores(kt):
        k_t = k_ref[pl.ds(pl.multiple_of(kt * tk, tk), tk), :]
        return jnp.dot(jnp.concatenate([k_t, pos_ref[...]], axis=1), qa_ref[...], preferred_element_type=F32)

    def scores(kt, s_ref):
        s = raw_scores(kt)
        s_ref[...] = s
        return jnp.max(s, axis=0, keepdims=True)

    def probs(s_ref, p_ref, mx, off, m_old):
        m_new = jnp.maximum(m_old, mx + off)
        p_ref[...] = jnp.exp(s_ref[...] - (m_new - off)).astype(BF16)
        return m_new, jnp.exp(m_old - m_new)

    def accumulate(kt, p_ref, alpha):
        acc_ref[...] = alpha * acc_ref[...] + jnp.dot(vT_ref[kt], p_ref[...], preferred_element_type=F32)

    n_full = q0 // tk
    mx_a = scores(0, sa_ref)

    kd = n_full * tk
    lane = lax.broadcasted_iota(I32, (tk, w), 1)
    t_loc = jnp.where(lane >= tq, lane - tq, lane)
    s = jnp.where(key_off - t_loc <= q0 - kd, raw_scores(n_full), NEG)
    off_d = slope * kd.astype(F32)
    m = jnp.max(s, axis=0, keepdims=True) + off_d
    acc_ref[...] = jnp.dot(vT_ref[n_full], jnp.exp(s - (m - off_d)).astype(BF16), preferred_element_type=F32)
    pb_ref[...] = jnp.zeros_like(pb_ref)

    def pair(j, carry):
        m, mx_a, alpha_b = carry
        ta = 2 * j
        tb = ta + 1
        off_a = slope * (ta * tk).astype(F32)
        off_b = jnp.where(tb < n_full, slope * (tb * tk).astype(F32), NEG)
        accumulate(jnp.maximum(ta - 1, 0), pb_ref, alpha_b)
        m, alpha_a = probs(sa_ref, pa_ref, mx_a, off_a, m)
        mx_b = scores(jnp.minimum(tb, nk - 1), sb_ref)
        accumulate(ta, pa_ref, alpha_a)
        m, alpha_b = probs(sb_ref, pb_ref, mx_b, off_b, m)
        mx_a = scores(jnp.minimum(ta + 2, nk - 1), sa_ref)
        return m, mx_a, alpha_b

    n_pairs = (n_full + 1) // 2
    _, _, alpha_b = lax.fori_loop(0, n_pairs, pair, (m, mx_a, jnp.ones((1, w), F32)))
    accumulate(jnp.minimum(jnp.maximum(2 * n_pairs - 1, 0), nk - 1), pb_ref, alpha_b)

    lam = (jnp.exp(jnp.sum(lq1_ref[...] * lk1_ref[...], axis=-1, keepdims=True))
           - jnp.exp(jnp.sum(lq2_ref[...] * lk2_ref[...], axis=-1, keepdims=True)) + lam_init)
    on = acc_ref[0:DA_VDIM, :] * (1.0 / acc_ref[DA_VDIM:DA_VDIM + 1, :])
    o = on[:, :tq] - lam * on[:, tq:]
    r = lax.rsqrt(jnp.mean(o * o, axis=0, keepdims=True) + DA_SUBLN_EPS)
    oT_ref[...] = (o * r * subw_ref[...] * (1.0 - lam_init)).astype(BF16)


def _diff_attention(slopes, qT, slab3, vT, srows, lq1, lk1, lq2, lk2, subw, lam_init, tq, tk):
    B, _, S = qT.shape
    nk = S // tk
    assert tk % tq == 0 and S % tk == 0
    tile = tq
    kcol = COL_DAK // 128
    v_rows = vT.shape[3]
    vec = lambda a: a.reshape(1, DA_DIM).astype(F32)
    grid_spec = pltpu.PrefetchScalarGridSpec(
        num_scalar_prefetch=1,
        grid=(B, DA_HEADS, S // tile),
        in_specs=[pl.BlockSpec((None, 128, tile), lambda b, h, i, s: (b, h, i)),
                  pl.BlockSpec((None, S, 128), lambda b, h, i, s: (b, 0, kcol + h)),
                  pl.BlockSpec((None, None, nk, v_rows, tk), lambda b, h, i, s: (b, h, 0, 0, 0)),
                  pl.BlockSpec((tk, 128), lambda b, h, i, s: (0, 0)),
                  pl.BlockSpec((None, POS_ROWS, 128), lambda b, h, i, s: (h, 0, 0)),
                  pl.BlockSpec((1, DA_DIM), lambda b, h, i, s: (0, 0)),
                  pl.BlockSpec((1, DA_DIM), lambda b, h, i, s: (0, 0)),
                  pl.BlockSpec((1, DA_DIM), lambda b, h, i, s: (0, 0)),
                  pl.BlockSpec((1, DA_DIM), lambda b, h, i, s: (0, 0)),
                  pl.BlockSpec((DA_VDIM, 1), lambda b, h, i, s: (0, 0))],
        out_specs=pl.BlockSpec((None, 128, tile), lambda b, h, i, s: (b, h, i)),
        scratch_shapes=[pltpu.VMEM((v_rows, 2 * tq), F32), pltpu.VMEM((4 * DA_DIM, 2 * tq), BF16),
                        pltpu.VMEM((tk, 2 * tq), F32), pltpu.VMEM((tk, 2 * tq), F32),
                        pltpu.VMEM((tk, 2 * tq), BF16), pltpu.VMEM((tk, 2 * tq), BF16)],
    )
    return pl.pallas_call(
        functools.partial(_da_kernel, tq=tq, tk=tk, lam_init=lam_init),
        grid_spec=grid_spec,
        out_shape=jax.ShapeDtypeStruct((B, DA_HEADS * DA_VDIM, S), BF16),
        compiler_params=_cparams(("parallel", "parallel", "arbitrary")),
        name="diffattn",
    )(slopes, qT, slab3, vT, _pos_features(tk), srows, vec(lq1), vec(lk1), vec(lq2), vec(lk2),
      subw.reshape(DA_VDIM, 1).astype(F32))


def _nsa_kernel(slopes_ref, qT_ref, kc_ref, vcT_ref, ksa_ref, vsT_ref, kwa_ref, vwT_ref, wb_ref, gT_ref, mT_ref,
                grp_ref, ltri_ref, oT_ref, acc_ref, res_ref, base_ref, qa_ref, sa_ref, sb_ref, pa_ref, pb_ref,
                list_ref, slist_ref, lsem, *, t, tk, n_top):
    g = pl.program_id(1)
    qi = pl.program_id(2)
    q0 = qi * t
    w = NSA_HPG * t
    n_cp = kc_ref.shape[0]
    n_sel = mT_ref.shape[0]
    lane1 = lax.broadcasted_iota(I32, (1, w), 1)
    slope_row = jnp.zeros((1, w), F32)
    for hh in range(NSA_HPG):
        slope_row = jnp.where((lane1 >= hh * t) & (lane1 < (hh + 1) * t), slopes_ref[g * NSA_HPG + hh], slope_row)
    q_all = qT_ref[...]
    gates = jax.nn.sigmoid(gT_ref[...])
    key_off = lax.broadcasted_iota(I32, (tk, w), 0)
    t_loc = jnp.bitwise_and(lax.broadcasted_iota(I32, (tk, w), 1), t - 1)
    base_ref[...] = slope_row * key_off.astype(F32)

    c_idx = lax.broadcasted_iota(I32, (n_cp, w), 0)
    t_c = q0 + jnp.bitwise_and(lax.broadcasted_iota(I32, (n_cp, w), 1), t - 1)
    d_c = t_c - (c_idx * CMP_STRIDE + (CMP_BLOCK - 1))
    ok_c = d_c >= 0
    s = jnp.dot(kc_ref[...], q_all, preferred_element_type=F32) - slope_row * d_c.astype(F32)
    s = jnp.where(ok_c, s, NEG)
    e = jnp.exp(s - jnp.max(s, axis=0, keepdims=True))
    inv = 1.0 / jnp.sum(e, axis=0, keepdims=True)
    p = jnp.where(ok_c, e * inv, 0.0)
    res_ref[...] = gates[0:1, :] * jnp.dot(vcT_ref[...], p.astype(BF16), preferred_element_type=F32)
    p_sum = p[:, 0:t]
    for hh in range(1, NSA_HPG):
        p_sum = p_sum + p[:, hh * t:(hh + 1) * t]

    p_hi = p_sum.astype(BF16)
    p_lo = (p_sum - p_hi.astype(F32)).astype(BF16)
    mT = mT_ref[...]
    imp = jnp.dot(mT, p_hi, preferred_element_type=F32) + jnp.dot(mT, p_lo, preferred_element_type=F32)
    blk = lax.broadcasted_iota(I32, (n_sel, t), 0)
    t_s = q0 + lax.broadcasted_iota(I32, (n_sel, t), 1)
    cur = jnp.right_shift(t_s, int(math.log2(SEL_BLOCK)))
    forced = (blk == 0) | (blk == cur) | (blk == cur - 1)
    score = jnp.where(blk * SEL_BLOCK <= t_s, imp + jnp.where(forced, FORCE_BONUS, 0.0), NEG)
    blk_f = blk.astype(F32)
    sel = jnp.zeros((n_sel, t), F32)
    for _ in range(n_top):
        _, idx = _first_max(score, blk_f)
        hit = blk_f == idx
        sel = jnp.where(hit, 1.0, sel)
        score = jnp.where(hit, -jnp.inf, score)
    n_full = q0 // tk
    nt_pad = grp_ref.shape[0]
    cnt = jnp.broadcast_to(jnp.sum(sel, axis=1, keepdims=True), (n_sel, 128)).astype(BF16)
    tile_cnt = jnp.dot(grp_ref[...], cnt, preferred_element_type=F32)
    kt_i = lax.broadcasted_iota(I32, (nt_pad, 128), 0)
    active = (tile_cnt > 0.5) & (kt_i < n_full)
    act_f = jnp.where(active, 1.0, 0.0)
    before = jnp.dot(ltri_ref[...], act_f.astype(BF16), preferred_element_type=F32)
    lane_j = lax.broadcasted_iota(I32, (nt_pad, 128), 1)
    slot_hit = active & (before == lane_j.astype(F32))
    tiles_row = jnp.sum(jnp.where(slot_hit, kt_i.astype(F32), 0.0), axis=0, keepdims=True)
    n_act_row = jnp.sum(act_f, axis=0, keepdims=True)
    lane8 = lax.broadcasted_iota(I32, (8, 128), 1)
    list_ref[...] = jnp.where(lane8 == 127, n_act_row, tiles_row).astype(I32)
    list_copy = pltpu.make_async_copy(list_ref, slist_ref, lsem)
    list_copy.start()

    selb = jnp.where(sel > 0.5, 0.0, NEG)
    row64 = lax.broadcasted_iota(I32, (NSA_DIM, w), 0)
    qa_ref[0:NSA_DIM, :] = q_all
    qa_ref[NSA_DIM:2 * NSA_DIM, :] = jnp.where(row64 == 0, NEG, 0.0).astype(BF16)
    qa_ref[2 * NSA_DIM:2 * NSA_DIM + n_sel, :] = jnp.concatenate([selb] * NSA_HPG, axis=1).astype(BF16)
    if n_sel < SEL_COLS:
        qa_ref[2 * NSA_DIM + n_sel:, :] = jnp.zeros((SEL_COLS - n_sel, w), BF16)

    n_wt = (WINDOW + t) // t
    kw_t = kwa_ref[pl.ds(pl.multiple_of(q0, t), WINDOW + t), :]
    s = jnp.dot(kw_t, qa_ref[0:2 * NSA_DIM, :], preferred_element_type=F32) + wb_ref[...]
    p = jnp.exp(s - jnp.max(s, axis=0, keepdims=True))
    inv = 1.0 / jnp.sum(p, axis=0, keepdims=True)
    p = p.astype(BF16)
    o_w = jnp.dot(vwT_ref[qi], p[0:t, :], preferred_element_type=F32)
    for c in range(1, n_wt):
        o_w = o_w + jnp.dot(vwT_ref[qi + c], p[c * t:(c + 1) * t, :], preferred_element_type=F32)
    res_ref[...] = res_ref[...] + gates[2:3, :] * (o_w * inv)

    def scores(kt, s_ref):
        k_t = ksa_ref[pl.ds(pl.multiple_of(kt * tk, tk), tk), :]
        s = jnp.dot(k_t, qa_ref[...], preferred_element_type=F32) + base_ref[...]
        s_ref[...] = s
        return jnp.max(s, axis=0, keepdims=True)

    def probs(s_ref, p_ref, mx, off, m_old, l_old):
        m_new = jnp.maximum(m_old, mx + off)
        p = jnp.exp(s_ref[...] - (m_new - off))
        alpha = jnp.exp(m_old - m_new)
        p_ref[...] = p.astype(BF16)
        return m_new, alpha * l_old + jnp.sum(p, axis=0, keepdims=True), alpha

    def accumulate(kt, p_ref, alpha):
        acc_ref[...] = alpha * acc_ref[...] + jnp.dot(vsT_ref[kt], p_ref[...], preferred_element_type=F32)

    list_copy.wait()
    n_act = slist_ref[0, 127]
    mx_a = scores(slist_ref[0, 0], sa_ref)

    kd = n_full * tk
    s = jnp.dot(ksa_ref[pl.ds(pl.multiple_of(kd, tk), tk), :], qa_ref[...], preferred_element_type=F32) + base_ref[...]
    s = jnp.where(key_off - t_loc <= q0 - kd, s, NEG)
    off_d = slope_row * kd.astype(F32)
    m = jnp.max(s, axis=0, keepdims=True) + off_d
    p = jnp.exp(s - (m - off_d))
    l = jnp.sum(p, axis=0, keepdims=True)
    acc_ref[...] = jnp.dot(vsT_ref[n_full], p.astype(BF16), preferred_element_type=F32)
    pb_ref[...] = jnp.zeros_like(pb_ref)

    def pair(j, carry):
        m, l, mx_a, alpha_b, tb_prev = carry
        ta = slist_ref[0, 2 * j]
        tb = slist_ref[0, 2 * j + 1]
        off_a = slope_row * (ta * tk).astype(F32)
        off_b = jnp.where(2 * j + 1 < n_act, slope_row * (tb * tk).astype(F32), NEG)
        accumulate(tb_prev, pb_ref, alpha_b)
        m, l, alpha_a = probs(sa_ref, pa_ref, mx_a, off_a, m, l)
        mx_b = scores(tb, sb_ref)
        accumulate(ta, pa_ref, alpha_a)
        m, l, alpha_b = probs(sb_ref, pb_ref, mx_b, off_b, m, l)
        mx_a = scores(slist_ref[0, 2 * j + 2], sa_ref)
        return m, l, mx_a, alpha_b, tb

    n_pairs = (n_act + 1) // 2
    m, l, _, alpha_b, tb_last = lax.fori_loop(0, n_pairs, pair, (m, l, mx_a, jnp.ones((1, w), F32), jnp.int32(0)))
    accumulate(tb_last, pb_ref, alpha_b)
    oT_ref[...] = (res_ref[...] + gates[1:2, :] * (acc_ref[...] * (1.0 / l))).astype(BF16)


def _nsa_attention(slopes, qT, kc, vcT, ksa, vsT, kwa, vwT, wb, gT, mT, t, tk):
    B, G, nq, _, w = qT.shape
    S = nq * t
    n_cp = kc.shape[2]
    n_sel = S // SEL_BLOCK
    n_top = min(SEL_TOPK, n_sel)
    assert tk % t == 0 and S % tk == 0 and t & (t - 1) == 0 and WINDOW % t == 0 and n_sel <= SEL_COLS
    n_tiles = S // tk
    nt_pad = -(-n_tiles // 16) * 16
    assert nt_pad + 2 < 127
    grp = jnp.asarray((np.arange(n_sel)[None, :] // (tk // SEL_BLOCK) == np.arange(nt_pad)[:, None])
                      .astype(np.float32), BF16)
    ltri = jnp.asarray((np.arange(nt_pad)[None, :] < np.arange(nt_pad)[:, None]).astype(np.float32), BF16)
    fixed = lambda b, g, i, s: (b, g, 0, 0)
    fixed5 = lambda b, g, i, s: (b, g, 0, 0, 0)
    tile5 = lambda b, g, i, s: (b, g, i, 0, 0)
    grid_spec = pltpu.PrefetchScalarGridSpec(
        num_scalar_prefetch=1,
        grid=(B, G, nq),
        in_specs=[pl.BlockSpec((None, None, None, NSA_DIM, w), tile5),
                  pl.BlockSpec((None, None, n_cp, NSA_DIM), fixed),
                  pl.BlockSpec((None, None, NSA_DIM, n_cp), fixed),
                  pl.BlockSpec((None, None, S, ksa.shape[-1]), fixed),
                  pl.BlockSpec((None, None, S // tk, NSA_DIM, tk), fixed5),
                  pl.BlockSpec((None, None, S + WINDOW, 2 * NSA_DIM), fixed),
                  pl.BlockSpec((None, None, (S + WINDOW) // t, NSA_DIM, t), fixed5),
                  pl.BlockSpec((None, WINDOW + t, w), lambda b, g, i, s: (g, 0, 0)),
                  pl.BlockSpec((None, None, None, 3, w), tile5),
                  pl.BlockSpec((n_sel, n_cp), lambda b, g, i, s: (0, 0)),
                  pl.BlockSpec((nt_pad, n_sel), lambda b, g, i, s: (0, 0)),
                  pl.BlockSpec((nt_pad, nt_pad), lambda b, g, i, s: (0, 0))],
        out_specs=pl.BlockSpec((None, None, None, NSA_DIM, w), tile5),
        scratch_shapes=[pltpu.VMEM((NSA_DIM, w), F32),
                        pltpu.VMEM((NSA_DIM, w), F32),
                        pltpu.VMEM((tk, w), F32),
                        pltpu.VMEM((2 * NSA_DIM + SEL_COLS, w), BF16),
                        pltpu.VMEM((tk, w), F32), pltpu.VMEM((tk, w), F32),
                        pltpu.VMEM((tk, w), BF16), pltpu.VMEM((tk, w), BF16),
                        pltpu.VMEM((8, 128), I32), pltpu.SMEM((8, 128), I32), pltpu.SemaphoreType.DMA(())],
    )
    return pl.pallas_call(
        functools.partial(_nsa_kernel, t=t, tk=tk, n_top=n_top),
        grid_spec=grid_spec,
        out_shape=jax.ShapeDtypeStruct((B, G, nq, NSA_DIM, w), BF16),
        compiler_params=_cparams(("parallel", "parallel", "arbitrary")),
        name="nsa",
    )(slopes, qT, kc, vcT, ksa, vsT, kwa, vwT, wb, gT, mT, grp, ltri)


def _pack_halves(v):
    c = v.shape[1] // 2
    lo = lax.bitcast_convert_type(v[:, :c].astype(BF16).astype(F32), U32)
    hi = lax.bitcast_convert_type(v[:, c:].astype(BF16).astype(F32), U32)
    return hi | (lo >> 16)


def _unpack_halves(u):
    lo = lax.bitcast_convert_type(u << 16, F32)
    hi = lax.bitcast_convert_type(u & jnp.uint32(0xFFFF0000), F32)
    return lo, hi


def _merge_kernel(oa_ref, ob_ref, ga_ref, gb_ref, x_ref, g1_ref, nw_ref, sh_ref, sc_ref,
                  wa_ref, wb_ref, wo_ref, wr_ref, x1_ref, h2_ref, lg_ref):
    ya = jnp.dot(oa_ref[...], wa_ref[...], preferred_element_type=F32)
    yb = jnp.dot(ob_ref[...], wb_ref[...], preferred_element_type=F32)
    merged = (jax.nn.sigmoid(ga_ref[...].astype(F32)) * ya
              + jax.nn.sigmoid(gb_ref[...].astype(F32)) * yb).astype(BF16)
    mix = jnp.dot(merged, wo_ref[...], preferred_element_type=F32)
    x1 = x_ref[...] + g1_ref[...] * mix
    x1_ref[...] = x1
    h2 = _modulate(x1, nw_ref[...], sh_ref[...], sc_ref[...])
    h2_ref[...] = _pack_halves(h2)
    lg_ref[...] = jnp.dot(h2, wr_ref[...], preferred_element_type=F32, precision=HIGHEST)


def _merge(oa, ob, slab, x2, g1, nw, sh, sc, wa, wb, wo, wr, S):
    N, D = x2.shape
    tm = 256
    per_b = S // tm
    mcol = COL_MERGE // D
    row = lambda i: (i, 0)
    full = lambda i: (0, 0)
    perb = lambda i: (i // per_b, 0, 0)
    n_lg = wr.shape[1]
    return pl.pallas_call(
        _merge_kernel,
        grid=(N // tm,),
        in_specs=[pl.BlockSpec((tm, oa.shape[1]), row),
                  pl.BlockSpec((tm, ob.shape[1]), row),
                  pl.BlockSpec((tm, D), lambda i: (i, mcol)),
                  pl.BlockSpec((tm, D), lambda i: (i, mcol + 1)),
                  pl.BlockSpec((tm, D), row),
                  pl.BlockSpec((None, 1, D), perb),
                  pl.BlockSpec((1, D), full),
                  pl.BlockSpec((None, 1, D), perb),
                  pl.BlockSpec((None, 1, D), perb),
                  pl.BlockSpec(wa.shape, full),
                  pl.BlockSpec(wb.shape, full),
                  pl.BlockSpec(wo.shape, full),
                  pl.BlockSpec(wr.shape, full)],
        out_specs=[pl.BlockSpec((tm, D), row), pl.BlockSpec((tm, D // 2), row), pl.BlockSpec((tm, n_lg), row)],
        out_shape=[jax.ShapeDtypeStruct((N, D), F32), jax.ShapeDtypeStruct((N, D // 2), U32),
                   jax.ShapeDtypeStruct((N, n_lg), F32)],
        compiler_params=_cparams(("parallel",)),
        name="merge",
    )(oa, ob, slab, slab, x2, g1.reshape(-1, 1, D), nw.reshape(1, D), sh.reshape(-1, 1, D),
      sc.reshape(-1, 1, D), wa, wb, wo, wr)


def _first_max(v, idx):
    mx = jnp.max(v, axis=0, keepdims=True)
    first = jnp.min(jnp.where(v == mx, idx, float(v.shape[0])), axis=0, keepdims=True)
    return mx, first


def _route_kernel(lg_ref, b_ref, e_ref, w_ref):
    lg = lg_ref[...]
    tn = lg.shape[1]
    per_g = N_EXPERTS // N_EXPERT_GROUPS
    scores = jax.nn.sigmoid(lg)
    biased = scores + b_ref[...]
    gi = lax.broadcasted_iota(I32, (N_EXPERT_GROUPS, tn), 0).astype(F32)
    gscore = jnp.zeros((N_EXPERT_GROUPS, tn), F32)
    for g in range(N_EXPERT_GROUPS):
        grp = biased[g * per_g:(g + 1) * per_g, :]
        m1, f1 = _first_max(grp, gi)
        m2 = jnp.max(jnp.where(gi == f1, -jnp.inf, grp), axis=0, keepdims=True)
        gscore = jnp.where(gi == g, m1 + m2, gscore)
    gsel = jnp.zeros((N_EXPERT_GROUPS, tn), F32)
    for _ in range(TOPK_EXPERT_GROUPS):
        _, f = _first_max(gscore, gi)
        hit = gi == f
        gsel = jnp.where(hit, 1.0, gsel)
        gscore = jnp.where(hit, -jnp.inf, gscore)
    ei = lax.broadcasted_iota(I32, (N_EXPERTS, tn), 0).astype(F32)
    emask = jnp.zeros((N_EXPERTS, tn), F32)
    for g in range(N_EXPERT_GROUPS):
        in_g = (ei >= g * per_g) & (ei < (g + 1) * per_g)
        emask = jnp.where(in_g, gsel[g:g + 1, :], emask)
    masked = jnp.where(emask > 0.5, biased, NEG)
    e_out = jnp.zeros((MOE_TOPK, tn), F32)
    w_out = jnp.zeros((MOE_TOPK, tn), F32)
    for r in range(MOE_TOPK):
        _, f = _first_max(masked, ei)
        hit = ei == f
        wv = jnp.sum(jnp.where(hit, scores, 0.0), axis=0, keepdims=True)
        e_out = jnp.where(gi == r, f, e_out)
        w_out = jnp.where(gi == r, wv, w_out)
        masked = jnp.where(hit, -jnp.inf, masked)
    e_ref[...] = e_out.astype(I32)
    w_ref[...] = w_out / jnp.sum(w_out, axis=0, keepdims=True) * ROUTED_SCALE


def _route(lgT, rb):
    E, N = lgT.shape
    tn = 512
    return pl.pallas_call(
        _route_kernel,
        grid=(N // tn,),
        in_specs=[pl.BlockSpec((E, tn), lambda i: (0, i)), pl.BlockSpec((E, 1), lambda i: (0, 0))],
        out_specs=[pl.BlockSpec((MOE_TOPK, tn), lambda i: (0, i)), pl.BlockSpec((MOE_TOPK, tn), lambda i: (0, i))],
        out_shape=[jax.ShapeDtypeStruct((MOE_TOPK, N), I32), jax.ShapeDtypeStruct((MOE_TOPK, N), F32)],
        compiler_params=_cparams(("parallel",)),
        name="route",
    )(lgT, rb.reshape(E, 1).astype(F32))


def _slots_kernel(e_ref, tri_ref, lt_ref, dest_ref, cnt_ref, carry_ref, *, rb):
    phase = pl.program_id(0)
    i = pl.program_id(1)
    e = e_ref[...]
    tn = e.shape[1]
    ei = lax.broadcasted_iota(I32, (N_EXPERTS, tn), 0)

    @pl.when((phase == 0) & (i == 0))
    def _():
        carry_ref[...] = jnp.zeros_like(carry_ref)

    @pl.when(phase == 0)
    def _():
        tot = jnp.zeros((N_EXPERTS, 1), F32)
        for k in range(MOE_TOPK):
            oh = jnp.where(ei == e[k:k + 1, :], 1.0, 0.0)
            tot = tot + jnp.sum(oh, axis=1, keepdims=True)
        carry_ref[...] = carry_ref[...] + tot
        dest_ref[...] = jnp.zeros_like(dest_ref)

    @pl.when((phase == 1) & (i == 0))
    def _():
        cnt = carry_ref[...]
        cnt_ref[...] = cnt
        padded = jnp.broadcast_to(jnp.ceil(cnt / rb) * rb, (N_EXPERTS, 128))
        first = jnp.dot(lt_ref[...], padded, preferred_element_type=F32, precision=HIGHEST)
        carry_ref[...] = first[:, :1]

    @pl.when(phase == 1)
    def _():
        carry = carry_ref[...]
        ki = lax.broadcasted_iota(I32, (MOE_TOPK, tn), 0)
        dest = jnp.zeros((MOE_TOPK, tn), F32)
        for k in range(MOE_TOPK):
            hit = ei == e[k:k + 1, :]
            oh = jnp.where(hit, 1.0, 0.0)
            before = jnp.dot(oh.astype(BF16), tri_ref[...], preferred_element_type=F32)
            row = jnp.sum(jnp.where(hit, before + carry, 0.0), axis=0, keepdims=True)
            dest = jnp.where(ki == k, row, dest)
            carry = carry + jnp.sum(oh, axis=1, keepdims=True)
        carry_ref[...] = carry
        dest_ref[...] = dest.astype(I32)


def _slots(top_eT, rb):
    K, N = top_eT.shape
    tn = 512
    idx = np.arange(tn)
    tri = jnp.asarray((idx[:, None] < idx[None, :]).astype(np.float32), BF16)
    e_idx = np.arange(N_EXPERTS)
    lt = jnp.asarray((e_idx[None, :] < e_idx[:, None]).astype(np.float32), F32)
    return pl.pallas_call(
        functools.partial(_slots_kernel, rb=rb),
        grid=(2, N // tn),
        in_specs=[pl.BlockSpec((K, tn), lambda p, i: (0, i)),
                  pl.BlockSpec((tn, tn), lambda p, i: (0, 0)),
                  pl.BlockSpec((N_EXPERTS, N_EXPERTS), lambda p, i: (0, 0))],
        out_specs=[pl.BlockSpec((K, tn), lambda p, i: (0, i * p)),
                   pl.BlockSpec((N_EXPERTS, 1), lambda p, i: (0, 0))],
        out_shape=[jax.ShapeDtypeStruct((K, N), I32), jax.ShapeDtypeStruct((N_EXPERTS, 1), F32)],
        scratch_shapes=[pltpu.VMEM((N_EXPERTS, 1), F32)],
        compiler_params=_cparams(("arbitrary", "arbitrary")),
        name="slots",
    )(top_eT, tri, lt)


def _dispatch_kernel(dest_ref, h_ref, xs_in_ref, xs_ref, sem, *, tt):
    del xs_in_ref

    def body(n, c):
        src = h_ref.at[pl.ds(n, 1), :]
        for k in range(MOE_TOPK):
            d = dest_ref[0, 0, k * tt + n]
            pltpu.make_async_copy(src, xs_ref.at[pl.ds(d, 1), :], sem).start(priority=k % 2)
        return c

    lax.fori_loop(0, tt, body, 0)
    for k in range(MOE_TOPK):
        pltpu.make_async_copy(h_ref, xs_ref.at[pl.ds(0, tt), :], sem).wait()


def _dispatch(dest_tiles, h2, n_rows, tt):
    N, D = h2.shape
    xs0 = jnp.zeros((n_rows, D), h2.dtype)
    return pl.pallas_call(
        functools.partial(_dispatch_kernel, tt=tt),
        grid=(N // tt,),
        in_specs=[pl.BlockSpec((1, 1, MOE_TOPK * tt), lambda i: (i, 0, 0), memory_space=pltpu.SMEM),
                  pl.BlockSpec((tt, D), lambda i: (i, 0)),
                  pl.BlockSpec(memory_space=pl.ANY)],
        out_specs=pl.BlockSpec(memory_space=pl.ANY),
        out_shape=jax.ShapeDtypeStruct((n_rows, D), h2.dtype),
        scratch_shapes=[pltpu.SemaphoreType.DMA(())],
        input_output_aliases={2: 0},
        compiler_params=_cparams(("arbitrary",)),
        name="dispatch",
    )(dest_tiles, h2, xs0)


def _experts_kernel(be_ref, x_ref, wgu_ref, wd_ref, y_ref):
    del be_ref
    lo, hi = _unpack_halves(x_ref[...])
    x = jnp.concatenate([lo.astype(BF16), hi.astype(BF16)], axis=1)
    gu = jnp.dot(x, wgu_ref[...], preferred_element_type=F32)
    hdim = gu.shape[1] // 2
    g = gu[:, :hdim]
    act = (g * jax.nn.sigmoid(g) * gu[:, hdim:]).astype(BF16)
    y_ref[...] = _pack_halves(jnp.dot(act, wd_ref[...], preferred_element_type=F32))


def _experts(blk_e, xs, wgu, wd, rb):
    n_rows, D = xs.shape
    grid_spec = pltpu.PrefetchScalarGridSpec(
        num_scalar_prefetch=1,
        grid=(n_rows // rb,),
        in_specs=[pl.BlockSpec((rb, D), lambda i, be: (i, 0)),
                  pl.BlockSpec((None,) + wgu.shape[1:], lambda i, be: (be[i], 0, 0)),
                  pl.BlockSpec((None,) + wd.shape[1:], lambda i, be: (be[i], 0, 0))],
        out_specs=pl.BlockSpec((rb, D), lambda i, be: (i, 0)),
    )
    return pl.pallas_call(
        _experts_kernel,
        grid_spec=grid_spec,
        out_shape=jax.ShapeDtypeStruct((n_rows, D), U32),
        compiler_params=_cparams(("arbitrary",)),
        name="experts",
    )(blk_e, xs, wgu, wd)


def _combine_kernel(dest_ref, ys_ref, w_ref, h_ref, x1_ref, g2_ref, fw_ref, sgu_ref, sd_ref, o_ref, buf_ref, sem, *, tt):
    def body(n, c):
        for k in range(MOE_TOPK):
            d = dest_ref[0, 0, k * tt + n]
            pltpu.make_async_copy(ys_ref.at[pl.ds(d, 1), :], buf_ref.at[k, pl.ds(n, 1), :], sem).start(priority=k % 2)
        return c

    lax.fori_loop(0, tt, body, 0)

    h_lo, h_hi = _unpack_halves(h_ref[...])
    h = jnp.concatenate([h_lo.astype(BF16), h_hi.astype(BF16)], axis=1)
    gu = jnp.dot(h, sgu_ref[...], preferred_element_type=F32)
    hdim = gu.shape[1] // 2
    g = gu[:, :hdim]
    ffn = jnp.dot((g * jax.nn.sigmoid(g) * gu[:, hdim:]).astype(BF16), sd_ref[...], preferred_element_type=F32)

    for k in range(MOE_TOPK):
        pltpu.make_async_copy(ys_ref.at[pl.ds(0, tt), :], buf_ref.at[k], sem).wait()
    w = w_ref[...]
    r_lo = jnp.zeros(h_lo.shape, F32)
    r_hi = jnp.zeros(h_hi.shape, F32)
    for k in range(MOE_TOPK):
        y_lo, y_hi = _unpack_halves(buf_ref[k])
        r_lo = r_lo + w[:, k:k + 1] * y_lo
        r_hi = r_hi + w[:, k:k + 1] * y_hi
    ffn = ffn + jnp.concatenate([r_lo, r_hi], axis=1)
    x2 = x1_ref[...] + g2_ref[...] * ffn
    r = lax.rsqrt(jnp.mean(x2 * x2, axis=-1, keepdims=True) + NORM_EPS)
    o_ref[...] = x2 * r * fw_ref[...]


def _combine(dest_tiles, ys, top_w, h2p, x1, g2, fw, sgu, sd, S, tt):
    N, D = x1.shape
    per_b = S // tt
    row = lambda i: (i, 0)
    full = lambda i: (0, 0)
    return pl.pallas_call(
        functools.partial(_combine_kernel, tt=tt),
        grid=(N // tt,),
        in_specs=[pl.BlockSpec((1, 1, MOE_TOPK * tt), lambda i: (i, 0, 0), memory_space=pltpu.SMEM),
                  pl.BlockSpec(memory_space=pl.ANY),
                  pl.BlockSpec((tt, MOE_TOPK), row),
                  pl.BlockSpec((tt, D // 2), row),
                  pl.BlockSpec((tt, D), row),
                  pl.BlockSpec((None, 1, D), lambda i: (i // per_b, 0, 0)),
                  pl.BlockSpec((1, D), full),
                  pl.BlockSpec(sgu.shape, full),
                  pl.BlockSpec(sd.shape, full)],
        out_specs=pl.BlockSpec((tt, D), row),
        out_shape=jax.ShapeDtypeStruct((N, D), F32),
        scratch_shapes=[pltpu.VMEM((MOE_TOPK, tt, D // 2), U32), pltpu.SemaphoreType.DMA(())],
        compiler_params=_cparams(("arbitrary",)),
        name="combine",
    )(dest_tiles, ys, top_w, h2p, x1, g2.reshape(-1, 1, D), fw.reshape(1, D), sgu, sd)


def _cmp_to_sel_T(n_cp, n_sel):
    c0 = np.arange(n_cp)[None, :] * CMP_STRIDE
    s0 = np.arange(n_sel)[:, None] * SEL_BLOCK
    ov = np.minimum(c0 + CMP_BLOCK, s0 + SEL_BLOCK) - np.maximum(c0, s0)
    m = np.clip(ov, 0, None).astype(np.float32) / CMP_BLOCK
    m[:, n_cp - 1] = 0.0
    return jnp.asarray(m, BF16)


def _slab_weights(w_in):
    D = w_in.shape[0]
    sizes = [1024, 1024, 1024, 512, 128, 128, 128, 128, 128, 128, 24, 2048]
    offs = np.concatenate([[0], np.cumsum(sizes)])
    part = lambda i: w_in[:, offs[i]:offs[i + 1]]
    pieces = [part(0), part(1), part(2), part(11), part(3)] + [part(i) for i in range(4, 10)] + [part(10)]
    w = jnp.concatenate(pieces, axis=1)
    return jnp.pad(w, ((0, 0), (0, SLAB_COLS - w.shape[1]))).astype(BF16)


def _layer(x, c, ada_w, ada_b, norm1_w, w_in, da_lq1, da_lk1, da_lq2, da_lk2, da_subln_w,
           cmp_k_pe, cmp_k_w1, cmp_k_w2, cmp_v_pe, cmp_v_w1, cmp_v_w2, w_da_out, w_nsa_out, w_o,
           norm2_w, router_w, router_b, exp_w_gate, exp_w_up, exp_w_down,
           sh_w_gate, sh_w_up, sh_w_down, final_norm_w, lam_init):
    B, S, D = x.shape
    N = B * S
    G, HPG = NSA_GROUPS, NSA_HPG
    x2 = x.reshape(N, D)

    mod = _ada(c, ada_w, ada_b)
    sh1, sc1, g1, sh2, sc2, g2 = jnp.split(mod, 6, axis=-1)

    slab = _inproj(x2, norm1_w, sh1, sc1, _slab_weights(w_in), S)
    slab3 = slab.reshape(B, S, SLAB_COLS)

    i_all = np.arange(1, DA_HEADS + NSA_HEADS + 1, dtype=np.float32)
    slopes = (2.0 ** (-8.0 * i_all / (DA_HEADS + NSA_HEADS))).astype(np.float32)
    slopes_a = jnp.asarray(slopes[0::2])
    slopes_b = jnp.asarray(slopes[1::2])

    ta_q, ta_k = 256, min(512, S)
    qT = (slab3[:, :, COL_DAQ:COL_DAQ + 1024] * jnp.asarray(DA_DIM ** -0.5, BF16)).transpose(0, 2, 1)
    vT = slab3[:, :, COL_DAV:COL_DAV + 1024].reshape(B, S // ta_k, ta_k, DA_HEADS, DA_VDIM).transpose(0, 3, 1, 4, 2)
    oaT = _diff_attention(slopes_a, qT, slab3, _with_ones_row(vT), _slope_rows(slopes[0::2]), da_lq1, da_lk1,
                          da_lq2, da_lk2, da_subln_w, lam_init, ta_q, ta_k)
    oa = oaT.transpose(0, 2, 1).reshape(N, DA_HEADS * DA_VDIM)

    tb, tb_k = 128, min(256, S)
    n_cp = S // CMP_STRIDE
    n_sel = S // SEL_BLOCK

    def kv_groups(j):
        c0 = COL_KV6 + 128 * j
        return slab3[:, :, c0:c0 + 128].reshape(B, S, G, NSA_DIM).transpose(0, 2, 1, 3)

    def kv_tiles_T(a, rows):
        return a.reshape(B, G, S // rows, rows, NSA_DIM).transpose(0, 1, 2, 4, 3)

    ck_rows = kv_groups(0).reshape(B * G, n_cp, CMP_STRIDE * NSA_DIM)
    cv_rows = kv_groups(1).reshape(B * G, n_cp, CMP_STRIDE * NSA_DIM)
    kc = _compress(ck_rows, cmp_k_pe, cmp_k_w1, cmp_k_w2).reshape(B, G, n_cp, NSA_DIM).astype(BF16)
    vc = _compress(cv_rows, cmp_v_pe, cmp_v_w1, cmp_v_w2).reshape(B, G, n_cp, NSA_DIM).astype(BF16)
    nq = (slab3[:, :, COL_NSAQ:COL_NSAQ + 512] * jnp.asarray(NSA_DIM ** -0.5, BF16))
    nqT = nq.reshape(B, S // tb, tb, G, HPG, NSA_DIM).transpose(0, 3, 1, 5, 4, 2).reshape(B, G, S // tb, NSA_DIM, HPG * tb)
    gT = (slab3[:, :, COL_NSAG:COL_NSAG + 3 * NSA_HEADS].astype(F32).reshape(B, S // tb, tb, G, HPG, 3)
          .transpose(0, 3, 1, 5, 4, 2).reshape(B, G, S // tb, 3, HPG * tb))
    onehot = jnp.asarray((np.arange(S)[:, None] // SEL_BLOCK == np.arange(SEL_COLS)[None, :]).astype(np.float32), BF16)
    zeros64 = jnp.zeros((B, G, S, NSA_DIM), BF16)
    ksa = jnp.concatenate([kv_groups(2), zeros64, jnp.broadcast_to(onehot, (B, G, S, SEL_COLS))], axis=-1)
    pad_rows = jnp.zeros((B, G, WINDOW, 2 * NSA_DIM), BF16).at[..., NSA_DIM].set(1.0)
    kwa = jnp.concatenate([pad_rows, jnp.concatenate([kv_groups(4), zeros64], axis=-1)], axis=2)
    vw_pad = jnp.pad(kv_groups(5), ((0, 0), (0, 0), (WINDOW, 0), (0, 0)))
    vwT = vw_pad.reshape(B, G, (S + WINDOW) // tb, tb, NSA_DIM).transpose(0, 1, 2, 4, 3)
    r_w = np.arange(WINDOW + tb)[:, None]
    lane_w = np.arange(HPG * tb)[None, :]
    d_w = (WINDOW + lane_w % tb - r_w).astype(np.float32)
    slope_w = slopes[1::2].reshape(G, 1, HPG)[:, :, lane_w[0] // tb]
    wb = jnp.asarray(np.where((d_w >= 0) & (d_w < WINDOW), -slope_w * d_w[None], np.float32(NEG)).astype(np.float32))
    obT = _nsa_attention(slopes_b, nqT, kc, vc.transpose(0, 1, 3, 2), ksa, kv_tiles_T(kv_groups(3), tb_k),
                         kwa, vwT, wb, gT, _cmp_to_sel_T(n_cp, n_sel), tb, tb_k)
    ob = (obT.reshape(B, G, S // tb, NSA_DIM, HPG, tb).transpose(0, 2, 5, 1, 4, 3)
          .reshape(N, NSA_HEADS * NSA_DIM))

    x1, h2, logits = _merge(oa, ob, slab, x2, g1, norm2_w, sh2, sc2, w_da_out.astype(BF16),
                            w_nsa_out.astype(BF16), w_o.astype(BF16), router_w.astype(F32), S)

    rb = 512
    tt = 128
    top_eT, top_wT = _route(logits.T, router_b)
    destT, counts = _slots(top_eT, rb)
    n_rows = ((N * MOE_TOPK + N_EXPERTS * (rb - 1) + rb - 1) // rb) * rb
    padded = (jnp.ceil(counts[:, 0] / rb) * rb).astype(I32)
    pend = jnp.cumsum(padded)
    blk_start = jnp.arange(n_rows // rb, dtype=I32) * rb
    blk_e = jnp.minimum(jnp.sum((pend[None, :] <= blk_start[:, None]).astype(I32), axis=1), N_EXPERTS - 1)
    dest_tiles = destT.reshape(MOE_TOPK, N // tt, tt).transpose(1, 0, 2).reshape(N // tt, 1, MOE_TOPK * tt)
    xs = _dispatch(dest_tiles, h2, n_rows, tt)
    wgu = jnp.concatenate([exp_w_gate, exp_w_up], axis=-1).astype(BF16)
    ys = _experts(blk_e, xs, wgu, exp_w_down.astype(BF16), rb)
    sgu = jnp.concatenate([sh_w_gate, sh_w_up], axis=-1).astype(BF16)
    out = _combine(dest_tiles, ys, top_wT.T, h2, x1, g2, final_norm_w, sgu, sh_w_down.astype(BF16), S, tt)
    return out.reshape(B, S, D)


def kernel(x, c, ada_w, ada_b, norm1_w, w_in, da_lq1, da_lk1, da_lq2, da_lk2, da_subln_w, cmp_k_pe, cmp_k_w1, cmp_k_w2, cmp_v_pe, cmp_v_w1, cmp_v_w2, w_da_out, w_nsa_out, w_o, norm2_w, router_w, router_b, exp_w_gate, exp_w_up, exp_w_down, sh_w_gate, sh_w_up, sh_w_down, final_norm_w):
    depth = ada_w.shape[0]
    assert depth == 1, "one decoder layer"
    lam_init = 0.8 - 0.6 * math.exp(-0.3 * 0)
    return _layer(x, c, ada_w[0], ada_b[0], norm1_w[0], w_in[0], da_lq1[0], da_lk1[0], da_lq2[0], da_lk2[0],
                  da_subln_w[0], cmp_k_pe[0], cmp_k_w1[0], cmp_k_w2[0], cmp_v_pe[0], cmp_v_w1[0], cmp_v_w2[0],
                  w_da_out[0], w_nsa_out[0], w_o[0], norm2_w[0], router_w[0], router_b[0],
                  exp_w_gate[0], exp_w_up[0], exp_w_down[0], sh_w_gate[0], sh_w_up[0], sh_w_down[0],
                  final_norm_w, lam_init)
```

```python
import functools
import math

import numpy as np
import jax
import jax.numpy as jnp
from jax import lax
from jax.experimental import pallas as pl
from jax.experimental.pallas import tpu as pltpu

F32 = jnp.float32
BF16 = jnp.bfloat16
I32 = jnp.int32
U32 = jnp.uint32
HIGHEST = lax.Precision.HIGHEST

NORM_EPS = 1e-6
NEG = -1e30

DA_HEADS = 8
DA_DIM = 64
DA_VDIM = 128
DA_SUBLN_EPS = 1e-5

NSA_HEADS = 8
NSA_GROUPS = 2
NSA_HPG = 4
NSA_DIM = 64
CMP_BLOCK = 32
CMP_STRIDE = 16
SEL_BLOCK = 64
SEL_TOPK = 16
WINDOW = 512
FORCE_BONUS = 1e4
SEL_COLS = 128

N_EXPERTS = 64
MOE_TOPK = 8
N_EXPERT_GROUPS = 8
TOPK_EXPERT_GROUPS = 4
ROUTED_SCALE = 2.5

VMEM_LIMIT_V7X = 56 * 1024 * 1024

COL_DAQ, COL_DAK, COL_DAV = 0, 1024, 2048
COL_MERGE = 3072
COL_NSAQ = 5120
COL_KV6 = 5632
COL_NSAG = 6400
SLAB_COLS = 6528


POS_SPLIT = 16
POS_ROWS = 16
ONES_ROWS = 16


def _pos_features(rows):
    r = np.arange(rows)
    f = np.zeros((rows, 128), np.float32)
    f[:, 0:3] = (r // POS_SPLIT)[:, None]
    f[:, 3:6] = (r % POS_SPLIT)[:, None]
    return jnp.asarray(f, BF16)


def _slope_rows(slopes):
    s = np.asarray(slopes, np.float32)
    bf = lambda x: x.astype(BF16).astype(np.float32)
    p1 = bf(s)
    p2 = bf(s - p1)
    p3 = bf(s - p1 - p2)
    out = np.zeros((s.shape[0], POS_ROWS, 128), np.float32)
    for i, piece in enumerate((p1, p2, p3)):
        out[:, i, :] = POS_SPLIT * piece[:, None]
        out[:, 3 + i, :] = piece[:, None]
    return jnp.asarray(out)


def _with_ones_row(vT):
    shape = vT.shape[:-2] + (ONES_ROWS, vT.shape[-1])
    extra = jnp.zeros(shape, vT.dtype).at[..., 0, :].set(1.0)
    return jnp.concatenate([vT, extra], axis=-2)


def _cparams(sem, vmem=VMEM_LIMIT_V7X):
    return pltpu.CompilerParams(dimension_semantics=sem, vmem_limit_bytes=vmem)


def _ada_kernel(c_ref, w_ref, b_ref, o_ref):
    c = c_ref[...]
    ca = c * jax.nn.sigmoid(c)
    o_ref[...] = jnp.dot(ca, w_ref[...], preferred_element_type=F32, precision=HIGHEST) + b_ref[...]


def _ada(c, w, b):
    B, D = c.shape
    n_out = w.shape[1]
    rows = 8
    cp = jnp.zeros((rows, D), F32).at[:B].set(c)
    tn = 1024
    out = pl.pallas_call(
        _ada_kernel,
        grid=(n_out // tn,),
        in_specs=[pl.BlockSpec((rows, D), lambda j: (0, 0)),
                  pl.BlockSpec((D, tn), lambda j: (0, j)),
                  pl.BlockSpec((1, tn), lambda j: (0, j))],
        out_specs=pl.BlockSpec((rows, tn), lambda j: (0, j)),
        out_shape=jax.ShapeDtypeStruct((rows, n_out), F32),
        compiler_params=_cparams(("arbitrary",)),
        name="ada",
    )(cp, w, b.reshape(1, n_out))
    return out[:B]


def _modulate(x, nw, sh, sc):
    r = lax.rsqrt(jnp.mean(x * x, axis=-1, keepdims=True) + NORM_EPS)
    return (x * r) * nw * (1.0 + sc) + sh


def _inproj_kernel(x_ref, nw_ref, sh_ref, sc_ref, w_ref, o_ref, h_ref):
    @pl.when(pl.program_id(1) == 0)
    def _():
        h_ref[...] = _modulate(x_ref[...], nw_ref[...], sh_ref[...], sc_ref[...]).astype(BF16)

    o_ref[...] = jnp.dot(h_ref[...], w_ref[...], preferred_element_type=F32).astype(BF16)


def _inproj(x2, nw, sh, sc, w_slab, S):
    N, D = x2.shape
    tm = 512
    tn = SLAB_COLS // 3
    per_b = S // tm
    return pl.pallas_call(
        _inproj_kernel,
        grid=(N // tm, SLAB_COLS // tn),
        in_specs=[pl.BlockSpec((tm, D), lambda i, j: (i, 0)),
                  pl.BlockSpec((1, D), lambda i, j: (0, 0)),
                  pl.BlockSpec((None, 1, D), lambda i, j: (i // per_b, 0, 0)),
                  pl.BlockSpec((None, 1, D), lambda i, j: (i // per_b, 0, 0)),
                  pl.BlockSpec((D, tn), lambda i, j: (0, j))],
        out_specs=pl.BlockSpec((tm, tn), lambda i, j: (i, j)),
        out_shape=jax.ShapeDtypeStruct((N, SLAB_COLS), BF16),
        scratch_shapes=[pltpu.VMEM((tm, D), BF16)],
        compiler_params=_cparams(("parallel", "arbitrary")),
        name="inproj",
    )(x2, nw.reshape(1, D), sh.reshape(-1, 1, D), sc.reshape(-1, 1, D), w_slab)


def _cmp_kernel(r_ref, pe_ref, w1a_ref, w1b_ref, w2_ref, o_ref):
    r = r_ref[...]
    n_rows = r.shape[0]
    a = jnp.dot(r, w1a_ref[...], preferred_element_type=F32)
    b = jnp.dot(r, w1b_ref[...], preferred_element_type=F32)
    pe = pe_ref[...]
    half = pe.shape[1] // 2
    bias = (jnp.dot(pe[:, :half], w1a_ref[...].astype(F32), preferred_element_type=F32, precision=HIGHEST)
            + jnp.dot(pe[:, half:], w1b_ref[...].astype(F32), preferred_element_type=F32, precision=HIGHEST))
    hid = a + pltpu.roll(b, n_rows - 1, 0) + bias
    act = 0.5 * hid * (1.0 + jnp.tanh(0.7978845608028654 * (hid + 0.044715 * hid * hid * hid)))
    o_ref[...] = jnp.dot(act, w2_ref[...], preferred_element_type=F32, precision=HIGHEST)


def _compress(rows, pe, w1, w2):
    BG, n_rows, width = rows.shape
    hidden = w1.shape[1]
    w1a = w1[:width].astype(BF16)
    w1b = w1[width:].astype(BF16)
    return pl.pallas_call(
        _cmp_kernel,
        grid=(BG,),
        in_specs=[pl.BlockSpec((None, n_rows, width), lambda i: (i, 0, 0)),
                  pl.BlockSpec((1, 2 * width), lambda i: (0, 0)),
                  pl.BlockSpec((width, hidden), lambda i: (0, 0)),
                  pl.BlockSpec((width, hidden), lambda i: (0, 0)),
                  pl.BlockSpec((hidden, NSA_DIM), lambda i: (0, 0))],
        out_specs=pl.BlockSpec((None, n_rows, NSA_DIM), lambda i: (i, 0, 0)),
        out_shape=jax.ShapeDtypeStruct((BG, n_rows, NSA_DIM), F32),
        compiler_params=_cparams(("parallel",)),
        name="cmp",
    )(rows, pe.reshape(1, -1).astype(F32), w1a, w1b, w2.astype(F32))


def _da_kernel(slopes_ref, kmin_ref, qT_ref, k_ref, vT_ref, pos_ref, srow_ref, lq1_ref, lk1_ref, lq2_ref, lk2_ref, subw_ref,
               oT_ref, acc_ref, qa_ref, sa_ref, sb_ref, pa_ref, pb_ref, rel_ref, *, tq, tk, lam_init):
    h = pl.program_id(1)
    qi = pl.program_id(2)
    slope = slopes_ref[h]
    q0 = qi * tq
    w = 2 * tq
    nk = k_ref.shape[0] // tk
    qT = qT_ref[...]
    row = lax.broadcasted_iota(I32, qT.shape, 0)
    zero = jnp.zeros_like(qT)
    qa_ref[0:2 * DA_DIM, :] = jnp.concatenate(
        [jnp.where(row < DA_DIM, qT, zero), jnp.where(row >= DA_DIM, qT, zero)], axis=1)
    qa_ref[2 * DA_DIM:2 * DA_DIM + POS_ROWS, :] = jnp.concatenate([srow_ref[...]] * (w // 128), axis=1).astype(BF16)
    qa_ref[2 * DA_DIM + POS_ROWS:, :] = jnp.zeros((2 * DA_DIM - POS_ROWS, w), BF16)
    lane = lax.broadcasted_iota(I32, (tk, w), 1)
    rel_ref[...] = lax.broadcasted_iota(I32, (tk, w), 0) - jnp.where(lane >= tq, lane - tq, lane)

    def scores(kt, s_ref):
        k0 = kt * tk
        k_t = k_ref[pl.ds(pl.multiple_of(k0, tk), tk), :]
        s = jnp.dot(jnp.concatenate([k_t, pos_ref[...]], axis=1), qa_ref[...], preferred_element_type=F32)
        s = jnp.where(rel_ref[...] <= q0 - k0, s, NEG)
        s_ref[...] = s
        return jnp.max(s, axis=0, keepdims=True)

    def probs(s_ref, p_ref, mx, off, m_old):
        m_new = jnp.maximum(m_old, mx + off)
        p_ref[...] = jnp.exp(s_ref[...] - (m_new - off)).astype(BF16)
        return m_new, jnp.exp(m_old - m_new)

    def accumulate(kt, p_ref, alpha):
        acc_ref[...] = alpha * acc_ref[...] + jnp.dot(vT_ref[kt], p_ref[...], preferred_element_type=F32)

    n_tiles = (q0 + tq - 1) // tk + 1
    kt_min = kmin_ref[(pl.program_id(0) * pl.num_programs(1) + h) * pl.num_programs(2) + qi]
    acc_ref[...] = jnp.zeros_like(acc_ref)
    pb_ref[...] = jnp.zeros_like(pb_ref)
    mx_a = scores(kt_min, sa_ref)

    def pair(j, carry):
        m, mx_a, alpha_b = carry
        ta = kt_min + 2 * j
        tb = ta + 1
        off_a = slope * (ta * tk).astype(F32)
        off_b = jnp.where(tb < n_tiles, slope * (tb * tk).astype(F32), NEG)
        accumulate(jnp.maximum(ta - 1, 0), pb_ref, alpha_b)
        m, alpha_a = probs(sa_ref, pa_ref, mx_a, off_a, m)
        mx_b = scores(jnp.minimum(tb, nk - 1), sb_ref)
        accumulate(ta, pa_ref, alpha_a)
        m, alpha_b = probs(sb_ref, pb_ref, mx_b, off_b, m)
        mx_a = scores(jnp.minimum(ta + 2, nk - 1), sa_ref)
        return m, mx_a, alpha_b

    n_pairs = (n_tiles - kt_min + 1) // 2
    init = (jnp.full((1, w), NEG, F32), mx_a, jnp.ones((1, w), F32))
    _, _, alpha_b = lax.fori_loop(0, n_pairs, pair, init)
    accumulate(jnp.minimum(kt_min + 2 * n_pairs - 1, nk - 1), pb_ref, alpha_b)

    lam = (jnp.exp(jnp.sum(lq1_ref[...] * lk1_ref[...], axis=-1, keepdims=True))
           - jnp.exp(jnp.sum(lq2_ref[...] * lk2_ref[...], axis=-1, keepdims=True)) + lam_init)
    on = acc_ref[0:DA_VDIM, :] * (1.0 / acc_ref[DA_VDIM:DA_VDIM + 1, :])
    o = on[:, :tq] - lam * on[:, tq:]
    r = lax.rsqrt(jnp.mean(o * o, axis=0, keepdims=True) + DA_SUBLN_EPS)
    oT_ref[...] = (o * r * subw_ref[...] * (1.0 - lam_init)).astype(BF16)


EXP_UNDERFLOW = 104.0


def _first_live_tile(slopes_np, qT, slab3, tq, tk):
    B, _, S = qT.shape
    H = DA_HEADS
    qn = jnp.sqrt(jnp.sum(jnp.square(qT.astype(F32).reshape(B, H, 2, DA_DIM, S)), axis=3))
    qn = jnp.max(qn.reshape(B, H, 2, S // tq, tq), axis=-1)
    k = slab3[:, :, COL_DAK:COL_DAK + 2 * DA_DIM * H].astype(F32).reshape(B, S, H, 2, DA_DIM)
    kn = jnp.max(jnp.sqrt(jnp.sum(jnp.square(k), axis=-1)), axis=1)
    bound = jnp.max(qn * kn[..., None], axis=2) * 1.01 + 0.01
    q0 = (jnp.arange(S // tq, dtype=F32) * tq)[None, None, :]
    slope = jnp.asarray(slopes_np, F32)[None, :, None]
    first_key = q0 - (EXP_UNDERFLOW + 2.0 * bound) / slope
    kt = jnp.floor(first_key / tk).astype(I32)
    n_tiles = (jnp.arange(S // tq, dtype=I32) * tq + tq - 1) // tk + 1
    return jnp.clip(kt, 0, n_tiles[None, None, :] - 1).reshape(-1)


def _diff_attention(slopes, kmin, qT, slab3, vT, srows, lq1, lk1, lq2, lk2, subw, lam_init, tq, tk):
    B, _, S = qT.shape
    nk = S // tk
    assert tk % tq == 0 and S % tk == 0
    tile = tq
    kcol = COL_DAK // 128
    v_rows = vT.shape[3]
    vec = lambda a: a.reshape(1, DA_DIM).astype(F32)
    grid_spec = pltpu.PrefetchScalarGridSpec(
        num_scalar_prefetch=2,
        grid=(B, DA_HEADS, S // tile),
        in_specs=[pl.BlockSpec((None, 128, tile), lambda b, h, i, s, km: (b, h, i)),
                  pl.BlockSpec((None, S, 128), lambda b, h, i, s, km: (b, 0, kcol + h)),
                  pl.BlockSpec((None, None, nk, v_rows, tk), lambda b, h, i, s, km: (b, h, 0, 0, 0)),
                  pl.BlockSpec((tk, 128), lambda b, h, i, s, km: (0, 0)),
                  pl.BlockSpec((None, POS_ROWS, 128), lambda b, h, i, s, km: (h, 0, 0)),
                  pl.BlockSpec((1, DA_DIM), lambda b, h, i, s, km: (0, 0)),
                  pl.BlockSpec((1, DA_DIM), lambda b, h, i, s, km: (0, 0)),
                  pl.BlockSpec((1, DA_DIM), lambda b, h, i, s, km: (0, 0)),
                  pl.BlockSpec((1, DA_DIM), lambda b, h, i, s, km: (0, 0)),
                  pl.BlockSpec((DA_VDIM, 1), lambda b, h, i, s, km: (0, 0))],
        out_specs=pl.BlockSpec((None, 128, tile), lambda b, h, i, s, km: (b, h, i)),
        scratch_shapes=[pltpu.VMEM((v_rows, 2 * tq), F32), pltpu.VMEM((4 * DA_DIM, 2 * tq), BF16),
                        pltpu.VMEM((tk, 2 * tq), F32), pltpu.VMEM((tk, 2 * tq), F32),
                        pltpu.VMEM((tk, 2 * tq), BF16), pltpu.VMEM((tk, 2 * tq), BF16),
                        pltpu.VMEM((tk, 2 * tq), I32)],
    )
    return pl.pallas_call(
        functools.partial(_da_kernel, tq=tq, tk=tk, lam_init=lam_init),
        grid_spec=grid_spec,
        out_shape=jax.ShapeDtypeStruct((B, DA_HEADS * DA_VDIM, S), BF16),
        compiler_params=_cparams(("parallel", "parallel", "arbitrary")),
        name="diffattn",
    )(slopes, kmin, qT, slab3, vT, _pos_features(tk), srows, vec(lq1), vec(lk1), vec(lq2), vec(lk2),
      subw.reshape(DA_VDIM, 1).astype(F32))


def _nsa_kernel(slopes_ref, qT_ref, kc_ref, vcT_ref, ksa_ref, vsT_ref, kwa_ref, vwT_ref, wb_ref, gT_ref, mT_ref,
                grp_ref, ltri_ref, oT_ref, acc_ref, res_ref, base_ref, qa_ref, sa_ref, sb_ref, pa_ref, pb_ref,
                list_ref, slist_ref, lsem, *, t, tk, n_top):
    g = pl.program_id(1)
    qi = pl.program_id(2)
    q0 = qi * t
    w = NSA_HPG * t
    n_cp = kc_ref.shape[0]
    n_sel = mT_ref.shape[0]
    lane1 = lax.broadcasted_iota(I32, (1, w), 1)
    slope_row = jnp.zeros((1, w), F32)
    for hh in range(NSA_HPG):
        slope_row = jnp.where((lane1 >= hh * t) & (lane1 < (hh + 1) * t), slopes_ref[g * NSA_HPG + hh], slope_row)
    q_all = qT_ref[...]
    gates = jax.nn.sigmoid(gT_ref[...])
    key_off = lax.broadcasted_iota(I32, (tk, w), 0)
    t_loc = jnp.bitwise_and(lax.broadcasted_iota(I32, (tk, w), 1), t - 1)
    base_ref[...] = slope_row * key_off.astype(F32)

    c_idx = lax.broadcasted_iota(I32, (n_cp, w), 0)
    t_c = q0 + jnp.bitwise_and(lax.broadcasted_iota(I32, (n_cp, w), 1), t - 1)
    d_c = t_c - (c_idx * CMP_STRIDE + (CMP_BLOCK - 1))
    ok_c = d_c >= 0
    s = jnp.dot(kc_ref[...], q_all, preferred_element_type=F32) - slope_row * d_c.astype(F32)
    s = jnp.where(ok_c, s, NEG)
    e = jnp.exp(s - jnp.max(s, axis=0, keepdims=True))
    inv = 1.0 / jnp.sum(e, axis=0, keepdims=True)
    p = jnp.where(ok_c, e * inv, 0.0)
    res_ref[...] = gates[0:1, :] * jnp.dot(vcT_ref[...], p.astype(BF16), preferred_element_type=F32)
    p_sum = p[:, 0:t]
    for hh in range(1, NSA_HPG):
        p_sum = p_sum + p[:, hh * t:(hh + 1) * t]

    p_hi = p_sum.astype(BF16)
    p_lo = (p_sum - p_hi.astype(F32)).astype(BF16)
    mT = mT_ref[...]
    imp = jnp.dot(mT, p_hi, preferred_element_type=F32) + jnp.dot(mT, p_lo, preferred_element_type=F32)
    blk = lax.broadcasted_iota(I32, (n_sel, t), 0)
    t_s = q0 + lax.broadcasted_iota(I32, (n_sel, t), 1)
    cur = jnp.right_shift(t_s, int(math.log2(SEL_BLOCK)))
    forced = (blk == 0) | (blk == cur) | (blk == cur - 1)
    score = jnp.where(blk * SEL_BLOCK <= t_s, imp + jnp.where(forced, FORCE_BONUS, 0.0), NEG)
    blk_f = blk.astype(F32)
    sel = jnp.zeros((n_sel, t), F32)
    for _ in range(n_top):
        _, idx = _first_max(score, blk_f)
        hit = blk_f == idx
        sel = jnp.where(hit, 1.0, sel)
        score = jnp.where(hit, -jnp.inf, score)
    n_full = q0 // tk
    nt_pad = grp_ref.shape[0]
    cnt = jnp.broadcast_to(jnp.sum(sel, axis=1, keepdims=True), (n_sel, 128)).astype(BF16)
    tile_cnt = jnp.dot(grp_ref[...], cnt, preferred_element_type=F32)
    kt_i = lax.broadcasted_iota(I32, (nt_pad, 128), 0)
    active = (tile_cnt > 0.5) & (kt_i < n_full)
    act_f = jnp.where(active, 1.0, 0.0)
    before = jnp.dot(ltri_ref[...], act_f.astype(BF16), preferred_element_type=F32)
    lane_j = lax.broadcasted_iota(I32, (nt_pad, 128), 1)
    slot_hit = active & (before == lane_j.astype(F32))
    tiles_row = jnp.sum(jnp.where(slot_hit, kt_i.astype(F32), 0.0), axis=0, keepdims=True)
    n_act_row = jnp.sum(act_f, axis=0, keepdims=True)
    lane8 = lax.broadcasted_iota(I32, (8, 128), 1)
    list_ref[...] = jnp.where(lane8 == 127, n_act_row, tiles_row).astype(I32)
    list_copy = pltpu.make_async_copy(list_ref, slist_ref, lsem)
    list_copy.start()

    selb = jnp.where(sel > 0.5, 0.0, NEG)
    row64 = lax.broadcasted_iota(I32, (NSA_DIM, w), 0)
    qa_ref[0:NSA_DIM, :] = q_all
    qa_ref[NSA_DIM:2 * NSA_DIM, :] = jnp.where(row64 == 0, NEG, 0.0).astype(BF16)
    qa_ref[2 * NSA_DIM:2 * NSA_DIM + n_sel, :] = jnp.concatenate([selb] * NSA_HPG, axis=1).astype(BF16)
    if n_sel < SEL_COLS:
        qa_ref[2 * NSA_DIM + n_sel:, :] = jnp.zeros((SEL_COLS - n_sel, w), BF16)

    n_wt = (WINDOW + t) // t
    kw_t = kwa_ref[pl.ds(pl.multiple_of(q0, t), WINDOW + t), :]
    s = jnp.dot(kw_t, qa_ref[0:2 * NSA_DIM, :], preferred_element_type=F32) + wb_ref[...]
    p = jnp.exp(s - jnp.max(s, axis=0, keepdims=True))
    inv = 1.0 / jnp.sum(p, axis=0, keepdims=True)
    p = p.astype(BF16)
    o_w = jnp.dot(vwT_ref[qi], p[0:t, :], preferred_element_type=F32)
    for c in range(1, n_wt):
        o_w = o_w + jnp.dot(vwT_ref[qi + c], p[c * t:(c + 1) * t, :], preferred_element_type=F32)
    res_ref[...] = res_ref[...] + gates[2:3, :] * (o_w * inv)

    def scores(kt, s_ref):
        k_t = ksa_ref[pl.ds(pl.multiple_of(kt * tk, tk), tk), :]
        s = jnp.dot(k_t, qa_ref[...], preferred_element_type=F32) + base_ref[...]
        s_ref[...] = s
        return jnp.max(s, axis=0, keepdims=True)

    def probs(s_ref, p_ref, mx, off, m_old, l_old):
        m_new = jnp.maximum(m_old, mx + off)
        p = jnp.exp(s_ref[...] - (m_new - off))
        alpha = jnp.exp(m_old - m_new)
        p_ref[...] = p.astype(BF16)
        return m_new, alpha * l_old + jnp.sum(p, axis=0, keepdims=True), alpha

    def accumulate(kt, p_ref, alpha):
        acc_ref[...] = alpha * acc_ref[...] + jnp.dot(vsT_ref[kt], p_ref[...], preferred_element_type=F32)

    list_copy.wait()
    n_act = slist_ref[0, 127]
    mx_a = scores(slist_ref[0, 0], sa_ref)

    kd = n_full * tk
    s = jnp.dot(ksa_ref[pl.ds(pl.multiple_of(kd, tk), tk), :], qa_ref[...], preferred_element_type=F32) + base_ref[...]
    s = jnp.where(key_off - t_loc <= q0 - kd, s, NEG)
    off_d = slope_row * kd.astype(F32)
    m = jnp.max(s, axis=0, keepdims=True) + off_d
    p = jnp.exp(s - (m - off_d))
    l = jnp.sum(p, axis=0, keepdims=True)
    acc_ref[...] = jnp.dot(vsT_ref[n_full], p.astype(BF16), preferred_element_type=F32)
    pb_ref[...] = jnp.zeros_like(pb_ref)

    def pair(j, carry):
        m, l, mx_a, alpha_b, tb_prev = carry
        ta = slist_ref[0, 2 * j]
        tb = slist_ref[0, 2 * j + 1]
        off_a = slope_row * (ta * tk).astype(F32)
        off_b = jnp.where(2 * j + 1 < n_act, slope_row * (tb * tk).astype(F32), NEG)
        accumulate(tb_prev, pb_ref, alpha_b)
        m, l, alpha_a = probs(sa_ref, pa_ref, mx_a, off_a, m, l)
        mx_b = scores(tb, sb_ref)
        accumulate(ta, pa_ref, alpha_a)
        m, l, alpha_b = probs(sb_ref, pb_ref, mx_b, off_b, m, l)
        mx_a = scores(slist_ref[0, 2 * j + 2], sa_ref)
        return m, l, mx_a, alpha_b, tb

    n_pairs = (n_act + 1) // 2
    m, l, _, alpha_b, tb_last = lax.fori_loop(0, n_pairs, pair, (m, l, mx_a, jnp.ones((1, w), F32), jnp.int32(0)))
    accumulate(tb_last, pb_ref, alpha_b)
    oT_ref[...] = (res_ref[...] + gates[1:2, :] * (acc_ref[...] * (1.0 / l))).astype(BF16)


def _nsa_attention(slopes, qT, kc, vcT, ksa, vsT, kwa, vwT, wb, gT, mT, t, tk):
    B, G, nq, _, w = qT.shape
    S = nq * t
    n_cp = kc.shape[2]
    n_sel = S // SEL_BLOCK
    n_top = min(SEL_TOPK, n_sel)
    assert tk % t == 0 and S % tk == 0 and t & (t - 1) == 0 and WINDOW % t == 0 and n_sel <= SEL_COLS
    n_tiles = S // tk
    nt_pad = -(-n_tiles // 16) * 16
    assert nt_pad + 2 < 127
    grp = jnp.asarray((np.arange(n_sel)[None, :] // (tk // SEL_BLOCK) == np.arange(nt_pad)[:, None])
                      .astype(np.float32), BF16)
    ltri = jnp.asarray((np.arange(nt_pad)[None, :] < np.arange(nt_pad)[:, None]).astype(np.float32), BF16)
    fixed = lambda b, g, i, s: (b, g, 0, 0)
    fixed5 = lambda b, g, i, s: (b, g, 0, 0, 0)
    tile5 = lambda b, g, i, s: (b, g, i, 0, 0)
    grid_spec = pltpu.PrefetchScalarGridSpec(
        num_scalar_prefetch=1,
        grid=(B, G, nq),
        in_specs=[pl.BlockSpec((None, None, None, NSA_DIM, w), tile5),
                  pl.BlockSpec((None, None, n_cp, NSA_DIM), fixed),
                  pl.BlockSpec((None, None, NSA_DIM, n_cp), fixed),
                  pl.BlockSpec((None, None, S, ksa.shape[-1]), fixed),
                  pl.BlockSpec((None, None, S // tk, NSA_DIM, tk), fixed5),
                  pl.BlockSpec((None, None, S + WINDOW, 2 * NSA_DIM), fixed),
                  pl.BlockSpec((None, None, (S + WINDOW) // t, NSA_DIM, t), fixed5),
                  pl.BlockSpec((None, WINDOW + t, w), lambda b, g, i, s: (g, 0, 0)),
                  pl.BlockSpec((None, None, None, 3, w), tile5),
                  pl.BlockSpec((n_sel, n_cp), lambda b, g, i, s: (0, 0)),
                  pl.BlockSpec((nt_pad, n_sel), lambda b, g, i, s: (0, 0)),
                  pl.BlockSpec((nt_pad, nt_pad), lambda b, g, i, s: (0, 0))],
        out_specs=pl.BlockSpec((None, None, None, NSA_DIM, w), tile5),
        scratch_shapes=[pltpu.VMEM((NSA_DIM, w), F32),
                        pltpu.VMEM((NSA_DIM, w), F32),
                        pltpu.VMEM((tk, w), F32),
                        pltpu.VMEM((2 * NSA_DIM + SEL_COLS, w), BF16),
                        pltpu.VMEM((tk, w), F32), pltpu.VMEM((tk, w), F32),
                        pltpu.VMEM((tk, w), BF16), pltpu.VMEM((tk, w), BF16),
                        pltpu.VMEM((8, 128), I32), pltpu.SMEM((8, 128), I32), pltpu.SemaphoreType.DMA(())],
    )
    return pl.pallas_call(
        functools.partial(_nsa_kernel, t=t, tk=tk, n_top=n_top),
        grid_spec=grid_spec,
        out_shape=jax.ShapeDtypeStruct((B, G, nq, NSA_DIM, w), BF16),
        compiler_params=_cparams(("parallel", "parallel", "arbitrary")),
        name="nsa",
    )(slopes, qT, kc, vcT, ksa, vsT, kwa, vwT, wb, gT, mT, grp, ltri)


def _pack_halves(v):
    c = v.shape[1] // 2
    lo = lax.bitcast_convert_type(v[:, :c].astype(BF16).astype(F32), U32)
    hi = lax.bitcast_convert_type(v[:, c:].astype(BF16).astype(F32), U32)
    return hi | (lo >> 16)


def _unpack_halves(u):
    lo = lax.bitcast_convert_type(u << 16, F32)
    hi = lax.bitcast_convert_type(u & jnp.uint32(0xFFFF0000), F32)
    return lo, hi


def _merge_kernel(oa_ref, ob_ref, ga_ref, gb_ref, x_ref, g1_ref, nw_ref, sh_ref, sc_ref,
                  wa_ref, wb_ref, wo_ref, wr_ref, x1_ref, h2_ref, lg_ref):
    ya = jnp.dot(oa_ref[...], wa_ref[...], preferred_element_type=F32)
    yb = jnp.dot(ob_ref[...], wb_ref[...], preferred_element_type=F32)
    merged = (jax.nn.sigmoid(ga_ref[...].astype(F32)) * ya
              + jax.nn.sigmoid(gb_ref[...].astype(F32)) * yb).astype(BF16)
    mix = jnp.dot(merged, wo_ref[...], preferred_element_type=F32)
    x1 = x_ref[...] + g1_ref[...] * mix
    x1_ref[...] = x1
    h2 = _modulate(x1, nw_ref[...], sh_ref[...], sc_ref[...])
    h2_ref[...] = _pack_halves(h2)
    lg_ref[...] = jnp.dot(h2, wr_ref[...], preferred_element_type=F32, precision=HIGHEST)


def _merge(oa, ob, slab, x2, g1, nw, sh, sc, wa, wb, wo, wr, S):
    N, D = x2.shape
    tm = 256
    per_b = S // tm
    mcol = COL_MERGE // D
    row = lambda i: (i, 0)
    full = lambda i: (0, 0)
    perb = lambda i: (i // per_b, 0, 0)
    n_lg = wr.shape[1]
    return pl.pallas_call(
        _merge_kernel,
        grid=(N // tm,),
        in_specs=[pl.BlockSpec((tm, oa.shape[1]), row),
                  pl.BlockSpec((tm, ob.shape[1]), row),
                  pl.BlockSpec((tm, D), lambda i: (i, mcol)),
                  pl.BlockSpec((tm, D), lambda i: (i, mcol + 1)),
                  pl.BlockSpec((tm, D), row),
                  pl.BlockSpec((None, 1, D), perb),
                  pl.BlockSpec((1, D), full),
                  pl.BlockSpec((None, 1, D), perb),
                  pl.BlockSpec((None, 1, D), perb),
                  pl.BlockSpec(wa.shape, full),
                  pl.BlockSpec(wb.shape, full),
                  pl.BlockSpec(wo.shape, full),
                  pl.BlockSpec(wr.shape, full)],
        out_specs=[pl.BlockSpec((tm, D), row), pl.BlockSpec((tm, D // 2), row), pl.BlockSpec((tm, n_lg), row)],
        out_shape=[jax.ShapeDtypeStruct((N, D), F32), jax.ShapeDtypeStruct((N, D // 2), U32),
                   jax.ShapeDtypeStruct((N, n_lg), F32)],
        compiler_params=_cparams(("parallel",)),
        name="merge",
    )(oa, ob, slab, slab, x2, g1.reshape(-1, 1, D), nw.reshape(1, D), sh.reshape(-1, 1, D),
      sc.reshape(-1, 1, D), wa, wb, wo, wr)


def _first_max(v, idx):
    mx = jnp.max(v, axis=0, keepdims=True)
    first = jnp.min(jnp.where(v == mx, idx, float(v.shape[0])), axis=0, keepdims=True)
    return mx, first


def _route_kernel(lg_ref, b_ref, e_ref, w_ref):
    lg = lg_ref[...]
    tn = lg.shape[1]
    per_g = N_EXPERTS // N_EXPERT_GROUPS
    scores = jax.nn.sigmoid(lg)
    biased = scores + b_ref[...]
    gi = lax.broadcasted_iota(I32, (N_EXPERT_GROUPS, tn), 0).astype(F32)
    gscore = jnp.zeros((N_EXPERT_GROUPS, tn), F32)
    for g in range(N_EXPERT_GROUPS):
        grp = biased[g * per_g:(g + 1) * per_g, :]
        m1, f1 = _first_max(grp, gi)
        m2 = jnp.max(jnp.where(gi == f1, -jnp.inf, grp), axis=0, keepdims=True)
        gscore = jnp.where(gi == g, m1 + m2, gscore)
    gsel = jnp.zeros((N_EXPERT_GROUPS, tn), F32)
    for _ in range(TOPK_EXPERT_GROUPS):
        _, f = _first_max(gscore, gi)
        hit = gi == f
        gsel = jnp.where(hit, 1.0, gsel)
        gscore = jnp.where(hit, -jnp.inf, gscore)
    ei = lax.broadcasted_iota(I32, (N_EXPERTS, tn), 0).astype(F32)
    emask = jnp.zeros((N_EXPERTS, tn), F32)
    for g in range(N_EXPERT_GROUPS):
        in_g = (ei >= g * per_g) & (ei < (g + 1) * per_g)
        emask = jnp.where(in_g, gsel[g:g + 1, :], emask)
    masked = jnp.where(emask > 0.5, biased, NEG)
    e_out = jnp.zeros((MOE_TOPK, tn), F32)
    w_out = jnp.zeros((MOE_TOPK, tn), F32)
    for r in range(MOE_TOPK):
        _, f = _first_max(masked, ei)
        hit = ei == f
        wv = jnp.sum(jnp.where(hit, scores, 0.0), axis=0, keepdims=True)
        e_out = jnp.where(gi == r, f, e_out)
        w_out = jnp.where(gi == r, wv, w_out)
        masked = jnp.where(hit, -jnp.inf, masked)
    e_ref[...] = e_out.astype(I32)
    w_ref[...] = w_out / jnp.sum(w_out, axis=0, keepdims=True) * ROUTED_SCALE


def _route(lgT, rb):
    E, N = lgT.shape
    tn = 512
    return pl.pallas_call(
        _route_kernel,
        grid=(N // tn,),
        in_specs=[pl.BlockSpec((E, tn), lambda i: (0, i)), pl.BlockSpec((E, 1), lambda i: (0, 0))],
        out_specs=[pl.BlockSpec((MOE_TOPK, tn), lambda i: (0, i)), pl.BlockSpec((MOE_TOPK, tn), lambda i: (0, i))],
        out_shape=[jax.ShapeDtypeStruct((MOE_TOPK, N), I32), jax.ShapeDtypeStruct((MOE_TOPK, N), F32)],
        compiler_params=_cparams(("parallel",)),
        name="route",
    )(lgT, rb.reshape(E, 1).astype(F32))


def _slots_kernel(e_ref, tri_ref, lt_ref, dest_ref, cnt_ref, carry_ref, *, rb):
    phase = pl.program_id(0)
    i = pl.program_id(1)
    e = e_ref[...]
    tn = e.shape[1]
    ei = lax.broadcasted_iota(I32, (N_EXPERTS, tn), 0)

    @pl.when((phase == 0) & (i == 0))
    def _():
        carry_ref[...] = jnp.zeros_like(carry_ref)

    @pl.when(phase == 0)
    def _():
        tot = jnp.zeros((N_EXPERTS, 1), F32)
        for k in range(MOE_TOPK):
            oh = jnp.where(ei == e[k:k + 1, :], 1.0, 0.0)
            tot = tot + jnp.sum(oh, axis=1, keepdims=True)
        carry_ref[...] = carry_ref[...] + tot
        dest_ref[...] = jnp.zeros_like(dest_ref)

    @pl.when((phase == 1) & (i == 0))
    def _():
        cnt = carry_ref[...]
        cnt_ref[...] = cnt
        padded = jnp.broadcast_to(jnp.ceil(cnt / rb) * rb, (N_EXPERTS, 128))
        first = jnp.dot(lt_ref[...], padded, preferred_element_type=F32, precision=HIGHEST)
        carry_ref[...] = first[:, :1]

    @pl.when(phase == 1)
    def _():
        carry = carry_ref[...]
        ki = lax.broadcasted_iota(I32, (MOE_TOPK, tn), 0)
        dest = jnp.zeros((MOE_TOPK, tn), F32)
        for k in range(MOE_TOPK):
            hit = ei == e[k:k + 1, :]
            oh = jnp.where(hit, 1.0, 0.0)
            before = jnp.dot(oh.astype(BF16), tri_ref[...], preferred_element_type=F32)
            row = jnp.sum(jnp.where(hit, before + carry, 0.0), axis=0, keepdims=True)
            dest = jnp.where(ki == k, row, dest)
            carry = carry + jnp.sum(oh, axis=1, keepdims=True)
        carry_ref[...] = carry
        dest_ref[...] = dest.astype(I32)


def _slots(top_eT, rb):
    K, N = top_eT.shape
    tn = 512
    idx = np.arange(tn)
    tri = jnp.asarray((idx[:, None] < idx[None, :]).astype(np.float32), BF16)
    e_idx = np.arange(N_EXPERTS)
    lt = jnp.asarray((e_idx[None, :] < e_idx[:, None]).astype(np.float32), F32)
    return pl.pallas_call(
        functools.partial(_slots_kernel, rb=rb),
        grid=(2, N // tn),
        in_specs=[pl.BlockSpec((K, tn), lambda p, i: (0, i)),
                  pl.BlockSpec((tn, tn), lambda p, i: (0, 0)),
                  pl.BlockSpec((N_EXPERTS, N_EXPERTS), lambda p, i: (0, 0))],
        out_specs=[pl.BlockSpec((K, tn), lambda p, i: (0, i * p)),
                   pl.BlockSpec((N_EXPERTS, 1), lambda p, i: (0, 0))],
        out_shape=[jax.ShapeDtypeStruct((K, N), I32), jax.ShapeDtypeStruct((N_EXPERTS, 1), F32)],
        scratch_shapes=[pltpu.VMEM((N_EXPERTS, 1), F32)],
        compiler_params=_cparams(("arbitrary", "arbitrary")),
        name="slots",
    )(top_eT, tri, lt)


def _dispatch_kernel(dest_ref, h_ref, xs_in_ref, xs_ref, sem, *, tt):
    del xs_in_ref

    def body(n, c):
        src = h_ref.at[pl.ds(n, 1), :]
        for k in range(MOE_TOPK):
            d = dest_ref[0, 0, k * tt + n]
            pltpu.make_async_copy(src, xs_ref.at[pl.ds(d, 1), :], sem).start(priority=k % 2)
        return c

    lax.fori_loop(0, tt, body, 0)
    for k in range(MOE_TOPK):
        pltpu.make_async_copy(h_ref, xs_ref.at[pl.ds(0, tt), :], sem).wait()


def _dispatch(dest_tiles, h2, n_rows, tt):
    N, D = h2.shape
    xs0 = jnp.zeros((n_rows, D), h2.dtype)
    return pl.pallas_call(
        functools.partial(_dispatch_kernel, tt=tt),
        grid=(N // tt,),
        in_specs=[pl.BlockSpec((1, 1, MOE_TOPK * tt), lambda i: (i, 0, 0), memory_space=pltpu.SMEM),
                  pl.BlockSpec((tt, D), lambda i: (i, 0)),
                  pl.BlockSpec(memory_space=pl.ANY)],
        out_specs=pl.BlockSpec(memory_space=pl.ANY),
        out_shape=jax.ShapeDtypeStruct((n_rows, D), h2.dtype),
        scratch_shapes=[pltpu.SemaphoreType.DMA(())],
        input_output_aliases={2: 0},
        compiler_params=_cparams(("arbitrary",)),
        name="dispatch",
    )(dest_tiles, h2, xs0)


def _experts_kernel(be_ref, x_ref, wgu_ref, wd_ref, y_ref):
    del be_ref
    lo, hi = _unpack_halves(x_ref[...])
    x = jnp.concatenate([lo.astype(BF16), hi.astype(BF16)], axis=1)
    gu = jnp.dot(x, wgu_ref[...], preferred_element_type=F32)
    hdim = gu.shape[1] // 2
    g = gu[:, :hdim]
    act = (g * jax.nn.sigmoid(g) * gu[:, hdim:]).astype(BF16)
    y_ref[...] = _pack_halves(jnp.dot(act, wd_ref[...], preferred_element_type=F32))


def _experts(blk_e, xs, wgu, wd, rb):
    n_rows, D = xs.shape
    grid_spec = pltpu.PrefetchScalarGridSpec(
        num_scalar_prefetch=1,
        grid=(n_rows // rb,),
        in_specs=[pl.BlockSpec((rb, D), lambda i, be: (i, 0)),
                  pl.BlockSpec((None,) + wgu.shape[1:], lambda i, be: (be[i], 0, 0)),
                  pl.BlockSpec((None,) + wd.shape[1:], lambda i, be: (be[i], 0, 0))],
        out_specs=pl.BlockSpec((rb, D), lambda i, be: (i, 0)),
    )
    return pl.pallas_call(
        _experts_kernel,
        grid_spec=grid_spec,
        out_shape=jax.ShapeDtypeStruct((n_rows, D), U32),
        compiler_params=_cparams(("arbitrary",)),
        name="experts",
    )(blk_e, xs, wgu, wd)


def _combine_kernel(dest_ref, ys_ref, w_ref, h_ref, x1_ref, g2_ref, fw_ref, sgu_ref, sd_ref, o_ref, buf_ref, sem, *, tt):
    def body(n, c):
        for k in range(MOE_TOPK):
            d = dest_ref[0, 0, k * tt + n]
            pltpu.make_async_copy(ys_ref.at[pl.ds(d, 1), :], buf_ref.at[k, pl.ds(n, 1), :], sem).start(priority=k % 2)
        return c

    lax.fori_loop(0, tt, body, 0)

    h_lo, h_hi = _unpack_halves(h_ref[...])
    h = jnp.concatenate([h_lo.astype(BF16), h_hi.astype(BF16)], axis=1)
    gu = jnp.dot(h, sgu_ref[...], preferred_element_type=F32)
    hdim = gu.shape[1] // 2
    g = gu[:, :hdim]
    ffn = jnp.dot((g * jax.nn.sigmoid(g) * gu[:, hdim:]).astype(BF16), sd_ref[...], preferred_element_type=F32)

    for k in range(MOE_TOPK):
        pltpu.make_async_copy(ys_ref.at[pl.ds(0, tt), :], buf_ref.at[k], sem).wait()
    w = w_ref[...]
    r_lo = jnp.zeros(h_lo.shape, F32)
    r_hi = jnp.zeros(h_hi.shape, F32)
    for k in range(MOE_TOPK):
        y_lo, y_hi = _unpack_halves(buf_ref[k])
        r_lo = r_lo + w[:, k:k + 1] * y_lo
        r_hi = r_hi + w[:, k:k + 1] * y_hi
    ffn = ffn + jnp.concatenate([r_lo, r_hi], axis=1)
    x2 = x1_ref[...] + g2_ref[...] * ffn
    r = lax.rsqrt(jnp.mean(x2 * x2, axis=-1, keepdims=True) + NORM_EPS)
    o_ref[...] = x2 * r * fw_ref[...]


def _combine(dest_tiles, ys, top_w, h2p, x1, g2, fw, sgu, sd, S, tt):
    N, D = x1.shape
    per_b = S // tt
    row = lambda i: (i, 0)
    full = lambda i: (0, 0)
    return pl.pallas_call(
        functools.partial(_combine_kernel, tt=tt),
        grid=(N // tt,),
        in_specs=[pl.BlockSpec((1, 1, MOE_TOPK * tt), lambda i: (i, 0, 0), memory_space=pltpu.SMEM),
                  pl.BlockSpec(memory_space=pl.ANY),
                  pl.BlockSpec((tt, MOE_TOPK), row),
                  pl.BlockSpec((tt, D // 2), row),
                  pl.BlockSpec((tt, D), row),
                  pl.BlockSpec((None, 1, D), lambda i: (i // per_b, 0, 0)),
                  pl.BlockSpec((1, D), full),
                  pl.BlockSpec(sgu.shape, full),
                  pl.BlockSpec(sd.shape, full)],
        out_specs=pl.BlockSpec((tt, D), row),
        out_shape=jax.ShapeDtypeStruct((N, D), F32),
        scratch_shapes=[pltpu.VMEM((MOE_TOPK, tt, D // 2), U32), pltpu.SemaphoreType.DMA(())],
        compiler_params=_cparams(("arbitrary",)),
        name="combine",
    )(dest_tiles, ys, top_w, h2p, x1, g2.reshape(-1, 1, D), fw.reshape(1, D), sgu, sd)


def _cmp_to_sel_T(n_cp, n_sel):
    c0 = np.arange(n_cp)[None, :] * CMP_STRIDE
    s0 = np.arange(n_sel)[:, None] * SEL_BLOCK
    ov = np.minimum(c0 + CMP_BLOCK, s0 + SEL_BLOCK) - np.maximum(c0, s0)
    m = np.clip(ov, 0, None).astype(np.float32) / CMP_BLOCK
    m[:, n_cp - 1] = 0.0
    return jnp.asarray(m, BF16)


def _slab_weights(w_in):
    D = w_in.shape[0]
    sizes = [1024, 1024, 1024, 512, 128, 128, 128, 128, 128, 128, 24, 2048]
    offs = np.concatenate([[0], np.cumsum(sizes)])
    part = lambda i: w_in[:, offs[i]:offs[i + 1]]
    pieces = [part(0), part(1), part(2), part(11), part(3)] + [part(i) for i in range(4, 10)] + [part(10)]
    w = jnp.concatenate(pieces, axis=1)
    return jnp.pad(w, ((0, 0), (0, SLAB_COLS - w.shape[1]))).astype(BF16)


def _layer(x, c, ada_w, ada_b, norm1_w, w_in, da_lq1, da_lk1, da_lq2, da_lk2, da_subln_w,
           cmp_k_pe, cmp_k_w1, cmp_k_w2, cmp_v_pe, cmp_v_w1, cmp_v_w2, w_da_out, w_nsa_out, w_o,
           norm2_w, router_w, router_b, exp_w_gate, exp_w_up, exp_w_down,
           sh_w_gate, sh_w_up, sh_w_down, final_norm_w, lam_init):
    B, S, D = x.shape
    N = B * S
    G, HPG = NSA_GROUPS, NSA_HPG
    x2 = x.reshape(N, D)

    mod = _ada(c, ada_w, ada_b)
    sh1, sc1, g1, sh2, sc2, g2 = jnp.split(mod, 6, axis=-1)

    slab = _inproj(x2, norm1_w, sh1, sc1, _slab_weights(w_in), S)
    slab3 = slab.reshape(B, S, SLAB_COLS)

    i_all = np.arange(1, DA_HEADS + NSA_HEADS + 1, dtype=np.float32)
    slopes = (2.0 ** (-8.0 * i_all / (DA_HEADS + NSA_HEADS))).astype(np.float32)
    slopes_a = jnp.asarray(slopes[0::2])
    slopes_b = jnp.asarray(slopes[1::2])

    ta_q, ta_k = 256, min(512, S)
    qT = (slab3[:, :, COL_DAQ:COL_DAQ + 1024] * jnp.asarray(DA_DIM ** -0.5, BF16)).transpose(0, 2, 1)
    vT = slab3[:, :, COL_DAV:COL_DAV + 1024].reshape(B, S // ta_k, ta_k, DA_HEADS, DA_VDIM).transpose(0, 3, 1, 4, 2)
    kmin = _first_live_tile(slopes[0::2], qT, slab3, ta_q, ta_k)
    oaT = _diff_attention(slopes_a, kmin, qT, slab3, _with_ones_row(vT), _slope_rows(slopes[0::2]), da_lq1, da_lk1,
                          da_lq2, da_lk2, da_subln_w, lam_init, ta_q, ta_k)
    oa = oaT.transpose(0, 2, 1).reshape(N, DA_HEADS * DA_VDIM)

    tb, tb_k = 128, min(256, S)
    n_cp = S // CMP_STRIDE
    n_sel = S // SEL_BLOCK

    def kv_groups(j):
        c0 = COL_KV6 + 128 * j
        return slab3[:, :, c0:c0 + 128].reshape(B, S, G, NSA_DIM).transpose(0, 2, 1, 3)

    def kv_tiles_T(a, rows):
        return a.reshape(B, G, S // rows, rows, NSA_DIM).transpose(0, 1, 2, 4, 3)

    ck_rows = kv_groups(0).reshape(B * G, n_cp, CMP_STRIDE * NSA_DIM)
    cv_rows = kv_groups(1).reshape(B * G, n_cp, CMP_STRIDE * NSA_DIM)
    kc = _compress(ck_rows, cmp_k_pe, cmp_k_w1, cmp_k_w2).reshape(B, G, n_cp, NSA_DIM).astype(BF16)
    vc = _compress(cv_rows, cmp_v_pe, cmp_v_w1, cmp_v_w2).reshape(B, G, n_cp, NSA_DIM).astype(BF16)
    nq = (slab3[:, :, COL_NSAQ:COL_NSAQ + 512] * jnp.asarray(NSA_DIM ** -0.5, BF16))
    nqT = nq.reshape(B, S // tb, tb, G, HPG, NSA_DIM).transpose(0, 3, 1, 5, 4, 2).reshape(B, G, S // tb, NSA_DIM, HPG * tb)
    gT = (slab3[:, :, COL_NSAG:COL_NSAG + 3 * NSA_HEADS].astype(F32).reshape(B, S // tb, tb, G, HPG, 3)
          .transpose(0, 3, 1, 5, 4, 2).reshape(B, G, S // tb, 3, HPG * tb))
    onehot = jnp.asarray((np.arange(S)[:, None] // SEL_BLOCK == np.arange(SEL_COLS)[None, :]).astype(np.float32), BF16)
    zeros64 = jnp.zeros((B, G, S, NSA_DIM), BF16)
    ksa = jnp.concatenate([kv_groups(2), zeros64, jnp.broadcast_to(onehot, (B, G, S, SEL_COLS))], axis=-1)
    pad_rows = jnp.zeros((B, G, WINDOW, 2 * NSA_DIM), BF16).at[..., NSA_DIM].set(1.0)
    kwa = jnp.concatenate([pad_rows, jnp.concatenate([kv_groups(4), zeros64], axis=-1)], axis=2)
    vw_pad = jnp.pad(kv_groups(5), ((0, 0), (0, 0), (WINDOW, 0), (0, 0)))
    vwT = vw_pad.reshape(B, G, (S + WINDOW) // tb, tb, NSA_DIM).transpose(0, 1, 2, 4, 3)
    r_w = np.arange(WINDOW + tb)[:, None]
    lane_w = np.arange(HPG * tb)[None, :]
    d_w = (WINDOW + lane_w % tb - r_w).astype(np.float32)
    slope_w = slopes[1::2].reshape(G, 1, HPG)[:, :, lane_w[0] // tb]
    wb = jnp.asarray(np.where((d_w >= 0) & (d_w < WINDOW), -slope_w * d_w[None], np.float32(NEG)).astype(np.float32))
    obT = _nsa_attention(slopes_b, nqT, kc, vc.transpose(0, 1, 3, 2), ksa, kv_tiles_T(kv_groups(3), tb_k),
                         kwa, vwT, wb, gT, _cmp_to_sel_T(n_cp, n_sel), tb, tb_k)
    ob = (obT.reshape(B, G, S // tb, NSA_DIM, HPG, tb).transpose(0, 2, 5, 1, 4, 3)
          .reshape(N, NSA_HEADS * NSA_DIM))

    x1, h2, logits = _merge(oa, ob, slab, x2, g1, norm2_w, sh2, sc2, w_da_out.astype(BF16),
                            w_nsa_out.astype(BF16), w_o.astype(BF16), router_w.astype(F32), S)

    rb = 512
    tt = 128
    top_eT, top_wT = _route(logits.T, router_b)
    destT, counts = _slots(top_eT, rb)
    n_rows = ((N * MOE_TOPK + N_EXPERTS * (rb - 1) + rb - 1) // rb) * rb
    padded = (jnp.ceil(counts[:, 0] / rb) * rb).astype(I32)
    pend = jnp.cumsum(padded)
    blk_start = jnp.arange(n_rows // rb, dtype=I32) * rb
    blk_e = jnp.minimum(jnp.sum((pend[None, :] <= blk_start[:, None]).astype(I32), axis=1), N_EXPERTS - 1)
    dest_tiles = destT.reshape(MOE_TOPK, N // tt, tt).transpose(1, 0, 2).reshape(N // tt, 1, MOE_TOPK * tt)
    xs = _dispatch(dest_tiles, h2, n_rows, tt)
    wgu = jnp.concatenate([exp_w_gate, exp_w_up], axis=-1).astype(BF16)
    ys = _experts(blk_e, xs, wgu, exp_w_down.astype(BF16), rb)
    sgu = jnp.concatenate([sh_w_gate, sh_w_up], axis=-1).astype(BF16)
    out = _combine(dest_tiles, ys, top_wT.T, h2, x1, g2, final_norm_w, sgu, sh_w_down.astype(BF16), S, tt)
    return out.reshape(B, S, D)


def kernel(x, c, ada_w, ada_b, norm1_w, w_in, da_lq1, da_lk1, da_lq2, da_lk2, da_subln_w, cmp_k_pe, cmp_k_w1, cmp_k_w2, cmp_v_pe, cmp_v_w1, cmp_v_w2, w_da_out, w_nsa_out, w_o, norm2_w, router_w, router_b, exp_w_gate, exp_w_up, exp_w_down, sh_w_gate, sh_w_up, sh_w_down, final_norm_w):
    depth = ada_w.shape[0]
    assert depth == 1, "one decoder layer"
    lam_init = 0.8 - 0.6 * math.exp(-0.3 * 0)
    return _layer(x, c, ada_w[0], ada_b[0], norm1_w[0], w_in[0], da_lq1[0], da_lk1[0], da_lq2[0], da_lk2[0],
                  da_subln_w[0], cmp_k_pe[0], cmp_k_w1[0], cmp_k_w2[0], cmp_v_pe[0], cmp_v_w1[0], cmp_v_w2[0],
                  w_da_out[0], w_nsa_out[0], w_o[0], norm2_w[0], router_w[0], router_b[0],
                  exp_w_gate[0], exp_w_up[0], exp_w_down[0], sh_w_gate[0], sh_w_up[0], sh_w_down[0],
                  final_norm_w, lam_init)
```

```python
import functools
import math

import numpy as np
import jax
import jax.numpy as jnp
from jax import lax
from jax.experimental import pallas as pl
from jax.experimental.pallas import tpu as pltpu

F32 = jnp.float32
BF16 = jnp.bfloat16
I32 = jnp.int32
U32 = jnp.uint32
HIGHEST = lax.Precision.HIGHEST

NORM_EPS = 1e-6
NEG = -1e30

DA_HEADS = 8
DA_DIM = 64
DA_VDIM = 128
DA_SUBLN_EPS = 1e-5

NSA_HEADS = 8
NSA_GROUPS = 2
NSA_HPG = 4
NSA_DIM = 64
CMP_BLOCK = 32
CMP_STRIDE = 16
SEL_BLOCK = 64
SEL_TOPK = 16
WINDOW = 512
FORCE_BONUS = 1e4
SEL_COLS = 128

N_EXPERTS = 64
MOE_TOPK = 8
N_EXPERT_GROUPS = 8
TOPK_EXPERT_GROUPS = 4
ROUTED_SCALE = 2.5

VMEM_LIMIT_V7X = 56 * 1024 * 1024

COL_DAQ, COL_DAK, COL_DAV = 0, 1024, 2048
COL_MERGE = 3072
COL_NSAQ = 5120
COL_KV6 = 5632
COL_NSAG = 6400
SLAB_COLS = 6528


POS_SPLIT = 16
POS_ROWS = 16
ONES_ROWS = 16


def _pos_features(rows):
    r = np.arange(rows)
    f = np.zeros((rows, 128), np.float32)
    f[:, 0:3] = (r // POS_SPLIT)[:, None]
    f[:, 3:6] = (r % POS_SPLIT)[:, None]
    return jnp.asarray(f, BF16)


def _slope_rows(slopes):
    s = np.asarray(slopes, np.float32)
    bf = lambda x: x.astype(BF16).astype(np.float32)
    p1 = bf(s)
    p2 = bf(s - p1)
    p3 = bf(s - p1 - p2)
    out = np.zeros((s.shape[0], POS_ROWS, 128), np.float32)
    for i, piece in enumerate((p1, p2, p3)):
        out[:, i, :] = POS_SPLIT * piece[:, None]
        out[:, 3 + i, :] = piece[:, None]
    return jnp.asarray(out)


def _with_ones_row(vT):
    shape = vT.shape[:-2] + (ONES_ROWS, vT.shape[-1])
    extra = jnp.zeros(shape, vT.dtype).at[..., 0, :].set(1.0)
    return jnp.concatenate([vT, extra], axis=-2)


def _cparams(sem, vmem=VMEM_LIMIT_V7X):
    return pltpu.CompilerParams(dimension_semantics=sem, vmem_limit_bytes=vmem)


def _ada_kernel(c_ref, w_ref, b_ref, o_ref):
    c = c_ref[...]
    ca = c * jax.nn.sigmoid(c)
    o_ref[...] = jnp.dot(ca, w_ref[...], preferred_element_type=F32, precision=HIGHEST) + b_ref[...]


def _ada(c, w, b):
    B, D = c.shape
    n_out = w.shape[1]
    rows = 8
    cp = jnp.zeros((rows, D), F32).at[:B].set(c)
    tn = 1024
    out = pl.pallas_call(
        _ada_kernel,
        grid=(n_out // tn,),
        in_specs=[pl.BlockSpec((rows, D), lambda j: (0, 0)),
                  pl.BlockSpec((D, tn), lambda j: (0, j)),
                  pl.BlockSpec((1, tn), lambda j: (0, j))],
        out_specs=pl.BlockSpec((rows, tn), lambda j: (0, j)),
        out_shape=jax.ShapeDtypeStruct((rows, n_out), F32),
        compiler_params=_cparams(("arbitrary",)),
        name="ada",
    )(cp, w, b.reshape(1, n_out))
    return out[:B]


def _modulate(x, nw, sh, sc):
    r = lax.rsqrt(jnp.mean(x * x, axis=-1, keepdims=True) + NORM_EPS)
    return (x * r) * nw * (1.0 + sc) + sh


def _inproj_kernel(x_ref, nw_ref, sh_ref, sc_ref, w_ref, o_ref, h_ref):
    @pl.when(pl.program_id(1) == 0)
    def _():
        h_ref[...] = _modulate(x_ref[...], nw_ref[...], sh_ref[...], sc_ref[...]).astype(BF16)

    o_ref[...] = jnp.dot(h_ref[...], w_ref[...], preferred_element_type=F32).astype(BF16)


def _inproj(x2, nw, sh, sc, w_slab, S):
    N, D = x2.shape
    tm = 512
    tn = SLAB_COLS // 3
    per_b = S // tm
    return pl.pallas_call(
        _inproj_kernel,
        grid=(N // tm, SLAB_COLS // tn),
        in_specs=[pl.BlockSpec((tm, D), lambda i, j: (i, 0)),
                  pl.BlockSpec((1, D), lambda i, j: (0, 0)),
                  pl.BlockSpec((None, 1, D), lambda i, j: (i // per_b, 0, 0)),
                  pl.BlockSpec((None, 1, D), lambda i, j: (i // per_b, 0, 0)),
                  pl.BlockSpec((D, tn), lambda i, j: (0, j))],
        out_specs=pl.BlockSpec((tm, tn), lambda i, j: (i, j)),
        out_shape=jax.ShapeDtypeStruct((N, SLAB_COLS), BF16),
        scratch_shapes=[pltpu.VMEM((tm, D), BF16)],
        compiler_params=_cparams(("parallel", "arbitrary")),
        name="inproj",
    )(x2, nw.reshape(1, D), sh.reshape(-1, 1, D), sc.reshape(-1, 1, D), w_slab)


def _cmp_kernel(r_ref, pe_ref, w1a_ref, w1b_ref, w2_ref, o_ref):
    r = r_ref[...]
    n_rows = r.shape[0]
    a = jnp.dot(r, w1a_ref[...], preferred_element_type=F32)
    b = jnp.dot(r, w1b_ref[...], preferred_element_type=F32)
    pe = pe_ref[...]
    half = pe.shape[1] // 2
    bias = (jnp.dot(pe[:, :half], w1a_ref[...].astype(F32), preferred_element_type=F32, precision=HIGHEST)
            + jnp.dot(pe[:, half:], w1b_ref[...].astype(F32), preferred_element_type=F32, precision=HIGHEST))
    hid = a + pltpu.roll(b, n_rows - 1, 0) + bias
    act = 0.5 * hid * (1.0 + jnp.tanh(0.7978845608028654 * (hid + 0.044715 * hid * hid * hid)))
    o_ref[...] = jnp.dot(act, w2_ref[...], preferred_element_type=F32, precision=HIGHEST)


def _compress(rows, pe, w1, w2):
    BG, n_rows, width = rows.shape
    hidden = w1.shape[1]
    w1a = w1[:width].astype(BF16)
    w1b = w1[width:].astype(BF16)
    return pl.pallas_call(
        _cmp_kernel,
        grid=(BG,),
        in_specs=[pl.BlockSpec((None, n_rows, width), lambda i: (i, 0, 0)),
                  pl.BlockSpec((1, 2 * width), lambda i: (0, 0)),
                  pl.BlockSpec((width, hidden), lambda i: (0, 0)),
                  pl.BlockSpec((width, hidden), lambda i: (0, 0)),
                  pl.BlockSpec((hidden, NSA_DIM), lambda i: (0, 0))],
        out_specs=pl.BlockSpec((None, n_rows, NSA_DIM), lambda i: (i, 0, 0)),
        out_shape=jax.ShapeDtypeStruct((BG, n_rows, NSA_DIM), F32),
        compiler_params=_cparams(("parallel",)),
        name="cmp",
    )(rows, pe.reshape(1, -1).astype(F32), w1a, w1b, w2.astype(F32))


def _da_kernel(slopes_ref, kmin_ref, qT_ref, k_ref, vT_ref, pos_ref, srow_ref, lq1_ref, lk1_ref, lq2_ref, lk2_ref, subw_ref,
               oT_ref, acc_ref, qa_ref, sa_ref, sb_ref, pa_ref, pb_ref, rel_ref, *, tq, tk, lam_init):
    h = pl.program_id(1)
    qi = pl.program_id(2)
    slope = slopes_ref[h]
    q0 = qi * tq
    w = 2 * tq
    nk = k_ref.shape[0] // tk
    qT = qT_ref[...]
    row = lax.broadcasted_iota(I32, qT.shape, 0)
    zero = jnp.zeros_like(qT)
    qa_ref[0:2 * DA_DIM, :] = jnp.concatenate(
        [jnp.where(row < DA_DIM, qT, zero), jnp.where(row >= DA_DIM, qT, zero)], axis=1)
    qa_ref[2 * DA_DIM:2 * DA_DIM + POS_ROWS, :] = jnp.concatenate([srow_ref[...]] * (w // 128), axis=1).astype(BF16)
    qa_ref[2 * DA_DIM + POS_ROWS:, :] = jnp.zeros((2 * DA_DIM - POS_ROWS, w), BF16)
    lane = lax.broadcasted_iota(I32, (tk, w), 1)
    rel_ref[...] = lax.broadcasted_iota(I32, (tk, w), 0) - jnp.where(lane >= tq, lane - tq, lane)

    def scores(kt, s_ref):
        k0 = kt * tk
        k_t = k_ref[pl.ds(pl.multiple_of(k0, tk), tk), :]
        s = jnp.dot(jnp.concatenate([k_t, pos_ref[...]], axis=1), qa_ref[...], preferred_element_type=F32)
        s = jnp.where(rel_ref[...] <= q0 - k0, s, NEG)
        s_ref[...] = s
        return jnp.max(s, axis=0, keepdims=True)

    def probs(s_ref, p_ref, mx, off, m_old):
        m_new = jnp.maximum(m_old, mx + off)
        p_ref[...] = jnp.exp(s_ref[...] - (m_new - off)).astype(BF16)
        return m_new, jnp.exp(m_old - m_new)

    def accumulate(kt, p_ref, alpha):
        acc_ref[...] = alpha * acc_ref[...] + jnp.dot(vT_ref[kt], p_ref[...], preferred_element_type=F32)

    n_tiles = (q0 + tq - 1) // tk + 1
    kt_min = kmin_ref[(pl.program_id(0) * pl.num_programs(1) + h) * pl.num_programs(2) + qi]
    acc_ref[...] = jnp.zeros_like(acc_ref)
    pb_ref[...] = jnp.zeros_like(pb_ref)
    mx_a = scores(kt_min, sa_ref)

    def pair(j, carry):
        m, mx_a, alpha_b = carry
        ta = kt_min + 2 * j
        tb = ta + 1
        off_a = slope * (ta * tk).astype(F32)
        off_b = jnp.where(tb < n_tiles, slope * (tb * tk).astype(F32), NEG)
        accumulate(jnp.maximum(ta - 1, 0), pb_ref, alpha_b)
        m, alpha_a = probs(sa_ref, pa_ref, mx_a, off_a, m)
        mx_b = scores(jnp.minimum(tb, nk - 1), sb_ref)
        accumulate(ta, pa_ref, alpha_a)
        m, alpha_b = probs(sb_ref, pb_ref, mx_b, off_b, m)
        mx_a = scores(jnp.minimum(ta + 2, nk - 1), sa_ref)
        return m, mx_a, alpha_b

    n_pairs = (n_tiles - kt_min + 1) // 2
    init = (jnp.full((1, w), NEG, F32), mx_a, jnp.ones((1, w), F32))
    _, _, alpha_b = lax.fori_loop(0, n_pairs, pair, init)
    accumulate(jnp.minimum(kt_min + 2 * n_pairs - 1, nk - 1), pb_ref, alpha_b)

    lam = (jnp.exp(jnp.sum(lq1_ref[...] * lk1_ref[...], axis=-1, keepdims=True))
           - jnp.exp(jnp.sum(lq2_ref[...] * lk2_ref[...], axis=-1, keepdims=True)) + lam_init)
    on = acc_ref[0:DA_VDIM, :] * (1.0 / acc_ref[DA_VDIM:DA_VDIM + 1, :])
    o = on[:, :tq] - lam * on[:, tq:]
    r = lax.rsqrt(jnp.mean(o * o, axis=0, keepdims=True) + DA_SUBLN_EPS)
    oT_ref[...] = (o * r * subw_ref[...] * (1.0 - lam_init)).astype(BF16)


EXP_UNDERFLOW = 104.0


def _first_live_tile(slopes_np, slab3, tq, tk):
    B, S, _ = slab3.shape
    H = DA_HEADS
    width = 2 * DA_DIM * H
    grp = jnp.asarray((np.arange(width)[:, None] // DA_DIM == np.arange(2 * H)[None, :]).astype(np.float32))

    def norms(c0):
        v = slab3[:, :, c0:c0 + width].astype(F32)
        return jnp.sqrt(jnp.einsum('bsc,cg->bsg', v * v, grp, precision=HIGHEST))

    qn = norms(COL_DAQ) * DA_DIM ** -0.5
    qn = jnp.max(qn.reshape(B, S // tq, tq, H, 2), axis=2).transpose(0, 2, 3, 1)
    kn = jnp.max(norms(COL_DAK), axis=1).reshape(B, H, 2)
    bound = jnp.max(qn * kn[..., None], axis=2) * 1.01 + 0.01
    q0 = (jnp.arange(S // tq, dtype=F32) * tq)[None, None, :]
    slope = jnp.asarray(slopes_np, F32)[None, :, None]
    first_key = q0 - (EXP_UNDERFLOW + 2.0 * bound) / slope
    kt = jnp.floor(first_key / tk).astype(I32)
    n_tiles = (jnp.arange(S // tq, dtype=I32) * tq + tq - 1) // tk + 1
    return jnp.clip(kt, 0, n_tiles[None, None, :] - 1).reshape(-1)


def _diff_attention(slopes, kmin, qT, slab3, vT, srows, lq1, lk1, lq2, lk2, subw, lam_init, tq, tk):
    B, _, S = qT.shape
    nk = S // tk
    assert tk % tq == 0 and S % tk == 0
    tile = tq
    kcol = COL_DAK // 128
    v_rows = vT.shape[3]
    vec = lambda a: a.reshape(1, DA_DIM).astype(F32)
    grid_spec = pltpu.PrefetchScalarGridSpec(
        num_scalar_prefetch=2,
        grid=(B, DA_HEADS, S // tile),
        in_specs=[pl.BlockSpec((None, 128, tile), lambda b, h, i, s, km: (b, h, i)),
                  pl.BlockSpec((None, S, 128), lambda b, h, i, s, km: (b, 0, kcol + h)),
                  pl.BlockSpec((None, None, nk, v_rows, tk), lambda b, h, i, s, km: (b, h, 0, 0, 0)),
                  pl.BlockSpec((tk, 128), lambda b, h, i, s, km: (0, 0)),
                  pl.BlockSpec((None, POS_ROWS, 128), lambda b, h, i, s, km: (h, 0, 0)),
                  pl.BlockSpec((1, DA_DIM), lambda b, h, i, s, km: (0, 0)),
                  pl.BlockSpec((1, DA_DIM), lambda b, h, i, s, km: (0, 0)),
                  pl.BlockSpec((1, DA_DIM), lambda b, h, i, s, km: (0, 0)),
                  pl.BlockSpec((1, DA_DIM), lambda b, h, i, s, km: (0, 0)),
                  pl.BlockSpec((DA_VDIM, 1), lambda b, h, i, s, km: (0, 0))],
        out_specs=pl.BlockSpec((None, 128, tile), lambda b, h, i, s, km: (b, h, i)),
        scratch_shapes=[pltpu.VMEM((v_rows, 2 * tq), F32), pltpu.VMEM((4 * DA_DIM, 2 * tq), BF16),
                        pltpu.VMEM((tk, 2 * tq), F32), pltpu.VMEM((tk, 2 * tq), F32),
                        pltpu.VMEM((tk, 2 * tq), BF16), pltpu.VMEM((tk, 2 * tq), BF16),
                        pltpu.VMEM((tk, 2 * tq), I32)],
    )
    return pl.pallas_call(
        functools.partial(_da_kernel, tq=tq, tk=tk, lam_init=lam_init),
        grid_spec=grid_spec,
        out_shape=jax.ShapeDtypeStruct((B, DA_HEADS * DA_VDIM, S), BF16),
        compiler_params=_cparams(("parallel", "parallel", "arbitrary")),
        name="diffattn",
    )(slopes, kmin, qT, slab3, vT, _pos_features(tk), srows, vec(lq1), vec(lk1), vec(lq2), vec(lk2),
      subw.reshape(DA_VDIM, 1).astype(F32))


def _nsa_kernel(slopes_ref, qT_ref, kc_ref, vcT_ref, ksa_ref, vsT_ref, kwa_ref, vwT_ref, wb_ref, gT_ref, mT_ref,
                grp_ref, ltri_ref, oT_ref, acc_ref, res_ref, base_ref, qa_ref, sa_ref, sb_ref, pa_ref, pb_ref,
                list_ref, slist_ref, lsem, *, t, tk, n_top):
    g = pl.program_id(1)
    qi = pl.program_id(2)
    q0 = qi * t
    w = NSA_HPG * t
    n_cp = kc_ref.shape[0]
    n_sel = mT_ref.shape[0]
    lane1 = lax.broadcasted_iota(I32, (1, w), 1)
    slope_row = jnp.zeros((1, w), F32)
    for hh in range(NSA_HPG):
        slope_row = jnp.where((lane1 >= hh * t) & (lane1 < (hh + 1) * t), slopes_ref[g * NSA_HPG + hh], slope_row)
    q_all = qT_ref[...]
    gates = jax.nn.sigmoid(gT_ref[...])
    key_off = lax.broadcasted_iota(I32, (tk, w), 0)
    t_loc = jnp.bitwise_and(lax.broadcasted_iota(I32, (tk, w), 1), t - 1)
    base_ref[...] = slope_row * key_off.astype(F32)

    c_idx = lax.broadcasted_iota(I32, (n_cp, w), 0)
    t_c = q0 + jnp.bitwise_and(lax.broadcasted_iota(I32, (n_cp, w), 1), t - 1)
    d_c = t_c - (c_idx * CMP_STRIDE + (CMP_BLOCK - 1))
    ok_c = d_c >= 0
    s = jnp.dot(kc_ref[...], q_all, preferred_element_type=F32) - slope_row * d_c.astype(F32)
    s = jnp.where(ok_c, s, NEG)
    e = jnp.exp(s - jnp.max(s, axis=0, keepdims=True))
    inv = 1.0 / jnp.sum(e, axis=0, keepdims=True)
    p = jnp.where(ok_c, e * inv, 0.0)
    res_ref[...] = gates[0:1, :] * jnp.dot(vcT_ref[...], p.astype(BF16), preferred_element_type=F32)
    p_sum = p[:, 0:t]
    for hh in range(1, NSA_HPG):
        p_sum = p_sum + p[:, hh * t:(hh + 1) * t]

    p_hi = p_sum.astype(BF16)
    p_lo = (p_sum - p_hi.astype(F32)).astype(BF16)
    mT = mT_ref[...]
    imp = jnp.dot(mT, p_hi, preferred_element_type=F32) + jnp.dot(mT, p_lo, preferred_element_type=F32)
    blk = lax.broadcasted_iota(I32, (n_sel, t), 0)
    t_s = q0 + lax.broadcasted_iota(I32, (n_sel, t), 1)
    cur = jnp.right_shift(t_s, int(math.log2(SEL_BLOCK)))
    forced = (blk == 0) | (blk == cur) | (blk == cur - 1)
    score = jnp.where(blk * SEL_BLOCK <= t_s, imp + jnp.where(forced, FORCE_BONUS, 0.0), NEG)
    blk_f = blk.astype(F32)
    sel = jnp.zeros((n_sel, t), F32)
    for _ in range(n_top):
        _, idx = _first_max(score, blk_f)
        hit = blk_f == idx
        sel = jnp.where(hit, 1.0, sel)
        score = jnp.where(hit, -jnp.inf, score)
    n_full = q0 // tk
    nt_pad = grp_ref.shape[0]
    cnt = jnp.broadcast_to(jnp.sum(sel, axis=1, keepdims=True), (n_sel, 128)).astype(BF16)
    tile_cnt = jnp.dot(grp_ref[...], cnt, preferred_element_type=F32)
    kt_i = lax.broadcasted_iota(I32, (nt_pad, 128), 0)
    active = (tile_cnt > 0.5) & (kt_i < n_full)
    act_f = jnp.where(active, 1.0, 0.0)
    before = jnp.dot(ltri_ref[...], act_f.astype(BF16), preferred_element_type=F32)
    lane_j = lax.broadcasted_iota(I32, (nt_pad, 128), 1)
    slot_hit = active & (before == lane_j.astype(F32))
    tiles_row = jnp.sum(jnp.where(slot_hit, kt_i.astype(F32), 0.0), axis=0, keepdims=True)
    n_act_row = jnp.sum(act_f, axis=0, keepdims=True)
    lane8 = lax.broadcasted_iota(I32, (8, 128), 1)
    list_ref[...] = jnp.where(lane8 == 127, n_act_row, tiles_row).astype(I32)
    list_copy = pltpu.make_async_copy(list_ref, slist_ref, lsem)
    list_copy.start()

    selb = jnp.where(sel > 0.5, 0.0, NEG)
    row64 = lax.broadcasted_iota(I32, (NSA_DIM, w), 0)
    qa_ref[0:NSA_DIM, :] = q_all
    qa_ref[NSA_DIM:2 * NSA_DIM, :] = jnp.where(row64 == 0, NEG, 0.0).astype(BF16)
    qa_ref[2 * NSA_DIM:2 * NSA_DIM + n_sel, :] = jnp.concatenate([selb] * NSA_HPG, axis=1).astype(BF16)
    if n_sel < SEL_COLS:
        qa_ref[2 * NSA_DIM + n_sel:, :] = jnp.zeros((SEL_COLS - n_sel, w), BF16)

    n_wt = (WINDOW + t) // t
    kw_t = kwa_ref[pl.ds(pl.multiple_of(q0, t), WINDOW + t), :]
    s = jnp.dot(kw_t, qa_ref[0:2 * NSA_DIM, :], preferred_element_type=F32) + wb_ref[...]
    p = jnp.exp(s - jnp.max(s, axis=0, keepdims=True))
    inv = 1.0 / jnp.sum(p, axis=0, keepdims=True)
    p = p.astype(BF16)
    o_w = jnp.dot(vwT_ref[qi], p[0:t, :], preferred_element_type=F32)
    for c in range(1, n_wt):
        o_w = o_w + jnp.dot(vwT_ref[qi + c], p[c * t:(c + 1) * t, :], preferred_element_type=F32)
    res_ref[...] = res_ref[...] + gates[2:3, :] * (o_w * inv)

    def scores(kt, s_ref):
        k_t = ksa_ref[pl.ds(pl.multiple_of(kt * tk, tk), tk), :]
        s = jnp.dot(k_t, qa_ref[...], preferred_element_type=F32) + base_ref[...]
        s_ref[...] = s
        return jnp.max(s, axis=0, keepdims=True)

    def probs(s_ref, p_ref, mx, off, m_old, l_old):
        m_new = jnp.maximum(m_old, mx + off)
        p = jnp.exp(s_ref[...] - (m_new - off))
        alpha = jnp.exp(m_old - m_new)
        p_ref[...] = p.astype(BF16)
        return m_new, alpha * l_old + jnp.sum(p, axis=0, keepdims=True), alpha

    def accumulate(kt, p_ref, alpha):
        acc_ref[...] = alpha * acc_ref[...] + jnp.dot(vsT_ref[kt], p_ref[...], preferred_element_type=F32)

    list_copy.wait()
    n_act = slist_ref[0, 127]
    mx_a = scores(slist_ref[0, 0], sa_ref)

    kd = n_full * tk
    s = jnp.dot(ksa_ref[pl.ds(pl.multiple_of(kd, tk), tk), :], qa_ref[...], preferred_element_type=F32) + base_ref[...]
    s = jnp.where(key_off - t_loc <= q0 - kd, s, NEG)
    off_d = slope_row * kd.astype(F32)
    m = jnp.max(s, axis=0, keepdims=True) + off_d
    p = jnp.exp(s - (m - off_d))
    l = jnp.sum(p, axis=0, keepdims=True)
    acc_ref[...] = jnp.dot(vsT_ref[n_full], p.astype(BF16), preferred_element_type=F32)
    pb_ref[...] = jnp.zeros_like(pb_ref)

    def pair(j, carry):
        m, l, mx_a, alpha_b, tb_prev = carry
        ta = slist_ref[0, 2 * j]
        tb = slist_ref[0, 2 * j + 1]
        off_a = slope_row * (ta * tk).astype(F32)
        off_b = jnp.where(2 * j + 1 < n_act, slope_row * (tb * tk).astype(F32), NEG)
        accumulate(tb_prev, pb_ref, alpha_b)
        m, l, alpha_a = probs(sa_ref, pa_ref, mx_a, off_a, m, l)
        mx_b = scores(tb, sb_ref)
        accumulate(ta, pa_ref, alpha_a)
        m, l, alpha_b = probs(sb_ref, pb_ref, mx_b, off_b, m, l)
        mx_a = scores(slist_ref[0, 2 * j + 2], sa_ref)
        return m, l, mx_a, alpha_b, tb

    n_pairs = (n_act + 1) // 2
    m, l, _, alpha_b, tb_last = lax.fori_loop(0, n_pairs, pair, (m, l, mx_a, jnp.ones((1, w), F32), jnp.int32(0)))
    accumulate(tb_last, pb_ref, alpha_b)
    oT_ref[...] = (res_ref[...] + gates[1:2, :] * (acc_ref[...] * (1.0 / l))).astype(BF16)


def _nsa_attention(slopes, qT, kc, vcT, ksa, vsT, kwa, vwT, wb, gT, mT, t, tk):
    B, G, nq, _, w = qT.shape
    S = nq * t
    n_cp = kc.shape[2]
    n_sel = S // SEL_BLOCK
    n_top = min(SEL_TOPK, n_sel)
    assert tk % t == 0 and S % tk == 0 and t & (t - 1) == 0 and WINDOW % t == 0 and n_sel <= SEL_COLS
    n_tiles = S // tk
    nt_pad = -(-n_tiles // 16) * 16
    assert nt_pad + 2 < 127
    grp = jnp.asarray((np.arange(n_sel)[None, :] // (tk // SEL_BLOCK) == np.arange(nt_pad)[:, None])
                      .astype(np.float32), BF16)
    ltri = jnp.asarray((np.arange(nt_pad)[None, :] < np.arange(nt_pad)[:, None]).astype(np.float32), BF16)
    fixed = lambda b, g, i, s: (b, g, 0, 0)
    fixed5 = lambda b, g, i, s: (b, g, 0, 0, 0)
    tile5 = lambda b, g, i, s: (b, g, i, 0, 0)
    grid_spec = pltpu.PrefetchScalarGridSpec(
        num_scalar_prefetch=1,
        grid=(B, G, nq),
        in_specs=[pl.BlockSpec((None, None, None, NSA_DIM, w), tile5),
                  pl.BlockSpec((None, None, n_cp, NSA_DIM), fixed),
                  pl.BlockSpec((None, None, NSA_DIM, n_cp), fixed),
                  pl.BlockSpec((None, None, S, ksa.shape[-1]), fixed),
                  pl.BlockSpec((None, None, S // tk, NSA_DIM, tk), fixed5),
                  pl.BlockSpec((None, None, S + WINDOW, 2 * NSA_DIM), fixed),
                  pl.BlockSpec((None, None, (S + WINDOW) // t, NSA_DIM, t), fixed5),
                  pl.BlockSpec((None, WINDOW + t, w), lambda b, g, i, s: (g, 0, 0)),
                  pl.BlockSpec((None, None, None, 3, w), tile5),
                  pl.BlockSpec((n_sel, n_cp), lambda b, g, i, s: (0, 0)),
                  pl.BlockSpec((nt_pad, n_sel), lambda b, g, i, s: (0, 0)),
                  pl.BlockSpec((nt_pad, nt_pad), lambda b, g, i, s: (0, 0))],
        out_specs=pl.BlockSpec((None, None, None, NSA_DIM, w), tile5),
        scratch_shapes=[pltpu.VMEM((NSA_DIM, w), F32),
                        pltpu.VMEM((NSA_DIM, w), F32),
                        pltpu.VMEM((tk, w), F32),
                        pltpu.VMEM((2 * NSA_DIM + SEL_COLS, w), BF16),
                        pltpu.VMEM((tk, w), F32), pltpu.VMEM((tk, w), F32),
                        pltpu.VMEM((tk, w), BF16), pltpu.VMEM((tk, w), BF16),
                        pltpu.VMEM((8, 128), I32), pltpu.SMEM((8, 128), I32), pltpu.SemaphoreType.DMA(())],
    )
    return pl.pallas_call(
        functools.partial(_nsa_kernel, t=t, tk=tk, n_top=n_top),
        grid_spec=grid_spec,
        out_shape=jax.ShapeDtypeStruct((B, G, nq, NSA_DIM, w), BF16),
        compiler_params=_cparams(("parallel", "parallel", "arbitrary")),
        name="nsa",
    )(slopes, qT, kc, vcT, ksa, vsT, kwa, vwT, wb, gT, mT, grp, ltri)


def _pack_halves(v):
    c = v.shape[1] // 2
    lo = lax.bitcast_convert_type(v[:, :c].astype(BF16).astype(F32), U32)
    hi = lax.bitcast_convert_type(v[:, c:].astype(BF16).astype(F32), U32)
    return hi | (lo >> 16)


def _unpack_halves(u):
    lo = lax.bitcast_convert_type(u << 16, F32)
    hi = lax.bitcast_convert_type(u & jnp.uint32(0xFFFF0000), F32)
    return lo, hi


def _merge_kernel(oa_ref, ob_ref, ga_ref, gb_ref, x_ref, g1_ref, nw_ref, sh_ref, sc_ref,
                  wa_ref, wb_ref, wo_ref, wr_ref, x1_ref, h2_ref, lg_ref):
    ya = jnp.dot(oa_ref[...], wa_ref[...], preferred_element_type=F32)
    yb = jnp.dot(ob_ref[...], wb_ref[...], preferred_element_type=F32)
    merged = (jax.nn.sigmoid(ga_ref[...].astype(F32)) * ya
              + jax.nn.sigmoid(gb_ref[...].astype(F32)) * yb).astype(BF16)
    mix = jnp.dot(merged, wo_ref[...], preferred_element_type=F32)
    x1 = x_ref[...] + g1_ref[...] * mix
    x1_ref[...] = x1
    h2 = _modulate(x1, nw_ref[...], sh_ref[...], sc_ref[...])
    h2_ref[...] = _pack_halves(h2)
    lg_ref[...] = jnp.dot(h2, wr_ref[...], preferred_element_type=F32, precision=HIGHEST)


def _merge(oa, ob, slab, x2, g1, nw, sh, sc, wa, wb, wo, wr, S):
    N, D = x2.shape
    tm = 256
    per_b = S // tm
    mcol = COL_MERGE // D
    row = lambda i: (i, 0)
    full = lambda i: (0, 0)
    perb = lambda i: (i // per_b, 0, 0)
    n_lg = wr.shape[1]
    return pl.pallas_call(
        _merge_kernel,
        grid=(N // tm,),
        in_specs=[pl.BlockSpec((tm, oa.shape[1]), row),
                  pl.BlockSpec((tm, ob.shape[1]), row),
                  pl.BlockSpec((tm, D), lambda i: (i, mcol)),
                  pl.BlockSpec((tm, D), lambda i: (i, mcol + 1)),
                  pl.BlockSpec((tm, D), row),
                  pl.BlockSpec((None, 1, D), perb),
                  pl.BlockSpec((1, D), full),
                  pl.BlockSpec((None, 1, D), perb),
                  pl.BlockSpec((None, 1, D), perb),
                  pl.BlockSpec(wa.shape, full),
                  pl.BlockSpec(wb.shape, full),
                  pl.BlockSpec(wo.shape, full),
                  pl.BlockSpec(wr.shape, full)],
        out_specs=[pl.BlockSpec((tm, D), row), pl.BlockSpec((tm, D // 2), row), pl.BlockSpec((tm, n_lg), row)],
        out_shape=[jax.ShapeDtypeStruct((N, D), F32), jax.ShapeDtypeStruct((N, D // 2), U32),
                   jax.ShapeDtypeStruct((N, n_lg), F32)],
        compiler_params=_cparams(("parallel",)),
        name="merge",
    )(oa, ob, slab, slab, x2, g1.reshape(-1, 1, D), nw.reshape(1, D), sh.reshape(-1, 1, D),
      sc.reshape(-1, 1, D), wa, wb, wo, wr)


def _first_max(v, idx):
    mx = jnp.max(v, axis=0, keepdims=True)
    first = jnp.min(jnp.where(v == mx, idx, float(v.shape[0])), axis=0, keepdims=True)
    return mx, first


def _route_kernel(lg_ref, b_ref, e_ref, w_ref):
    lg = lg_ref[...]
    tn = lg.shape[1]
    per_g = N_EXPERTS // N_EXPERT_GROUPS
    scores = jax.nn.sigmoid(lg)
    biased = scores + b_ref[...]
    gi = lax.broadcasted_iota(I32, (N_EXPERT_GROUPS, tn), 0).astype(F32)
    gscore = jnp.zeros((N_EXPERT_GROUPS, tn), F32)
    for g in range(N_EXPERT_GROUPS):
        grp = biased[g * per_g:(g + 1) * per_g, :]
        m1, f1 = _first_max(grp, gi)
        m2 = jnp.max(jnp.where(gi == f1, -jnp.inf, grp), axis=0, keepdims=True)
        gscore = jnp.where(gi == g, m1 + m2, gscore)
    gsel = jnp.zeros((N_EXPERT_GROUPS, tn), F32)
    for _ in range(TOPK_EXPERT_GROUPS):
        _, f = _first_max(gscore, gi)
        hit = gi == f
        gsel = jnp.where(hit, 1.0, gsel)
        gscore = jnp.where(hit, -jnp.inf, gscore)
    ei = lax.broadcasted_iota(I32, (N_EXPERTS, tn), 0).astype(F32)
    emask = jnp.zeros((N_EXPERTS, tn), F32)
    for g in range(N_EXPERT_GROUPS):
        in_g = (ei >= g * per_g) & (ei < (g + 1) * per_g)
        emask = jnp.where(in_g, gsel[g:g + 1, :], emask)
    masked = jnp.where(emask > 0.5, biased, NEG)
    e_out = jnp.zeros((MOE_TOPK, tn), F32)
    w_out = jnp.zeros((MOE_TOPK, tn), F32)
    for r in range(MOE_TOPK):
        _, f = _first_max(masked, ei)
        hit = ei == f
        wv = jnp.sum(jnp.where(hit, scores, 0.0), axis=0, keepdims=True)
        e_out = jnp.where(gi == r, f, e_out)
        w_out = jnp.where(gi == r, wv, w_out)
        masked = jnp.where(hit, -jnp.inf, masked)
    e_ref[...] = e_out.astype(I32)
    w_ref[...] = w_out / jnp.sum(w_out, axis=0, keepdims=True) * ROUTED_SCALE


def _route(lgT, rb):
    E, N = lgT.shape
    tn = 512
    return pl.pallas_call(
        _route_kernel,
        grid=(N // tn,),
        in_specs=[pl.BlockSpec((E, tn), lambda i: (0, i)), pl.BlockSpec((E, 1), lambda i: (0, 0))],
        out_specs=[pl.BlockSpec((MOE_TOPK, tn), lambda i: (0, i)), pl.BlockSpec((MOE_TOPK, tn), lambda i: (0, i))],
        out_shape=[jax.ShapeDtypeStruct((MOE_TOPK, N), I32), jax.ShapeDtypeStruct((MOE_TOPK, N), F32)],
        compiler_params=_cparams(("parallel",)),
        name="route",
    )(lgT, rb.reshape(E, 1).astype(F32))


def _slots_kernel(e_ref, tri_ref, lt_ref, dest_ref, cnt_ref, carry_ref, *, rb):
    phase = pl.program_id(0)
    i = pl.program_id(1)
    e = e_ref[...]
    tn = e.shape[1]
    ei = lax.broadcasted_iota(I32, (N_EXPERTS, tn), 0)

    @pl.when((phase == 0) & (i == 0))
    def _():
        carry_ref[...] = jnp.zeros_like(carry_ref)

    @pl.when(phase == 0)
    def _():
        tot = jnp.zeros((N_EXPERTS, 1), F32)
        for k in range(MOE_TOPK):
            oh = jnp.where(ei == e[k:k + 1, :], 1.0, 0.0)
            tot = tot + jnp.sum(oh, axis=1, keepdims=True)
        carry_ref[...] = carry_ref[...] + tot
        dest_ref[...] = jnp.zeros_like(dest_ref)

    @pl.when((phase == 1) & (i == 0))
    def _():
        cnt = carry_ref[...]
        cnt_ref[...] = cnt
        padded = jnp.broadcast_to(jnp.ceil(cnt / rb) * rb, (N_EXPERTS, 128))
        first = jnp.dot(lt_ref[...], padded, preferred_element_type=F32, precision=HIGHEST)
        carry_ref[...] = first[:, :1]

    @pl.when(phase == 1)
    def _():
        carry = carry_ref[...]
        ki = lax.broadcasted_iota(I32, (MOE_TOPK, tn), 0)
        dest = jnp.zeros((MOE_TOPK, tn), F32)
        for k in range(MOE_TOPK):
            hit = ei == e[k:k + 1, :]
            oh = jnp.where(hit, 1.0, 0.0)
            before = jnp.dot(oh.astype(BF16), tri_ref[...], preferred_element_type=F32)
            row = jnp.sum(jnp.where(hit, before + carry, 0.0), axis=0, keepdims=True)
            dest = jnp.where(ki == k, row, dest)
            carry = carry + jnp.sum(oh, axis=1, keepdims=True)
        carry_ref[...] = carry
        dest_ref[...] = dest.astype(I32)


def _slots(top_eT, rb):
    K, N = top_eT.shape
    tn = 512
    idx = np.arange(tn)
    tri = jnp.asarray((idx[:, None] < idx[None, :]).astype(np.float32), BF16)
    e_idx = np.arange(N_EXPERTS)
    lt = jnp.asarray((e_idx[None, :] < e_idx[:, None]).astype(np.float32), F32)
    return pl.pallas_call(
        functools.partial(_slots_kernel, rb=rb),
        grid=(2, N // tn),
        in_specs=[pl.BlockSpec((K, tn), lambda p, i: (0, i)),
                  pl.BlockSpec((tn, tn), lambda p, i: (0, 0)),
                  pl.BlockSpec((N_EXPERTS, N_EXPERTS), lambda p, i: (0, 0))],
        out_specs=[pl.BlockSpec((K, tn), lambda p, i: (0, i * p)),
                   pl.BlockSpec((N_EXPERTS, 1), lambda p, i: (0, 0))],
        out_shape=[jax.ShapeDtypeStruct((K, N), I32), jax.ShapeDtypeStruct((N_EXPERTS, 1), F32)],
        scratch_shapes=[pltpu.VMEM((N_EXPERTS, 1), F32)],
        compiler_params=_cparams(("arbitrary", "arbitrary")),
        name="slots",
    )(top_eT, tri, lt)


def _dispatch_kernel(dest_ref, h_ref, xs_in_ref, xs_ref, sem, *, tt):
    del xs_in_ref

    def body(n, c):
        src = h_ref.at[pl.ds(n, 1), :]
        for k in range(MOE_TOPK):
            d = dest_ref[0, 0, k * tt + n]
            pltpu.make_async_copy(src, xs_ref.at[pl.ds(d, 1), :], sem).start(priority=k % 2)
        return c

    lax.fori_loop(0, tt, body, 0)
    for k in range(MOE_TOPK):
        pltpu.make_async_copy(h_ref, xs_ref.at[pl.ds(0, tt), :], sem).wait()


def _dispatch(dest_tiles, h2, n_rows, tt):
    N, D = h2.shape
    xs0 = jnp.zeros((n_rows, D), h2.dtype)
    return pl.pallas_call(
        functools.partial(_dispatch_kernel, tt=tt),
        grid=(N // tt,),
        in_specs=[pl.BlockSpec((1, 1, MOE_TOPK * tt), lambda i: (i, 0, 0), memory_space=pltpu.SMEM),
                  pl.BlockSpec((tt, D), lambda i: (i, 0)),
                  pl.BlockSpec(memory_space=pl.ANY)],
        out_specs=pl.BlockSpec(memory_space=pl.ANY),
        out_shape=jax.ShapeDtypeStruct((n_rows, D), h2.dtype),
        scratch_shapes=[pltpu.SemaphoreType.DMA(())],
        input_output_aliases={2: 0},
        compiler_params=_cparams(("arbitrary",)),
        name="dispatch",
    )(dest_tiles, h2, xs0)


def _experts_kernel(be_ref, x_ref, wgu_ref, wd_ref, y_ref):
    del be_ref
    lo, hi = _unpack_halves(x_ref[...])
    x = jnp.concatenate([lo.astype(BF16), hi.astype(BF16)], axis=1)
    gu = jnp.dot(x, wgu_ref[...], preferred_element_type=F32)
    hdim = gu.shape[1] // 2
    g = gu[:, :hdim]
    act = (g * jax.nn.sigmoid(g) * gu[:, hdim:]).astype(BF16)
    y_ref[...] = _pack_halves(jnp.dot(act, wd_ref[...], preferred_element_type=F32))


def _experts(blk_e, xs, wgu, wd, rb):
    n_rows, D = xs.shape
    grid_spec = pltpu.PrefetchScalarGridSpec(
        num_scalar_prefetch=1,
        grid=(n_rows // rb,),
        in_specs=[pl.BlockSpec((rb, D), lambda i, be: (i, 0)),
                  pl.BlockSpec((None,) + wgu.shape[1:], lambda i, be: (be[i], 0, 0)),
                  pl.BlockSpec((None,) + wd.shape[1:], lambda i, be: (be[i], 0, 0))],
        out_specs=pl.BlockSpec((rb, D), lambda i, be: (i, 0)),
    )
    return pl.pallas_call(
        _experts_kernel,
        grid_spec=grid_spec,
        out_shape=jax.ShapeDtypeStruct((n_rows, D), U32),
        compiler_params=_cparams(("arbitrary",)),
        name="experts",
    )(blk_e, xs, wgu, wd)


def _combine_kernel(dest_ref, dnext_ref, ys_ref, w_ref, h_ref, x1_ref, g2_ref, fw_ref, sgu_ref, sd_ref, o_ref,
                    buf_ref, sem, *, tt):
    i = pl.program_id(0)
    slot = i % 2

    def gather(idx_ref, s):
        def body(n, c):
            for k in range(MOE_TOPK):
                d = idx_ref[0, 0, k * tt + n]
                pltpu.make_async_copy(ys_ref.at[pl.ds(d, 1), :], buf_ref.at[s, k, pl.ds(n, 1), :],
                                      sem.at[s]).start(priority=k % 2)
            return c

        lax.fori_loop(0, tt, body, 0)

    @pl.when(i == 0)
    def _():
        gather(dest_ref, 0)

    @pl.when(i + 1 < pl.num_programs(0))
    def _():
        gather(dnext_ref, 1 - slot)

    buf_ref = buf_ref.at[slot]
    sem = sem.at[slot]
    h_lo, h_hi = _unpack_halves(h_ref[...])
    h = jnp.concatenate([h_lo.astype(BF16), h_hi.astype(BF16)], axis=1)
    gu = jnp.dot(h, sgu_ref[...], preferred_element_type=F32)
    hdim = gu.shape[1] // 2
    g = gu[:, :hdim]
    ffn = jnp.dot((g * jax.nn.sigmoid(g) * gu[:, hdim:]).astype(BF16), sd_ref[...], preferred_element_type=F32)

    for k in range(MOE_TOPK):
        pltpu.make_async_copy(ys_ref.at[pl.ds(0, tt), :], buf_ref.at[k], sem).wait()
    w = w_ref[...]
    r_lo = jnp.zeros(h_lo.shape, F32)
    r_hi = jnp.zeros(h_hi.shape, F32)
    for k in range(MOE_TOPK):
        y_lo, y_hi = _unpack_halves(buf_ref[k])
        r_lo = r_lo + w[:, k:k + 1] * y_lo
        r_hi = r_hi + w[:, k:k + 1] * y_hi
    ffn = ffn + jnp.concatenate([r_lo, r_hi], axis=1)
    x2 = x1_ref[...] + g2_ref[...] * ffn
    r = lax.rsqrt(jnp.mean(x2 * x2, axis=-1, keepdims=True) + NORM_EPS)
    o_ref[...] = x2 * r * fw_ref[...]


def _combine(dest_tiles, ys, top_w, h2p, x1, g2, fw, sgu, sd, S, tt):
    N, D = x1.shape
    per_b = S // tt
    row = lambda i: (i, 0)
    full = lambda i: (0, 0)
    n_steps = N // tt
    return pl.pallas_call(
        functools.partial(_combine_kernel, tt=tt),
        grid=(n_steps,),
        in_specs=[pl.BlockSpec((1, 1, MOE_TOPK * tt), lambda i: (i, 0, 0), memory_space=pltpu.SMEM),
                  pl.BlockSpec((1, 1, MOE_TOPK * tt), lambda i: (jnp.minimum(i + 1, n_steps - 1), 0, 0),
                               memory_space=pltpu.SMEM),
                  pl.BlockSpec(memory_space=pl.ANY),
                  pl.BlockSpec((tt, MOE_TOPK), row),
                  pl.BlockSpec((tt, D // 2), row),
                  pl.BlockSpec((tt, D), row),
                  pl.BlockSpec((None, 1, D), lambda i: (i // per_b, 0, 0)),
                  pl.BlockSpec((1, D), full),
                  pl.BlockSpec(sgu.shape, full),
                  pl.BlockSpec(sd.shape, full)],
        out_specs=pl.BlockSpec((tt, D), row),
        out_shape=jax.ShapeDtypeStruct((N, D), F32),
        scratch_shapes=[pltpu.VMEM((2, MOE_TOPK, tt, D // 2), U32), pltpu.SemaphoreType.DMA((2,))],
        compiler_params=_cparams(("arbitrary",)),
        name="combine",
    )(dest_tiles, dest_tiles, ys, top_w, h2p, x1, g2.reshape(-1, 1, D), fw.reshape(1, D), sgu, sd)


def _cmp_to_sel_T(n_cp, n_sel):
    c0 = np.arange(n_cp)[None, :] * CMP_STRIDE
    s0 = np.arange(n_sel)[:, None] * SEL_BLOCK
    ov = np.minimum(c0 + CMP_BLOCK, s0 + SEL_BLOCK) - np.maximum(c0, s0)
    m = np.clip(ov, 0, None).astype(np.float32) / CMP_BLOCK
    m[:, n_cp - 1] = 0.0
    return jnp.asarray(m, BF16)


def _slab_weights(w_in):
    D = w_in.shape[0]
    sizes = [1024, 1024, 1024, 512, 128, 128, 128, 128, 128, 128, 24, 2048]
    offs = np.concatenate([[0], np.cumsum(sizes)])
    part = lambda i: w_in[:, offs[i]:offs[i + 1]]
    pieces = [part(0), part(1), part(2), part(11), part(3)] + [part(i) for i in range(4, 10)] + [part(10)]
    w = jnp.concatenate(pieces, axis=1)
    return jnp.pad(w, ((0, 0), (0, SLAB_COLS - w.shape[1]))).astype(BF16)


def _layer(x, c, ada_w, ada_b, norm1_w, w_in, da_lq1, da_lk1, da_lq2, da_lk2, da_subln_w,
           cmp_k_pe, cmp_k_w1, cmp_k_w2, cmp_v_pe, cmp_v_w1, cmp_v_w2, w_da_out, w_nsa_out, w_o,
           norm2_w, router_w, router_b, exp_w_gate, exp_w_up, exp_w_down,
           sh_w_gate, sh_w_up, sh_w_down, final_norm_w, lam_init):
    B, S, D = x.shape
    N = B * S
    G, HPG = NSA_GROUPS, NSA_HPG
    x2 = x.reshape(N, D)

    mod = _ada(c, ada_w, ada_b)
    sh1, sc1, g1, sh2, sc2, g2 = jnp.split(mod, 6, axis=-1)

    slab = _inproj(x2, norm1_w, sh1, sc1, _slab_weights(w_in), S)
    slab3 = slab.reshape(B, S, SLAB_COLS)

    i_all = np.arange(1, DA_HEADS + NSA_HEADS + 1, dtype=np.float32)
    slopes = (2.0 ** (-8.0 * i_all / (DA_HEADS + NSA_HEADS))).astype(np.float32)
    slopes_a = jnp.asarray(slopes[0::2])
    slopes_b = jnp.asarray(slopes[1::2])

    ta_q, ta_k = min(512, S), min(512, S)
    qT = (slab3[:, :, COL_DAQ:COL_DAQ + 1024] * jnp.asarray(DA_DIM ** -0.5, BF16)).transpose(0, 2, 1)
    vT = slab3[:, :, COL_DAV:COL_DAV + 1024].reshape(B, S // ta_k, ta_k, DA_HEADS, DA_VDIM).transpose(0, 3, 1, 4, 2)
    kmin = _first_live_tile(slopes[0::2], slab3, ta_q, ta_k)
    oaT = _diff_attention(slopes_a, kmin, qT, slab3, _with_ones_row(vT), _slope_rows(slopes[0::2]), da_lq1, da_lk1,
                          da_lq2, da_lk2, da_subln_w, lam_init, ta_q, ta_k)
    oa = oaT.transpose(0, 2, 1).reshape(N, DA_HEADS * DA_VDIM)

    tb, tb_k = 128, min(256, S)
    n_cp = S // CMP_STRIDE
    n_sel = S // SEL_BLOCK

    def kv_groups(j):
        c0 = COL_KV6 + 128 * j
        return slab3[:, :, c0:c0 + 128].reshape(B, S, G, NSA_DIM).transpose(0, 2, 1, 3)

    def kv_tiles_T(a, rows):
        return a.reshape(B, G, S // rows, rows, NSA_DIM).transpose(0, 1, 2, 4, 3)

    ck_rows = kv_groups(0).reshape(B * G, n_cp, CMP_STRIDE * NSA_DIM)
    cv_rows = kv_groups(1).reshape(B * G, n_cp, CMP_STRIDE * NSA_DIM)
    kc = _compress(ck_rows, cmp_k_pe, cmp_k_w1, cmp_k_w2).reshape(B, G, n_cp, NSA_DIM).astype(BF16)
    vc = _compress(cv_rows, cmp_v_pe, cmp_v_w1, cmp_v_w2).reshape(B, G, n_cp, NSA_DIM).astype(BF16)
    nq = (slab3[:, :, COL_NSAQ:COL_NSAQ + 512] * jnp.asarray(NSA_DIM ** -0.5, BF16))
    nqT = nq.reshape(B, S // tb, tb, G, HPG, NSA_DIM).transpose(0, 3, 1, 5, 4, 2).reshape(B, G, S // tb, NSA_DIM, HPG * tb)
    gT = (slab3[:, :, COL_NSAG:COL_NSAG + 3 * NSA_HEADS].astype(F32).reshape(B, S // tb, tb, G, HPG, 3)
          .transpose(0, 3, 1, 5, 4, 2).reshape(B, G, S // tb, 3, HPG * tb))
    onehot = jnp.asarray((np.arange(S)[:, None] // SEL_BLOCK == np.arange(SEL_COLS)[None, :]).astype(np.float32), BF16)
    zeros64 = jnp.zeros((B, G, S, NSA_DIM), BF16)
    ksa = jnp.concatenate([kv_groups(2), zeros64, jnp.broadcast_to(onehot, (B, G, S, SEL_COLS))], axis=-1)
    pad_rows = jnp.zeros((B, G, WINDOW, 2 * NSA_DIM), BF16).at[..., NSA_DIM].set(1.0)
    kwa = jnp.concatenate([pad_rows, jnp.concatenate([kv_groups(4), zeros64], axis=-1)], axis=2)
    vw_pad = jnp.pad(kv_groups(5), ((0, 0), (0, 0), (WINDOW, 0), (0, 0)))
    vwT = vw_pad.reshape(B, G, (S + WINDOW) // tb, tb, NSA_DIM).transpose(0, 1, 2, 4, 3)
    r_w = np.arange(WINDOW + tb)[:, None]
    lane_w = np.arange(HPG * tb)[None, :]
    d_w = (WINDOW + lane_w % tb - r_w).astype(np.float32)
    slope_w = slopes[1::2].reshape(G, 1, HPG)[:, :, lane_w[0] // tb]
    wb = jnp.asarray(np.where((d_w >= 0) & (d_w < WINDOW), -slope_w * d_w[None], np.float32(NEG)).astype(np.float32))
    obT = _nsa_attention(slopes_b, nqT, kc, vc.transpose(0, 1, 3, 2), ksa, kv_tiles_T(kv_groups(3), tb_k),
                         kwa, vwT, wb, gT, _cmp_to_sel_T(n_cp, n_sel), tb, tb_k)
    ob = (obT.reshape(B, G, S // tb, NSA_DIM, HPG, tb).transpose(0, 2, 5, 1, 4, 3)
          .reshape(N, NSA_HEADS * NSA_DIM))

    x1, h2, logits = _merge(oa, ob, slab, x2, g1, norm2_w, sh2, sc2, w_da_out.astype(BF16),
                            w_nsa_out.astype(BF16), w_o.astype(BF16), router_w.astype(F32), S)

    rb = 512
    tt = 128
    top_eT, top_wT = _route(logits.T, router_b)
    destT, counts = _slots(top_eT, rb)
    n_rows = ((N * MOE_TOPK + N_EXPERTS * (rb - 1) + rb - 1) // rb) * rb
    padded = (jnp.ceil(counts[:, 0] / rb) * rb).astype(I32)
    pend = jnp.cumsum(padded)
    blk_start = jnp.arange(n_rows // rb, dtype=I32) * rb
    blk_e = jnp.minimum(jnp.sum((pend[None, :] <= blk_start[:, None]).astype(I32), axis=1), N_EXPERTS - 1)
    dest_tiles = destT.reshape(MOE_TOPK, N // tt, tt).transpose(1, 0, 2).reshape(N // tt, 1, MOE_TOPK * tt)
    xs = _dispatch(dest_tiles, h2, n_rows, tt)
    wgu = jnp.concatenate([exp_w_gate, exp_w_up], axis=-1).astype(BF16)
    ys = _experts(blk_e, xs, wgu, exp_w_down.astype(BF16), rb)
    sgu = jnp.concatenate([sh_w_gate, sh_w_up], axis=-1).astype(BF16)
    out = _combine(dest_tiles, ys, top_wT.T, h2, x1, g2, final_norm_w, sgu, sh_w_down.astype(BF16), S, tt)
    return out.reshape(B, S, D)


def kernel(x, c, ada_w, ada_b, norm1_w, w_in, da_lq1, da_lk1, da_lq2, da_lk2, da_subln_w, cmp_k_pe, cmp_k_w1, cmp_k_w2, cmp_v_pe, cmp_v_w1, cmp_v_w2, w_da_out, w_nsa_out, w_o, norm2_w, router_w, router_b, exp_w_gate, exp_w_up, exp_w_down, sh_w_gate, sh_w_up, sh_w_down, final_norm_w):
    depth = ada_w.shape[0]
    assert depth == 1, "one decoder layer"
    lam_init = 0.8 - 0.6 * math.exp(-0.3 * 0)
    return _layer(x, c, ada_w[0], ada_b[0], norm1_w[0], w_in[0], da_lq1[0], da_lk1[0], da_lq2[0], da_lk2[0],
                  da_subln_w[0], cmp_k_pe[0], cmp_k_w1[0], cmp_k_w2[0], cmp_v_pe[0], cmp_v_w1[0], cmp_v_w2[0],
                  w_da_out[0], w_nsa_out[0], w_o[0], norm2_w[0], router_w[0], router_b[0],
                  exp_w_gate[0], exp_w_up[0], exp_w_down[0], sh_w_gate[0], sh_w_up[0], sh_w_down[0],
                  final_norm_w, lam_init)
```

```python
import functools
import math

import numpy as np
import jax
import jax.numpy as jnp
from jax import lax
from jax.experimental import pallas as pl
from jax.experimental.pallas import tpu as pltpu
from jax.experimental.pallas import tpu_sc as plsc

F32 = jnp.float32
BF16 = jnp.bfloat16
I32 = jnp.int32
U32 = jnp.uint32
HIGHEST = lax.Precision.HIGHEST

NORM_EPS = 1e-6
NEG = -1e30

DA_HEADS = 8
DA_DIM = 64
DA_VDIM = 128
DA_SUBLN_EPS = 1e-5

NSA_HEADS = 8
NSA_GROUPS = 2
NSA_HPG = 4
NSA_DIM = 64
CMP_BLOCK = 32
CMP_STRIDE = 16
SEL_BLOCK = 64
SEL_TOPK = 16
WINDOW = 512
FORCE_BONUS = 1e4
SEL_COLS = 128

N_EXPERTS = 64
MOE_TOPK = 8
N_EXPERT_GROUPS = 8
TOPK_EXPERT_GROUPS = 4
ROUTED_SCALE = 2.5

VMEM_LIMIT_V7X = 56 * 1024 * 1024

COL_DAQ, COL_DAK, COL_DAV = 0, 1024, 2048
COL_MERGE = 3072
COL_NSAQ = 5120
COL_KV6 = 5632
COL_NSAG = 6400
SLAB_COLS = 6528


POS_SPLIT = 16
POS_ROWS = 16
ONES_ROWS = 16


def _pos_features(rows):
    r = np.arange(rows)
    f = np.zeros((rows, 128), np.float32)
    f[:, 0:3] = (r // POS_SPLIT)[:, None]
    f[:, 3:6] = (r % POS_SPLIT)[:, None]
    return jnp.asarray(f, BF16)


def _slope_rows(slopes):
    s = np.asarray(slopes, np.float32)
    bf = lambda x: x.astype(BF16).astype(np.float32)
    p1 = bf(s)
    p2 = bf(s - p1)
    p3 = bf(s - p1 - p2)
    out = np.zeros((s.shape[0], POS_ROWS, 128), np.float32)
    for i, piece in enumerate((p1, p2, p3)):
        out[:, i, :] = POS_SPLIT * piece[:, None]
        out[:, 3 + i, :] = piece[:, None]
    return jnp.asarray(out)


def _with_ones_row(vT):
    shape = vT.shape[:-2] + (ONES_ROWS, vT.shape[-1])
    extra = jnp.zeros(shape, vT.dtype).at[..., 0, :].set(1.0)
    return jnp.concatenate([vT, extra], axis=-2)


def _cparams(sem, vmem=VMEM_LIMIT_V7X):
    return pltpu.CompilerParams(dimension_semantics=sem, vmem_limit_bytes=vmem)


def _ada_kernel(c_ref, w_ref, b_ref, o_ref):
    c = c_ref[...]
    ca = c * jax.nn.sigmoid(c)
    o_ref[...] = jnp.dot(ca, w_ref[...], preferred_element_type=F32, precision=HIGHEST) + b_ref[...]


def _ada(c, w, b):
    B, D = c.shape
    n_out = w.shape[1]
    rows = 8
    cp = jnp.zeros((rows, D), F32).at[:B].set(c)
    tn = 1024
    out = pl.pallas_call(
        _ada_kernel,
        grid=(n_out // tn,),
        in_specs=[pl.BlockSpec((rows, D), lambda j: (0, 0)),
                  pl.BlockSpec((D, tn), lambda j: (0, j)),
                  pl.BlockSpec((1, tn), lambda j: (0, j))],
        out_specs=pl.BlockSpec((rows, tn), lambda j: (0, j)),
        out_shape=jax.ShapeDtypeStruct((rows, n_out), F32),
        compiler_params=_cparams(("arbitrary",)),
        name="ada",
    )(cp, w, b.reshape(1, n_out))
    return out[:B]


def _modulate(x, nw, sh, sc):
    r = lax.rsqrt(jnp.mean(x * x, axis=-1, keepdims=True) + NORM_EPS)
    return (x * r) * nw * (1.0 + sc) + sh


def _inproj_kernel(x_ref, nw_ref, sh_ref, sc_ref, w_ref, o_ref, h_ref):
    @pl.when(pl.program_id(1) == 0)
    def _():
        h_ref[...] = _modulate(x_ref[...], nw_ref[...], sh_ref[...], sc_ref[...]).astype(BF16)

    o_ref[...] = jnp.dot(h_ref[...], w_ref[...], preferred_element_type=F32).astype(BF16)


def _inproj(x2, nw, sh, sc, w_slab, S):
    N, D = x2.shape
    tm = 512
    tn = SLAB_COLS // 3
    per_b = S // tm
    return pl.pallas_call(
        _inproj_kernel,
        grid=(N // tm, SLAB_COLS // tn),
        in_specs=[pl.BlockSpec((tm, D), lambda i, j: (i, 0)),
                  pl.BlockSpec((1, D), lambda i, j: (0, 0)),
                  pl.BlockSpec((None, 1, D), lambda i, j: (i // per_b, 0, 0)),
                  pl.BlockSpec((None, 1, D), lambda i, j: (i // per_b, 0, 0)),
                  pl.BlockSpec((D, tn), lambda i, j: (0, j))],
        out_specs=pl.BlockSpec((tm, tn), lambda i, j: (i, j)),
        out_shape=jax.ShapeDtypeStruct((N, SLAB_COLS), BF16),
        scratch_shapes=[pltpu.VMEM((tm, D), BF16)],
        compiler_params=_cparams(("parallel", "arbitrary")),
        name="inproj",
    )(x2, nw.reshape(1, D), sh.reshape(-1, 1, D), sc.reshape(-1, 1, D), w_slab)


def _cmp_kernel(r_ref, pe_ref, w1a_ref, w1b_ref, w2_ref, o_ref):
    r = r_ref[...]
    n_rows = r.shape[0]
    a = jnp.dot(r, w1a_ref[...], preferred_element_type=F32)
    b = jnp.dot(r, w1b_ref[...], preferred_element_type=F32)
    pe = pe_ref[...]
    half = pe.shape[1] // 2
    bias = (jnp.dot(pe[:, :half], w1a_ref[...].astype(F32), preferred_element_type=F32, precision=HIGHEST)
            + jnp.dot(pe[:, half:], w1b_ref[...].astype(F32), preferred_element_type=F32, precision=HIGHEST))
    hid = a + pltpu.roll(b, n_rows - 1, 0) + bias
    act = 0.5 * hid * (1.0 + jnp.tanh(0.7978845608028654 * (hid + 0.044715 * hid * hid * hid)))
    o_ref[...] = jnp.dot(act, w2_ref[...], preferred_element_type=F32, precision=HIGHEST)


def _compress(rows, pe, w1, w2):
    BG, n_rows, width = rows.shape
    hidden = w1.shape[1]
    w1a = w1[:width].astype(BF16)
    w1b = w1[width:].astype(BF16)
    return pl.pallas_call(
        _cmp_kernel,
        grid=(BG,),
        in_specs=[pl.BlockSpec((None, n_rows, width), lambda i: (i, 0, 0)),
                  pl.BlockSpec((1, 2 * width), lambda i: (0, 0)),
                  pl.BlockSpec((width, hidden), lambda i: (0, 0)),
                  pl.BlockSpec((width, hidden), lambda i: (0, 0)),
                  pl.BlockSpec((hidden, NSA_DIM), lambda i: (0, 0))],
        out_specs=pl.BlockSpec((None, n_rows, NSA_DIM), lambda i: (i, 0, 0)),
        out_shape=jax.ShapeDtypeStruct((BG, n_rows, NSA_DIM), F32),
        compiler_params=_cparams(("parallel",)),
        name="cmp",
    )(rows, pe.reshape(1, -1).astype(F32), w1a, w1b, w2.astype(F32))


def _da_kernel(slopes_ref, kmin_ref, qT_ref, k_ref, vT_ref, pos_ref, srow_ref, lq1_ref, lk1_ref, lq2_ref, lk2_ref, subw_ref,
               oT_ref, acc_ref, qa_ref, sa_ref, sb_ref, pa_ref, pb_ref, rel_ref, *, tq, tk, lam_init):
    h = pl.program_id(1)
    qi = pl.program_id(2)
    slope = slopes_ref[h]
    q0 = qi * tq
    w = 2 * tq
    nk = k_ref.shape[0] // tk
    qT = qT_ref[...]
    row = lax.broadcasted_iota(I32, qT.shape, 0)
    zero = jnp.zeros_like(qT)
    qa_ref[0:2 * DA_DIM, :] = jnp.concatenate(
        [jnp.where(row < DA_DIM, qT, zero), jnp.where(row >= DA_DIM, qT, zero)], axis=1)
    qa_ref[2 * DA_DIM:2 * DA_DIM + POS_ROWS, :] = jnp.concatenate([srow_ref[...]] * (w // 128), axis=1).astype(BF16)
    qa_ref[2 * DA_DIM + POS_ROWS:, :] = jnp.zeros((2 * DA_DIM - POS_ROWS, w), BF16)
    lane = lax.broadcasted_iota(I32, (tk, w), 1)
    rel_ref[...] = lax.broadcasted_iota(I32, (tk, w), 0) - jnp.where(lane >= tq, lane - tq, lane)

    def scores(kt, s_ref):
        k0 = kt * tk
        k_t = k_ref[pl.ds(pl.multiple_of(k0, tk), tk), :]
        s = jnp.dot(jnp.concatenate([k_t, pos_ref[...]], axis=1), qa_ref[...], preferred_element_type=F32)
        s = jnp.where(rel_ref[...] <= q0 - k0, s, NEG)
        s_ref[...] = s
        return jnp.max(s, axis=0, keepdims=True)

    def probs(s_ref, p_ref, mx, off, m_old):
        m_new = jnp.maximum(m_old, mx + off)
        p_ref[...] = jnp.exp(s_ref[...] - (m_new - off)).astype(BF16)
        return m_new, jnp.exp(m_old - m_new)

    def accumulate(kt, p_ref, alpha):
        acc_ref[...] = alpha * acc_ref[...] + jnp.dot(vT_ref[kt], p_ref[...], preferred_element_type=F32)

    n_tiles = (q0 + tq - 1) // tk + 1
    kt_min = kmin_ref[(pl.program_id(0) * pl.num_programs(1) + h) * pl.num_programs(2) + qi]
    acc_ref[...] = jnp.zeros_like(acc_ref)
    pb_ref[...] = jnp.zeros_like(pb_ref)
    mx_a = scores(kt_min, sa_ref)

    def pair(j, carry):
        m, mx_a, alpha_b = carry
        ta = kt_min + 2 * j
        tb = ta + 1
        off_a = slope * (ta * tk).astype(F32)
        off_b = jnp.where(tb < n_tiles, slope * (tb * tk).astype(F32), NEG)
        accumulate(jnp.maximum(ta - 1, 0), pb_ref, alpha_b)
        m, alpha_a = probs(sa_ref, pa_ref, mx_a, off_a, m)
        mx_b = scores(jnp.minimum(tb, nk - 1), sb_ref)
        accumulate(ta, pa_ref, alpha_a)
        m, alpha_b = probs(sb_ref, pb_ref, mx_b, off_b, m)
        mx_a = scores(jnp.minimum(ta + 2, nk - 1), sa_ref)
        return m, mx_a, alpha_b

    n_pairs = (n_tiles - kt_min + 1) // 2
    init = (jnp.full((1, w), NEG, F32), mx_a, jnp.ones((1, w), F32))
    _, _, alpha_b = lax.fori_loop(0, n_pairs, pair, init)
    accumulate(jnp.minimum(kt_min + 2 * n_pairs - 1, nk - 1), pb_ref, alpha_b)

    lam = (jnp.exp(jnp.sum(lq1_ref[...] * lk1_ref[...], axis=-1, keepdims=True))
           - jnp.exp(jnp.sum(lq2_ref[...] * lk2_ref[...], axis=-1, keepdims=True)) + lam_init)
    on = acc_ref[0:DA_VDIM, :] * (1.0 / acc_ref[DA_VDIM:DA_VDIM + 1, :])
    o = on[:, :tq] - lam * on[:, tq:]
    r = lax.rsqrt(jnp.mean(o * o, axis=0, keepdims=True) + DA_SUBLN_EPS)
    oT_ref[...] = (o * r * subw_ref[...] * (1.0 - lam_init)).astype(BF16)


EXP_UNDERFLOW = 104.0


def _first_live_tile(slopes_np, slab3, tq, tk):
    B, S, _ = slab3.shape
    H = DA_HEADS
    width = 2 * DA_DIM * H
    grp = jnp.asarray((np.arange(width)[:, None] // DA_DIM == np.arange(2 * H)[None, :]).astype(np.float32))

    def norms(c0):
        v = slab3[:, :, c0:c0 + width].astype(F32)
        return jnp.sqrt(jnp.einsum('bsc,cg->bsg', v * v, grp, precision=HIGHEST))

    qn = norms(COL_DAQ) * DA_DIM ** -0.5
    qn = jnp.max(qn.reshape(B, S // tq, tq, H, 2), axis=2).transpose(0, 2, 3, 1)
    kn = jnp.max(norms(COL_DAK), axis=1).reshape(B, H, 2)
    bound = jnp.max(qn * kn[..., None], axis=2) * 1.01 + 0.01
    q0 = (jnp.arange(S // tq, dtype=F32) * tq)[None, None, :]
    slope = jnp.asarray(slopes_np, F32)[None, :, None]
    first_key = q0 - (EXP_UNDERFLOW + 2.0 * bound) / slope
    kt = jnp.floor(first_key / tk).astype(I32)
    n_tiles = (jnp.arange(S // tq, dtype=I32) * tq + tq - 1) // tk + 1
    return jnp.clip(kt, 0, n_tiles[None, None, :] - 1).reshape(-1)


def _diff_attention(slopes, kmin, qT, slab3, vT, srows, lq1, lk1, lq2, lk2, subw, lam_init, tq, tk):
    B, _, S = qT.shape
    nk = S // tk
    assert tk % tq == 0 and S % tk == 0
    tile = tq
    kcol = COL_DAK // 128
    v_rows = vT.shape[3]
    vec = lambda a: a.reshape(1, DA_DIM).astype(F32)
    grid_spec = pltpu.PrefetchScalarGridSpec(
        num_scalar_prefetch=2,
        grid=(B, DA_HEADS, S // tile),
        in_specs=[pl.BlockSpec((None, 128, tile), lambda b, h, i, s, km: (b, h, i)),
                  pl.BlockSpec((None, S, 128), lambda b, h, i, s, km: (b, 0, kcol + h)),
                  pl.BlockSpec((None, None, nk, v_rows, tk), lambda b, h, i, s, km: (b, h, 0, 0, 0)),
                  pl.BlockSpec((tk, 128), lambda b, h, i, s, km: (0, 0)),
                  pl.BlockSpec((None, POS_ROWS, 128), lambda b, h, i, s, km: (h, 0, 0)),
                  pl.BlockSpec((1, DA_DIM), lambda b, h, i, s, km: (0, 0)),
                  pl.BlockSpec((1, DA_DIM), lambda b, h, i, s, km: (0, 0)),
                  pl.BlockSpec((1, DA_DIM), lambda b, h, i, s, km: (0, 0)),
                  pl.BlockSpec((1, DA_DIM), lambda b, h, i, s, km: (0, 0)),
                  pl.BlockSpec((DA_VDIM, 1), lambda b, h, i, s, km: (0, 0))],
        out_specs=pl.BlockSpec((None, 128, tile), lambda b, h, i, s, km: (b, h, i)),
        scratch_shapes=[pltpu.VMEM((v_rows, 2 * tq), F32), pltpu.VMEM((4 * DA_DIM, 2 * tq), BF16),
                        pltpu.VMEM((tk, 2 * tq), F32), pltpu.VMEM((tk, 2 * tq), F32),
                        pltpu.VMEM((tk, 2 * tq), BF16), pltpu.VMEM((tk, 2 * tq), BF16),
                        pltpu.VMEM((tk, 2 * tq), I32)],
    )
    return pl.pallas_call(
        functools.partial(_da_kernel, tq=tq, tk=tk, lam_init=lam_init),
        grid_spec=grid_spec,
        out_shape=jax.ShapeDtypeStruct((B, DA_HEADS * DA_VDIM, S), BF16),
        compiler_params=_cparams(("parallel", "parallel", "arbitrary")),
        name="diffattn",
    )(slopes, kmin, qT, slab3, vT, _pos_features(tk), srows, vec(lq1), vec(lk1), vec(lq2), vec(lk2),
      subw.reshape(DA_VDIM, 1).astype(F32))


def _nsa_kernel(slopes_ref, qT_ref, kc_ref, vcT_ref, ksa_ref, vsT_ref, kwa_ref, vwT_ref, wb_ref, gT_ref, mT_ref,
                grp_ref, ltri_ref, oT_ref, acc_ref, res_ref, base_ref, qa_ref, sa_ref, sb_ref, pa_ref, pb_ref,
                list_ref, slist_ref, lsem, *, t, tk, n_top):
    g = pl.program_id(1)
    qi = pl.program_id(2)
    q0 = qi * t
    w = NSA_HPG * t
    n_cp = kc_ref.shape[0]
    n_sel = mT_ref.shape[0]
    lane1 = lax.broadcasted_iota(I32, (1, w), 1)
    slope_row = jnp.zeros((1, w), F32)
    for hh in range(NSA_HPG):
        slope_row = jnp.where((lane1 >= hh * t) & (lane1 < (hh + 1) * t), slopes_ref[g * NSA_HPG + hh], slope_row)
    q_all = qT_ref[...]
    gates = jax.nn.sigmoid(gT_ref[...])
    key_off = lax.broadcasted_iota(I32, (tk, w), 0)
    t_loc = jnp.bitwise_and(lax.broadcasted_iota(I32, (tk, w), 1), t - 1)
    base_ref[...] = slope_row * key_off.astype(F32)

    c_idx = lax.broadcasted_iota(I32, (n_cp, w), 0)
    t_c = q0 + jnp.bitwise_and(lax.broadcasted_iota(I32, (n_cp, w), 1), t - 1)
    d_c = t_c - (c_idx * CMP_STRIDE + (CMP_BLOCK - 1))
    ok_c = d_c >= 0
    s = jnp.dot(kc_ref[...], q_all, preferred_element_type=F32) - slope_row * d_c.astype(F32)
    s = jnp.where(ok_c, s, NEG)
    e = jnp.exp(s - jnp.max(s, axis=0, keepdims=True))
    inv = 1.0 / jnp.sum(e, axis=0, keepdims=True)
    p = jnp.where(ok_c, e * inv, 0.0)
    res_ref[...] = gates[0:1, :] * jnp.dot(vcT_ref[...], p.astype(BF16), preferred_element_type=F32)
    p_sum = p[:, 0:t]
    for hh in range(1, NSA_HPG):
        p_sum = p_sum + p[:, hh * t:(hh + 1) * t]

    p_hi = p_sum.astype(BF16)
    p_lo = (p_sum - p_hi.astype(F32)).astype(BF16)
    mT = mT_ref[...]
    imp = jnp.dot(mT, p_hi, preferred_element_type=F32) + jnp.dot(mT, p_lo, preferred_element_type=F32)
    blk = lax.broadcasted_iota(I32, (n_sel, t), 0)
    t_s = q0 + lax.broadcasted_iota(I32, (n_sel, t), 1)
    cur = jnp.right_shift(t_s, int(math.log2(SEL_BLOCK)))
    forced = (blk == 0) | (blk == cur) | (blk == cur - 1)
    score = jnp.where(blk * SEL_BLOCK <= t_s, imp + jnp.where(forced, FORCE_BONUS, 0.0), NEG)
    blk_f = blk.astype(F32)
    sel = jnp.zeros((n_sel, t), F32)
    for _ in range(n_top):
        _, idx = _first_max(score, blk_f)
        hit = blk_f == idx
        sel = jnp.where(hit, 1.0, sel)
        score = jnp.where(hit, -jnp.inf, score)
    n_full = q0 // tk
    nt_pad = grp_ref.shape[0]
    cnt = jnp.broadcast_to(jnp.sum(sel, axis=1, keepdims=True), (n_sel, 128)).astype(BF16)
    tile_cnt = jnp.dot(grp_ref[...], cnt, preferred_element_type=F32)
    kt_i = lax.broadcasted_iota(I32, (nt_pad, 128), 0)
    active = (tile_cnt > 0.5) & (kt_i < n_full)
    act_f = jnp.where(active, 1.0, 0.0)
    before = jnp.dot(ltri_ref[...], act_f.astype(BF16), preferred_element_type=F32)
    lane_j = lax.broadcasted_iota(I32, (nt_pad, 128), 1)
    slot_hit = active & (before == lane_j.astype(F32))
    tiles_row = jnp.sum(jnp.where(slot_hit, kt_i.astype(F32), 0.0), axis=0, keepdims=True)
    n_act_row = jnp.sum(act_f, axis=0, keepdims=True)
    lane8 = lax.broadcasted_iota(I32, (8, 128), 1)
    list_ref[...] = jnp.where(lane8 == 127, n_act_row, tiles_row).astype(I32)
    list_copy = pltpu.make_async_copy(list_ref, slist_ref, lsem)
    list_copy.start()

    selb = jnp.where(sel > 0.5, 0.0, NEG)
    row64 = lax.broadcasted_iota(I32, (NSA_DIM, w), 0)
    qa_ref[0:NSA_DIM, :] = q_all
    qa_ref[NSA_DIM:2 * NSA_DIM, :] = jnp.where(row64 == 0, NEG, 0.0).astype(BF16)
    qa_ref[2 * NSA_DIM:2 * NSA_DIM + n_sel, :] = jnp.concatenate([selb] * NSA_HPG, axis=1).astype(BF16)
    if n_sel < SEL_COLS:
        qa_ref[2 * NSA_DIM + n_sel:, :] = jnp.zeros((SEL_COLS - n_sel, w), BF16)

    n_wt = (WINDOW + t) // t
    kw_t = kwa_ref[pl.ds(pl.multiple_of(q0, t), WINDOW + t), :]
    s = jnp.dot(kw_t, qa_ref[0:2 * NSA_DIM, :], preferred_element_type=F32) + wb_ref[...]
    p = jnp.exp(s - jnp.max(s, axis=0, keepdims=True))
    inv = 1.0 / jnp.sum(p, axis=0, keepdims=True)
    p = p.astype(BF16)
    o_w = jnp.dot(vwT_ref[qi], p[0:t, :], preferred_element_type=F32)
    for c in range(1, n_wt):
        o_w = o_w + jnp.dot(vwT_ref[qi + c], p[c * t:(c + 1) * t, :], preferred_element_type=F32)
    res_ref[...] = res_ref[...] + gates[2:3, :] * (o_w * inv)

    def scores(kt, s_ref):
        k_t = ksa_ref[pl.ds(pl.multiple_of(kt * tk, tk), tk), :]
        s = jnp.dot(k_t, qa_ref[...], preferred_element_type=F32) + base_ref[...]
        s_ref[...] = s
        return jnp.max(s, axis=0, keepdims=True)

    def probs(s_ref, p_ref, mx, off, m_old, l_old):
        m_new = jnp.maximum(m_old, mx + off)
        p = jnp.exp(s_ref[...] - (m_new - off))
        alpha = jnp.exp(m_old - m_new)
        p_ref[...] = p.astype(BF16)
        return m_new, alpha * l_old + jnp.sum(p, axis=0, keepdims=True), alpha

    def accumulate(kt, p_ref, alpha):
        acc_ref[...] = alpha * acc_ref[...] + jnp.dot(vsT_ref[kt], p_ref[...], preferred_element_type=F32)

    list_copy.wait()
    n_act = slist_ref[0, 127]
    mx_a = scores(slist_ref[0, 0], sa_ref)

    kd = n_full * tk
    s = jnp.dot(ksa_ref[pl.ds(pl.multiple_of(kd, tk), tk), :], qa_ref[...], preferred_element_type=F32) + base_ref[...]
    s = jnp.where(key_off - t_loc <= q0 - kd, s, NEG)
    off_d = slope_row * kd.astype(F32)
    m = jnp.max(s, axis=0, keepdims=True) + off_d
    p = jnp.exp(s - (m - off_d))
    l = jnp.sum(p, axis=0, keepdims=True)
    acc_ref[...] = jnp.dot(vsT_ref[n_full], p.astype(BF16), preferred_element_type=F32)
    pb_ref[...] = jnp.zeros_like(pb_ref)

    def pair(j, carry):
        m, l, mx_a, alpha_b, tb_prev = carry
        ta = slist_ref[0, 2 * j]
        tb = slist_ref[0, 2 * j + 1]
        off_a = slope_row * (ta * tk).astype(F32)
        off_b = jnp.where(2 * j + 1 < n_act, slope_row * (tb * tk).astype(F32), NEG)
        accumulate(tb_prev, pb_ref, alpha_b)
        m, l, alpha_a = probs(sa_ref, pa_ref, mx_a, off_a, m, l)
        mx_b = scores(tb, sb_ref)
        accumulate(ta, pa_ref, alpha_a)
        m, l, alpha_b = probs(sb_ref, pb_ref, mx_b, off_b, m, l)
        mx_a = scores(slist_ref[0, 2 * j + 2], sa_ref)
        return m, l, mx_a, alpha_b, tb

    n_pairs = (n_act + 1) // 2
    m, l, _, alpha_b, tb_last = lax.fori_loop(0, n_pairs, pair, (m, l, mx_a, jnp.ones((1, w), F32), jnp.int32(0)))
    accumulate(tb_last, pb_ref, alpha_b)
    oT_ref[...] = (res_ref[...] + gates[1:2, :] * (acc_ref[...] * (1.0 / l))).astype(BF16)


def _nsa_attention(slopes, qT, kc, vcT, ksa, vsT, kwa, vwT, wb, gT, mT, t, tk):
    B, G, nq, _, w = qT.shape
    S = nq * t
    n_cp = kc.shape[2]
    n_sel = S // SEL_BLOCK
    n_top = min(SEL_TOPK, n_sel)
    assert tk % t == 0 and S % tk == 0 and t & (t - 1) == 0 and WINDOW % t == 0 and n_sel <= SEL_COLS
    n_tiles = S // tk
    nt_pad = -(-n_tiles // 16) * 16
    assert nt_pad + 2 < 127
    grp = jnp.asarray((np.arange(n_sel)[None, :] // (tk // SEL_BLOCK) == np.arange(nt_pad)[:, None])
                      .astype(np.float32), BF16)
    ltri = jnp.asarray((np.arange(nt_pad)[None, :] < np.arange(nt_pad)[:, None]).astype(np.float32), BF16)
    fixed = lambda b, g, i, s: (b, g, 0, 0)
    fixed5 = lambda b, g, i, s: (b, g, 0, 0, 0)
    tile5 = lambda b, g, i, s: (b, g, i, 0, 0)
    grid_spec = pltpu.PrefetchScalarGridSpec(
        num_scalar_prefetch=1,
        grid=(B, G, nq),
        in_specs=[pl.BlockSpec((None, None, None, NSA_DIM, w), tile5),
                  pl.BlockSpec((None, None, n_cp, NSA_DIM), fixed),
                  pl.BlockSpec((None, None, NSA_DIM, n_cp), fixed),
                  pl.BlockSpec((None, None, S, ksa.shape[-1]), fixed),
                  pl.BlockSpec((None, None, S // tk, NSA_DIM, tk), fixed5),
                  pl.BlockSpec((None, None, S + WINDOW, 2 * NSA_DIM), fixed),
                  pl.BlockSpec((None, None, (S + WINDOW) // t, NSA_DIM, t), fixed5),
                  pl.BlockSpec((None, WINDOW + t, w), lambda b, g, i, s: (g, 0, 0)),
                  pl.BlockSpec((None, None, None, 3, w), tile5),
                  pl.BlockSpec((n_sel, n_cp), lambda b, g, i, s: (0, 0)),
                  pl.BlockSpec((nt_pad, n_sel), lambda b, g, i, s: (0, 0)),
                  pl.BlockSpec((nt_pad, nt_pad), lambda b, g, i, s: (0, 0))],
        out_specs=pl.BlockSpec((None, None, None, NSA_DIM, w), tile5),
        scratch_shapes=[pltpu.VMEM((NSA_DIM, w), F32),
                        pltpu.VMEM((NSA_DIM, w), F32),
                        pltpu.VMEM((tk, w), F32),
                        pltpu.VMEM((2 * NSA_DIM + SEL_COLS, w), BF16),
                        pltpu.VMEM((tk, w), F32), pltpu.VMEM((tk, w), F32),
                        pltpu.VMEM((tk, w), BF16), pltpu.VMEM((tk, w), BF16),
                        pltpu.VMEM((8, 128), I32), pltpu.SMEM((8, 128), I32), pltpu.SemaphoreType.DMA(())],
    )
    return pl.pallas_call(
        functools.partial(_nsa_kernel, t=t, tk=tk, n_top=n_top),
        grid_spec=grid_spec,
        out_shape=jax.ShapeDtypeStruct((B, G, nq, NSA_DIM, w), BF16),
        compiler_params=_cparams(("parallel", "parallel", "arbitrary")),
        name="nsa",
    )(slopes, qT, kc, vcT, ksa, vsT, kwa, vwT, wb, gT, mT, grp, ltri)


def _pack_halves(v):
    c = v.shape[1] // 2
    lo = lax.bitcast_convert_type(v[:, :c].astype(BF16).astype(F32), U32)
    hi = lax.bitcast_convert_type(v[:, c:].astype(BF16).astype(F32), U32)
    return hi | (lo >> 16)


def _unpack_halves(u):
    lo = lax.bitcast_convert_type(u << 16, F32)
    hi = lax.bitcast_convert_type(u & jnp.uint32(0xFFFF0000), F32)
    return lo, hi


def _merge_kernel(oa_ref, ob_ref, ga_ref, gb_ref, x_ref, g1_ref, nw_ref, sh_ref, sc_ref,
                  wa_ref, wb_ref, wo_ref, wr_ref, x1_ref, h2_ref, lg_ref):
    ya = jnp.dot(oa_ref[...], wa_ref[...], preferred_element_type=F32)
    yb = jnp.dot(ob_ref[...], wb_ref[...], preferred_element_type=F32)
    merged = (jax.nn.sigmoid(ga_ref[...].astype(F32)) * ya
              + jax.nn.sigmoid(gb_ref[...].astype(F32)) * yb).astype(BF16)
    mix = jnp.dot(merged, wo_ref[...], preferred_element_type=F32)
    x1 = x_ref[...] + g1_ref[...] * mix
    x1_ref[...] = x1
    h2 = _modulate(x1, nw_ref[...], sh_ref[...], sc_ref[...])
    h2_ref[...] = _pack_halves(h2)
    lg_ref[...] = jnp.dot(h2, wr_ref[...], preferred_element_type=F32, precision=HIGHEST)


def _merge(oa, ob, slab, x2, g1, nw, sh, sc, wa, wb, wo, wr, S):
    N, D = x2.shape
    tm = 256
    per_b = S // tm
    mcol = COL_MERGE // D
    row = lambda i: (i, 0)
    full = lambda i: (0, 0)
    perb = lambda i: (i // per_b, 0, 0)
    n_lg = wr.shape[1]
    return pl.pallas_call(
        _merge_kernel,
        grid=(N // tm,),
        in_specs=[pl.BlockSpec((tm, oa.shape[1]), row),
                  pl.BlockSpec((tm, ob.shape[1]), row),
                  pl.BlockSpec((tm, D), lambda i: (i, mcol)),
                  pl.BlockSpec((tm, D), lambda i: (i, mcol + 1)),
                  pl.BlockSpec((tm, D), row),
                  pl.BlockSpec((None, 1, D), perb),
                  pl.BlockSpec((1, D), full),
                  pl.BlockSpec((None, 1, D), perb),
                  pl.BlockSpec((None, 1, D), perb),
                  pl.BlockSpec(wa.shape, full),
                  pl.BlockSpec(wb.shape, full),
                  pl.BlockSpec(wo.shape, full),
                  pl.BlockSpec(wr.shape, full)],
        out_specs=[pl.BlockSpec((tm, D), row), pl.BlockSpec((tm, D // 2), row), pl.BlockSpec((tm, n_lg), row)],
        out_shape=[jax.ShapeDtypeStruct((N, D), F32), jax.ShapeDtypeStruct((N, D // 2), U32),
                   jax.ShapeDtypeStruct((N, n_lg), F32)],
        compiler_params=_cparams(("parallel",)),
        name="merge",
    )(oa, ob, slab, slab, x2, g1.reshape(-1, 1, D), nw.reshape(1, D), sh.reshape(-1, 1, D),
      sc.reshape(-1, 1, D), wa, wb, wo, wr)


def _first_max(v, idx):
    mx = jnp.max(v, axis=0, keepdims=True)
    first = jnp.min(jnp.where(v == mx, idx, float(v.shape[0])), axis=0, keepdims=True)
    return mx, first


def _route_kernel(lg_ref, b_ref, e_ref, w_ref):
    lg = lg_ref[...]
    tn = lg.shape[1]
    per_g = N_EXPERTS // N_EXPERT_GROUPS
    scores = jax.nn.sigmoid(lg)
    biased = scores + b_ref[...]
    gi = lax.broadcasted_iota(I32, (N_EXPERT_GROUPS, tn), 0).astype(F32)
    gscore = jnp.zeros((N_EXPERT_GROUPS, tn), F32)
    for g in range(N_EXPERT_GROUPS):
        grp = biased[g * per_g:(g + 1) * per_g, :]
        m1, f1 = _first_max(grp, gi)
        m2 = jnp.max(jnp.where(gi == f1, -jnp.inf, grp), axis=0, keepdims=True)
        gscore = jnp.where(gi == g, m1 + m2, gscore)
    gsel = jnp.zeros((N_EXPERT_GROUPS, tn), F32)
    for _ in range(TOPK_EXPERT_GROUPS):
        _, f = _first_max(gscore, gi)
        hit = gi == f
        gsel = jnp.where(hit, 1.0, gsel)
        gscore = jnp.where(hit, -jnp.inf, gscore)
    ei = lax.broadcasted_iota(I32, (N_EXPERTS, tn), 0).astype(F32)
    emask = jnp.zeros((N_EXPERTS, tn), F32)
    for g in range(N_EXPERT_GROUPS):
        in_g = (ei >= g * per_g) & (ei < (g + 1) * per_g)
        emask = jnp.where(in_g, gsel[g:g + 1, :], emask)
    masked = jnp.where(emask > 0.5, biased, NEG)
    e_out = jnp.zeros((MOE_TOPK, tn), F32)
    w_out = jnp.zeros((MOE_TOPK, tn), F32)
    for r in range(MOE_TOPK):
        _, f = _first_max(masked, ei)
        hit = ei == f
        wv = jnp.sum(jnp.where(hit, scores, 0.0), axis=0, keepdims=True)
        e_out = jnp.where(gi == r, f, e_out)
        w_out = jnp.where(gi == r, wv, w_out)
        masked = jnp.where(hit, -jnp.inf, masked)
    e_ref[...] = e_out.astype(I32)
    w_ref[...] = w_out / jnp.sum(w_out, axis=0, keepdims=True) * ROUTED_SCALE


def _route(lgT, rb):
    E, N = lgT.shape
    tn = 512
    return pl.pallas_call(
        _route_kernel,
        grid=(N // tn,),
        in_specs=[pl.BlockSpec((E, tn), lambda i: (0, i)), pl.BlockSpec((E, 1), lambda i: (0, 0))],
        out_specs=[pl.BlockSpec((MOE_TOPK, tn), lambda i: (0, i)), pl.BlockSpec((MOE_TOPK, tn), lambda i: (0, i))],
        out_shape=[jax.ShapeDtypeStruct((MOE_TOPK, N), I32), jax.ShapeDtypeStruct((MOE_TOPK, N), F32)],
        compiler_params=_cparams(("parallel",)),
        name="route",
    )(lgT, rb.reshape(E, 1).astype(F32))


def _slots_kernel(e_ref, tri_ref, lt_ref, dest_ref, cnt_ref, carry_ref, *, rb):
    phase = pl.program_id(0)
    i = pl.program_id(1)
    e = e_ref[...]
    tn = e.shape[1]
    ei = lax.broadcasted_iota(I32, (N_EXPERTS, tn), 0)

    @pl.when((phase == 0) & (i == 0))
    def _():
        carry_ref[...] = jnp.zeros_like(carry_ref)

    @pl.when(phase == 0)
    def _():
        tot = jnp.zeros((N_EXPERTS, 1), F32)
        for k in range(MOE_TOPK):
            oh = jnp.where(ei == e[k:k + 1, :], 1.0, 0.0)
            tot = tot + jnp.sum(oh, axis=1, keepdims=True)
        carry_ref[...] = carry_ref[...] + tot
        dest_ref[...] = jnp.zeros_like(dest_ref)

    @pl.when((phase == 1) & (i == 0))
    def _():
        cnt = carry_ref[...]
        cnt_ref[...] = cnt
        padded = jnp.broadcast_to(jnp.ceil(cnt / rb) * rb, (N_EXPERTS, 128))
        first = jnp.dot(lt_ref[...], padded, preferred_element_type=F32, precision=HIGHEST)
        carry_ref[...] = first[:, :1]

    @pl.when(phase == 1)
    def _():
        carry = carry_ref[...]
        ki = lax.broadcasted_iota(I32, (MOE_TOPK, tn), 0)
        dest = jnp.zeros((MOE_TOPK, tn), F32)
        for k in range(MOE_TOPK):
            hit = ei == e[k:k + 1, :]
            oh = jnp.where(hit, 1.0, 0.0)
            before = jnp.dot(oh.astype(BF16), tri_ref[...], preferred_element_type=F32)
            row = jnp.sum(jnp.where(hit, before + carry, 0.0), axis=0, keepdims=True)
            dest = jnp.where(ki == k, row, dest)
            carry = carry + jnp.sum(oh, axis=1, keepdims=True)
        carry_ref[...] = carry
        dest_ref[...] = dest.astype(I32)


def _slots(top_eT, rb):
    K, N = top_eT.shape
    tn = 512
    idx = np.arange(tn)
    tri = jnp.asarray((idx[:, None] < idx[None, :]).astype(np.float32), BF16)
    e_idx = np.arange(N_EXPERTS)
    lt = jnp.asarray((e_idx[None, :] < e_idx[:, None]).astype(np.float32), F32)
    return pl.pallas_call(
        functools.partial(_slots_kernel, rb=rb),
        grid=(2, N // tn),
        in_specs=[pl.BlockSpec((K, tn), lambda p, i: (0, i)),
                  pl.BlockSpec((tn, tn), lambda p, i: (0, 0)),
                  pl.BlockSpec((N_EXPERTS, N_EXPERTS), lambda p, i: (0, 0))],
        out_specs=[pl.BlockSpec((K, tn), lambda p, i: (0, i * p)),
                   pl.BlockSpec((N_EXPERTS, 1), lambda p, i: (0, 0))],
        out_shape=[jax.ShapeDtypeStruct((K, N), I32), jax.ShapeDtypeStruct((N_EXPERTS, 1), F32)],
        scratch_shapes=[pltpu.VMEM((N_EXPERTS, 1), F32)],
        compiler_params=_cparams(("arbitrary", "arbitrary")),
        name="slots",
    )(top_eT, tri, lt)


def _dispatch_kernel(dest_ref, h_ref, xs_in_ref, xs_ref, sem, *, tt):
    del xs_in_ref

    def body(n, c):
        src = h_ref.at[pl.ds(n, 1), :]
        for k in range(MOE_TOPK):
            d = dest_ref[0, 0, k * tt + n]
            pltpu.make_async_copy(src, xs_ref.at[pl.ds(d, 1), :], sem).start(priority=k % 2)
        return c

    lax.fori_loop(0, tt, body, 0)
    for k in range(MOE_TOPK):
        pltpu.make_async_copy(h_ref, xs_ref.at[pl.ds(0, tt), :], sem).wait()


def _dispatch(dest_tiles, h2, n_rows, tt):
    N, D = h2.shape
    xs0 = jnp.zeros((n_rows, D), h2.dtype)
    return pl.pallas_call(
        functools.partial(_dispatch_kernel, tt=tt),
        grid=(N // tt,),
        in_specs=[pl.BlockSpec((1, 1, MOE_TOPK * tt), lambda i: (i, 0, 0), memory_space=pltpu.SMEM),
                  pl.BlockSpec((tt, D), lambda i: (i, 0)),
                  pl.BlockSpec(memory_space=pl.ANY)],
        out_specs=pl.BlockSpec(memory_space=pl.ANY),
        out_shape=jax.ShapeDtypeStruct((n_rows, D), h2.dtype),
        scratch_shapes=[pltpu.SemaphoreType.DMA(())],
        input_output_aliases={2: 0},
        compiler_params=_cparams(("arbitrary",)),
        name="dispatch",
    )(dest_tiles, h2, xs0)


def _experts_kernel(be_ref, x_ref, wgu_ref, wd_ref, y_ref):
    del be_ref
    lo, hi = _unpack_halves(x_ref[...])
    x = jnp.concatenate([lo.astype(BF16), hi.astype(BF16)], axis=1)
    gu = jnp.dot(x, wgu_ref[...], preferred_element_type=F32)
    hdim = gu.shape[1] // 2
    g = gu[:, :hdim]
    act = (g * jax.nn.sigmoid(g) * gu[:, hdim:]).astype(BF16)
    y_ref[...] = _pack_halves(jnp.dot(act, wd_ref[...], preferred_element_type=F32))


def _experts(blk_e, xs, wgu, wd, rb):
    n_rows, D = xs.shape
    grid_spec = pltpu.PrefetchScalarGridSpec(
        num_scalar_prefetch=1,
        grid=(n_rows // rb,),
        in_specs=[pl.BlockSpec((rb, D), lambda i, be: (i, 0)),
                  pl.BlockSpec((None,) + wgu.shape[1:], lambda i, be: (be[i], 0, 0)),
                  pl.BlockSpec((None,) + wd.shape[1:], lambda i, be: (be[i], 0, 0))],
        out_specs=pl.BlockSpec((rb, D), lambda i, be: (i, 0)),
    )
    return pl.pallas_call(
        _experts_kernel,
        grid_spec=grid_spec,
        out_shape=jax.ShapeDtypeStruct((n_rows, D), U32),
        compiler_params=_cparams(("arbitrary",)),
        name="experts",
    )(blk_e, xs, wgu, wd)


SC_WINDOW = 128
SC_SPLIT = 2


def _sc_gather_rows(table, idx):
    rows, full_width = table.shape
    table = table.reshape(rows * SC_SPLIT, full_width // SC_SPLIT)
    idx = (idx[:, None] * SC_SPLIT + jnp.arange(SC_SPLIT, dtype=I32)[None, :]).reshape(-1)
    n = idx.shape[0]
    width = table.shape[1]
    mesh = plsc.VectorSubcoreMesh(core_axis_name="c", subcore_axis_name="s")

    @pl.kernel(out_type=jax.ShapeDtypeStruct((n, width), table.dtype), mesh=mesh)
    def gather_kernel(t_hbm, i_hbm, o_hbm):
        def body(i_vmem, o_vmem):
            pltpu.sync_copy(t_hbm.at[i_vmem.at[0]], o_vmem)

        pltpu.emit_pipeline(
            body,
            grid=(n // SC_WINDOW,),
            in_specs=[pl.BlockSpec((1, SC_WINDOW), lambda i: (0, i))],
            out_specs=[pl.BlockSpec((SC_WINDOW, width), lambda i: (i, 0))],
            core_axis_name=("c", "s"),
            dimension_semantics=(pltpu.PARALLEL,),
        )(i_hbm, o_hbm)

    return gather_kernel(table, idx.reshape(1, n)).reshape(n // SC_SPLIT, full_width)


def _combine_kernel(yg_ref, w_ref, h_ref, x1_ref, g2_ref, fw_ref, sgu_ref, sd_ref, o_ref):
    h_lo, h_hi = _unpack_halves(h_ref[...])
    h = jnp.concatenate([h_lo.astype(BF16), h_hi.astype(BF16)], axis=1)
    gu = jnp.dot(h, sgu_ref[...], preferred_element_type=F32)
    hdim = gu.shape[1] // 2
    g = gu[:, :hdim]
    ffn = jnp.dot((g * jax.nn.sigmoid(g) * gu[:, hdim:]).astype(BF16), sd_ref[...], preferred_element_type=F32)

    w = w_ref[...]
    r_lo = jnp.zeros(h_lo.shape, F32)
    r_hi = jnp.zeros(h_hi.shape, F32)
    for k in range(MOE_TOPK):
        y_lo, y_hi = _unpack_halves(yg_ref[k])
        r_lo = r_lo + w[:, k:k + 1] * y_lo
        r_hi = r_hi + w[:, k:k + 1] * y_hi
    ffn = ffn + jnp.concatenate([r_lo, r_hi], axis=1)
    x2 = x1_ref[...] + g2_ref[...] * ffn
    r = lax.rsqrt(jnp.mean(x2 * x2, axis=-1, keepdims=True) + NORM_EPS)
    o_ref[...] = x2 * r * fw_ref[...]


def _combine(yg, top_w, h2p, x1, g2, fw, sgu, sd, S, tt):
    N, D = x1.shape
    per_b = S // tt
    row = lambda i: (i, 0)
    full = lambda i: (0, 0)
    return pl.pallas_call(
        _combine_kernel,
        grid=(N // tt,),
        in_specs=[pl.BlockSpec((MOE_TOPK, tt, D // 2), lambda i: (0, i, 0)),
                  pl.BlockSpec((tt, MOE_TOPK), row),
                  pl.BlockSpec((tt, D // 2), row),
                  pl.BlockSpec((tt, D), row),
                  pl.BlockSpec((None, 1, D), lambda i: (i // per_b, 0, 0)),
                  pl.BlockSpec((1, D), full),
                  pl.BlockSpec(sgu.shape, full),
                  pl.BlockSpec(sd.shape, full)],
        out_specs=pl.BlockSpec((tt, D), row),
        out_shape=jax.ShapeDtypeStruct((N, D), F32),
        compiler_params=_cparams(("parallel",)),
        name="combine",
    )(yg, top_w, h2p, x1, g2.reshape(-1, 1, D), fw.reshape(1, D), sgu, sd)


def _cmp_to_sel_T(n_cp, n_sel):
    c0 = np.arange(n_cp)[None, :] * CMP_STRIDE
    s0 = np.arange(n_sel)[:, None] * SEL_BLOCK
    ov = np.minimum(c0 + CMP_BLOCK, s0 + SEL_BLOCK) - np.maximum(c0, s0)
    m = np.clip(ov, 0, None).astype(np.float32) / CMP_BLOCK
    m[:, n_cp - 1] = 0.0
    return jnp.asarray(m, BF16)


def _slab_weights(w_in):
    D = w_in.shape[0]
    sizes = [1024, 1024, 1024, 512, 128, 128, 128, 128, 128, 128, 24, 2048]
    offs = np.concatenate([[0], np.cumsum(sizes)])
    part = lambda i: w_in[:, offs[i]:offs[i + 1]]
    pieces = [part(0), part(1), part(2), part(11), part(3)] + [part(i) for i in range(4, 10)] + [part(10)]
    w = jnp.concatenate(pieces, axis=1)
    return jnp.pad(w, ((0, 0), (0, SLAB_COLS - w.shape[1]))).astype(BF16)


def _layer(x, c, ada_w, ada_b, norm1_w, w_in, da_lq1, da_lk1, da_lq2, da_lk2, da_subln_w,
           cmp_k_pe, cmp_k_w1, cmp_k_w2, cmp_v_pe, cmp_v_w1, cmp_v_w2, w_da_out, w_nsa_out, w_o,
           norm2_w, router_w, router_b, exp_w_gate, exp_w_up, exp_w_down,
           sh_w_gate, sh_w_up, sh_w_down, final_norm_w, lam_init):
    B, S, D = x.shape
    N = B * S
    G, HPG = NSA_GROUPS, NSA_HPG
    x2 = x.reshape(N, D)

    mod = _ada(c, ada_w, ada_b)
    sh1, sc1, g1, sh2, sc2, g2 = jnp.split(mod, 6, axis=-1)

    slab = _inproj(x2, norm1_w, sh1, sc1, _slab_weights(w_in), S)
    slab3 = slab.reshape(B, S, SLAB_COLS)

    i_all = np.arange(1, DA_HEADS + NSA_HEADS + 1, dtype=np.float32)
    slopes = (2.0 ** (-8.0 * i_all / (DA_HEADS + NSA_HEADS))).astype(np.float32)
    slopes_a = jnp.asarray(slopes[0::2])
    slopes_b = jnp.asarray(slopes[1::2])

    ta_q, ta_k = min(512, S), min(512, S)
    qT = (slab3[:, :, COL_DAQ:COL_DAQ + 1024] * jnp.asarray(DA_DIM ** -0.5, BF16)).transpose(0, 2, 1)
    vT = slab3[:, :, COL_DAV:COL_DAV + 1024].reshape(B, S // ta_k, ta_k, DA_HEADS, DA_VDIM).transpose(0, 3, 1, 4, 2)
    kmin = _first_live_tile(slopes[0::2], slab3, ta_q, ta_k)
    oaT = _diff_attention(slopes_a, kmin, qT, slab3, _with_ones_row(vT), _slope_rows(slopes[0::2]), da_lq1, da_lk1,
                          da_lq2, da_lk2, da_subln_w, lam_init, ta_q, ta_k)
    oa = oaT.transpose(0, 2, 1).reshape(N, DA_HEADS * DA_VDIM)

    tb, tb_k = 128, min(256, S)
    n_cp = S // CMP_STRIDE
    n_sel = S // SEL_BLOCK

    def kv_groups(j):
        c0 = COL_KV6 + 128 * j
        return slab3[:, :, c0:c0 + 128].reshape(B, S, G, NSA_DIM).transpose(0, 2, 1, 3)

    def kv_tiles_T(a, rows):
        return a.reshape(B, G, S // rows, rows, NSA_DIM).transpose(0, 1, 2, 4, 3)

    ck_rows = kv_groups(0).reshape(B * G, n_cp, CMP_STRIDE * NSA_DIM)
    cv_rows = kv_groups(1).reshape(B * G, n_cp, CMP_STRIDE * NSA_DIM)
    kc = _compress(ck_rows, cmp_k_pe, cmp_k_w1, cmp_k_w2).reshape(B, G, n_cp, NSA_DIM).astype(BF16)
    vc = _compress(cv_rows, cmp_v_pe, cmp_v_w1, cmp_v_w2).reshape(B, G, n_cp, NSA_DIM).astype(BF16)
    nq = (slab3[:, :, COL_NSAQ:COL_NSAQ + 512] * jnp.asarray(NSA_DIM ** -0.5, BF16))
    nqT = nq.reshape(B, S // tb, tb, G, HPG, NSA_DIM).transpose(0, 3, 1, 5, 4, 2).reshape(B, G, S // tb, NSA_DIM, HPG * tb)
    gT = (slab3[:, :, COL_NSAG:COL_NSAG + 3 * NSA_HEADS].astype(F32).reshape(B, S // tb, tb, G, HPG, 3)
          .transpose(0, 3, 1, 5, 4, 2).reshape(B, G, S // tb, 3, HPG * tb))
    onehot = jnp.asarray((np.arange(S)[:, None] // SEL_BLOCK == np.arange(SEL_COLS)[None, :]).astype(np.float32), BF16)
    zeros64 = jnp.zeros((B, G, S, NSA_DIM), BF16)
    ksa = jnp.concatenate([kv_groups(2), zeros64, jnp.broadcast_to(onehot, (B, G, S, SEL_COLS))], axis=-1)
    pad_rows = jnp.zeros((B, G, WINDOW, 2 * NSA_DIM), BF16).at[..., NSA_DIM].set(1.0)
    kwa = jnp.concatenate([pad_rows, jnp.concatenate([kv_groups(4), zeros64], axis=-1)], axis=2)
    vw_pad = jnp.pad(kv_groups(5), ((0, 0), (0, 0), (WINDOW, 0), (0, 0)))
    vwT = vw_pad.reshape(B, G, (S + WINDOW) // tb, tb, NSA_DIM).transpose(0, 1, 2, 4, 3)
    r_w = np.arange(WINDOW + tb)[:, None]
    lane_w = np.arange(HPG * tb)[None, :]
    d_w = (WINDOW + lane_w % tb - r_w).astype(np.float32)
    slope_w = slopes[1::2].reshape(G, 1, HPG)[:, :, lane_w[0] // tb]
    wb = jnp.asarray(np.where((d_w >= 0) & (d_w < WINDOW), -slope_w * d_w[None], np.float32(NEG)).astype(np.float32))
    obT = _nsa_attention(slopes_b, nqT, kc, vc.transpose(0, 1, 3, 2), ksa, kv_tiles_T(kv_groups(3), tb_k),
                         kwa, vwT, wb, gT, _cmp_to_sel_T(n_cp, n_sel), tb, tb_k)
    ob = (obT.reshape(B, G, S // tb, NSA_DIM, HPG, tb).transpose(0, 2, 5, 1, 4, 3)
          .reshape(N, NSA_HEADS * NSA_DIM))

    x1, h2, logits = _merge(oa, ob, slab, x2, g1, norm2_w, sh2, sc2, w_da_out.astype(BF16),
                            w_nsa_out.astype(BF16), w_o.astype(BF16), router_w.astype(F32), S)

    rb = 512
    tt = 128
    top_eT, top_wT = _route(logits.T, router_b)
    destT, counts = _slots(top_eT, rb)
    n_rows = ((N * MOE_TOPK + N_EXPERTS * (rb - 1) + rb - 1) // rb) * rb
    padded = (jnp.ceil(counts[:, 0] / rb) * rb).astype(I32)
    pend = jnp.cumsum(padded)
    blk_start = jnp.arange(n_rows // rb, dtype=I32) * rb
    blk_e = jnp.minimum(jnp.sum((pend[None, :] <= blk_start[:, None]).astype(I32), axis=1), N_EXPERTS - 1)
    dest_tiles = destT.reshape(MOE_TOPK, N // tt, tt).transpose(1, 0, 2).reshape(N // tt, 1, MOE_TOPK * tt)
    xs = _dispatch(dest_tiles, h2, n_rows, tt)
    wgu = jnp.concatenate([exp_w_gate, exp_w_up], axis=-1).astype(BF16)
    ys = _experts(blk_e, xs, wgu, exp_w_down.astype(BF16), rb)
    sgu = jnp.concatenate([sh_w_gate, sh_w_up], axis=-1).astype(BF16)
    yg = _sc_gather_rows(ys, destT.reshape(-1)).reshape(MOE_TOPK, N, D // 2)
    out = _combine(yg, top_wT.T, h2, x1, g2, final_norm_w, sgu, sh_w_down.astype(BF16), S, 2 * tt)
    return out.reshape(B, S, D)


def kernel(x, c, ada_w, ada_b, norm1_w, w_in, da_lq1, da_lk1, da_lq2, da_lk2, da_subln_w, cmp_k_pe, cmp_k_w1, cmp_k_w2, cmp_v_pe, cmp_v_w1, cmp_v_w2, w_da_out, w_nsa_out, w_o, norm2_w, router_w, router_b, exp_w_gate, exp_w_up, exp_w_down, sh_w_gate, sh_w_up, sh_w_down, final_norm_w):
    depth = ada_w.shape[0]
    assert depth == 1, "one decoder layer"
    lam_init = 0.8 - 0.6 * math.exp(-0.3 * 0)
    return _layer(x, c, ada_w[0], ada_b[0], norm1_w[0], w_in[0], da_lq1[0], da_lk1[0], da_lq2[0], da_lk2[0],
                  da_subln_w[0], cmp_k_pe[0], cmp_k_w1[0], cmp_k_w2[0], cmp_v_pe[0], cmp_v_w1[0], cmp_v_w2[0],
                  w_da_out[0], w_nsa_out[0], w_o[0], norm2_w[0], router_w[0], router_b[0],
                  exp_w_gate[0], exp_w_up[0], exp_w_down[0], sh_w_gate[0], sh_w_up[0], sh_w_down[0],
                  final_norm_w, lam_init)
```

```python
import functools
import math

import numpy as np
import jax
import jax.numpy as jnp
from jax import lax
from jax.experimental import pallas as pl
from jax.experimental.pallas import tpu as pltpu
from jax.experimental.pallas import tpu_sc as plsc

F32 = jnp.float32
BF16 = jnp.bfloat16
I32 = jnp.int32
U32 = jnp.uint32
HIGHEST = lax.Precision.HIGHEST

NORM_EPS = 1e-6
NEG = -1e30

DA_HEADS = 8
DA_DIM = 64
DA_VDIM = 128
DA_SUBLN_EPS = 1e-5

NSA_HEADS = 8
NSA_GROUPS = 2
NSA_HPG = 4
NSA_DIM = 64
CMP_BLOCK = 32
CMP_STRIDE = 16
SEL_BLOCK = 64
SEL_TOPK = 16
WINDOW = 512
FORCE_BONUS = 1e4
SEL_COLS = 128

N_EXPERTS = 64
MOE_TOPK = 8
N_EXPERT_GROUPS = 8
TOPK_EXPERT_GROUPS = 4
ROUTED_SCALE = 2.5

VMEM_LIMIT_V7X = 56 * 1024 * 1024

COL_DAQ, COL_DAK, COL_DAV = 0, 1024, 2048
COL_MERGE = 3072
COL_NSAQ = 5120
COL_KV6 = 5632
COL_NSAG = 6400
SLAB_COLS = 6528


SC_WINDOW = 128
SC_SPLIT = 2
SC_PIECE = 256

POS_SPLIT = 16
POS_ROWS = 16
ONES_ROWS = 16


def _pos_features(rows):
    r = np.arange(rows)
    f = np.zeros((rows, 128), np.float32)
    f[:, 0:3] = (r // POS_SPLIT)[:, None]
    f[:, 3:6] = (r % POS_SPLIT)[:, None]
    return jnp.asarray(f, BF16)


def _slope_rows(slopes):
    s = np.asarray(slopes, np.float32)
    bf = lambda x: x.astype(BF16).astype(np.float32)
    p1 = bf(s)
    p2 = bf(s - p1)
    p3 = bf(s - p1 - p2)
    out = np.zeros((s.shape[0], POS_ROWS, 128), np.float32)
    for i, piece in enumerate((p1, p2, p3)):
        out[:, i, :] = POS_SPLIT * piece[:, None]
        out[:, 3 + i, :] = piece[:, None]
    return jnp.asarray(out)


def _with_ones_row(vT):
    shape = vT.shape[:-2] + (ONES_ROWS, vT.shape[-1])
    extra = jnp.zeros(shape, vT.dtype).at[..., 0, :].set(1.0)
    return jnp.concatenate([vT, extra], axis=-2)


def _cparams(sem, vmem=VMEM_LIMIT_V7X):
    return pltpu.CompilerParams(dimension_semantics=sem, vmem_limit_bytes=vmem)


def _ada_kernel(c_ref, w_ref, b_ref, o_ref):
    c = c_ref[...]
    ca = c * jax.nn.sigmoid(c)
    o_ref[...] = jnp.dot(ca, w_ref[...], preferred_element_type=F32, precision=HIGHEST) + b_ref[...]


def _ada(c, w, b):
    B, D = c.shape
    n_out = w.shape[1]
    rows = 8
    cp = jnp.zeros((rows, D), F32).at[:B].set(c)
    tn = 1024
    out = pl.pallas_call(
        _ada_kernel,
        grid=(n_out // tn,),
        in_specs=[pl.BlockSpec((rows, D), lambda j: (0, 0)),
                  pl.BlockSpec((D, tn), lambda j: (0, j)),
                  pl.BlockSpec((1, tn), lambda j: (0, j))],
        out_specs=pl.BlockSpec((rows, tn), lambda j: (0, j)),
        out_shape=jax.ShapeDtypeStruct((rows, n_out), F32),
        compiler_params=_cparams(("arbitrary",)),
        name="ada",
    )(cp, w, b.reshape(1, n_out))
    return out[:B]


def _modulate(x, nw, sh, sc):
    r = lax.rsqrt(jnp.mean(x * x, axis=-1, keepdims=True) + NORM_EPS)
    return (x * r) * nw * (1.0 + sc) + sh


def _inproj_kernel(x_ref, nw_ref, sh_ref, sc_ref, w_ref, o_ref, h_ref):
    @pl.when(pl.program_id(1) == 0)
    def _():
        h_ref[...] = _modulate(x_ref[...], nw_ref[...], sh_ref[...], sc_ref[...]).astype(BF16)

    o_ref[...] = jnp.dot(h_ref[...], w_ref[...], preferred_element_type=F32).astype(BF16)


def _inproj(x2, nw, sh, sc, w_slab, S):
    N, D = x2.shape
    tm = 512
    tn = SLAB_COLS // 3
    per_b = S // tm
    return pl.pallas_call(
        _inproj_kernel,
        grid=(N // tm, SLAB_COLS // tn),
        in_specs=[pl.BlockSpec((tm, D), lambda i, j: (i, 0)),
                  pl.BlockSpec((1, D), lambda i, j: (0, 0)),
                  pl.BlockSpec((None, 1, D), lambda i, j: (i // per_b, 0, 0)),
                  pl.BlockSpec((None, 1, D), lambda i, j: (i // per_b, 0, 0)),
                  pl.BlockSpec((D, tn), lambda i, j: (0, j))],
        out_specs=pl.BlockSpec((tm, tn), lambda i, j: (i, j)),
        out_shape=jax.ShapeDtypeStruct((N, SLAB_COLS), BF16),
        scratch_shapes=[pltpu.VMEM((tm, D), BF16)],
        compiler_params=_cparams(("parallel", "arbitrary")),
        name="inproj",
    )(x2, nw.reshape(1, D), sh.reshape(-1, 1, D), sc.reshape(-1, 1, D), w_slab)


def _cmp_kernel(r_ref, pe_ref, w1a_ref, w1b_ref, w2_ref, o_ref):
    r = r_ref[...]
    n_rows = r.shape[0]
    a = jnp.dot(r, w1a_ref[...], preferred_element_type=F32)
    b = jnp.dot(r, w1b_ref[...], preferred_element_type=F32)
    pe = pe_ref[...]
    half = pe.shape[1] // 2
    bias = (jnp.dot(pe[:, :half], w1a_ref[...].astype(F32), preferred_element_type=F32, precision=HIGHEST)
            + jnp.dot(pe[:, half:], w1b_ref[...].astype(F32), preferred_element_type=F32, precision=HIGHEST))
    hid = a + pltpu.roll(b, n_rows - 1, 0) + bias
    act = 0.5 * hid * (1.0 + jnp.tanh(0.7978845608028654 * (hid + 0.044715 * hid * hid * hid)))
    o_ref[...] = jnp.dot(act, w2_ref[...], preferred_element_type=F32, precision=HIGHEST)


def _compress(rows, pe, w1, w2):
    BG, n_rows, width = rows.shape
    hidden = w1.shape[1]
    w1a = w1[:width].astype(BF16)
    w1b = w1[width:].astype(BF16)
    return pl.pallas_call(
        _cmp_kernel,
        grid=(BG,),
        in_specs=[pl.BlockSpec((None, n_rows, width), lambda i: (i, 0, 0)),
                  pl.BlockSpec((1, 2 * width), lambda i: (0, 0)),
                  pl.BlockSpec((width, hidden), lambda i: (0, 0)),
                  pl.BlockSpec((width, hidden), lambda i: (0, 0)),
                  pl.BlockSpec((hidden, NSA_DIM), lambda i: (0, 0))],
        out_specs=pl.BlockSpec((None, n_rows, NSA_DIM), lambda i: (i, 0, 0)),
        out_shape=jax.ShapeDtypeStruct((BG, n_rows, NSA_DIM), F32),
        compiler_params=_cparams(("parallel",)),
        name="cmp",
    )(rows, pe.reshape(1, -1).astype(F32), w1a, w1b, w2.astype(F32))


def _da_kernel(slopes_ref, kmin_ref, qT_ref, k_ref, vT_ref, pos_ref, srow_ref, lq1_ref, lk1_ref, lq2_ref, lk2_ref, subw_ref,
               oT_ref, acc_ref, qa_ref, sa_ref, sb_ref, pa_ref, pb_ref, rel_ref, *, tq, tk, lam_init):
    h = pl.program_id(1)
    qi = pl.program_id(2)
    slope = slopes_ref[h]
    q0 = qi * tq
    w = 2 * tq
    nk = k_ref.shape[0] // tk
    qT = qT_ref[...]
    row = lax.broadcasted_iota(I32, qT.shape, 0)
    zero = jnp.zeros_like(qT)
    qa_ref[0:2 * DA_DIM, :] = jnp.concatenate(
        [jnp.where(row < DA_DIM, qT, zero), jnp.where(row >= DA_DIM, qT, zero)], axis=1)
    qa_ref[2 * DA_DIM:2 * DA_DIM + POS_ROWS, :] = jnp.concatenate([srow_ref[...]] * (w // 128), axis=1).astype(BF16)
    qa_ref[2 * DA_DIM + POS_ROWS:, :] = jnp.zeros((2 * DA_DIM - POS_ROWS, w), BF16)
    lane = lax.broadcasted_iota(I32, (tk, w), 1)
    rel_ref[...] = lax.broadcasted_iota(I32, (tk, w), 0) - jnp.where(lane >= tq, lane - tq, lane)

    def scores(kt, s_ref):
        k0 = kt * tk
        k_t = k_ref[pl.ds(pl.multiple_of(k0, tk), tk), :]
        s = jnp.dot(jnp.concatenate([k_t, pos_ref[...]], axis=1), qa_ref[...], preferred_element_type=F32)
        s = jnp.where(rel_ref[...] <= q0 - k0, s, NEG)
        s_ref[...] = s
        return jnp.max(s, axis=0, keepdims=True)

    def probs(s_ref, p_ref, mx, off, m_old):
        m_new = jnp.maximum(m_old, mx + off)
        p_ref[...] = jnp.exp(s_ref[...] - (m_new - off)).astype(BF16)
        return m_new, jnp.exp(m_old - m_new)

    def accumulate(kt, p_ref, alpha):
        acc_ref[...] = alpha * acc_ref[...] + jnp.dot(vT_ref[kt], p_ref[...], preferred_element_type=F32)

    n_tiles = (q0 + tq - 1) // tk + 1
    kt_min = kmin_ref[(pl.program_id(0) * pl.num_programs(1) + h) * pl.num_programs(2) + qi]
    acc_ref[...] = jnp.zeros_like(acc_ref)
    pb_ref[...] = jnp.zeros_like(pb_ref)
    mx_a = scores(kt_min, sa_ref)

    def pair(j, carry):
        m, mx_a, alpha_b = carry
        ta = kt_min + 2 * j
        tb = ta + 1
        off_a = slope * (ta * tk).astype(F32)
        off_b = jnp.where(tb < n_tiles, slope * (tb * tk).astype(F32), NEG)
        accumulate(jnp.maximum(ta - 1, 0), pb_ref, alpha_b)
        m, alpha_a = probs(sa_ref, pa_ref, mx_a, off_a, m)
        mx_b = scores(jnp.minimum(tb, nk - 1), sb_ref)
        accumulate(ta, pa_ref, alpha_a)
        m, alpha_b = probs(sb_ref, pb_ref, mx_b, off_b, m)
        mx_a = scores(jnp.minimum(ta + 2, nk - 1), sa_ref)
        return m, mx_a, alpha_b

    n_pairs = (n_tiles - kt_min + 1) // 2
    init = (jnp.full((1, w), NEG, F32), mx_a, jnp.ones((1, w), F32))
    _, _, alpha_b = lax.fori_loop(0, n_pairs, pair, init)
    accumulate(jnp.minimum(kt_min + 2 * n_pairs - 1, nk - 1), pb_ref, alpha_b)

    lam = (jnp.exp(jnp.sum(lq1_ref[...] * lk1_ref[...], axis=-1, keepdims=True))
           - jnp.exp(jnp.sum(lq2_ref[...] * lk2_ref[...], axis=-1, keepdims=True)) + lam_init)
    on = acc_ref[0:DA_VDIM, :] * (1.0 / acc_ref[DA_VDIM:DA_VDIM + 1, :])
    o = on[:, :tq] - lam * on[:, tq:]
    r = lax.rsqrt(jnp.mean(o * o, axis=0, keepdims=True) + DA_SUBLN_EPS)
    oT_ref[...] = (o * r * subw_ref[...] * (1.0 - lam_init)).astype(BF16)


EXP_UNDERFLOW = 104.0


def _first_live_tile(slopes_np, slab3, tq, tk):
    B, S, _ = slab3.shape
    H = DA_HEADS
    width = 2 * DA_DIM * H
    grp = jnp.asarray((np.arange(width)[:, None] // DA_DIM == np.arange(2 * H)[None, :]).astype(np.float32))

    def norms(c0):
        v = slab3[:, :, c0:c0 + width].astype(F32)
        return jnp.sqrt(jnp.einsum('bsc,cg->bsg', v * v, grp, precision=HIGHEST))

    qn = norms(COL_DAQ) * DA_DIM ** -0.5
    qn = jnp.max(qn.reshape(B, S // tq, tq, H, 2), axis=2).transpose(0, 2, 3, 1)
    kn = jnp.max(norms(COL_DAK), axis=1).reshape(B, H, 2)
    bound = jnp.max(qn * kn[..., None], axis=2) * 1.01 + 0.01
    q0 = (jnp.arange(S // tq, dtype=F32) * tq)[None, None, :]
    slope = jnp.asarray(slopes_np, F32)[None, :, None]
    first_key = q0 - (EXP_UNDERFLOW + 2.0 * bound) / slope
    kt = jnp.floor(first_key / tk).astype(I32)
    n_tiles = (jnp.arange(S // tq, dtype=I32) * tq + tq - 1) // tk + 1
    return jnp.clip(kt, 0, n_tiles[None, None, :] - 1).reshape(-1)


def _diff_attention(slopes, kmin, qT, slab3, vT, srows, lq1, lk1, lq2, lk2, subw, lam_init, tq, tk):
    B, _, S = qT.shape
    nk = S // tk
    assert tk % tq == 0 and S % tk == 0
    tile = tq
    kcol = COL_DAK // 128
    v_rows = vT.shape[3]
    vec = lambda a: a.reshape(1, DA_DIM).astype(F32)
    grid_spec = pltpu.PrefetchScalarGridSpec(
        num_scalar_prefetch=2,
        grid=(B, DA_HEADS, S // tile),
        in_specs=[pl.BlockSpec((None, 128, tile), lambda b, h, i, s, km: (b, h, i)),
                  pl.BlockSpec((None, S, 128), lambda b, h, i, s, km: (b, 0, kcol + h)),
                  pl.BlockSpec((None, None, nk, v_rows, tk), lambda b, h, i, s, km: (b, h, 0, 0, 0)),
                  pl.BlockSpec((tk, 128), lambda b, h, i, s, km: (0, 0)),
                  pl.BlockSpec((None, POS_ROWS, 128), lambda b, h, i, s, km: (h, 0, 0)),
                  pl.BlockSpec((1, DA_DIM), lambda b, h, i, s, km: (0, 0)),
                  pl.BlockSpec((1, DA_DIM), lambda b, h, i, s, km: (0, 0)),
                  pl.BlockSpec((1, DA_DIM), lambda b, h, i, s, km: (0, 0)),
                  pl.BlockSpec((1, DA_DIM), lambda b, h, i, s, km: (0, 0)),
                  pl.BlockSpec((DA_VDIM, 1), lambda b, h, i, s, km: (0, 0))],
        out_specs=pl.BlockSpec((None, 128, tile), lambda b, h, i, s, km: (b, h, i)),
        scratch_shapes=[pltpu.VMEM((v_rows, 2 * tq), F32), pltpu.VMEM((4 * DA_DIM, 2 * tq), BF16),
                        pltpu.VMEM((tk, 2 * tq), F32), pltpu.VMEM((tk, 2 * tq), F32),
                        pltpu.VMEM((tk, 2 * tq), BF16), pltpu.VMEM((tk, 2 * tq), BF16),
                        pltpu.VMEM((tk, 2 * tq), I32)],
    )
    return pl.pallas_call(
        functools.partial(_da_kernel, tq=tq, tk=tk, lam_init=lam_init),
        grid_spec=grid_spec,
        out_shape=jax.ShapeDtypeStruct((B, DA_HEADS * DA_VDIM, S), BF16),
        compiler_params=_cparams(("parallel", "parallel", "arbitrary")),
        name="diffattn",
    )(slopes, kmin, qT, slab3, vT, _pos_features(tk), srows, vec(lq1), vec(lk1), vec(lq2), vec(lk2),
      subw.reshape(DA_VDIM, 1).astype(F32))


def _nsa_kernel(slopes_ref, qT_ref, kc_ref, vcT_ref, ksa_ref, vsT_ref, kwa_ref, vwT_ref, wb_ref, gT_ref, mT_ref,
                grp_ref, ltri_ref, oT_ref, acc_ref, res_ref, base_ref, qa_ref, sa_ref, sb_ref, pa_ref, pb_ref,
                list_ref, slist_ref, lsem, *, t, tk, n_top):
    g = pl.program_id(1)
    qi = pl.program_id(2)
    q0 = qi * t
    w = NSA_HPG * t
    n_cp = kc_ref.shape[0]
    n_sel = mT_ref.shape[0]
    lane1 = lax.broadcasted_iota(I32, (1, w), 1)
    slope_row = jnp.zeros((1, w), F32)
    for hh in range(NSA_HPG):
        slope_row = jnp.where((lane1 >= hh * t) & (lane1 < (hh + 1) * t), slopes_ref[g * NSA_HPG + hh], slope_row)
    q_all = qT_ref[...]
    gates = jax.nn.sigmoid(gT_ref[...])
    key_off = lax.broadcasted_iota(I32, (tk, w), 0)
    t_loc = jnp.bitwise_and(lax.broadcasted_iota(I32, (tk, w), 1), t - 1)
    base_ref[...] = slope_row * key_off.astype(F32)

    c_idx = lax.broadcasted_iota(I32, (n_cp, w), 0)
    t_c = q0 + jnp.bitwise_and(lax.broadcasted_iota(I32, (n_cp, w), 1), t - 1)
    d_c = t_c - (c_idx * CMP_STRIDE + (CMP_BLOCK - 1))
    ok_c = d_c >= 0
    s = jnp.dot(kc_ref[...], q_all, preferred_element_type=F32) - slope_row * d_c.astype(F32)
    s = jnp.where(ok_c, s, NEG)
    e = jnp.exp(s - jnp.max(s, axis=0, keepdims=True))
    inv = 1.0 / jnp.sum(e, axis=0, keepdims=True)
    p = jnp.where(ok_c, e * inv, 0.0)
    res_ref[...] = gates[0:1, :] * jnp.dot(vcT_ref[...], p.astype(BF16), preferred_element_type=F32)
    p_sum = p[:, 0:t]
    for hh in range(1, NSA_HPG):
        p_sum = p_sum + p[:, hh * t:(hh + 1) * t]

    p_hi = p_sum.astype(BF16)
    p_lo = (p_sum - p_hi.astype(F32)).astype(BF16)
    mT = mT_ref[...]
    imp = jnp.dot(mT, p_hi, preferred_element_type=F32) + jnp.dot(mT, p_lo, preferred_element_type=F32)
    blk = lax.broadcasted_iota(I32, (n_sel, t), 0)
    t_s = q0 + lax.broadcasted_iota(I32, (n_sel, t), 1)
    cur = jnp.right_shift(t_s, int(math.log2(SEL_BLOCK)))
    forced = (blk == 0) | (blk == cur) | (blk == cur - 1)
    score = jnp.where(blk * SEL_BLOCK <= t_s, imp + jnp.where(forced, FORCE_BONUS, 0.0), NEG)
    blk_f = blk.astype(F32)
    sel = jnp.zeros((n_sel, t), F32)
    for _ in range(n_top):
        _, idx = _first_max(score, blk_f)
        hit = blk_f == idx
        sel = jnp.where(hit, 1.0, sel)
        score = jnp.where(hit, -jnp.inf, score)
    n_full = q0 // tk
    nt_pad = grp_ref.shape[0]
    cnt = jnp.broadcast_to(jnp.sum(sel, axis=1, keepdims=True), (n_sel, 128)).astype(BF16)
    tile_cnt = jnp.dot(grp_ref[...], cnt, preferred_element_type=F32)
    kt_i = lax.broadcasted_iota(I32, (nt_pad, 128), 0)
    active = (tile_cnt > 0.5) & (kt_i < n_full)
    act_f = jnp.where(active, 1.0, 0.0)
    before = jnp.dot(ltri_ref[...], act_f.astype(BF16), preferred_element_type=F32)
    lane_j = lax.broadcasted_iota(I32, (nt_pad, 128), 1)
    slot_hit = active & (before == lane_j.astype(F32))
    tiles_row = jnp.sum(jnp.where(slot_hit, kt_i.astype(F32), 0.0), axis=0, keepdims=True)
    n_act_row = jnp.sum(act_f, axis=0, keepdims=True)
    lane8 = lax.broadcasted_iota(I32, (8, 128), 1)
    list_ref[...] = jnp.where(lane8 == 127, n_act_row, tiles_row).astype(I32)
    list_copy = pltpu.make_async_copy(list_ref, slist_ref, lsem)
    list_copy.start()

    selb = jnp.where(sel > 0.5, 0.0, NEG)
    row64 = lax.broadcasted_iota(I32, (NSA_DIM, w), 0)
    qa_ref[0:NSA_DIM, :] = q_all
    qa_ref[NSA_DIM:2 * NSA_DIM, :] = jnp.where(row64 == 0, NEG, 0.0).astype(BF16)
    qa_ref[2 * NSA_DIM:2 * NSA_DIM + n_sel, :] = jnp.concatenate([selb] * NSA_HPG, axis=1).astype(BF16)
    if n_sel < SEL_COLS:
        qa_ref[2 * NSA_DIM + n_sel:, :] = jnp.zeros((SEL_COLS - n_sel, w), BF16)

    n_wt = (WINDOW + t) // t
    kw_t = kwa_ref[pl.ds(pl.multiple_of(q0, t), WINDOW + t), :]
    s = jnp.dot(kw_t, qa_ref[0:2 * NSA_DIM, :], preferred_element_type=F32) + wb_ref[...]
    p = jnp.exp(s - jnp.max(s, axis=0, keepdims=True))
    inv = 1.0 / jnp.sum(p, axis=0, keepdims=True)
    p = p.astype(BF16)
    o_w = jnp.dot(vwT_ref[qi], p[0:t, :], preferred_element_type=F32)
    for c in range(1, n_wt):
        o_w = o_w + jnp.dot(vwT_ref[qi + c], p[c * t:(c + 1) * t, :], preferred_element_type=F32)
    res_ref[...] = res_ref[...] + gates[2:3, :] * (o_w * inv)

    def scores(kt, s_ref):
        k_t = ksa_ref[pl.ds(pl.multiple_of(kt * tk, tk), tk), :]
        s = jnp.dot(k_t, qa_ref[...], preferred_element_type=F32) + base_ref[...]
        s_ref[...] = s
        return jnp.max(s, axis=0, keepdims=True)

    def probs(s_ref, p_ref, mx, off, m_old, l_old):
        m_new = jnp.maximum(m_old, mx + off)
        p = jnp.exp(s_ref[...] - (m_new - off))
        alpha = jnp.exp(m_old - m_new)
        p_ref[...] = p.astype(BF16)
        return m_new, alpha * l_old + jnp.sum(p, axis=0, keepdims=True), alpha

    def accumulate(kt, p_ref, alpha):
        acc_ref[...] = alpha * acc_ref[...] + jnp.dot(vsT_ref[kt], p_ref[...], preferred_element_type=F32)

    list_copy.wait()
    n_act = slist_ref[0, 127]
    mx_a = scores(slist_ref[0, 0], sa_ref)

    kd = n_full * tk
    s = jnp.dot(ksa_ref[pl.ds(pl.multiple_of(kd, tk), tk), :], qa_ref[...], preferred_element_type=F32) + base_ref[...]
    s = jnp.where(key_off - t_loc <= q0 - kd, s, NEG)
    off_d = slope_row * kd.astype(F32)
    m = jnp.max(s, axis=0, keepdims=True) + off_d
    p = jnp.exp(s - (m - off_d))
    l = jnp.sum(p, axis=0, keepdims=True)
    acc_ref[...] = jnp.dot(vsT_ref[n_full], p.astype(BF16), preferred_element_type=F32)
    pb_ref[...] = jnp.zeros_like(pb_ref)

    def pair(j, carry):
        m, l, mx_a, alpha_b, tb_prev = carry
        ta = slist_ref[0, 2 * j]
        tb = slist_ref[0, 2 * j + 1]
        off_a = slope_row * (ta * tk).astype(F32)
        off_b = jnp.where(2 * j + 1 < n_act, slope_row * (tb * tk).astype(F32), NEG)
        accumulate(tb_prev, pb_ref, alpha_b)
        m, l, alpha_a = probs(sa_ref, pa_ref, mx_a, off_a, m, l)
        mx_b = scores(tb, sb_ref)
        accumulate(ta, pa_ref, alpha_a)
        m, l, alpha_b = probs(sb_ref, pb_ref, mx_b, off_b, m, l)
        mx_a = scores(slist_ref[0, 2 * j + 2], sa_ref)
        return m, l, mx_a, alpha_b, tb

    n_pairs = (n_act + 1) // 2
    m, l, _, alpha_b, tb_last = lax.fori_loop(0, n_pairs, pair, (m, l, mx_a, jnp.ones((1, w), F32), jnp.int32(0)))
    accumulate(tb_last, pb_ref, alpha_b)
    oT_ref[...] = (res_ref[...] + gates[1:2, :] * (acc_ref[...] * (1.0 / l))).astype(BF16)


def _nsa_attention(slopes, qT, kc, vcT, ksa, vsT, kwa, vwT, wb, gT, mT, t, tk):
    B, G, nq, _, w = qT.shape
    S = nq * t
    n_cp = kc.shape[2]
    n_sel = S // SEL_BLOCK
    n_top = min(SEL_TOPK, n_sel)
    assert tk % t == 0 and S % tk == 0 and t & (t - 1) == 0 and WINDOW % t == 0 and n_sel <= SEL_COLS
    n_tiles = S // tk
    nt_pad = -(-n_tiles // 16) * 16
    assert nt_pad + 2 < 127
    grp = jnp.asarray((np.arange(n_sel)[None, :] // (tk // SEL_BLOCK) == np.arange(nt_pad)[:, None])
                      .astype(np.float32), BF16)
    ltri = jnp.asarray((np.arange(nt_pad)[None, :] < np.arange(nt_pad)[:, None]).astype(np.float32), BF16)
    fixed = lambda b, g, i, s: (b, g, 0, 0)
    fixed5 = lambda b, g, i, s: (b, g, 0, 0, 0)
    tile5 = lambda b, g, i, s: (b, g, i, 0, 0)
    grid_spec = pltpu.PrefetchScalarGridSpec(
        num_scalar_prefetch=1,
        grid=(B, G, nq),
        in_specs=[pl.BlockSpec((None, None, None, NSA_DIM, w), tile5),
                  pl.BlockSpec((None, None, n_cp, NSA_DIM), fixed),
                  pl.BlockSpec((None, None, NSA_DIM, n_cp), fixed),
                  pl.BlockSpec((None, None, S, ksa.shape[-1]), fixed),
                  pl.BlockSpec((None, None, S // tk, NSA_DIM, tk), fixed5),
                  pl.BlockSpec((None, None, S + WINDOW, 2 * NSA_DIM), fixed),
                  pl.BlockSpec((None, None, (S + WINDOW) // t, NSA_DIM, t), fixed5),
                  pl.BlockSpec((None, WINDOW + t, w), lambda b, g, i, s: (g, 0, 0)),
                  pl.BlockSpec((None, None, None, 3, w), tile5),
                  pl.BlockSpec((n_sel, n_cp), lambda b, g, i, s: (0, 0)),
                  pl.BlockSpec((nt_pad, n_sel), lambda b, g, i, s: (0, 0)),
                  pl.BlockSpec((nt_pad, nt_pad), lambda b, g, i, s: (0, 0))],
        out_specs=pl.BlockSpec((None, None, None, NSA_DIM, w), tile5),
        scratch_shapes=[pltpu.VMEM((NSA_DIM, w), F32),
                        pltpu.VMEM((NSA_DIM, w), F32),
                        pltpu.VMEM((tk, w), F32),
                        pltpu.VMEM((2 * NSA_DIM + SEL_COLS, w), BF16),
                        pltpu.VMEM((tk, w), F32), pltpu.VMEM((tk, w), F32),
                        pltpu.VMEM((tk, w), BF16), pltpu.VMEM((tk, w), BF16),
                        pltpu.VMEM((8, 128), I32), pltpu.SMEM((8, 128), I32), pltpu.SemaphoreType.DMA(())],
    )
    return pl.pallas_call(
        functools.partial(_nsa_kernel, t=t, tk=tk, n_top=n_top),
        grid_spec=grid_spec,
        out_shape=jax.ShapeDtypeStruct((B, G, nq, NSA_DIM, w), BF16),
        compiler_params=_cparams(("parallel", "parallel", "arbitrary")),
        name="nsa",
    )(slopes, qT, kc, vcT, ksa, vsT, kwa, vwT, wb, gT, mT, grp, ltri)


def _pack_halves(v):
    c = v.shape[1] // 2
    lo = lax.bitcast_convert_type(v[:, :c].astype(BF16).astype(F32), U32)
    hi = lax.bitcast_convert_type(v[:, c:].astype(BF16).astype(F32), U32)
    return hi | (lo >> 16)


def _unpack_halves(u):
    lo = lax.bitcast_convert_type(u << 16, F32)
    hi = lax.bitcast_convert_type(u & jnp.uint32(0xFFFF0000), F32)
    return lo, hi


def _merge_kernel(oa_ref, ob_ref, ga_ref, gb_ref, x_ref, g1_ref, nw_ref, sh_ref, sc_ref,
                  wa_ref, wb_ref, wo_ref, wr_ref, x1_ref, h2_ref, lg_ref):
    ya = jnp.dot(oa_ref[...], wa_ref[...], preferred_element_type=F32)
    yb = jnp.dot(ob_ref[...], wb_ref[...], preferred_element_type=F32)
    merged = (jax.nn.sigmoid(ga_ref[...].astype(F32)) * ya
              + jax.nn.sigmoid(gb_ref[...].astype(F32)) * yb).astype(BF16)
    mix = jnp.dot(merged, wo_ref[...], preferred_element_type=F32)
    x1 = x_ref[...] + g1_ref[...] * mix
    x1_ref[...] = x1
    h2 = _modulate(x1, nw_ref[...], sh_ref[...], sc_ref[...])
    h2_ref[...] = _pack_halves(h2)
    lg_ref[...] = jnp.dot(h2, wr_ref[...], preferred_element_type=F32, precision=HIGHEST)


def _merge(oa, ob, slab, x2, g1, nw, sh, sc, wa, wb, wo, wr, S):
    N, D = x2.shape
    tm = 256
    per_b = S // tm
    mcol = COL_MERGE // D
    row = lambda i: (i, 0)
    full = lambda i: (0, 0)
    perb = lambda i: (i // per_b, 0, 0)
    n_lg = wr.shape[1]
    return pl.pallas_call(
        _merge_kernel,
        grid=(N // tm,),
        in_specs=[pl.BlockSpec((tm, oa.shape[1]), row),
                  pl.BlockSpec((tm, ob.shape[1]), row),
                  pl.BlockSpec((tm, D), lambda i: (i, mcol)),
                  pl.BlockSpec((tm, D), lambda i: (i, mcol + 1)),
                  pl.BlockSpec((tm, D), row),
                  pl.BlockSpec((None, 1, D), perb),
                  pl.BlockSpec((1, D), full),
                  pl.BlockSpec((None, 1, D), perb),
                  pl.BlockSpec((None, 1, D), perb),
                  pl.BlockSpec(wa.shape, full),
                  pl.BlockSpec(wb.shape, full),
                  pl.BlockSpec(wo.shape, full),
                  pl.BlockSpec(wr.shape, full)],
        out_specs=[pl.BlockSpec((tm, D), row), pl.BlockSpec((tm, D // 2), row), pl.BlockSpec((tm, n_lg), row)],
        out_shape=[jax.ShapeDtypeStruct((N, D), F32), jax.ShapeDtypeStruct((N, D // 2), U32),
                   jax.ShapeDtypeStruct((N, n_lg), F32)],
        compiler_params=_cparams(("parallel",)),
        name="merge",
    )(oa, ob, slab, slab, x2, g1.reshape(-1, 1, D), nw.reshape(1, D), sh.reshape(-1, 1, D),
      sc.reshape(-1, 1, D), wa, wb, wo, wr)


def _first_max(v, idx):
    mx = jnp.max(v, axis=0, keepdims=True)
    first = jnp.min(jnp.where(v == mx, idx, float(v.shape[0])), axis=0, keepdims=True)
    return mx, first


def _route_kernel(lg_ref, b_ref, e_ref, w_ref):
    lg = lg_ref[...]
    tn = lg.shape[1]
    per_g = N_EXPERTS // N_EXPERT_GROUPS
    scores = jax.nn.sigmoid(lg)
    biased = scores + b_ref[...]
    gi = lax.broadcasted_iota(I32, (N_EXPERT_GROUPS, tn), 0).astype(F32)
    gscore = jnp.zeros((N_EXPERT_GROUPS, tn), F32)
    for g in range(N_EXPERT_GROUPS):
        grp = biased[g * per_g:(g + 1) * per_g, :]
        m1, f1 = _first_max(grp, gi)
        m2 = jnp.max(jnp.where(gi == f1, -jnp.inf, grp), axis=0, keepdims=True)
        gscore = jnp.where(gi == g, m1 + m2, gscore)
    gsel = jnp.zeros((N_EXPERT_GROUPS, tn), F32)
    for _ in range(TOPK_EXPERT_GROUPS):
        _, f = _first_max(gscore, gi)
        hit = gi == f
        gsel = jnp.where(hit, 1.0, gsel)
        gscore = jnp.where(hit, -jnp.inf, gscore)
    ei = lax.broadcasted_iota(I32, (N_EXPERTS, tn), 0).astype(F32)
    emask = jnp.zeros((N_EXPERTS, tn), F32)
    for g in range(N_EXPERT_GROUPS):
        in_g = (ei >= g * per_g) & (ei < (g + 1) * per_g)
        emask = jnp.where(in_g, gsel[g:g + 1, :], emask)
    masked = jnp.where(emask > 0.5, biased, NEG)
    e_out = jnp.zeros((MOE_TOPK, tn), F32)
    w_out = jnp.zeros((MOE_TOPK, tn), F32)
    for r in range(MOE_TOPK):
        _, f = _first_max(masked, ei)
        hit = ei == f
        wv = jnp.sum(jnp.where(hit, scores, 0.0), axis=0, keepdims=True)
        e_out = jnp.where(gi == r, f, e_out)
        w_out = jnp.where(gi == r, wv, w_out)
        masked = jnp.where(hit, -jnp.inf, masked)
    e_ref[...] = e_out.astype(I32)
    w_ref[...] = w_out / jnp.sum(w_out, axis=0, keepdims=True) * ROUTED_SCALE


def _route(lgT, rb):
    E, N = lgT.shape
    tn = 512
    return pl.pallas_call(
        _route_kernel,
        grid=(N // tn,),
        in_specs=[pl.BlockSpec((E, tn), lambda i: (0, i)), pl.BlockSpec((E, 1), lambda i: (0, 0))],
        out_specs=[pl.BlockSpec((MOE_TOPK, tn), lambda i: (0, i)), pl.BlockSpec((MOE_TOPK, tn), lambda i: (0, i))],
        out_shape=[jax.ShapeDtypeStruct((MOE_TOPK, N), I32), jax.ShapeDtypeStruct((MOE_TOPK, N), F32)],
        compiler_params=_cparams(("parallel",)),
        name="route",
    )(lgT, rb.reshape(E, 1).astype(F32))


def _slots_kernel(e_ref, tri_ref, lt_ref, dest_ref, cnt_ref, carry_ref, *, rb):
    phase = pl.program_id(0)
    i = pl.program_id(1)
    e = e_ref[...]
    tn = e.shape[1]
    ei = lax.broadcasted_iota(I32, (N_EXPERTS, tn), 0)

    @pl.when((phase == 0) & (i == 0))
    def _():
        carry_ref[...] = jnp.zeros_like(carry_ref)

    @pl.when(phase == 0)
    def _():
        tot = jnp.zeros((N_EXPERTS, 1), F32)
        for k in range(MOE_TOPK):
            oh = jnp.where(ei == e[k:k + 1, :], 1.0, 0.0)
            tot = tot + jnp.sum(oh, axis=1, keepdims=True)
        carry_ref[...] = carry_ref[...] + tot
        dest_ref[...] = jnp.zeros_like(dest_ref)

    @pl.when((phase == 1) & (i == 0))
    def _():
        cnt = carry_ref[...]
        cnt_ref[...] = cnt
        padded = jnp.broadcast_to(jnp.ceil(cnt / rb) * rb, (N_EXPERTS, 128))
        first = jnp.dot(lt_ref[...], padded, preferred_element_type=F32, precision=HIGHEST)
        carry_ref[...] = first[:, :1]

    @pl.when(phase == 1)
    def _():
        carry = carry_ref[...]
        ki = lax.broadcasted_iota(I32, (MOE_TOPK, tn), 0)
        dest = jnp.zeros((MOE_TOPK, tn), F32)
        for k in range(MOE_TOPK):
            hit = ei == e[k:k + 1, :]
            oh = jnp.where(hit, 1.0, 0.0)
            before = jnp.dot(oh.astype(BF16), tri_ref[...], preferred_element_type=F32)
            row = jnp.sum(jnp.where(hit, before + carry, 0.0), axis=0, keepdims=True)
            dest = jnp.where(ki == k, row, dest)
            carry = carry + jnp.sum(oh, axis=1, keepdims=True)
        carry_ref[...] = carry
        dest_ref[...] = dest.astype(I32)


def _slots(top_eT, rb):
    K, N = top_eT.shape
    tn = 512
    idx = np.arange(tn)
    tri = jnp.asarray((idx[:, None] < idx[None, :]).astype(np.float32), BF16)
    e_idx = np.arange(N_EXPERTS)
    lt = jnp.asarray((e_idx[None, :] < e_idx[:, None]).astype(np.float32), F32)
    return pl.pallas_call(
        functools.partial(_slots_kernel, rb=rb),
        grid=(2, N // tn),
        in_specs=[pl.BlockSpec((K, tn), lambda p, i: (0, i)),
                  pl.BlockSpec((tn, tn), lambda p, i: (0, 0)),
                  pl.BlockSpec((N_EXPERTS, N_EXPERTS), lambda p, i: (0, 0))],
        out_specs=[pl.BlockSpec((K, tn), lambda p, i: (0, i * p)),
                   pl.BlockSpec((N_EXPERTS, 1), lambda p, i: (0, 0))],
        out_shape=[jax.ShapeDtypeStruct((K, N), I32), jax.ShapeDtypeStruct((N_EXPERTS, 1), F32)],
        scratch_shapes=[pltpu.VMEM((N_EXPERTS, 1), F32)],
        compiler_params=_cparams(("arbitrary", "arbitrary")),
        name="slots",
    )(top_eT, tri, lt)


def _dispatch_kernel(dest_ref, h_ref, xs_in_ref, xs_ref, sem, *, tt):
    del xs_in_ref

    def body(n, c):
        src = h_ref.at[pl.ds(n, 1), :]
        for k in range(MOE_TOPK):
            d = dest_ref[0, 0, k * tt + n]
            pltpu.make_async_copy(src, xs_ref.at[pl.ds(d, 1), :], sem).start(priority=k % 2)
        return c

    lax.fori_loop(0, tt, body, 0)
    for k in range(MOE_TOPK):
        pltpu.make_async_copy(h_ref, xs_ref.at[pl.ds(0, tt), :], sem).wait()


def _dispatch(dest_tiles, h2, n_rows, tt):
    N, D = h2.shape
    xs0 = jnp.zeros((n_rows, D), h2.dtype)
    return pl.pallas_call(
        functools.partial(_dispatch_kernel, tt=tt),
        grid=(N // tt,),
        in_specs=[pl.BlockSpec((1, 1, MOE_TOPK * tt), lambda i: (i, 0, 0), memory_space=pltpu.SMEM),
                  pl.BlockSpec((tt, D), lambda i: (i, 0)),
                  pl.BlockSpec(memory_space=pl.ANY)],
        out_specs=pl.BlockSpec(memory_space=pl.ANY),
        out_shape=jax.ShapeDtypeStruct((n_rows, D), h2.dtype),
        scratch_shapes=[pltpu.SemaphoreType.DMA(())],
        input_output_aliases={2: 0},
        compiler_params=_cparams(("arbitrary",)),
        name="dispatch",
    )(dest_tiles, h2, xs0)


def _experts_kernel(be_ref, x_ref, wgu_ref, wd_ref, y_ref):
    del be_ref
    lo, hi = _unpack_halves(x_ref[...])
    x = jnp.concatenate([lo.astype(BF16), hi.astype(BF16)], axis=1)
    gu = jnp.dot(x, wgu_ref[...], preferred_element_type=F32)
    hdim = gu.shape[1] // 2
    g = gu[:, :hdim]
    act = (g * jax.nn.sigmoid(g) * gu[:, hdim:]).astype(BF16)
    packed = _pack_halves(jnp.dot(act, wd_ref[...], preferred_element_type=F32))
    for s in range(SC_SPLIT):
        y_ref[s] = packed[:, s * SC_PIECE:(s + 1) * SC_PIECE]


def _experts(blk_e, xs, wgu, wd, rb):
    n_rows, D = xs.shape
    grid_spec = pltpu.PrefetchScalarGridSpec(
        num_scalar_prefetch=1,
        grid=(n_rows // rb,),
        in_specs=[pl.BlockSpec((rb, D), lambda i, be: (i, 0)),
                  pl.BlockSpec((None,) + wgu.shape[1:], lambda i, be: (be[i], 0, 0)),
                  pl.BlockSpec((None,) + wd.shape[1:], lambda i, be: (be[i], 0, 0))],
        out_specs=pl.BlockSpec((SC_SPLIT, rb, SC_PIECE), lambda i, be: (0, i, 0)),
    )
    return pl.pallas_call(
        _experts_kernel,
        grid_spec=grid_spec,
        out_shape=jax.ShapeDtypeStruct((SC_SPLIT, n_rows, SC_PIECE), U32),
        compiler_params=_cparams(("arbitrary",)),
        name="experts",
    )(blk_e, xs, wgu, wd)


def _sc_gather_rows(table, idx):
    n = idx.shape[0]
    width = table.shape[1]
    mesh = plsc.VectorSubcoreMesh(core_axis_name="c", subcore_axis_name="s")

    @pl.kernel(out_type=jax.ShapeDtypeStruct((n, width), table.dtype), mesh=mesh)
    def gather_kernel(t_hbm, i_hbm, o_hbm):
        def body(i_vmem, o_vmem):
            pltpu.sync_copy(t_hbm.at[i_vmem.at[0]], o_vmem)

        pltpu.emit_pipeline(
            body,
            grid=(n // SC_WINDOW,),
            in_specs=[pl.BlockSpec((1, SC_WINDOW), lambda i: (0, i))],
            out_specs=[pl.BlockSpec((SC_WINDOW, width), lambda i: (i, 0))],
            core_axis_name=("c", "s"),
            dimension_semantics=(pltpu.PARALLEL,),
        )(i_hbm, o_hbm)

    return gather_kernel(table, idx.reshape(1, n))


def _combine_kernel(yg_ref, w_ref, h_ref, x1_ref, g2_ref, fw_ref, sgu_ref, sd_ref, o_ref):
    h_lo, h_hi = _unpack_halves(h_ref[...])
    h = jnp.concatenate([h_lo.astype(BF16), h_hi.astype(BF16)], axis=1)
    gu = jnp.dot(h, sgu_ref[...], preferred_element_type=F32)
    hdim = gu.shape[1] // 2
    g = gu[:, :hdim]
    ffn = jnp.dot((g * jax.nn.sigmoid(g) * gu[:, hdim:]).astype(BF16), sd_ref[...], preferred_element_type=F32)

    w = w_ref[...]
    tt = w.shape[0]
    lo = [jnp.zeros((tt, SC_PIECE), F32) for _ in range(SC_SPLIT)]
    hi = [jnp.zeros((tt, SC_PIECE), F32) for _ in range(SC_SPLIT)]
    for k in range(MOE_TOPK):
        wk = w[:, k:k + 1]
        for s in range(SC_SPLIT):
            y_lo, y_hi = _unpack_halves(yg_ref[s, k])
            lo[s] = lo[s] + wk * y_lo
            hi[s] = hi[s] + wk * y_hi
    ffn = ffn + jnp.concatenate(lo + hi, axis=1)
    x2 = x1_ref[...] + g2_ref[...] * ffn
    r = lax.rsqrt(jnp.mean(x2 * x2, axis=-1, keepdims=True) + NORM_EPS)
    o_ref[...] = x2 * r * fw_ref[...]


def _combine(yg, top_w, h2p, x1, g2, fw, sgu, sd, S, tt):
    N, D = x1.shape
    per_b = S // tt
    row = lambda i: (i, 0)
    full = lambda i: (0, 0)
    return pl.pallas_call(
        _combine_kernel,
        grid=(N // tt,),
        in_specs=[pl.BlockSpec((SC_SPLIT, MOE_TOPK, tt, SC_PIECE), lambda i: (0, 0, i, 0)),
                  pl.BlockSpec((tt, MOE_TOPK), row),
                  pl.BlockSpec((tt, D // 2), row),
                  pl.BlockSpec((tt, D), row),
                  pl.BlockSpec((None, 1, D), lambda i: (i // per_b, 0, 0)),
                  pl.BlockSpec((1, D), full),
                  pl.BlockSpec(sgu.shape, full),
                  pl.BlockSpec(sd.shape, full)],
        out_specs=pl.BlockSpec((tt, D), row),
        out_shape=jax.ShapeDtypeStruct((N, D), F32),
        compiler_params=_cparams(("parallel",)),
        name="combine",
    )(yg, top_w, h2p, x1, g2.reshape(-1, 1, D), fw.reshape(1, D), sgu, sd)


def _cmp_to_sel_T(n_cp, n_sel):
    c0 = np.arange(n_cp)[None, :] * CMP_STRIDE
    s0 = np.arange(n_sel)[:, None] * SEL_BLOCK
    ov = np.minimum(c0 + CMP_BLOCK, s0 + SEL_BLOCK) - np.maximum(c0, s0)
    m = np.clip(ov, 0, None).astype(np.float32) / CMP_BLOCK
    m[:, n_cp - 1] = 0.0
    return jnp.asarray(m, BF16)


def _slab_weights(w_in):
    D = w_in.shape[0]
    sizes = [1024, 1024, 1024, 512, 128, 128, 128, 128, 128, 128, 24, 2048]
    offs = np.concatenate([[0], np.cumsum(sizes)])
    part = lambda i: w_in[:, offs[i]:offs[i + 1]]
    pieces = [part(0), part(1), part(2), part(11), part(3)] + [part(i) for i in range(4, 10)] + [part(10)]
    w = jnp.concatenate(pieces, axis=1)
    return jnp.pad(w, ((0, 0), (0, SLAB_COLS - w.shape[1]))).astype(BF16)


def _layer(x, c, ada_w, ada_b, norm1_w, w_in, da_lq1, da_lk1, da_lq2, da_lk2, da_subln_w,
           cmp_k_pe, cmp_k_w1, cmp_k_w2, cmp_v_pe, cmp_v_w1, cmp_v_w2, w_da_out, w_nsa_out, w_o,
           norm2_w, router_w, router_b, exp_w_gate, exp_w_up, exp_w_down,
           sh_w_gate, sh_w_up, sh_w_down, final_norm_w, lam_init):
    B, S, D = x.shape
    N = B * S
    G, HPG = NSA_GROUPS, NSA_HPG
    x2 = x.reshape(N, D)

    mod = _ada(c, ada_w, ada_b)
    sh1, sc1, g1, sh2, sc2, g2 = jnp.split(mod, 6, axis=-1)

    slab = _inproj(x2, norm1_w, sh1, sc1, _slab_weights(w_in), S)
    slab3 = slab.reshape(B, S, SLAB_COLS)

    i_all = np.arange(1, DA_HEADS + NSA_HEADS + 1, dtype=np.float32)
    slopes = (2.0 ** (-8.0 * i_all / (DA_HEADS + NSA_HEADS))).astype(np.float32)
    slopes_a = jnp.asarray(slopes[0::2])
    slopes_b = jnp.asarray(slopes[1::2])

    ta_q, ta_k = min(512, S), min(512, S)
    qT = (slab3[:, :, COL_DAQ:COL_DAQ + 1024] * jnp.asarray(DA_DIM ** -0.5, BF16)).transpose(0, 2, 1)
    vT = slab3[:, :, COL_DAV:COL_DAV + 1024].reshape(B, S // ta_k, ta_k, DA_HEADS, DA_VDIM).transpose(0, 3, 1, 4, 2)
    kmin = _first_live_tile(slopes[0::2], slab3, ta_q, ta_k)
    oaT = _diff_attention(slopes_a, kmin, qT, slab3, _with_ones_row(vT), _slope_rows(slopes[0::2]), da_lq1, da_lk1,
                          da_lq2, da_lk2, da_subln_w, lam_init, ta_q, ta_k)
    oa = oaT.transpose(0, 2, 1).reshape(N, DA_HEADS * DA_VDIM)

    tb, tb_k = 128, min(256, S)
    n_cp = S // CMP_STRIDE
    n_sel = S // SEL_BLOCK

    def kv_groups(j):
        c0 = COL_KV6 + 128 * j
        return slab3[:, :, c0:c0 + 128].reshape(B, S, G, NSA_DIM).transpose(0, 2, 1, 3)

    def kv_tiles_T(a, rows):
        return a.reshape(B, G, S // rows, rows, NSA_DIM).transpose(0, 1, 2, 4, 3)

    ck_rows = kv_groups(0).reshape(B * G, n_cp, CMP_STRIDE * NSA_DIM)
    cv_rows = kv_groups(1).reshape(B * G, n_cp, CMP_STRIDE * NSA_DIM)
    kc = _compress(ck_rows, cmp_k_pe, cmp_k_w1, cmp_k_w2).reshape(B, G, n_cp, NSA_DIM).astype(BF16)
    vc = _compress(cv_rows, cmp_v_pe, cmp_v_w1, cmp_v_w2).reshape(B, G, n_cp, NSA_DIM).astype(BF16)
    nq = (slab3[:, :, COL_NSAQ:COL_NSAQ + 512] * jnp.asarray(NSA_DIM ** -0.5, BF16))
    nqT = nq.reshape(B, S // tb, tb, G, HPG, NSA_DIM).transpose(0, 3, 1, 5, 4, 2).reshape(B, G, S // tb, NSA_DIM, HPG * tb)
    gT = (slab3[:, :, COL_NSAG:COL_NSAG + 3 * NSA_HEADS].astype(F32).reshape(B, S // tb, tb, G, HPG, 3)
          .transpose(0, 3, 1, 5, 4, 2).reshape(B, G, S // tb, 3, HPG * tb))
    onehot = jnp.asarray((np.arange(S)[:, None] // SEL_BLOCK == np.arange(SEL_COLS)[None, :]).astype(np.float32), BF16)
    zeros64 = jnp.zeros((B, G, S, NSA_DIM), BF16)
    ksa = jnp.concatenate([kv_groups(2), zeros64, jnp.broadcast_to(onehot, (B, G, S, SEL_COLS))], axis=-1)
    pad_rows = jnp.zeros((B, G, WINDOW, 2 * NSA_DIM), BF16).at[..., NSA_DIM].set(1.0)
    kwa = jnp.concatenate([pad_rows, jnp.concatenate([kv_groups(4), zeros64], axis=-1)], axis=2)
    vw_pad = jnp.pad(kv_groups(5), ((0, 0), (0, 0), (WINDOW, 0), (0, 0)))
    vwT = vw_pad.reshape(B, G, (S + WINDOW) // tb, tb, NSA_DIM).transpose(0, 1, 2, 4, 3)
    r_w = np.arange(WINDOW + tb)[:, None]
    lane_w = np.arange(HPG * tb)[None, :]
    d_w = (WINDOW + lane_w % tb - r_w).astype(np.float32)
    slope_w = slopes[1::2].reshape(G, 1, HPG)[:, :, lane_w[0] // tb]
    wb = jnp.asarray(np.where((d_w >= 0) & (d_w < WINDOW), -slope_w * d_w[None], np.float32(NEG)).astype(np.float32))
    obT = _nsa_attention(slopes_b, nqT, kc, vc.transpose(0, 1, 3, 2), ksa, kv_tiles_T(kv_groups(3), tb_k),
                         kwa, vwT, wb, gT, _cmp_to_sel_T(n_cp, n_sel), tb, tb_k)
    ob = (obT.reshape(B, G, S // tb, NSA_DIM, HPG, tb).transpose(0, 2, 5, 1, 4, 3)
          .reshape(N, NSA_HEADS * NSA_DIM))

    x1, h2, logits = _merge(oa, ob, slab, x2, g1, norm2_w, sh2, sc2, w_da_out.astype(BF16),
                            w_nsa_out.astype(BF16), w_o.astype(BF16), router_w.astype(F32), S)

    rb = 512
    tt = 128
    top_eT, top_wT = _route(logits.T, router_b)
    destT, counts = _slots(top_eT, rb)
    n_rows = ((N * MOE_TOPK + N_EXPERTS * (rb - 1) + rb - 1) // rb) * rb
    padded = (jnp.ceil(counts[:, 0] / rb) * rb).astype(I32)
    pend = jnp.cumsum(padded)
    blk_start = jnp.arange(n_rows // rb, dtype=I32) * rb
    blk_e = jnp.minimum(jnp.sum((pend[None, :] <= blk_start[:, None]).astype(I32), axis=1), N_EXPERTS - 1)
    dest_tiles = destT.reshape(MOE_TOPK, N // tt, tt).transpose(1, 0, 2).reshape(N // tt, 1, MOE_TOPK * tt)
    xs = _dispatch(dest_tiles, h2, n_rows, tt)
    wgu = jnp.concatenate([exp_w_gate, exp_w_up], axis=-1).astype(BF16)
    ys = _experts(blk_e, xs, wgu, exp_w_down.astype(BF16), rb)
    sgu = jnp.concatenate([sh_w_gate, sh_w_up], axis=-1).astype(BF16)
    d_flat = destT.reshape(-1)
    piece_idx = jnp.concatenate([d_flat + s * n_rows for s in range(SC_SPLIT)])
    yg = _sc_gather_rows(ys.reshape(SC_SPLIT * n_rows, SC_PIECE), piece_idx).reshape(SC_SPLIT, MOE_TOPK, N, SC_PIECE)
    out = _combine(yg, top_wT.T, h2, x1, g2, final_norm_w, sgu, sh_w_down.astype(BF16), S, 2 * tt)
    return out.reshape(B, S, D)


def kernel(x, c, ada_w, ada_b, norm1_w, w_in, da_lq1, da_lk1, da_lq2, da_lk2, da_subln_w, cmp_k_pe, cmp_k_w1, cmp_k_w2, cmp_v_pe, cmp_v_w1, cmp_v_w2, w_da_out, w_nsa_out, w_o, norm2_w, router_w, router_b, exp_w_gate, exp_w_up, exp_w_down, sh_w_gate, sh_w_up, sh_w_down, final_norm_w):
    depth = ada_w.shape[0]
    assert depth == 1, "one decoder layer"
    lam_init = 0.8 - 0.6 * math.exp(-0.3 * 0)
    return _layer(x, c, ada_w[0], ada_b[0], norm1_w[0], w_in[0], da_lq1[0], da_lk1[0], da_lq2[0], da_lk2[0],
                  da_subln_w[0], cmp_k_pe[0], cmp_k_w1[0], cmp_k_w2[0], cmp_v_pe[0], cmp_v_w1[0], cmp_v_w2[0],
                  w_da_out[0], w_nsa_out[0], w_o[0], norm2_w[0], router_w[0], router_b[0],
                  exp_w_gate[0], exp_w_up[0], exp_w_down[0], sh_w_gate[0], sh_w_up[0], sh_w_down[0],
                  final_norm_w, lam_init)
```

```python
import functools
import math

import numpy as np
import jax
import jax.numpy as jnp
from jax import lax
from jax.experimental import pallas as pl
from jax.experimental.pallas import tpu as pltpu
from jax.experimental.pallas import tpu_sc as plsc

F32 = jnp.float32
BF16 = jnp.bfloat16
I32 = jnp.int32
U32 = jnp.uint32
HIGHEST = lax.Precision.HIGHEST

NORM_EPS = 1e-6
NEG = -1e30

DA_HEADS = 8
DA_DIM = 64
DA_VDIM = 128
DA_SUBLN_EPS = 1e-5

NSA_HEADS = 8
NSA_GROUPS = 2
NSA_HPG = 4
NSA_DIM = 64
CMP_BLOCK = 32
CMP_STRIDE = 16
SEL_BLOCK = 64
SEL_TOPK = 16
WINDOW = 512
FORCE_BONUS = 1e4
SEL_COLS = 128

N_EXPERTS = 64
MOE_TOPK = 8
N_EXPERT_GROUPS = 8
TOPK_EXPERT_GROUPS = 4
ROUTED_SCALE = 2.5

VMEM_LIMIT_V7X = 56 * 1024 * 1024

COL_DAQ, COL_DAK, COL_DAV = 0, 1024, 2048
COL_MERGE = 3072
COL_NSAQ = 5120
COL_KV6 = 5632
COL_NSAG = 6400
SLAB_COLS = 6528


SUBLANES = 8
SC_WINDOW = 128
SC_SPLIT = 2
SC_PIECE = 256

POS_SPLIT = 16
POS_ROWS = 16
ONES_ROWS = 16


def _pos_features(rows):
    r = np.arange(rows)
    f = np.zeros((rows, 128), np.float32)
    f[:, 0:3] = (r // POS_SPLIT)[:, None]
    f[:, 3:6] = (r % POS_SPLIT)[:, None]
    return jnp.asarray(f, BF16)


def _slope_rows(slopes):
    s = np.asarray(slopes, np.float32)
    bf = lambda x: x.astype(BF16).astype(np.float32)
    p1 = bf(s)
    p2 = bf(s - p1)
    p3 = bf(s - p1 - p2)
    out = np.zeros((s.shape[0], POS_ROWS, 128), np.float32)
    for i, piece in enumerate((p1, p2, p3)):
        out[:, i, :] = POS_SPLIT * piece[:, None]
        out[:, 3 + i, :] = piece[:, None]
    return jnp.asarray(out)


def _with_ones_row(vT):
    shape = vT.shape[:-2] + (ONES_ROWS, vT.shape[-1])
    extra = jnp.zeros(shape, vT.dtype).at[..., 0, :].set(1.0)
    return jnp.concatenate([vT, extra], axis=-2)


def _cparams(sem, vmem=VMEM_LIMIT_V7X):
    return pltpu.CompilerParams(dimension_semantics=sem, vmem_limit_bytes=vmem)


def _ada_kernel(c_ref, w_ref, b_ref, o_ref):
    c = c_ref[...]
    ca = c * jax.nn.sigmoid(c)
    o_ref[...] = jnp.dot(ca, w_ref[...], preferred_element_type=F32, precision=HIGHEST) + b_ref[...]


def _ada(c, w, b):
    B, D = c.shape
    n_out = w.shape[1]
    rows = 8
    cp = jnp.zeros((rows, D), F32).at[:B].set(c)
    tn = 1024
    out = pl.pallas_call(
        _ada_kernel,
        grid=(n_out // tn,),
        in_specs=[pl.BlockSpec((rows, D), lambda j: (0, 0)),
                  pl.BlockSpec((D, tn), lambda j: (0, j)),
                  pl.BlockSpec((1, tn), lambda j: (0, j))],
        out_specs=pl.BlockSpec((rows, tn), lambda j: (0, j)),
        out_shape=jax.ShapeDtypeStruct((rows, n_out), F32),
        compiler_params=_cparams(("arbitrary",)),
        name="ada",
    )(cp, w, b.reshape(1, n_out))
    return out[:B]


def _modulate(x, nw, sh, sc):
    r = lax.rsqrt(jnp.mean(x * x, axis=-1, keepdims=True) + NORM_EPS)
    return (x * r) * nw * (1.0 + sc) + sh


def _inproj_kernel(x_ref, nw_ref, sh_ref, sc_ref, w_ref, o_ref, h_ref):
    @pl.when(pl.program_id(1) == 0)
    def _():
        h_ref[...] = _modulate(x_ref[...], nw_ref[...], sh_ref[...], sc_ref[...]).astype(BF16)

    o_ref[...] = jnp.dot(h_ref[...], w_ref[...], preferred_element_type=F32).astype(BF16)


def _inproj(x2, nw, sh, sc, w_slab, S):
    N, D = x2.shape
    tm = 512
    tn = SLAB_COLS // 3
    per_b = S // tm
    return pl.pallas_call(
        _inproj_kernel,
        grid=(N // tm, SLAB_COLS // tn),
        in_specs=[pl.BlockSpec((tm, D), lambda i, j: (i, 0)),
                  pl.BlockSpec((1, D), lambda i, j: (0, 0)),
                  pl.BlockSpec((None, 1, D), lambda i, j: (i // per_b, 0, 0)),
                  pl.BlockSpec((None, 1, D), lambda i, j: (i // per_b, 0, 0)),
                  pl.BlockSpec((D, tn), lambda i, j: (0, j))],
        out_specs=pl.BlockSpec((tm, tn), lambda i, j: (i, j)),
        out_shape=jax.ShapeDtypeStruct((N, SLAB_COLS), BF16),
        scratch_shapes=[pltpu.VMEM((tm, D), BF16)],
        compiler_params=_cparams(("parallel", "arbitrary")),
        name="inproj",
    )(x2, nw.reshape(1, D), sh.reshape(-1, 1, D), sc.reshape(-1, 1, D), w_slab)


def _cmp_kernel(r_ref, pe_ref, w1a_ref, w1b_ref, w2_ref, o_ref):
    r = r_ref[...]
    n_rows = r.shape[0]
    a = jnp.dot(r, w1a_ref[...], preferred_element_type=F32)
    b = jnp.dot(r, w1b_ref[...], preferred_element_type=F32)
    pe = pe_ref[...]
    half = pe.shape[1] // 2
    bias = (jnp.dot(pe[:, :half], w1a_ref[...].astype(F32), preferred_element_type=F32, precision=HIGHEST)
            + jnp.dot(pe[:, half:], w1b_ref[...].astype(F32), preferred_element_type=F32, precision=HIGHEST))
    hid = a + pltpu.roll(b, n_rows - 1, 0) + bias
    act = 0.5 * hid * (1.0 + jnp.tanh(0.7978845608028654 * (hid + 0.044715 * hid * hid * hid)))
    o_ref[...] = jnp.dot(act, w2_ref[...], preferred_element_type=F32, precision=HIGHEST)


def _compress(rows, pe, w1, w2):
    BG, n_rows, width = rows.shape
    hidden = w1.shape[1]
    w1a = w1[:width].astype(BF16)
    w1b = w1[width:].astype(BF16)
    return pl.pallas_call(
        _cmp_kernel,
        grid=(BG,),
        in_specs=[pl.BlockSpec((None, n_rows, width), lambda i: (i, 0, 0)),
                  pl.BlockSpec((1, 2 * width), lambda i: (0, 0)),
                  pl.BlockSpec((width, hidden), lambda i: (0, 0)),
                  pl.BlockSpec((width, hidden), lambda i: (0, 0)),
                  pl.BlockSpec((hidden, NSA_DIM), lambda i: (0, 0))],
        out_specs=pl.BlockSpec((None, n_rows, NSA_DIM), lambda i: (i, 0, 0)),
        out_shape=jax.ShapeDtypeStruct((BG, n_rows, NSA_DIM), F32),
        compiler_params=_cparams(("parallel",)),
        name="cmp",
    )(rows, pe.reshape(1, -1).astype(F32), w1a, w1b, w2.astype(F32))


def _da_kernel(slopes_ref, kmin_ref, qT_ref, k_ref, vT_ref, pos_ref, srow_ref, lq1_ref, lk1_ref, lq2_ref, lk2_ref, subw_ref,
               oT_ref, acc_ref, qa_ref, sa_ref, sb_ref, pa_ref, pb_ref, rel_ref, *, tq, tk, lam_init):
    h = pl.program_id(1)
    qi = pl.program_id(2)
    slope = slopes_ref[h]
    q0 = qi * tq
    w = 2 * tq
    nk = k_ref.shape[0] // tk
    qT = qT_ref[...]
    row = lax.broadcasted_iota(I32, qT.shape, 0)
    zero = jnp.zeros_like(qT)
    qa_ref[0:2 * DA_DIM, :] = jnp.concatenate(
        [jnp.where(row < DA_DIM, qT, zero), jnp.where(row >= DA_DIM, qT, zero)], axis=1)
    qa_ref[2 * DA_DIM:2 * DA_DIM + POS_ROWS, :] = jnp.concatenate([srow_ref[...]] * (w // 128), axis=1).astype(BF16)
    qa_ref[2 * DA_DIM + POS_ROWS:, :] = jnp.zeros((2 * DA_DIM - POS_ROWS, w), BF16)
    lane = lax.broadcasted_iota(I32, (tk, w), 1)
    rel_ref[...] = lax.broadcasted_iota(I32, (tk, w), 0) - jnp.where(lane >= tq, lane - tq, lane)

    def scores(kt, s_ref):
        k0 = kt * tk
        k_t = k_ref[pl.ds(pl.multiple_of(k0, tk), tk), :]
        s = jnp.dot(jnp.concatenate([k_t, pos_ref[...]], axis=1), qa_ref[...], preferred_element_type=F32)
        s = jnp.where(rel_ref[...] <= q0 - k0, s, NEG)
        s_ref[...] = s
        return jnp.max(s, axis=0, keepdims=True)

    def probs(s_ref, p_ref, mx, off, m_old):
        m_new = jnp.maximum(m_old, mx + off)
        p_ref[...] = jnp.exp(s_ref[...] - (m_new - off)).astype(BF16)
        return m_new, jnp.exp(m_old - m_new)

    def accumulate(kt, p_ref, alpha):
        acc_ref[...] = alpha * acc_ref[...] + jnp.dot(vT_ref[kt], p_ref[...], preferred_element_type=F32)

    n_tiles = (q0 + tq - 1) // tk + 1
    kt_min = kmin_ref[(pl.program_id(0) * pl.num_programs(1) + h) * pl.num_programs(2) + qi]
    acc_ref[...] = jnp.zeros_like(acc_ref)
    pb_ref[...] = jnp.zeros_like(pb_ref)
    mx_a = scores(kt_min, sa_ref)

    def pair(j, carry):
        m, mx_a, alpha_b = carry
        ta = kt_min + 2 * j
        tb = ta + 1
        off_a = slope * (ta * tk).astype(F32)
        off_b = jnp.where(tb < n_tiles, slope * (tb * tk).astype(F32), NEG)
        accumulate(jnp.maximum(ta - 1, 0), pb_ref, alpha_b)
        m, alpha_a = probs(sa_ref, pa_ref, mx_a, off_a, m)
        mx_b = scores(jnp.minimum(tb, nk - 1), sb_ref)
        accumulate(ta, pa_ref, alpha_a)
        m, alpha_b = probs(sb_ref, pb_ref, mx_b, off_b, m)
        mx_a = scores(jnp.minimum(ta + 2, nk - 1), sa_ref)
        return m, mx_a, alpha_b

    n_pairs = (n_tiles - kt_min + 1) // 2
    init = (jnp.full((1, w), NEG, F32), mx_a, jnp.ones((1, w), F32))
    _, _, alpha_b = lax.fori_loop(0, n_pairs, pair, init)
    accumulate(jnp.minimum(kt_min + 2 * n_pairs - 1, nk - 1), pb_ref, alpha_b)

    lam = (jnp.exp(jnp.sum(lq1_ref[...] * lk1_ref[...], axis=-1, keepdims=True))
           - jnp.exp(jnp.sum(lq2_ref[...] * lk2_ref[...], axis=-1, keepdims=True)) + lam_init)
    on = acc_ref[0:DA_VDIM, :] * (1.0 / acc_ref[DA_VDIM:DA_VDIM + 1, :])
    o = on[:, :tq] - lam * on[:, tq:]
    r = lax.rsqrt(jnp.mean(o * o, axis=0, keepdims=True) + DA_SUBLN_EPS)
    oT_ref[...] = (o * r * subw_ref[...] * (1.0 - lam_init)).astype(BF16)


EXP_UNDERFLOW = 104.0


def _first_live_tile(slopes_np, slab3, tq, tk):
    B, S, _ = slab3.shape
    H = DA_HEADS
    width = 2 * DA_DIM * H
    grp = jnp.asarray((np.arange(width)[:, None] // DA_DIM == np.arange(2 * H)[None, :]).astype(np.float32))

    def norms(c0):
        v = slab3[:, :, c0:c0 + width].astype(F32)
        return jnp.sqrt(jnp.einsum('bsc,cg->bsg', v * v, grp, precision=HIGHEST))

    qn = norms(COL_DAQ) * DA_DIM ** -0.5
    qn = jnp.max(qn.reshape(B, S // tq, tq, H, 2), axis=2).transpose(0, 2, 3, 1)
    kn = jnp.max(norms(COL_DAK), axis=1).reshape(B, H, 2)
    bound = jnp.max(qn * kn[..., None], axis=2) * 1.01 + 0.01
    q0 = (jnp.arange(S // tq, dtype=F32) * tq)[None, None, :]
    slope = jnp.asarray(slopes_np, F32)[None, :, None]
    first_key = q0 - (EXP_UNDERFLOW + 2.0 * bound) / slope
    kt = jnp.floor(first_key / tk).astype(I32)
    n_tiles = (jnp.arange(S // tq, dtype=I32) * tq + tq - 1) // tk + 1
    return jnp.clip(kt, 0, n_tiles[None, None, :] - 1).reshape(-1)


def _diff_attention(slopes, kmin, qT, slab3, vT, srows, lq1, lk1, lq2, lk2, subw, lam_init, tq, tk):
    B, _, S = qT.shape
    nk = S // tk
    assert tk % tq == 0 and S % tk == 0
    tile = tq
    kcol = COL_DAK // 128
    v_rows = vT.shape[3]
    vec = lambda a: a.reshape(1, DA_DIM).astype(F32)
    grid_spec = pltpu.PrefetchScalarGridSpec(
        num_scalar_prefetch=2,
        grid=(B, DA_HEADS, S // tile),
        in_specs=[pl.BlockSpec((None, 128, tile), lambda b, h, i, s, km: (b, h, i)),
                  pl.BlockSpec((None, S, 128), lambda b, h, i, s, km: (b, 0, kcol + h)),
                  pl.BlockSpec((None, None, nk, v_rows, tk), lambda b, h, i, s, km: (b, h, 0, 0, 0)),
                  pl.BlockSpec((tk, 128), lambda b, h, i, s, km: (0, 0)),
                  pl.BlockSpec((None, POS_ROWS, 128), lambda b, h, i, s, km: (h, 0, 0)),
                  pl.BlockSpec((1, DA_DIM), lambda b, h, i, s, km: (0, 0)),
                  pl.BlockSpec((1, DA_DIM), lambda b, h, i, s, km: (0, 0)),
                  pl.BlockSpec((1, DA_DIM), lambda b, h, i, s, km: (0, 0)),
                  pl.BlockSpec((1, DA_DIM), lambda b, h, i, s, km: (0, 0)),
                  pl.BlockSpec((DA_VDIM, 1), lambda b, h, i, s, km: (0, 0))],
        out_specs=pl.BlockSpec((None, 128, tile), lambda b, h, i, s, km: (b, h, i)),
        scratch_shapes=[pltpu.VMEM((v_rows, 2 * tq), F32), pltpu.VMEM((4 * DA_DIM, 2 * tq), BF16),
                        pltpu.VMEM((tk, 2 * tq), F32), pltpu.VMEM((tk, 2 * tq), F32),
                        pltpu.VMEM((tk, 2 * tq), BF16), pltpu.VMEM((tk, 2 * tq), BF16),
                        pltpu.VMEM((tk, 2 * tq), I32)],
    )
    return pl.pallas_call(
        functools.partial(_da_kernel, tq=tq, tk=tk, lam_init=lam_init),
        grid_spec=grid_spec,
        out_shape=jax.ShapeDtypeStruct((B, DA_HEADS * DA_VDIM, S), BF16),
        compiler_params=_cparams(("parallel", "parallel", "arbitrary")),
        name="diffattn",
    )(slopes, kmin, qT, slab3, vT, _pos_features(tk), srows, vec(lq1), vec(lk1), vec(lq2), vec(lk2),
      subw.reshape(DA_VDIM, 1).astype(F32))


def _nsa_kernel(slopes_ref, qT_ref, kc_ref, vcT_ref, ksa_ref, vsT_ref, kwa_ref, vwT_ref, wb_ref, gT_ref, mT_ref,
                grp_ref, ltri_ref, oT_ref, acc_ref, res_ref, base_ref, qa_ref, sa_ref, sb_ref, pa_ref, pb_ref,
                list_ref, slist_ref, lsem, *, t, tk, n_top):
    g = pl.program_id(1)
    qi = pl.program_id(2)
    q0 = qi * t
    w = NSA_HPG * t
    n_cp = kc_ref.shape[0]
    n_sel = mT_ref.shape[0]
    lane1 = lax.broadcasted_iota(I32, (1, w), 1)
    slope_row = jnp.zeros((1, w), F32)
    for hh in range(NSA_HPG):
        slope_row = jnp.where((lane1 >= hh * t) & (lane1 < (hh + 1) * t), slopes_ref[g * NSA_HPG + hh], slope_row)
    q_all = qT_ref[...]
    gates = jax.nn.sigmoid(gT_ref[...])
    key_off = lax.broadcasted_iota(I32, (tk, w), 0)
    t_loc = jnp.bitwise_and(lax.broadcasted_iota(I32, (tk, w), 1), t - 1)
    base_ref[...] = slope_row * key_off.astype(F32)

    c_idx = lax.broadcasted_iota(I32, (n_cp, w), 0)
    t_c = q0 + jnp.bitwise_and(lax.broadcasted_iota(I32, (n_cp, w), 1), t - 1)
    d_c = t_c - (c_idx * CMP_STRIDE + (CMP_BLOCK - 1))
    ok_c = d_c >= 0
    s = jnp.dot(kc_ref[...], q_all, preferred_element_type=F32) - slope_row * d_c.astype(F32)
    s = jnp.where(ok_c, s, NEG)
    e = jnp.exp(s - jnp.max(s, axis=0, keepdims=True))
    inv = 1.0 / jnp.sum(e, axis=0, keepdims=True)
    p = jnp.where(ok_c, e * inv, 0.0)
    res_ref[...] = gates[0:1, :] * jnp.dot(vcT_ref[...], p.astype(BF16), preferred_element_type=F32)
    p_sum = p[:, 0:t]
    for hh in range(1, NSA_HPG):
        p_sum = p_sum + p[:, hh * t:(hh + 1) * t]

    p_hi = p_sum.astype(BF16)
    p_lo = (p_sum - p_hi.astype(F32)).astype(BF16)
    mT = mT_ref[...]
    imp = jnp.dot(mT, p_hi, preferred_element_type=F32) + jnp.dot(mT, p_lo, preferred_element_type=F32)
    blk = lax.broadcasted_iota(I32, (n_sel, t), 0)
    t_s = q0 + lax.broadcasted_iota(I32, (n_sel, t), 1)
    cur = jnp.right_shift(t_s, int(math.log2(SEL_BLOCK)))
    forced = (blk == 0) | (blk == cur) | (blk == cur - 1)
    score = jnp.where(blk * SEL_BLOCK <= t_s, imp + jnp.where(forced, FORCE_BONUS, 0.0), NEG)
    blk_f = blk.astype(F32)
    sel = jnp.zeros((n_sel, t), F32)
    for _ in range(n_top):
        _, idx = _first_max(score, blk_f)
        hit = blk_f == idx
        sel = jnp.where(hit, 1.0, sel)
        score = jnp.where(hit, -jnp.inf, score)
    n_full = q0 // tk
    nt_pad = grp_ref.shape[0]
    cnt = jnp.broadcast_to(jnp.sum(sel, axis=1, keepdims=True), (n_sel, 128)).astype(BF16)
    tile_cnt = jnp.dot(grp_ref[...], cnt, preferred_element_type=F32)
    kt_i = lax.broadcasted_iota(I32, (nt_pad, 128), 0)
    active = (tile_cnt > 0.5) & (kt_i < n_full)
    act_f = jnp.where(active, 1.0, 0.0)
    before = jnp.dot(ltri_ref[...], act_f.astype(BF16), preferred_element_type=F32)
    lane_j = lax.broadcasted_iota(I32, (nt_pad, 128), 1)
    slot_hit = active & (before == lane_j.astype(F32))
    tiles_row = jnp.sum(jnp.where(slot_hit, kt_i.astype(F32), 0.0), axis=0, keepdims=True)
    n_act_row = jnp.sum(act_f, axis=0, keepdims=True)
    lane8 = lax.broadcasted_iota(I32, (8, 128), 1)
    list_ref[...] = jnp.where(lane8 == 127, n_act_row, tiles_row).astype(I32)
    list_copy = pltpu.make_async_copy(list_ref, slist_ref, lsem)
    list_copy.start()

    selb = jnp.where(sel > 0.5, 0.0, NEG)
    row64 = lax.broadcasted_iota(I32, (NSA_DIM, w), 0)
    qa_ref[0:NSA_DIM, :] = q_all
    qa_ref[NSA_DIM:2 * NSA_DIM, :] = jnp.where(row64 == 0, NEG, 0.0).astype(BF16)
    qa_ref[2 * NSA_DIM:2 * NSA_DIM + n_sel, :] = jnp.concatenate([selb] * NSA_HPG, axis=1).astype(BF16)
    if n_sel < SEL_COLS:
        qa_ref[2 * NSA_DIM + n_sel:, :] = jnp.zeros((SEL_COLS - n_sel, w), BF16)

    n_wt = (WINDOW + t) // t
    kw_t = kwa_ref[pl.ds(pl.multiple_of(q0, t), WINDOW + t), :]
    s = jnp.dot(kw_t, qa_ref[0:2 * NSA_DIM, :], preferred_element_type=F32) + wb_ref[...]
    p = jnp.exp(s - jnp.max(s, axis=0, keepdims=True))
    inv = 1.0 / jnp.sum(p, axis=0, keepdims=True)
    p = p.astype(BF16)
    o_w = jnp.dot(vwT_ref[qi], p[0:t, :], preferred_element_type=F32)
    for c in range(1, n_wt):
        o_w = o_w + jnp.dot(vwT_ref[qi + c], p[c * t:(c + 1) * t, :], preferred_element_type=F32)
    res_ref[...] = res_ref[...] + gates[2:3, :] * (o_w * inv)

    def scores(kt, s_ref):
        k_t = ksa_ref[pl.ds(pl.multiple_of(kt * tk, tk), tk), :]
        s = jnp.dot(k_t, qa_ref[...], preferred_element_type=F32) + base_ref[...]
        s_ref[...] = s
        return jnp.max(s, axis=0, keepdims=True)

    def probs(s_ref, p_ref, mx, off, m_old, l_old):
        m_new = jnp.maximum(m_old, mx + off)
        p = jnp.exp(s_ref[...] - (m_new - off))
        alpha = jnp.exp(m_old - m_new)
        p_ref[...] = p.astype(BF16)
        return m_new, alpha * l_old + jnp.sum(p, axis=0, keepdims=True), alpha

    def accumulate(kt, p_ref, alpha):
        acc_ref[...] = alpha * acc_ref[...] + jnp.dot(vsT_ref[kt], p_ref[...], preferred_element_type=F32)

    list_copy.wait()
    n_act = slist_ref[0, 127]
    mx_a = scores(slist_ref[0, 0], sa_ref)

    kd = n_full * tk
    s = jnp.dot(ksa_ref[pl.ds(pl.multiple_of(kd, tk), tk), :], qa_ref[...], preferred_element_type=F32) + base_ref[...]
    s = jnp.where(key_off - t_loc <= q0 - kd, s, NEG)
    off_d = slope_row * kd.astype(F32)
    m = jnp.max(s, axis=0, keepdims=True) + off_d
    p = jnp.exp(s - (m - off_d))
    l = jnp.sum(p, axis=0, keepdims=True)
    acc_ref[...] = jnp.dot(vsT_ref[n_full], p.astype(BF16), preferred_element_type=F32)
    pb_ref[...] = jnp.zeros_like(pb_ref)

    def pair(j, carry):
        m, l, mx_a, alpha_b, tb_prev = carry
        ta = slist_ref[0, 2 * j]
        tb = slist_ref[0, 2 * j + 1]
        off_a = slope_row * (ta * tk).astype(F32)
        off_b = jnp.where(2 * j + 1 < n_act, slope_row * (tb * tk).astype(F32), NEG)
        accumulate(tb_prev, pb_ref, alpha_b)
        m, l, alpha_a = probs(sa_ref, pa_ref, mx_a, off_a, m, l)
        mx_b = scores(tb, sb_ref)
        accumulate(ta, pa_ref, alpha_a)
        m, l, alpha_b = probs(sb_ref, pb_ref, mx_b, off_b, m, l)
        mx_a = scores(slist_ref[0, 2 * j + 2], sa_ref)
        return m, l, mx_a, alpha_b, tb

    n_pairs = (n_act + 1) // 2
    m, l, _, alpha_b, tb_last = lax.fori_loop(0, n_pairs, pair, (m, l, mx_a, jnp.ones((1, w), F32), jnp.int32(0)))
    accumulate(tb_last, pb_ref, alpha_b)
    oT_ref[...] = (res_ref[...] + gates[1:2, :] * (acc_ref[...] * (1.0 / l))).astype(BF16)


def _nsa_attention(slopes, qT, kc, vcT, ksa, vsT, kwa, vwT, wb, gT, mT, t, tk):
    B, G, nq, _, w = qT.shape
    S = nq * t
    n_cp = kc.shape[2]
    n_sel = S // SEL_BLOCK
    n_top = min(SEL_TOPK, n_sel)
    assert tk % t == 0 and S % tk == 0 and t & (t - 1) == 0 and WINDOW % t == 0 and n_sel <= SEL_COLS
    n_tiles = S // tk
    nt_pad = -(-n_tiles // 16) * 16
    assert nt_pad + 2 < 127
    grp = jnp.asarray((np.arange(n_sel)[None, :] // (tk // SEL_BLOCK) == np.arange(nt_pad)[:, None])
                      .astype(np.float32), BF16)
    ltri = jnp.asarray((np.arange(nt_pad)[None, :] < np.arange(nt_pad)[:, None]).astype(np.float32), BF16)
    fixed = lambda b, g, i, s: (b, g, 0, 0)
    fixed5 = lambda b, g, i, s: (b, g, 0, 0, 0)
    tile5 = lambda b, g, i, s: (b, g, i, 0, 0)
    grid_spec = pltpu.PrefetchScalarGridSpec(
        num_scalar_prefetch=1,
        grid=(B, G, nq),
        in_specs=[pl.BlockSpec((None, None, None, NSA_DIM, w), tile5),
                  pl.BlockSpec((None, None, n_cp, NSA_DIM), fixed),
                  pl.BlockSpec((None, None, NSA_DIM, n_cp), fixed),
                  pl.BlockSpec((None, None, S, ksa.shape[-1]), fixed),
                  pl.BlockSpec((None, None, S // tk, NSA_DIM, tk), fixed5),
                  pl.BlockSpec((None, None, S + WINDOW, 2 * NSA_DIM), fixed),
                  pl.BlockSpec((None, None, (S + WINDOW) // t, NSA_DIM, t), fixed5),
                  pl.BlockSpec((None, WINDOW + t, w), lambda b, g, i, s: (g, 0, 0)),
                  pl.BlockSpec((None, None, None, 3, w), tile5),
                  pl.BlockSpec((n_sel, n_cp), lambda b, g, i, s: (0, 0)),
                  pl.BlockSpec((nt_pad, n_sel), lambda b, g, i, s: (0, 0)),
                  pl.BlockSpec((nt_pad, nt_pad), lambda b, g, i, s: (0, 0))],
        out_specs=pl.BlockSpec((None, None, None, NSA_DIM, w), tile5),
        scratch_shapes=[pltpu.VMEM((NSA_DIM, w), F32),
                        pltpu.VMEM((NSA_DIM, w), F32),
                        pltpu.VMEM((tk, w), F32),
                        pltpu.VMEM((2 * NSA_DIM + SEL_COLS, w), BF16),
                        pltpu.VMEM((tk, w), F32), pltpu.VMEM((tk, w), F32),
                        pltpu.VMEM((tk, w), BF16), pltpu.VMEM((tk, w), BF16),
                        pltpu.VMEM((8, 128), I32), pltpu.SMEM((8, 128), I32), pltpu.SemaphoreType.DMA(())],
    )
    return pl.pallas_call(
        functools.partial(_nsa_kernel, t=t, tk=tk, n_top=n_top),
        grid_spec=grid_spec,
        out_shape=jax.ShapeDtypeStruct((B, G, nq, NSA_DIM, w), BF16),
        compiler_params=_cparams(("parallel", "parallel", "arbitrary")),
        name="nsa",
    )(slopes, qT, kc, vcT, ksa, vsT, kwa, vwT, wb, gT, mT, grp, ltri)


def _pack_halves(v):
    c = v.shape[1] // 2
    lo = lax.bitcast_convert_type(v[:, :c].astype(BF16).astype(F32), U32)
    hi = lax.bitcast_convert_type(v[:, c:].astype(BF16).astype(F32), U32)
    return hi | (lo >> 16)


def _unpack_halves(u):
    lo = lax.bitcast_convert_type(u << 16, F32)
    hi = lax.bitcast_convert_type(u & jnp.uint32(0xFFFF0000), F32)
    return lo, hi


def _store_slabs(ref, v):
    packed = _pack_halves(v)
    for s in range(SC_SPLIT):
        ref[s] = packed[:, s * SC_PIECE:(s + 1) * SC_PIECE]


def _load_slabs_bf16(ref):
    parts = [_unpack_halves(ref[s]) for s in range(SC_SPLIT)]
    return jnp.concatenate([p[0].astype(BF16) for p in parts] + [p[1].astype(BF16) for p in parts], axis=1)


def _merge_kernel(oa_ref, ob_ref, ga_ref, gb_ref, x_ref, g1_ref, nw_ref, sh_ref, sc_ref,
                  wa_ref, wb_ref, wo_ref, wr_ref, x1_ref, h2_ref, lg_ref):
    ya = jnp.dot(oa_ref[...], wa_ref[...], preferred_element_type=F32)
    yb = jnp.dot(ob_ref[...], wb_ref[...], preferred_element_type=F32)
    merged = (jax.nn.sigmoid(ga_ref[...].astype(F32)) * ya
              + jax.nn.sigmoid(gb_ref[...].astype(F32)) * yb).astype(BF16)
    mix = jnp.dot(merged, wo_ref[...], preferred_element_type=F32)
    x1 = x_ref[...] + g1_ref[...] * mix
    x1_ref[...] = x1
    h2 = _modulate(x1, nw_ref[...], sh_ref[...], sc_ref[...])
    _store_slabs(h2_ref, h2)
    lg_ref[...] = jnp.dot(h2, wr_ref[...], preferred_element_type=F32, precision=HIGHEST)


def _merge(oa, ob, slab, x2, g1, nw, sh, sc, wa, wb, wo, wr, S):
    N, D = x2.shape
    tm = 256
    per_b = S // tm
    mcol = COL_MERGE // D
    row = lambda i: (i, 0)
    full = lambda i: (0, 0)
    perb = lambda i: (i // per_b, 0, 0)
    n_lg = wr.shape[1]
    return pl.pallas_call(
        _merge_kernel,
        grid=(N // tm,),
        in_specs=[pl.BlockSpec((tm, oa.shape[1]), row),
                  pl.BlockSpec((tm, ob.shape[1]), row),
                  pl.BlockSpec((tm, D), lambda i: (i, mcol)),
                  pl.BlockSpec((tm, D), lambda i: (i, mcol + 1)),
                  pl.BlockSpec((tm, D), row),
                  pl.BlockSpec((None, 1, D), perb),
                  pl.BlockSpec((1, D), full),
                  pl.BlockSpec((None, 1, D), perb),
                  pl.BlockSpec((None, 1, D), perb),
                  pl.BlockSpec(wa.shape, full),
                  pl.BlockSpec(wb.shape, full),
                  pl.BlockSpec(wo.shape, full),
                  pl.BlockSpec(wr.shape, full)],
        out_specs=[pl.BlockSpec((tm, D), row), pl.BlockSpec((SC_SPLIT, tm, SC_PIECE), lambda i: (0, i, 0)),
                   pl.BlockSpec((tm, n_lg), row)],
        out_shape=[jax.ShapeDtypeStruct((N, D), F32), jax.ShapeDtypeStruct((SC_SPLIT, N, SC_PIECE), U32),
                   jax.ShapeDtypeStruct((N, n_lg), F32)],
        compiler_params=_cparams(("parallel",)),
        name="merge",
    )(oa, ob, slab, slab, x2, g1.reshape(-1, 1, D), nw.reshape(1, D), sh.reshape(-1, 1, D),
      sc.reshape(-1, 1, D), wa, wb, wo, wr)


def _first_max(v, idx):
    mx = jnp.max(v, axis=0, keepdims=True)
    first = jnp.min(jnp.where(v == mx, idx, float(v.shape[0])), axis=0, keepdims=True)
    return mx, first


def _route_kernel(lg_ref, b_ref, e_ref, w_ref):
    lg = lg_ref[...]
    tn = lg.shape[1]
    per_g = N_EXPERTS // N_EXPERT_GROUPS
    scores = jax.nn.sigmoid(lg)
    biased = scores + b_ref[...]
    gi = lax.broadcasted_iota(I32, (N_EXPERT_GROUPS, tn), 0).astype(F32)
    gscore = jnp.zeros((N_EXPERT_GROUPS, tn), F32)
    for g in range(N_EXPERT_GROUPS):
        grp = biased[g * per_g:(g + 1) * per_g, :]
        m1, f1 = _first_max(grp, gi)
        m2 = jnp.max(jnp.where(gi == f1, -jnp.inf, grp), axis=0, keepdims=True)
        gscore = jnp.where(gi == g, m1 + m2, gscore)
    gsel = jnp.zeros((N_EXPERT_GROUPS, tn), F32)
    for _ in range(TOPK_EXPERT_GROUPS):
        _, f = _first_max(gscore, gi)
        hit = gi == f
        gsel = jnp.where(hit, 1.0, gsel)
        gscore = jnp.where(hit, -jnp.inf, gscore)
    ei = lax.broadcasted_iota(I32, (N_EXPERTS, tn), 0).astype(F32)
    emask = jnp.zeros((N_EXPERTS, tn), F32)
    for g in range(N_EXPERT_GROUPS):
        in_g = (ei >= g * per_g) & (ei < (g + 1) * per_g)
        emask = jnp.where(in_g, gsel[g:g + 1, :], emask)
    masked = jnp.where(emask > 0.5, biased, NEG)
    e_out = jnp.zeros((MOE_TOPK, tn), F32)
    w_out = jnp.zeros((MOE_TOPK, tn), F32)
    for r in range(MOE_TOPK):
        _, f = _first_max(masked, ei)
        hit = ei == f
        wv = jnp.sum(jnp.where(hit, scores, 0.0), axis=0, keepdims=True)
        e_out = jnp.where(gi == r, f, e_out)
        w_out = jnp.where(gi == r, wv, w_out)
        masked = jnp.where(hit, -jnp.inf, masked)
    e_ref[...] = e_out.astype(I32)
    w_ref[...] = w_out / jnp.sum(w_out, axis=0, keepdims=True) * ROUTED_SCALE


def _route(lgT, rb):
    E, N = lgT.shape
    tn = 512
    return pl.pallas_call(
        _route_kernel,
        grid=(N // tn,),
        in_specs=[pl.BlockSpec((E, tn), lambda i: (0, i)), pl.BlockSpec((E, 1), lambda i: (0, 0))],
        out_specs=[pl.BlockSpec((MOE_TOPK, tn), lambda i: (0, i)), pl.BlockSpec((MOE_TOPK, tn), lambda i: (0, i))],
        out_shape=[jax.ShapeDtypeStruct((MOE_TOPK, N), I32), jax.ShapeDtypeStruct((MOE_TOPK, N), F32)],
        compiler_params=_cparams(("parallel",)),
        name="route",
    )(lgT, rb.reshape(E, 1).astype(F32))


def _slots_kernel(e_ref, tri_ref, lt_ref, dest_ref, cnt_ref, carry_ref, *, rb):
    phase = pl.program_id(0)
    i = pl.program_id(1)
    e = e_ref[...]
    tn = e.shape[1]
    ei = lax.broadcasted_iota(I32, (N_EXPERTS, tn), 0)

    @pl.when((phase == 0) & (i == 0))
    def _():
        carry_ref[...] = jnp.zeros_like(carry_ref)

    @pl.when(phase == 0)
    def _():
        tot = jnp.zeros((N_EXPERTS, 1), F32)
        for k in range(MOE_TOPK):
            oh = jnp.where(ei == e[k:k + 1, :], 1.0, 0.0)
            tot = tot + jnp.sum(oh, axis=1, keepdims=True)
        carry_ref[...] = carry_ref[...] + tot
        dest_ref[...] = jnp.zeros_like(dest_ref)

    @pl.when((phase == 1) & (i == 0))
    def _():
        cnt = carry_ref[...]
        cnt_ref[...] = cnt
        padded = jnp.broadcast_to(jnp.ceil(cnt / rb) * rb, (N_EXPERTS, 128))
        first = jnp.dot(lt_ref[...], padded, preferred_element_type=F32, precision=HIGHEST)
        carry_ref[...] = first[:, :1]

    @pl.when(phase == 1)
    def _():
        carry = carry_ref[...]
        ki = lax.broadcasted_iota(I32, (MOE_TOPK, tn), 0)
        dest = jnp.zeros((MOE_TOPK, tn), F32)
        for k in range(MOE_TOPK):
            hit = ei == e[k:k + 1, :]
            oh = jnp.where(hit, 1.0, 0.0)
            before = jnp.dot(oh.astype(BF16), tri_ref[...], preferred_element_type=F32)
            row = jnp.sum(jnp.where(hit, before + carry, 0.0), axis=0, keepdims=True)
            dest = jnp.where(ki == k, row, dest)
            carry = carry + jnp.sum(oh, axis=1, keepdims=True)
        carry_ref[...] = carry
        dest_ref[...] = dest.astype(I32)


def _slots(top_eT, rb):
    K, N = top_eT.shape
    tn = 512
    idx = np.arange(tn)
    tri = jnp.asarray((idx[:, None] < idx[None, :]).astype(np.float32), BF16)
    e_idx = np.arange(N_EXPERTS)
    lt = jnp.asarray((e_idx[None, :] < e_idx[:, None]).astype(np.float32), F32)
    return pl.pallas_call(
        functools.partial(_slots_kernel, rb=rb),
        grid=(2, N // tn),
        in_specs=[pl.BlockSpec((K, tn), lambda p, i: (0, i)),
                  pl.BlockSpec((tn, tn), lambda p, i: (0, 0)),
                  pl.BlockSpec((N_EXPERTS, N_EXPERTS), lambda p, i: (0, 0))],
        out_specs=[pl.BlockSpec((K, tn), lambda p, i: (0, i * p)),
                   pl.BlockSpec((N_EXPERTS, 1), lambda p, i: (0, 0))],
        out_shape=[jax.ShapeDtypeStruct((K, N), I32), jax.ShapeDtypeStruct((N_EXPERTS, 1), F32)],
        scratch_shapes=[pltpu.VMEM((N_EXPERTS, 1), F32)],
        compiler_params=_cparams(("arbitrary", "arbitrary")),
        name="slots",
    )(top_eT, tri, lt)


def _sc_scatter_rows(src, idx, n_out, repeat):
    n_idx = idx.shape[0]
    width = src.shape[1]
    w = src.shape[0] // SC_SPLIT // SC_WINDOW
    mesh = plsc.VectorSubcoreMesh(core_axis_name="c", subcore_axis_name="s")

    @pl.kernel(out_type=jax.ShapeDtypeStruct((n_out, width), src.dtype), mesh=mesh)
    def scatter_kernel(x_hbm, i_hbm, o_hbm):
        def body(x_vmem, i_vmem):
            pltpu.sync_copy(x_vmem, o_hbm.at[i_vmem.at[0]])

        pltpu.emit_pipeline(
            body,
            grid=(n_idx // SC_WINDOW,),
            in_specs=[pl.BlockSpec((SC_WINDOW, width), lambda g: ((g // (repeat * w)) * w + g % w, 0)),
                      pl.BlockSpec((1, SC_WINDOW), lambda g: (0, g))],
            out_specs=[],
            core_axis_name=("c", "s"),
            dimension_semantics=(pltpu.PARALLEL,),
        )(x_hbm, i_hbm)

    return scatter_kernel(src, idx.reshape(1, n_idx))


def _zero_pads_kernel(start_ref, len_ref, xs_in_ref, xs_ref, zeros_ref, sem, *, rb, n_rows):
    del xs_in_ref
    zeros_ref[...] = jnp.zeros_like(zeros_ref)
    n_seg = start_ref.shape[0]
    bits = rb.bit_length() - 1
    assert rb == 1 << bits

    def for_each_copy(act):
        def rows(row0, size):
            for s in range(SC_SPLIT):
                act(pltpu.make_async_copy(zeros_ref.at[pl.ds(0, size), :],
                                          xs_ref.at[pl.ds(s * n_rows + row0, size), :], sem))

        def segment(e, c):
            start = start_ref[e]
            length = len_ref[e]
            end = start + length
            n_whole = length // rb

            def whole(j, c2):
                rows(pl.multiple_of(end - (j + 1) * rb, SUBLANES), rb)
                return c2

            lax.fori_loop(0, n_whole, whole, 0)
            off = end - n_whole * rb
            rest = length - n_whole * rb
            for b in reversed(range(SUBLANES.bit_length() - 1, bits)):
                size = 1 << b
                take = jnp.bitwise_and(jnp.right_shift(rest, b), 1)
                off = off - take * size

                @pl.when(take == 1)
                def _(off=off, size=size):
                    rows(pl.multiple_of(off, SUBLANES), size)

            lead = jnp.bitwise_and(rest, SUBLANES - 1)
            for i in range(SUBLANES - 1):
                @pl.when(i < lead)
                def _(i=i):
                    rows(start + i, 1)

            return c

        lax.fori_loop(0, n_seg, segment, 0)

    for_each_copy(lambda cp: cp.start())
    for_each_copy(lambda cp: cp.wait())


def _zero_pads(xs2d, start, length, rb, n_rows):
    grid_spec = pltpu.PrefetchScalarGridSpec(
        num_scalar_prefetch=2,
        grid=(1,),
        in_specs=[pl.BlockSpec(memory_space=pl.ANY)],
        out_specs=pl.BlockSpec(memory_space=pl.ANY),
        scratch_shapes=[pltpu.VMEM((rb, xs2d.shape[1]), xs2d.dtype), pltpu.SemaphoreType.DMA(())],
    )
    return pl.pallas_call(
        functools.partial(_zero_pads_kernel, rb=rb, n_rows=n_rows),
        grid_spec=grid_spec,
        out_shape=jax.ShapeDtypeStruct(xs2d.shape, xs2d.dtype),
        input_output_aliases={2: 0},
        compiler_params=_cparams(("arbitrary",)),
        name="zeropads",
    )(start, length, xs2d)


def _experts_kernel(be_ref, x_ref, wgu_ref, wd_ref, y_ref):
    del be_ref
    x = _load_slabs_bf16(x_ref)
    gu = jnp.dot(x, wgu_ref[...], preferred_element_type=F32)
    hdim = gu.shape[1] // 2
    g = gu[:, :hdim]
    act = (g * jax.nn.sigmoid(g) * gu[:, hdim:]).astype(BF16)
    _store_slabs(y_ref, jnp.dot(act, wd_ref[...], preferred_element_type=F32))


def _experts(blk_e, xs, wgu, wd, rb):
    n_rows = xs.shape[1]
    grid_spec = pltpu.PrefetchScalarGridSpec(
        num_scalar_prefetch=1,
        grid=(n_rows // rb,),
        in_specs=[pl.BlockSpec((SC_SPLIT, rb, SC_PIECE), lambda i, be: (0, i, 0)),
                  pl.BlockSpec((None,) + wgu.shape[1:], lambda i, be: (be[i], 0, 0)),
                  pl.BlockSpec((None,) + wd.shape[1:], lambda i, be: (be[i], 0, 0))],
        out_specs=pl.BlockSpec((SC_SPLIT, rb, SC_PIECE), lambda i, be: (0, i, 0)),
    )
    return pl.pallas_call(
        _experts_kernel,
        grid_spec=grid_spec,
        out_shape=jax.ShapeDtypeStruct((SC_SPLIT, n_rows, SC_PIECE), U32),
        compiler_params=_cparams(("arbitrary",)),
        name="experts",
    )(blk_e, xs, wgu, wd)


def _sc_gather_rows(table, idx):
    n = idx.shape[0]
    width = table.shape[1]
    mesh = plsc.VectorSubcoreMesh(core_axis_name="c", subcore_axis_name="s")

    @pl.kernel(out_type=jax.ShapeDtypeStruct((n, width), table.dtype), mesh=mesh)
    def gather_kernel(t_hbm, i_hbm, o_hbm):
        def body(i_vmem, o_vmem):
            pltpu.sync_copy(t_hbm.at[i_vmem.at[0]], o_vmem)

        pltpu.emit_pipeline(
            body,
            grid=(n // SC_WINDOW,),
            in_specs=[pl.BlockSpec((1, SC_WINDOW), lambda i: (0, i))],
            out_specs=[pl.BlockSpec((SC_WINDOW, width), lambda i: (i, 0))],
            core_axis_name=("c", "s"),
            dimension_semantics=(pltpu.PARALLEL,),
        )(i_hbm, o_hbm)

    return gather_kernel(table, idx.reshape(1, n))


def _combine_kernel(yg_ref, w_ref, h_ref, x1_ref, g2_ref, fw_ref, sgu_ref, sd_ref, o_ref):
    gu = jnp.dot(_load_slabs_bf16(h_ref), sgu_ref[...], preferred_element_type=F32)
    hdim = gu.shape[1] // 2
    g = gu[:, :hdim]
    ffn = jnp.dot((g * jax.nn.sigmoid(g) * gu[:, hdim:]).astype(BF16), sd_ref[...], preferred_element_type=F32)

    w = w_ref[...]
    tt = w.shape[0]
    lo = [jnp.zeros((tt, SC_PIECE), F32) for _ in range(SC_SPLIT)]
    hi = [jnp.zeros((tt, SC_PIECE), F32) for _ in range(SC_SPLIT)]
    for k in range(MOE_TOPK):
        wk = w[:, k:k + 1]
        for s in range(SC_SPLIT):
            y_lo, y_hi = _unpack_halves(yg_ref[s, k])
            lo[s] = lo[s] + wk * y_lo
            hi[s] = hi[s] + wk * y_hi
    ffn = ffn + jnp.concatenate(lo + hi, axis=1)
    x2 = x1_ref[...] + g2_ref[...] * ffn
    r = lax.rsqrt(jnp.mean(x2 * x2, axis=-1, keepdims=True) + NORM_EPS)
    o_ref[...] = x2 * r * fw_ref[...]


def _combine(yg, top_w, h2p, x1, g2, fw, sgu, sd, S, tt):
    N, D = x1.shape
    per_b = S // tt
    row = lambda i: (i, 0)
    full = lambda i: (0, 0)
    return pl.pallas_call(
        _combine_kernel,
        grid=(N // tt,),
        in_specs=[pl.BlockSpec((SC_SPLIT, MOE_TOPK, tt, SC_PIECE), lambda i: (0, 0, i, 0)),
                  pl.BlockSpec((tt, MOE_TOPK), row),
                  pl.BlockSpec((SC_SPLIT, tt, SC_PIECE), lambda i: (0, i, 0)),
                  pl.BlockSpec((tt, D), row),
                  pl.BlockSpec((None, 1, D), lambda i: (i // per_b, 0, 0)),
                  pl.BlockSpec((1, D), full),
                  pl.BlockSpec(sgu.shape, full),
                  pl.BlockSpec(sd.shape, full)],
        out_specs=pl.BlockSpec((tt, D), row),
        out_shape=jax.ShapeDtypeStruct((N, D), F32),
        compiler_params=_cparams(("parallel",)),
        name="combine",
    )(yg, top_w, h2p, x1, g2.reshape(-1, 1, D), fw.reshape(1, D), sgu, sd)


def _cmp_to_sel_T(n_cp, n_sel):
    c0 = np.arange(n_cp)[None, :] * CMP_STRIDE
    s0 = np.arange(n_sel)[:, None] * SEL_BLOCK
    ov = np.minimum(c0 + CMP_BLOCK, s0 + SEL_BLOCK) - np.maximum(c0, s0)
    m = np.clip(ov, 0, None).astype(np.float32) / CMP_BLOCK
    m[:, n_cp - 1] = 0.0
    return jnp.asarray(m, BF16)


def _slab_weights(w_in):
    D = w_in.shape[0]
    sizes = [1024, 1024, 1024, 512, 128, 128, 128, 128, 128, 128, 24, 2048]
    offs = np.concatenate([[0], np.cumsum(sizes)])
    part = lambda i: w_in[:, offs[i]:offs[i + 1]]
    pieces = [part(0), part(1), part(2), part(11), part(3)] + [part(i) for i in range(4, 10)] + [part(10)]
    w = jnp.concatenate(pieces, axis=1)
    return jnp.pad(w, ((0, 0), (0, SLAB_COLS - w.shape[1]))).astype(BF16)


def _layer(x, c, ada_w, ada_b, norm1_w, w_in, da_lq1, da_lk1, da_lq2, da_lk2, da_subln_w,
           cmp_k_pe, cmp_k_w1, cmp_k_w2, cmp_v_pe, cmp_v_w1, cmp_v_w2, w_da_out, w_nsa_out, w_o,
           norm2_w, router_w, router_b, exp_w_gate, exp_w_up, exp_w_down,
           sh_w_gate, sh_w_up, sh_w_down, final_norm_w, lam_init):
    B, S, D = x.shape
    N = B * S
    G, HPG = NSA_GROUPS, NSA_HPG
    x2 = x.reshape(N, D)

    mod = _ada(c, ada_w, ada_b)
    sh1, sc1, g1, sh2, sc2, g2 = jnp.split(mod, 6, axis=-1)

    slab = _inproj(x2, norm1_w, sh1, sc1, _slab_weights(w_in), S)
    slab3 = slab.reshape(B, S, SLAB_COLS)

    i_all = np.arange(1, DA_HEADS + NSA_HEADS + 1, dtype=np.float32)
    slopes = (2.0 ** (-8.0 * i_all / (DA_HEADS + NSA_HEADS))).astype(np.float32)
    slopes_a = jnp.asarray(slopes[0::2])
    slopes_b = jnp.asarray(slopes[1::2])

    ta_q, ta_k = min(512, S), min(512, S)
    qT = (slab3[:, :, COL_DAQ:COL_DAQ + 1024] * jnp.asarray(DA_DIM ** -0.5, BF16)).transpose(0, 2, 1)
    vT = slab3[:, :, COL_DAV:COL_DAV + 1024].reshape(B, S // ta_k, ta_k, DA_HEADS, DA_VDIM).transpose(0, 3, 1, 4, 2)
    kmin = _first_live_tile(slopes[0::2], slab3, ta_q, ta_k)
    oaT = _diff_attention(slopes_a, kmin, qT, slab3, _with_ones_row(vT), _slope_rows(slopes[0::2]), da_lq1, da_lk1,
                          da_lq2, da_lk2, da_subln_w, lam_init, ta_q, ta_k)
    oa = oaT.transpose(0, 2, 1).reshape(N, DA_HEADS * DA_VDIM)

    tb, tb_k = 128, min(256, S)
    n_cp = S // CMP_STRIDE
    n_sel = S // SEL_BLOCK

    def kv_groups(j):
        c0 = COL_KV6 + 128 * j
        return slab3[:, :, c0:c0 + 128].reshape(B, S, G, NSA_DIM).transpose(0, 2, 1, 3)

    def kv_tiles_T(a, rows):
        return a.reshape(B, G, S // rows, rows, NSA_DIM).transpose(0, 1, 2, 4, 3)

    ck_rows = kv_groups(0).reshape(B * G, n_cp, CMP_STRIDE * NSA_DIM)
    cv_rows = kv_groups(1).reshape(B * G, n_cp, CMP_STRIDE * NSA_DIM)
    kc = _compress(ck_rows, cmp_k_pe, cmp_k_w1, cmp_k_w2).reshape(B, G, n_cp, NSA_DIM).astype(BF16)
    vc = _compress(cv_rows, cmp_v_pe, cmp_v_w1, cmp_v_w2).reshape(B, G, n_cp, NSA_DIM).astype(BF16)
    nq = (slab3[:, :, COL_NSAQ:COL_NSAQ + 512] * jnp.asarray(NSA_DIM ** -0.5, BF16))
    nqT = nq.reshape(B, S // tb, tb, G, HPG, NSA_DIM).transpose(0, 3, 1, 5, 4, 2).reshape(B, G, S // tb, NSA_DIM, HPG * tb)
    gT = (slab3[:, :, COL_NSAG:COL_NSAG + 3 * NSA_HEADS].astype(F32).reshape(B, S // tb, tb, G, HPG, 3)
          .transpose(0, 3, 1, 5, 4, 2).reshape(B, G, S // tb, 3, HPG * tb))
    onehot = jnp.asarray((np.arange(S)[:, None] // SEL_BLOCK == np.arange(SEL_COLS)[None, :]).astype(np.float32), BF16)
    zeros64 = jnp.zeros((B, G, S, NSA_DIM), BF16)
    ksa = jnp.concatenate([kv_groups(2), zeros64, jnp.broadcast_to(onehot, (B, G, S, SEL_COLS))], axis=-1)
    pad_rows = jnp.zeros((B, G, WINDOW, 2 * NSA_DIM), BF16).at[..., NSA_DIM].set(1.0)
    kwa = jnp.concatenate([pad_rows, jnp.concatenate([kv_groups(4), zeros64], axis=-1)], axis=2)
    vw_pad = jnp.pad(kv_groups(5), ((0, 0), (0, 0), (WINDOW, 0), (0, 0)))
    vwT = vw_pad.reshape(B, G, (S + WINDOW) // tb, tb, NSA_DIM).transpose(0, 1, 2, 4, 3)
    r_w = np.arange(WINDOW + tb)[:, None]
    lane_w = np.arange(HPG * tb)[None, :]
    d_w = (WINDOW + lane_w % tb - r_w).astype(np.float32)
    slope_w = slopes[1::2].reshape(G, 1, HPG)[:, :, lane_w[0] // tb]
    wb = jnp.asarray(np.where((d_w >= 0) & (d_w < WINDOW), -slope_w * d_w[None], np.float32(NEG)).astype(np.float32))
    obT = _nsa_attention(slopes_b, nqT, kc, vc.transpose(0, 1, 3, 2), ksa, kv_tiles_T(kv_groups(3), tb_k),
                         kwa, vwT, wb, gT, _cmp_to_sel_T(n_cp, n_sel), tb, tb_k)
    ob = (obT.reshape(B, G, S // tb, NSA_DIM, HPG, tb).transpose(0, 2, 5, 1, 4, 3)
          .reshape(N, NSA_HEADS * NSA_DIM))

    x1, h2, logits = _merge(oa, ob, slab, x2, g1, norm2_w, sh2, sc2, w_da_out.astype(BF16),
                            w_nsa_out.astype(BF16), w_o.astype(BF16), router_w.astype(F32), S)

    rb = 512
    tt = 128
    top_eT, top_wT = _route(logits.T, router_b)
    destT, counts = _slots(top_eT, rb)
    n_rows = ((N * MOE_TOPK + N_EXPERTS * (rb - 1) + rb - 1) // rb) * rb
    padded = (jnp.ceil(counts[:, 0] / rb) * rb).astype(I32)
    pend = jnp.cumsum(padded)
    blk_start = jnp.arange(n_rows // rb, dtype=I32) * rb
    blk_e = jnp.minimum(jnp.sum((pend[None, :] <= blk_start[:, None]).astype(I32), axis=1), N_EXPERTS - 1)
    d_flat = destT.reshape(-1)
    piece_idx = jnp.concatenate([d_flat + s * n_rows for s in range(SC_SPLIT)])
    xs2d = _sc_scatter_rows(h2.reshape(SC_SPLIT * N, SC_PIECE), piece_idx, SC_SPLIT * n_rows, MOE_TOPK)
    cnt = counts[:, 0].astype(I32)
    pad_start = jnp.concatenate([pend - padded + cnt, pend[-1:]])
    pad_len = jnp.concatenate([padded - cnt, n_rows - pend[-1:]])
    xs = _zero_pads(xs2d, pad_start, pad_len, rb, n_rows).reshape(SC_SPLIT, n_rows, SC_PIECE)
    wgu = jnp.concatenate([exp_w_gate, exp_w_up], axis=-1).astype(BF16)
    ys = _experts(blk_e, xs, wgu, exp_w_down.astype(BF16), rb)
    sgu = jnp.concatenate([sh_w_gate, sh_w_up], axis=-1).astype(BF16)
    yg = _sc_gather_rows(ys.reshape(SC_SPLIT * n_rows, SC_PIECE), piece_idx).reshape(SC_SPLIT, MOE_TOPK, N, SC_PIECE)
    out = _combine(yg, top_wT.T, h2, x1, g2, final_norm_w, sgu, sh_w_down.astype(BF16), S, 2 * tt)
    return out.reshape(B, S, D)


def kernel(x, c, ada_w, ada_b, norm1_w, w_in, da_lq1, da_lk1, da_lq2, da_lk2, da_subln_w, cmp_k_pe, cmp_k_w1, cmp_k_w2, cmp_v_pe, cmp_v_w1, cmp_v_w2, w_da_out, w_nsa_out, w_o, norm2_w, router_w, router_b, exp_w_gate, exp_w_up, exp_w_down, sh_w_gate, sh_w_up, sh_w_down, final_norm_w):
    depth = ada_w.shape[0]
    assert depth == 1, "one decoder layer"
    lam_init = 0.8 - 0.6 * math.exp(-0.3 * 0)
    return _layer(x, c, ada_w[0], ada_b[0], norm1_w[0], w_in[0], da_lq1[0], da_lk1[0], da_lq2[0], da_lk2[0],
                  da_subln_w[0], cmp_k_pe[0], cmp_k_w1[0], cmp_k_w2[0], cmp_v_pe[0], cmp_v_w1[0], cmp_v_w2[0],
                  w_da_out[0], w_nsa_out[0], w_o[0], norm2_w[0], router_w[0], router_b[0],
                  exp_w_gate[0], exp_w_up[0], exp_w_down[0], sh_w_gate[0], sh_w_up[0], sh_w_down[0],
                  final_norm_w, lam_init)
```

```python
import functools
import math

import numpy as np
import jax
import jax.numpy as jnp
from jax import lax
from jax.experimental import pallas as pl
from jax.experimental.pallas import tpu as pltpu
from jax.experimental.pallas import tpu_sc as plsc

F32 = jnp.float32
BF16 = jnp.bfloat16
I32 = jnp.int32
U32 = jnp.uint32
HIGHEST = lax.Precision.HIGHEST

NORM_EPS = 1e-6
NEG = -1e30

DA_HEADS = 8
DA_DIM = 64
DA_VDIM = 128
DA_SUBLN_EPS = 1e-5

NSA_HEADS = 8
NSA_GROUPS = 2
NSA_HPG = 4
NSA_DIM = 64
CMP_BLOCK = 32
CMP_STRIDE = 16
SEL_BLOCK = 64
SEL_TOPK = 16
WINDOW = 512
FORCE_BONUS = 1e4
SEL_COLS = 128

N_EXPERTS = 64
MOE_TOPK = 8
N_EXPERT_GROUPS = 8
TOPK_EXPERT_GROUPS = 4
ROUTED_SCALE = 2.5

VMEM_LIMIT_V7X = 56 * 1024 * 1024

COL_DAQ, COL_DAK, COL_DAV = 0, 1024, 2048
COL_MERGE = 3072
COL_NSAQ = 5120
COL_KV6 = 5632
COL_NSAG = 6400
SLAB_COLS = 6528


SUBLANES = 8
SC_WINDOW = 128
SC_SPLIT = 2
SC_PIECE = 256

POS_SPLIT = 16
POS_ROWS = 16
ONES_ROWS = 16


def _pos_features(rows):
    r = np.arange(rows)
    f = np.zeros((rows, 128), np.float32)
    f[:, 0:3] = (r // POS_SPLIT)[:, None]
    f[:, 3:6] = (r % POS_SPLIT)[:, None]
    return jnp.asarray(f, BF16)


def _slope_rows(slopes):
    s = np.asarray(slopes, np.float32)
    bf = lambda x: x.astype(BF16).astype(np.float32)
    p1 = bf(s)
    p2 = bf(s - p1)
    p3 = bf(s - p1 - p2)
    out = np.zeros((s.shape[0], POS_ROWS, 128), np.float32)
    for i, piece in enumerate((p1, p2, p3)):
        out[:, i, :] = POS_SPLIT * piece[:, None]
        out[:, 3 + i, :] = piece[:, None]
    return jnp.asarray(out)


def _with_ones_row(vT):
    shape = vT.shape[:-2] + (ONES_ROWS, vT.shape[-1])
    extra = jnp.zeros(shape, vT.dtype).at[..., 0, :].set(1.0)
    return jnp.concatenate([vT, extra], axis=-2)


def _cparams(sem, vmem=VMEM_LIMIT_V7X):
    return pltpu.CompilerParams(dimension_semantics=sem, vmem_limit_bytes=vmem)


def _ada_kernel(c_ref, w_ref, b_ref, o_ref):
    c = c_ref[...]
    ca = c * jax.nn.sigmoid(c)
    o_ref[...] = jnp.dot(ca, w_ref[...], preferred_element_type=F32, precision=HIGHEST) + b_ref[...]


def _ada(c, w, b):
    B, D = c.shape
    n_out = w.shape[1]
    rows = 8
    cp = jnp.zeros((rows, D), F32).at[:B].set(c)
    tn = 1024
    out = pl.pallas_call(
        _ada_kernel,
        grid=(n_out // tn,),
        in_specs=[pl.BlockSpec((rows, D), lambda j: (0, 0)),
                  pl.BlockSpec((D, tn), lambda j: (0, j)),
                  pl.BlockSpec((1, tn), lambda j: (0, j))],
        out_specs=pl.BlockSpec((rows, tn), lambda j: (0, j)),
        out_shape=jax.ShapeDtypeStruct((rows, n_out), F32),
        compiler_params=_cparams(("arbitrary",)),
        name="ada",
    )(cp, w, b.reshape(1, n_out))
    return out[:B]


def _modulate(x, nw, sh, sc):
    r = lax.rsqrt(jnp.mean(x * x, axis=-1, keepdims=True) + NORM_EPS)
    return (x * r) * nw * (1.0 + sc) + sh


def _inproj_kernel(x_ref, nw_ref, sh_ref, sc_ref, w_ref, o_ref, h_ref):
    @pl.when(pl.program_id(1) == 0)
    def _():
        h_ref[...] = _modulate(x_ref[...], nw_ref[...], sh_ref[...], sc_ref[...]).astype(BF16)

    o_ref[...] = jnp.dot(h_ref[...], w_ref[...], preferred_element_type=F32).astype(BF16)


def _inproj(x2, nw, sh, sc, w_slab, S):
    N, D = x2.shape
    tm = 512
    tn = SLAB_COLS // 3
    per_b = S // tm
    return pl.pallas_call(
        _inproj_kernel,
        grid=(N // tm, SLAB_COLS // tn),
        in_specs=[pl.BlockSpec((tm, D), lambda i, j: (i, 0)),
                  pl.BlockSpec((1, D), lambda i, j: (0, 0)),
                  pl.BlockSpec((None, 1, D), lambda i, j: (i // per_b, 0, 0)),
                  pl.BlockSpec((None, 1, D), lambda i, j: (i // per_b, 0, 0)),
                  pl.BlockSpec((D, tn), lambda i, j: (0, j))],
        out_specs=pl.BlockSpec((tm, tn), lambda i, j: (i, j)),
        out_shape=jax.ShapeDtypeStruct((N, SLAB_COLS), BF16),
        scratch_shapes=[pltpu.VMEM((tm, D), BF16)],
        compiler_params=_cparams(("parallel", "arbitrary")),
        name="inproj",
    )(x2, nw.reshape(1, D), sh.reshape(-1, 1, D), sc.reshape(-1, 1, D), w_slab)


def _cmp_kernel(r_ref, pe_ref, w1a_ref, w1b_ref, w2_ref, o_ref):
    r = r_ref[...]
    n_rows = r.shape[0]
    a = jnp.dot(r, w1a_ref[...], preferred_element_type=F32)
    b = jnp.dot(r, w1b_ref[...], preferred_element_type=F32)
    pe = pe_ref[...]
    half = pe.shape[1] // 2
    bias = (jnp.dot(pe[:, :half], w1a_ref[...].astype(F32), preferred_element_type=F32, precision=HIGHEST)
            + jnp.dot(pe[:, half:], w1b_ref[...].astype(F32), preferred_element_type=F32, precision=HIGHEST))
    hid = a + pltpu.roll(b, n_rows - 1, 0) + bias
    act = 0.5 * hid * (1.0 + jnp.tanh(0.7978845608028654 * (hid + 0.044715 * hid * hid * hid)))
    o_ref[...] = jnp.dot(act, w2_ref[...], preferred_element_type=F32, precision=HIGHEST)


def _compress(rows, pe, w1, w2):
    BG, n_rows, width = rows.shape
    hidden = w1.shape[1]
    w1a = w1[:width].astype(BF16)
    w1b = w1[width:].astype(BF16)
    return pl.pallas_call(
        _cmp_kernel,
        grid=(BG,),
        in_specs=[pl.BlockSpec((None, n_rows, width), lambda i: (i, 0, 0)),
                  pl.BlockSpec((1, 2 * width), lambda i: (0, 0)),
                  pl.BlockSpec((width, hidden), lambda i: (0, 0)),
                  pl.BlockSpec((width, hidden), lambda i: (0, 0)),
                  pl.BlockSpec((hidden, NSA_DIM), lambda i: (0, 0))],
        out_specs=pl.BlockSpec((None, n_rows, NSA_DIM), lambda i: (i, 0, 0)),
        out_shape=jax.ShapeDtypeStruct((BG, n_rows, NSA_DIM), F32),
        compiler_params=_cparams(("parallel",)),
        name="cmp",
    )(rows, pe.reshape(1, -1).astype(F32), w1a, w1b, w2.astype(F32))


def _da_kernel(slopes_ref, kmin_ref, qT_ref, k_ref, vT_ref, pos_ref, srow_ref, lq1_ref, lk1_ref, lq2_ref, lk2_ref, subw_ref,
               oT_ref, acc_ref, qa_ref, sa_ref, sb_ref, pa_ref, pb_ref, rel_ref, *, tq, tk, lam_init):
    h = pl.program_id(1)
    qi = pl.program_id(2)
    slope = slopes_ref[h]
    q0 = qi * tq
    w = 2 * tq
    nk = k_ref.shape[0] // tk
    qT = qT_ref[...]
    row = lax.broadcasted_iota(I32, qT.shape, 0)
    zero = jnp.zeros_like(qT)
    qa_ref[0:2 * DA_DIM, :] = jnp.concatenate(
        [jnp.where(row < DA_DIM, qT, zero), jnp.where(row >= DA_DIM, qT, zero)], axis=1)
    qa_ref[2 * DA_DIM:2 * DA_DIM + POS_ROWS, :] = jnp.concatenate([srow_ref[...]] * (w // 128), axis=1).astype(BF16)
    qa_ref[2 * DA_DIM + POS_ROWS:, :] = jnp.zeros((2 * DA_DIM - POS_ROWS, w), BF16)
    lane = lax.broadcasted_iota(I32, (tk, w), 1)
    rel_ref[...] = lax.broadcasted_iota(I32, (tk, w), 0) - jnp.where(lane >= tq, lane - tq, lane)

    def scores(kt, s_ref):
        k0 = kt * tk
        k_t = k_ref[pl.ds(pl.multiple_of(k0, tk), tk), :]
        s = jnp.dot(jnp.concatenate([k_t, pos_ref[...]], axis=1), qa_ref[...], preferred_element_type=F32)
        s = jnp.where(rel_ref[...] <= q0 - k0, s, NEG)
        s_ref[...] = s
        return jnp.max(s, axis=0, keepdims=True)

    def probs(s_ref, p_ref, mx, off, m_old):
        m_new = jnp.maximum(m_old, mx + off)
        p_ref[...] = jnp.exp(s_ref[...] - (m_new - off)).astype(BF16)
        return m_new, jnp.exp(m_old - m_new)

    def accumulate(kt, p_ref, alpha):
        acc_ref[...] = alpha * acc_ref[...] + jnp.dot(vT_ref[kt], p_ref[...], preferred_element_type=F32)

    n_tiles = (q0 + tq - 1) // tk + 1
    kt_min = kmin_ref[(pl.program_id(0) * pl.num_programs(1) + h) * pl.num_programs(2) + qi]
    acc_ref[...] = jnp.zeros_like(acc_ref)
    pb_ref[...] = jnp.zeros_like(pb_ref)
    mx_a = scores(kt_min, sa_ref)

    def pair(j, carry):
        m, mx_a, alpha_b = carry
        ta = kt_min + 2 * j
        tb = ta + 1
        off_a = slope * (ta * tk).astype(F32)
        off_b = jnp.where(tb < n_tiles, slope * (tb * tk).astype(F32), NEG)
        accumulate(jnp.maximum(ta - 1, 0), pb_ref, alpha_b)
        m, alpha_a = probs(sa_ref, pa_ref, mx_a, off_a, m)
        mx_b = scores(jnp.minimum(tb, nk - 1), sb_ref)
        accumulate(ta, pa_ref, alpha_a)
        m, alpha_b = probs(sb_ref, pb_ref, mx_b, off_b, m)
        mx_a = scores(jnp.minimum(ta + 2, nk - 1), sa_ref)
        return m, mx_a, alpha_b

    n_pairs = (n_tiles - kt_min + 1) // 2
    init = (jnp.full((1, w), NEG, F32), mx_a, jnp.ones((1, w), F32))
    _, _, alpha_b = lax.fori_loop(0, n_pairs, pair, init)
    accumulate(jnp.minimum(kt_min + 2 * n_pairs - 1, nk - 1), pb_ref, alpha_b)

    lam = (jnp.exp(jnp.sum(lq1_ref[...] * lk1_ref[...], axis=-1, keepdims=True))
           - jnp.exp(jnp.sum(lq2_ref[...] * lk2_ref[...], axis=-1, keepdims=True)) + lam_init)
    on = acc_ref[0:DA_VDIM, :] * (1.0 / acc_ref[DA_VDIM:DA_VDIM + 1, :])
    o = on[:, :tq] - lam * on[:, tq:]
    r = lax.rsqrt(jnp.mean(o * o, axis=0, keepdims=True) + DA_SUBLN_EPS)
    oT_ref[...] = (o * r * subw_ref[...] * (1.0 - lam_init)).astype(BF16)


EXP_UNDERFLOW = 104.0


def _first_live_tile(slopes_np, slab3, tq, tk):
    B, S, _ = slab3.shape
    H = DA_HEADS
    width = 2 * DA_DIM * H
    grp = jnp.asarray((np.arange(width)[:, None] // DA_DIM == np.arange(2 * H)[None, :]).astype(np.float32))

    def norms(c0):
        v = slab3[:, :, c0:c0 + width].astype(F32)
        return jnp.sqrt(jnp.einsum('bsc,cg->bsg', v * v, grp, precision=HIGHEST))

    qn = norms(COL_DAQ) * DA_DIM ** -0.5
    qn = jnp.max(qn.reshape(B, S // tq, tq, H, 2), axis=2).transpose(0, 2, 3, 1)
    kn = jnp.max(norms(COL_DAK), axis=1).reshape(B, H, 2)
    bound = jnp.max(qn * kn[..., None], axis=2) * 1.01 + 0.01
    q0 = (jnp.arange(S // tq, dtype=F32) * tq)[None, None, :]
    slope = jnp.asarray(slopes_np, F32)[None, :, None]
    first_key = q0 - (EXP_UNDERFLOW + 2.0 * bound) / slope
    kt = jnp.floor(first_key / tk).astype(I32)
    n_tiles = (jnp.arange(S // tq, dtype=I32) * tq + tq - 1) // tk + 1
    return jnp.clip(kt, 0, n_tiles[None, None, :] - 1).reshape(-1)


def _diff_attention(slopes, kmin, qT, slab3, vT, srows, lq1, lk1, lq2, lk2, subw, lam_init, tq, tk):
    B, _, S = qT.shape
    nk = S // tk
    assert tk % tq == 0 and S % tk == 0
    tile = tq
    kcol = COL_DAK // 128
    v_rows = vT.shape[3]
    vec = lambda a: a.reshape(1, DA_DIM).astype(F32)
    grid_spec = pltpu.PrefetchScalarGridSpec(
        num_scalar_prefetch=2,
        grid=(B, DA_HEADS, S // tile),
        in_specs=[pl.BlockSpec((None, 128, tile), lambda b, h, i, s, km: (b, h, i)),
                  pl.BlockSpec((None, S, 128), lambda b, h, i, s, km: (b, 0, kcol + h)),
                  pl.BlockSpec((None, None, nk, v_rows, tk), lambda b, h, i, s, km: (b, h, 0, 0, 0)),
                  pl.BlockSpec((tk, 128), lambda b, h, i, s, km: (0, 0)),
                  pl.BlockSpec((None, POS_ROWS, 128), lambda b, h, i, s, km: (h, 0, 0)),
                  pl.BlockSpec((1, DA_DIM), lambda b, h, i, s, km: (0, 0)),
                  pl.BlockSpec((1, DA_DIM), lambda b, h, i, s, km: (0, 0)),
                  pl.BlockSpec((1, DA_DIM), lambda b, h, i, s, km: (0, 0)),
                  pl.BlockSpec((1, DA_DIM), lambda b, h, i, s, km: (0, 0)),
                  pl.BlockSpec((DA_VDIM, 1), lambda b, h, i, s, km: (0, 0))],
        out_specs=pl.BlockSpec((None, 128, tile), lambda b, h, i, s, km: (b, h, i)),
        scratch_shapes=[pltpu.VMEM((v_rows, 2 * tq), F32), pltpu.VMEM((4 * DA_DIM, 2 * tq), BF16),
                        pltpu.VMEM((tk, 2 * tq), F32), pltpu.VMEM((tk, 2 * tq), F32),
                        pltpu.VMEM((tk, 2 * tq), BF16), pltpu.VMEM((tk, 2 * tq), BF16),
                        pltpu.VMEM((tk, 2 * tq), I32)],
    )
    return pl.pallas_call(
        functools.partial(_da_kernel, tq=tq, tk=tk, lam_init=lam_init),
        grid_spec=grid_spec,
        out_shape=jax.ShapeDtypeStruct((B, DA_HEADS * DA_VDIM, S), BF16),
        compiler_params=_cparams(("parallel", "parallel", "arbitrary")),
        name="diffattn",
    )(slopes, kmin, qT, slab3, vT, _pos_features(tk), srows, vec(lq1), vec(lk1), vec(lq2), vec(lk2),
      subw.reshape(DA_VDIM, 1).astype(F32))


def _nsa_kernel(slopes_ref, qT_ref, kc_ref, vcT_ref, ksa_ref, vsT_ref, kwa_ref, vwT_ref, wb_ref, gT_ref, mT_ref,
                grp_ref, ltri_ref, oT_ref, acc_ref, res_ref, base_ref, qa_ref, sa_ref, sb_ref, pa_ref, pb_ref,
                list_ref, slist_ref, lsem, *, t, tk, n_top):
    g = pl.program_id(1)
    qi = pl.program_id(2)
    q0 = qi * t
    w = NSA_HPG * t
    n_cp = kc_ref.shape[0]
    n_sel = mT_ref.shape[0]
    lane1 = lax.broadcasted_iota(I32, (1, w), 1)
    slope_row = jnp.zeros((1, w), F32)
    for hh in range(NSA_HPG):
        slope_row = jnp.where((lane1 >= hh * t) & (lane1 < (hh + 1) * t), slopes_ref[g * NSA_HPG + hh], slope_row)
    q_all = qT_ref[...]
    gates = jax.nn.sigmoid(gT_ref[...])
    key_off = lax.broadcasted_iota(I32, (tk, w), 0)
    t_loc = jnp.bitwise_and(lax.broadcasted_iota(I32, (tk, w), 1), t - 1)
    base_ref[...] = slope_row * key_off.astype(F32)

    c_idx = lax.broadcasted_iota(I32, (n_cp, w), 0)
    t_c = q0 + jnp.bitwise_and(lax.broadcasted_iota(I32, (n_cp, w), 1), t - 1)
    d_c = t_c - (c_idx * CMP_STRIDE + (CMP_BLOCK - 1))
    ok_c = d_c >= 0
    s = jnp.dot(kc_ref[...], q_all, preferred_element_type=F32) - slope_row * d_c.astype(F32)
    s = jnp.where(ok_c, s, NEG)
    e = jnp.exp(s - jnp.max(s, axis=0, keepdims=True))
    inv = 1.0 / jnp.sum(e, axis=0, keepdims=True)
    p = jnp.where(ok_c, e * inv, 0.0)
    res_ref[...] = gates[0:1, :] * jnp.dot(vcT_ref[...], p.astype(BF16), preferred_element_type=F32)
    p_sum = p[:, 0:t]
    for hh in range(1, NSA_HPG):
        p_sum = p_sum + p[:, hh * t:(hh + 1) * t]

    p_hi = p_sum.astype(BF16)
    p_lo = (p_sum - p_hi.astype(F32)).astype(BF16)
    mT = mT_ref[...]
    imp = jnp.dot(mT, p_hi, preferred_element_type=F32) + jnp.dot(mT, p_lo, preferred_element_type=F32)
    blk = lax.broadcasted_iota(I32, (n_sel, t), 0)
    t_s = q0 + lax.broadcasted_iota(I32, (n_sel, t), 1)
    cur = jnp.right_shift(t_s, int(math.log2(SEL_BLOCK)))
    forced = (blk == 0) | (blk == cur) | (blk == cur - 1)
    score = jnp.where(blk * SEL_BLOCK <= t_s, imp + jnp.where(forced, FORCE_BONUS, 0.0), NEG)
    blk_f = blk.astype(F32)
    sel = jnp.zeros((n_sel, t), F32)
    for _ in range(n_top):
        _, idx = _first_max(score, blk_f)
        hit = blk_f == idx
        sel = jnp.where(hit, 1.0, sel)
        score = jnp.where(hit, -jnp.inf, score)
    n_full = q0 // tk
    nt_pad = grp_ref.shape[0]
    cnt = jnp.broadcast_to(jnp.sum(sel, axis=1, keepdims=True), (n_sel, 128)).astype(BF16)
    tile_cnt = jnp.dot(grp_ref[...], cnt, preferred_element_type=F32)
    kt_i = lax.broadcasted_iota(I32, (nt_pad, 128), 0)
    active = (tile_cnt > 0.5) & (kt_i < n_full)
    act_f = jnp.where(active, 1.0, 0.0)
    before = jnp.dot(ltri_ref[...], act_f.astype(BF16), preferred_element_type=F32)
    lane_j = lax.broadcasted_iota(I32, (nt_pad, 128), 1)
    slot_hit = active & (before == lane_j.astype(F32))
    tiles_row = jnp.sum(jnp.where(slot_hit, kt_i.astype(F32), 0.0), axis=0, keepdims=True)
    n_act_row = jnp.sum(act_f, axis=0, keepdims=True)
    lane8 = lax.broadcasted_iota(I32, (8, 128), 1)
    list_ref[...] = jnp.where(lane8 == 127, n_act_row, tiles_row).astype(I32)
    list_copy = pltpu.make_async_copy(list_ref, slist_ref, lsem)
    list_copy.start()

    selb = jnp.where(sel > 0.5, 0.0, NEG)
    row64 = lax.broadcasted_iota(I32, (NSA_DIM, w), 0)
    qa_ref[0:NSA_DIM, :] = q_all
    qa_ref[NSA_DIM:2 * NSA_DIM, :] = jnp.where(row64 == 0, NEG, 0.0).astype(BF16)
    qa_ref[2 * NSA_DIM:2 * NSA_DIM + n_sel, :] = jnp.concatenate([selb] * NSA_HPG, axis=1).astype(BF16)
    if n_sel < SEL_COLS:
        qa_ref[2 * NSA_DIM + n_sel:, :] = jnp.zeros((SEL_COLS - n_sel, w), BF16)

    n_wt = (WINDOW + t) // t
    kw_t = kwa_ref[pl.ds(pl.multiple_of(q0, t), WINDOW + t), :]
    s = jnp.dot(kw_t, qa_ref[0:2 * NSA_DIM, :], preferred_element_type=F32) + wb_ref[...]
    p = jnp.exp(s - jnp.max(s, axis=0, keepdims=True))
    inv = 1.0 / jnp.sum(p, axis=0, keepdims=True)
    p = p.astype(BF16)
    o_w = jnp.dot(vwT_ref[qi], p[0:t, :], preferred_element_type=F32)
    for c in range(1, n_wt):
        o_w = o_w + jnp.dot(vwT_ref[qi + c], p[c * t:(c + 1) * t, :], preferred_element_type=F32)
    res_ref[...] = res_ref[...] + gates[2:3, :] * (o_w * inv)

    def scores(kt, s_ref):
        k_t = ksa_ref[pl.ds(pl.multiple_of(kt * tk, tk), tk), :]
        s = jnp.dot(k_t, qa_ref[...], preferred_element_type=F32) + base_ref[...]
        s_ref[...] = s
        return jnp.max(s, axis=0, keepdims=True)

    def probs(s_ref, p_ref, mx, off, m_old, l_old):
        m_new = jnp.maximum(m_old, mx + off)
        p = jnp.exp(s_ref[...] - (m_new - off))
        alpha = jnp.exp(m_old - m_new)
        p_ref[...] = p.astype(BF16)
        return m_new, alpha * l_old + jnp.sum(p, axis=0, keepdims=True), alpha

    def accumulate(kt, p_ref, alpha):
        acc_ref[...] = alpha * acc_ref[...] + jnp.dot(vsT_ref[kt], p_ref[...], preferred_element_type=F32)

    list_copy.wait()
    n_act = slist_ref[0, 127]
    mx_a = scores(slist_ref[0, 0], sa_ref)

    kd = n_full * tk
    s = jnp.dot(ksa_ref[pl.ds(pl.multiple_of(kd, tk), tk), :], qa_ref[...], preferred_element_type=F32) + base_ref[...]
    s = jnp.where(key_off - t_loc <= q0 - kd, s, NEG)
    off_d = slope_row * kd.astype(F32)
    m = jnp.max(s, axis=0, keepdims=True) + off_d
    p = jnp.exp(s - (m - off_d))
    l = jnp.sum(p, axis=0, keepdims=True)
    acc_ref[...] = jnp.dot(vsT_ref[n_full], p.astype(BF16), preferred_element_type=F32)
    pb_ref[...] = jnp.zeros_like(pb_ref)

    def pair(j, carry):
        m, l, mx_a, alpha_b, tb_prev = carry
        ta = slist_ref[0, 2 * j]
        tb = slist_ref[0, 2 * j + 1]
        off_a = slope_row * (ta * tk).astype(F32)
        off_b = jnp.where(2 * j + 1 < n_act, slope_row * (tb * tk).astype(F32), NEG)
        accumulate(tb_prev, pb_ref, alpha_b)
        m, l, alpha_a = probs(sa_ref, pa_ref, mx_a, off_a, m, l)
        mx_b = scores(tb, sb_ref)
        accumulate(ta, pa_ref, alpha_a)
        m, l, alpha_b = probs(sb_ref, pb_ref, mx_b, off_b, m, l)
        mx_a = scores(slist_ref[0, 2 * j + 2], sa_ref)
        return m, l, mx_a, alpha_b, tb

    n_pairs = (n_act + 1) // 2
    m, l, _, alpha_b, tb_last = lax.fori_loop(0, n_pairs, pair, (m, l, mx_a, jnp.ones((1, w), F32), jnp.int32(0)))
    accumulate(tb_last, pb_ref, alpha_b)
    oT_ref[...] = (res_ref[...] + gates[1:2, :] * (acc_ref[...] * (1.0 / l))).astype(BF16)


def _nsa_attention(slopes, qT, kc, vcT, ksa, vsT, kwa, vwT, wb, gT, mT, t, tk):
    B, G, nq, _, w = qT.shape
    S = nq * t
    n_cp = kc.shape[2]
    n_sel = S // SEL_BLOCK
    n_top = min(SEL_TOPK, n_sel)
    assert tk % t == 0 and S % tk == 0 and t & (t - 1) == 0 and WINDOW % t == 0 and n_sel <= SEL_COLS
    n_tiles = S // tk
    nt_pad = -(-n_tiles // 16) * 16
    assert nt_pad + 2 < 127
    grp = jnp.asarray((np.arange(n_sel)[None, :] // (tk // SEL_BLOCK) == np.arange(nt_pad)[:, None])
                      .astype(np.float32), BF16)
    ltri = jnp.asarray((np.arange(nt_pad)[None, :] < np.arange(nt_pad)[:, None]).astype(np.float32), BF16)
    fixed = lambda b, g, i, s: (b, g, 0, 0)
    fixed5 = lambda b, g, i, s: (b, g, 0, 0, 0)
    tile5 = lambda b, g, i, s: (b, g, i, 0, 0)
    grid_spec = pltpu.PrefetchScalarGridSpec(
        num_scalar_prefetch=1,
        grid=(B, G, nq),
        in_specs=[pl.BlockSpec((None, None, None, NSA_DIM, w), tile5),
                  pl.BlockSpec((None, None, n_cp, NSA_DIM), fixed),
                  pl.BlockSpec((None, None, NSA_DIM, n_cp), fixed),
                  pl.BlockSpec((None, None, S, ksa.shape[-1]), fixed),
                  pl.BlockSpec((None, None, S // tk, NSA_DIM, tk), fixed5),
                  pl.BlockSpec((None, None, S + WINDOW, 2 * NSA_DIM), fixed),
                  pl.BlockSpec((None, None, (S + WINDOW) // t, NSA_DIM, t), fixed5),
                  pl.BlockSpec((None, WINDOW + t, w), lambda b, g, i, s: (g, 0, 0)),
                  pl.BlockSpec((None, None, None, 3, w), tile5),
                  pl.BlockSpec((n_sel, n_cp), lambda b, g, i, s: (0, 0)),
                  pl.BlockSpec((nt_pad, n_sel), lambda b, g, i, s: (0, 0)),
                  pl.BlockSpec((nt_pad, nt_pad), lambda b, g, i, s: (0, 0))],
        out_specs=pl.BlockSpec((None, None, None, NSA_DIM, w), tile5),
        scratch_shapes=[pltpu.VMEM((NSA_DIM, w), F32),
                        pltpu.VMEM((NSA_DIM, w), F32),
                        pltpu.VMEM((tk, w), F32),
                        pltpu.VMEM((2 * NSA_DIM + SEL_COLS, w), BF16),
                        pltpu.VMEM((tk, w), F32), pltpu.VMEM((tk, w), F32),
                        pltpu.VMEM((tk, w), BF16), pltpu.VMEM((tk, w), BF16),
                        pltpu.VMEM((8, 128), I32), pltpu.SMEM((8, 128), I32), pltpu.SemaphoreType.DMA(())],
    )
    return pl.pallas_call(
        functools.partial(_nsa_kernel, t=t, tk=tk, n_top=n_top),
        grid_spec=grid_spec,
        out_shape=jax.ShapeDtypeStruct((B, G, nq, NSA_DIM, w), BF16),
        compiler_params=_cparams(("parallel", "parallel", "arbitrary")),
        name="nsa",
    )(slopes, qT, kc, vcT, ksa, vsT, kwa, vwT, wb, gT, mT, grp, ltri)


def _pack_halves(v):
    c = v.shape[1] // 2
    lo = lax.bitcast_convert_type(v[:, :c].astype(BF16).astype(F32), U32)
    hi = lax.bitcast_convert_type(v[:, c:].astype(BF16).astype(F32), U32)
    return hi | (lo >> 16)


def _unpack_halves(u):
    lo = lax.bitcast_convert_type(u << 16, F32)
    hi = lax.bitcast_convert_type(u & jnp.uint32(0xFFFF0000), F32)
    return lo, hi


def _store_slabs(ref, v):
    packed = _pack_halves(v)
    for s in range(SC_SPLIT):
        ref[s] = packed[:, s * SC_PIECE:(s + 1) * SC_PIECE]


def _load_slabs_bf16(ref):
    parts = [_unpack_halves(ref[s]) for s in range(SC_SPLIT)]
    return jnp.concatenate([p[0].astype(BF16) for p in parts] + [p[1].astype(BF16) for p in parts], axis=1)


def _merge_kernel(oa_ref, ob_ref, ga_ref, gb_ref, x_ref, g1_ref, nw_ref, sh_ref, sc_ref,
                  wa_ref, wb_ref, wo_ref, wr_ref, x1_ref, h2_ref, lg_ref):
    ya = jnp.dot(oa_ref[...], wa_ref[...], preferred_element_type=F32)
    yb = jnp.dot(ob_ref[...], wb_ref[...], preferred_element_type=F32)
    merged = (jax.nn.sigmoid(ga_ref[...].astype(F32)) * ya
              + jax.nn.sigmoid(gb_ref[...].astype(F32)) * yb).astype(BF16)
    mix = jnp.dot(merged, wo_ref[...], preferred_element_type=F32)
    x1 = x_ref[...] + g1_ref[...] * mix
    x1_ref[...] = x1
    h2 = _modulate(x1, nw_ref[...], sh_ref[...], sc_ref[...])
    _store_slabs(h2_ref, h2)
    lg_ref[...] = jnp.dot(h2, wr_ref[...], preferred_element_type=F32, precision=HIGHEST)


def _merge(oa, ob, slab, x2, g1, nw, sh, sc, wa, wb, wo, wr, S):
    N, D = x2.shape
    tm = 256
    per_b = S // tm
    mcol = COL_MERGE // D
    row = lambda i: (i, 0)
    full = lambda i: (0, 0)
    perb = lambda i: (i // per_b, 0, 0)
    n_lg = wr.shape[1]
    return pl.pallas_call(
        _merge_kernel,
        grid=(N // tm,),
        in_specs=[pl.BlockSpec((tm, oa.shape[1]), row),
                  pl.BlockSpec((tm, ob.shape[1]), row),
                  pl.BlockSpec((tm, D), lambda i: (i, mcol)),
                  pl.BlockSpec((tm, D), lambda i: (i, mcol + 1)),
                  pl.BlockSpec((tm, D), row),
                  pl.BlockSpec((None, 1, D), perb),
                  pl.BlockSpec((1, D), full),
                  pl.BlockSpec((None, 1, D), perb),
                  pl.BlockSpec((None, 1, D), perb),
                  pl.BlockSpec(wa.shape, full),
                  pl.BlockSpec(wb.shape, full),
                  pl.BlockSpec(wo.shape, full),
                  pl.BlockSpec(wr.shape, full)],
        out_specs=[pl.BlockSpec((tm, D), row), pl.BlockSpec((SC_SPLIT, tm, SC_PIECE), lambda i: (0, i, 0)),
                   pl.BlockSpec((tm, n_lg), row)],
        out_shape=[jax.ShapeDtypeStruct((N, D), F32), jax.ShapeDtypeStruct((SC_SPLIT, N, SC_PIECE), U32),
                   jax.ShapeDtypeStruct((N, n_lg), F32)],
        compiler_params=_cparams(("parallel",)),
        name="merge",
    )(oa, ob, slab, slab, x2, g1.reshape(-1, 1, D), nw.reshape(1, D), sh.reshape(-1, 1, D),
      sc.reshape(-1, 1, D), wa, wb, wo, wr)


def _first_max(v, idx):
    mx = jnp.max(v, axis=0, keepdims=True)
    first = jnp.min(jnp.where(v == mx, idx, float(v.shape[0])), axis=0, keepdims=True)
    return mx, first


def _route_kernel(lg_ref, b_ref, e_ref, w_ref):
    lg = lg_ref[...]
    tn = lg.shape[1]
    per_g = N_EXPERTS // N_EXPERT_GROUPS
    scores = jax.nn.sigmoid(lg)
    biased = scores + b_ref[...]
    gi = lax.broadcasted_iota(I32, (N_EXPERT_GROUPS, tn), 0).astype(F32)
    gscore = jnp.zeros((N_EXPERT_GROUPS, tn), F32)
    for g in range(N_EXPERT_GROUPS):
        grp = biased[g * per_g:(g + 1) * per_g, :]
        m1, f1 = _first_max(grp, gi)
        m2 = jnp.max(jnp.where(gi == f1, -jnp.inf, grp), axis=0, keepdims=True)
        gscore = jnp.where(gi == g, m1 + m2, gscore)
    gsel = jnp.zeros((N_EXPERT_GROUPS, tn), F32)
    for _ in range(TOPK_EXPERT_GROUPS):
        _, f = _first_max(gscore, gi)
        hit = gi == f
        gsel = jnp.where(hit, 1.0, gsel)
        gscore = jnp.where(hit, -jnp.inf, gscore)
    ei = lax.broadcasted_iota(I32, (N_EXPERTS, tn), 0).astype(F32)
    emask = jnp.zeros((N_EXPERTS, tn), F32)
    for g in range(N_EXPERT_GROUPS):
        in_g = (ei >= g * per_g) & (ei < (g + 1) * per_g)
        emask = jnp.where(in_g, gsel[g:g + 1, :], emask)
    masked = jnp.where(emask > 0.5, biased, NEG)
    e_out = jnp.zeros((MOE_TOPK, tn), F32)
    w_out = jnp.zeros((MOE_TOPK, tn), F32)
    for r in range(MOE_TOPK):
        _, f = _first_max(masked, ei)
        hit = ei == f
        wv = jnp.sum(jnp.where(hit, scores, 0.0), axis=0, keepdims=True)
        e_out = jnp.where(gi == r, f, e_out)
        w_out = jnp.where(gi == r, wv, w_out)
        masked = jnp.where(hit, -jnp.inf, masked)
    e_ref[...] = e_out.astype(I32)
    w_ref[...] = w_out / jnp.sum(w_out, axis=0, keepdims=True) * ROUTED_SCALE


def _route(lgT, rb):
    E, N = lgT.shape
    tn = 512
    return pl.pallas_call(
        _route_kernel,
        grid=(N // tn,),
        in_specs=[pl.BlockSpec((E, tn), lambda i: (0, i)), pl.BlockSpec((E, 1), lambda i: (0, 0))],
        out_specs=[pl.BlockSpec((MOE_TOPK, tn), lambda i: (0, i)), pl.BlockSpec((MOE_TOPK, tn), lambda i: (0, i))],
        out_shape=[jax.ShapeDtypeStruct((MOE_TOPK, N), I32), jax.ShapeDtypeStruct((MOE_TOPK, N), F32)],
        compiler_params=_cparams(("parallel",)),
        name="route",
    )(lgT, rb.reshape(E, 1).astype(F32))


def _slots_kernel(e_ref, tri_ref, lt_ref, dest_ref, cnt_ref, carry_ref, *, rb):
    phase = pl.program_id(0)
    i = pl.program_id(1)
    e = e_ref[...]
    tn = e.shape[1]
    ei = lax.broadcasted_iota(I32, (N_EXPERTS, tn), 0)

    @pl.when((phase == 0) & (i == 0))
    def _():
        carry_ref[...] = jnp.zeros_like(carry_ref)

    @pl.when(phase == 0)
    def _():
        tot = jnp.zeros((N_EXPERTS, 1), F32)
        for k in range(MOE_TOPK):
            oh = jnp.where(ei == e[k:k + 1, :], 1.0, 0.0)
            tot = tot + jnp.sum(oh, axis=1, keepdims=True)
        carry_ref[...] = carry_ref[...] + tot
        dest_ref[...] = jnp.zeros_like(dest_ref)

    @pl.when((phase == 1) & (i == 0))
    def _():
        cnt = carry_ref[...]
        cnt_ref[...] = cnt
        padded = jnp.broadcast_to(jnp.ceil(cnt / rb) * rb, (N_EXPERTS, 128))
        first = jnp.dot(lt_ref[...], padded, preferred_element_type=F32, precision=HIGHEST)
        carry_ref[...] = first[:, :1]

    @pl.when(phase == 1)
    def _():
        carry = carry_ref[...]
        ki = lax.broadcasted_iota(I32, (MOE_TOPK, tn), 0)
        dest = jnp.zeros((MOE_TOPK, tn), F32)
        for k in range(MOE_TOPK):
            hit = ei == e[k:k + 1, :]
            oh = jnp.where(hit, 1.0, 0.0)
            before = jnp.dot(oh.astype(BF16), tri_ref[...], preferred_element_type=F32)
            row = jnp.sum(jnp.where(hit, before + carry, 0.0), axis=0, keepdims=True)
            dest = jnp.where(ki == k, row, dest)
            carry = carry + jnp.sum(oh, axis=1, keepdims=True)
        carry_ref[...] = carry
        dest_ref[...] = dest.astype(I32)


def _slots(top_eT, rb):
    K, N = top_eT.shape
    tn = 512
    idx = np.arange(tn)
    tri = jnp.asarray((idx[:, None] < idx[None, :]).astype(np.float32), BF16)
    e_idx = np.arange(N_EXPERTS)
    lt = jnp.asarray((e_idx[None, :] < e_idx[:, None]).astype(np.float32), F32)
    return pl.pallas_call(
        functools.partial(_slots_kernel, rb=rb),
        grid=(2, N // tn),
        in_specs=[pl.BlockSpec((K, tn), lambda p, i: (0, i)),
                  pl.BlockSpec((tn, tn), lambda p, i: (0, 0)),
                  pl.BlockSpec((N_EXPERTS, N_EXPERTS), lambda p, i: (0, 0))],
        out_specs=[pl.BlockSpec((K, tn), lambda p, i: (0, i * p)),
                   pl.BlockSpec((N_EXPERTS, 1), lambda p, i: (0, 0))],
        out_shape=[jax.ShapeDtypeStruct((K, N), I32), jax.ShapeDtypeStruct((N_EXPERTS, 1), F32)],
        scratch_shapes=[pltpu.VMEM((N_EXPERTS, 1), F32)],
        compiler_params=_cparams(("arbitrary", "arbitrary")),
        name="slots",
    )(top_eT, tri, lt)


def _sc_scatter_rows(src, idx, n_out, repeat):
    n_idx = idx.shape[0]
    width = src.shape[1]
    w = src.shape[0] // SC_SPLIT // SC_WINDOW
    mesh = plsc.VectorSubcoreMesh(core_axis_name="c", subcore_axis_name="s")

    @pl.kernel(out_type=jax.ShapeDtypeStruct((n_out, width), src.dtype), mesh=mesh)
    def scatter_kernel(x_hbm, i_hbm, o_hbm):
        def body(x_vmem, i_vmem):
            pltpu.sync_copy(x_vmem, o_hbm.at[i_vmem.at[0]])

        pltpu.emit_pipeline(
            body,
            grid=(n_idx // SC_WINDOW,),
            in_specs=[pl.BlockSpec((SC_WINDOW, width), lambda g: ((g // (repeat * w)) * w + g % w, 0)),
                      pl.BlockSpec((1, SC_WINDOW), lambda g: (0, g))],
            out_specs=[],
            core_axis_name=("c", "s"),
            dimension_semantics=(pltpu.PARALLEL,),
        )(x_hbm, i_hbm)

    return scatter_kernel(src, idx.reshape(1, n_idx))


def _zero_pads_kernel(start_ref, len_ref, xs_in_ref, xs_ref, zeros_ref, sem, *, rb, n_rows):
    del xs_in_ref
    zeros_ref[...] = jnp.zeros_like(zeros_ref)
    n_seg = start_ref.shape[0]
    bits = rb.bit_length() - 1
    assert rb == 1 << bits

    def for_each_copy(act):
        def rows(row0, size):
            for s in range(SC_SPLIT):
                act(pltpu.make_async_copy(zeros_ref.at[pl.ds(0, size), :],
                                          xs_ref.at[pl.ds(s * n_rows + row0, size), :], sem))

        def segment(e, c):
            start = start_ref[e]
            length = len_ref[e]
            end = start + length
            n_whole = length // rb

            def whole(j, c2):
                rows(pl.multiple_of(end - (j + 1) * rb, SUBLANES), rb)
                return c2

            lax.fori_loop(0, n_whole, whole, 0)
            off = end - n_whole * rb
            rest = length - n_whole * rb
            for b in reversed(range(SUBLANES.bit_length() - 1, bits)):
                size = 1 << b
                take = jnp.bitwise_and(jnp.right_shift(rest, b), 1)
                off = off - take * size

                @pl.when(take == 1)
                def _(off=off, size=size):
                    rows(pl.multiple_of(off, SUBLANES), size)

            lead = jnp.bitwise_and(rest, SUBLANES - 1)
            for i in range(SUBLANES - 1):
                @pl.when(i < lead)
                def _(i=i):
                    rows(start + i, 1)

            return c

        lax.fori_loop(0, n_seg, segment, 0)

    for_each_copy(lambda cp: cp.start())
    for_each_copy(lambda cp: cp.wait())


def _zero_pads(xs2d, start, length, rb, n_rows):
    grid_spec = pltpu.PrefetchScalarGridSpec(
        num_scalar_prefetch=2,
        grid=(1,),
        in_specs=[pl.BlockSpec(memory_space=pl.ANY)],
        out_specs=pl.BlockSpec(memory_space=pl.ANY),
        scratch_shapes=[pltpu.VMEM((rb, xs2d.shape[1]), xs2d.dtype), pltpu.SemaphoreType.DMA(())],
    )
    return pl.pallas_call(
        functools.partial(_zero_pads_kernel, rb=rb, n_rows=n_rows),
        grid_spec=grid_spec,
        out_shape=jax.ShapeDtypeStruct(xs2d.shape, xs2d.dtype),
        input_output_aliases={2: 0},
        compiler_params=_cparams(("arbitrary",)),
        name="zeropads",
    )(start, length, xs2d)


def _experts_kernel(be_ref, x_ref, wg_ref, wu_ref, wd_ref, y_ref, wgu_sc, wd_sc):
    i = pl.program_id(0)
    hdim = wg_ref.shape[1]

    @pl.when((i == 0) | (be_ref[i] != be_ref[jnp.maximum(i - 1, 0)]))
    def _():
        wgu_sc[:, 0:hdim] = wg_ref[...].astype(BF16)
        wgu_sc[:, hdim:] = wu_ref[...].astype(BF16)
        wd_sc[...] = wd_ref[...].astype(BF16)

    x = _load_slabs_bf16(x_ref)
    gu = jnp.dot(x, wgu_sc[...], preferred_element_type=F32)
    g = gu[:, :hdim]
    act = (g * jax.nn.sigmoid(g) * gu[:, hdim:]).astype(BF16)
    _store_slabs(y_ref, jnp.dot(act, wd_sc[...], preferred_element_type=F32))


def _experts(blk_e, xs, wg, wu, wd, rb):
    n_rows = xs.shape[1]
    d_model, hdim = wg.shape[1:]
    per_expert = lambda i, be: (be[i], 0, 0)
    grid_spec = pltpu.PrefetchScalarGridSpec(
        num_scalar_prefetch=1,
        grid=(n_rows // rb,),
        in_specs=[pl.BlockSpec((SC_SPLIT, rb, SC_PIECE), lambda i, be: (0, i, 0)),
                  pl.BlockSpec((None, d_model, hdim), per_expert),
                  pl.BlockSpec((None, d_model, hdim), per_expert),
                  pl.BlockSpec((None, hdim, d_model), per_expert)],
        out_specs=pl.BlockSpec((SC_SPLIT, rb, SC_PIECE), lambda i, be: (0, i, 0)),
        scratch_shapes=[pltpu.VMEM((d_model, 2 * hdim), BF16), pltpu.VMEM((hdim, d_model), BF16)],
    )
    return pl.pallas_call(
        _experts_kernel,
        grid_spec=grid_spec,
        out_shape=jax.ShapeDtypeStruct((SC_SPLIT, n_rows, SC_PIECE), U32),
        compiler_params=_cparams(("arbitrary",)),
        name="experts",
    )(blk_e, xs, wg, wu, wd)


def _sc_gather_rows(table, idx):
    n = idx.shape[0]
    width = table.shape[1]
    mesh = plsc.VectorSubcoreMesh(core_axis_name="c", subcore_axis_name="s")

    @pl.kernel(out_type=jax.ShapeDtypeStruct((n, width), table.dtype), mesh=mesh)
    def gather_kernel(t_hbm, i_hbm, o_hbm):
        def body(i_vmem, o_vmem):
            pltpu.sync_copy(t_hbm.at[i_vmem.at[0]], o_vmem)

        pltpu.emit_pipeline(
            body,
            grid=(n // SC_WINDOW,),
            in_specs=[pl.BlockSpec((1, SC_WINDOW), lambda i: (0, i))],
            out_specs=[pl.BlockSpec((SC_WINDOW, width), lambda i: (i, 0))],
            core_axis_name=("c", "s"),
            dimension_semantics=(pltpu.PARALLEL,),
        )(i_hbm, o_hbm)

    return gather_kernel(table, idx.reshape(1, n))


def _combine_kernel(yg_ref, w_ref, h_ref, x1_ref, g2_ref, fw_ref, sgu_ref, sd_ref, o_ref):
    gu = jnp.dot(_load_slabs_bf16(h_ref), sgu_ref[...], preferred_element_type=F32)
    hdim = gu.shape[1] // 2
    g = gu[:, :hdim]
    ffn = jnp.dot((g * jax.nn.sigmoid(g) * gu[:, hdim:]).astype(BF16), sd_ref[...], preferred_element_type=F32)

    w = w_ref[...]
    tt = w.shape[0]
    lo = [jnp.zeros((tt, SC_PIECE), F32) for _ in range(SC_SPLIT)]
    hi = [jnp.zeros((tt, SC_PIECE), F32) for _ in range(SC_SPLIT)]
    for k in range(MOE_TOPK):
        wk = w[:, k:k + 1]
        for s in range(SC_SPLIT):
            y_lo, y_hi = _unpack_halves(yg_ref[s, k])
            lo[s] = lo[s] + wk * y_lo
            hi[s] = hi[s] + wk * y_hi
    ffn = ffn + jnp.concatenate(lo + hi, axis=1)
    x2 = x1_ref[...] + g2_ref[...] * ffn
    r = lax.rsqrt(jnp.mean(x2 * x2, axis=-1, keepdims=True) + NORM_EPS)
    o_ref[...] = x2 * r * fw_ref[...]


def _combine(yg, top_w, h2p, x1, g2, fw, sgu, sd, S, tt):
    N, D = x1.shape
    per_b = S // tt
    row = lambda i: (i, 0)
    full = lambda i: (0, 0)
    return pl.pallas_call(
        _combine_kernel,
        grid=(N // tt,),
        in_specs=[pl.BlockSpec((SC_SPLIT, MOE_TOPK, tt, SC_PIECE), lambda i: (0, 0, i, 0)),
                  pl.BlockSpec((tt, MOE_TOPK), row),
                  pl.BlockSpec((SC_SPLIT, tt, SC_PIECE), lambda i: (0, i, 0)),
                  pl.BlockSpec((tt, D), row),
                  pl.BlockSpec((None, 1, D), lambda i: (i // per_b, 0, 0)),
                  pl.BlockSpec((1, D), full),
                  pl.BlockSpec(sgu.shape, full),
                  pl.BlockSpec(sd.shape, full)],
        out_specs=pl.BlockSpec((tt, D), row),
        out_shape=jax.ShapeDtypeStruct((N, D), F32),
        compiler_params=_cparams(("parallel",)),
        name="combine",
    )(yg, top_w, h2p, x1, g2.reshape(-1, 1, D), fw.reshape(1, D), sgu, sd)


def _cmp_to_sel_T(n_cp, n_sel):
    c0 = np.arange(n_cp)[None, :] * CMP_STRIDE
    s0 = np.arange(n_sel)[:, None] * SEL_BLOCK
    ov = np.minimum(c0 + CMP_BLOCK, s0 + SEL_BLOCK) - np.maximum(c0, s0)
    m = np.clip(ov, 0, None).astype(np.float32) / CMP_BLOCK
    m[:, n_cp - 1] = 0.0
    return jnp.asarray(m, BF16)


def _slab_weights(w_in):
    D = w_in.shape[0]
    sizes = [1024, 1024, 1024, 512, 128, 128, 128, 128, 128, 128, 24, 2048]
    offs = np.concatenate([[0], np.cumsum(sizes)])
    part = lambda i: w_in[:, offs[i]:offs[i + 1]]
    pieces = [part(0), part(1), part(2), part(11), part(3)] + [part(i) for i in range(4, 10)] + [part(10)]
    w = jnp.concatenate(pieces, axis=1)
    return jnp.pad(w, ((0, 0), (0, SLAB_COLS - w.shape[1]))).astype(BF16)


def _layer(x, c, ada_w, ada_b, norm1_w, w_in, da_lq1, da_lk1, da_lq2, da_lk2, da_subln_w,
           cmp_k_pe, cmp_k_w1, cmp_k_w2, cmp_v_pe, cmp_v_w1, cmp_v_w2, w_da_out, w_nsa_out, w_o,
           norm2_w, router_w, router_b, exp_w_gate, exp_w_up, exp_w_down,
           sh_w_gate, sh_w_up, sh_w_down, final_norm_w, lam_init):
    B, S, D = x.shape
    N = B * S
    G, HPG = NSA_GROUPS, NSA_HPG
    x2 = x.reshape(N, D)

    mod = _ada(c, ada_w, ada_b)
    sh1, sc1, g1, sh2, sc2, g2 = jnp.split(mod, 6, axis=-1)

    slab = _inproj(x2, norm1_w, sh1, sc1, _slab_weights(w_in), S)
    slab3 = slab.reshape(B, S, SLAB_COLS)

    i_all = np.arange(1, DA_HEADS + NSA_HEADS + 1, dtype=np.float32)
    slopes = (2.0 ** (-8.0 * i_all / (DA_HEADS + NSA_HEADS))).astype(np.float32)
    slopes_a = jnp.asarray(slopes[0::2])
    slopes_b = jnp.asarray(slopes[1::2])

    ta_q, ta_k = min(512, S), min(512, S)
    qT = (slab3[:, :, COL_DAQ:COL_DAQ + 1024] * jnp.asarray(DA_DIM ** -0.5, BF16)).transpose(0, 2, 1)
    vT = slab3[:, :, COL_DAV:COL_DAV + 1024].reshape(B, S // ta_k, ta_k, DA_HEADS, DA_VDIM).transpose(0, 3, 1, 4, 2)
    kmin = _first_live_tile(slopes[0::2], slab3, ta_q, ta_k)
    oaT = _diff_attention(slopes_a, kmin, qT, slab3, _with_ones_row(vT), _slope_rows(slopes[0::2]), da_lq1, da_lk1,
                          da_lq2, da_lk2, da_subln_w, lam_init, ta_q, ta_k)
    oa = oaT.transpose(0, 2, 1).reshape(N, DA_HEADS * DA_VDIM)

    tb, tb_k = 256, min(256, S)
    n_cp = S // CMP_STRIDE
    n_sel = S // SEL_BLOCK

    def kv_groups(j):
        c0 = COL_KV6 + 128 * j
        return slab3[:, :, c0:c0 + 128].reshape(B, S, G, NSA_DIM).transpose(0, 2, 1, 3)

    def kv_tiles_T(a, rows):
        return a.reshape(B, G, S // rows, rows, NSA_DIM).transpose(0, 1, 2, 4, 3)

    ck_rows = kv_groups(0).reshape(B * G, n_cp, CMP_STRIDE * NSA_DIM)
    cv_rows = kv_groups(1).reshape(B * G, n_cp, CMP_STRIDE * NSA_DIM)
    kc = _compress(ck_rows, cmp_k_pe, cmp_k_w1, cmp_k_w2).reshape(B, G, n_cp, NSA_DIM).astype(BF16)
    vc = _compress(cv_rows, cmp_v_pe, cmp_v_w1, cmp_v_w2).reshape(B, G, n_cp, NSA_DIM).astype(BF16)
    nq = (slab3[:, :, COL_NSAQ:COL_NSAQ + 512] * jnp.asarray(NSA_DIM ** -0.5, BF16))
    nqT = nq.reshape(B, S // tb, tb, G, HPG, NSA_DIM).transpose(0, 3, 1, 5, 4, 2).reshape(B, G, S // tb, NSA_DIM, HPG * tb)
    gT = (slab3[:, :, COL_NSAG:COL_NSAG + 3 * NSA_HEADS].astype(F32).reshape(B, S // tb, tb, G, HPG, 3)
          .transpose(0, 3, 1, 5, 4, 2).reshape(B, G, S // tb, 3, HPG * tb))
    onehot = jnp.asarray((np.arange(S)[:, None] // SEL_BLOCK == np.arange(SEL_COLS)[None, :]).astype(np.float32), BF16)
    zeros64 = jnp.zeros((B, G, S, NSA_DIM), BF16)
    ksa = jnp.concatenate([kv_groups(2), zeros64, jnp.broadcast_to(onehot, (B, G, S, SEL_COLS))], axis=-1)
    pad_rows = jnp.zeros((B, G, WINDOW, 2 * NSA_DIM), BF16).at[..., NSA_DIM].set(1.0)
    kwa = jnp.concatenate([pad_rows, jnp.concatenate([kv_groups(4), zeros64], axis=-1)], axis=2)
    vw_pad = jnp.pad(kv_groups(5), ((0, 0), (0, 0), (WINDOW, 0), (0, 0)))
    vwT = vw_pad.reshape(B, G, (S + WINDOW) // tb, tb, NSA_DIM).transpose(0, 1, 2, 4, 3)
    r_w = np.arange(WINDOW + tb)[:, None]
    lane_w = np.arange(HPG * tb)[None, :]
    d_w = (WINDOW + lane_w % tb - r_w).astype(np.float32)
    slope_w = slopes[1::2].reshape(G, 1, HPG)[:, :, lane_w[0] // tb]
    wb = jnp.asarray(np.where((d_w >= 0) & (d_w < WINDOW), -slope_w * d_w[None], np.float32(NEG)).astype(np.float32))
    obT = _nsa_attention(slopes_b, nqT, kc, vc.transpose(0, 1, 3, 2), ksa, kv_tiles_T(kv_groups(3), tb_k),
                         kwa, vwT, wb, gT, _cmp_to_sel_T(n_cp, n_sel), tb, tb_k)
    ob = (obT.reshape(B, G, S // tb, NSA_DIM, HPG, tb).transpose(0, 2, 5, 1, 4, 3)
          .reshape(N, NSA_HEADS * NSA_DIM))

    x1, h2, logits = _merge(oa, ob, slab, x2, g1, norm2_w, sh2, sc2, w_da_out.astype(BF16),
                            w_nsa_out.astype(BF16), w_o.astype(BF16), router_w.astype(F32), S)

    rb = 512
    tt = 128
    top_eT, top_wT = _route(logits.T, router_b)
    destT, counts = _slots(top_eT, rb)
    n_rows = ((N * MOE_TOPK + N_EXPERTS * (rb - 1) + rb - 1) // rb) * rb
    padded = (jnp.ceil(counts[:, 0] / rb) * rb).astype(I32)
    pend = jnp.cumsum(padded)
    blk_start = jnp.arange(n_rows // rb, dtype=I32) * rb
    blk_e = jnp.minimum(jnp.sum((pend[None, :] <= blk_start[:, None]).astype(I32), axis=1), N_EXPERTS - 1)
    d_flat = destT.reshape(-1)
    piece_idx = jnp.concatenate([d_flat + s * n_rows for s in range(SC_SPLIT)])
    xs2d = _sc_scatter_rows(h2.reshape(SC_SPLIT * N, SC_PIECE), piece_idx, SC_SPLIT * n_rows, MOE_TOPK)
    cnt = counts[:, 0].astype(I32)
    pad_start = jnp.concatenate([pend - padded + cnt, pend[-1:]])
    pad_len = jnp.concatenate([padded - cnt, n_rows - pend[-1:]])
    xs = _zero_pads(xs2d, pad_start, pad_len, rb, n_rows).reshape(SC_SPLIT, n_rows, SC_PIECE)
    ys = _experts(blk_e, xs, exp_w_gate, exp_w_up, exp_w_down, rb)
    sgu = jnp.concatenate([sh_w_gate, sh_w_up], axis=-1).astype(BF16)
    yg = _sc_gather_rows(ys.reshape(SC_SPLIT * n_rows, SC_PIECE), piece_idx).reshape(SC_SPLIT, MOE_TOPK, N, SC_PIECE)
    out = _combine(yg, top_wT.T, h2, x1, g2, final_norm_w, sgu, sh_w_down.astype(BF16), S, 2 * tt)
    return out.reshape(B, S, D)


def kernel(x, c, ada_w, ada_b, norm1_w, w_in, da_lq1, da_lk1, da_lq2, da_lk2, da_subln_w, cmp_k_pe, cmp_k_w1, cmp_k_w2, cmp_v_pe, cmp_v_w1, cmp_v_w2, w_da_out, w_nsa_out, w_o, norm2_w, router_w, router_b, exp_w_gate, exp_w_up, exp_w_down, sh_w_gate, sh_w_up, sh_w_down, final_norm_w):
    depth = ada_w.shape[0]
    assert depth == 1, "one decoder layer"
    lam_init = 0.8 - 0.6 * math.exp(-0.3 * 0)
    return _layer(x, c, ada_w[0], ada_b[0], norm1_w[0], w_in[0], da_lq1[0], da_lk1[0], da_lq2[0], da_lk2[0],
                  da_subln_w[0], cmp_k_pe[0], cmp_k_w1[0], cmp_k_w2[0], cmp_v_pe[0], cmp_v_w1[0], cmp_v_w2[0],
                  w_da_out[0], w_nsa_out[0], w_o[0], norm2_w[0], router_w[0], router_b[0],
                  exp_w_gate[0], exp_w_up[0], exp_w_down[0], sh_w_gate[0], sh_w_up[0], sh_w_down[0],
                  final_norm_w, lam_init)
```

```python
import functools
import math

import numpy as np
import jax
import jax.numpy as jnp
from jax import lax
from jax.experimental import pallas as pl
from jax.experimental.pallas import tpu as pltpu
from jax.experimental.pallas import tpu_sc as plsc

F32 = jnp.float32
BF16 = jnp.bfloat16
I32 = jnp.int32
U32 = jnp.uint32
HIGHEST = lax.Precision.HIGHEST

NORM_EPS = 1e-6
NEG = -1e30

DA_HEADS = 8
DA_DIM = 64
DA_VDIM = 128
DA_SUBLN_EPS = 1e-5

NSA_HEADS = 8
NSA_GROUPS = 2
NSA_HPG = 4
NSA_DIM = 64
CMP_BLOCK = 32
CMP_STRIDE = 16
SEL_BLOCK = 64
SEL_TOPK = 16
WINDOW = 512
FORCE_BONUS = 1e4
SEL_COLS = 128

N_EXPERTS = 64
MOE_TOPK = 8
N_EXPERT_GROUPS = 8
TOPK_EXPERT_GROUPS = 4
ROUTED_SCALE = 2.5

VMEM_LIMIT_V7X = 56 * 1024 * 1024

COL_DAQ, COL_DAK, COL_DAV = 0, 1024, 2048
COL_MERGE = 3072
COL_NSAQ = 5120
COL_KV6 = 5632
COL_NSAG = 6400
SLAB_COLS = 6528


SUBLANES = 8
SC_WINDOW = 128
SC_SPLIT = 2
SC_PIECE = 256

POS_SPLIT = 16
POS_ROWS = 16
ONES_ROWS = 16


def _pos_features(rows):
    r = np.arange(rows)
    f = np.zeros((rows, 128), np.float32)
    f[:, 0:3] = (r // POS_SPLIT)[:, None]
    f[:, 3:6] = (r % POS_SPLIT)[:, None]
    return jnp.asarray(f, BF16)


def _slope_rows(slopes):
    s = np.asarray(slopes, np.float32)
    bf = lambda x: x.astype(BF16).astype(np.float32)
    p1 = bf(s)
    p2 = bf(s - p1)
    p3 = bf(s - p1 - p2)
    out = np.zeros((s.shape[0], POS_ROWS, 128), np.float32)
    for i, piece in enumerate((p1, p2, p3)):
        out[:, i, :] = POS_SPLIT * piece[:, None]
        out[:, 3 + i, :] = piece[:, None]
    return jnp.asarray(out)


def _with_ones_row(vT):
    shape = vT.shape[:-2] + (ONES_ROWS, vT.shape[-1])
    extra = jnp.zeros(shape, vT.dtype).at[..., 0, :].set(1.0)
    return jnp.concatenate([vT, extra], axis=-2)


def _cparams(sem, vmem=VMEM_LIMIT_V7X):
    return pltpu.CompilerParams(dimension_semantics=sem, vmem_limit_bytes=vmem)


def _ada_kernel(c_ref, w_ref, b_ref, o_ref):
    c = c_ref[...]
    ca = c * jax.nn.sigmoid(c)
    o_ref[...] = jnp.dot(ca, w_ref[...], preferred_element_type=F32, precision=HIGHEST) + b_ref[...]


def _ada(c, w, b):
    B, D = c.shape
    n_out = w.shape[1]
    rows = 8
    cp = jnp.zeros((rows, D), F32).at[:B].set(c)
    tn = 1024
    out = pl.pallas_call(
        _ada_kernel,
        grid=(n_out // tn,),
        in_specs=[pl.BlockSpec((rows, D), lambda j: (0, 0)),
                  pl.BlockSpec((D, tn), lambda j: (0, j)),
                  pl.BlockSpec((1, tn), lambda j: (0, j))],
        out_specs=pl.BlockSpec((rows, tn), lambda j: (0, j)),
        out_shape=jax.ShapeDtypeStruct((rows, n_out), F32),
        compiler_params=_cparams(("arbitrary",)),
        name="ada",
    )(cp, w, b.reshape(1, n_out))
    return out[:B]


def _modulate(x, nw, sh, sc):
    r = lax.rsqrt(jnp.mean(x * x, axis=-1, keepdims=True) + NORM_EPS)
    return (x * r) * nw * (1.0 + sc) + sh


def _inproj_kernel(x_ref, nw_ref, sh_ref, sc_ref, w_ref, o_ref, h_ref):
    @pl.when(pl.program_id(1) == 0)
    def _():
        h_ref[...] = _modulate(x_ref[...], nw_ref[...], sh_ref[...], sc_ref[...]).astype(BF16)

    o_ref[...] = jnp.dot(h_ref[...], w_ref[...], preferred_element_type=F32).astype(BF16)


def _inproj(x2, nw, sh, sc, w_slab, S):
    N, D = x2.shape
    tm = min(1024, S)
    tn = SLAB_COLS // 3
    per_b = S // tm
    return pl.pallas_call(
        _inproj_kernel,
        grid=(N // tm, SLAB_COLS // tn),
        in_specs=[pl.BlockSpec((tm, D), lambda i, j: (i, 0)),
                  pl.BlockSpec((1, D), lambda i, j: (0, 0)),
                  pl.BlockSpec((None, 1, D), lambda i, j: (i // per_b, 0, 0)),
                  pl.BlockSpec((None, 1, D), lambda i, j: (i // per_b, 0, 0)),
                  pl.BlockSpec((D, tn), lambda i, j: (0, j))],
        out_specs=pl.BlockSpec((tm, tn), lambda i, j: (i, j)),
        out_shape=jax.ShapeDtypeStruct((N, SLAB_COLS), BF16),
        scratch_shapes=[pltpu.VMEM((tm, D), BF16)],
        compiler_params=_cparams(("parallel", "arbitrary")),
        name="inproj",
    )(x2, nw.reshape(1, D), sh.reshape(-1, 1, D), sc.reshape(-1, 1, D), w_slab)


def _cmp_kernel(r_ref, pe_ref, w1a_ref, w1b_ref, w2_ref, o_ref):
    r = r_ref[...]
    n_rows = r.shape[0]
    a = jnp.dot(r, w1a_ref[...], preferred_element_type=F32)
    b = jnp.dot(r, w1b_ref[...], preferred_element_type=F32)
    pe = pe_ref[...]
    half = pe.shape[1] // 2
    bias = (jnp.dot(pe[:, :half], w1a_ref[...].astype(F32), preferred_element_type=F32, precision=HIGHEST)
            + jnp.dot(pe[:, half:], w1b_ref[...].astype(F32), preferred_element_type=F32, precision=HIGHEST))
    hid = a + pltpu.roll(b, n_rows - 1, 0) + bias
    act = 0.5 * hid * (1.0 + jnp.tanh(0.7978845608028654 * (hid + 0.044715 * hid * hid * hid)))
    o_ref[...] = jnp.dot(act, w2_ref[...], preferred_element_type=F32, precision=HIGHEST)


def _compress(rows, pe, w1, w2):
    BG, n_rows, width = rows.shape
    hidden = w1.shape[1]
    w1a = w1[:width].astype(BF16)
    w1b = w1[width:].astype(BF16)
    return pl.pallas_call(
        _cmp_kernel,
        grid=(BG,),
        in_specs=[pl.BlockSpec((None, n_rows, width), lambda i: (i, 0, 0)),
                  pl.BlockSpec((1, 2 * width), lambda i: (0, 0)),
                  pl.BlockSpec((width, hidden), lambda i: (0, 0)),
                  pl.BlockSpec((width, hidden), lambda i: (0, 0)),
                  pl.BlockSpec((hidden, NSA_DIM), lambda i: (0, 0))],
        out_specs=pl.BlockSpec((None, n_rows, NSA_DIM), lambda i: (i, 0, 0)),
        out_shape=jax.ShapeDtypeStruct((BG, n_rows, NSA_DIM), F32),
        compiler_params=_cparams(("parallel",)),
        name="cmp",
    )(rows, pe.reshape(1, -1).astype(F32), w1a, w1b, w2.astype(F32))


def _da_kernel(slopes_ref, kmin_ref, qT_ref, k_ref, vT_ref, pos_ref, srow_ref, lq1_ref, lk1_ref, lq2_ref, lk2_ref, subw_ref,
               oT_ref, acc_ref, qa_ref, sa_ref, sb_ref, pa_ref, pb_ref, rel_ref, *, tq, tk, lam_init):
    h = pl.program_id(1)
    qi = pl.program_id(2)
    slope = slopes_ref[h]
    q0 = qi * tq
    w = 2 * tq
    nk = k_ref.shape[0] // tk
    qT = qT_ref[...]
    row = lax.broadcasted_iota(I32, qT.shape, 0)
    zero = jnp.zeros_like(qT)
    qa_ref[0:2 * DA_DIM, :] = jnp.concatenate(
        [jnp.where(row < DA_DIM, qT, zero), jnp.where(row >= DA_DIM, qT, zero)], axis=1)
    qa_ref[2 * DA_DIM:2 * DA_DIM + POS_ROWS, :] = jnp.concatenate([srow_ref[...]] * (w // 128), axis=1).astype(BF16)
    qa_ref[2 * DA_DIM + POS_ROWS:, :] = jnp.zeros((2 * DA_DIM - POS_ROWS, w), BF16)
    lane = lax.broadcasted_iota(I32, (tk, w), 1)
    rel_ref[...] = lax.broadcasted_iota(I32, (tk, w), 0) - jnp.where(lane >= tq, lane - tq, lane)

    def scores(kt, s_ref):
        k0 = kt * tk
        k_t = k_ref[pl.ds(pl.multiple_of(k0, tk), tk), :]
        s = jnp.dot(jnp.concatenate([k_t, pos_ref[...]], axis=1), qa_ref[...], preferred_element_type=F32)
        s = jnp.where(rel_ref[...] <= q0 - k0, s, NEG)
        s_ref[...] = s
        return jnp.max(s, axis=0, keepdims=True)

    def probs(s_ref, p_ref, mx, off, m_old):
        m_new = jnp.maximum(m_old, mx + off)
        p_ref[...] = jnp.exp(s_ref[...] - (m_new - off)).astype(BF16)
        return m_new, jnp.exp(m_old - m_new)

    def accumulate(kt, p_ref, alpha):
        acc_ref[...] = alpha * acc_ref[...] + jnp.dot(vT_ref[kt], p_ref[...], preferred_element_type=F32)

    n_tiles = (q0 + tq - 1) // tk + 1
    kt_min = kmin_ref[(pl.program_id(0) * pl.num_programs(1) + h) * pl.num_programs(2) + qi]
    acc_ref[...] = jnp.zeros_like(acc_ref)
    pb_ref[...] = jnp.zeros_like(pb_ref)
    mx_a = scores(kt_min, sa_ref)

    def pair(j, carry):
        m, mx_a, alpha_b = carry
        ta = kt_min + 2 * j
        tb = ta + 1
        off_a = slope * (ta * tk).astype(F32)
        off_b = jnp.where(tb < n_tiles, slope * (tb * tk).astype(F32), NEG)
        accumulate(jnp.maximum(ta - 1, 0), pb_ref, alpha_b)
        m, alpha_a = probs(sa_ref, pa_ref, mx_a, off_a, m)
        mx_b = scores(jnp.minimum(tb, nk - 1), sb_ref)
        accumulate(ta, pa_ref, alpha_a)
        m, alpha_b = probs(sb_ref, pb_ref, mx_b, off_b, m)
        mx_a = scores(jnp.minimum(ta + 2, nk - 1), sa_ref)
        return m, mx_a, alpha_b

    n_pairs = (n_tiles - kt_min + 1) // 2
    init = (jnp.full((1, w), NEG, F32), mx_a, jnp.ones((1, w), F32))
    _, _, alpha_b = lax.fori_loop(0, n_pairs, pair, init)
    accumulate(jnp.minimum(kt_min + 2 * n_pairs - 1, nk - 1), pb_ref, alpha_b)

    lam = (jnp.exp(jnp.sum(lq1_ref[...] * lk1_ref[...], axis=-1, keepdims=True))
           - jnp.exp(jnp.sum(lq2_ref[...] * lk2_ref[...], axis=-1, keepdims=True)) + lam_init)
    on = acc_ref[0:DA_VDIM, :] * (1.0 / acc_ref[DA_VDIM:DA_VDIM + 1, :])
    o = on[:, :tq] - lam * on[:, tq:]
    r = lax.rsqrt(jnp.mean(o * o, axis=0, keepdims=True) + DA_SUBLN_EPS)
    oT_ref[...] = (o * r * subw_ref[...] * (1.0 - lam_init)).astype(BF16)


EXP_UNDERFLOW = 104.0


def _first_live_tile(slopes_np, slab3, tq, tk):
    B, S, _ = slab3.shape
    H = DA_HEADS
    width = 2 * DA_DIM * H
    grp = jnp.asarray((np.arange(width)[:, None] // DA_DIM == np.arange(2 * H)[None, :]).astype(np.float32), BF16)

    def norms(c0):
        v = slab3[:, :, c0:c0 + width]
        return jnp.sqrt(jnp.einsum('bsc,cg->bsg', v * v, grp, preferred_element_type=F32))

    qn = norms(COL_DAQ) * DA_DIM ** -0.5
    qn = jnp.max(qn.reshape(B, S // tq, tq, H, 2), axis=2).transpose(0, 2, 3, 1)
    kn = jnp.max(norms(COL_DAK), axis=1).reshape(B, H, 2)
    bound = jnp.max(qn * kn[..., None], axis=2) * 1.01 + 0.01
    q0 = (jnp.arange(S // tq, dtype=F32) * tq)[None, None, :]
    slope = jnp.asarray(slopes_np, F32)[None, :, None]
    first_key = q0 - (EXP_UNDERFLOW + 2.0 * bound) / slope
    kt = jnp.floor(first_key / tk).astype(I32)
    n_tiles = (jnp.arange(S // tq, dtype=I32) * tq + tq - 1) // tk + 1
    return jnp.clip(kt, 0, n_tiles[None, None, :] - 1).reshape(-1)


def _diff_attention(slopes, kmin, qT, slab3, vT, srows, lq1, lk1, lq2, lk2, subw, lam_init, tq, tk):
    B, _, S = qT.shape
    nk = S // tk
    assert tk % tq == 0 and S % tk == 0
    tile = tq
    kcol = COL_DAK // 128
    v_rows = vT.shape[3]
    vec = lambda a: a.reshape(1, DA_DIM).astype(F32)
    grid_spec = pltpu.PrefetchScalarGridSpec(
        num_scalar_prefetch=2,
        grid=(B, DA_HEADS, S // tile),
        in_specs=[pl.BlockSpec((None, 128, tile), lambda b, h, i, s, km: (b, h, i)),
                  pl.BlockSpec((None, S, 128), lambda b, h, i, s, km: (b, 0, kcol + h)),
                  pl.BlockSpec((None, None, nk, v_rows, tk), lambda b, h, i, s, km: (b, h, 0, 0, 0)),
                  pl.BlockSpec((tk, 128), lambda b, h, i, s, km: (0, 0)),
                  pl.BlockSpec((None, POS_ROWS, 128), lambda b, h, i, s, km: (h, 0, 0)),
                  pl.BlockSpec((1, DA_DIM), lambda b, h, i, s, km: (0, 0)),
                  pl.BlockSpec((1, DA_DIM), lambda b, h, i, s, km: (0, 0)),
                  pl.BlockSpec((1, DA_DIM), lambda b, h, i, s, km: (0, 0)),
                  pl.BlockSpec((1, DA_DIM), lambda b, h, i, s, km: (0, 0)),
                  pl.BlockSpec((DA_VDIM, 1), lambda b, h, i, s, km: (0, 0))],
        out_specs=pl.BlockSpec((None, 128, tile), lambda b, h, i, s, km: (b, h, i)),
        scratch_shapes=[pltpu.VMEM((v_rows, 2 * tq), F32), pltpu.VMEM((4 * DA_DIM, 2 * tq), BF16),
                        pltpu.VMEM((tk, 2 * tq), F32), pltpu.VMEM((tk, 2 * tq), F32),
                        pltpu.VMEM((tk, 2 * tq), BF16), pltpu.VMEM((tk, 2 * tq), BF16),
                        pltpu.VMEM((tk, 2 * tq), I32)],
    )
    return pl.pallas_call(
        functools.partial(_da_kernel, tq=tq, tk=tk, lam_init=lam_init),
        grid_spec=grid_spec,
        out_shape=jax.ShapeDtypeStruct((B, DA_HEADS * DA_VDIM, S), BF16),
        compiler_params=_cparams(("parallel", "parallel", "arbitrary")),
        name="diffattn",
    )(slopes, kmin, qT, slab3, vT, _pos_features(tk), srows, vec(lq1), vec(lk1), vec(lq2), vec(lk2),
      subw.reshape(DA_VDIM, 1).astype(F32))


def _nsa_kernel(slopes_ref, qT_ref, kc_ref, vcT_ref, ksa_ref, vsT_ref, kwa_ref, vwT_ref, wb_ref, gT_ref, mT_ref,
                grp_ref, ltri_ref, oT_ref, acc_ref, res_ref, base_ref, qa_ref, sa_ref, sb_ref, pa_ref, pb_ref,
                list_ref, slist_ref, lsem, *, t, tk, n_top):
    g = pl.program_id(1)
    qi = pl.program_id(2)
    q0 = qi * t
    w = NSA_HPG * t
    n_cp = kc_ref.shape[0]
    n_sel = mT_ref.shape[0]
    lane1 = lax.broadcasted_iota(I32, (1, w), 1)
    slope_row = jnp.zeros((1, w), F32)
    for hh in range(NSA_HPG):
        slope_row = jnp.where((lane1 >= hh * t) & (lane1 < (hh + 1) * t), slopes_ref[g * NSA_HPG + hh], slope_row)
    q_all = qT_ref[...]
    gates = jax.nn.sigmoid(gT_ref[...])
    key_off = lax.broadcasted_iota(I32, (tk, w), 0)
    t_loc = jnp.bitwise_and(lax.broadcasted_iota(I32, (tk, w), 1), t - 1)
    base_ref[...] = slope_row * key_off.astype(F32)

    c_idx = lax.broadcasted_iota(I32, (n_cp, w), 0)
    t_c = q0 + jnp.bitwise_and(lax.broadcasted_iota(I32, (n_cp, w), 1), t - 1)
    d_c = t_c - (c_idx * CMP_STRIDE + (CMP_BLOCK - 1))
    ok_c = d_c >= 0
    s = jnp.dot(kc_ref[...], q_all, preferred_element_type=F32) - slope_row * d_c.astype(F32)
    s = jnp.where(ok_c, s, NEG)
    e = jnp.exp(s - jnp.max(s, axis=0, keepdims=True))
    inv = 1.0 / jnp.sum(e, axis=0, keepdims=True)
    p = jnp.where(ok_c, e * inv, 0.0)
    res_ref[...] = gates[0:1, :] * jnp.dot(vcT_ref[...], p.astype(BF16), preferred_element_type=F32)
    p_sum = p[:, 0:t]
    for hh in range(1, NSA_HPG):
        p_sum = p_sum + p[:, hh * t:(hh + 1) * t]

    p_hi = p_sum.astype(BF16)
    p_lo = (p_sum - p_hi.astype(F32)).astype(BF16)
    mT = mT_ref[...]
    imp = jnp.dot(mT, p_hi, preferred_element_type=F32) + jnp.dot(mT, p_lo, preferred_element_type=F32)
    blk = lax.broadcasted_iota(I32, (n_sel, t), 0)
    t_s = q0 + lax.broadcasted_iota(I32, (n_sel, t), 1)
    cur = jnp.right_shift(t_s, int(math.log2(SEL_BLOCK)))
    forced = (blk == 0) | (blk == cur) | (blk == cur - 1)
    score = jnp.where(blk * SEL_BLOCK <= t_s, imp + jnp.where(forced, FORCE_BONUS, 0.0), NEG)
    blk_f = blk.astype(F32)
    sel = jnp.zeros((n_sel, t), F32)
    for _ in range(n_top):
        _, idx = _first_max(score, blk_f)
        hit = blk_f == idx
        sel = jnp.where(hit, 1.0, sel)
        score = jnp.where(hit, -jnp.inf, score)
    n_full = q0 // tk
    nt_pad = grp_ref.shape[0]
    cnt = jnp.broadcast_to(jnp.sum(sel, axis=1, keepdims=True), (n_sel, 128)).astype(BF16)
    tile_cnt = jnp.dot(grp_ref[...], cnt, preferred_element_type=F32)
    kt_i = lax.broadcasted_iota(I32, (nt_pad, 128), 0)
    active = (tile_cnt > 0.5) & (kt_i < n_full)
    act_f = jnp.where(active, 1.0, 0.0)
    before = jnp.dot(ltri_ref[...], act_f.astype(BF16), preferred_element_type=F32)
    lane_j = lax.broadcasted_iota(I32, (nt_pad, 128), 1)
    slot_hit = active & (before == lane_j.astype(F32))
    tiles_row = jnp.sum(jnp.where(slot_hit, kt_i.astype(F32), 0.0), axis=0, keepdims=True)
    n_act_row = jnp.sum(act_f, axis=0, keepdims=True)
    lane8 = lax.broadcasted_iota(I32, (8, 128), 1)
    list_ref[...] = jnp.where(lane8 == 127, n_act_row, tiles_row).astype(I32)
    list_copy = pltpu.make_async_copy(list_ref, slist_ref, lsem)
    list_copy.start()

    selb = jnp.where(sel > 0.5, 0.0, NEG)
    row64 = lax.broadcasted_iota(I32, (NSA_DIM, w), 0)
    qa_ref[0:NSA_DIM, :] = q_all
    qa_ref[NSA_DIM:2 * NSA_DIM, :] = jnp.where(row64 == 0, NEG, 0.0).astype(BF16)
    qa_ref[2 * NSA_DIM:2 * NSA_DIM + n_sel, :] = jnp.concatenate([selb] * NSA_HPG, axis=1).astype(BF16)
    if n_sel < SEL_COLS:
        qa_ref[2 * NSA_DIM + n_sel:, :] = jnp.zeros((SEL_COLS - n_sel, w), BF16)

    n_wt = (WINDOW + t) // t
    kw_t = kwa_ref[pl.ds(pl.multiple_of(q0, t), WINDOW + t), :]
    s = jnp.dot(kw_t, qa_ref[0:2 * NSA_DIM, :], preferred_element_type=F32) + wb_ref[...]
    p = jnp.exp(s - jnp.max(s, axis=0, keepdims=True))
    inv = 1.0 / jnp.sum(p, axis=0, keepdims=True)
    p = p.astype(BF16)
    o_w = jnp.dot(vwT_ref[qi], p[0:t, :], preferred_element_type=F32)
    for c in range(1, n_wt):
        o_w = o_w + jnp.dot(vwT_ref[qi + c], p[c * t:(c + 1) * t, :], preferred_element_type=F32)
    res_ref[...] = res_ref[...] + gates[2:3, :] * (o_w * inv)

    def scores(kt, s_ref):
        k_t = ksa_ref[pl.ds(pl.multiple_of(kt * tk, tk), tk), :]
        s = jnp.dot(k_t, qa_ref[...], preferred_element_type=F32) + base_ref[...]
        s_ref[...] = s
        return jnp.max(s, axis=0, keepdims=True)

    def probs(s_ref, p_ref, mx, off, m_old, l_old):
        m_new = jnp.maximum(m_old, mx + off)
        p = jnp.exp(s_ref[...] - (m_new - off))
        alpha = jnp.exp(m_old - m_new)
        p_ref[...] = p.astype(BF16)
        return m_new, alpha * l_old + jnp.sum(p, axis=0, keepdims=True), alpha

    def accumulate(kt, p_ref, alpha):
        acc_ref[...] = alpha * acc_ref[...] + jnp.dot(vsT_ref[kt], p_ref[...], preferred_element_type=F32)

    list_copy.wait()
    n_act = slist_ref[0, 127]
    mx_a = scores(slist_ref[0, 0], sa_ref)

    kd = n_full * tk
    s = jnp.dot(ksa_ref[pl.ds(pl.multiple_of(kd, tk), tk), :], qa_ref[...], preferred_element_type=F32) + base_ref[...]
    s = jnp.where(key_off - t_loc <= q0 - kd, s, NEG)
    off_d = slope_row * kd.astype(F32)
    m = jnp.max(s, axis=0, keepdims=True) + off_d
    p = jnp.exp(s - (m - off_d))
    l = jnp.sum(p, axis=0, keepdims=True)
    acc_ref[...] = jnp.dot(vsT_ref[n_full], p.astype(BF16), preferred_element_type=F32)
    pb_ref[...] = jnp.zeros_like(pb_ref)

    def pair(j, carry):
        m, l, mx_a, alpha_b, tb_prev = carry
        ta = slist_ref[0, 2 * j]
        tb = slist_ref[0, 2 * j + 1]
        off_a = slope_row * (ta * tk).astype(F32)
        off_b = jnp.where(2 * j + 1 < n_act, slope_row * (tb * tk).astype(F32), NEG)
        accumulate(tb_prev, pb_ref, alpha_b)
        m, l, alpha_a = probs(sa_ref, pa_ref, mx_a, off_a, m, l)
        mx_b = scores(tb, sb_ref)
        accumulate(ta, pa_ref, alpha_a)
        m, l, alpha_b = probs(sb_ref, pb_ref, mx_b, off_b, m, l)
        mx_a = scores(slist_ref[0, 2 * j + 2], sa_ref)
        return m, l, mx_a, alpha_b, tb

    n_pairs = (n_act + 1) // 2
    m, l, _, alpha_b, tb_last = lax.fori_loop(0, n_pairs, pair, (m, l, mx_a, jnp.ones((1, w), F32), jnp.int32(0)))
    accumulate(tb_last, pb_ref, alpha_b)
    oT_ref[...] = (res_ref[...] + gates[1:2, :] * (acc_ref[...] * (1.0 / l))).astype(BF16)


def _nsa_attention(slopes, qT, kc, vcT, ksa, vsT, kwa, vwT, wb, gT, mT, t, tk):
    B, G, nq, _, w = qT.shape
    S = nq * t
    n_cp = kc.shape[2]
    n_sel = S // SEL_BLOCK
    n_top = min(SEL_TOPK, n_sel)
    assert tk % t == 0 and S % tk == 0 and t & (t - 1) == 0 and WINDOW % t == 0 and n_sel <= SEL_COLS
    n_tiles = S // tk
    nt_pad = -(-n_tiles // 16) * 16
    assert nt_pad + 2 < 127
    grp = jnp.asarray((np.arange(n_sel)[None, :] // (tk // SEL_BLOCK) == np.arange(nt_pad)[:, None])
                      .astype(np.float32), BF16)
    ltri = jnp.asarray((np.arange(nt_pad)[None, :] < np.arange(nt_pad)[:, None]).astype(np.float32), BF16)
    fixed = lambda b, g, i, s: (b, g, 0, 0)
    fixed5 = lambda b, g, i, s: (b, g, 0, 0, 0)
    tile5 = lambda b, g, i, s: (b, g, i, 0, 0)
    grid_spec = pltpu.PrefetchScalarGridSpec(
        num_scalar_prefetch=1,
        grid=(B, G, nq),
        in_specs=[pl.BlockSpec((None, None, None, NSA_DIM, w), tile5),
                  pl.BlockSpec((None, None, n_cp, NSA_DIM), fixed),
                  pl.BlockSpec((None, None, NSA_DIM, n_cp), fixed),
                  pl.BlockSpec((None, None, S, ksa.shape[-1]), fixed),
                  pl.BlockSpec((None, None, S // tk, NSA_DIM, tk), fixed5),
                  pl.BlockSpec((None, None, S + WINDOW, 2 * NSA_DIM), fixed),
                  pl.BlockSpec((None, None, (S + WINDOW) // t, NSA_DIM, t), fixed5),
                  pl.BlockSpec((None, WINDOW + t, w), lambda b, g, i, s: (g, 0, 0)),
                  pl.BlockSpec((None, None, None, 3, w), tile5),
                  pl.BlockSpec((n_sel, n_cp), lambda b, g, i, s: (0, 0)),
                  pl.BlockSpec((nt_pad, n_sel), lambda b, g, i, s: (0, 0)),
                  pl.BlockSpec((nt_pad, nt_pad), lambda b, g, i, s: (0, 0))],
        out_specs=pl.BlockSpec((None, None, None, NSA_DIM, w), tile5),
        scratch_shapes=[pltpu.VMEM((NSA_DIM, w), F32),
                        pltpu.VMEM((NSA_DIM, w), F32),
                        pltpu.VMEM((tk, w), F32),
                        pltpu.VMEM((2 * NSA_DIM + SEL_COLS, w), BF16),
                        pltpu.VMEM((tk, w), F32), pltpu.VMEM((tk, w), F32),
                        pltpu.VMEM((tk, w), BF16), pltpu.VMEM((tk, w), BF16),
                        pltpu.VMEM((8, 128), I32), pltpu.SMEM((8, 128), I32), pltpu.SemaphoreType.DMA(())],
    )
    return pl.pallas_call(
        functools.partial(_nsa_kernel, t=t, tk=tk, n_top=n_top),
        grid_spec=grid_spec,
        out_shape=jax.ShapeDtypeStruct((B, G, nq, NSA_DIM, w), BF16),
        compiler_params=_cparams(("parallel", "parallel", "arbitrary")),
        name="nsa",
    )(slopes, qT, kc, vcT, ksa, vsT, kwa, vwT, wb, gT, mT, grp, ltri)


def _pack_halves(v):
    c = v.shape[1] // 2
    lo = lax.bitcast_convert_type(v[:, :c].astype(BF16).astype(F32), U32)
    hi = lax.bitcast_convert_type(v[:, c:].astype(BF16).astype(F32), U32)
    return hi | (lo >> 16)


def _unpack_halves(u):
    lo = lax.bitcast_convert_type(u << 16, F32)
    hi = lax.bitcast_convert_type(u & jnp.uint32(0xFFFF0000), F32)
    return lo, hi


def _store_slabs(ref, v):
    packed = _pack_halves(v)
    for s in range(SC_SPLIT):
        ref[s] = packed[:, s * SC_PIECE:(s + 1) * SC_PIECE]


def _load_slabs_bf16(ref):
    parts = [_unpack_halves(ref[s]) for s in range(SC_SPLIT)]
    return jnp.concatenate([p[0].astype(BF16) for p in parts] + [p[1].astype(BF16) for p in parts], axis=1)


def _merge_kernel(oa_ref, ob_ref, ga_ref, gb_ref, x_ref, g1_ref, nw_ref, sh_ref, sc_ref,
                  wa_ref, wb_ref, wo_ref, wr_ref, x1_ref, h2_ref, lg_ref):
    ya = jnp.dot(oa_ref[...], wa_ref[...], preferred_element_type=F32)
    yb = jnp.dot(ob_ref[...], wb_ref[...], preferred_element_type=F32)
    merged = (jax.nn.sigmoid(ga_ref[...].astype(F32)) * ya
              + jax.nn.sigmoid(gb_ref[...].astype(F32)) * yb).astype(BF16)
    mix = jnp.dot(merged, wo_ref[...], preferred_element_type=F32)
    x1 = x_ref[...] + g1_ref[...] * mix
    x1_ref[...] = x1
    h2 = _modulate(x1, nw_ref[...], sh_ref[...], sc_ref[...])
    _store_slabs(h2_ref, h2)
    h_hi = h2.astype(BF16)
    h_lo = (h2 - h_hi.astype(F32)).astype(BF16)
    lg_ref[...] = (jnp.dot(h_hi, wr_ref[0], preferred_element_type=F32)
                   + jnp.dot(h_lo, wr_ref[0], preferred_element_type=F32)
                   + jnp.dot(h_hi, wr_ref[1], preferred_element_type=F32))


def _merge(oa, ob, slab, x2, g1, nw, sh, sc, wa, wb, wo, wr, S):
    N, D = x2.shape
    tm = 256
    per_b = S // tm
    mcol = COL_MERGE // D
    row = lambda i: (i, 0)
    full = lambda i: (0, 0)
    perb = lambda i: (i // per_b, 0, 0)
    n_lg = wr.shape[-1]
    return pl.pallas_call(
        _merge_kernel,
        grid=(N // tm,),
        in_specs=[pl.BlockSpec((tm, oa.shape[1]), row),
                  pl.BlockSpec((tm, ob.shape[1]), row),
                  pl.BlockSpec((tm, D), lambda i: (i, mcol)),
                  pl.BlockSpec((tm, D), lambda i: (i, mcol + 1)),
                  pl.BlockSpec((tm, D), row),
                  pl.BlockSpec((None, 1, D), perb),
                  pl.BlockSpec((1, D), full),
                  pl.BlockSpec((None, 1, D), perb),
                  pl.BlockSpec((None, 1, D), perb),
                  pl.BlockSpec(wa.shape, full),
                  pl.BlockSpec(wb.shape, full),
                  pl.BlockSpec(wo.shape, full),
                  pl.BlockSpec(wr.shape, lambda i: (0, 0, 0))],
        out_specs=[pl.BlockSpec((tm, D), row), pl.BlockSpec((SC_SPLIT, tm, SC_PIECE), lambda i: (0, i, 0)),
                   pl.BlockSpec((tm, n_lg), row)],
        out_shape=[jax.ShapeDtypeStruct((N, D), F32), jax.ShapeDtypeStruct((SC_SPLIT, N, SC_PIECE), U32),
                   jax.ShapeDtypeStruct((N, n_lg), F32)],
        compiler_params=_cparams(("parallel",)),
        name="merge",
    )(oa, ob, slab, slab, x2, g1.reshape(-1, 1, D), nw.reshape(1, D), sh.reshape(-1, 1, D),
      sc.reshape(-1, 1, D), wa, wb, wo, wr)


def _first_max(v, idx):
    mx = jnp.max(v, axis=0, keepdims=True)
    first = jnp.min(jnp.where(v == mx, idx, float(v.shape[0])), axis=0, keepdims=True)
    return mx, first


def _route_kernel(lg_ref, b_ref, e_ref, w_ref):
    lg = lg_ref[...]
    tn = lg.shape[1]
    per_g = N_EXPERTS // N_EXPERT_GROUPS
    scores = jax.nn.sigmoid(lg)
    biased = scores + b_ref[...]
    gi = lax.broadcasted_iota(I32, (N_EXPERT_GROUPS, tn), 0).astype(F32)
    gscore = jnp.zeros((N_EXPERT_GROUPS, tn), F32)
    for g in range(N_EXPERT_GROUPS):
        grp = biased[g * per_g:(g + 1) * per_g, :]
        m1, f1 = _first_max(grp, gi)
        m2 = jnp.max(jnp.where(gi == f1, -jnp.inf, grp), axis=0, keepdims=True)
        gscore = jnp.where(gi == g, m1 + m2, gscore)
    gsel = jnp.zeros((N_EXPERT_GROUPS, tn), F32)
    for _ in range(TOPK_EXPERT_GROUPS):
        _, f = _first_max(gscore, gi)
        hit = gi == f
        gsel = jnp.where(hit, 1.0, gsel)
        gscore = jnp.where(hit, -jnp.inf, gscore)
    ei = lax.broadcasted_iota(I32, (N_EXPERTS, tn), 0).astype(F32)
    emask = jnp.zeros((N_EXPERTS, tn), F32)
    for g in range(N_EXPERT_GROUPS):
        in_g = (ei >= g * per_g) & (ei < (g + 1) * per_g)
        emask = jnp.where(in_g, gsel[g:g + 1, :], emask)
    masked = jnp.where(emask > 0.5, biased, NEG)
    e_out = jnp.zeros((MOE_TOPK, tn), F32)
    w_out = jnp.zeros((MOE_TOPK, tn), F32)
    for r in range(MOE_TOPK):
        _, f = _first_max(masked, ei)
        hit = ei == f
        wv = jnp.sum(jnp.where(hit, scores, 0.0), axis=0, keepdims=True)
        e_out = jnp.where(gi == r, f, e_out)
        w_out = jnp.where(gi == r, wv, w_out)
        masked = jnp.where(hit, -jnp.inf, masked)
    e_ref[...] = e_out.astype(I32)
    w_ref[...] = w_out / jnp.sum(w_out, axis=0, keepdims=True) * ROUTED_SCALE


def _route(lgT, rb):
    E, N = lgT.shape
    tn = 512
    return pl.pallas_call(
        _route_kernel,
        grid=(N // tn,),
        in_specs=[pl.BlockSpec((E, tn), lambda i: (0, i)), pl.BlockSpec((E, 1), lambda i: (0, 0))],
        out_specs=[pl.BlockSpec((MOE_TOPK, tn), lambda i: (0, i)), pl.BlockSpec((MOE_TOPK, tn), lambda i: (0, i))],
        out_shape=[jax.ShapeDtypeStruct((MOE_TOPK, N), I32), jax.ShapeDtypeStruct((MOE_TOPK, N), F32)],
        compiler_params=_cparams(("parallel",)),
        name="route",
    )(lgT, rb.reshape(E, 1).astype(F32))


def _slots_kernel(e_ref, tri_ref, lt_ref, dest_ref, cnt_ref, carry_ref, *, rb):
    phase = pl.program_id(0)
    i = pl.program_id(1)
    e = e_ref[...]
    tn = e.shape[1]
    ei = lax.broadcasted_iota(I32, (N_EXPERTS, tn), 0)

    @pl.when((phase == 0) & (i == 0))
    def _():
        carry_ref[...] = jnp.zeros_like(carry_ref)

    @pl.when(phase == 0)
    def _():
        tot = jnp.zeros((N_EXPERTS, 1), F32)
        for k in range(MOE_TOPK):
            oh = jnp.where(ei == e[k:k + 1, :], 1.0, 0.0)
            tot = tot + jnp.sum(oh, axis=1, keepdims=True)
        carry_ref[...] = carry_ref[...] + tot
        dest_ref[...] = jnp.zeros_like(dest_ref)

    @pl.when((phase == 1) & (i == 0))
    def _():
        cnt = carry_ref[...]
        cnt_ref[...] = cnt
        padded = jnp.broadcast_to(jnp.ceil(cnt / rb) * rb, (N_EXPERTS, 128))
        first = jnp.dot(lt_ref[...], padded, preferred_element_type=F32, precision=HIGHEST)
        carry_ref[...] = first[:, :1]

    @pl.when(phase == 1)
    def _():
        carry = carry_ref[...]
        ki = lax.broadcasted_iota(I32, (MOE_TOPK, tn), 0)
        dest = jnp.zeros((MOE_TOPK, tn), F32)
        for k in range(MOE_TOPK):
            hit = ei == e[k:k + 1, :]
            oh = jnp.where(hit, 1.0, 0.0)
            before = jnp.dot(oh.astype(BF16), tri_ref[...], preferred_element_type=F32)
            row = jnp.sum(jnp.where(hit, before + carry, 0.0), axis=0, keepdims=True)
            dest = jnp.where(ki == k, row, dest)
            carry = carry + jnp.sum(oh, axis=1, keepdims=True)
        carry_ref[...] = carry
        dest_ref[...] = dest.astype(I32)


def _slots(top_eT, rb):
    K, N = top_eT.shape
    tn = 512
    idx = np.arange(tn)
    tri = jnp.asarray((idx[:, None] < idx[None, :]).astype(np.float32), BF16)
    e_idx = np.arange(N_EXPERTS)
    lt = jnp.asarray((e_idx[None, :] < e_idx[:, None]).astype(np.float32), F32)
    return pl.pallas_call(
        functools.partial(_slots_kernel, rb=rb),
        grid=(2, N // tn),
        in_specs=[pl.BlockSpec((K, tn), lambda p, i: (0, i)),
                  pl.BlockSpec((tn, tn), lambda p, i: (0, 0)),
                  pl.BlockSpec((N_EXPERTS, N_EXPERTS), lambda p, i: (0, 0))],
        out_specs=[pl.BlockSpec((K, tn), lambda p, i: (0, i * p)),
                   pl.BlockSpec((N_EXPERTS, 1), lambda p, i: (0, 0))],
        out_shape=[jax.ShapeDtypeStruct((K, N), I32), jax.ShapeDtypeStruct((N_EXPERTS, 1), F32)],
        scratch_shapes=[pltpu.VMEM((N_EXPERTS, 1), F32)],
        compiler_params=_cparams(("arbitrary", "arbitrary")),
        name="slots",
    )(top_eT, tri, lt)


def _sc_scatter_rows(src, idx, n_out, repeat):
    n_idx = idx.shape[0]
    width = src.shape[1]
    w = src.shape[0] // SC_SPLIT // SC_WINDOW
    mesh = plsc.VectorSubcoreMesh(core_axis_name="c", subcore_axis_name="s")

    @pl.kernel(out_type=jax.ShapeDtypeStruct((n_out, width), src.dtype), mesh=mesh)
    def scatter_kernel(x_hbm, i_hbm, o_hbm):
        def body(x_vmem, i_vmem):
            pltpu.sync_copy(x_vmem, o_hbm.at[i_vmem.at[0]])

        pltpu.emit_pipeline(
            body,
            grid=(n_idx // SC_WINDOW,),
            in_specs=[pl.BlockSpec((SC_WINDOW, width), lambda g: ((g // (repeat * w)) * w + g % w, 0)),
                      pl.BlockSpec((1, SC_WINDOW), lambda g: (0, g))],
            out_specs=[],
            core_axis_name=("c", "s"),
            dimension_semantics=(pltpu.PARALLEL,),
        )(x_hbm, i_hbm)

    return scatter_kernel(src, idx.reshape(1, n_idx))


def _zero_pads_kernel(start_ref, len_ref, xs_in_ref, xs_ref, zeros_ref, sem, *, rb, n_rows):
    del xs_in_ref
    zeros_ref[...] = jnp.zeros_like(zeros_ref)
    n_seg = start_ref.shape[0]
    bits = rb.bit_length() - 1
    assert rb == 1 << bits

    def for_each_copy(act):
        def rows(row0, size):
            for s in range(SC_SPLIT):
                act(pltpu.make_async_copy(zeros_ref.at[pl.ds(0, size), :],
                                          xs_ref.at[pl.ds(s * n_rows + row0, size), :], sem))

        def segment(e, c):
            start = start_ref[e]
            length = len_ref[e]
            end = start + length
            n_whole = length // rb

            def whole(j, c2):
                rows(pl.multiple_of(end - (j + 1) * rb, SUBLANES), rb)
                return c2

            lax.fori_loop(0, n_whole, whole, 0)
            off = end - n_whole * rb
            rest = length - n_whole * rb
            for b in reversed(range(SUBLANES.bit_length() - 1, bits)):
                size = 1 << b
                take = jnp.bitwise_and(jnp.right_shift(rest, b), 1)
                off = off - take * size

                @pl.when(take == 1)
                def _(off=off, size=size):
                    rows(pl.multiple_of(off, SUBLANES), size)

            lead = jnp.bitwise_and(rest, SUBLANES - 1)
            for i in range(SUBLANES - 1):
                @pl.when(i < lead)
                def _(i=i):
                    rows(start + i, 1)

            return c

        lax.fori_loop(0, n_seg, segment, 0)

    for_each_copy(lambda cp: cp.start())
    for_each_copy(lambda cp: cp.wait())


def _zero_pads(xs2d, start, length, rb, n_rows):
    grid_spec = pltpu.PrefetchScalarGridSpec(
        num_scalar_prefetch=2,
        grid=(1,),
        in_specs=[pl.BlockSpec(memory_space=pl.ANY)],
        out_specs=pl.BlockSpec(memory_space=pl.ANY),
        scratch_shapes=[pltpu.VMEM((rb, xs2d.shape[1]), xs2d.dtype), pltpu.SemaphoreType.DMA(())],
    )
    return pl.pallas_call(
        functools.partial(_zero_pads_kernel, rb=rb, n_rows=n_rows),
        grid_spec=grid_spec,
        out_shape=jax.ShapeDtypeStruct(xs2d.shape, xs2d.dtype),
        input_output_aliases={2: 0},
        compiler_params=_cparams(("arbitrary",)),
        name="zeropads",
    )(start, length, xs2d)


def _experts_kernel(be_ref, x_ref, wg_ref, wu_ref, wd_ref, y_ref, wgu_sc, wd_sc):
    i = pl.program_id(0)
    hdim = wg_ref.shape[1]

    @pl.when((i == 0) | (be_ref[i] != be_ref[jnp.maximum(i - 1, 0)]))
    def _():
        wgu_sc[:, 0:hdim] = wg_ref[...].astype(BF16)
        wgu_sc[:, hdim:] = wu_ref[...].astype(BF16)
        wd_sc[...] = wd_ref[...].astype(BF16)

    x = _load_slabs_bf16(x_ref)
    gu = jnp.dot(x, wgu_sc[...], preferred_element_type=F32)
    g = gu[:, :hdim]
    act = (g * jax.nn.sigmoid(g) * gu[:, hdim:]).astype(BF16)
    _store_slabs(y_ref, jnp.dot(act, wd_sc[...], preferred_element_type=F32))


def _experts(blk_e, xs, wg, wu, wd, rb):
    n_rows = xs.shape[1]
    d_model, hdim = wg.shape[1:]
    per_expert = lambda i, be: (be[i], 0, 0)
    grid_spec = pltpu.PrefetchScalarGridSpec(
        num_scalar_prefetch=1,
        grid=(n_rows // rb,),
        in_specs=[pl.BlockSpec((SC_SPLIT, rb, SC_PIECE), lambda i, be: (0, i, 0)),
                  pl.BlockSpec((None, d_model, hdim), per_expert),
                  pl.BlockSpec((None, d_model, hdim), per_expert),
                  pl.BlockSpec((None, hdim, d_model), per_expert)],
        out_specs=pl.BlockSpec((SC_SPLIT, rb, SC_PIECE), lambda i, be: (0, i, 0)),
        scratch_shapes=[pltpu.VMEM((d_model, 2 * hdim), BF16), pltpu.VMEM((hdim, d_model), BF16)],
    )
    return pl.pallas_call(
        _experts_kernel,
        grid_spec=grid_spec,
        out_shape=jax.ShapeDtypeStruct((SC_SPLIT, n_rows, SC_PIECE), U32),
        compiler_params=_cparams(("arbitrary",)),
        name="experts",
    )(blk_e, xs, wg, wu, wd)


def _sc_gather_rows(table, idx):
    n = idx.shape[0]
    width = table.shape[1]
    mesh = plsc.VectorSubcoreMesh(core_axis_name="c", subcore_axis_name="s")

    @pl.kernel(out_type=jax.ShapeDtypeStruct((n, width), table.dtype), mesh=mesh)
    def gather_kernel(t_hbm, i_hbm, o_hbm):
        def body(i_vmem, o_vmem):
            pltpu.sync_copy(t_hbm.at[i_vmem.at[0]], o_vmem)

        pltpu.emit_pipeline(
            body,
            grid=(n // SC_WINDOW,),
            in_specs=[pl.BlockSpec((1, SC_WINDOW), lambda i: (0, i))],
            out_specs=[pl.BlockSpec((SC_WINDOW, width), lambda i: (i, 0))],
            core_axis_name=("c", "s"),
            dimension_semantics=(pltpu.PARALLEL,),
        )(i_hbm, o_hbm)

    return gather_kernel(table, idx.reshape(1, n))


def _combine_kernel(yg_ref, w_ref, h_ref, x1_ref, g2_ref, fw_ref, sgu_ref, sd_ref, o_ref):
    gu = jnp.dot(_load_slabs_bf16(h_ref), sgu_ref[...], preferred_element_type=F32)
    hdim = gu.shape[1] // 2
    g = gu[:, :hdim]
    ffn = jnp.dot((g * jax.nn.sigmoid(g) * gu[:, hdim:]).astype(BF16), sd_ref[...], preferred_element_type=F32)

    w = w_ref[...]
    tt = w.shape[0]
    lo = [jnp.zeros((tt, SC_PIECE), F32) for _ in range(SC_SPLIT)]
    hi = [jnp.zeros((tt, SC_PIECE), F32) for _ in range(SC_SPLIT)]
    for k in range(MOE_TOPK):
        wk = w[:, k:k + 1]
        for s in range(SC_SPLIT):
            y_lo, y_hi = _unpack_halves(yg_ref[s, k])
            lo[s] = lo[s] + wk * y_lo
            hi[s] = hi[s] + wk * y_hi
    ffn = ffn + jnp.concatenate(lo + hi, axis=1)
    x2 = x1_ref[...] + g2_ref[...] * ffn
    r = lax.rsqrt(jnp.mean(x2 * x2, axis=-1, keepdims=True) + NORM_EPS)
    o_ref[...] = x2 * r * fw_ref[...]


def _combine(yg, top_w, h2p, x1, g2, fw, sgu, sd, S, tt):
    N, D = x1.shape
    per_b = S // tt
    row = lambda i: (i, 0)
    full = lambda i: (0, 0)
    return pl.pallas_call(
        _combine_kernel,
        grid=(N // tt,),
        in_specs=[pl.BlockSpec((SC_SPLIT, MOE_TOPK, tt, SC_PIECE), lambda i: (0, 0, i, 0)),
                  pl.BlockSpec((tt, MOE_TOPK), row),
                  pl.BlockSpec((SC_SPLIT, tt, SC_PIECE), lambda i: (0, i, 0)),
                  pl.BlockSpec((tt, D), row),
                  pl.BlockSpec((None, 1, D), lambda i: (i // per_b, 0, 0)),
                  pl.BlockSpec((1, D), full),
                  pl.BlockSpec(sgu.shape, full),
                  pl.BlockSpec(sd.shape, full)],
        out_specs=pl.BlockSpec((tt, D), row),
        out_shape=jax.ShapeDtypeStruct((N, D), F32),
        compiler_params=_cparams(("parallel",)),
        name="combine",
    )(yg, top_w, h2p, x1, g2.reshape(-1, 1, D), fw.reshape(1, D), sgu, sd)


def _cmp_to_sel_T(n_cp, n_sel):
    c0 = np.arange(n_cp)[None, :] * CMP_STRIDE
    s0 = np.arange(n_sel)[:, None] * SEL_BLOCK
    ov = np.minimum(c0 + CMP_BLOCK, s0 + SEL_BLOCK) - np.maximum(c0, s0)
    m = np.clip(ov, 0, None).astype(np.float32) / CMP_BLOCK
    m[:, n_cp - 1] = 0.0
    return jnp.asarray(m, BF16)


def _slab_weights(w_in):
    D = w_in.shape[0]
    sizes = [1024, 1024, 1024, 512, 128, 128, 128, 128, 128, 128, 24, 2048]
    offs = np.concatenate([[0], np.cumsum(sizes)])
    part = lambda i: w_in[:, offs[i]:offs[i + 1]]
    pieces = [part(0), part(1), part(2), part(11), part(3)] + [part(i) for i in range(4, 10)] + [part(10)]
    w = jnp.concatenate(pieces, axis=1)
    return jnp.pad(w, ((0, 0), (0, SLAB_COLS - w.shape[1]))).astype(BF16)


def _layer(x, c, ada_w, ada_b, norm1_w, w_in, da_lq1, da_lk1, da_lq2, da_lk2, da_subln_w,
           cmp_k_pe, cmp_k_w1, cmp_k_w2, cmp_v_pe, cmp_v_w1, cmp_v_w2, w_da_out, w_nsa_out, w_o,
           norm2_w, router_w, router_b, exp_w_gate, exp_w_up, exp_w_down,
           sh_w_gate, sh_w_up, sh_w_down, final_norm_w, lam_init):
    B, S, D = x.shape
    N = B * S
    G, HPG = NSA_GROUPS, NSA_HPG
    x2 = x.reshape(N, D)

    mod = _ada(c, ada_w, ada_b)
    sh1, sc1, g1, sh2, sc2, g2 = jnp.split(mod, 6, axis=-1)

    slab = _inproj(x2, norm1_w, sh1, sc1, _slab_weights(w_in), S)
    slab3 = slab.reshape(B, S, SLAB_COLS)

    i_all = np.arange(1, DA_HEADS + NSA_HEADS + 1, dtype=np.float32)
    slopes = (2.0 ** (-8.0 * i_all / (DA_HEADS + NSA_HEADS))).astype(np.float32)
    slopes_a = jnp.asarray(slopes[0::2])
    slopes_b = jnp.asarray(slopes[1::2])

    ta_q, ta_k = min(512, S), min(512, S)
    qT = (slab3[:, :, COL_DAQ:COL_DAQ + 1024] * jnp.asarray(DA_DIM ** -0.5, BF16)).transpose(0, 2, 1)
    vT = slab3[:, :, COL_DAV:COL_DAV + 1024].reshape(B, S // ta_k, ta_k, DA_HEADS, DA_VDIM).transpose(0, 3, 1, 4, 2)
    kmin = _first_live_tile(slopes[0::2], slab3, ta_q, ta_k)
    oaT = _diff_attention(slopes_a, kmin, qT, slab3, _with_ones_row(vT), _slope_rows(slopes[0::2]), da_lq1, da_lk1,
                          da_lq2, da_lk2, da_subln_w, lam_init, ta_q, ta_k)
    oa = oaT.transpose(0, 2, 1).reshape(N, DA_HEADS * DA_VDIM)

    tb, tb_k = 256, min(256, S)
    n_cp = S // CMP_STRIDE
    n_sel = S // SEL_BLOCK

    def kv_groups(j):
        c0 = COL_KV6 + 128 * j
        return slab3[:, :, c0:c0 + 128].reshape(B, S, G, NSA_DIM).transpose(0, 2, 1, 3)

    def kv_tiles_T(a, rows):
        return a.reshape(B, G, S // rows, rows, NSA_DIM).transpose(0, 1, 2, 4, 3)

    ck_rows = kv_groups(0).reshape(B * G, n_cp, CMP_STRIDE * NSA_DIM)
    cv_rows = kv_groups(1).reshape(B * G, n_cp, CMP_STRIDE * NSA_DIM)
    kc = _compress(ck_rows, cmp_k_pe, cmp_k_w1, cmp_k_w2).reshape(B, G, n_cp, NSA_DIM).astype(BF16)
    vc = _compress(cv_rows, cmp_v_pe, cmp_v_w1, cmp_v_w2).reshape(B, G, n_cp, NSA_DIM).astype(BF16)
    nq = (slab3[:, :, COL_NSAQ:COL_NSAQ + 512] * jnp.asarray(NSA_DIM ** -0.5, BF16))
    nqT = nq.reshape(B, S // tb, tb, G, HPG, NSA_DIM).transpose(0, 3, 1, 5, 4, 2).reshape(B, G, S // tb, NSA_DIM, HPG * tb)
    gT = (slab3[:, :, COL_NSAG:COL_NSAG + 3 * NSA_HEADS].astype(F32).reshape(B, S // tb, tb, G, HPG, 3)
          .transpose(0, 3, 1, 5, 4, 2).reshape(B, G, S // tb, 3, HPG * tb))
    onehot = jnp.asarray((np.arange(S)[:, None] // SEL_BLOCK == np.arange(SEL_COLS)[None, :]).astype(np.float32), BF16)
    zeros64 = jnp.zeros((B, G, S, NSA_DIM), BF16)
    ksa = jnp.concatenate([kv_groups(2), zeros64, jnp.broadcast_to(onehot, (B, G, S, SEL_COLS))], axis=-1)
    pad_rows = jnp.zeros((B, G, WINDOW, 2 * NSA_DIM), BF16).at[..., NSA_DIM].set(1.0)
    kwa = jnp.concatenate([pad_rows, jnp.concatenate([kv_groups(4), zeros64], axis=-1)], axis=2)
    vw_pad = jnp.pad(kv_groups(5), ((0, 0), (0, 0), (WINDOW, 0), (0, 0)))
    vwT = vw_pad.reshape(B, G, (S + WINDOW) // tb, tb, NSA_DIM).transpose(0, 1, 2, 4, 3)
    r_w = np.arange(WINDOW + tb)[:, None]
    lane_w = np.arange(HPG * tb)[None, :]
    d_w = (WINDOW + lane_w % tb - r_w).astype(np.float32)
    slope_w = slopes[1::2].reshape(G, 1, HPG)[:, :, lane_w[0] // tb]
    wb = jnp.asarray(np.where((d_w >= 0) & (d_w < WINDOW), -slope_w * d_w[None], np.float32(NEG)).astype(np.float32))
    obT = _nsa_attention(slopes_b, nqT, kc, vc.transpose(0, 1, 3, 2), ksa, kv_tiles_T(kv_groups(3), tb_k),
                         kwa, vwT, wb, gT, _cmp_to_sel_T(n_cp, n_sel), tb, tb_k)
    ob = (obT.reshape(B, G, S // tb, NSA_DIM, HPG, tb).transpose(0, 2, 5, 1, 4, 3)
          .reshape(N, NSA_HEADS * NSA_DIM))

    wr_hi = router_w.astype(BF16)
    wr_lo = (router_w.astype(F32) - wr_hi.astype(F32)).astype(BF16)
    x1, h2, logits = _merge(oa, ob, slab, x2, g1, norm2_w, sh2, sc2, w_da_out.astype(BF16),
                            w_nsa_out.astype(BF16), w_o.astype(BF16), jnp.stack([wr_hi, wr_lo]), S)

    rb = 512
    tt = 128
    top_eT, top_wT = _route(logits.T, router_b)
    destT, counts = _slots(top_eT, rb)
    n_rows = ((N * MOE_TOPK + N_EXPERTS * (rb - 1) + rb - 1) // rb) * rb
    padded = (jnp.ceil(counts[:, 0] / rb) * rb).astype(I32)
    pend = jnp.cumsum(padded)
    blk_start = jnp.arange(n_rows // rb, dtype=I32) * rb
    blk_e = jnp.minimum(jnp.sum((pend[None, :] <= blk_start[:, None]).astype(I32), axis=1), N_EXPERTS - 1)
    d_flat = destT.reshape(-1)
    piece_idx = jnp.concatenate([d_flat + s * n_rows for s in range(SC_SPLIT)])
    xs2d = _sc_scatter_rows(h2.reshape(SC_SPLIT * N, SC_PIECE), piece_idx, SC_SPLIT * n_rows, MOE_TOPK)
    cnt = counts[:, 0].astype(I32)
    pad_start = jnp.concatenate([pend - padded + cnt, pend[-1:]])
    pad_len = jnp.concatenate([padded - cnt, n_rows - pend[-1:]])
    xs = _zero_pads(xs2d, pad_start, pad_len, rb, n_rows).reshape(SC_SPLIT, n_rows, SC_PIECE)
    ys = _experts(blk_e, xs, exp_w_gate, exp_w_up, exp_w_down, rb)
    sgu = jnp.concatenate([sh_w_gate, sh_w_up], axis=-1).astype(BF16)
    yg = _sc_gather_rows(ys.reshape(SC_SPLIT * n_rows, SC_PIECE), piece_idx).reshape(SC_SPLIT, MOE_TOPK, N, SC_PIECE)
    out = _combine(yg, top_wT.T, h2, x1, g2, final_norm_w, sgu, sh_w_down.astype(BF16), S, 2 * tt)
    return out.reshape(B, S, D)


def kernel(x, c, ada_w, ada_b, norm1_w, w_in, da_lq1, da_lk1, da_lq2, da_lk2, da_subln_w, cmp_k_pe, cmp_k_w1, cmp_k_w2, cmp_v_pe, cmp_v_w1, cmp_v_w2, w_da_out, w_nsa_out, w_o, norm2_w, router_w, router_b, exp_w_gate, exp_w_up, exp_w_down, sh_w_gate, sh_w_up, sh_w_down, final_norm_w):
    depth = ada_w.shape[0]
    assert depth == 1, "one decoder layer"
    lam_init = 0.8 - 0.6 * math.exp(-0.3 * 0)
    return _layer(x, c, ada_w[0], ada_b[0], norm1_w[0], w_in[0], da_lq1[0], da_lk1[0], da_lq2[0], da_lk2[0],
                  da_subln_w[0], cmp_k_pe[0], cmp_k_w1[0], cmp_k_w2[0], cmp_v_pe[0], cmp_v_w1[0], cmp_v_w2[0],
                  w_da_out[0], w_nsa_out[0], w_o[0], norm2_w[0], router_w[0], router_b[0],
                  exp_w_gate[0], exp_w_up[0], exp_w_down[0], sh_w_gate[0], sh_w_up[0], sh_w_down[0],
                  final_norm_w, lam_init)
```

```python
import functools
import math

import numpy as np
import jax
import jax.numpy as jnp
from jax import lax
from jax.experimental import pallas as pl
from jax.experimental.pallas import tpu as pltpu
from jax.experimental.pallas import tpu_sc as plsc

F32 = jnp.float32
BF16 = jnp.bfloat16
I32 = jnp.int32
U32 = jnp.uint32
HIGHEST = lax.Precision.HIGHEST

NORM_EPS = 1e-6
NEG = -1e30

DA_HEADS = 8
DA_DIM = 64
DA_VDIM = 128
DA_SUBLN_EPS = 1e-5

NSA_HEADS = 8
NSA_GROUPS = 2
NSA_HPG = 4
NSA_DIM = 64
CMP_BLOCK = 32
CMP_STRIDE = 16
SEL_BLOCK = 64
SEL_TOPK = 16
WINDOW = 512
FORCE_BONUS = 1e4
SEL_COLS = 128

N_EXPERTS = 64
MOE_TOPK = 8
N_EXPERT_GROUPS = 8
TOPK_EXPERT_GROUPS = 4
ROUTED_SCALE = 2.5

VMEM_LIMIT_V7X = 56 * 1024 * 1024

COL_DAQ, COL_DAK, COL_DAV = 0, 1024, 2048
COL_MERGE = 3072
COL_NSAQ = 5120
COL_KV6 = 5632
COL_NSAG = 6400
SLAB_COLS = 6528


SUBLANES = 8
SC_WINDOW = 128
SC_SPLIT = 2
SC_PIECE = 256

POS_SPLIT = 16
POS_ROWS = 16
ONES_ROWS = 16


def _pos_features(rows):
    r = np.arange(rows)
    f = np.zeros((rows, 128), np.float32)
    f[:, 0:3] = (r // POS_SPLIT)[:, None]
    f[:, 3:6] = (r % POS_SPLIT)[:, None]
    return jnp.asarray(f, BF16)


def _slope_rows(slopes):
    s = np.asarray(slopes, np.float32)
    bf = lambda x: x.astype(BF16).astype(np.float32)
    p1 = bf(s)
    p2 = bf(s - p1)
    p3 = bf(s - p1 - p2)
    out = np.zeros((s.shape[0], POS_ROWS, 128), np.float32)
    for i, piece in enumerate((p1, p2, p3)):
        out[:, i, :] = POS_SPLIT * piece[:, None]
        out[:, 3 + i, :] = piece[:, None]
    return jnp.asarray(out)


def _with_ones_row(vT):
    shape = vT.shape[:-2] + (ONES_ROWS, vT.shape[-1])
    extra = jnp.zeros(shape, vT.dtype).at[..., 0, :].set(1.0)
    return jnp.concatenate([vT, extra], axis=-2)


def _cparams(sem, vmem=VMEM_LIMIT_V7X):
    return pltpu.CompilerParams(dimension_semantics=sem, vmem_limit_bytes=vmem)


def _ada_kernel(c_ref, w_ref, b_ref, o_ref):
    c = c_ref[...]
    ca = c * jax.nn.sigmoid(c)
    o_ref[...] = jnp.dot(ca, w_ref[...], preferred_element_type=F32, precision=HIGHEST) + b_ref[...]


def _ada(c, w, b):
    B, D = c.shape
    n_out = w.shape[1]
    rows = 8
    cp = jnp.zeros((rows, D), F32).at[:B].set(c)
    tn = 1024
    out = pl.pallas_call(
        _ada_kernel,
        grid=(n_out // tn,),
        in_specs=[pl.BlockSpec((rows, D), lambda j: (0, 0)),
                  pl.BlockSpec((D, tn), lambda j: (0, j)),
                  pl.BlockSpec((1, tn), lambda j: (0, j))],
        out_specs=pl.BlockSpec((rows, tn), lambda j: (0, j)),
        out_shape=jax.ShapeDtypeStruct((rows, n_out), F32),
        compiler_params=_cparams(("arbitrary",)),
        name="ada",
    )(cp, w, b.reshape(1, n_out))
    return out[:B]


def _modulate(x, nw, sh, sc):
    r = lax.rsqrt(jnp.mean(x * x, axis=-1, keepdims=True) + NORM_EPS)
    return (x * r) * nw * (1.0 + sc) + sh


def _inproj_kernel(x_ref, nw_ref, sh_ref, sc_ref, w_ref, o_ref, h_ref):
    @pl.when(pl.program_id(1) == 0)
    def _():
        h_ref[...] = _modulate(x_ref[...], nw_ref[...], sh_ref[...], sc_ref[...]).astype(BF16)

    o_ref[...] = jnp.dot(h_ref[...], w_ref[...], preferred_element_type=F32).astype(BF16)


def _inproj(x2, nw, sh, sc, w_slab, S):
    N, D = x2.shape
    tm = min(1024, S)
    tn = SLAB_COLS // 3
    per_b = S // tm
    return pl.pallas_call(
        _inproj_kernel,
        grid=(N // tm, SLAB_COLS // tn),
        in_specs=[pl.BlockSpec((tm, D), lambda i, j: (i, 0)),
                  pl.BlockSpec((1, D), lambda i, j: (0, 0)),
                  pl.BlockSpec((None, 1, D), lambda i, j: (i // per_b, 0, 0)),
                  pl.BlockSpec((None, 1, D), lambda i, j: (i // per_b, 0, 0)),
                  pl.BlockSpec((D, tn), lambda i, j: (0, j))],
        out_specs=pl.BlockSpec((tm, tn), lambda i, j: (i, j)),
        out_shape=jax.ShapeDtypeStruct((N, SLAB_COLS), BF16),
        scratch_shapes=[pltpu.VMEM((tm, D), BF16)],
        compiler_params=_cparams(("parallel", "arbitrary")),
        name="inproj",
    )(x2, nw.reshape(1, D), sh.reshape(-1, 1, D), sc.reshape(-1, 1, D), w_slab)


def _cmp_kernel(r_ref, pe_ref, w1a_ref, w1b_ref, w2_ref, o_ref):
    r = r_ref[...]
    n_rows = r.shape[0]
    a = jnp.dot(r, w1a_ref[...], preferred_element_type=F32)
    b = jnp.dot(r, w1b_ref[...], preferred_element_type=F32)
    pe = pe_ref[...]
    half = pe.shape[1] // 2
    bias = (jnp.dot(pe[:, :half], w1a_ref[...].astype(F32), preferred_element_type=F32, precision=HIGHEST)
            + jnp.dot(pe[:, half:], w1b_ref[...].astype(F32), preferred_element_type=F32, precision=HIGHEST))
    hid = a + pltpu.roll(b, n_rows - 1, 0) + bias
    act = 0.5 * hid * (1.0 + jnp.tanh(0.7978845608028654 * (hid + 0.044715 * hid * hid * hid)))
    o_ref[...] = jnp.dot(act, w2_ref[...], preferred_element_type=F32, precision=HIGHEST)


def _compress(rows, pe, w1, w2):
    BG, n_rows, width = rows.shape
    hidden = w1.shape[1]
    w1a = w1[:width].astype(BF16)
    w1b = w1[width:].astype(BF16)
    return pl.pallas_call(
        _cmp_kernel,
        grid=(BG,),
        in_specs=[pl.BlockSpec((None, n_rows, width), lambda i: (i, 0, 0)),
                  pl.BlockSpec((1, 2 * width), lambda i: (0, 0)),
                  pl.BlockSpec((width, hidden), lambda i: (0, 0)),
                  pl.BlockSpec((width, hidden), lambda i: (0, 0)),
                  pl.BlockSpec((hidden, NSA_DIM), lambda i: (0, 0))],
        out_specs=pl.BlockSpec((None, n_rows, NSA_DIM), lambda i: (i, 0, 0)),
        out_shape=jax.ShapeDtypeStruct((BG, n_rows, NSA_DIM), F32),
        compiler_params=_cparams(("parallel",)),
        name="cmp",
    )(rows, pe.reshape(1, -1).astype(F32), w1a, w1b, w2.astype(F32))


def _da_kernel(slopes_ref, kmin_ref, qT_ref, k_ref, vT_ref, pos_ref, rel_ref, srow_ref, lq1_ref, lk1_ref, lq2_ref,
               lk2_ref, subw_ref, oT_ref, acc_ref, qa_ref, sa_ref, sb_ref, pa_ref, pb_ref, *, tq, tk, lam_init):
    h = pl.program_id(1)
    qi = pl.program_id(2)
    slope = slopes_ref[h]
    q0 = qi * tq
    w = 2 * tq
    nk = k_ref.shape[0] // tk
    qT = qT_ref[...]
    row = lax.broadcasted_iota(I32, qT.shape, 0)
    zero = jnp.zeros_like(qT)
    qa_ref[0:2 * DA_DIM, :] = jnp.concatenate(
        [jnp.where(row < DA_DIM, qT, zero), jnp.where(row >= DA_DIM, qT, zero)], axis=1)
    qa_ref[2 * DA_DIM:2 * DA_DIM + POS_ROWS, :] = jnp.concatenate([srow_ref[...]] * (w // 128), axis=1).astype(BF16)
    qa_ref[2 * DA_DIM + POS_ROWS:, :] = jnp.zeros((2 * DA_DIM - POS_ROWS, w), BF16)

    def scores(kt, s_ref):
        k0 = kt * tk
        k_t = k_ref[pl.ds(pl.multiple_of(k0, tk), tk), :]
        s = jnp.dot(jnp.concatenate([k_t, pos_ref[...]], axis=1), qa_ref[...], preferred_element_type=F32)
        s = jnp.where(rel_ref[...] <= q0 - k0, s, NEG)
        s_ref[...] = s
        return jnp.max(s, axis=0, keepdims=True)

    def probs(s_ref, p_ref, mx, off, m_old):
        m_new = jnp.maximum(m_old, mx + off)
        p_ref[...] = jnp.exp(s_ref[...] - (m_new - off)).astype(BF16)
        return m_new, jnp.exp(m_old - m_new)

    def accumulate(kt, p_ref, alpha):
        acc_ref[...] = alpha * acc_ref[...] + jnp.dot(vT_ref[kt], p_ref[...], preferred_element_type=F32)

    n_tiles = (q0 + tq - 1) // tk + 1
    kt_min = kmin_ref[(pl.program_id(0) * pl.num_programs(1) + h) * pl.num_programs(2) + qi]
    acc_ref[...] = jnp.zeros_like(acc_ref)
    pb_ref[...] = jnp.zeros_like(pb_ref)
    mx_a = scores(kt_min, sa_ref)

    def pair(j, carry):
        m, mx_a, alpha_b = carry
        ta = kt_min + 2 * j
        tb = ta + 1
        off_a = slope * (ta * tk).astype(F32)
        off_b = jnp.where(tb < n_tiles, slope * (tb * tk).astype(F32), NEG)
        accumulate(jnp.maximum(ta - 1, 0), pb_ref, alpha_b)
        m, alpha_a = probs(sa_ref, pa_ref, mx_a, off_a, m)
        mx_b = scores(jnp.minimum(tb, nk - 1), sb_ref)
        accumulate(ta, pa_ref, alpha_a)
        m, alpha_b = probs(sb_ref, pb_ref, mx_b, off_b, m)
        mx_a = scores(jnp.minimum(ta + 2, nk - 1), sa_ref)
        return m, mx_a, alpha_b

    n_pairs = (n_tiles - kt_min + 1) // 2
    init = (jnp.full((1, w), NEG, F32), mx_a, jnp.ones((1, w), F32))
    _, _, alpha_b = lax.fori_loop(0, n_pairs, pair, init)
    accumulate(jnp.minimum(kt_min + 2 * n_pairs - 1, nk - 1), pb_ref, alpha_b)

    lam = (jnp.exp(jnp.sum(lq1_ref[...] * lk1_ref[...], axis=-1, keepdims=True))
           - jnp.exp(jnp.sum(lq2_ref[...] * lk2_ref[...], axis=-1, keepdims=True)) + lam_init)
    on = acc_ref[0:DA_VDIM, :] * (1.0 / acc_ref[DA_VDIM:DA_VDIM + 1, :])
    o = on[:, :tq] - lam * on[:, tq:]
    r = lax.rsqrt(jnp.mean(o * o, axis=0, keepdims=True) + DA_SUBLN_EPS)
    oT_ref[...] = (o * r * subw_ref[...] * (1.0 - lam_init)).astype(BF16)


EXP_UNDERFLOW = 104.0


def _first_live_tile(slopes_np, slab3, tq, tk):
    B, S, _ = slab3.shape
    H = DA_HEADS
    width = 2 * DA_DIM * H
    grp = jnp.asarray((np.arange(width)[:, None] // DA_DIM == np.arange(2 * H)[None, :]).astype(np.float32), BF16)

    def norms(c0):
        v = slab3[:, :, c0:c0 + width]
        return jnp.sqrt(jnp.einsum('bsc,cg->bsg', v * v, grp, preferred_element_type=F32))

    qn = norms(COL_DAQ) * DA_DIM ** -0.5
    qn = jnp.max(qn.reshape(B, S // tq, tq, H, 2), axis=2).transpose(0, 2, 3, 1)
    kn = jnp.max(norms(COL_DAK), axis=1).reshape(B, H, 2)
    bound = jnp.max(qn * kn[..., None], axis=2) * 1.01 + 0.01
    q0 = (jnp.arange(S // tq, dtype=F32) * tq)[None, None, :]
    slope = jnp.asarray(slopes_np, F32)[None, :, None]
    first_key = q0 - (EXP_UNDERFLOW + 2.0 * bound) / slope
    kt = jnp.floor(first_key / tk).astype(I32)
    n_tiles = (jnp.arange(S // tq, dtype=I32) * tq + tq - 1) // tk + 1
    return jnp.clip(kt, 0, n_tiles[None, None, :] - 1).reshape(-1)


def _diff_attention(slopes, kmin, qT, slab3, vT, srows, lq1, lk1, lq2, lk2, subw, lam_init, tq, tk):
    B, _, S = qT.shape
    nk = S // tk
    assert tk % tq == 0 and S % tk == 0
    tile = tq
    kcol = COL_DAK // 128
    v_rows = vT.shape[3]
    vec = lambda a: a.reshape(1, DA_DIM).astype(F32)
    grid_spec = pltpu.PrefetchScalarGridSpec(
        num_scalar_prefetch=2,
        grid=(B, DA_HEADS, S // tile),
        in_specs=[pl.BlockSpec((None, 128, tile), lambda b, h, i, s, km: (b, h, i)),
                  pl.BlockSpec((None, S, 128), lambda b, h, i, s, km: (b, 0, kcol + h)),
                  pl.BlockSpec((None, None, nk, v_rows, tk), lambda b, h, i, s, km: (b, h, 0, 0, 0)),
                  pl.BlockSpec((tk, 128), lambda b, h, i, s, km: (0, 0)),
                  pl.BlockSpec((tk, 2 * tq), lambda b, h, i, s, km: (0, 0)),
                  pl.BlockSpec((None, POS_ROWS, 128), lambda b, h, i, s, km: (h, 0, 0)),
                  pl.BlockSpec((1, DA_DIM), lambda b, h, i, s, km: (0, 0)),
                  pl.BlockSpec((1, DA_DIM), lambda b, h, i, s, km: (0, 0)),
                  pl.BlockSpec((1, DA_DIM), lambda b, h, i, s, km: (0, 0)),
                  pl.BlockSpec((1, DA_DIM), lambda b, h, i, s, km: (0, 0)),
                  pl.BlockSpec((DA_VDIM, 1), lambda b, h, i, s, km: (0, 0))],
        out_specs=pl.BlockSpec((None, 128, tile), lambda b, h, i, s, km: (b, h, i)),
        scratch_shapes=[pltpu.VMEM((v_rows, 2 * tq), F32), pltpu.VMEM((4 * DA_DIM, 2 * tq), BF16),
                        pltpu.VMEM((tk, 2 * tq), F32), pltpu.VMEM((tk, 2 * tq), F32),
                        pltpu.VMEM((tk, 2 * tq), BF16), pltpu.VMEM((tk, 2 * tq), BF16)],
    )
    rel = jnp.asarray(np.arange(tk)[:, None] - (np.arange(2 * tq) % tq)[None, :], I32)
    return pl.pallas_call(
        functools.partial(_da_kernel, tq=tq, tk=tk, lam_init=lam_init),
        grid_spec=grid_spec,
        out_shape=jax.ShapeDtypeStruct((B, DA_HEADS * DA_VDIM, S), BF16),
        compiler_params=_cparams(("parallel", "parallel", "arbitrary")),
        name="diffattn",
    )(slopes, kmin, qT, slab3, vT, _pos_features(tk), rel, srows, vec(lq1), vec(lk1), vec(lq2), vec(lk2),
      subw.reshape(DA_VDIM, 1).astype(F32))


def _nsa_kernel(slopes_ref, qT_ref, kc_ref, vcT_ref, ksa_ref, vsT_ref, kwa_ref, vwT_ref, wb_ref, gT_ref, mT_ref,
                grp_ref, ltri_ref, oT_ref, acc_ref, res_ref, base_ref, qa_ref, sa_ref, sb_ref, pa_ref, pb_ref,
                list_ref, slist_ref, lsem, *, t, tk, n_top):
    g = pl.program_id(1)
    qi = pl.program_id(2)
    q0 = qi * t
    w = NSA_HPG * t
    n_cp = kc_ref.shape[0]
    n_sel = mT_ref.shape[0]
    lane1 = lax.broadcasted_iota(I32, (1, w), 1)
    slope_row = jnp.zeros((1, w), F32)
    for hh in range(NSA_HPG):
        slope_row = jnp.where((lane1 >= hh * t) & (lane1 < (hh + 1) * t), slopes_ref[g * NSA_HPG + hh], slope_row)
    q_all = qT_ref[...]
    gates = jax.nn.sigmoid(gT_ref[...])
    key_off = lax.broadcasted_iota(I32, (tk, w), 0)
    t_loc = jnp.bitwise_and(lax.broadcasted_iota(I32, (tk, w), 1), t - 1)
    base_ref[...] = slope_row * key_off.astype(F32)

    c_idx = lax.broadcasted_iota(I32, (n_cp, w), 0)
    t_c = q0 + jnp.bitwise_and(lax.broadcasted_iota(I32, (n_cp, w), 1), t - 1)
    d_c = t_c - (c_idx * CMP_STRIDE + (CMP_BLOCK - 1))
    ok_c = d_c >= 0
    s = jnp.dot(kc_ref[...], q_all, preferred_element_type=F32) - slope_row * d_c.astype(F32)
    s = jnp.where(ok_c, s, NEG)
    e = jnp.exp(s - jnp.max(s, axis=0, keepdims=True))
    inv = 1.0 / jnp.sum(e, axis=0, keepdims=True)
    p = jnp.where(ok_c, e * inv, 0.0)
    res_ref[...] = gates[0:1, :] * jnp.dot(vcT_ref[...], p.astype(BF16), preferred_element_type=F32)
    p_sum = p[:, 0:t]
    for hh in range(1, NSA_HPG):
        p_sum = p_sum + p[:, hh * t:(hh + 1) * t]

    p_hi = p_sum.astype(BF16)
    p_lo = (p_sum - p_hi.astype(F32)).astype(BF16)
    mT = mT_ref[...]
    imp = jnp.dot(mT, p_hi, preferred_element_type=F32) + jnp.dot(mT, p_lo, preferred_element_type=F32)
    blk = lax.broadcasted_iota(I32, (n_sel, t), 0)
    t_s = q0 + lax.broadcasted_iota(I32, (n_sel, t), 1)
    cur = jnp.right_shift(t_s, int(math.log2(SEL_BLOCK)))
    forced = (blk == 0) | (blk == cur) | (blk == cur - 1)
    score = jnp.where(blk * SEL_BLOCK <= t_s, imp + jnp.where(forced, FORCE_BONUS, 0.0), NEG)
    blk_f = blk.astype(F32)
    sel = jnp.zeros((n_sel, t), F32)
    for _ in range(n_top):
        _, idx = _first_max(score, blk_f)
        hit = blk_f == idx
        sel = jnp.where(hit, 1.0, sel)
        score = jnp.where(hit, -jnp.inf, score)
    n_full = q0 // tk
    nt_pad = grp_ref.shape[0]
    cnt = jnp.broadcast_to(jnp.sum(sel, axis=1, keepdims=True), (n_sel, 128)).astype(BF16)
    tile_cnt = jnp.dot(grp_ref[...], cnt, preferred_element_type=F32)
    kt_i = lax.broadcasted_iota(I32, (nt_pad, 128), 0)
    active = (tile_cnt > 0.5) & (kt_i < n_full)
    act_f = jnp.where(active, 1.0, 0.0)
    before = jnp.dot(ltri_ref[...], act_f.astype(BF16), preferred_element_type=F32)
    lane_j = lax.broadcasted_iota(I32, (nt_pad, 128), 1)
    slot_hit = active & (before == lane_j.astype(F32))
    tiles_row = jnp.sum(jnp.where(slot_hit, kt_i.astype(F32), 0.0), axis=0, keepdims=True)
    n_act_row = jnp.sum(act_f, axis=0, keepdims=True)
    lane8 = lax.broadcasted_iota(I32, (8, 128), 1)
    list_ref[...] = jnp.where(lane8 == 127, n_act_row, tiles_row).astype(I32)
    list_copy = pltpu.make_async_copy(list_ref, slist_ref, lsem)
    list_copy.start()

    selb = jnp.where(sel > 0.5, 0.0, NEG)
    row64 = lax.broadcasted_iota(I32, (NSA_DIM, w), 0)
    qa_ref[0:NSA_DIM, :] = q_all
    qa_ref[NSA_DIM:2 * NSA_DIM, :] = jnp.where(row64 == 0, NEG, 0.0).astype(BF16)
    qa_ref[2 * NSA_DIM:2 * NSA_DIM + n_sel, :] = jnp.concatenate([selb] * NSA_HPG, axis=1).astype(BF16)
    if n_sel < SEL_COLS:
        qa_ref[2 * NSA_DIM + n_sel:, :] = jnp.zeros((SEL_COLS - n_sel, w), BF16)

    n_wt = (WINDOW + t) // t
    kw_t = kwa_ref[pl.ds(pl.multiple_of(q0, t), WINDOW + t), :]
    s = jnp.dot(kw_t, qa_ref[0:2 * NSA_DIM, :], preferred_element_type=F32) + wb_ref[...]
    p = jnp.exp(s - jnp.max(s, axis=0, keepdims=True))
    inv = 1.0 / jnp.sum(p, axis=0, keepdims=True)
    p = p.astype(BF16)
    o_w = jnp.dot(vwT_ref[qi], p[0:t, :], preferred_element_type=F32)
    for c in range(1, n_wt):
        o_w = o_w + jnp.dot(vwT_ref[qi + c], p[c * t:(c + 1) * t, :], preferred_element_type=F32)
    res_ref[...] = res_ref[...] + gates[2:3, :] * (o_w * inv)

    def scores(kt, s_ref):
        k_t = ksa_ref[pl.ds(pl.multiple_of(kt * tk, tk), tk), :]
        s = jnp.dot(k_t, qa_ref[...], preferred_element_type=F32) + base_ref[...]
        s_ref[...] = s
        return jnp.max(s, axis=0, keepdims=True)

    def probs(s_ref, p_ref, mx, off, m_old, l_old):
        m_new = jnp.maximum(m_old, mx + off)
        p = jnp.exp(s_ref[...] - (m_new - off))
        alpha = jnp.exp(m_old - m_new)
        p_ref[...] = p.astype(BF16)
        return m_new, alpha * l_old + jnp.sum(p, axis=0, keepdims=True), alpha

    def accumulate(kt, p_ref, alpha):
        acc_ref[...] = alpha * acc_ref[...] + jnp.dot(vsT_ref[kt], p_ref[...], preferred_element_type=F32)

    list_copy.wait()
    n_act = slist_ref[0, 127]
    mx_a = scores(slist_ref[0, 0], sa_ref)

    kd = n_full * tk
    s = jnp.dot(ksa_ref[pl.ds(pl.multiple_of(kd, tk), tk), :], qa_ref[...], preferred_element_type=F32) + base_ref[...]
    s = jnp.where(key_off - t_loc <= q0 - kd, s, NEG)
    off_d = slope_row * kd.astype(F32)
    m = jnp.max(s, axis=0, keepdims=True) + off_d
    p = jnp.exp(s - (m - off_d))
    l = jnp.sum(p, axis=0, keepdims=True)
    acc_ref[...] = jnp.dot(vsT_ref[n_full], p.astype(BF16), preferred_element_type=F32)
    pb_ref[...] = jnp.zeros_like(pb_ref)

    def pair(j, carry):
        m, l, mx_a, alpha_b, tb_prev = carry
        ta = slist_ref[0, 2 * j]
        tb = slist_ref[0, 2 * j + 1]
        off_a = slope_row * (ta * tk).astype(F32)
        off_b = jnp.where(2 * j + 1 < n_act, slope_row * (tb * tk).astype(F32), NEG)
        accumulate(tb_prev, pb_ref, alpha_b)
        m, l, alpha_a = probs(sa_ref, pa_ref, mx_a, off_a, m, l)
        mx_b = scores(tb, sb_ref)
        accumulate(ta, pa_ref, alpha_a)
        m, l, alpha_b = probs(sb_ref, pb_ref, mx_b, off_b, m, l)
        mx_a = scores(slist_ref[0, 2 * j + 2], sa_ref)
        return m, l, mx_a, alpha_b, tb

    n_pairs = (n_act + 1) // 2
    m, l, _, alpha_b, tb_last = lax.fori_loop(0, n_pairs, pair, (m, l, mx_a, jnp.ones((1, w), F32), jnp.int32(0)))
    accumulate(tb_last, pb_ref, alpha_b)
    oT_ref[...] = (res_ref[...] + gates[1:2, :] * (acc_ref[...] * (1.0 / l))).astype(BF16)


def _nsa_attention(slopes, qT, kc, vcT, ksa, vsT, kwa, vwT, wb, gT, mT, t, tk):
    B, G, nq, _, w = qT.shape
    S = nq * t
    n_cp = kc.shape[2]
    n_sel = S // SEL_BLOCK
    n_top = min(SEL_TOPK, n_sel)
    assert tk % t == 0 and S % tk == 0 and t & (t - 1) == 0 and WINDOW % t == 0 and n_sel <= SEL_COLS
    n_tiles = S // tk
    nt_pad = -(-n_tiles // 16) * 16
    assert nt_pad + 2 < 127
    grp = jnp.asarray((np.arange(n_sel)[None, :] // (tk // SEL_BLOCK) == np.arange(nt_pad)[:, None])
                      .astype(np.float32), BF16)
    ltri = jnp.asarray((np.arange(nt_pad)[None, :] < np.arange(nt_pad)[:, None]).astype(np.float32), BF16)
    fixed = lambda b, g, i, s: (b, g, 0, 0)
    fixed5 = lambda b, g, i, s: (b, g, 0, 0, 0)
    tile5 = lambda b, g, i, s: (b, g, i, 0, 0)
    grid_spec = pltpu.PrefetchScalarGridSpec(
        num_scalar_prefetch=1,
        grid=(B, G, nq),
        in_specs=[pl.BlockSpec((None, None, None, NSA_DIM, w), tile5),
                  pl.BlockSpec((None, None, n_cp, NSA_DIM), fixed),
                  pl.BlockSpec((None, None, NSA_DIM, n_cp), fixed),
                  pl.BlockSpec((None, None, S, ksa.shape[-1]), fixed),
                  pl.BlockSpec((None, None, S // tk, NSA_DIM, tk), fixed5),
                  pl.BlockSpec((None, None, S + WINDOW, 2 * NSA_DIM), fixed),
                  pl.BlockSpec((None, None, (S + WINDOW) // t, NSA_DIM, t), fixed5),
                  pl.BlockSpec((None, WINDOW + t, w), lambda b, g, i, s: (g, 0, 0)),
                  pl.BlockSpec((None, None, None, 3, w), tile5),
                  pl.BlockSpec((n_sel, n_cp), lambda b, g, i, s: (0, 0)),
                  pl.BlockSpec((nt_pad, n_sel), lambda b, g, i, s: (0, 0)),
                  pl.BlockSpec((nt_pad, nt_pad), lambda b, g, i, s: (0, 0))],
        out_specs=pl.BlockSpec((None, None, None, NSA_DIM, w), tile5),
        scratch_shapes=[pltpu.VMEM((NSA_DIM, w), F32),
                        pltpu.VMEM((NSA_DIM, w), F32),
                        pltpu.VMEM((tk, w), F32),
                        pltpu.VMEM((2 * NSA_DIM + SEL_COLS, w), BF16),
                        pltpu.VMEM((tk, w), F32), pltpu.VMEM((tk, w), F32),
                        pltpu.VMEM((tk, w), BF16), pltpu.VMEM((tk, w), BF16),
                        pltpu.VMEM((8, 128), I32), pltpu.SMEM((8, 128), I32), pltpu.SemaphoreType.DMA(())],
    )
    return pl.pallas_call(
        functools.partial(_nsa_kernel, t=t, tk=tk, n_top=n_top),
        grid_spec=grid_spec,
        out_shape=jax.ShapeDtypeStruct((B, G, nq, NSA_DIM, w), BF16),
        compiler_params=_cparams(("parallel", "parallel", "arbitrary")),
        name="nsa",
    )(slopes, qT, kc, vcT, ksa, vsT, kwa, vwT, wb, gT, mT, grp, ltri)


def _pack_halves(v):
    c = v.shape[1] // 2
    lo = lax.bitcast_convert_type(v[:, :c].astype(BF16).astype(F32), U32)
    hi = lax.bitcast_convert_type(v[:, c:].astype(BF16).astype(F32), U32)
    return hi | (lo >> 16)


def _unpack_halves(u):
    lo = lax.bitcast_convert_type(u << 16, F32)
    hi = lax.bitcast_convert_type(u & jnp.uint32(0xFFFF0000), F32)
    return lo, hi


def _store_slabs(ref, v, rows=slice(None)):
    packed = _pack_halves(v)
    for s in range(SC_SPLIT):
        ref[s, rows, :] = packed[:, s * SC_PIECE:(s + 1) * SC_PIECE]


def _load_slabs_bf16(ref, rows=slice(None)):
    parts = [_unpack_halves(ref[s, rows, :]) for s in range(SC_SPLIT)]
    return jnp.concatenate([p[0].astype(BF16) for p in parts] + [p[1].astype(BF16) for p in parts], axis=1)


def _merge_kernel(oa_ref, ob_ref, ga_ref, gb_ref, x_ref, g1_ref, nw_ref, sh_ref, sc_ref,
                  wa_ref, wb_ref, wo_ref, wr_ref, x1_ref, h2_ref, lg_ref):
    ya = jnp.dot(oa_ref[...], wa_ref[...], preferred_element_type=F32)
    yb = jnp.dot(ob_ref[...], wb_ref[...], preferred_element_type=F32)
    merged = (jax.nn.sigmoid(ga_ref[...].astype(F32)) * ya
              + jax.nn.sigmoid(gb_ref[...].astype(F32)) * yb).astype(BF16)
    mix = jnp.dot(merged, wo_ref[...], preferred_element_type=F32)
    x1 = x_ref[...] + g1_ref[...] * mix
    x1_ref[...] = x1
    h2 = _modulate(x1, nw_ref[...], sh_ref[...], sc_ref[...])
    _store_slabs(h2_ref, h2)
    h_hi = h2.astype(BF16)
    h_lo = (h2 - h_hi.astype(F32)).astype(BF16)
    lg_ref[...] = (jnp.dot(h_hi, wr_ref[0], preferred_element_type=F32)
                   + jnp.dot(h_lo, wr_ref[0], preferred_element_type=F32)
                   + jnp.dot(h_hi, wr_ref[1], preferred_element_type=F32))


def _merge(oa, ob, slab, x2, g1, nw, sh, sc, wa, wb, wo, wr, S):
    N, D = x2.shape
    tm = 256
    per_b = S // tm
    mcol = COL_MERGE // D
    row = lambda i: (i, 0)
    full = lambda i: (0, 0)
    perb = lambda i: (i // per_b, 0, 0)
    n_lg = wr.shape[-1]
    return pl.pallas_call(
        _merge_kernel,
        grid=(N // tm,),
        in_specs=[pl.BlockSpec((tm, oa.shape[1]), row),
                  pl.BlockSpec((tm, ob.shape[1]), row),
                  pl.BlockSpec((tm, D), lambda i: (i, mcol)),
                  pl.BlockSpec((tm, D), lambda i: (i, mcol + 1)),
                  pl.BlockSpec((tm, D), row),
                  pl.BlockSpec((None, 1, D), perb),
                  pl.BlockSpec((1, D), full),
                  pl.BlockSpec((None, 1, D), perb),
                  pl.BlockSpec((None, 1, D), perb),
                  pl.BlockSpec(wa.shape, full),
                  pl.BlockSpec(wb.shape, full),
                  pl.BlockSpec(wo.shape, full),
                  pl.BlockSpec(wr.shape, lambda i: (0, 0, 0))],
        out_specs=[pl.BlockSpec((tm, D), row), pl.BlockSpec((SC_SPLIT, tm, SC_PIECE), lambda i: (0, i, 0)),
                   pl.BlockSpec((tm, n_lg), row)],
        out_shape=[jax.ShapeDtypeStruct((N, D), F32), jax.ShapeDtypeStruct((SC_SPLIT, N, SC_PIECE), U32),
                   jax.ShapeDtypeStruct((N, n_lg), F32)],
        compiler_params=_cparams(("parallel",)),
        name="merge",
    )(oa, ob, slab, slab, x2, g1.reshape(-1, 1, D), nw.reshape(1, D), sh.reshape(-1, 1, D),
      sc.reshape(-1, 1, D), wa, wb, wo, wr)


def _first_max(v, idx):
    mx = jnp.max(v, axis=0, keepdims=True)
    first = jnp.min(jnp.where(v == mx, idx, float(v.shape[0])), axis=0, keepdims=True)
    return mx, first


def _route_kernel(lg_ref, b_ref, e_ref, w_ref):
    lg = lg_ref[...]
    tn = lg.shape[1]
    per_g = N_EXPERTS // N_EXPERT_GROUPS
    scores = jax.nn.sigmoid(lg)
    biased = scores + b_ref[...]
    gi = lax.broadcasted_iota(I32, (N_EXPERT_GROUPS, tn), 0).astype(F32)
    gscore = jnp.zeros((N_EXPERT_GROUPS, tn), F32)
    for g in range(N_EXPERT_GROUPS):
        grp = biased[g * per_g:(g + 1) * per_g, :]
        m1, f1 = _first_max(grp, gi)
        m2 = jnp.max(jnp.where(gi == f1, -jnp.inf, grp), axis=0, keepdims=True)
        gscore = jnp.where(gi == g, m1 + m2, gscore)
    gsel = jnp.zeros((N_EXPERT_GROUPS, tn), F32)
    for _ in range(TOPK_EXPERT_GROUPS):
        _, f = _first_max(gscore, gi)
        hit = gi == f
        gsel = jnp.where(hit, 1.0, gsel)
        gscore = jnp.where(hit, -jnp.inf, gscore)
    ei = lax.broadcasted_iota(I32, (N_EXPERTS, tn), 0).astype(F32)
    emask = jnp.zeros((N_EXPERTS, tn), F32)
    for g in range(N_EXPERT_GROUPS):
        in_g = (ei >= g * per_g) & (ei < (g + 1) * per_g)
        emask = jnp.where(in_g, gsel[g:g + 1, :], emask)
    masked = jnp.where(emask > 0.5, biased, NEG)
    e_out = jnp.zeros((MOE_TOPK, tn), F32)
    w_out = jnp.zeros((MOE_TOPK, tn), F32)
    for r in range(MOE_TOPK):
        _, f = _first_max(masked, ei)
        hit = ei == f
        wv = jnp.sum(jnp.where(hit, scores, 0.0), axis=0, keepdims=True)
        e_out = jnp.where(gi == r, f, e_out)
        w_out = jnp.where(gi == r, wv, w_out)
        masked = jnp.where(hit, -jnp.inf, masked)
    e_ref[...] = e_out.astype(I32)
    w_ref[...] = w_out / jnp.sum(w_out, axis=0, keepdims=True) * ROUTED_SCALE


def _route(lgT, rb):
    E, N = lgT.shape
    tn = 512
    return pl.pallas_call(
        _route_kernel,
        grid=(N // tn,),
        in_specs=[pl.BlockSpec((E, tn), lambda i: (0, i)), pl.BlockSpec((E, 1), lambda i: (0, 0))],
        out_specs=[pl.BlockSpec((MOE_TOPK, tn), lambda i: (0, i)), pl.BlockSpec((MOE_TOPK, tn), lambda i: (0, i))],
        out_shape=[jax.ShapeDtypeStruct((MOE_TOPK, N), I32), jax.ShapeDtypeStruct((MOE_TOPK, N), F32)],
        compiler_params=_cparams(("parallel",)),
        name="route",
    )(lgT, rb.reshape(E, 1).astype(F32))


def _slots_kernel(e_ref, tri_ref, lt_ref, dest_ref, cnt_ref, carry_ref, *, rb):
    phase = pl.program_id(0)
    i = pl.program_id(1)
    e = e_ref[...]
    tn = e.shape[1]
    ei = lax.broadcasted_iota(I32, (N_EXPERTS, tn), 0)

    @pl.when((phase == 0) & (i == 0))
    def _():
        carry_ref[...] = jnp.zeros_like(carry_ref)

    @pl.when(phase == 0)
    def _():
        tot = jnp.zeros((N_EXPERTS, 1), F32)
        for k in range(MOE_TOPK):
            oh = jnp.where(ei == e[k:k + 1, :], 1.0, 0.0)
            tot = tot + jnp.sum(oh, axis=1, keepdims=True)
        carry_ref[...] = carry_ref[...] + tot
        dest_ref[...] = jnp.zeros_like(dest_ref)

    @pl.when((phase == 1) & (i == 0))
    def _():
        cnt = carry_ref[...]
        cnt_ref[...] = cnt
        padded = jnp.broadcast_to(jnp.ceil(cnt / rb) * rb, (N_EXPERTS, 128))
        first = jnp.dot(lt_ref[...], padded, preferred_element_type=F32, precision=HIGHEST)
        carry_ref[...] = first[:, :1]

    @pl.when(phase == 1)
    def _():
        carry = carry_ref[...]
        ki = lax.broadcasted_iota(I32, (MOE_TOPK, tn), 0)
        dest = jnp.zeros((MOE_TOPK, tn), F32)
        for k in range(MOE_TOPK):
            hit = ei == e[k:k + 1, :]
            oh = jnp.where(hit, 1.0, 0.0)
            before = jnp.dot(oh.astype(BF16), tri_ref[...], preferred_element_type=F32)
            row = jnp.sum(jnp.where(hit, before + carry, 0.0), axis=0, keepdims=True)
            dest = jnp.where(ki == k, row, dest)
            carry = carry + jnp.sum(oh, axis=1, keepdims=True)
        carry_ref[...] = carry
        dest_ref[...] = dest.astype(I32)


def _slots(top_eT, rb):
    K, N = top_eT.shape
    tn = 512
    idx = np.arange(tn)
    tri = jnp.asarray((idx[:, None] < idx[None, :]).astype(np.float32), BF16)
    e_idx = np.arange(N_EXPERTS)
    lt = jnp.asarray((e_idx[None, :] < e_idx[:, None]).astype(np.float32), F32)
    return pl.pallas_call(
        functools.partial(_slots_kernel, rb=rb),
        grid=(2, N // tn),
        in_specs=[pl.BlockSpec((K, tn), lambda p, i: (0, i)),
                  pl.BlockSpec((tn, tn), lambda p, i: (0, 0)),
                  pl.BlockSpec((N_EXPERTS, N_EXPERTS), lambda p, i: (0, 0))],
        out_specs=[pl.BlockSpec((K, tn), lambda p, i: (0, i * p)),
                   pl.BlockSpec((N_EXPERTS, 1), lambda p, i: (0, 0))],
        out_shape=[jax.ShapeDtypeStruct((K, N), I32), jax.ShapeDtypeStruct((N_EXPERTS, 1), F32)],
        scratch_shapes=[pltpu.VMEM((N_EXPERTS, 1), F32)],
        compiler_params=_cparams(("arbitrary", "arbitrary")),
        name="slots",
    )(top_eT, tri, lt)


def _sc_scatter_rows(src, idx, n_out, repeat):
    n_idx = idx.shape[0]
    width = src.shape[1]
    w = src.shape[0] // SC_SPLIT // SC_WINDOW
    mesh = plsc.VectorSubcoreMesh(core_axis_name="c", subcore_axis_name="s")

    @pl.kernel(out_type=jax.ShapeDtypeStruct((n_out, width), src.dtype), mesh=mesh)
    def scatter_kernel(x_hbm, i_hbm, o_hbm):
        def body(x_vmem, i_vmem):
            pltpu.sync_copy(x_vmem, o_hbm.at[i_vmem.at[0]])

        pltpu.emit_pipeline(
            body,
            grid=(n_idx // SC_WINDOW,),
            in_specs=[pl.BlockSpec((SC_WINDOW, width), lambda g: ((g // (repeat * w)) * w + g % w, 0)),
                      pl.BlockSpec((1, SC_WINDOW), lambda g: (0, g))],
            out_specs=[],
            core_axis_name=("c", "s"),
            dimension_semantics=(pltpu.PARALLEL,),
        )(x_hbm, i_hbm)

    return scatter_kernel(src, idx.reshape(1, n_idx))


def _zero_pads_kernel(start_ref, len_ref, xs_in_ref, xs_ref, zeros_ref, sem, *, rb, n_rows):
    del xs_in_ref
    zeros_ref[...] = jnp.zeros_like(zeros_ref)
    n_seg = start_ref.shape[0]
    bits = rb.bit_length() - 1
    assert rb == 1 << bits

    def for_each_copy(act):
        def rows(row0, size):
            for s in range(SC_SPLIT):
                act(pltpu.make_async_copy(zeros_ref.at[pl.ds(0, size), :],
                                          xs_ref.at[pl.ds(s * n_rows + row0, size), :], sem))

        def segment(e, c):
            start = start_ref[e]
            length = len_ref[e]
            end = start + length
            n_whole = length // rb

            def whole(j, c2):
                rows(pl.multiple_of(end - (j + 1) * rb, SUBLANES), rb)
                return c2

            lax.fori_loop(0, n_whole, whole, 0)
            off = end - n_whole * rb
            rest = length - n_whole * rb
            for b in reversed(range(SUBLANES.bit_length() - 1, bits)):
                size = 1 << b
                take = jnp.bitwise_and(jnp.right_shift(rest, b), 1)
                off = off - take * size

                @pl.when(take == 1)
                def _(off=off, size=size):
                    rows(pl.multiple_of(off, SUBLANES), size)

            lead = jnp.bitwise_and(rest, SUBLANES - 1)
            for i in range(SUBLANES - 1):
                @pl.when(i < lead)
                def _(i=i):
                    rows(start + i, 1)

            return c

        lax.fori_loop(0, n_seg, segment, 0)

    for_each_copy(lambda cp: cp.start())
    for_each_copy(lambda cp: cp.wait())


def _zero_pads(xs2d, start, length, rb, n_rows):
    grid_spec = pltpu.PrefetchScalarGridSpec(
        num_scalar_prefetch=2,
        grid=(1,),
        in_specs=[pl.BlockSpec(memory_space=pl.ANY)],
        out_specs=pl.BlockSpec(memory_space=pl.ANY),
        scratch_shapes=[pltpu.VMEM((rb, xs2d.shape[1]), xs2d.dtype), pltpu.SemaphoreType.DMA(())],
    )
    return pl.pallas_call(
        functools.partial(_zero_pads_kernel, rb=rb, n_rows=n_rows),
        grid_spec=grid_spec,
        out_shape=jax.ShapeDtypeStruct(xs2d.shape, xs2d.dtype),
        input_output_aliases={2: 0},
        compiler_params=_cparams(("arbitrary",)),
        name="zeropads",
    )(start, length, xs2d)


def _experts_kernel(be_ref, x_ref, wg_ref, wu_ref, wd_ref, y_ref, wgu_sc, wd_sc):
    i = pl.program_id(0)
    hdim = wg_ref.shape[1]

    @pl.when((i == 0) | (be_ref[i] != be_ref[jnp.maximum(i - 1, 0)]))
    def _():
        wgu_sc[:, 0:hdim] = wg_ref[...].astype(BF16)
        wgu_sc[:, hdim:] = wu_ref[...].astype(BF16)
        wd_sc[...] = wd_ref[...].astype(BF16)

    half = x_ref.shape[1] // 2
    for r in range(2):
        rows = slice(r * half, (r + 1) * half)
        gu = jnp.dot(_load_slabs_bf16(x_ref, rows), wgu_sc[...], preferred_element_type=F32)
        g = gu[:, :hdim]
        act = (g * jax.nn.sigmoid(g) * gu[:, hdim:]).astype(BF16)
        _store_slabs(y_ref, jnp.dot(act, wd_sc[...], preferred_element_type=F32), rows)


def _experts(blk_e, xs, wg, wu, wd, rb):
    n_rows = xs.shape[1]
    d_model, hdim = wg.shape[1:]
    per_expert = lambda i, be: (be[i], 0, 0)
    grid_spec = pltpu.PrefetchScalarGridSpec(
        num_scalar_prefetch=1,
        grid=(n_rows // rb,),
        in_specs=[pl.BlockSpec((SC_SPLIT, rb, SC_PIECE), lambda i, be: (0, i, 0)),
                  pl.BlockSpec((None, d_model, hdim), per_expert),
                  pl.BlockSpec((None, d_model, hdim), per_expert),
                  pl.BlockSpec((None, hdim, d_model), per_expert)],
        out_specs=pl.BlockSpec((SC_SPLIT, rb, SC_PIECE), lambda i, be: (0, i, 0)),
        scratch_shapes=[pltpu.VMEM((d_model, 2 * hdim), BF16), pltpu.VMEM((hdim, d_model), BF16)],
    )
    return pl.pallas_call(
        _experts_kernel,
        grid_spec=grid_spec,
        out_shape=jax.ShapeDtypeStruct((SC_SPLIT, n_rows, SC_PIECE), U32),
        compiler_params=_cparams(("arbitrary",)),
        name="experts",
    )(blk_e, xs, wg, wu, wd)


def _sc_gather_rows(table, idx):
    n = idx.shape[0]
    width = table.shape[1]
    mesh = plsc.VectorSubcoreMesh(core_axis_name="c", subcore_axis_name="s")

    @pl.kernel(out_type=jax.ShapeDtypeStruct((n, width), table.dtype), mesh=mesh)
    def gather_kernel(t_hbm, i_hbm, o_hbm):
        def body(i_vmem, o_vmem):
            pltpu.sync_copy(t_hbm.at[i_vmem.at[0]], o_vmem)

        pltpu.emit_pipeline(
            body,
            grid=(n // SC_WINDOW,),
            in_specs=[pl.BlockSpec((1, SC_WINDOW), lambda i: (0, i))],
            out_specs=[pl.BlockSpec((SC_WINDOW, width), lambda i: (i, 0))],
            core_axis_name=("c", "s"),
            dimension_semantics=(pltpu.PARALLEL,),
        )(i_hbm, o_hbm)

    return gather_kernel(table, idx.reshape(1, n))


def _combine_kernel(yg_ref, w_ref, h_ref, x1_ref, g2_ref, fw_ref, sgu_ref, sd_ref, o_ref):
    gu = jnp.dot(_load_slabs_bf16(h_ref), sgu_ref[...], preferred_element_type=F32)
    hdim = gu.shape[1] // 2
    g = gu[:, :hdim]
    ffn = jnp.dot((g * jax.nn.sigmoid(g) * gu[:, hdim:]).astype(BF16), sd_ref[...], preferred_element_type=F32)

    w = w_ref[...]
    tt = w.shape[0]
    lo = [jnp.zeros((tt, SC_PIECE), F32) for _ in range(SC_SPLIT)]
    hi = [jnp.zeros((tt, SC_PIECE), F32) for _ in range(SC_SPLIT)]
    for k in range(MOE_TOPK):
        wk = w[:, k:k + 1]
        for s in range(SC_SPLIT):
            y_lo, y_hi = _unpack_halves(yg_ref[s, k])
            lo[s] = lo[s] + wk * y_lo
            hi[s] = hi[s] + wk * y_hi
    ffn = ffn + jnp.concatenate(lo + hi, axis=1)
    x2 = x1_ref[...] + g2_ref[...] * ffn
    r = lax.rsqrt(jnp.mean(x2 * x2, axis=-1, keepdims=True) + NORM_EPS)
    o_ref[...] = x2 * r * fw_ref[...]


def _combine(yg, top_w, h2p, x1, g2, fw, sgu, sd, S, tt):
    N, D = x1.shape
    per_b = S // tt
    row = lambda i: (i, 0)
    full = lambda i: (0, 0)
    return pl.pallas_call(
        _combine_kernel,
        grid=(N // tt,),
        in_specs=[pl.BlockSpec((SC_SPLIT, MOE_TOPK, tt, SC_PIECE), lambda i: (0, 0, i, 0)),
                  pl.BlockSpec((tt, MOE_TOPK), row),
                  pl.BlockSpec((SC_SPLIT, tt, SC_PIECE), lambda i: (0, i, 0)),
                  pl.BlockSpec((tt, D), row),
                  pl.BlockSpec((None, 1, D), lambda i: (i // per_b, 0, 0)),
                  pl.BlockSpec((1, D), full),
                  pl.BlockSpec(sgu.shape, full),
                  pl.BlockSpec(sd.shape, full)],
        out_specs=pl.BlockSpec((tt, D), row),
        out_shape=jax.ShapeDtypeStruct((N, D), F32),
        compiler_params=_cparams(("parallel",)),
        name="combine",
    )(yg, top_w, h2p, x1, g2.reshape(-1, 1, D), fw.reshape(1, D), sgu, sd)


def _cmp_to_sel_T(n_cp, n_sel):
    c0 = np.arange(n_cp)[None, :] * CMP_STRIDE
    s0 = np.arange(n_sel)[:, None] * SEL_BLOCK
    ov = np.minimum(c0 + CMP_BLOCK, s0 + SEL_BLOCK) - np.maximum(c0, s0)
    m = np.clip(ov, 0, None).astype(np.float32) / CMP_BLOCK
    m[:, n_cp - 1] = 0.0
    return jnp.asarray(m, BF16)


def _slab_weights(w_in):
    D = w_in.shape[0]
    sizes = [1024, 1024, 1024, 512, 128, 128, 128, 128, 128, 128, 24, 2048]
    offs = np.concatenate([[0], np.cumsum(sizes)])
    part = lambda i: w_in[:, offs[i]:offs[i + 1]]
    pieces = [part(0), part(1), part(2), part(11), part(3)] + [part(i) for i in range(4, 10)] + [part(10)]
    w = jnp.concatenate(pieces, axis=1)
    return jnp.pad(w, ((0, 0), (0, SLAB_COLS - w.shape[1]))).astype(BF16)


def _layer(x, c, ada_w, ada_b, norm1_w, w_in, da_lq1, da_lk1, da_lq2, da_lk2, da_subln_w,
           cmp_k_pe, cmp_k_w1, cmp_k_w2, cmp_v_pe, cmp_v_w1, cmp_v_w2, w_da_out, w_nsa_out, w_o,
           norm2_w, router_w, router_b, exp_w_gate, exp_w_up, exp_w_down,
           sh_w_gate, sh_w_up, sh_w_down, final_norm_w, lam_init):
    B, S, D = x.shape
    N = B * S
    G, HPG = NSA_GROUPS, NSA_HPG
    x2 = x.reshape(N, D)

    mod = _ada(c, ada_w, ada_b)
    sh1, sc1, g1, sh2, sc2, g2 = jnp.split(mod, 6, axis=-1)

    slab = _inproj(x2, norm1_w, sh1, sc1, _slab_weights(w_in), S)
    slab3 = slab.reshape(B, S, SLAB_COLS)

    i_all = np.arange(1, DA_HEADS + NSA_HEADS + 1, dtype=np.float32)
    slopes = (2.0 ** (-8.0 * i_all / (DA_HEADS + NSA_HEADS))).astype(np.float32)
    slopes_a = jnp.asarray(slopes[0::2])
    slopes_b = jnp.asarray(slopes[1::2])

    ta_q, ta_k = min(512, S), min(512, S)
    qT = (slab3[:, :, COL_DAQ:COL_DAQ + 1024] * jnp.asarray(DA_DIM ** -0.5, BF16)).transpose(0, 2, 1)
    vT = slab3[:, :, COL_DAV:COL_DAV + 1024].reshape(B, S // ta_k, ta_k, DA_HEADS, DA_VDIM).transpose(0, 3, 1, 4, 2)
    kmin = _first_live_tile(slopes[0::2], slab3, ta_q, ta_k)
    oaT = _diff_attention(slopes_a, kmin, qT, slab3, _with_ones_row(vT), _slope_rows(slopes[0::2]), da_lq1, da_lk1,
                          da_lq2, da_lk2, da_subln_w, lam_init, ta_q, ta_k)
    oa = oaT.transpose(0, 2, 1).reshape(N, DA_HEADS * DA_VDIM)

    tb, tb_k = 256, min(256, S)
    n_cp = S // CMP_STRIDE
    n_sel = S // SEL_BLOCK

    def kv_groups(j):
        c0 = COL_KV6 + 128 * j
        return slab3[:, :, c0:c0 + 128].reshape(B, S, G, NSA_DIM).transpose(0, 2, 1, 3)

    def kv_tiles_T(a, rows):
        return a.reshape(B, G, S // rows, rows, NSA_DIM).transpose(0, 1, 2, 4, 3)

    ck_rows = kv_groups(0).reshape(B * G, n_cp, CMP_STRIDE * NSA_DIM)
    cv_rows = kv_groups(1).reshape(B * G, n_cp, CMP_STRIDE * NSA_DIM)
    kc = _compress(ck_rows, cmp_k_pe, cmp_k_w1, cmp_k_w2).reshape(B, G, n_cp, NSA_DIM).astype(BF16)
    vc = _compress(cv_rows, cmp_v_pe, cmp_v_w1, cmp_v_w2).reshape(B, G, n_cp, NSA_DIM).astype(BF16)
    nq = (slab3[:, :, COL_NSAQ:COL_NSAQ + 512] * jnp.asarray(NSA_DIM ** -0.5, BF16))
    nqT = nq.reshape(B, S // tb, tb, G, HPG, NSA_DIM).transpose(0, 3, 1, 5, 4, 2).reshape(B, G, S // tb, NSA_DIM, HPG * tb)
    gT = (slab3[:, :, COL_NSAG:COL_NSAG + 3 * NSA_HEADS].astype(F32).reshape(B, S // tb, tb, G, HPG, 3)
          .transpose(0, 3, 1, 5, 4, 2).reshape(B, G, S // tb, 3, HPG * tb))
    onehot = jnp.asarray((np.arange(S)[:, None] // SEL_BLOCK == np.arange(SEL_COLS)[None, :]).astype(np.float32), BF16)
    zeros64 = jnp.zeros((B, G, S, NSA_DIM), BF16)
    ksa = jnp.concatenate([kv_groups(2), zeros64, jnp.broadcast_to(onehot, (B, G, S, SEL_COLS))], axis=-1)
    pad_rows = jnp.zeros((B, G, WINDOW, 2 * NSA_DIM), BF16).at[..., NSA_DIM].set(1.0)
    kwa = jnp.concatenate([pad_rows, jnp.concatenate([kv_groups(4), zeros64], axis=-1)], axis=2)
    vw_pad = jnp.pad(kv_groups(5), ((0, 0), (0, 0), (WINDOW, 0), (0, 0)))
    vwT = vw_pad.reshape(B, G, (S + WINDOW) // tb, tb, NSA_DIM).transpose(0, 1, 2, 4, 3)
    r_w = np.arange(WINDOW + tb)[:, None]
    lane_w = np.arange(HPG * tb)[None, :]
    d_w = (WINDOW + lane_w % tb - r_w).astype(np.float32)
    slope_w = slopes[1::2].reshape(G, 1, HPG)[:, :, lane_w[0] // tb]
    wb = jnp.asarray(np.where((d_w >= 0) & (d_w < WINDOW), -slope_w * d_w[None], np.float32(NEG)).astype(np.float32))
    obT = _nsa_attention(slopes_b, nqT, kc, vc.transpose(0, 1, 3, 2), ksa, kv_tiles_T(kv_groups(3), tb_k),
                         kwa, vwT, wb, gT, _cmp_to_sel_T(n_cp, n_sel), tb, tb_k)
    ob = (obT.reshape(B, G, S // tb, NSA_DIM, HPG, tb).transpose(0, 2, 5, 1, 4, 3)
          .reshape(N, NSA_HEADS * NSA_DIM))

    wr_hi = router_w.astype(BF16)
    wr_lo = (router_w.astype(F32) - wr_hi.astype(F32)).astype(BF16)
    x1, h2, logits = _merge(oa, ob, slab, x2, g1, norm2_w, sh2, sc2, w_da_out.astype(BF16),
                            w_nsa_out.astype(BF16), w_o.astype(BF16), jnp.stack([wr_hi, wr_lo]), S)

    rb = 512
    tt = 128
    top_eT, top_wT = _route(logits.T, router_b)
    destT, counts = _slots(top_eT, rb)
    n_rows = ((N * MOE_TOPK + N_EXPERTS * (rb - 1) + rb - 1) // rb) * rb
    padded = (jnp.ceil(counts[:, 0] / rb) * rb).astype(I32)
    pend = jnp.cumsum(padded)
    blk_start = jnp.arange(n_rows // rb, dtype=I32) * rb
    blk_e = jnp.minimum(jnp.sum((pend[None, :] <= blk_start[:, None]).astype(I32), axis=1), N_EXPERTS - 1)
    d_flat = destT.reshape(-1)
    piece_idx = jnp.concatenate([d_flat + s * n_rows for s in range(SC_SPLIT)])
    xs2d = _sc_scatter_rows(h2.reshape(SC_SPLIT * N, SC_PIECE), piece_idx, SC_SPLIT * n_rows, MOE_TOPK)
    cnt = counts[:, 0].astype(I32)
    pad_start = jnp.concatenate([pend - padded + cnt, pend[-1:]])
    pad_len = jnp.concatenate([padded - cnt, n_rows - pend[-1:]])
    xs = _zero_pads(xs2d, pad_start, pad_len, rb, n_rows).reshape(SC_SPLIT, n_rows, SC_PIECE)
    ys = _experts(blk_e, xs, exp_w_gate, exp_w_up, exp_w_down, rb)
    sgu = jnp.concatenate([sh_w_gate, sh_w_up], axis=-1).astype(BF16)
    yg = _sc_gather_rows(ys.reshape(SC_SPLIT * n_rows, SC_PIECE), piece_idx).reshape(SC_SPLIT, MOE_TOPK, N, SC_PIECE)
    out = _combine(yg, top_wT.T, h2, x1, g2, final_norm_w, sgu, sh_w_down.astype(BF16), S, 2 * tt)
    return out.reshape(B, S, D)


def kernel(x, c, ada_w, ada_b, norm1_w, w_in, da_lq1, da_lk1, da_lq2, da_lk2, da_subln_w, cmp_k_pe, cmp_k_w1, cmp_k_w2, cmp_v_pe, cmp_v_w1, cmp_v_w2, w_da_out, w_nsa_out, w_o, norm2_w, router_w, router_b, exp_w_gate, exp_w_up, exp_w_down, sh_w_gate, sh_w_up, sh_w_down, final_norm_w):
    depth = ada_w.shape[0]
    assert depth == 1, "one decoder layer"
    lam_init = 0.8 - 0.6 * math.exp(-0.3 * 0)
    return _layer(x, c, ada_w[0], ada_b[0], norm1_w[0], w_in[0], da_lq1[0], da_lk1[0], da_lq2[0], da_lk2[0],
                  da_subln_w[0], cmp_k_pe[0], cmp_k_w1[0], cmp_k_w2[0], cmp_v_pe[0], cmp_v_w1[0], cmp_v_w2[0],
                  w_da_out[0], w_nsa_out[0], w_o[0], norm2_w[0], router_w[0], router_b[0],
                  exp_w_gate[0], exp_w_up[0], exp_w_down[0], sh_w_gate[0], sh_w_up[0], sh_w_down[0],
                  final_norm_w, lam_init)
```

```python
import functools
import math

import numpy as np
import jax
import jax.numpy as jnp
from jax import lax
from jax.experimental import pallas as pl
from jax.experimental.pallas import tpu as pltpu
from jax.experimental.pallas import tpu_sc as plsc

F32 = jnp.float32
BF16 = jnp.bfloat16
I32 = jnp.int32
U32 = jnp.uint32
HIGHEST = lax.Precision.HIGHEST

NORM_EPS = 1e-6
NEG = -1e30

DA_HEADS = 8
DA_DIM = 64
DA_VDIM = 128
DA_SUBLN_EPS = 1e-5

NSA_HEADS = 8
NSA_GROUPS = 2
NSA_HPG = 4
NSA_DIM = 64
CMP_BLOCK = 32
CMP_STRIDE = 16
SEL_BLOCK = 64
SEL_TOPK = 16
WINDOW = 512
FORCE_BONUS = 1e4
SEL_COLS = 128

N_EXPERTS = 64
MOE_TOPK = 8
N_EXPERT_GROUPS = 8
TOPK_EXPERT_GROUPS = 4
ROUTED_SCALE = 2.5

VMEM_LIMIT_V7X = 56 * 1024 * 1024

COL_DAQ, COL_DAK, COL_DAV = 0, 1024, 2048
COL_MERGE = 3072
COL_NSAQ = 5120
COL_KV6 = 5632
COL_NSAG = 6400
SLAB_COLS = 6528


SUBLANES = 8
SC_WINDOW = 128
SC_SPLIT = 2
SC_PIECE = 256

POS_SPLIT = 16
POS_ROWS = 16
ONES_ROWS = 16


def _pos_features(rows):
    r = np.arange(rows)
    f = np.zeros((rows, 128), np.float32)
    f[:, 0:3] = (r // POS_SPLIT)[:, None]
    f[:, 3:6] = (r % POS_SPLIT)[:, None]
    return jnp.asarray(f, BF16)


def _slope_rows(slopes):
    s = np.asarray(slopes, np.float32)
    bf = lambda x: x.astype(BF16).astype(np.float32)
    p1 = bf(s)
    p2 = bf(s - p1)
    p3 = bf(s - p1 - p2)
    out = np.zeros((s.shape[0], POS_ROWS, 128), np.float32)
    for i, piece in enumerate((p1, p2, p3)):
        out[:, i, :] = POS_SPLIT * piece[:, None]
        out[:, 3 + i, :] = piece[:, None]
    return jnp.asarray(out)


def _with_ones_row(vT):
    shape = vT.shape[:-2] + (ONES_ROWS, vT.shape[-1])
    extra = jnp.zeros(shape, vT.dtype).at[..., 0, :].set(1.0)
    return jnp.concatenate([vT, extra], axis=-2)


def _cparams(sem, vmem=VMEM_LIMIT_V7X):
    return pltpu.CompilerParams(dimension_semantics=sem, vmem_limit_bytes=vmem)


def _ada_kernel(c_ref, w_ref, b_ref, o_ref):
    c = c_ref[...]
    ca = c * jax.nn.sigmoid(c)
    o_ref[...] = jnp.dot(ca, w_ref[...], preferred_element_type=F32, precision=HIGHEST) + b_ref[...]


def _ada(c, w, b):
    B, D = c.shape
    n_out = w.shape[1]
    rows = 8
    cp = jnp.zeros((rows, D), F32).at[:B].set(c)
    tn = 1024
    out = pl.pallas_call(
        _ada_kernel,
        grid=(n_out // tn,),
        in_specs=[pl.BlockSpec((rows, D), lambda j: (0, 0)),
                  pl.BlockSpec((D, tn), lambda j: (0, j)),
                  pl.BlockSpec((1, tn), lambda j: (0, j))],
        out_specs=pl.BlockSpec((rows, tn), lambda j: (0, j)),
        out_shape=jax.ShapeDtypeStruct((rows, n_out), F32),
        compiler_params=_cparams(("arbitrary",)),
        name="ada",
    )(cp, w, b.reshape(1, n_out))
    return out[:B]


def _modulate(x, nw, sh, sc):
    r = lax.rsqrt(jnp.mean(x * x, axis=-1, keepdims=True) + NORM_EPS)
    return (x * r) * nw * (1.0 + sc) + sh


def _inproj_kernel(x_ref, nw_ref, sh_ref, sc_ref, w_ref, o_ref, h_ref):
    @pl.when(pl.program_id(1) == 0)
    def _():
        h_ref[...] = _modulate(x_ref[...], nw_ref[...], sh_ref[...], sc_ref[...]).astype(BF16)

    o_ref[...] = jnp.dot(h_ref[...], w_ref[...], preferred_element_type=F32).astype(BF16)


def _inproj(x2, nw, sh, sc, w_slab, S):
    N, D = x2.shape
    tm = min(1024, S)
    tn = SLAB_COLS // 3
    per_b = S // tm
    return pl.pallas_call(
        _inproj_kernel,
        grid=(N // tm, SLAB_COLS // tn),
        in_specs=[pl.BlockSpec((tm, D), lambda i, j: (i, 0)),
                  pl.BlockSpec((1, D), lambda i, j: (0, 0)),
                  pl.BlockSpec((None, 1, D), lambda i, j: (i // per_b, 0, 0)),
                  pl.BlockSpec((None, 1, D), lambda i, j: (i // per_b, 0, 0)),
                  pl.BlockSpec((D, tn), lambda i, j: (0, j))],
        out_specs=pl.BlockSpec((tm, tn), lambda i, j: (i, j)),
        out_shape=jax.ShapeDtypeStruct((N, SLAB_COLS), BF16),
        scratch_shapes=[pltpu.VMEM((tm, D), BF16)],
        compiler_params=_cparams(("parallel", "arbitrary")),
        name="inproj",
    )(x2, nw.reshape(1, D), sh.reshape(-1, 1, D), sc.reshape(-1, 1, D), w_slab)


def _cmp_kernel(r_ref, pe_ref, w1a_ref, w1b_ref, w2_ref, o_ref):
    r = r_ref[...]
    n_rows = r.shape[0]
    a = jnp.dot(r, w1a_ref[...], preferred_element_type=F32)
    b = jnp.dot(r, w1b_ref[...], preferred_element_type=F32)
    pe = pe_ref[...]
    half = pe.shape[1] // 2
    bias = (jnp.dot(pe[:, :half], w1a_ref[...].astype(F32), preferred_element_type=F32, precision=HIGHEST)
            + jnp.dot(pe[:, half:], w1b_ref[...].astype(F32), preferred_element_type=F32, precision=HIGHEST))
    hid = a + pltpu.roll(b, n_rows - 1, 0) + bias
    act = 0.5 * hid * (1.0 + jnp.tanh(0.7978845608028654 * (hid + 0.044715 * hid * hid * hid)))
    o_ref[...] = jnp.dot(act, w2_ref[...], preferred_element_type=F32, precision=HIGHEST)


def _compress(rows, pe, w1, w2):
    BG, n_rows, width = rows.shape
    hidden = w1.shape[1]
    w1a = w1[:width].astype(BF16)
    w1b = w1[width:].astype(BF16)
    return pl.pallas_call(
        _cmp_kernel,
        grid=(BG,),
        in_specs=[pl.BlockSpec((None, n_rows, width), lambda i: (i, 0, 0)),
                  pl.BlockSpec((1, 2 * width), lambda i: (0, 0)),
                  pl.BlockSpec((width, hidden), lambda i: (0, 0)),
                  pl.BlockSpec((width, hidden), lambda i: (0, 0)),
                  pl.BlockSpec((hidden, NSA_DIM), lambda i: (0, 0))],
        out_specs=pl.BlockSpec((None, n_rows, NSA_DIM), lambda i: (i, 0, 0)),
        out_shape=jax.ShapeDtypeStruct((BG, n_rows, NSA_DIM), F32),
        compiler_params=_cparams(("parallel",)),
        name="cmp",
    )(rows, pe.reshape(1, -1).astype(F32), w1a, w1b, w2.astype(F32))


def _da_kernel(slopes_ref, kmin_ref, qT_ref, k_ref, vT_ref, pos_ref, rel_ref, srow_ref, lq1_ref, lk1_ref, lq2_ref,
               lk2_ref, subw_ref, oT_ref, acc_ref, qa_ref, sa_ref, sb_ref, pa_ref, pb_ref, *, tq, tk, lam_init):
    h = pl.program_id(1)
    qi = pl.program_id(2)
    slope = slopes_ref[h]
    q0 = qi * tq
    w = 2 * tq
    nk = k_ref.shape[0] // tk
    qT = qT_ref[...]
    row = lax.broadcasted_iota(I32, qT.shape, 0)
    zero = jnp.zeros_like(qT)
    qa_ref[0:2 * DA_DIM, :] = jnp.concatenate(
        [jnp.where(row < DA_DIM, qT, zero), jnp.where(row >= DA_DIM, qT, zero)], axis=1)
    qa_ref[2 * DA_DIM:2 * DA_DIM + POS_ROWS, :] = jnp.concatenate([srow_ref[...]] * (w // 128), axis=1).astype(BF16)
    qa_ref[2 * DA_DIM + POS_ROWS:, :] = jnp.zeros((2 * DA_DIM - POS_ROWS, w), BF16)

    def scores(kt, s_ref):
        k0 = kt * tk
        k_t = k_ref[pl.ds(pl.multiple_of(k0, tk), tk), :]
        s = jnp.dot(jnp.concatenate([k_t, pos_ref[...]], axis=1), qa_ref[...], preferred_element_type=F32)
        s = jnp.where(rel_ref[...] <= q0 - k0, s, NEG)
        s_ref[...] = s
        return jnp.max(s, axis=0, keepdims=True)

    def probs(s_ref, p_ref, mx, off, m_old):
        m_new = jnp.maximum(m_old, mx + off)
        p_ref[...] = jnp.exp(s_ref[...] - (m_new - off)).astype(BF16)
        return m_new, jnp.exp(m_old - m_new)

    def accumulate(kt, p_ref, alpha):
        acc_ref[...] = alpha * acc_ref[...] + jnp.dot(vT_ref[kt], p_ref[...], preferred_element_type=F32)

    n_tiles = (q0 + tq - 1) // tk + 1
    kt_min = kmin_ref[(pl.program_id(0) * pl.num_programs(1) + h) * pl.num_programs(2) + qi]
    acc_ref[...] = jnp.zeros_like(acc_ref)
    pb_ref[...] = jnp.zeros_like(pb_ref)
    mx_a = scores(kt_min, sa_ref)

    def pair(j, carry):
        m, mx_a, alpha_b = carry
        ta = kt_min + 2 * j
        tb = ta + 1
        off_a = slope * (ta * tk).astype(F32)
        off_b = jnp.where(tb < n_tiles, slope * (tb * tk).astype(F32), NEG)
        accumulate(jnp.maximum(ta - 1, 0), pb_ref, alpha_b)
        m, alpha_a = probs(sa_ref, pa_ref, mx_a, off_a, m)
        mx_b = scores(jnp.minimum(tb, nk - 1), sb_ref)
        accumulate(ta, pa_ref, alpha_a)
        m, alpha_b = probs(sb_ref, pb_ref, mx_b, off_b, m)
        mx_a = scores(jnp.minimum(ta + 2, nk - 1), sa_ref)
        return m, mx_a, alpha_b

    n_pairs = (n_tiles - kt_min + 1) // 2
    init = (jnp.full((1, w), NEG, F32), mx_a, jnp.ones((1, w), F32))
    _, _, alpha_b = lax.fori_loop(0, n_pairs, pair, init)
    accumulate(jnp.minimum(kt_min + 2 * n_pairs - 1, nk - 1), pb_ref, alpha_b)

    lam = (jnp.exp(jnp.sum(lq1_ref[...] * lk1_ref[...], axis=-1, keepdims=True))
           - jnp.exp(jnp.sum(lq2_ref[...] * lk2_ref[...], axis=-1, keepdims=True)) + lam_init)
    on = acc_ref[0:DA_VDIM, :] * (1.0 / acc_ref[DA_VDIM:DA_VDIM + 1, :])
    o = on[:, :tq] - lam * on[:, tq:]
    r = lax.rsqrt(jnp.mean(o * o, axis=0, keepdims=True) + DA_SUBLN_EPS)
    oT_ref[...] = (o * r * subw_ref[...] * (1.0 - lam_init)).astype(BF16)


EXP_UNDERFLOW = 104.0


def _first_live_tile(slopes_np, slab3, tq, tk):
    B, S, _ = slab3.shape
    H = DA_HEADS
    width = 2 * DA_DIM * H
    grp = jnp.asarray((np.arange(width)[:, None] // DA_DIM == np.arange(2 * H)[None, :]).astype(np.float32), BF16)

    def norms(c0):
        v = slab3[:, :, c0:c0 + width]
        return jnp.sqrt(jnp.einsum('bsc,cg->bsg', v * v, grp, preferred_element_type=F32))

    qn = norms(COL_DAQ) * DA_DIM ** -0.5
    qn = jnp.max(qn.reshape(B, S // tq, tq, H, 2), axis=2).transpose(0, 2, 3, 1)
    kn = jnp.max(norms(COL_DAK), axis=1).reshape(B, H, 2)
    bound = jnp.max(qn * kn[..., None], axis=2) * 1.01 + 0.01
    q0 = (jnp.arange(S // tq, dtype=F32) * tq)[None, None, :]
    slope = jnp.asarray(slopes_np, F32)[None, :, None]
    first_key = q0 - (EXP_UNDERFLOW + 2.0 * bound) / slope
    kt = jnp.floor(first_key / tk).astype(I32)
    n_tiles = (jnp.arange(S // tq, dtype=I32) * tq + tq - 1) // tk + 1
    return jnp.clip(kt, 0, n_tiles[None, None, :] - 1).reshape(-1)


def _diff_attention(slopes, kmin, qT, slab3, vT, srows, lq1, lk1, lq2, lk2, subw, lam_init, tq, tk):
    B, _, S = qT.shape
    nk = S // tk
    assert tk % tq == 0 and S % tk == 0
    tile = tq
    kcol = COL_DAK // 128
    v_rows = vT.shape[3]
    vec = lambda a: a.reshape(1, DA_DIM).astype(F32)
    grid_spec = pltpu.PrefetchScalarGridSpec(
        num_scalar_prefetch=2,
        grid=(B, DA_HEADS, S // tile),
        in_specs=[pl.BlockSpec((None, 128, tile), lambda b, h, i, s, km: (b, h, i)),
                  pl.BlockSpec((None, S, 128), lambda b, h, i, s, km: (b, 0, kcol + h)),
                  pl.BlockSpec((None, None, nk, v_rows, tk), lambda b, h, i, s, km: (b, h, 0, 0, 0)),
                  pl.BlockSpec((tk, 128), lambda b, h, i, s, km: (0, 0)),
                  pl.BlockSpec((tk, 2 * tq), lambda b, h, i, s, km: (0, 0)),
                  pl.BlockSpec((None, POS_ROWS, 128), lambda b, h, i, s, km: (h, 0, 0)),
                  pl.BlockSpec((1, DA_DIM), lambda b, h, i, s, km: (0, 0)),
                  pl.BlockSpec((1, DA_DIM), lambda b, h, i, s, km: (0, 0)),
                  pl.BlockSpec((1, DA_DIM), lambda b, h, i, s, km: (0, 0)),
                  pl.BlockSpec((1, DA_DIM), lambda b, h, i, s, km: (0, 0)),
                  pl.BlockSpec((DA_VDIM, 1), lambda b, h, i, s, km: (0, 0))],
        out_specs=pl.BlockSpec((None, 128, tile), lambda b, h, i, s, km: (b, h, i)),
        scratch_shapes=[pltpu.VMEM((v_rows, 2 * tq), F32), pltpu.VMEM((4 * DA_DIM, 2 * tq), BF16),
                        pltpu.VMEM((tk, 2 * tq), F32), pltpu.VMEM((tk, 2 * tq), F32),
                        pltpu.VMEM((tk, 2 * tq), BF16), pltpu.VMEM((tk, 2 * tq), BF16)],
    )
    rel = jnp.asarray(np.arange(tk)[:, None] - (np.arange(2 * tq) % tq)[None, :], I32)
    return pl.pallas_call(
        functools.partial(_da_kernel, tq=tq, tk=tk, lam_init=lam_init),
        grid_spec=grid_spec,
        out_shape=jax.ShapeDtypeStruct((B, DA_HEADS * DA_VDIM, S), BF16),
        compiler_params=_cparams(("parallel", "parallel", "arbitrary")),
        name="diffattn",
    )(slopes, kmin, qT, slab3, vT, _pos_features(tk), rel, srows, vec(lq1), vec(lk1), vec(lq2), vec(lk2),
      subw.reshape(DA_VDIM, 1).astype(F32))


def _nsa_kernel(slopes_ref, qT_ref, kc_ref, vcT_ref, ksa_ref, vsT_ref, kwa_ref, vwT_ref, wb_ref, gT_ref, mT_ref,
                grp_ref, ltri_ref, oT_ref, acc_ref, res_ref, base_ref, qa_ref, sa_ref, sb_ref, pa_ref, pb_ref,
                list_ref, slist_ref, lsem, *, t, tk, n_top):
    g = pl.program_id(1)
    qi = pl.program_id(2)
    q0 = qi * t
    w = NSA_HPG * t
    n_cp = kc_ref.shape[0]
    n_sel = mT_ref.shape[0]
    lane1 = lax.broadcasted_iota(I32, (1, w), 1)
    slope_row = jnp.zeros((1, w), F32)
    for hh in range(NSA_HPG):
        slope_row = jnp.where((lane1 >= hh * t) & (lane1 < (hh + 1) * t), slopes_ref[g * NSA_HPG + hh], slope_row)
    q_all = qT_ref[...]
    gates = jax.nn.sigmoid(gT_ref[...])
    key_off = lax.broadcasted_iota(I32, (tk, w), 0)
    t_loc = jnp.bitwise_and(lax.broadcasted_iota(I32, (tk, w), 1), t - 1)
    base_ref[...] = slope_row * key_off.astype(F32)

    c_idx = lax.broadcasted_iota(I32, (n_cp, w), 0)
    t_c = q0 + jnp.bitwise_and(lax.broadcasted_iota(I32, (n_cp, w), 1), t - 1)
    d_c = t_c - (c_idx * CMP_STRIDE + (CMP_BLOCK - 1))
    ok_c = d_c >= 0
    s = jnp.dot(kc_ref[...], q_all, preferred_element_type=F32) - slope_row * d_c.astype(F32)
    s = jnp.where(ok_c, s, NEG)
    e = jnp.exp(s - jnp.max(s, axis=0, keepdims=True))
    inv = 1.0 / jnp.sum(e, axis=0, keepdims=True)
    p = jnp.where(ok_c, e * inv, 0.0)
    res_ref[...] = gates[0:1, :] * jnp.dot(vcT_ref[...], p.astype(BF16), preferred_element_type=F32)
    p_sum = p[:, 0:t]
    for hh in range(1, NSA_HPG):
        p_sum = p_sum + p[:, hh * t:(hh + 1) * t]

    p_hi = p_sum.astype(BF16)
    p_lo = (p_sum - p_hi.astype(F32)).astype(BF16)
    mT = mT_ref[...]
    imp = jnp.dot(mT, p_hi, preferred_element_type=F32) + jnp.dot(mT, p_lo, preferred_element_type=F32)
    blk = lax.broadcasted_iota(I32, (n_sel, t), 0)
    t_s = q0 + lax.broadcasted_iota(I32, (n_sel, t), 1)
    cur = jnp.right_shift(t_s, int(math.log2(SEL_BLOCK)))
    forced = (blk == 0) | (blk == cur) | (blk == cur - 1)
    score = jnp.where(blk * SEL_BLOCK <= t_s, imp + jnp.where(forced, FORCE_BONUS, 0.0), NEG)
    blk_f = blk.astype(F32)
    sel = jnp.zeros((n_sel, t), F32)
    for _ in range(n_top):
        _, idx = _first_max(score, blk_f)
        hit = blk_f == idx
        sel = jnp.where(hit, 1.0, sel)
        score = jnp.where(hit, -jnp.inf, score)
    n_full = q0 // tk
    nt_pad = grp_ref.shape[0]
    cnt = jnp.broadcast_to(jnp.sum(sel, axis=1, keepdims=True), (n_sel, 128)).astype(BF16)
    tile_cnt = jnp.dot(grp_ref[...], cnt, preferred_element_type=F32)
    kt_i = lax.broadcasted_iota(I32, (nt_pad, 128), 0)
    active = (tile_cnt > 0.5) & (kt_i < n_full)
    act_f = jnp.where(active, 1.0, 0.0)
    before = jnp.dot(ltri_ref[...], act_f.astype(BF16), preferred_element_type=F32)
    lane_j = lax.broadcasted_iota(I32, (nt_pad, 128), 1)
    slot_hit = active & (before == lane_j.astype(F32))
    tiles_row = jnp.sum(jnp.where(slot_hit, kt_i.astype(F32), 0.0), axis=0, keepdims=True)
    n_act_row = jnp.sum(act_f, axis=0, keepdims=True)
    lane8 = lax.broadcasted_iota(I32, (8, 128), 1)
    list_ref[...] = jnp.where(lane8 == 127, n_act_row, tiles_row).astype(I32)
    list_copy = pltpu.make_async_copy(list_ref, slist_ref, lsem)
    list_copy.start()

    selb = jnp.where(sel > 0.5, 0.0, NEG)
    row64 = lax.broadcasted_iota(I32, (NSA_DIM, w), 0)
    qa_ref[0:NSA_DIM, :] = q_all
    qa_ref[NSA_DIM:2 * NSA_DIM, :] = jnp.where(row64 == 0, NEG, 0.0).astype(BF16)
    qa_ref[2 * NSA_DIM:2 * NSA_DIM + n_sel, :] = jnp.concatenate([selb] * NSA_HPG, axis=1).astype(BF16)
    if n_sel < SEL_COLS:
        qa_ref[2 * NSA_DIM + n_sel:, :] = jnp.zeros((SEL_COLS - n_sel, w), BF16)

    n_wt = (WINDOW + t) // t
    kw_t = kwa_ref[pl.ds(pl.multiple_of(q0, t), WINDOW + t), :]
    s = jnp.dot(kw_t, qa_ref[0:2 * NSA_DIM, :], preferred_element_type=F32) + wb_ref[...]
    p = jnp.exp(s - jnp.max(s, axis=0, keepdims=True))
    inv = 1.0 / jnp.sum(p, axis=0, keepdims=True)
    p = p.astype(BF16)
    o_w = jnp.dot(vwT_ref[qi], p[0:t, :], preferred_element_type=F32)
    for c in range(1, n_wt):
        o_w = o_w + jnp.dot(vwT_ref[qi + c], p[c * t:(c + 1) * t, :], preferred_element_type=F32)
    res_ref[...] = res_ref[...] + gates[2:3, :] * (o_w * inv)

    def scores(kt, s_ref):
        k_t = ksa_ref[pl.ds(pl.multiple_of(kt * tk, tk), tk), :]
        s = jnp.dot(k_t, qa_ref[...], preferred_element_type=F32) + base_ref[...]
        s_ref[...] = s
        return jnp.max(s, axis=0, keepdims=True)

    def probs(s_ref, p_ref, mx, off, m_old, l_old):
        m_new = jnp.maximum(m_old, mx + off)
        p = jnp.exp(s_ref[...] - (m_new - off))
        alpha = jnp.exp(m_old - m_new)
        p_ref[...] = p.astype(BF16)
        return m_new, alpha * l_old + jnp.sum(p, axis=0, keepdims=True), alpha

    def accumulate(kt, p_ref, alpha):
        acc_ref[...] = alpha * acc_ref[...] + jnp.dot(vsT_ref[kt], p_ref[...], preferred_element_type=F32)

    list_copy.wait()
    n_act = slist_ref[0, 127]
    mx_a = scores(slist_ref[0, 0], sa_ref)

    kd = n_full * tk
    s = jnp.dot(ksa_ref[pl.ds(pl.multiple_of(kd, tk), tk), :], qa_ref[...], preferred_element_type=F32) + base_ref[...]
    s = jnp.where(key_off - t_loc <= q0 - kd, s, NEG)
    off_d = slope_row * kd.astype(F32)
    m = jnp.max(s, axis=0, keepdims=True) + off_d
    p = jnp.exp(s - (m - off_d))
    l = jnp.sum(p, axis=0, keepdims=True)
    acc_ref[...] = jnp.dot(vsT_ref[n_full], p.astype(BF16), preferred_element_type=F32)
    pb_ref[...] = jnp.zeros_like(pb_ref)

    def pair(j, carry):
        m, l, mx_a, alpha_b, tb_prev = carry
        ta = slist_ref[0, 2 * j]
        tb = slist_ref[0, 2 * j + 1]
        off_a = slope_row * (ta * tk).astype(F32)
        off_b = jnp.where(2 * j + 1 < n_act, slope_row * (tb * tk).astype(F32), NEG)
        accumulate(tb_prev, pb_ref, alpha_b)
        m, l, alpha_a = probs(sa_ref, pa_ref, mx_a, off_a, m, l)
        mx_b = scores(tb, sb_ref)
        accumulate(ta, pa_ref, alpha_a)
        m, l, alpha_b = probs(sb_ref, pb_ref, mx_b, off_b, m, l)
        mx_a = scores(slist_ref[0, 2 * j + 2], sa_ref)
        return m, l, mx_a, alpha_b, tb

    n_pairs = (n_act + 1) // 2
    m, l, _, alpha_b, tb_last = lax.fori_loop(0, n_pairs, pair, (m, l, mx_a, jnp.ones((1, w), F32), jnp.int32(0)))
    accumulate(tb_last, pb_ref, alpha_b)
    oT_ref[...] = (res_ref[...] + gates[1:2, :] * (acc_ref[...] * (1.0 / l))).astype(BF16)


def _nsa_attention(slopes, qT, kc, vcT, ksa, vsT, kwa, vwT, wb, gT, mT, t, tk):
    B, G, nq, _, w = qT.shape
    S = nq * t
    n_cp = kc.shape[2]
    n_sel = S // SEL_BLOCK
    n_top = min(SEL_TOPK, n_sel)
    assert tk % t == 0 and S % tk == 0 and t & (t - 1) == 0 and WINDOW % t == 0 and n_sel <= SEL_COLS
    n_tiles = S // tk
    nt_pad = -(-n_tiles // 16) * 16
    assert nt_pad + 2 < 127
    grp = jnp.asarray((np.arange(n_sel)[None, :] // (tk // SEL_BLOCK) == np.arange(nt_pad)[:, None])
                      .astype(np.float32), BF16)
    ltri = jnp.asarray((np.arange(nt_pad)[None, :] < np.arange(nt_pad)[:, None]).astype(np.float32), BF16)
    fixed = lambda b, g, i, s: (b, g, 0, 0)
    fixed5 = lambda b, g, i, s: (b, g, 0, 0, 0)
    tile5 = lambda b, g, i, s: (b, g, i, 0, 0)
    grid_spec = pltpu.PrefetchScalarGridSpec(
        num_scalar_prefetch=1,
        grid=(B, G, nq),
        in_specs=[pl.BlockSpec((None, None, None, NSA_DIM, w), tile5),
                  pl.BlockSpec((None, None, n_cp, NSA_DIM), fixed),
                  pl.BlockSpec((None, None, NSA_DIM, n_cp), fixed),
                  pl.BlockSpec((None, None, S, ksa.shape[-1]), fixed),
                  pl.BlockSpec((None, None, S // tk, NSA_DIM, tk), fixed5),
                  pl.BlockSpec((None, None, S + WINDOW, 2 * NSA_DIM), fixed),
                  pl.BlockSpec((None, None, (S + WINDOW) // t, NSA_DIM, t), fixed5),
                  pl.BlockSpec((None, WINDOW + t, w), lambda b, g, i, s: (g, 0, 0)),
                  pl.BlockSpec((None, None, None, 3, w), tile5),
                  pl.BlockSpec((n_sel, n_cp), lambda b, g, i, s: (0, 0)),
                  pl.BlockSpec((nt_pad, n_sel), lambda b, g, i, s: (0, 0)),
                  pl.BlockSpec((nt_pad, nt_pad), lambda b, g, i, s: (0, 0))],
        out_specs=pl.BlockSpec((None, None, None, NSA_DIM, w), tile5),
        scratch_shapes=[pltpu.VMEM((NSA_DIM, w), F32),
                        pltpu.VMEM((NSA_DIM, w), F32),
                        pltpu.VMEM((tk, w), F32),
                        pltpu.VMEM((2 * NSA_DIM + SEL_COLS, w), BF16),
                        pltpu.VMEM((tk, w), F32), pltpu.VMEM((tk, w), F32),
                        pltpu.VMEM((tk, w), BF16), pltpu.VMEM((tk, w), BF16),
                        pltpu.VMEM((8, 128), I32), pltpu.SMEM((8, 128), I32), pltpu.SemaphoreType.DMA(())],
    )
    return pl.pallas_call(
        functools.partial(_nsa_kernel, t=t, tk=tk, n_top=n_top),
        grid_spec=grid_spec,
        out_shape=jax.ShapeDtypeStruct((B, G, nq, NSA_DIM, w), BF16),
        compiler_params=_cparams(("parallel", "parallel", "arbitrary")),
        name="nsa",
    )(slopes, qT, kc, vcT, ksa, vsT, kwa, vwT, wb, gT, mT, grp, ltri)


def _pack_halves(v):
    c = v.shape[1] // 2
    lo = lax.bitcast_convert_type(v[:, :c].astype(BF16).astype(F32), U32)
    hi = lax.bitcast_convert_type(v[:, c:].astype(BF16).astype(F32), U32)
    return hi | (lo >> 16)


def _unpack_halves(u):
    lo = lax.bitcast_convert_type(u << 16, F32)
    hi = lax.bitcast_convert_type(u & jnp.uint32(0xFFFF0000), F32)
    return lo, hi


def _store_slabs(ref, v):
    packed = _pack_halves(v)
    for s in range(SC_SPLIT):
        ref[s] = packed[:, s * SC_PIECE:(s + 1) * SC_PIECE]


def _load_slabs_bf16(ref):
    parts = [_unpack_halves(ref[s]) for s in range(SC_SPLIT)]
    return jnp.concatenate([p[0].astype(BF16) for p in parts] + [p[1].astype(BF16) for p in parts], axis=1)


def _merge_kernel(oa_ref, ob_ref, ga_ref, gb_ref, x_ref, g1_ref, nw_ref, sh_ref, sc_ref,
                  wa_ref, wb_ref, wo_ref, wr_ref, x1_ref, h2_ref, lg_ref):
    ya = jnp.dot(oa_ref[...], wa_ref[...], preferred_element_type=F32)
    yb = jnp.dot(ob_ref[...], wb_ref[...], preferred_element_type=F32)
    merged = (jax.nn.sigmoid(ga_ref[...].astype(F32)) * ya
              + jax.nn.sigmoid(gb_ref[...].astype(F32)) * yb).astype(BF16)
    mix = jnp.dot(merged, wo_ref[...], preferred_element_type=F32)
    x1 = x_ref[...] + g1_ref[...] * mix
    x1_ref[...] = x1
    h2 = _modulate(x1, nw_ref[...], sh_ref[...], sc_ref[...])
    _store_slabs(h2_ref, h2)
    h_hi = h2.astype(BF16)
    h_lo = (h2 - h_hi.astype(F32)).astype(BF16)
    lg_ref[...] = (jnp.dot(h_hi, wr_ref[0], preferred_element_type=F32)
                   + jnp.dot(h_lo, wr_ref[0], preferred_element_type=F32)
                   + jnp.dot(h_hi, wr_ref[1], preferred_element_type=F32))


def _merge(oa, ob, slab, x2, g1, nw, sh, sc, wa, wb, wo, wr, S):
    N, D = x2.shape
    tm = 256
    per_b = S // tm
    mcol = COL_MERGE // D
    row = lambda i: (i, 0)
    full = lambda i: (0, 0)
    perb = lambda i: (i // per_b, 0, 0)
    n_lg = wr.shape[-1]
    return pl.pallas_call(
        _merge_kernel,
        grid=(N // tm,),
        in_specs=[pl.BlockSpec((tm, oa.shape[1]), row),
                  pl.BlockSpec((tm, ob.shape[1]), row),
                  pl.BlockSpec((tm, D), lambda i: (i, mcol)),
                  pl.BlockSpec((tm, D), lambda i: (i, mcol + 1)),
                  pl.BlockSpec((tm, D), row),
                  pl.BlockSpec((None, 1, D), perb),
                  pl.BlockSpec((1, D), full),
                  pl.BlockSpec((None, 1, D), perb),
                  pl.BlockSpec((None, 1, D), perb),
                  pl.BlockSpec(wa.shape, full),
                  pl.BlockSpec(wb.shape, full),
                  pl.BlockSpec(wo.shape, full),
                  pl.BlockSpec(wr.shape, lambda i: (0, 0, 0))],
        out_specs=[pl.BlockSpec((tm, D), row), pl.BlockSpec((SC_SPLIT, tm, SC_PIECE), lambda i: (0, i, 0)),
                   pl.BlockSpec((tm, n_lg), row)],
        out_shape=[jax.ShapeDtypeStruct((N, D), F32), jax.ShapeDtypeStruct((SC_SPLIT, N, SC_PIECE), U32),
                   jax.ShapeDtypeStruct((N, n_lg), F32)],
        compiler_params=_cparams(("parallel",)),
        name="merge",
    )(oa, ob, slab, slab, x2, g1.reshape(-1, 1, D), nw.reshape(1, D), sh.reshape(-1, 1, D),
      sc.reshape(-1, 1, D), wa, wb, wo, wr)


def _first_max(v, idx):
    mx = jnp.max(v, axis=0, keepdims=True)
    first = jnp.min(jnp.where(v == mx, idx, float(v.shape[0])), axis=0, keepdims=True)
    return mx, first


def _route_kernel(lg_ref, b_ref, e_ref, w_ref):
    lg = lg_ref[...]
    tn = lg.shape[1]
    per_g = N_EXPERTS // N_EXPERT_GROUPS
    scores = jax.nn.sigmoid(lg)
    biased = scores + b_ref[...]
    gi = lax.broadcasted_iota(I32, (N_EXPERT_GROUPS, tn), 0).astype(F32)
    gscore = jnp.zeros((N_EXPERT_GROUPS, tn), F32)
    for g in range(N_EXPERT_GROUPS):
        grp = biased[g * per_g:(g + 1) * per_g, :]
        m1, f1 = _first_max(grp, gi)
        m2 = jnp.max(jnp.where(gi == f1, -jnp.inf, grp), axis=0, keepdims=True)
        gscore = jnp.where(gi == g, m1 + m2, gscore)
    gsel = jnp.zeros((N_EXPERT_GROUPS, tn), F32)
    for _ in range(TOPK_EXPERT_GROUPS):
        _, f = _first_max(gscore, gi)
        hit = gi == f
        gsel = jnp.where(hit, 1.0, gsel)
        gscore = jnp.where(hit, -jnp.inf, gscore)
    ei = lax.broadcasted_iota(I32, (N_EXPERTS, tn), 0).astype(F32)
    emask = jnp.zeros((N_EXPERTS, tn), F32)
    for g in range(N_EXPERT_GROUPS):
        in_g = (ei >= g * per_g) & (ei < (g + 1) * per_g)
        emask = jnp.where(in_g, gsel[g:g + 1, :], emask)
    masked = jnp.where(emask > 0.5, biased, NEG)
    e_out = jnp.zeros((MOE_TOPK, tn), F32)
    w_out = jnp.zeros((MOE_TOPK, tn), F32)
    for r in range(MOE_TOPK):
        _, f = _first_max(masked, ei)
        hit = ei == f
        wv = jnp.sum(jnp.where(hit, scores, 0.0), axis=0, keepdims=True)
        e_out = jnp.where(gi == r, f, e_out)
        w_out = jnp.where(gi == r, wv, w_out)
        masked = jnp.where(hit, -jnp.inf, masked)
    e_ref[...] = e_out.astype(I32)
    w_ref[...] = w_out / jnp.sum(w_out, axis=0, keepdims=True) * ROUTED_SCALE


def _route(lgT, rb):
    E, N = lgT.shape
    tn = 512
    return pl.pallas_call(
        _route_kernel,
        grid=(N // tn,),
        in_specs=[pl.BlockSpec((E, tn), lambda i: (0, i)), pl.BlockSpec((E, 1), lambda i: (0, 0))],
        out_specs=[pl.BlockSpec((MOE_TOPK, tn), lambda i: (0, i)), pl.BlockSpec((MOE_TOPK, tn), lambda i: (0, i))],
        out_shape=[jax.ShapeDtypeStruct((MOE_TOPK, N), I32), jax.ShapeDtypeStruct((MOE_TOPK, N), F32)],
        compiler_params=_cparams(("parallel",)),
        name="route",
    )(lgT, rb.reshape(E, 1).astype(F32))


def _slots_kernel(e_ref, tri_ref, lt_ref, dest_ref, cnt_ref, carry_ref, *, rb):
    phase = pl.program_id(0)
    i = pl.program_id(1)
    e = e_ref[...]
    tn = e.shape[1]
    ei = lax.broadcasted_iota(I32, (N_EXPERTS, tn), 0)

    @pl.when((phase == 0) & (i == 0))
    def _():
        carry_ref[...] = jnp.zeros_like(carry_ref)

    @pl.when(phase == 0)
    def _():
        tot = jnp.zeros((N_EXPERTS, 1), F32)
        for k in range(MOE_TOPK):
            oh = jnp.where(ei == e[k:k + 1, :], 1.0, 0.0)
            tot = tot + jnp.sum(oh, axis=1, keepdims=True)
        carry_ref[...] = carry_ref[...] + tot
        dest_ref[...] = jnp.zeros_like(dest_ref)

    @pl.when((phase == 1) & (i == 0))
    def _():
        cnt = carry_ref[...]
        cnt_ref[...] = cnt
        padded = jnp.broadcast_to(jnp.ceil(cnt / rb) * rb, (N_EXPERTS, 128))
        first = jnp.dot(lt_ref[...], padded, preferred_element_type=F32, precision=HIGHEST)
        carry_ref[...] = first[:, :1]

    @pl.when(phase == 1)
    def _():
        carry = carry_ref[...]
        ki = lax.broadcasted_iota(I32, (MOE_TOPK, tn), 0)
        dest = jnp.zeros((MOE_TOPK, tn), F32)
        for k in range(MOE_TOPK):
            hit = ei == e[k:k + 1, :]
            oh = jnp.where(hit, 1.0, 0.0)
            before = jnp.dot(oh.astype(BF16), tri_ref[...], preferred_element_type=F32)
            row = jnp.sum(jnp.where(hit, before + carry, 0.0), axis=0, keepdims=True)
            dest = jnp.where(ki == k, row, dest)
            carry = carry + jnp.sum(oh, axis=1, keepdims=True)
        carry_ref[...] = carry
        dest_ref[...] = dest.astype(I32)


def _slots(top_eT, rb):
    K, N = top_eT.shape
    tn = 512
    idx = np.arange(tn)
    tri = jnp.asarray((idx[:, None] < idx[None, :]).astype(np.float32), BF16)
    e_idx = np.arange(N_EXPERTS)
    lt = jnp.asarray((e_idx[None, :] < e_idx[:, None]).astype(np.float32), F32)
    return pl.pallas_call(
        functools.partial(_slots_kernel, rb=rb),
        grid=(2, N // tn),
        in_specs=[pl.BlockSpec((K, tn), lambda p, i: (0, i)),
                  pl.BlockSpec((tn, tn), lambda p, i: (0, 0)),
                  pl.BlockSpec((N_EXPERTS, N_EXPERTS), lambda p, i: (0, 0))],
        out_specs=[pl.BlockSpec((K, tn), lambda p, i: (0, i * p)),
                   pl.BlockSpec((N_EXPERTS, 1), lambda p, i: (0, 0))],
        out_shape=[jax.ShapeDtypeStruct((K, N), I32), jax.ShapeDtypeStruct((N_EXPERTS, 1), F32)],
        scratch_shapes=[pltpu.VMEM((N_EXPERTS, 1), F32)],
        compiler_params=_cparams(("arbitrary", "arbitrary")),
        name="slots",
    )(top_eT, tri, lt)


def _sc_scatter_rows(src, idx, n_out, repeat):
    width = src.shape[1]
    n_tok = src.shape[0] // SC_SPLIT
    w = n_tok // SC_WINDOW
    mesh = plsc.VectorSubcoreMesh(core_axis_name="c", subcore_axis_name="s")

    @pl.kernel(out_type=jax.ShapeDtypeStruct((n_out, width), src.dtype), mesh=mesh)
    def scatter_kernel(x_hbm, i_hbm, o_hbm):
        def body(x_vmem, i_vmem):
            for k in range(repeat):
                pltpu.sync_copy(x_vmem, o_hbm.at[i_vmem.at[k]])

        pltpu.emit_pipeline(
            body,
            grid=(SC_SPLIT * w,),
            in_specs=[pl.BlockSpec((SC_WINDOW, width), lambda g: (g, 0)),
                      pl.BlockSpec((repeat, SC_WINDOW), lambda g: (g // w, g % w))],
            out_specs=[],
            core_axis_name=("c", "s"),
            dimension_semantics=(pltpu.PARALLEL,),
        )(x_hbm, i_hbm)

    return scatter_kernel(src, idx.reshape(SC_SPLIT * repeat, n_tok))


def _zero_pads_kernel(start_ref, len_ref, xs_in_ref, xs_ref, zeros_ref, sem, *, rb, n_rows):
    del xs_in_ref
    zeros_ref[...] = jnp.zeros_like(zeros_ref)
    n_seg = start_ref.shape[0]
    bits = rb.bit_length() - 1
    assert rb == 1 << bits

    def for_each_copy(act):
        def rows(row0, size):
            for s in range(SC_SPLIT):
                act(pltpu.make_async_copy(zeros_ref.at[pl.ds(0, size), :],
                                          xs_ref.at[pl.ds(s * n_rows + row0, size), :], sem))

        def segment(e, c):
            start = start_ref[e]
            length = len_ref[e]
            end = start + length
            n_whole = length // rb

            def whole(j, c2):
                rows(pl.multiple_of(end - (j + 1) * rb, SUBLANES), rb)
                return c2

            lax.fori_loop(0, n_whole, whole, 0)
            off = end - n_whole * rb
            rest = length - n_whole * rb
            for b in reversed(range(SUBLANES.bit_length() - 1, bits)):
                size = 1 << b
                take = jnp.bitwise_and(jnp.right_shift(rest, b), 1)
                off = off - take * size

                @pl.when(take == 1)
                def _(off=off, size=size):
                    rows(pl.multiple_of(off, SUBLANES), size)

            lead = jnp.bitwise_and(rest, SUBLANES - 1)
            for i in range(SUBLANES - 1):
                @pl.when(i < lead)
                def _(i=i):
                    rows(start + i, 1)

            return c

        lax.fori_loop(0, n_seg, segment, 0)

    for_each_copy(lambda cp: cp.start())
    for_each_copy(lambda cp: cp.wait())


def _zero_pads(xs2d, start, length, rb, n_rows):
    grid_spec = pltpu.PrefetchScalarGridSpec(
        num_scalar_prefetch=2,
        grid=(1,),
        in_specs=[pl.BlockSpec(memory_space=pl.ANY)],
        out_specs=pl.BlockSpec(memory_space=pl.ANY),
        scratch_shapes=[pltpu.VMEM((rb, xs2d.shape[1]), xs2d.dtype), pltpu.SemaphoreType.DMA(())],
    )
    return pl.pallas_call(
        functools.partial(_zero_pads_kernel, rb=rb, n_rows=n_rows),
        grid_spec=grid_spec,
        out_shape=jax.ShapeDtypeStruct(xs2d.shape, xs2d.dtype),
        input_output_aliases={2: 0},
        compiler_params=_cparams(("arbitrary",)),
        name="zeropads",
    )(start, length, xs2d)


def _experts_kernel(be_ref, x_ref, wg_ref, wu_ref, wd_ref, y_ref, wgu_sc, wd_sc):
    i = pl.program_id(0)
    hdim = wg_ref.shape[1]

    @pl.when((i == 0) | (be_ref[i] != be_ref[jnp.maximum(i - 1, 0)]))
    def _():
        wgu_sc[:, 0:hdim] = wg_ref[...].astype(BF16)
        wgu_sc[:, hdim:] = wu_ref[...].astype(BF16)
        wd_sc[...] = wd_ref[...].astype(BF16)

    gu = jnp.dot(_load_slabs_bf16(x_ref), wgu_sc[...], preferred_element_type=F32)
    g = gu[:, :hdim]
    act = (g * jax.nn.sigmoid(g) * gu[:, hdim:]).astype(BF16)
    _store_slabs(y_ref, jnp.dot(act, wd_sc[...], preferred_element_type=F32))


def _experts(blk_e, xs, wg, wu, wd, rb):
    n_rows = xs.shape[1]
    d_model, hdim = wg.shape[1:]
    per_expert = lambda i, be: (be[i], 0, 0)
    grid_spec = pltpu.PrefetchScalarGridSpec(
        num_scalar_prefetch=1,
        grid=(n_rows // rb,),
        in_specs=[pl.BlockSpec((SC_SPLIT, rb, SC_PIECE), lambda i, be: (0, i, 0)),
                  pl.BlockSpec((None, d_model, hdim), per_expert),
                  pl.BlockSpec((None, d_model, hdim), per_expert),
                  pl.BlockSpec((None, hdim, d_model), per_expert)],
        out_specs=pl.BlockSpec((SC_SPLIT, rb, SC_PIECE), lambda i, be: (0, i, 0)),
        scratch_shapes=[pltpu.VMEM((d_model, 2 * hdim), BF16), pltpu.VMEM((hdim, d_model), BF16)],
    )
    return pl.pallas_call(
        _experts_kernel,
        grid_spec=grid_spec,
        out_shape=jax.ShapeDtypeStruct((SC_SPLIT, n_rows, SC_PIECE), U32),
        compiler_params=_cparams(("arbitrary",)),
        name="experts",
    )(blk_e, xs, wg, wu, wd)


def _sc_gather_rows(table, idx):
    n = idx.shape[0]
    width = table.shape[1]
    mesh = plsc.VectorSubcoreMesh(core_axis_name="c", subcore_axis_name="s")

    @pl.kernel(out_type=jax.ShapeDtypeStruct((n, width), table.dtype), mesh=mesh)
    def gather_kernel(t_hbm, i_hbm, o_hbm):
        def body(i_vmem, o_vmem):
            pltpu.sync_copy(t_hbm.at[i_vmem.at[0]], o_vmem)

        pltpu.emit_pipeline(
            body,
            grid=(n // SC_WINDOW,),
            in_specs=[pl.BlockSpec((1, SC_WINDOW), lambda i: (0, i))],
            out_specs=[pl.BlockSpec((SC_WINDOW, width), lambda i: (i, 0))],
            core_axis_name=("c", "s"),
            dimension_semantics=(pltpu.PARALLEL,),
        )(i_hbm, o_hbm)

    return gather_kernel(table, idx.reshape(1, n))


def _combine_kernel(yg_ref, w_ref, h_ref, x1_ref, g2_ref, fw_ref, sgu_ref, sd_ref, o_ref):
    gu = jnp.dot(_load_slabs_bf16(h_ref), sgu_ref[...], preferred_element_type=F32)
    hdim = gu.shape[1] // 2
    g = gu[:, :hdim]
    ffn = jnp.dot((g * jax.nn.sigmoid(g) * gu[:, hdim:]).astype(BF16), sd_ref[...], preferred_element_type=F32)

    w = w_ref[...]
    tt = w.shape[0]
    lo = [jnp.zeros((tt, SC_PIECE), F32) for _ in range(SC_SPLIT)]
    hi = [jnp.zeros((tt, SC_PIECE), F32) for _ in range(SC_SPLIT)]
    for k in range(MOE_TOPK):
        wk = w[:, k:k + 1]
        for s in range(SC_SPLIT):
            y_lo, y_hi = _unpack_halves(yg_ref[s, k])
            lo[s] = lo[s] + wk * y_lo
            hi[s] = hi[s] + wk * y_hi
    ffn = ffn + jnp.concatenate(lo + hi, axis=1)
    x2 = x1_ref[...] + g2_ref[...] * ffn
    r = lax.rsqrt(jnp.mean(x2 * x2, axis=-1, keepdims=True) + NORM_EPS)
    o_ref[...] = x2 * r * fw_ref[...]


def _combine(yg, top_w, h2p, x1, g2, fw, sgu, sd, S, tt):
    N, D = x1.shape
    per_b = S // tt
    row = lambda i: (i, 0)
    full = lambda i: (0, 0)
    return pl.pallas_call(
        _combine_kernel,
        grid=(N // tt,),
        in_specs=[pl.BlockSpec((SC_SPLIT, MOE_TOPK, tt, SC_PIECE), lambda i: (0, 0, i, 0)),
                  pl.BlockSpec((tt, MOE_TOPK), row),
                  pl.BlockSpec((SC_SPLIT, tt, SC_PIECE), lambda i: (0, i, 0)),
                  pl.BlockSpec((tt, D), row),
                  pl.BlockSpec((None, 1, D), lambda i: (i // per_b, 0, 0)),
                  pl.BlockSpec((1, D), full),
                  pl.BlockSpec(sgu.shape, full),
                  pl.BlockSpec(sd.shape, full)],
        out_specs=pl.BlockSpec((tt, D), row),
        out_shape=jax.ShapeDtypeStruct((N, D), F32),
        compiler_params=_cparams(("parallel",)),
        name="combine",
    )(yg, top_w, h2p, x1, g2.reshape(-1, 1, D), fw.reshape(1, D), sgu, sd)


def _cmp_to_sel_T(n_cp, n_sel):
    c0 = np.arange(n_cp)[None, :] * CMP_STRIDE
    s0 = np.arange(n_sel)[:, None] * SEL_BLOCK
    ov = np.minimum(c0 + CMP_BLOCK, s0 + SEL_BLOCK) - np.maximum(c0, s0)
    m = np.clip(ov, 0, None).astype(np.float32) / CMP_BLOCK
    m[:, n_cp - 1] = 0.0
    return jnp.asarray(m, BF16)


def _slab_weights(w_in):
    D = w_in.shape[0]
    sizes = [1024, 1024, 1024, 512, 128, 128, 128, 128, 128, 128, 24, 2048]
    offs = np.concatenate([[0], np.cumsum(sizes)])
    part = lambda i: w_in[:, offs[i]:offs[i + 1]]
    pieces = [part(0), part(1), part(2), part(11), part(3)] + [part(i) for i in range(4, 10)] + [part(10)]
    w = jnp.concatenate(pieces, axis=1)
    return jnp.pad(w, ((0, 0), (0, SLAB_COLS - w.shape[1]))).astype(BF16)


def _layer(x, c, ada_w, ada_b, norm1_w, w_in, da_lq1, da_lk1, da_lq2, da_lk2, da_subln_w,
           cmp_k_pe, cmp_k_w1, cmp_k_w2, cmp_v_pe, cmp_v_w1, cmp_v_w2, w_da_out, w_nsa_out, w_o,
           norm2_w, router_w, router_b, exp_w_gate, exp_w_up, exp_w_down,
           sh_w_gate, sh_w_up, sh_w_down, final_norm_w, lam_init):
    B, S, D = x.shape
    N = B * S
    G, HPG = NSA_GROUPS, NSA_HPG
    x2 = x.reshape(N, D)

    mod = _ada(c, ada_w, ada_b)
    sh1, sc1, g1, sh2, sc2, g2 = jnp.split(mod, 6, axis=-1)

    slab = _inproj(x2, norm1_w, sh1, sc1, _slab_weights(w_in), S)
    slab3 = slab.reshape(B, S, SLAB_COLS)

    i_all = np.arange(1, DA_HEADS + NSA_HEADS + 1, dtype=np.float32)
    slopes = (2.0 ** (-8.0 * i_all / (DA_HEADS + NSA_HEADS))).astype(np.float32)
    slopes_a = jnp.asarray(slopes[0::2])
    slopes_b = jnp.asarray(slopes[1::2])

    ta_q, ta_k = min(512, S), min(512, S)
    qT = (slab3[:, :, COL_DAQ:COL_DAQ + 1024] * jnp.asarray(DA_DIM ** -0.5, BF16)).transpose(0, 2, 1)
    vT = slab3[:, :, COL_DAV:COL_DAV + 1024].reshape(B, S // ta_k, ta_k, DA_HEADS, DA_VDIM).transpose(0, 3, 1, 4, 2)
    kmin = _first_live_tile(slopes[0::2], slab3, ta_q, ta_k)
    oaT = _diff_attention(slopes_a, kmin, qT, slab3, _with_ones_row(vT), _slope_rows(slopes[0::2]), da_lq1, da_lk1,
                          da_lq2, da_lk2, da_subln_w, lam_init, ta_q, ta_k)
    oa = oaT.transpose(0, 2, 1).reshape(N, DA_HEADS * DA_VDIM)

    tb, tb_k = 256, min(256, S)
    n_cp = S // CMP_STRIDE
    n_sel = S // SEL_BLOCK

    def kv_groups(j):
        c0 = COL_KV6 + 128 * j
        return slab3[:, :, c0:c0 + 128].reshape(B, S, G, NSA_DIM).transpose(0, 2, 1, 3)

    def kv_tiles_T(a, rows):
        return a.reshape(B, G, S // rows, rows, NSA_DIM).transpose(0, 1, 2, 4, 3)

    ck_rows = kv_groups(0).reshape(B * G, n_cp, CMP_STRIDE * NSA_DIM)
    cv_rows = kv_groups(1).reshape(B * G, n_cp, CMP_STRIDE * NSA_DIM)
    kc = _compress(ck_rows, cmp_k_pe, cmp_k_w1, cmp_k_w2).reshape(B, G, n_cp, NSA_DIM).astype(BF16)
    vc = _compress(cv_rows, cmp_v_pe, cmp_v_w1, cmp_v_w2).reshape(B, G, n_cp, NSA_DIM).astype(BF16)
    nq = (slab3[:, :, COL_NSAQ:COL_NSAQ + 512] * jnp.asarray(NSA_DIM ** -0.5, BF16))
    nqT = nq.reshape(B, S // tb, tb, G, HPG, NSA_DIM).transpose(0, 3, 1, 5, 4, 2).reshape(B, G, S // tb, NSA_DIM, HPG * tb)
    gT = (slab3[:, :, COL_NSAG:COL_NSAG + 3 * NSA_HEADS].astype(F32).reshape(B, S // tb, tb, G, HPG, 3)
          .transpose(0, 3, 1, 5, 4, 2).reshape(B, G, S // tb, 3, HPG * tb))
    onehot = jnp.asarray((np.arange(S)[:, None] // SEL_BLOCK == np.arange(SEL_COLS)[None, :]).astype(np.float32), BF16)
    zeros64 = jnp.zeros((B, G, S, NSA_DIM), BF16)
    ksa = jnp.concatenate([kv_groups(2), zeros64, jnp.broadcast_to(onehot, (B, G, S, SEL_COLS))], axis=-1)
    pad_rows = jnp.zeros((B, G, WINDOW, 2 * NSA_DIM), BF16).at[..., NSA_DIM].set(1.0)
    kwa = jnp.concatenate([pad_rows, jnp.concatenate([kv_groups(4), zeros64], axis=-1)], axis=2)
    vw_pad = jnp.pad(kv_groups(5), ((0, 0), (0, 0), (WINDOW, 0), (0, 0)))
    vwT = vw_pad.reshape(B, G, (S + WINDOW) // tb, tb, NSA_DIM).transpose(0, 1, 2, 4, 3)
    r_w = np.arange(WINDOW + tb)[:, None]
    lane_w = np.arange(HPG * tb)[None, :]
    d_w = (WINDOW + lane_w % tb - r_w).astype(np.float32)
    slope_w = slopes[1::2].reshape(G, 1, HPG)[:, :, lane_w[0] // tb]
    wb = jnp.asarray(np.where((d_w >= 0) & (d_w < WINDOW), -slope_w * d_w[None], np.float32(NEG)).astype(np.float32))
    obT = _nsa_attention(slopes_b, nqT, kc, vc.transpose(0, 1, 3, 2), ksa, kv_tiles_T(kv_groups(3), tb_k),
                         kwa, vwT, wb, gT, _cmp_to_sel_T(n_cp, n_sel), tb, tb_k)
    ob = (obT.reshape(B, G, S // tb, NSA_DIM, HPG, tb).transpose(0, 2, 5, 1, 4, 3)
          .reshape(N, NSA_HEADS * NSA_DIM))

    wr_hi = router_w.astype(BF16)
    wr_lo = (router_w.astype(F32) - wr_hi.astype(F32)).astype(BF16)
    x1, h2, logits = _merge(oa, ob, slab, x2, g1, norm2_w, sh2, sc2, w_da_out.astype(BF16),
                            w_nsa_out.astype(BF16), w_o.astype(BF16), jnp.stack([wr_hi, wr_lo]), S)

    rb = 512
    tt = 128
    top_eT, top_wT = _route(logits.T, router_b)
    destT, counts = _slots(top_eT, rb)
    n_rows = ((N * MOE_TOPK + N_EXPERTS * (rb - 1) + rb - 1) // rb) * rb
    padded = (jnp.ceil(counts[:, 0] / rb) * rb).astype(I32)
    pend = jnp.cumsum(padded)
    blk_start = jnp.arange(n_rows // rb, dtype=I32) * rb
    blk_e = jnp.minimum(jnp.sum((pend[None, :] <= blk_start[:, None]).astype(I32), axis=1), N_EXPERTS - 1)
    d_flat = destT.reshape(-1)
    piece_idx = jnp.concatenate([d_flat + s * n_rows for s in range(SC_SPLIT)])
    xs2d = _sc_scatter_rows(h2.reshape(SC_SPLIT * N, SC_PIECE), piece_idx, SC_SPLIT * n_rows, MOE_TOPK)
    cnt = counts[:, 0].astype(I32)
    pad_start = jnp.concatenate([pend - padded + cnt, pend[-1:]])
    pad_len = jnp.concatenate([padded - cnt, n_rows - pend[-1:]])
    xs = _zero_pads(xs2d, pad_start, pad_len, rb, n_rows).reshape(SC_SPLIT, n_rows, SC_PIECE)
    ys = _experts(blk_e, xs, exp_w_gate, exp_w_up, exp_w_down, rb)
    sgu = jnp.concatenate([sh_w_gate, sh_w_up], axis=-1).astype(BF16)
    yg = _sc_gather_rows(ys.reshape(SC_SPLIT * n_rows, SC_PIECE), piece_idx).reshape(SC_SPLIT, MOE_TOPK, N, SC_PIECE)
    out = _combine(yg, top_wT.T, h2, x1, g2, final_norm_w, sgu, sh_w_down.astype(BF16), S, 2 * tt)
    return out.reshape(B, S, D)


def kernel(x, c, ada_w, ada_b, norm1_w, w_in, da_lq1, da_lk1, da_lq2, da_lk2, da_subln_w, cmp_k_pe, cmp_k_w1, cmp_k_w2, cmp_v_pe, cmp_v_w1, cmp_v_w2, w_da_out, w_nsa_out, w_o, norm2_w, router_w, router_b, exp_w_gate, exp_w_up, exp_w_down, sh_w_gate, sh_w_up, sh_w_down, final_norm_w):
    depth = ada_w.shape[0]
    assert depth == 1, "one decoder layer"
    lam_init = 0.8 - 0.6 * math.exp(-0.3 * 0)
    return _layer(x, c, ada_w[0], ada_b[0], norm1_w[0], w_in[0], da_lq1[0], da_lk1[0], da_lq2[0], da_lk2[0],
                  da_subln_w[0], cmp_k_pe[0], cmp_k_w1[0], cmp_k_w2[0], cmp_v_pe[0], cmp_v_w1[0], cmp_v_w2[0],
                  w_da_out[0], w_nsa_out[0], w_o[0], norm2_w[0], router_w[0], router_b[0],
                  exp_w_gate[0], exp_w_up[0], exp_w_down[0], sh_w_gate[0], sh_w_up[0], sh_w_down[0],
                  final_norm_w, lam_init)
```

```python
import functools
import math

import numpy as np
import jax
import jax.numpy as jnp
from jax import lax
from jax.experimental import pallas as pl
from jax.experimental.pallas import tpu as pltpu
from jax.experimental.pallas import tpu_sc as plsc

F32 = jnp.float32
BF16 = jnp.bfloat16
I32 = jnp.int32
U32 = jnp.uint32
HIGHEST = lax.Precision.HIGHEST

NORM_EPS = 1e-6
NEG = -1e30

DA_HEADS = 8
DA_DIM = 64
DA_VDIM = 128
DA_SUBLN_EPS = 1e-5

NSA_HEADS = 8
NSA_GROUPS = 2
NSA_HPG = 4
NSA_DIM = 64
CMP_BLOCK = 32
CMP_STRIDE = 16
SEL_BLOCK = 64
SEL_TOPK = 16
WINDOW = 512
FORCE_BONUS = 1e4
SEL_COLS = 128

N_EXPERTS = 64
MOE_TOPK = 8
N_EXPERT_GROUPS = 8
TOPK_EXPERT_GROUPS = 4
ROUTED_SCALE = 2.5

VMEM_LIMIT_V7X = 56 * 1024 * 1024

COL_DAQ, COL_DAK, COL_DAV = 0, 1024, 2048
COL_MERGE = 3072
COL_NSAQ = 5120
COL_KV6 = 5632
COL_NSAG = 6400
SLAB_COLS = 6528


SUBLANES = 8
SC_WINDOW = 128
SC_SPLIT = 2
SC_PIECE = 256

POS_SPLIT = 16
POS_ROWS = 16
ONES_ROWS = 16


def _pos_features(rows):
    r = np.arange(rows)
    f = np.zeros((rows, 128), np.float32)
    f[:, 0:3] = (r // POS_SPLIT)[:, None]
    f[:, 3:6] = (r % POS_SPLIT)[:, None]
    return jnp.asarray(f, BF16)


def _slope_rows(slopes):
    s = np.asarray(slopes, np.float32)
    bf = lambda x: x.astype(BF16).astype(np.float32)
    p1 = bf(s)
    p2 = bf(s - p1)
    p3 = bf(s - p1 - p2)
    out = np.zeros((s.shape[0], POS_ROWS, 128), np.float32)
    for i, piece in enumerate((p1, p2, p3)):
        out[:, i, :] = POS_SPLIT * piece[:, None]
        out[:, 3 + i, :] = piece[:, None]
    return jnp.asarray(out)


def _with_ones_row(vT):
    shape = vT.shape[:-2] + (ONES_ROWS, vT.shape[-1])
    extra = jnp.zeros(shape, vT.dtype).at[..., 0, :].set(1.0)
    return jnp.concatenate([vT, extra], axis=-2)


def _cparams(sem, vmem=VMEM_LIMIT_V7X):
    return pltpu.CompilerParams(dimension_semantics=sem, vmem_limit_bytes=vmem)


def _ada_kernel(c_ref, w_ref, b_ref, o_ref):
    c = c_ref[...]
    ca = c * jax.nn.sigmoid(c)
    o_ref[...] = jnp.dot(ca, w_ref[...], preferred_element_type=F32, precision=HIGHEST) + b_ref[...]


def _ada(c, w, b):
    B, D = c.shape
    n_out = w.shape[1]
    rows = 8
    cp = jnp.zeros((rows, D), F32).at[:B].set(c)
    tn = 1024
    out = pl.pallas_call(
        _ada_kernel,
        grid=(n_out // tn,),
        in_specs=[pl.BlockSpec((rows, D), lambda j: (0, 0)),
                  pl.BlockSpec((D, tn), lambda j: (0, j)),
                  pl.BlockSpec((1, tn), lambda j: (0, j))],
        out_specs=pl.BlockSpec((rows, tn), lambda j: (0, j)),
        out_shape=jax.ShapeDtypeStruct((rows, n_out), F32),
        compiler_params=_cparams(("arbitrary",)),
        name="ada",
    )(cp, w, b.reshape(1, n_out))
    return out[:B]


def _modulate(x, nw, sh, sc):
    r = lax.rsqrt(jnp.mean(x * x, axis=-1, keepdims=True) + NORM_EPS)
    return (x * r) * nw * (1.0 + sc) + sh


def _inproj_kernel(x_ref, nw_ref, sh_ref, sc_ref, w_ref, o_ref, h_ref):
    @pl.when(pl.program_id(1) == 0)
    def _():
        h_ref[...] = _modulate(x_ref[...], nw_ref[...], sh_ref[...], sc_ref[...]).astype(BF16)

    o_ref[...] = jnp.dot(h_ref[...], w_ref[...], preferred_element_type=F32).astype(BF16)


def _inproj(x2, nw, sh, sc, w_slab, S):
    N, D = x2.shape
    tm = min(1024, S)
    tn = SLAB_COLS // 3
    per_b = S // tm
    return pl.pallas_call(
        _inproj_kernel,
        grid=(N // tm, SLAB_COLS // tn),
        in_specs=[pl.BlockSpec((tm, D), lambda i, j: (i, 0)),
                  pl.BlockSpec((1, D), lambda i, j: (0, 0)),
                  pl.BlockSpec((None, 1, D), lambda i, j: (i // per_b, 0, 0)),
                  pl.BlockSpec((None, 1, D), lambda i, j: (i // per_b, 0, 0)),
                  pl.BlockSpec((D, tn), lambda i, j: (0, j))],
        out_specs=pl.BlockSpec((tm, tn), lambda i, j: (i, j)),
        out_shape=jax.ShapeDtypeStruct((N, SLAB_COLS), BF16),
        scratch_shapes=[pltpu.VMEM((tm, D), BF16)],
        compiler_params=_cparams(("parallel", "arbitrary")),
        name="inproj",
    )(x2, nw.reshape(1, D), sh.reshape(-1, 1, D), sc.reshape(-1, 1, D), w_slab)


def _cmp_kernel(r_ref, pe_ref, w1a_ref, w1b_ref, w2_ref, o_ref):
    r = r_ref[...]
    n_rows = r.shape[0]
    a = jnp.dot(r, w1a_ref[...], preferred_element_type=F32)
    b = jnp.dot(r, w1b_ref[...], preferred_element_type=F32)
    pe = pe_ref[...]
    half = pe.shape[1] // 2
    bias = (jnp.dot(pe[:, :half], w1a_ref[...].astype(F32), preferred_element_type=F32, precision=HIGHEST)
            + jnp.dot(pe[:, half:], w1b_ref[...].astype(F32), preferred_element_type=F32, precision=HIGHEST))
    hid = a + pltpu.roll(b, n_rows - 1, 0) + bias
    act = 0.5 * hid * (1.0 + jnp.tanh(0.7978845608028654 * (hid + 0.044715 * hid * hid * hid)))
    o_ref[...] = jnp.dot(act, w2_ref[...], preferred_element_type=F32, precision=HIGHEST)


def _compress(rows, pe, w1, w2):
    BG, n_rows, width = rows.shape
    hidden = w1.shape[1]
    w1a = w1[:width].astype(BF16)
    w1b = w1[width:].astype(BF16)
    return pl.pallas_call(
        _cmp_kernel,
        grid=(BG,),
        in_specs=[pl.BlockSpec((None, n_rows, width), lambda i: (i, 0, 0)),
                  pl.BlockSpec((1, 2 * width), lambda i: (0, 0)),
                  pl.BlockSpec((width, hidden), lambda i: (0, 0)),
                  pl.BlockSpec((width, hidden), lambda i: (0, 0)),
                  pl.BlockSpec((hidden, NSA_DIM), lambda i: (0, 0))],
        out_specs=pl.BlockSpec((None, n_rows, NSA_DIM), lambda i: (i, 0, 0)),
        out_shape=jax.ShapeDtypeStruct((BG, n_rows, NSA_DIM), F32),
        compiler_params=_cparams(("parallel",)),
        name="cmp",
    )(rows, pe.reshape(1, -1).astype(F32), w1a, w1b, w2.astype(F32))


def _da_kernel(slopes_ref, kmin_ref, q_ref, k_ref, vT_ref, pos_ref, rel_ref, srow_ref, lq1_ref, lk1_ref, lq2_ref,
               lk2_ref, subw_ref, o_ref, acc_ref, qa_ref, sa_ref, sb_ref, pa_ref, pb_ref, *, tq, tk, lam_init):
    h = pl.program_id(1)
    qi = pl.program_id(2)
    slope = slopes_ref[h]
    q0 = qi * tq
    w = 2 * tq
    nk = k_ref.shape[0] // tk
    qT = (q_ref[...].astype(F32) * DA_DIM ** -0.5).T.astype(BF16)
    row = lax.broadcasted_iota(I32, qT.shape, 0)
    zero = jnp.zeros_like(qT)
    qa_ref[0:2 * DA_DIM, :] = jnp.concatenate(
        [jnp.where(row < DA_DIM, qT, zero), jnp.where(row >= DA_DIM, qT, zero)], axis=1)
    qa_ref[2 * DA_DIM:2 * DA_DIM + POS_ROWS, :] = jnp.concatenate([srow_ref[...]] * (w // 128), axis=1).astype(BF16)
    qa_ref[2 * DA_DIM + POS_ROWS:, :] = jnp.zeros((2 * DA_DIM - POS_ROWS, w), BF16)

    def scores(kt, s_ref):
        k0 = kt * tk
        k_t = k_ref[pl.ds(pl.multiple_of(k0, tk), tk), :]
        s = jnp.dot(jnp.concatenate([k_t, pos_ref[...]], axis=1), qa_ref[...], preferred_element_type=F32)
        s = jnp.where(rel_ref[...] <= q0 - k0, s, NEG)
        s_ref[...] = s
        return jnp.max(s, axis=0, keepdims=True)

    def probs(s_ref, p_ref, mx, off, m_old):
        m_new = jnp.maximum(m_old, mx + off)
        p_ref[...] = jnp.exp(s_ref[...] - (m_new - off)).astype(BF16)
        return m_new, jnp.exp(m_old - m_new)

    def accumulate(kt, p_ref, alpha):
        acc_ref[...] = alpha * acc_ref[...] + jnp.dot(vT_ref[kt], p_ref[...], preferred_element_type=F32)

    n_tiles = (q0 + tq - 1) // tk + 1
    kt_min = kmin_ref[(pl.program_id(0) * pl.num_programs(1) + h) * pl.num_programs(2) + qi]
    acc_ref[...] = jnp.zeros_like(acc_ref)
    pb_ref[...] = jnp.zeros_like(pb_ref)
    mx_a = scores(kt_min, sa_ref)

    def pair(j, carry):
        m, mx_a, alpha_b = carry
        ta = kt_min + 2 * j
        tb = ta + 1
        off_a = slope * (ta * tk).astype(F32)
        off_b = jnp.where(tb < n_tiles, slope * (tb * tk).astype(F32), NEG)
        accumulate(jnp.maximum(ta - 1, 0), pb_ref, alpha_b)
        m, alpha_a = probs(sa_ref, pa_ref, mx_a, off_a, m)
        mx_b = scores(jnp.minimum(tb, nk - 1), sb_ref)
        accumulate(ta, pa_ref, alpha_a)
        m, alpha_b = probs(sb_ref, pb_ref, mx_b, off_b, m)
        mx_a = scores(jnp.minimum(ta + 2, nk - 1), sa_ref)
        return m, mx_a, alpha_b

    n_pairs = (n_tiles - kt_min + 1) // 2
    init = (jnp.full((1, w), NEG, F32), mx_a, jnp.ones((1, w), F32))
    _, _, alpha_b = lax.fori_loop(0, n_pairs, pair, init)
    accumulate(jnp.minimum(kt_min + 2 * n_pairs - 1, nk - 1), pb_ref, alpha_b)

    lam = (jnp.exp(jnp.sum(lq1_ref[...] * lk1_ref[...], axis=-1, keepdims=True))
           - jnp.exp(jnp.sum(lq2_ref[...] * lk2_ref[...], axis=-1, keepdims=True)) + lam_init)
    on = acc_ref[0:DA_VDIM, :] * (1.0 / acc_ref[DA_VDIM:DA_VDIM + 1, :])
    o = on[:, :tq] - lam * on[:, tq:]
    r = lax.rsqrt(jnp.mean(o * o, axis=0, keepdims=True) + DA_SUBLN_EPS)
    o_ref[...] = (o * r * subw_ref[...] * (1.0 - lam_init)).T.astype(BF16)


EXP_UNDERFLOW = 104.0


def _first_live_tile(slopes_np, slab3, tq, tk):
    B, S, _ = slab3.shape
    H = DA_HEADS
    width = 2 * DA_DIM * H
    grp = jnp.asarray((np.arange(width)[:, None] // DA_DIM == np.arange(2 * H)[None, :]).astype(np.float32), BF16)

    def norms(c0):
        v = slab3[:, :, c0:c0 + width]
        return jnp.sqrt(jnp.einsum('bsc,cg->bsg', v * v, grp, preferred_element_type=F32))

    qn = norms(COL_DAQ) * DA_DIM ** -0.5
    qn = jnp.max(qn.reshape(B, S // tq, tq, H, 2), axis=2).transpose(0, 2, 3, 1)
    kn = jnp.max(norms(COL_DAK), axis=1).reshape(B, H, 2)
    bound = jnp.max(qn * kn[..., None], axis=2) * 1.01 + 0.01
    q0 = (jnp.arange(S // tq, dtype=F32) * tq)[None, None, :]
    slope = jnp.asarray(slopes_np, F32)[None, :, None]
    first_key = q0 - (EXP_UNDERFLOW + 2.0 * bound) / slope
    kt = jnp.floor(first_key / tk).astype(I32)
    n_tiles = (jnp.arange(S // tq, dtype=I32) * tq + tq - 1) // tk + 1
    return jnp.clip(kt, 0, n_tiles[None, None, :] - 1).reshape(-1)


def _diff_attention(slopes, kmin, slab3, vT, srows, lq1, lk1, lq2, lk2, subw, lam_init, tq, tk):
    B, S, _ = slab3.shape
    nk = S // tk
    assert tk % tq == 0 and S % tk == 0
    qcol = COL_DAQ // 128
    tile = tq
    kcol = COL_DAK // 128
    v_rows = vT.shape[3]
    vec = lambda a: a.reshape(1, DA_DIM).astype(F32)
    grid_spec = pltpu.PrefetchScalarGridSpec(
        num_scalar_prefetch=2,
        grid=(B, DA_HEADS, S // tile),
        in_specs=[pl.BlockSpec((None, tile, 128), lambda b, h, i, s, km: (b, i, qcol + h)),
                  pl.BlockSpec((None, S, 128), lambda b, h, i, s, km: (b, 0, kcol + h)),
                  pl.BlockSpec((None, None, nk, v_rows, tk), lambda b, h, i, s, km: (b, h, 0, 0, 0)),
                  pl.BlockSpec((tk, 128), lambda b, h, i, s, km: (0, 0)),
                  pl.BlockSpec((tk, 2 * tq), lambda b, h, i, s, km: (0, 0)),
                  pl.BlockSpec((None, POS_ROWS, 128), lambda b, h, i, s, km: (h, 0, 0)),
                  pl.BlockSpec((1, DA_DIM), lambda b, h, i, s, km: (0, 0)),
                  pl.BlockSpec((1, DA_DIM), lambda b, h, i, s, km: (0, 0)),
                  pl.BlockSpec((1, DA_DIM), lambda b, h, i, s, km: (0, 0)),
                  pl.BlockSpec((1, DA_DIM), lambda b, h, i, s, km: (0, 0)),
                  pl.BlockSpec((DA_VDIM, 1), lambda b, h, i, s, km: (0, 0))],
        out_specs=pl.BlockSpec((None, tile, 128), lambda b, h, i, s, km: (b, i, h)),
        scratch_shapes=[pltpu.VMEM((v_rows, 2 * tq), F32), pltpu.VMEM((4 * DA_DIM, 2 * tq), BF16),
                        pltpu.VMEM((tk, 2 * tq), F32), pltpu.VMEM((tk, 2 * tq), F32),
                        pltpu.VMEM((tk, 2 * tq), BF16), pltpu.VMEM((tk, 2 * tq), BF16)],
    )
    rel = jnp.asarray(np.arange(tk)[:, None] - (np.arange(2 * tq) % tq)[None, :], I32)
    return pl.pallas_call(
        functools.partial(_da_kernel, tq=tq, tk=tk, lam_init=lam_init),
        grid_spec=grid_spec,
        out_shape=jax.ShapeDtypeStruct((B, S, DA_HEADS * DA_VDIM), BF16),
        compiler_params=_cparams(("parallel", "parallel", "arbitrary")),
        name="diffattn",
    )(slopes, kmin, slab3, slab3, vT, _pos_features(tk), rel, srows, vec(lq1), vec(lk1), vec(lq2), vec(lk2),
      subw.reshape(DA_VDIM, 1).astype(F32))


def _nsa_kernel(slopes_ref, qT_ref, kc_ref, vcT_ref, ksa_ref, vsT_ref, kwa_ref, vwT_ref, wb_ref, gT_ref, mT_ref,
                grp_ref, ltri_ref, oT_ref, acc_ref, res_ref, base_ref, qa_ref, sa_ref, sb_ref, pa_ref, pb_ref,
                list_ref, slist_ref, lsem, *, t, tk, n_top):
    g = pl.program_id(1)
    qi = pl.program_id(2)
    q0 = qi * t
    w = NSA_HPG * t
    n_cp = kc_ref.shape[0]
    n_sel = mT_ref.shape[0]
    lane1 = lax.broadcasted_iota(I32, (1, w), 1)
    slope_row = jnp.zeros((1, w), F32)
    for hh in range(NSA_HPG):
        slope_row = jnp.where((lane1 >= hh * t) & (lane1 < (hh + 1) * t), slopes_ref[g * NSA_HPG + hh], slope_row)
    q_all = qT_ref[...]
    gates = jax.nn.sigmoid(gT_ref[...])
    key_off = lax.broadcasted_iota(I32, (tk, w), 0)
    t_loc = jnp.bitwise_and(lax.broadcasted_iota(I32, (tk, w), 1), t - 1)
    base_ref[...] = slope_row * key_off.astype(F32)

    c_idx = lax.broadcasted_iota(I32, (n_cp, w), 0)
    t_c = q0 + jnp.bitwise_and(lax.broadcasted_iota(I32, (n_cp, w), 1), t - 1)
    d_c = t_c - (c_idx * CMP_STRIDE + (CMP_BLOCK - 1))
    ok_c = d_c >= 0
    s = jnp.dot(kc_ref[...], q_all, preferred_element_type=F32) - slope_row * d_c.astype(F32)
    s = jnp.where(ok_c, s, NEG)
    e = jnp.exp(s - jnp.max(s, axis=0, keepdims=True))
    inv = 1.0 / jnp.sum(e, axis=0, keepdims=True)
    p = jnp.where(ok_c, e * inv, 0.0)
    res_ref[...] = gates[0:1, :] * jnp.dot(vcT_ref[...], p.astype(BF16), preferred_element_type=F32)
    p_sum = p[:, 0:t]
    for hh in range(1, NSA_HPG):
        p_sum = p_sum + p[:, hh * t:(hh + 1) * t]

    p_hi = p_sum.astype(BF16)
    p_lo = (p_sum - p_hi.astype(F32)).astype(BF16)
    mT = mT_ref[...]
    imp = jnp.dot(mT, p_hi, preferred_element_type=F32) + jnp.dot(mT, p_lo, preferred_element_type=F32)
    blk = lax.broadcasted_iota(I32, (n_sel, t), 0)
    t_s = q0 + lax.broadcasted_iota(I32, (n_sel, t), 1)
    cur = jnp.right_shift(t_s, int(math.log2(SEL_BLOCK)))
    forced = (blk == 0) | (blk == cur) | (blk == cur - 1)
    score = jnp.where(blk * SEL_BLOCK <= t_s, imp + jnp.where(forced, FORCE_BONUS, 0.0), NEG)
    blk_f = blk.astype(F32)
    sel = jnp.zeros((n_sel, t), F32)
    for _ in range(n_top):
        _, idx = _first_max(score, blk_f)
        hit = blk_f == idx
        sel = jnp.where(hit, 1.0, sel)
        score = jnp.where(hit, -jnp.inf, score)
    n_full = q0 // tk
    nt_pad = grp_ref.shape[0]
    cnt = jnp.broadcast_to(jnp.sum(sel, axis=1, keepdims=True), (n_sel, 128)).astype(BF16)
    tile_cnt = jnp.dot(grp_ref[...], cnt, preferred_element_type=F32)
    kt_i = lax.broadcasted_iota(I32, (nt_pad, 128), 0)
    active = (tile_cnt > 0.5) & (kt_i < n_full)
    act_f = jnp.where(active, 1.0, 0.0)
    before = jnp.dot(ltri_ref[...], act_f.astype(BF16), preferred_element_type=F32)
    lane_j = lax.broadcasted_iota(I32, (nt_pad, 128), 1)
    slot_hit = active & (before == lane_j.astype(F32))
    tiles_row = jnp.sum(jnp.where(slot_hit, kt_i.astype(F32), 0.0), axis=0, keepdims=True)
    n_act_row = jnp.sum(act_f, axis=0, keepdims=True)
    lane8 = lax.broadcasted_iota(I32, (8, 128), 1)
    list_ref[...] = jnp.where(lane8 == 127, n_act_row, tiles_row).astype(I32)
    list_copy = pltpu.make_async_copy(list_ref, slist_ref, lsem)
    list_copy.start()

    selb = jnp.where(sel > 0.5, 0.0, NEG)
    row64 = lax.broadcasted_iota(I32, (NSA_DIM, w), 0)
    qa_ref[0:NSA_DIM, :] = q_all
    qa_ref[NSA_DIM:2 * NSA_DIM, :] = jnp.where(row64 == 0, NEG, 0.0).astype(BF16)
    qa_ref[2 * NSA_DIM:2 * NSA_DIM + n_sel, :] = jnp.concatenate([selb] * NSA_HPG, axis=1).astype(BF16)
    if n_sel < SEL_COLS:
        qa_ref[2 * NSA_DIM + n_sel:, :] = jnp.zeros((SEL_COLS - n_sel, w), BF16)

    n_wt = (WINDOW + t) // t
    kw_t = kwa_ref[pl.ds(pl.multiple_of(q0, t), WINDOW + t), :]
    s = jnp.dot(kw_t, qa_ref[0:2 * NSA_DIM, :], preferred_element_type=F32) + wb_ref[...]
    p = jnp.exp(s - jnp.max(s, axis=0, keepdims=True))
    inv = 1.0 / jnp.sum(p, axis=0, keepdims=True)
    p = p.astype(BF16)
    o_w = jnp.dot(vwT_ref[qi], p[0:t, :], preferred_element_type=F32)
    for c in range(1, n_wt):
        o_w = o_w + jnp.dot(vwT_ref[qi + c], p[c * t:(c + 1) * t, :], preferred_element_type=F32)
    res_ref[...] = res_ref[...] + gates[2:3, :] * (o_w * inv)

    def scores(kt, s_ref):
        k_t = ksa_ref[pl.ds(pl.multiple_of(kt * tk, tk), tk), :]
        s = jnp.dot(k_t, qa_ref[...], preferred_element_type=F32) + base_ref[...]
        s_ref[...] = s
        return jnp.max(s, axis=0, keepdims=True)

    def probs(s_ref, p_ref, mx, off, m_old, l_old):
        m_new = jnp.maximum(m_old, mx + off)
        p = jnp.exp(s_ref[...] - (m_new - off))
        alpha = jnp.exp(m_old - m_new)
        p_ref[...] = p.astype(BF16)
        return m_new, alpha * l_old + jnp.sum(p, axis=0, keepdims=True), alpha

    def accumulate(kt, p_ref, alpha):
        acc_ref[...] = alpha * acc_ref[...] + jnp.dot(vsT_ref[kt], p_ref[...], preferred_element_type=F32)

    list_copy.wait()
    n_act = slist_ref[0, 127]
    mx_a = scores(slist_ref[0, 0], sa_ref)

    kd = n_full * tk
    s = jnp.dot(ksa_ref[pl.ds(pl.multiple_of(kd, tk), tk), :], qa_ref[...], preferred_element_type=F32) + base_ref[...]
    s = jnp.where(key_off - t_loc <= q0 - kd, s, NEG)
    off_d = slope_row * kd.astype(F32)
    m = jnp.max(s, axis=0, keepdims=True) + off_d
    p = jnp.exp(s - (m - off_d))
    l = jnp.sum(p, axis=0, keepdims=True)
    acc_ref[...] = jnp.dot(vsT_ref[n_full], p.astype(BF16), preferred_element_type=F32)
    pb_ref[...] = jnp.zeros_like(pb_ref)

    def pair(j, carry):
        m, l, mx_a, alpha_b, tb_prev = carry
        ta = slist_ref[0, 2 * j]
        tb = slist_ref[0, 2 * j + 1]
        off_a = slope_row * (ta * tk).astype(F32)
        off_b = jnp.where(2 * j + 1 < n_act, slope_row * (tb * tk).astype(F32), NEG)
        accumulate(tb_prev, pb_ref, alpha_b)
        m, l, alpha_a = probs(sa_ref, pa_ref, mx_a, off_a, m, l)
        mx_b = scores(tb, sb_ref)
        accumulate(ta, pa_ref, alpha_a)
        m, l, alpha_b = probs(sb_ref, pb_ref, mx_b, off_b, m, l)
        mx_a = scores(slist_ref[0, 2 * j + 2], sa_ref)
        return m, l, mx_a, alpha_b, tb

    n_pairs = (n_act + 1) // 2
    m, l, _, alpha_b, tb_last = lax.fori_loop(0, n_pairs, pair, (m, l, mx_a, jnp.ones((1, w), F32), jnp.int32(0)))
    accumulate(tb_last, pb_ref, alpha_b)
    oT_ref[...] = (res_ref[...] + gates[1:2, :] * (acc_ref[...] * (1.0 / l))).astype(BF16)


def _nsa_attention(slopes, qT, kc, vcT, ksa, vsT, kwa, vwT, wb, gT, mT, t, tk):
    B, G, nq, _, w = qT.shape
    S = nq * t
    n_cp = kc.shape[2]
    n_sel = S // SEL_BLOCK
    n_top = min(SEL_TOPK, n_sel)
    assert tk % t == 0 and S % tk == 0 and t & (t - 1) == 0 and WINDOW % t == 0 and n_sel <= SEL_COLS
    n_tiles = S // tk
    nt_pad = -(-n_tiles // 16) * 16
    assert nt_pad + 2 < 127
    grp = jnp.asarray((np.arange(n_sel)[None, :] // (tk // SEL_BLOCK) == np.arange(nt_pad)[:, None])
                      .astype(np.float32), BF16)
    ltri = jnp.asarray((np.arange(nt_pad)[None, :] < np.arange(nt_pad)[:, None]).astype(np.float32), BF16)
    fixed = lambda b, g, i, s: (b, g, 0, 0)
    fixed5 = lambda b, g, i, s: (b, g, 0, 0, 0)
    tile5 = lambda b, g, i, s: (b, g, i, 0, 0)
    grid_spec = pltpu.PrefetchScalarGridSpec(
        num_scalar_prefetch=1,
        grid=(B, G, nq),
        in_specs=[pl.BlockSpec((None, None, None, NSA_DIM, w), tile5),
                  pl.BlockSpec((None, None, n_cp, NSA_DIM), fixed),
                  pl.BlockSpec((None, None, NSA_DIM, n_cp), fixed),
                  pl.BlockSpec((None, None, S, ksa.shape[-1]), fixed),
                  pl.BlockSpec((None, None, S // tk, NSA_DIM, tk), fixed5),
                  pl.BlockSpec((None, None, S + WINDOW, 2 * NSA_DIM), fixed),
                  pl.BlockSpec((None, None, (S + WINDOW) // t, NSA_DIM, t), fixed5),
                  pl.BlockSpec((None, WINDOW + t, w), lambda b, g, i, s: (g, 0, 0)),
                  pl.BlockSpec((None, None, None, 3, w), tile5),
                  pl.BlockSpec((n_sel, n_cp), lambda b, g, i, s: (0, 0)),
                  pl.BlockSpec((nt_pad, n_sel), lambda b, g, i, s: (0, 0)),
                  pl.BlockSpec((nt_pad, nt_pad), lambda b, g, i, s: (0, 0))],
        out_specs=pl.BlockSpec((None, None, None, NSA_DIM, w), tile5),
        scratch_shapes=[pltpu.VMEM((NSA_DIM, w), F32),
                        pltpu.VMEM((NSA_DIM, w), F32),
                        pltpu.VMEM((tk, w), F32),
                        pltpu.VMEM((2 * NSA_DIM + SEL_COLS, w), BF16),
                        pltpu.VMEM((tk, w), F32), pltpu.VMEM((tk, w), F32),
                        pltpu.VMEM((tk, w), BF16), pltpu.VMEM((tk, w), BF16),
                        pltpu.VMEM((8, 128), I32), pltpu.SMEM((8, 128), I32), pltpu.SemaphoreType.DMA(())],
    )
    return pl.pallas_call(
        functools.partial(_nsa_kernel, t=t, tk=tk, n_top=n_top),
        grid_spec=grid_spec,
        out_shape=jax.ShapeDtypeStruct((B, G, nq, NSA_DIM, w), BF16),
        compiler_params=_cparams(("parallel", "parallel", "arbitrary")),
        name="nsa",
    )(slopes, qT, kc, vcT, ksa, vsT, kwa, vwT, wb, gT, mT, grp, ltri)


def _pack_halves(v):
    c = v.shape[1] // 2
    lo = lax.bitcast_convert_type(v[:, :c].astype(BF16).astype(F32), U32)
    hi = lax.bitcast_convert_type(v[:, c:].astype(BF16).astype(F32), U32)
    return hi | (lo >> 16)


def _unpack_halves(u):
    lo = lax.bitcast_convert_type(u << 16, F32)
    hi = lax.bitcast_convert_type(u & jnp.uint32(0xFFFF0000), F32)
    return lo, hi


def _store_slabs(ref, v):
    packed = _pack_halves(v)
    for s in range(SC_SPLIT):
        ref[s] = packed[:, s * SC_PIECE:(s + 1) * SC_PIECE]


def _load_slabs_bf16(ref):
    parts = [_unpack_halves(ref[s]) for s in range(SC_SPLIT)]
    return jnp.concatenate([p[0].astype(BF16) for p in parts] + [p[1].astype(BF16) for p in parts], axis=1)


def _merge_kernel(oa_ref, ob_ref, ga_ref, gb_ref, x_ref, g1_ref, nw_ref, sh_ref, sc_ref,
                  wa_ref, wb_ref, wo_ref, wr_ref, x1_ref, h2_ref, lg_ref):
    ya = jnp.dot(oa_ref[...], wa_ref[...], preferred_element_type=F32)
    yb = jnp.dot(ob_ref[...], wb_ref[...], preferred_element_type=F32)
    merged = (jax.nn.sigmoid(ga_ref[...].astype(F32)) * ya
              + jax.nn.sigmoid(gb_ref[...].astype(F32)) * yb).astype(BF16)
    mix = jnp.dot(merged, wo_ref[...], preferred_element_type=F32)
    x1 = x_ref[...] + g1_ref[...] * mix
    x1_ref[...] = x1
    h2 = _modulate(x1, nw_ref[...], sh_ref[...], sc_ref[...])
    _store_slabs(h2_ref, h2)
    h_hi = h2.astype(BF16)
    h_lo = (h2 - h_hi.astype(F32)).astype(BF16)
    lg_ref[...] = (jnp.dot(h_hi, wr_ref[0], preferred_element_type=F32)
                   + jnp.dot(h_lo, wr_ref[0], preferred_element_type=F32)
                   + jnp.dot(h_hi, wr_ref[1], preferred_element_type=F32))


def _merge(oa, ob, slab, x2, g1, nw, sh, sc, wa, wb, wo, wr, S):
    N, D = x2.shape
    tm = 256
    per_b = S // tm
    mcol = COL_MERGE // D
    row = lambda i: (i, 0)
    full = lambda i: (0, 0)
    perb = lambda i: (i // per_b, 0, 0)
    n_lg = wr.shape[-1]
    return pl.pallas_call(
        _merge_kernel,
        grid=(N // tm,),
        in_specs=[pl.BlockSpec((tm, oa.shape[1]), row),
                  pl.BlockSpec((tm, ob.shape[1]), row),
                  pl.BlockSpec((tm, D), lambda i: (i, mcol)),
                  pl.BlockSpec((tm, D), lambda i: (i, mcol + 1)),
                  pl.BlockSpec((tm, D), row),
                  pl.BlockSpec((None, 1, D), perb),
                  pl.BlockSpec((1, D), full),
                  pl.BlockSpec((None, 1, D), perb),
                  pl.BlockSpec((None, 1, D), perb),
                  pl.BlockSpec(wa.shape, full),
                  pl.BlockSpec(wb.shape, full),
                  pl.BlockSpec(wo.shape, full),
                  pl.BlockSpec(wr.shape, lambda i: (0, 0, 0))],
        out_specs=[pl.BlockSpec((tm, D), row), pl.BlockSpec((SC_SPLIT, tm, SC_PIECE), lambda i: (0, i, 0)),
                   pl.BlockSpec((tm, n_lg), row)],
        out_shape=[jax.ShapeDtypeStruct((N, D), F32), jax.ShapeDtypeStruct((SC_SPLIT, N, SC_PIECE), U32),
                   jax.ShapeDtypeStruct((N, n_lg), F32)],
        compiler_params=_cparams(("parallel",)),
        name="merge",
    )(oa, ob, slab, slab, x2, g1.reshape(-1, 1, D), nw.reshape(1, D), sh.reshape(-1, 1, D),
      sc.reshape(-1, 1, D), wa, wb, wo, wr)


def _first_max(v, idx):
    mx = jnp.max(v, axis=0, keepdims=True)
    first = jnp.min(jnp.where(v == mx, idx, float(v.shape[0])), axis=0, keepdims=True)
    return mx, first


def _route_kernel(lg_ref, b_ref, e_ref, w_ref):
    lg = lg_ref[...]
    tn = lg.shape[1]
    per_g = N_EXPERTS // N_EXPERT_GROUPS
    scores = jax.nn.sigmoid(lg)
    biased = scores + b_ref[...]
    gi = lax.broadcasted_iota(I32, (N_EXPERT_GROUPS, tn), 0).astype(F32)
    gscore = jnp.zeros((N_EXPERT_GROUPS, tn), F32)
    for g in range(N_EXPERT_GROUPS):
        grp = biased[g * per_g:(g + 1) * per_g, :]
        m1, f1 = _first_max(grp, gi)
        m2 = jnp.max(jnp.where(gi == f1, -jnp.inf, grp), axis=0, keepdims=True)
        gscore = jnp.where(gi == g, m1 + m2, gscore)
    gsel = jnp.zeros((N_EXPERT_GROUPS, tn), F32)
    for _ in range(TOPK_EXPERT_GROUPS):
        _, f = _first_max(gscore, gi)
        hit = gi == f
        gsel = jnp.where(hit, 1.0, gsel)
        gscore = jnp.where(hit, -jnp.inf, gscore)
    ei = lax.broadcasted_iota(I32, (N_EXPERTS, tn), 0).astype(F32)
    emask = jnp.zeros((N_EXPERTS, tn), F32)
    for g in range(N_EXPERT_GROUPS):
        in_g = (ei >= g * per_g) & (ei < (g + 1) * per_g)
        emask = jnp.where(in_g, gsel[g:g + 1, :], emask)
    masked = jnp.where(emask > 0.5, biased, NEG)
    e_out = jnp.zeros((MOE_TOPK, tn), F32)
    w_out = jnp.zeros((MOE_TOPK, tn), F32)
    for r in range(MOE_TOPK):
        _, f = _first_max(masked, ei)
        hit = ei == f
        wv = jnp.sum(jnp.where(hit, scores, 0.0), axis=0, keepdims=True)
        e_out = jnp.where(gi == r, f, e_out)
        w_out = jnp.where(gi == r, wv, w_out)
        masked = jnp.where(hit, -jnp.inf, masked)
    e_ref[...] = e_out.astype(I32)
    w_ref[...] = w_out / jnp.sum(w_out, axis=0, keepdims=True) * ROUTED_SCALE


def _route(lgT, rb):
    E, N = lgT.shape
    tn = 512
    return pl.pallas_call(
        _route_kernel,
        grid=(N // tn,),
        in_specs=[pl.BlockSpec((E, tn), lambda i: (0, i)), pl.BlockSpec((E, 1), lambda i: (0, 0))],
        out_specs=[pl.BlockSpec((MOE_TOPK, tn), lambda i: (0, i)), pl.BlockSpec((MOE_TOPK, tn), lambda i: (0, i))],
        out_shape=[jax.ShapeDtypeStruct((MOE_TOPK, N), I32), jax.ShapeDtypeStruct((MOE_TOPK, N), F32)],
        compiler_params=_cparams(("parallel",)),
        name="route",
    )(lgT, rb.reshape(E, 1).astype(F32))


def _slots_kernel(e_ref, tri_ref, lt_ref, dest_ref, cnt_ref, carry_ref, *, rb):
    phase = pl.program_id(0)
    i = pl.program_id(1)
    e = e_ref[...]
    tn = e.shape[1]
    ei = lax.broadcasted_iota(I32, (N_EXPERTS, tn), 0)

    @pl.when((phase == 0) & (i == 0))
    def _():
        carry_ref[...] = jnp.zeros_like(carry_ref)

    @pl.when(phase == 0)
    def _():
        tot = jnp.zeros((N_EXPERTS, 1), F32)
        for k in range(MOE_TOPK):
            oh = jnp.where(ei == e[k:k + 1, :], 1.0, 0.0)
            tot = tot + jnp.sum(oh, axis=1, keepdims=True)
        carry_ref[...] = carry_ref[...] + tot
        dest_ref[...] = jnp.zeros_like(dest_ref)

    @pl.when((phase == 1) & (i == 0))
    def _():
        cnt = carry_ref[...]
        cnt_ref[...] = cnt
        padded = jnp.broadcast_to(jnp.ceil(cnt / rb) * rb, (N_EXPERTS, 128))
        first = jnp.dot(lt_ref[...], padded, preferred_element_type=F32, precision=HIGHEST)
        carry_ref[...] = first[:, :1]

    @pl.when(phase == 1)
    def _():
        carry = carry_ref[...]
        ki = lax.broadcasted_iota(I32, (MOE_TOPK, tn), 0)
        dest = jnp.zeros((MOE_TOPK, tn), F32)
        for k in range(MOE_TOPK):
            hit = ei == e[k:k + 1, :]
            oh = jnp.where(hit, 1.0, 0.0)
            before = jnp.dot(oh.astype(BF16), tri_ref[...], preferred_element_type=F32)
            row = jnp.sum(jnp.where(hit, before + carry, 0.0), axis=0, keepdims=True)
            dest = jnp.where(ki == k, row, dest)
            carry = carry + jnp.sum(oh, axis=1, keepdims=True)
        carry_ref[...] = carry
        dest_ref[...] = dest.astype(I32)


def _slots(top_eT, rb):
    K, N = top_eT.shape
    tn = 512
    idx = np.arange(tn)
    tri = jnp.asarray((idx[:, None] < idx[None, :]).astype(np.float32), BF16)
    e_idx = np.arange(N_EXPERTS)
    lt = jnp.asarray((e_idx[None, :] < e_idx[:, None]).astype(np.float32), F32)
    return pl.pallas_call(
        functools.partial(_slots_kernel, rb=rb),
        grid=(2, N // tn),
        in_specs=[pl.BlockSpec((K, tn), lambda p, i: (0, i)),
                  pl.BlockSpec((tn, tn), lambda p, i: (0, 0)),
                  pl.BlockSpec((N_EXPERTS, N_EXPERTS), lambda p, i: (0, 0))],
        out_specs=[pl.BlockSpec((K, tn), lambda p, i: (0, i * p)),
                   pl.BlockSpec((N_EXPERTS, 1), lambda p, i: (0, 0))],
        out_shape=[jax.ShapeDtypeStruct((K, N), I32), jax.ShapeDtypeStruct((N_EXPERTS, 1), F32)],
        scratch_shapes=[pltpu.VMEM((N_EXPERTS, 1), F32)],
        compiler_params=_cparams(("arbitrary", "arbitrary")),
        name="slots",
    )(top_eT, tri, lt)


def _sc_scatter_rows(src, idx, n_out, repeat):
    width = src.shape[1]
    n_tok = src.shape[0] // SC_SPLIT
    w = n_tok // SC_WINDOW
    mesh = plsc.VectorSubcoreMesh(core_axis_name="c", subcore_axis_name="s")

    @pl.kernel(out_type=jax.ShapeDtypeStruct((n_out, width), src.dtype), mesh=mesh)
    def scatter_kernel(x_hbm, i_hbm, o_hbm):
        def body(x_vmem, i_vmem):
            for k in range(repeat):
                pltpu.sync_copy(x_vmem, o_hbm.at[i_vmem.at[k]])

        pltpu.emit_pipeline(
            body,
            grid=(SC_SPLIT * w,),
            in_specs=[pl.BlockSpec((SC_WINDOW, width), lambda g: (g, 0)),
                      pl.BlockSpec((repeat, SC_WINDOW), lambda g: (g // w, g % w))],
            out_specs=[],
            core_axis_name=("c", "s"),
            dimension_semantics=(pltpu.PARALLEL,),
        )(x_hbm, i_hbm)

    return scatter_kernel(src, idx.reshape(SC_SPLIT * repeat, n_tok))


def _zero_pads_kernel(start_ref, len_ref, xs_in_ref, xs_ref, zeros_ref, sem, *, rb, n_rows):
    del xs_in_ref
    zeros_ref[...] = jnp.zeros_like(zeros_ref)
    n_seg = start_ref.shape[0]
    bits = rb.bit_length() - 1
    assert rb == 1 << bits

    def for_each_copy(act):
        def rows(row0, size):
            for s in range(SC_SPLIT):
                act(pltpu.make_async_copy(zeros_ref.at[pl.ds(0, size), :],
                                          xs_ref.at[pl.ds(s * n_rows + row0, size), :], sem))

        def segment(e, c):
            start = start_ref[e]
            length = len_ref[e]
            end = start + length
            n_whole = length // rb

            def whole(j, c2):
                rows(pl.multiple_of(end - (j + 1) * rb, SUBLANES), rb)
                return c2

            lax.fori_loop(0, n_whole, whole, 0)
            off = end - n_whole * rb
            rest = length - n_whole * rb
            for b in reversed(range(SUBLANES.bit_length() - 1, bits)):
                size = 1 << b
                take = jnp.bitwise_and(jnp.right_shift(rest, b), 1)
                off = off - take * size

                @pl.when(take == 1)
                def _(off=off, size=size):
                    rows(pl.multiple_of(off, SUBLANES), size)

            lead = jnp.bitwise_and(rest, SUBLANES - 1)
            for i in range(SUBLANES - 1):
                @pl.when(i < lead)
                def _(i=i):
                    rows(start + i, 1)

            return c

        lax.fori_loop(0, n_seg, segment, 0)

    for_each_copy(lambda cp: cp.start())
    for_each_copy(lambda cp: cp.wait())


def _zero_pads(xs2d, start, length, rb, n_rows):
    grid_spec = pltpu.PrefetchScalarGridSpec(
        num_scalar_prefetch=2,
        grid=(1,),
        in_specs=[pl.BlockSpec(memory_space=pl.ANY)],
        out_specs=pl.BlockSpec(memory_space=pl.ANY),
        scratch_shapes=[pltpu.VMEM((rb, xs2d.shape[1]), xs2d.dtype), pltpu.SemaphoreType.DMA(())],
    )
    return pl.pallas_call(
        functools.partial(_zero_pads_kernel, rb=rb, n_rows=n_rows),
        grid_spec=grid_spec,
        out_shape=jax.ShapeDtypeStruct(xs2d.shape, xs2d.dtype),
        input_output_aliases={2: 0},
        compiler_params=_cparams(("arbitrary",)),
        name="zeropads",
    )(start, length, xs2d)


def _experts_kernel(be_ref, x_ref, wg_ref, wu_ref, wd_ref, y_ref, wgu_sc, wd_sc):
    i = pl.program_id(0)
    hdim = wg_ref.shape[1]

    @pl.when((i == 0) | (be_ref[i] != be_ref[jnp.maximum(i - 1, 0)]))
    def _():
        wgu_sc[:, 0:hdim] = wg_ref[...].astype(BF16)
        wgu_sc[:, hdim:] = wu_ref[...].astype(BF16)
        wd_sc[...] = wd_ref[...].astype(BF16)

    gu = jnp.dot(_load_slabs_bf16(x_ref), wgu_sc[...], preferred_element_type=F32)
    g = gu[:, :hdim]
    act = (g * jax.nn.sigmoid(g) * gu[:, hdim:]).astype(BF16)
    _store_slabs(y_ref, jnp.dot(act, wd_sc[...], preferred_element_type=F32))


def _experts(blk_e, xs, wg, wu, wd, rb):
    n_rows = xs.shape[1]
    d_model, hdim = wg.shape[1:]
    per_expert = lambda i, be: (be[i], 0, 0)
    grid_spec = pltpu.PrefetchScalarGridSpec(
        num_scalar_prefetch=1,
        grid=(n_rows // rb,),
        in_specs=[pl.BlockSpec((SC_SPLIT, rb, SC_PIECE), lambda i, be: (0, i, 0)),
                  pl.BlockSpec((None, d_model, hdim), per_expert),
                  pl.BlockSpec((None, d_model, hdim), per_expert),
                  pl.BlockSpec((None, hdim, d_model), per_expert)],
        out_specs=pl.BlockSpec((SC_SPLIT, rb, SC_PIECE), lambda i, be: (0, i, 0)),
        scratch_shapes=[pltpu.VMEM((d_model, 2 * hdim), BF16), pltpu.VMEM((hdim, d_model), BF16)],
    )
    return pl.pallas_call(
        _experts_kernel,
        grid_spec=grid_spec,
        out_shape=jax.ShapeDtypeStruct((SC_SPLIT, n_rows, SC_PIECE), U32),
        compiler_params=_cparams(("arbitrary",)),
        name="experts",
    )(blk_e, xs, wg, wu, wd)


def _sc_gather_rows(table, idx):
    n = idx.shape[0]
    width = table.shape[1]
    mesh = plsc.VectorSubcoreMesh(core_axis_name="c", subcore_axis_name="s")

    @pl.kernel(out_type=jax.ShapeDtypeStruct((n, width), table.dtype), mesh=mesh)
    def gather_kernel(t_hbm, i_hbm, o_hbm):
        def body(i_vmem, o_vmem):
            pltpu.sync_copy(t_hbm.at[i_vmem.at[0]], o_vmem)

        pltpu.emit_pipeline(
            body,
            grid=(n // SC_WINDOW,),
            in_specs=[pl.BlockSpec((1, SC_WINDOW), lambda i: (0, i))],
            out_specs=[pl.BlockSpec((SC_WINDOW, width), lambda i: (i, 0))],
            core_axis_name=("c", "s"),
            dimension_semantics=(pltpu.PARALLEL,),
        )(i_hbm, o_hbm)

    return gather_kernel(table, idx.reshape(1, n))


def _combine_kernel(yg_ref, w_ref, h_ref, x1_ref, g2_ref, fw_ref, sgu_ref, sd_ref, o_ref):
    gu = jnp.dot(_load_slabs_bf16(h_ref), sgu_ref[...], preferred_element_type=F32)
    hdim = gu.shape[1] // 2
    g = gu[:, :hdim]
    ffn = jnp.dot((g * jax.nn.sigmoid(g) * gu[:, hdim:]).astype(BF16), sd_ref[...], preferred_element_type=F32)

    w = w_ref[...]
    tt = w.shape[0]
    lo = [jnp.zeros((tt, SC_PIECE), F32) for _ in range(SC_SPLIT)]
    hi = [jnp.zeros((tt, SC_PIECE), F32) for _ in range(SC_SPLIT)]
    for k in range(MOE_TOPK):
        wk = w[:, k:k + 1]
        for s in range(SC_SPLIT):
            y_lo, y_hi = _unpack_halves(yg_ref[s, k])
            lo[s] = lo[s] + wk * y_lo
            hi[s] = hi[s] + wk * y_hi
    ffn = ffn + jnp.concatenate(lo + hi, axis=1)
    x2 = x1_ref[...] + g2_ref[...] * ffn
    r = lax.rsqrt(jnp.mean(x2 * x2, axis=-1, keepdims=True) + NORM_EPS)
    o_ref[...] = x2 * r * fw_ref[...]


def _combine(yg, top_w, h2p, x1, g2, fw, sgu, sd, S, tt):
    N, D = x1.shape
    per_b = S // tt
    row = lambda i: (i, 0)
    full = lambda i: (0, 0)
    return pl.pallas_call(
        _combine_kernel,
        grid=(N // tt,),
        in_specs=[pl.BlockSpec((SC_SPLIT, MOE_TOPK, tt, SC_PIECE), lambda i: (0, 0, i, 0)),
                  pl.BlockSpec((tt, MOE_TOPK), row),
                  pl.BlockSpec((SC_SPLIT, tt, SC_PIECE), lambda i: (0, i, 0)),
                  pl.BlockSpec((tt, D), row),
                  pl.BlockSpec((None, 1, D), lambda i: (i // per_b, 0, 0)),
                  pl.BlockSpec((1, D), full),
                  pl.BlockSpec(sgu.shape, full),
                  pl.BlockSpec(sd.shape, full)],
        out_specs=pl.BlockSpec((tt, D), row),
        out_shape=jax.ShapeDtypeStruct((N, D), F32),
        compiler_params=_cparams(("parallel",)),
        name="combine",
    )(yg, top_w, h2p, x1, g2.reshape(-1, 1, D), fw.reshape(1, D), sgu, sd)


def _cmp_to_sel_T(n_cp, n_sel):
    c0 = np.arange(n_cp)[None, :] * CMP_STRIDE
    s0 = np.arange(n_sel)[:, None] * SEL_BLOCK
    ov = np.minimum(c0 + CMP_BLOCK, s0 + SEL_BLOCK) - np.maximum(c0, s0)
    m = np.clip(ov, 0, None).astype(np.float32) / CMP_BLOCK
    m[:, n_cp - 1] = 0.0
    return jnp.asarray(m, BF16)


def _slab_weights(w_in):
    D = w_in.shape[0]
    sizes = [1024, 1024, 1024, 512, 128, 128, 128, 128, 128, 128, 24, 2048]
    offs = np.concatenate([[0], np.cumsum(sizes)])
    part = lambda i: w_in[:, offs[i]:offs[i + 1]]
    pieces = [part(0), part(1), part(2), part(11), part(3)] + [part(i) for i in range(4, 10)] + [part(10)]
    w = jnp.concatenate(pieces, axis=1)
    return jnp.pad(w, ((0, 0), (0, SLAB_COLS - w.shape[1]))).astype(BF16)


def _layer(x, c, ada_w, ada_b, norm1_w, w_in, da_lq1, da_lk1, da_lq2, da_lk2, da_subln_w,
           cmp_k_pe, cmp_k_w1, cmp_k_w2, cmp_v_pe, cmp_v_w1, cmp_v_w2, w_da_out, w_nsa_out, w_o,
           norm2_w, router_w, router_b, exp_w_gate, exp_w_up, exp_w_down,
           sh_w_gate, sh_w_up, sh_w_down, final_norm_w, lam_init):
    B, S, D = x.shape
    N = B * S
    G, HPG = NSA_GROUPS, NSA_HPG
    x2 = x.reshape(N, D)

    mod = _ada(c, ada_w, ada_b)
    sh1, sc1, g1, sh2, sc2, g2 = jnp.split(mod, 6, axis=-1)

    slab = _inproj(x2, norm1_w, sh1, sc1, _slab_weights(w_in), S)
    slab3 = slab.reshape(B, S, SLAB_COLS)

    i_all = np.arange(1, DA_HEADS + NSA_HEADS + 1, dtype=np.float32)
    slopes = (2.0 ** (-8.0 * i_all / (DA_HEADS + NSA_HEADS))).astype(np.float32)
    slopes_a = jnp.asarray(slopes[0::2])
    slopes_b = jnp.asarray(slopes[1::2])

    ta_q, ta_k = min(512, S), min(512, S)
    vT = slab3[:, :, COL_DAV:COL_DAV + 1024].reshape(B, S // ta_k, ta_k, DA_HEADS, DA_VDIM).transpose(0, 3, 1, 4, 2)
    kmin = _first_live_tile(slopes[0::2], slab3, ta_q, ta_k)
    oa = _diff_attention(slopes_a, kmin, slab3, _with_ones_row(vT), _slope_rows(slopes[0::2]), da_lq1, da_lk1,
                         da_lq2, da_lk2, da_subln_w, lam_init, ta_q, ta_k).reshape(N, DA_HEADS * DA_VDIM)

    tb, tb_k = 256, min(256, S)
    n_cp = S // CMP_STRIDE
    n_sel = S // SEL_BLOCK

    def kv_groups(j):
        c0 = COL_KV6 + 128 * j
        return slab3[:, :, c0:c0 + 128].reshape(B, S, G, NSA_DIM).transpose(0, 2, 1, 3)

    def kv_tiles_T(a, rows):
        return a.reshape(B, G, S // rows, rows, NSA_DIM).transpose(0, 1, 2, 4, 3)

    ck_rows = kv_groups(0).reshape(B * G, n_cp, CMP_STRIDE * NSA_DIM)
    cv_rows = kv_groups(1).reshape(B * G, n_cp, CMP_STRIDE * NSA_DIM)
    kc = _compress(ck_rows, cmp_k_pe, cmp_k_w1, cmp_k_w2).reshape(B, G, n_cp, NSA_DIM).astype(BF16)
    vc = _compress(cv_rows, cmp_v_pe, cmp_v_w1, cmp_v_w2).reshape(B, G, n_cp, NSA_DIM).astype(BF16)
    nq = (slab3[:, :, COL_NSAQ:COL_NSAQ + 512] * jnp.asarray(NSA_DIM ** -0.5, BF16))
    nqT = nq.reshape(B, S // tb, tb, G, HPG, NSA_DIM).transpose(0, 3, 1, 5, 4, 2).reshape(B, G, S // tb, NSA_DIM, HPG * tb)
    gT = (slab3[:, :, COL_NSAG:COL_NSAG + 3 * NSA_HEADS].astype(F32).reshape(B, S // tb, tb, G, HPG, 3)
          .transpose(0, 3, 1, 5, 4, 2).reshape(B, G, S // tb, 3, HPG * tb))
    onehot = jnp.asarray((np.arange(S)[:, None] // SEL_BLOCK == np.arange(SEL_COLS)[None, :]).astype(np.float32), BF16)
    zeros64 = jnp.zeros((B, G, S, NSA_DIM), BF16)
    ksa = jnp.concatenate([kv_groups(2), zeros64, jnp.broadcast_to(onehot, (B, G, S, SEL_COLS))], axis=-1)
    pad_rows = jnp.zeros((B, G, WINDOW, 2 * NSA_DIM), BF16).at[..., NSA_DIM].set(1.0)
    kwa = jnp.concatenate([pad_rows, jnp.concatenate([kv_groups(4), zeros64], axis=-1)], axis=2)
    vw_pad = jnp.pad(kv_groups(5), ((0, 0), (0, 0), (WINDOW, 0), (0, 0)))
    vwT = vw_pad.reshape(B, G, (S + WINDOW) // tb, tb, NSA_DIM).transpose(0, 1, 2, 4, 3)
    r_w = np.arange(WINDOW + tb)[:, None]
    lane_w = np.arange(HPG * tb)[None, :]
    d_w = (WINDOW + lane_w % tb - r_w).astype(np.float32)
    slope_w = slopes[1::2].reshape(G, 1, HPG)[:, :, lane_w[0] // tb]
    wb = jnp.asarray(np.where((d_w >= 0) & (d_w < WINDOW), -slope_w * d_w[None], np.float32(NEG)).astype(np.float32))
    obT = _nsa_attention(slopes_b, nqT, kc, vc.transpose(0, 1, 3, 2), ksa, kv_tiles_T(kv_groups(3), tb_k),
                         kwa, vwT, wb, gT, _cmp_to_sel_T(n_cp, n_sel), tb, tb_k)
    ob = (obT.reshape(B, G, S // tb, NSA_DIM, HPG, tb).transpose(0, 2, 5, 1, 4, 3)
          .reshape(N, NSA_HEADS * NSA_DIM))

    wr_hi = router_w.astype(BF16)
    wr_lo = (router_w.astype(F32) - wr_hi.astype(F32)).astype(BF16)
    x1, h2, logits = _merge(oa, ob, slab, x2, g1, norm2_w, sh2, sc2, w_da_out.astype(BF16),
                            w_nsa_out.astype(BF16), w_o.astype(BF16), jnp.stack([wr_hi, wr_lo]), S)

    rb = 512
    tt = 128
    top_eT, top_wT = _route(logits.T, router_b)
    destT, counts = _slots(top_eT, rb)
    n_rows = ((N * MOE_TOPK + N_EXPERTS * (rb - 1) + rb - 1) // rb) * rb
    padded = (jnp.ceil(counts[:, 0] / rb) * rb).astype(I32)
    pend = jnp.cumsum(padded)
    blk_start = jnp.arange(n_rows // rb, dtype=I32) * rb
    blk_e = jnp.minimum(jnp.sum((pend[None, :] <= blk_start[:, None]).astype(I32), axis=1), N_EXPERTS - 1)
    d_flat = destT.reshape(-1)
    piece_idx = jnp.concatenate([d_flat + s * n_rows for s in range(SC_SPLIT)])
    xs2d = _sc_scatter_rows(h2.reshape(SC_SPLIT * N, SC_PIECE), piece_idx, SC_SPLIT * n_rows, MOE_TOPK)
    cnt = counts[:, 0].astype(I32)
    pad_start = jnp.concatenate([pend - padded + cnt, pend[-1:]])
    pad_len = jnp.concatenate([padded - cnt, n_rows - pend[-1:]])
    xs = _zero_pads(xs2d, pad_start, pad_len, rb, n_rows).reshape(SC_SPLIT, n_rows, SC_PIECE)
    ys = _experts(blk_e, xs, exp_w_gate, exp_w_up, exp_w_down, rb)
    sgu = jnp.concatenate([sh_w_gate, sh_w_up], axis=-1).astype(BF16)
    yg = _sc_gather_rows(ys.reshape(SC_SPLIT * n_rows, SC_PIECE), piece_idx).reshape(SC_SPLIT, MOE_TOPK, N, SC_PIECE)
    out = _combine(yg, top_wT.T, h2, x1, g2, final_norm_w, sgu, sh_w_down.astype(BF16), S, 2 * tt)
    return out.reshape(B, S, D)


def kernel(x, c, ada_w, ada_b, norm1_w, w_in, da_lq1, da_lk1, da_lq2, da_lk2, da_subln_w, cmp_k_pe, cmp_k_w1, cmp_k_w2, cmp_v_pe, cmp_v_w1, cmp_v_w2, w_da_out, w_nsa_out, w_o, norm2_w, router_w, router_b, exp_w_gate, exp_w_up, exp_w_down, sh_w_gate, sh_w_up, sh_w_down, final_norm_w):
    depth = ada_w.shape[0]
    assert depth == 1, "one decoder layer"
    lam_init = 0.8 - 0.6 * math.exp(-0.3 * 0)
    return _layer(x, c, ada_w[0], ada_b[0], norm1_w[0], w_in[0], da_lq1[0], da_lk1[0], da_lq2[0], da_lk2[0],
                  da_subln_w[0], cmp_k_pe[0], cmp_k_w1[0], cmp_k_w2[0], cmp_v_pe[0], cmp_v_w1[0], cmp_v_w2[0],
                  w_da_out[0], w_nsa_out[0], w_o[0], norm2_w[0], router_w[0], router_b[0],
                  exp_w_gate[0], exp_w_up[0], exp_w_down[0], sh_w_gate[0], sh_w_up[0], sh_w_down[0],
                  final_norm_w, lam_init)
```

```python
import functools
import math

import numpy as np
import jax
import jax.numpy as jnp
from jax import lax
from jax.experimental import pallas as pl
from jax.experimental.pallas import tpu as pltpu
from jax.experimental.pallas import tpu_sc as plsc

F32 = jnp.float32
BF16 = jnp.bfloat16
I32 = jnp.int32
U32 = jnp.uint32
HIGHEST = lax.Precision.HIGHEST

NORM_EPS = 1e-6
NEG = -1e30

DA_HEADS = 8
DA_DIM = 64
DA_VDIM = 128
DA_SUBLN_EPS = 1e-5

NSA_HEADS = 8
NSA_GROUPS = 2
NSA_HPG = 4
NSA_DIM = 64
CMP_BLOCK = 32
CMP_STRIDE = 16
SEL_BLOCK = 64
SEL_TOPK = 16
WINDOW = 512
FORCE_BONUS = 1e4
SEL_COLS = 128

N_EXPERTS = 64
MOE_TOPK = 8
N_EXPERT_GROUPS = 8
TOPK_EXPERT_GROUPS = 4
ROUTED_SCALE = 2.5

VMEM_LIMIT_V7X = 56 * 1024 * 1024

COL_DAQ, COL_DAK, COL_DAV = 0, 1024, 2048
COL_MERGE = 3072
COL_NSAQ = 5120
COL_KV6 = 5632
COL_NSAG = 6400
SLAB_COLS = 6528


SUBLANES = 8
SC_WINDOW = 128
SC_SPLIT = 2
SC_PIECE = 256

POS_SPLIT = 16
POS_ROWS = 16
ONES_ROWS = 16


def _pos_features(rows):
    r = np.arange(rows)
    f = np.zeros((rows, 128), np.float32)
    f[:, 0:3] = (r // POS_SPLIT)[:, None]
    f[:, 3:6] = (r % POS_SPLIT)[:, None]
    return jnp.asarray(f, BF16)


def _slope_rows(slopes):
    s = np.asarray(slopes, np.float32)
    bf = lambda x: x.astype(BF16).astype(np.float32)
    p1 = bf(s)
    p2 = bf(s - p1)
    p3 = bf(s - p1 - p2)
    out = np.zeros((s.shape[0], POS_ROWS, 128), np.float32)
    for i, piece in enumerate((p1, p2, p3)):
        out[:, i, :] = POS_SPLIT * piece[:, None]
        out[:, 3 + i, :] = piece[:, None]
    return jnp.asarray(out)


def _with_ones_row(vT):
    shape = vT.shape[:-2] + (ONES_ROWS, vT.shape[-1])
    extra = jnp.zeros(shape, vT.dtype).at[..., 0, :].set(1.0)
    return jnp.concatenate([vT, extra], axis=-2)


def _cparams(sem, vmem=VMEM_LIMIT_V7X):
    return pltpu.CompilerParams(dimension_semantics=sem, vmem_limit_bytes=vmem)


def _ada_kernel(c_ref, w_ref, b_ref, o_ref):
    c = c_ref[...]
    ca = c * jax.nn.sigmoid(c)
    o_ref[...] = jnp.dot(ca, w_ref[...], preferred_element_type=F32, precision=HIGHEST) + b_ref[...]


def _ada(c, w, b):
    B, D = c.shape
    n_out = w.shape[1]
    rows = 8
    cp = jnp.zeros((rows, D), F32).at[:B].set(c)
    tn = 1024
    out = pl.pallas_call(
        _ada_kernel,
        grid=(n_out // tn,),
        in_specs=[pl.BlockSpec((rows, D), lambda j: (0, 0)),
                  pl.BlockSpec((D, tn), lambda j: (0, j)),
                  pl.BlockSpec((1, tn), lambda j: (0, j))],
        out_specs=pl.BlockSpec((rows, tn), lambda j: (0, j)),
        out_shape=jax.ShapeDtypeStruct((rows, n_out), F32),
        compiler_params=_cparams(("arbitrary",)),
        name="ada",
    )(cp, w, b.reshape(1, n_out))
    return out[:B]


def _modulate(x, nw, sh, sc):
    r = lax.rsqrt(jnp.mean(x * x, axis=-1, keepdims=True) + NORM_EPS)
    return (x * r) * nw * (1.0 + sc) + sh


def _inproj_kernel(x_ref, nw_ref, sh_ref, sc_ref, w_ref, o_ref, h_ref):
    @pl.when(pl.program_id(1) == 0)
    def _():
        h_ref[...] = _modulate(x_ref[...], nw_ref[...], sh_ref[...], sc_ref[...]).astype(BF16)

    o_ref[...] = jnp.dot(h_ref[...], w_ref[...], preferred_element_type=F32).astype(BF16)


def _inproj(x2, nw, sh, sc, w_slab, S):
    N, D = x2.shape
    tm = min(1024, S)
    tn = SLAB_COLS // 3
    per_b = S // tm
    return pl.pallas_call(
        _inproj_kernel,
        grid=(N // tm, SLAB_COLS // tn),
        in_specs=[pl.BlockSpec((tm, D), lambda i, j: (i, 0)),
                  pl.BlockSpec((1, D), lambda i, j: (0, 0)),
                  pl.BlockSpec((None, 1, D), lambda i, j: (i // per_b, 0, 0)),
                  pl.BlockSpec((None, 1, D), lambda i, j: (i // per_b, 0, 0)),
                  pl.BlockSpec((D, tn), lambda i, j: (0, j))],
        out_specs=pl.BlockSpec((tm, tn), lambda i, j: (i, j)),
        out_shape=jax.ShapeDtypeStruct((N, SLAB_COLS), BF16),
        scratch_shapes=[pltpu.VMEM((tm, D), BF16)],
        compiler_params=_cparams(("parallel", "arbitrary")),
        name="inproj",
    )(x2, nw.reshape(1, D), sh.reshape(-1, 1, D), sc.reshape(-1, 1, D), w_slab)


def _cmp_kernel(r_ref, pe_ref, w1a_ref, w1b_ref, w2_ref, o_ref):
    r = r_ref[...]
    n_rows = r.shape[0]
    a = jnp.dot(r, w1a_ref[...], preferred_element_type=F32)
    b = jnp.dot(r, w1b_ref[...], preferred_element_type=F32)
    pe = pe_ref[...]
    half = pe.shape[1] // 2
    bias = (jnp.dot(pe[:, :half], w1a_ref[...].astype(F32), preferred_element_type=F32, precision=HIGHEST)
            + jnp.dot(pe[:, half:], w1b_ref[...].astype(F32), preferred_element_type=F32, precision=HIGHEST))
    hid = a + pltpu.roll(b, n_rows - 1, 0) + bias
    act = 0.5 * hid * (1.0 + jnp.tanh(0.7978845608028654 * (hid + 0.044715 * hid * hid * hid)))
    o_ref[...] = jnp.dot(act, w2_ref[...], preferred_element_type=F32, precision=HIGHEST)


def _compress(rows, pe, w1, w2):
    BG, n_rows, width = rows.shape
    hidden = w1.shape[1]
    w1a = w1[:width].astype(BF16)
    w1b = w1[width:].astype(BF16)
    return pl.pallas_call(
        _cmp_kernel,
        grid=(BG,),
        in_specs=[pl.BlockSpec((None, n_rows, width), lambda i: (i, 0, 0)),
                  pl.BlockSpec((1, 2 * width), lambda i: (0, 0)),
                  pl.BlockSpec((width, hidden), lambda i: (0, 0)),
                  pl.BlockSpec((width, hidden), lambda i: (0, 0)),
                  pl.BlockSpec((hidden, NSA_DIM), lambda i: (0, 0))],
        out_specs=pl.BlockSpec((None, n_rows, NSA_DIM), lambda i: (i, 0, 0)),
        out_shape=jax.ShapeDtypeStruct((BG, n_rows, NSA_DIM), F32),
        compiler_params=_cparams(("parallel",)),
        name="cmp",
    )(rows, pe.reshape(1, -1).astype(F32), w1a, w1b, w2.astype(F32))


def _da_kernel(slopes_ref, kmin_ref, q_ref, k_ref, vT_ref, pos_ref, rel_ref, srow_ref, lq1_ref, lk1_ref, lq2_ref,
               lk2_ref, subw_ref, o_ref, acc_ref, qa_ref, sa_ref, sb_ref, pa_ref, pb_ref, *, tq, tk, lam_init):
    h = pl.program_id(1)
    qi = pl.program_id(2)
    slope = slopes_ref[h]
    q0 = qi * tq
    w = 2 * tq
    nk = k_ref.shape[0] // tk
    qT = (q_ref[...].astype(F32) * DA_DIM ** -0.5).T.astype(BF16)
    row = lax.broadcasted_iota(I32, qT.shape, 0)
    zero = jnp.zeros_like(qT)
    qa_ref[0:2 * DA_DIM, :] = jnp.concatenate(
        [jnp.where(row < DA_DIM, qT, zero), jnp.where(row >= DA_DIM, qT, zero)], axis=1)
    qa_ref[2 * DA_DIM:2 * DA_DIM + POS_ROWS, :] = jnp.concatenate([srow_ref[...]] * (w // 128), axis=1).astype(BF16)
    qa_ref[2 * DA_DIM + POS_ROWS:, :] = jnp.zeros((2 * DA_DIM - POS_ROWS, w), BF16)

    def scores(kt, s_ref):
        k0 = kt * tk
        k_t = k_ref[pl.ds(pl.multiple_of(k0, tk), tk), :]
        s = jnp.dot(jnp.concatenate([k_t, pos_ref[...]], axis=1), qa_ref[...], preferred_element_type=F32)
        s = jnp.where(rel_ref[...] <= q0 - k0, s, NEG)
        s_ref[...] = s
        return jnp.max(s, axis=0, keepdims=True)

    def probs(s_ref, p_ref, mx, off, m_old):
        m_new = jnp.maximum(m_old, mx + off)
        p_ref[...] = jnp.exp(s_ref[...] - (m_new - off)).astype(BF16)
        return m_new, jnp.exp(m_old - m_new)

    def accumulate(kt, p_ref, alpha):
        acc_ref[...] = alpha * acc_ref[...] + jnp.dot(vT_ref[kt], p_ref[...], preferred_element_type=F32)

    n_tiles = (q0 + tq - 1) // tk + 1
    kt_min = kmin_ref[(pl.program_id(0) * pl.num_programs(1) + h) * pl.num_programs(2) + qi]
    acc_ref[...] = jnp.zeros_like(acc_ref)
    pb_ref[...] = jnp.zeros_like(pb_ref)
    mx_a = scores(kt_min, sa_ref)

    def pair(j, carry):
        m, mx_a, alpha_b = carry
        ta = kt_min + 2 * j
        tb = ta + 1
        off_a = slope * (ta * tk).astype(F32)
        off_b = jnp.where(tb < n_tiles, slope * (tb * tk).astype(F32), NEG)
        accumulate(jnp.maximum(ta - 1, 0), pb_ref, alpha_b)
        m, alpha_a = probs(sa_ref, pa_ref, mx_a, off_a, m)
        mx_b = scores(jnp.minimum(tb, nk - 1), sb_ref)
        accumulate(ta, pa_ref, alpha_a)
        m, alpha_b = probs(sb_ref, pb_ref, mx_b, off_b, m)
        mx_a = scores(jnp.minimum(ta + 2, nk - 1), sa_ref)
        return m, mx_a, alpha_b

    n_pairs = (n_tiles - kt_min + 1) // 2
    init = (jnp.full((1, w), NEG, F32), mx_a, jnp.ones((1, w), F32))
    _, _, alpha_b = lax.fori_loop(0, n_pairs, pair, init)
    accumulate(jnp.minimum(kt_min + 2 * n_pairs - 1, nk - 1), pb_ref, alpha_b)

    lam = (jnp.exp(jnp.sum(lq1_ref[...] * lk1_ref[...], axis=-1, keepdims=True))
           - jnp.exp(jnp.sum(lq2_ref[...] * lk2_ref[...], axis=-1, keepdims=True)) + lam_init)
    on = acc_ref[0:DA_VDIM, :] * (1.0 / acc_ref[DA_VDIM:DA_VDIM + 1, :])
    o = on[:, :tq] - lam * on[:, tq:]
    r = lax.rsqrt(jnp.mean(o * o, axis=0, keepdims=True) + DA_SUBLN_EPS)
    o_ref[...] = (o * r * subw_ref[...] * (1.0 - lam_init)).T.astype(BF16)


EXP_UNDERFLOW = 104.0


def _first_live_tile(slopes_np, slab3, tq, tk):
    B, S, _ = slab3.shape
    H = DA_HEADS
    width = 2 * DA_DIM * H
    grp = jnp.asarray((np.arange(width)[:, None] // DA_DIM == np.arange(2 * H)[None, :]).astype(np.float32), BF16)

    def norms(c0):
        v = slab3[:, :, c0:c0 + width]
        return jnp.sqrt(jnp.einsum('bsc,cg->bsg', v * v, grp, preferred_element_type=F32))

    qn = norms(COL_DAQ) * DA_DIM ** -0.5
    qn = jnp.max(qn.reshape(B, S // tq, tq, H, 2), axis=2).transpose(0, 2, 3, 1)
    kn = jnp.max(norms(COL_DAK), axis=1).reshape(B, H, 2)
    bound = jnp.max(qn * kn[..., None], axis=2) * 1.01 + 0.01
    q0 = (jnp.arange(S // tq, dtype=F32) * tq)[None, None, :]
    slope = jnp.asarray(slopes_np, F32)[None, :, None]
    first_key = q0 - (EXP_UNDERFLOW + 2.0 * bound) / slope
    kt = jnp.floor(first_key / tk).astype(I32)
    n_tiles = (jnp.arange(S // tq, dtype=I32) * tq + tq - 1) // tk + 1
    return jnp.clip(kt, 0, n_tiles[None, None, :] - 1).reshape(-1)


def _diff_attention(slopes, kmin, slab3, vT, srows, lq1, lk1, lq2, lk2, subw, lam_init, tq, tk):
    B, S, _ = slab3.shape
    nk = S // tk
    assert tk % tq == 0 and S % tk == 0
    qcol = COL_DAQ // 128
    tile = tq
    kcol = COL_DAK // 128
    v_rows = vT.shape[3]
    vec = lambda a: a.reshape(1, DA_DIM).astype(F32)
    grid_spec = pltpu.PrefetchScalarGridSpec(
        num_scalar_prefetch=2,
        grid=(B, DA_HEADS, S // tile),
        in_specs=[pl.BlockSpec((None, tile, 128), lambda b, h, i, s, km: (b, i, qcol + h)),
                  pl.BlockSpec((None, S, 128), lambda b, h, i, s, km: (b, 0, kcol + h)),
                  pl.BlockSpec((None, None, nk, v_rows, tk), lambda b, h, i, s, km: (b, h, 0, 0, 0)),
                  pl.BlockSpec((tk, 128), lambda b, h, i, s, km: (0, 0)),
                  pl.BlockSpec((tk, 2 * tq), lambda b, h, i, s, km: (0, 0)),
                  pl.BlockSpec((None, POS_ROWS, 128), lambda b, h, i, s, km: (h, 0, 0)),
                  pl.BlockSpec((1, DA_DIM), lambda b, h, i, s, km: (0, 0)),
                  pl.BlockSpec((1, DA_DIM), lambda b, h, i, s, km: (0, 0)),
                  pl.BlockSpec((1, DA_DIM), lambda b, h, i, s, km: (0, 0)),
                  pl.BlockSpec((1, DA_DIM), lambda b, h, i, s, km: (0, 0)),
                  pl.BlockSpec((DA_VDIM, 1), lambda b, h, i, s, km: (0, 0))],
        out_specs=pl.BlockSpec((None, tile, 128), lambda b, h, i, s, km: (b, i, h)),
        scratch_shapes=[pltpu.VMEM((v_rows, 2 * tq), F32), pltpu.VMEM((4 * DA_DIM, 2 * tq), BF16),
                        pltpu.VMEM((tk, 2 * tq), F32), pltpu.VMEM((tk, 2 * tq), F32),
                        pltpu.VMEM((tk, 2 * tq), BF16), pltpu.VMEM((tk, 2 * tq), BF16)],
    )
    rel = jnp.asarray(np.arange(tk)[:, None] - (np.arange(2 * tq) % tq)[None, :], I32)
    return pl.pallas_call(
        functools.partial(_da_kernel, tq=tq, tk=tk, lam_init=lam_init),
        grid_spec=grid_spec,
        out_shape=jax.ShapeDtypeStruct((B, S, DA_HEADS * DA_VDIM), BF16),
        compiler_params=_cparams(("parallel", "parallel", "arbitrary")),
        name="diffattn",
    )(slopes, kmin, slab3, slab3, vT, _pos_features(tk), rel, srows, vec(lq1), vec(lk1), vec(lq2), vec(lk2),
      subw.reshape(DA_VDIM, 1).astype(F32))


def _nsa_kernel(slopes_ref, qT_ref, kc_ref, vcT_ref, ksa_ref, vsT_ref, kwa_ref, vwT_ref, wb_ref, gT_ref, mT_ref,
                grp_ref, ltri_ref, oT_ref, acc_ref, res_ref, base_ref, qa_ref, sa_ref, sb_ref, pa_ref, pb_ref,
                list_ref, slist_ref, lsem, *, t, tk, n_top):
    g = pl.program_id(1)
    qi = pl.program_id(2)
    q0 = qi * t
    w = NSA_HPG * t
    n_cp = kc_ref.shape[0]
    n_sel = mT_ref.shape[0]
    lane1 = lax.broadcasted_iota(I32, (1, w), 1)
    slope_row = jnp.zeros((1, w), F32)
    for hh in range(NSA_HPG):
        slope_row = jnp.where((lane1 >= hh * t) & (lane1 < (hh + 1) * t), slopes_ref[g * NSA_HPG + hh], slope_row)
    q_all = qT_ref[...]
    gates = jax.nn.sigmoid(gT_ref[...])
    key_off = lax.broadcasted_iota(I32, (tk, w), 0)
    t_loc = jnp.bitwise_and(lax.broadcasted_iota(I32, (tk, w), 1), t - 1)
    base_ref[...] = slope_row * key_off.astype(F32)

    c_idx = lax.broadcasted_iota(I32, (n_cp, w), 0)
    t_c = q0 + jnp.bitwise_and(lax.broadcasted_iota(I32, (n_cp, w), 1), t - 1)
    d_c = t_c - (c_idx * CMP_STRIDE + (CMP_BLOCK - 1))
    ok_c = d_c >= 0
    s = jnp.dot(kc_ref[...], q_all, preferred_element_type=F32) - slope_row * d_c.astype(F32)
    s = jnp.where(ok_c, s, NEG)
    e = jnp.exp(s - jnp.max(s, axis=0, keepdims=True))
    inv = 1.0 / jnp.sum(e, axis=0, keepdims=True)
    p = jnp.where(ok_c, e * inv, 0.0)
    res_ref[...] = gates[0:1, :] * jnp.dot(vcT_ref[...], p.astype(BF16), preferred_element_type=F32)
    p_sum = p[:, 0:t]
    for hh in range(1, NSA_HPG):
        p_sum = p_sum + p[:, hh * t:(hh + 1) * t]

    p_hi = p_sum.astype(BF16)
    p_lo = (p_sum - p_hi.astype(F32)).astype(BF16)
    mT = mT_ref[...]
    imp = jnp.dot(mT, p_hi, preferred_element_type=F32) + jnp.dot(mT, p_lo, preferred_element_type=F32)
    blk = lax.broadcasted_iota(I32, (n_sel, t), 0)
    t_s = q0 + lax.broadcasted_iota(I32, (n_sel, t), 1)
    cur = jnp.right_shift(t_s, int(math.log2(SEL_BLOCK)))
    forced = (blk == 0) | (blk == cur) | (blk == cur - 1)
    score = jnp.where(blk * SEL_BLOCK <= t_s, imp + jnp.where(forced, FORCE_BONUS, 0.0), NEG)
    blk_f = blk.astype(F32)
    sel = jnp.zeros((n_sel, t), F32)
    for _ in range(n_top):
        _, idx = _first_max(score, blk_f)
        hit = blk_f == idx
        sel = jnp.where(hit, 1.0, sel)
        score = jnp.where(hit, -jnp.inf, score)
    n_full = q0 // tk
    nt_pad = grp_ref.shape[0]
    cnt = jnp.broadcast_to(jnp.sum(sel, axis=1, keepdims=True), (n_sel, 128)).astype(BF16)
    tile_cnt = jnp.dot(grp_ref[...], cnt, preferred_element_type=F32)
    kt_i = lax.broadcasted_iota(I32, (nt_pad, 128), 0)
    active = (tile_cnt > 0.5) & (kt_i < n_full)
    act_f = jnp.where(active, 1.0, 0.0)
    before = jnp.dot(ltri_ref[...], act_f.astype(BF16), preferred_element_type=F32)
    lane_j = lax.broadcasted_iota(I32, (nt_pad, 128), 1)
    slot_hit = active & (before == lane_j.astype(F32))
    tiles_row = jnp.sum(jnp.where(slot_hit, kt_i.astype(F32), 0.0), axis=0, keepdims=True)
    n_act_row = jnp.sum(act_f, axis=0, keepdims=True)
    lane8 = lax.broadcasted_iota(I32, (8, 128), 1)
    list_ref[...] = jnp.where(lane8 == 127, n_act_row, tiles_row).astype(I32)
    list_copy = pltpu.make_async_copy(list_ref, slist_ref, lsem)
    list_copy.start()

    selb = jnp.where(sel > 0.5, 0.0, NEG)
    row64 = lax.broadcasted_iota(I32, (NSA_DIM, w), 0)
    qa_ref[0:NSA_DIM, :] = q_all
    qa_ref[NSA_DIM:2 * NSA_DIM, :] = jnp.where(row64 == 0, NEG, 0.0).astype(BF16)
    qa_ref[2 * NSA_DIM:2 * NSA_DIM + n_sel, :] = jnp.concatenate([selb] * NSA_HPG, axis=1).astype(BF16)
    if n_sel < SEL_COLS:
        qa_ref[2 * NSA_DIM + n_sel:, :] = jnp.zeros((SEL_COLS - n_sel, w), BF16)

    n_wt = (WINDOW + t) // t
    kw_t = kwa_ref[pl.ds(pl.multiple_of(q0, t), WINDOW + t), :]
    s = jnp.dot(kw_t, qa_ref[0:2 * NSA_DIM, :], preferred_element_type=F32) + wb_ref[...]
    p = jnp.exp(s - jnp.max(s, axis=0, keepdims=True))
    inv = 1.0 / jnp.sum(p, axis=0, keepdims=True)
    p = p.astype(BF16)
    o_w = jnp.dot(vwT_ref[qi], p[0:t, :], preferred_element_type=F32)
    for c in range(1, n_wt):
        o_w = o_w + jnp.dot(vwT_ref[qi + c], p[c * t:(c + 1) * t, :], preferred_element_type=F32)
    res_ref[...] = res_ref[...] + gates[2:3, :] * (o_w * inv)

    def scores(kt, s_ref):
        k_t = ksa_ref[pl.ds(pl.multiple_of(kt * tk, tk), tk), :]
        s = jnp.dot(k_t, qa_ref[...], preferred_element_type=F32) + base_ref[...]
        s_ref[...] = s
        return jnp.max(s, axis=0, keepdims=True)

    def probs(s_ref, p_ref, mx, off, m_old, l_old):
        m_new = jnp.maximum(m_old, mx + off)
        p = jnp.exp(s_ref[...] - (m_new - off))
        alpha = jnp.exp(m_old - m_new)
        p_ref[...] = p.astype(BF16)
        return m_new, alpha * l_old + jnp.sum(p, axis=0, keepdims=True), alpha

    def accumulate(kt, p_ref, alpha):
        acc_ref[...] = alpha * acc_ref[...] + jnp.dot(vsT_ref[kt], p_ref[...], preferred_element_type=F32)

    list_copy.wait()
    n_act = slist_ref[0, 127]
    mx_a = scores(slist_ref[0, 0], sa_ref)

    kd = n_full * tk
    s = jnp.dot(ksa_ref[pl.ds(pl.multiple_of(kd, tk), tk), :], qa_ref[...], preferred_element_type=F32) + base_ref[...]
    s = jnp.where(key_off - t_loc <= q0 - kd, s, NEG)
    off_d = slope_row * kd.astype(F32)
    m = jnp.max(s, axis=0, keepdims=True) + off_d
    p = jnp.exp(s - (m - off_d))
    l = jnp.sum(p, axis=0, keepdims=True)
    acc_ref[...] = jnp.dot(vsT_ref[n_full], p.astype(BF16), preferred_element_type=F32)
    pb_ref[...] = jnp.zeros_like(pb_ref)

    def pair(j, carry):
        m, l, mx_a, alpha_b, tb_prev = carry
        ta = slist_ref[0, 2 * j]
        tb = slist_ref[0, 2 * j + 1]
        off_a = slope_row * (ta * tk).astype(F32)
        off_b = jnp.where(2 * j + 1 < n_act, slope_row * (tb * tk).astype(F32), NEG)
        accumulate(tb_prev, pb_ref, alpha_b)
        m, l, alpha_a = probs(sa_ref, pa_ref, mx_a, off_a, m, l)
        mx_b = scores(tb, sb_ref)
        accumulate(ta, pa_ref, alpha_a)
        m, l, alpha_b = probs(sb_ref, pb_ref, mx_b, off_b, m, l)
        mx_a = scores(slist_ref[0, 2 * j + 2], sa_ref)
        return m, l, mx_a, alpha_b, tb

    n_pairs = (n_act + 1) // 2
    m, l, _, alpha_b, tb_last = lax.fori_loop(0, n_pairs, pair, (m, l, mx_a, jnp.ones((1, w), F32), jnp.int32(0)))
    accumulate(tb_last, pb_ref, alpha_b)
    oT_ref[...] = (res_ref[...] + gates[1:2, :] * (acc_ref[...] * (1.0 / l))).astype(BF16)


def _nsa_attention(slopes, qT, kc, vcT, ksa, vsT, kwa, vwT, wb, gT, mT, t, tk):
    B, G, nq, _, w = qT.shape
    S = nq * t
    n_cp = kc.shape[2]
    n_sel = S // SEL_BLOCK
    n_top = min(SEL_TOPK, n_sel)
    assert tk % t == 0 and S % tk == 0 and t & (t - 1) == 0 and WINDOW % t == 0 and n_sel <= SEL_COLS
    n_tiles = S // tk
    nt_pad = -(-n_tiles // 16) * 16
    assert nt_pad + 2 < 127
    grp = jnp.asarray((np.arange(n_sel)[None, :] // (tk // SEL_BLOCK) == np.arange(nt_pad)[:, None])
                      .astype(np.float32), BF16)
    ltri = jnp.asarray((np.arange(nt_pad)[None, :] < np.arange(nt_pad)[:, None]).astype(np.float32), BF16)
    fixed = lambda b, g, i, s: (b, g, 0, 0)
    fixed5 = lambda b, g, i, s: (b, g, 0, 0, 0)
    tile5 = lambda b, g, i, s: (b, g, i, 0, 0)
    grid_spec = pltpu.PrefetchScalarGridSpec(
        num_scalar_prefetch=1,
        grid=(B, G, nq),
        in_specs=[pl.BlockSpec((None, None, None, NSA_DIM, w), tile5),
                  pl.BlockSpec((None, None, n_cp, NSA_DIM), fixed),
                  pl.BlockSpec((None, None, NSA_DIM, n_cp), fixed),
                  pl.BlockSpec((None, None, S, ksa.shape[-1]), fixed),
                  pl.BlockSpec((None, None, S // tk, NSA_DIM, tk), fixed5),
                  pl.BlockSpec((None, None, S + WINDOW, 2 * NSA_DIM), fixed),
                  pl.BlockSpec((None, None, (S + WINDOW) // t, NSA_DIM, t), fixed5),
                  pl.BlockSpec((None, WINDOW + t, w), lambda b, g, i, s: (g, 0, 0)),
                  pl.BlockSpec((None, None, None, 3, w), tile5),
                  pl.BlockSpec((n_sel, n_cp), lambda b, g, i, s: (0, 0)),
                  pl.BlockSpec((nt_pad, n_sel), lambda b, g, i, s: (0, 0)),
                  pl.BlockSpec((nt_pad, nt_pad), lambda b, g, i, s: (0, 0))],
        out_specs=pl.BlockSpec((None, None, None, NSA_DIM, w), tile5),
        scratch_shapes=[pltpu.VMEM((NSA_DIM, w), F32),
                        pltpu.VMEM((NSA_DIM, w), F32),
                        pltpu.VMEM((tk, w), F32),
                        pltpu.VMEM((2 * NSA_DIM + SEL_COLS, w), BF16),
                        pltpu.VMEM((tk, w), F32), pltpu.VMEM((tk, w), F32),
                        pltpu.VMEM((tk, w), BF16), pltpu.VMEM((tk, w), BF16),
                        pltpu.VMEM((8, 128), I32), pltpu.SMEM((8, 128), I32), pltpu.SemaphoreType.DMA(())],
    )
    return pl.pallas_call(
        functools.partial(_nsa_kernel, t=t, tk=tk, n_top=n_top),
        grid_spec=grid_spec,
        out_shape=jax.ShapeDtypeStruct((B, G, nq, NSA_DIM, w), BF16),
        compiler_params=_cparams(("parallel", "parallel", "arbitrary")),
        name="nsa",
    )(slopes, qT, kc, vcT, ksa, vsT, kwa, vwT, wb, gT, mT, grp, ltri)


def _pack_halves(v):
    c = v.shape[1] // 2
    lo = lax.bitcast_convert_type(v[:, :c].astype(BF16).astype(F32), U32)
    hi = lax.bitcast_convert_type(v[:, c:].astype(BF16).astype(F32), U32)
    return hi | (lo >> 16)


def _unpack_halves(u):
    lo = lax.bitcast_convert_type(u << 16, F32)
    hi = lax.bitcast_convert_type(u & jnp.uint32(0xFFFF0000), F32)
    return lo, hi


def _store_slabs(ref, v):
    packed = _pack_halves(v)
    for s in range(SC_SPLIT):
        ref[s] = packed[:, s * SC_PIECE:(s + 1) * SC_PIECE]


def _load_slabs_bf16(ref):
    parts = [_unpack_halves(ref[s]) for s in range(SC_SPLIT)]
    return jnp.concatenate([p[0].astype(BF16) for p in parts] + [p[1].astype(BF16) for p in parts], axis=1)


def _merge_kernel(oa_ref, ob_ref, ga_ref, gb_ref, x_ref, g1_ref, nw_ref, sh_ref, sc_ref,
                  wa_ref, wb_ref, wo_ref, wr_ref, x1_ref, h2_ref, lg_ref):
    ya = jnp.dot(oa_ref[...], wa_ref[...], preferred_element_type=F32)
    yb = jnp.dot(ob_ref[...], wb_ref[...], preferred_element_type=F32)
    merged = (jax.nn.sigmoid(ga_ref[...].astype(F32)) * ya
              + jax.nn.sigmoid(gb_ref[...].astype(F32)) * yb).astype(BF16)
    mix = jnp.dot(merged, wo_ref[...], preferred_element_type=F32)
    x1 = x_ref[...] + g1_ref[...] * mix
    x1_ref[...] = x1
    h2 = _modulate(x1, nw_ref[...], sh_ref[...], sc_ref[...])
    _store_slabs(h2_ref, h2)
    h_hi = h2.astype(BF16)
    h_lo = (h2 - h_hi.astype(F32)).astype(BF16)
    lg_ref[...] = (jnp.dot(h_hi, wr_ref[0], preferred_element_type=F32)
                   + jnp.dot(h_lo, wr_ref[0], preferred_element_type=F32)
                   + jnp.dot(h_hi, wr_ref[1], preferred_element_type=F32))


def _merge(oa, ob, slab, x2, g1, nw, sh, sc, wa, wb, wo, wr, S):
    N, D = x2.shape
    tm = 256
    per_b = S // tm
    mcol = COL_MERGE // D
    row = lambda i: (i, 0)
    full = lambda i: (0, 0)
    perb = lambda i: (i // per_b, 0, 0)
    n_lg = wr.shape[-1]
    return pl.pallas_call(
        _merge_kernel,
        grid=(N // tm,),
        in_specs=[pl.BlockSpec((tm, oa.shape[1]), row),
                  pl.BlockSpec((tm, ob.shape[1]), row),
                  pl.BlockSpec((tm, D), lambda i: (i, mcol)),
                  pl.BlockSpec((tm, D), lambda i: (i, mcol + 1)),
                  pl.BlockSpec((tm, D), row),
                  pl.BlockSpec((None, 1, D), perb),
                  pl.BlockSpec((1, D), full),
                  pl.BlockSpec((None, 1, D), perb),
                  pl.BlockSpec((None, 1, D), perb),
                  pl.BlockSpec(wa.shape, full),
                  pl.BlockSpec(wb.shape, full),
                  pl.BlockSpec(wo.shape, full),
                  pl.BlockSpec(wr.shape, lambda i: (0, 0, 0))],
        out_specs=[pl.BlockSpec((tm, D), row), pl.BlockSpec((SC_SPLIT, tm, SC_PIECE), lambda i: (0, i, 0)),
                   pl.BlockSpec((tm, n_lg), row)],
        out_shape=[jax.ShapeDtypeStruct((N, D), F32), jax.ShapeDtypeStruct((SC_SPLIT, N, SC_PIECE), U32),
                   jax.ShapeDtypeStruct((N, n_lg), F32)],
        compiler_params=_cparams(("parallel",)),
        name="merge",
    )(oa, ob, slab, slab, x2, g1.reshape(-1, 1, D), nw.reshape(1, D), sh.reshape(-1, 1, D),
      sc.reshape(-1, 1, D), wa, wb, wo, wr)


def _first_max(v, idx):
    mx = jnp.max(v, axis=0, keepdims=True)
    first = jnp.min(jnp.where(v == mx, idx, float(v.shape[0])), axis=0, keepdims=True)
    return mx, first


def _route_kernel(lg_ref, b_ref, e_ref, w_ref):
    lg = lg_ref[...]
    tn = lg.shape[1]
    per_g = N_EXPERTS // N_EXPERT_GROUPS
    scores = jax.nn.sigmoid(lg)
    biased = scores + b_ref[...]
    gi = lax.broadcasted_iota(I32, (N_EXPERT_GROUPS, tn), 0).astype(F32)
    gscore = jnp.zeros((N_EXPERT_GROUPS, tn), F32)
    for g in range(N_EXPERT_GROUPS):
        grp = biased[g * per_g:(g + 1) * per_g, :]
        m1, f1 = _first_max(grp, gi)
        m2 = jnp.max(jnp.where(gi == f1, -jnp.inf, grp), axis=0, keepdims=True)
        gscore = jnp.where(gi == g, m1 + m2, gscore)
    gsel = jnp.zeros((N_EXPERT_GROUPS, tn), F32)
    for _ in range(TOPK_EXPERT_GROUPS):
        _, f = _first_max(gscore, gi)
        hit = gi == f
        gsel = jnp.where(hit, 1.0, gsel)
        gscore = jnp.where(hit, -jnp.inf, gscore)
    ei = lax.broadcasted_iota(I32, (N_EXPERTS, tn), 0).astype(F32)
    emask = jnp.zeros((N_EXPERTS, tn), F32)
    for g in range(N_EXPERT_GROUPS):
        in_g = (ei >= g * per_g) & (ei < (g + 1) * per_g)
        emask = jnp.where(in_g, gsel[g:g + 1, :], emask)
    masked = jnp.where(emask > 0.5, biased, NEG)
    e_out = jnp.zeros((MOE_TOPK, tn), F32)
    w_out = jnp.zeros((MOE_TOPK, tn), F32)
    for r in range(MOE_TOPK):
        _, f = _first_max(masked, ei)
        hit = ei == f
        wv = jnp.sum(jnp.where(hit, scores, 0.0), axis=0, keepdims=True)
        e_out = jnp.where(gi == r, f, e_out)
        w_out = jnp.where(gi == r, wv, w_out)
        masked = jnp.where(hit, -jnp.inf, masked)
    e_ref[...] = e_out.astype(I32)
    w_ref[...] = w_out / jnp.sum(w_out, axis=0, keepdims=True) * ROUTED_SCALE


def _route(lgT, rb):
    E, N = lgT.shape
    tn = 512
    return pl.pallas_call(
        _route_kernel,
        grid=(N // tn,),
        in_specs=[pl.BlockSpec((E, tn), lambda i: (0, i)), pl.BlockSpec((E, 1), lambda i: (0, 0))],
        out_specs=[pl.BlockSpec((MOE_TOPK, tn), lambda i: (0, i)), pl.BlockSpec((MOE_TOPK, tn), lambda i: (0, i))],
        out_shape=[jax.ShapeDtypeStruct((MOE_TOPK, N), I32), jax.ShapeDtypeStruct((MOE_TOPK, N), F32)],
        compiler_params=_cparams(("parallel",)),
        name="route",
    )(lgT, rb.reshape(E, 1).astype(F32))


def _slots_kernel(e_ref, tri_ref, lt_ref, dest_ref, cnt_ref, carry_ref, *, rb):
    phase = pl.program_id(0)
    i = pl.program_id(1)
    e = e_ref[...]
    tn = e.shape[1]
    ei = lax.broadcasted_iota(I32, (N_EXPERTS, tn), 0)

    @pl.when((phase == 0) & (i == 0))
    def _():
        carry_ref[...] = jnp.zeros_like(carry_ref)

    @pl.when(phase == 0)
    def _():
        tot = jnp.zeros((N_EXPERTS, 1), F32)
        for k in range(MOE_TOPK):
            oh = jnp.where(ei == e[k:k + 1, :], 1.0, 0.0)
            tot = tot + jnp.sum(oh, axis=1, keepdims=True)
        carry_ref[...] = carry_ref[...] + tot
        dest_ref[...] = jnp.zeros_like(dest_ref)

    @pl.when((phase == 1) & (i == 0))
    def _():
        cnt = carry_ref[...]
        cnt_ref[...] = cnt
        padded = jnp.broadcast_to(jnp.ceil(cnt / rb) * rb, (N_EXPERTS, 128))
        first = jnp.dot(lt_ref[...], padded, preferred_element_type=F32, precision=HIGHEST)
        carry_ref[...] = first[:, :1]

    @pl.when(phase == 1)
    def _():
        carry = carry_ref[...]
        ki = lax.broadcasted_iota(I32, (MOE_TOPK, tn), 0)
        dest = jnp.zeros((MOE_TOPK, tn), F32)
        for k in range(MOE_TOPK):
            hit = ei == e[k:k + 1, :]
            oh = jnp.where(hit, 1.0, 0.0)
            before = jnp.dot(oh.astype(BF16), tri_ref[...], preferred_element_type=F32)
            row = jnp.sum(jnp.where(hit, before + carry, 0.0), axis=0, keepdims=True)
            dest = jnp.where(ki == k, row, dest)
            carry = carry + jnp.sum(oh, axis=1, keepdims=True)
        carry_ref[...] = carry
        dest_ref[...] = dest.astype(I32)


def _slots(top_eT, rb):
    K, N = top_eT.shape
    tn = 512
    idx = np.arange(tn)
    tri = jnp.asarray((idx[:, None] < idx[None, :]).astype(np.float32), BF16)
    e_idx = np.arange(N_EXPERTS)
    lt = jnp.asarray((e_idx[None, :] < e_idx[:, None]).astype(np.float32), F32)
    return pl.pallas_call(
        functools.partial(_slots_kernel, rb=rb),
        grid=(2, N // tn),
        in_specs=[pl.BlockSpec((K, tn), lambda p, i: (0, i)),
                  pl.BlockSpec((tn, tn), lambda p, i: (0, 0)),
                  pl.BlockSpec((N_EXPERTS, N_EXPERTS), lambda p, i: (0, 0))],
        out_specs=[pl.BlockSpec((K, tn), lambda p, i: (0, i * p)),
                   pl.BlockSpec((N_EXPERTS, 1), lambda p, i: (0, 0))],
        out_shape=[jax.ShapeDtypeStruct((K, N), I32), jax.ShapeDtypeStruct((N_EXPERTS, 1), F32)],
        scratch_shapes=[pltpu.VMEM((N_EXPERTS, 1), F32)],
        compiler_params=_cparams(("arbitrary", "arbitrary")),
        name="slots",
    )(top_eT, tri, lt)


def _sc_scatter_rows(src, idx, n_out, repeat):
    width = src.shape[1]
    n_tok = src.shape[0] // SC_SPLIT
    w = n_tok // SC_WINDOW
    mesh = plsc.VectorSubcoreMesh(core_axis_name="c", subcore_axis_name="s")

    @pl.kernel(out_type=jax.ShapeDtypeStruct((n_out, width), src.dtype), mesh=mesh)
    def scatter_kernel(x_hbm, i_hbm, o_hbm):
        def body(x_vmem, i_vmem):
            for k in range(repeat):
                pltpu.sync_copy(x_vmem, o_hbm.at[i_vmem.at[k]])

        pltpu.emit_pipeline(
            body,
            grid=(SC_SPLIT * w,),
            in_specs=[pl.BlockSpec((SC_WINDOW, width), lambda g: (g, 0)),
                      pl.BlockSpec((repeat, SC_WINDOW), lambda g: (g // w, g % w))],
            out_specs=[],
            core_axis_name=("c", "s"),
            dimension_semantics=(pltpu.PARALLEL,),
        )(x_hbm, i_hbm)

    return scatter_kernel(src, idx.reshape(SC_SPLIT * repeat, n_tok))


def _zero_pads_kernel(start_ref, len_ref, xs_in_ref, xs_ref, zeros_ref, sem, *, rb, n_rows):
    del xs_in_ref
    zeros_ref[...] = jnp.zeros_like(zeros_ref)
    n_seg = start_ref.shape[0]
    bits = rb.bit_length() - 1
    assert rb == 1 << bits

    def for_each_copy(act):
        def rows(row0, size):
            for s in range(SC_SPLIT):
                act(pltpu.make_async_copy(zeros_ref.at[pl.ds(0, size), :],
                                          xs_ref.at[pl.ds(s * n_rows + row0, size), :], sem))

        def segment(e, c):
            start = start_ref[e]
            length = len_ref[e]
            end = start + length
            n_whole = length // rb

            def whole(j, c2):
                rows(pl.multiple_of(end - (j + 1) * rb, SUBLANES), rb)
                return c2

            lax.fori_loop(0, n_whole, whole, 0)
            off = end - n_whole * rb
            rest = length - n_whole * rb
            for b in reversed(range(SUBLANES.bit_length() - 1, bits)):
                size = 1 << b
                take = jnp.bitwise_and(jnp.right_shift(rest, b), 1)
                off = off - take * size

                @pl.when(take == 1)
                def _(off=off, size=size):
                    rows(pl.multiple_of(off, SUBLANES), size)

            lead = jnp.bitwise_and(rest, SUBLANES - 1)
            for i in range(SUBLANES - 1):
                @pl.when(i < lead)
                def _(i=i):
                    rows(start + i, 1)

            return c

        lax.fori_loop(0, n_seg, segment, 0)

    for_each_copy(lambda cp: cp.start())
    for_each_copy(lambda cp: cp.wait())


def _zero_pads(xs2d, start, length, rb, n_rows):
    grid_spec = pltpu.PrefetchScalarGridSpec(
        num_scalar_prefetch=2,
        grid=(1,),
        in_specs=[pl.BlockSpec(memory_space=pl.ANY)],
        out_specs=pl.BlockSpec(memory_space=pl.ANY),
        scratch_shapes=[pltpu.VMEM((rb, xs2d.shape[1]), xs2d.dtype), pltpu.SemaphoreType.DMA(())],
    )
    return pl.pallas_call(
        functools.partial(_zero_pads_kernel, rb=rb, n_rows=n_rows),
        grid_spec=grid_spec,
        out_shape=jax.ShapeDtypeStruct(xs2d.shape, xs2d.dtype),
        input_output_aliases={2: 0},
        compiler_params=_cparams(("arbitrary",)),
        name="zeropads",
    )(start, length, xs2d)


def _experts_kernel(be_ref, x_ref, wg_ref, wu_ref, wd_ref, y_ref, wgu_sc, wd_sc):
    i = pl.program_id(0)
    hdim = wg_ref.shape[1]

    @pl.when((i == 0) | (be_ref[i] != be_ref[jnp.maximum(i - 1, 0)]))
    def _():
        wgu_sc[:, 0:hdim] = wg_ref[...].astype(BF16)
        wgu_sc[:, hdim:] = wu_ref[...].astype(BF16)
        wd_sc[...] = wd_ref[...].astype(BF16)

    gu = jnp.dot(_load_slabs_bf16(x_ref), wgu_sc[...], preferred_element_type=F32)
    g = gu[:, :hdim]
    act = (g * jax.nn.sigmoid(g) * gu[:, hdim:]).astype(BF16)
    _store_slabs(y_ref, jnp.dot(act, wd_sc[...], preferred_element_type=F32))


def _experts(blk_e, xs, wg, wu, wd, rb):
    n_rows = xs.shape[1]
    d_model, hdim = wg.shape[1:]
    per_expert = lambda i, be: (be[i], 0, 0)
    grid_spec = pltpu.PrefetchScalarGridSpec(
        num_scalar_prefetch=1,
        grid=(n_rows // rb,),
        in_specs=[pl.BlockSpec((SC_SPLIT, rb, SC_PIECE), lambda i, be: (0, i, 0)),
                  pl.BlockSpec((None, d_model, hdim), per_expert),
                  pl.BlockSpec((None, d_model, hdim), per_expert),
                  pl.BlockSpec((None, hdim, d_model), per_expert)],
        out_specs=pl.BlockSpec((SC_SPLIT, rb, SC_PIECE), lambda i, be: (0, i, 0)),
        scratch_shapes=[pltpu.VMEM((d_model, 2 * hdim), BF16), pltpu.VMEM((hdim, d_model), BF16)],
    )
    return pl.pallas_call(
        _experts_kernel,
        grid_spec=grid_spec,
        out_shape=jax.ShapeDtypeStruct((SC_SPLIT, n_rows, SC_PIECE), U32),
        compiler_params=_cparams(("arbitrary",)),
        name="experts",
    )(blk_e, xs, wg, wu, wd)


def _sc_gather_rows(table, idx):
    n = idx.shape[0]
    width = table.shape[1]
    mesh = plsc.VectorSubcoreMesh(core_axis_name="c", subcore_axis_name="s")

    @pl.kernel(out_type=jax.ShapeDtypeStruct((n, width), table.dtype), mesh=mesh)
    def gather_kernel(t_hbm, i_hbm, o_hbm):
        def body(i_vmem, o_vmem):
            pltpu.sync_copy(t_hbm.at[i_vmem.at[0]], o_vmem)

        pltpu.emit_pipeline(
            body,
            grid=(n // SC_WINDOW,),
            in_specs=[pl.BlockSpec((1, SC_WINDOW), lambda i: (0, i))],
            out_specs=[pl.BlockSpec((SC_WINDOW, width), lambda i: (i, 0))],
            core_axis_name=("c", "s"),
            dimension_semantics=(pltpu.PARALLEL,),
        )(i_hbm, o_hbm)

    return gather_kernel(table, idx.reshape(1, n))


def _combine_kernel(yg_ref, w_ref, h_ref, x1_ref, g2_ref, fw_ref, sgu_ref, sd_ref, o_ref):
    gu = jnp.dot(_load_slabs_bf16(h_ref), sgu_ref[...], preferred_element_type=F32)
    hdim = gu.shape[1] // 2
    g = gu[:, :hdim]
    ffn = jnp.dot((g * jax.nn.sigmoid(g) * gu[:, hdim:]).astype(BF16), sd_ref[...], preferred_element_type=F32)

    w = w_ref[...]
    tt = w.shape[0]
    lo = [jnp.zeros((tt, SC_PIECE), F32) for _ in range(SC_SPLIT)]
    hi = [jnp.zeros((tt, SC_PIECE), F32) for _ in range(SC_SPLIT)]
    for k in range(MOE_TOPK):
        wk = w[:, k:k + 1]
        for s in range(SC_SPLIT):
            y_lo, y_hi = _unpack_halves(yg_ref[s, k])
            lo[s] = lo[s] + wk * y_lo
            hi[s] = hi[s] + wk * y_hi
    ffn = ffn + jnp.concatenate(lo + hi, axis=1)
    x2 = x1_ref[...] + g2_ref[...] * ffn
    r = lax.rsqrt(jnp.mean(x2 * x2, axis=-1, keepdims=True) + NORM_EPS)
    o_ref[...] = x2 * r * fw_ref[...]


def _combine(yg, top_w, h2p, x1, g2, fw, sgu, sd, S, tt):
    N, D = x1.shape
    per_b = S // tt
    row = lambda i: (i, 0)
    full = lambda i: (0, 0)
    return pl.pallas_call(
        _combine_kernel,
        grid=(N // tt,),
        in_specs=[pl.BlockSpec((SC_SPLIT, MOE_TOPK, tt, SC_PIECE), lambda i: (0, 0, i, 0)),
                  pl.BlockSpec((tt, MOE_TOPK), row),
                  pl.BlockSpec((SC_SPLIT, tt, SC_PIECE), lambda i: (0, i, 0)),
                  pl.BlockSpec((tt, D), row),
                  pl.BlockSpec((None, 1, D), lambda i: (i // per_b, 0, 0)),
                  pl.BlockSpec((1, D), full),
                  pl.BlockSpec(sgu.shape, full),
                  pl.BlockSpec(sd.shape, full)],
        out_specs=pl.BlockSpec((tt, D), row),
        out_shape=jax.ShapeDtypeStruct((N, D), F32),
        compiler_params=_cparams(("parallel",)),
        name="combine",
    )(yg, top_w, h2p, x1, g2.reshape(-1, 1, D), fw.reshape(1, D), sgu, sd)


def _cmp_to_sel_T(n_cp, n_sel):
    c0 = np.arange(n_cp)[None, :] * CMP_STRIDE
    s0 = np.arange(n_sel)[:, None] * SEL_BLOCK
    ov = np.minimum(c0 + CMP_BLOCK, s0 + SEL_BLOCK) - np.maximum(c0, s0)
    m = np.clip(ov, 0, None).astype(np.float32) / CMP_BLOCK
    m[:, n_cp - 1] = 0.0
    return jnp.asarray(m, BF16)


def _slab_weights(w_in):
    D = w_in.shape[0]
    sizes = [1024, 1024, 1024, 512, 128, 128, 128, 128, 128, 128, 24, 2048]
    offs = np.concatenate([[0], np.cumsum(sizes)])
    part = lambda i: w_in[:, offs[i]:offs[i + 1]]
    pieces = [part(0), part(1), part(2), part(11), part(3)] + [part(i) for i in range(4, 10)] + [part(10)]
    w = jnp.concatenate(pieces, axis=1)
    return jnp.pad(w, ((0, 0), (0, SLAB_COLS - w.shape[1]))).astype(BF16)


def _layer(x, c, ada_w, ada_b, norm1_w, w_in, da_lq1, da_lk1, da_lq2, da_lk2, da_subln_w,
           cmp_k_pe, cmp_k_w1, cmp_k_w2, cmp_v_pe, cmp_v_w1, cmp_v_w2, w_da_out, w_nsa_out, w_o,
           norm2_w, router_w, router_b, exp_w_gate, exp_w_up, exp_w_down,
           sh_w_gate, sh_w_up, sh_w_down, final_norm_w, lam_init):
    B, S, D = x.shape
    N = B * S
    G, HPG = NSA_GROUPS, NSA_HPG
    x2 = x.reshape(N, D)

    mod = _ada(c, ada_w, ada_b)
    sh1, sc1, g1, sh2, sc2, g2 = jnp.split(mod, 6, axis=-1)

    slab = _inproj(x2, norm1_w, sh1, sc1, _slab_weights(w_in), S)
    slab3 = slab.reshape(B, S, SLAB_COLS)

    i_all = np.arange(1, DA_HEADS + NSA_HEADS + 1, dtype=np.float32)
    slopes = (2.0 ** (-8.0 * i_all / (DA_HEADS + NSA_HEADS))).astype(np.float32)
    slopes_a = jnp.asarray(slopes[0::2])
    slopes_b = jnp.asarray(slopes[1::2])

    ta_q, ta_k = min(512, S), min(512, S)
    vT = slab3[:, :, COL_DAV:COL_DAV + 1024].reshape(B, S // ta_k, ta_k, DA_HEADS, DA_VDIM).transpose(0, 3, 1, 4, 2)
    kmin = _first_live_tile(slopes[0::2], slab3, ta_q, ta_k)
    oa = _diff_attention(slopes_a, kmin, slab3, _with_ones_row(vT), _slope_rows(slopes[0::2]), da_lq1, da_lk1,
                         da_lq2, da_lk2, da_subln_w, lam_init, ta_q, ta_k).reshape(N, DA_HEADS * DA_VDIM)

    tb, tb_k = 256, min(256, S)
    n_cp = S // CMP_STRIDE
    n_sel = S // SEL_BLOCK

    def kv_groups(j):
        c0 = COL_KV6 + 128 * j
        return slab3[:, :, c0:c0 + 128].reshape(B, S, G, NSA_DIM).transpose(0, 2, 1, 3)

    def kv_tiles_T(a, rows):
        return a.reshape(B, G, S // rows, rows, NSA_DIM).transpose(0, 1, 2, 4, 3)

    ck_rows = kv_groups(0).reshape(B * G, n_cp, CMP_STRIDE * NSA_DIM)
    cv_rows = kv_groups(1).reshape(B * G, n_cp, CMP_STRIDE * NSA_DIM)
    kc = _compress(ck_rows, cmp_k_pe, cmp_k_w1, cmp_k_w2).reshape(B, G, n_cp, NSA_DIM).astype(BF16)
    vc = _compress(cv_rows, cmp_v_pe, cmp_v_w1, cmp_v_w2).reshape(B, G, n_cp, NSA_DIM).astype(BF16)
    nq = (slab3[:, :, COL_NSAQ:COL_NSAQ + 512] * jnp.asarray(NSA_DIM ** -0.5, BF16))
    nqT = nq.reshape(B, S // tb, tb, G, HPG, NSA_DIM).transpose(0, 3, 1, 5, 4, 2).reshape(B, G, S // tb, NSA_DIM, HPG * tb)
    gT = (slab3[:, :, COL_NSAG:COL_NSAG + 3 * NSA_HEADS].astype(F32).reshape(B, S // tb, tb, G, HPG, 3)
          .transpose(0, 3, 1, 5, 4, 2).reshape(B, G, S // tb, 3, HPG * tb))
    onehot = jnp.asarray((np.arange(S)[:, None] // SEL_BLOCK == np.arange(SEL_COLS)[None, :]).astype(np.float32), BF16)
    zeros64 = jnp.zeros((B, G, S, NSA_DIM), BF16)
    ksa = jnp.concatenate([kv_groups(2), zeros64, jnp.broadcast_to(onehot, (B, G, S, SEL_COLS))], axis=-1)
    pad_rows = jnp.zeros((B, G, WINDOW, 2 * NSA_DIM), BF16).at[..., NSA_DIM].set(1.0)
    kwa = jnp.concatenate([pad_rows, jnp.concatenate([kv_groups(4), zeros64], axis=-1)], axis=2)
    vw_pad = jnp.pad(kv_groups(5), ((0, 0), (0, 0), (WINDOW, 0), (0, 0)))
    vwT = vw_pad.reshape(B, G, (S + WINDOW) // tb, tb, NSA_DIM).transpose(0, 1, 2, 4, 3)
    r_w = np.arange(WINDOW + tb)[:, None]
    lane_w = np.arange(HPG * tb)[None, :]
    d_w = (WINDOW + lane_w % tb - r_w).astype(np.float32)
    slope_w = slopes[1::2].reshape(G, 1, HPG)[:, :, lane_w[0] // tb]
    wb = jnp.asarray(np.where((d_w >= 0) & (d_w < WINDOW), -slope_w * d_w[None], np.float32(NEG)).astype(np.float32))
    obT = _nsa_attention(slopes_b, nqT, kc, vc.transpose(0, 1, 3, 2), ksa, kv_tiles_T(kv_groups(3), tb_k),
                         kwa, vwT, wb, gT, _cmp_to_sel_T(n_cp, n_sel), tb, tb_k)
    ob = (obT.reshape(B, G, S // tb, NSA_DIM, HPG, tb).transpose(0, 2, 5, 1, 4, 3)
          .reshape(N, NSA_HEADS * NSA_DIM))

    wr_hi = router_w.astype(BF16)
    wr_lo = (router_w.astype(F32) - wr_hi.astype(F32)).astype(BF16)
    x1, h2, logits = _merge(oa, ob, slab, x2, g1, norm2_w, sh2, sc2, w_da_out.astype(BF16),
                            w_nsa_out.astype(BF16), w_o.astype(BF16), jnp.stack([wr_hi, wr_lo]), S)

    rb = 512
    tc = 256
    top_eT, top_wT = _route(logits.T, router_b)
    destT, counts = _slots(top_eT, rb)
    n_rows = ((N * MOE_TOPK + N_EXPERTS * (rb - 1) + rb - 1) // rb) * rb
    padded = (jnp.ceil(counts[:, 0] / rb) * rb).astype(I32)
    pend = jnp.cumsum(padded)
    blk_start = jnp.arange(n_rows // rb, dtype=I32) * rb
    blk_e = jnp.minimum(jnp.sum((pend[None, :] <= blk_start[:, None]).astype(I32), axis=1), N_EXPERTS - 1)
    d_flat = destT.reshape(-1)
    piece_idx = jnp.concatenate([d_flat + s * n_rows for s in range(SC_SPLIT)])
    xs2d = _sc_scatter_rows(h2.reshape(SC_SPLIT * N, SC_PIECE), piece_idx, SC_SPLIT * n_rows, MOE_TOPK)
    cnt = counts[:, 0].astype(I32)
    pad_start = jnp.concatenate([pend - padded + cnt, pend[-1:]])
    pad_len = jnp.concatenate([padded - cnt, n_rows - pend[-1:]])
    xs = _zero_pads(xs2d, pad_start, pad_len, rb, n_rows).reshape(SC_SPLIT, n_rows, SC_PIECE)
    ys = _experts(blk_e, xs, exp_w_gate, exp_w_up, exp_w_down, rb)
    sgu = jnp.concatenate([sh_w_gate, sh_w_up], axis=-1).astype(BF16)
    yg = _sc_gather_rows(ys.reshape(SC_SPLIT * n_rows, SC_PIECE), piece_idx).reshape(SC_SPLIT, MOE_TOPK, N, SC_PIECE)
    out = _combine(yg, top_wT.T, h2, x1, g2, final_norm_w, sgu, sh_w_down.astype(BF16), S, tc)
    return out.reshape(B, S, D)


def kernel(x, c, ada_w, ada_b, norm1_w, w_in, da_lq1, da_lk1, da_lq2, da_lk2, da_subln_w, cmp_k_pe, cmp_k_w1, cmp_k_w2, cmp_v_pe, cmp_v_w1, cmp_v_w2, w_da_out, w_nsa_out, w_o, norm2_w, router_w, router_b, exp_w_gate, exp_w_up, exp_w_down, sh_w_gate, sh_w_up, sh_w_down, final_norm_w):
    depth = ada_w.shape[0]
    assert depth == 1, "one decoder layer"
    lam_init = 0.8 - 0.6 * math.exp(-0.3 * 0)
    return _layer(x, c, ada_w[0], ada_b[0], norm1_w[0], w_in[0], da_lq1[0], da_lk1[0], da_lq2[0], da_lk2[0],
                  da_subln_w[0], cmp_k_pe[0], cmp_k_w1[0], cmp_k_w2[0], cmp_v_pe[0], cmp_v_w1[0], cmp_v_w2[0],
                  w_da_out[0], w_nsa_out[0], w_o[0], norm2_w[0], router_w[0], router_b[0],
                  exp_w_gate[0], exp_w_up[0], exp_w_down[0], sh_w_gate[0], sh_w_up[0], sh_w_down[0],
                  final_norm_w, lam_init)
```

```python
import functools
import math

import numpy as np
import jax
import jax.numpy as jnp
from jax import lax
from jax.experimental import pallas as pl
from jax.experimental.pallas import tpu as pltpu
from jax.experimental.pallas import tpu_sc as plsc

F32 = jnp.float32
BF16 = jnp.bfloat16
I32 = jnp.int32
U32 = jnp.uint32
HIGHEST = lax.Precision.HIGHEST

NORM_EPS = 1e-6
NEG = -1e30

DA_HEADS = 8
DA_DIM = 64
DA_VDIM = 128
DA_SUBLN_EPS = 1e-5

NSA_HEADS = 8
NSA_GROUPS = 2
NSA_HPG = 4
NSA_DIM = 64
CMP_BLOCK = 32
CMP_STRIDE = 16
SEL_BLOCK = 64
SEL_TOPK = 16
WINDOW = 512
FORCE_BONUS = 1e4
SEL_COLS = 128

N_EXPERTS = 64
MOE_TOPK = 8
N_EXPERT_GROUPS = 8
TOPK_EXPERT_GROUPS = 4
ROUTED_SCALE = 2.5

VMEM_LIMIT_V7X = 56 * 1024 * 1024

COL_DAQ, COL_DAK, COL_DAV = 0, 1024, 2048
COL_MERGE = 3072
COL_NSAQ = 5120
COL_KV6 = 5632
COL_NSAG = 6400
SLAB_COLS = 6528


SUBLANES = 8
SC_WINDOW = 128
SC_SPLIT = 2
SC_PIECE = 256

POS_SPLIT = 16
POS_ROWS = 16
ONES_ROWS = 16


def _pos_features(rows):
    r = np.arange(rows)
    f = np.zeros((rows, 128), np.float32)
    f[:, 0:3] = (r // POS_SPLIT)[:, None]
    f[:, 3:6] = (r % POS_SPLIT)[:, None]
    return jnp.asarray(f, BF16)


def _slope_rows(slopes):
    s = np.asarray(slopes, np.float32)
    bf = lambda x: x.astype(BF16).astype(np.float32)
    p1 = bf(s)
    p2 = bf(s - p1)
    p3 = bf(s - p1 - p2)
    out = np.zeros((s.shape[0], POS_ROWS, 128), np.float32)
    for i, piece in enumerate((p1, p2, p3)):
        out[:, i, :] = POS_SPLIT * piece[:, None]
        out[:, 3 + i, :] = piece[:, None]
    return jnp.asarray(out)


def _with_ones_row(vT):
    shape = vT.shape[:-2] + (ONES_ROWS, vT.shape[-1])
    extra = jnp.zeros(shape, vT.dtype).at[..., 0, :].set(1.0)
    return jnp.concatenate([vT, extra], axis=-2)


def _cparams(sem, vmem=VMEM_LIMIT_V7X):
    return pltpu.CompilerParams(dimension_semantics=sem, vmem_limit_bytes=vmem)


def _ada_kernel(c_ref, w_ref, b_ref, o_ref):
    c = c_ref[...]
    ca = c * jax.nn.sigmoid(c)
    o_ref[...] = jnp.dot(ca, w_ref[...], preferred_element_type=F32, precision=HIGHEST) + b_ref[...]


def _ada(c, w, b):
    B, D = c.shape
    n_out = w.shape[1]
    rows = 8
    cp = jnp.zeros((rows, D), F32).at[:B].set(c)
    tn = 1024
    out = pl.pallas_call(
        _ada_kernel,
        grid=(n_out // tn,),
        in_specs=[pl.BlockSpec((rows, D), lambda j: (0, 0)),
                  pl.BlockSpec((D, tn), lambda j: (0, j)),
                  pl.BlockSpec((1, tn), lambda j: (0, j))],
        out_specs=pl.BlockSpec((rows, tn), lambda j: (0, j)),
        out_shape=jax.ShapeDtypeStruct((rows, n_out), F32),
        compiler_params=_cparams(("arbitrary",)),
        name="ada",
    )(cp, w, b.reshape(1, n_out))
    return out[:B]


def _modulate(x, nw, sh, sc):
    r = lax.rsqrt(jnp.mean(x * x, axis=-1, keepdims=True) + NORM_EPS)
    return (x * r) * nw * (1.0 + sc) + sh


def _inproj_kernel(x_ref, nw_ref, sh_ref, sc_ref, w_ref, o_ref, h_ref):
    @pl.when(pl.program_id(1) == 0)
    def _():
        h_ref[...] = _modulate(x_ref[...], nw_ref[...], sh_ref[...], sc_ref[...]).astype(BF16)

    o_ref[...] = jnp.dot(h_ref[...], w_ref[...], preferred_element_type=F32).astype(BF16)


def _inproj(x2, nw, sh, sc, w_slab, S):
    N, D = x2.shape
    tm = min(1024, S)
    tn = SLAB_COLS // 3
    per_b = S // tm
    return pl.pallas_call(
        _inproj_kernel,
        grid=(N // tm, SLAB_COLS // tn),
        in_specs=[pl.BlockSpec((tm, D), lambda i, j: (i, 0)),
                  pl.BlockSpec((1, D), lambda i, j: (0, 0)),
                  pl.BlockSpec((None, 1, D), lambda i, j: (i // per_b, 0, 0)),
                  pl.BlockSpec((None, 1, D), lambda i, j: (i // per_b, 0, 0)),
                  pl.BlockSpec((D, tn), lambda i, j: (0, j))],
        out_specs=pl.BlockSpec((tm, tn), lambda i, j: (i, j)),
        out_shape=jax.ShapeDtypeStruct((N, SLAB_COLS), BF16),
        scratch_shapes=[pltpu.VMEM((tm, D), BF16)],
        compiler_params=_cparams(("parallel", "arbitrary")),
        name="inproj",
    )(x2, nw.reshape(1, D), sh.reshape(-1, 1, D), sc.reshape(-1, 1, D), w_slab)


def _cmp_kernel(r_ref, pe_ref, w1a_ref, w1b_ref, w2_ref, o_ref):
    r = r_ref[...]
    n_rows = r.shape[0]
    a = jnp.dot(r, w1a_ref[...], preferred_element_type=F32)
    b = jnp.dot(r, w1b_ref[...], preferred_element_type=F32)
    pe = pe_ref[...]
    half = pe.shape[1] // 2
    bias = (jnp.dot(pe[:, :half], w1a_ref[...].astype(F32), preferred_element_type=F32, precision=HIGHEST)
            + jnp.dot(pe[:, half:], w1b_ref[...].astype(F32), preferred_element_type=F32, precision=HIGHEST))
    hid = a + pltpu.roll(b, n_rows - 1, 0) + bias
    act = 0.5 * hid * (1.0 + jnp.tanh(0.7978845608028654 * (hid + 0.044715 * hid * hid * hid)))
    o_ref[...] = jnp.dot(act, w2_ref[...], preferred_element_type=F32, precision=HIGHEST)


def _compress(rows, pe, w1, w2):
    BG, n_rows, width = rows.shape
    hidden = w1.shape[1]
    w1a = w1[:width].astype(BF16)
    w1b = w1[width:].astype(BF16)
    return pl.pallas_call(
        _cmp_kernel,
        grid=(BG,),
        in_specs=[pl.BlockSpec((None, n_rows, width), lambda i: (i, 0, 0)),
                  pl.BlockSpec((1, 2 * width), lambda i: (0, 0)),
                  pl.BlockSpec((width, hidden), lambda i: (0, 0)),
                  pl.BlockSpec((width, hidden), lambda i: (0, 0)),
                  pl.BlockSpec((hidden, NSA_DIM), lambda i: (0, 0))],
        out_specs=pl.BlockSpec((None, n_rows, NSA_DIM), lambda i: (i, 0, 0)),
        out_shape=jax.ShapeDtypeStruct((BG, n_rows, NSA_DIM), F32),
        compiler_params=_cparams(("parallel",)),
        name="cmp",
    )(rows, pe.reshape(1, -1).astype(F32), w1a, w1b, w2.astype(F32))


def _da_kernel(slopes_ref, kmin_ref, q_ref, k_ref, vT_ref, pos_ref, rel_ref, srow_ref, lq1_ref, lk1_ref, lq2_ref,
               lk2_ref, subw_ref, o_ref, acc_ref, qa_ref, sa_ref, sb_ref, pa_ref, pb_ref, *, tq, tk, lam_init):
    h = pl.program_id(1)
    qi = pl.program_id(2)
    slope = slopes_ref[h]
    q0 = qi * tq
    w = 2 * tq
    nk = k_ref.shape[0] // tk
    qT = (q_ref[...].astype(F32) * DA_DIM ** -0.5).T.astype(BF16)
    row = lax.broadcasted_iota(I32, qT.shape, 0)
    zero = jnp.zeros_like(qT)
    qa_ref[0:2 * DA_DIM, :] = jnp.concatenate(
        [jnp.where(row < DA_DIM, qT, zero), jnp.where(row >= DA_DIM, qT, zero)], axis=1)
    qa_ref[2 * DA_DIM:2 * DA_DIM + POS_ROWS, :] = jnp.concatenate([srow_ref[...]] * (w // 128), axis=1).astype(BF16)
    qa_ref[2 * DA_DIM + POS_ROWS:, :] = jnp.zeros((2 * DA_DIM - POS_ROWS, w), BF16)

    def scores(kt, s_ref):
        k0 = kt * tk
        k_t = k_ref[pl.ds(pl.multiple_of(k0, tk), tk), :]
        s = jnp.dot(jnp.concatenate([k_t, pos_ref[...]], axis=1), qa_ref[...], preferred_element_type=F32)
        s = jnp.where(rel_ref[...] <= q0 - k0, s, NEG)
        s_ref[...] = s
        return jnp.max(s, axis=0, keepdims=True)

    def probs(s_ref, p_ref, mx, off, m_old):
        m_new = jnp.maximum(m_old, mx + off)
        p_ref[...] = jnp.exp(s_ref[...] - (m_new - off)).astype(BF16)
        return m_new, jnp.exp(m_old - m_new)

    def accumulate(kt, p_ref, alpha):
        acc_ref[...] = alpha * acc_ref[...] + jnp.dot(vT_ref[kt], p_ref[...], preferred_element_type=F32)

    n_tiles = (q0 + tq - 1) // tk + 1
    kt_min = kmin_ref[(pl.program_id(0) * pl.num_programs(1) + h) * pl.num_programs(2) + qi]
    acc_ref[...] = jnp.zeros_like(acc_ref)
    pb_ref[...] = jnp.zeros_like(pb_ref)
    mx_a = scores(kt_min, sa_ref)

    def pair(j, carry):
        m, mx_a, alpha_b = carry
        ta = kt_min + 2 * j
        tb = ta + 1
        off_a = slope * (ta * tk).astype(F32)
        off_b = jnp.where(tb < n_tiles, slope * (tb * tk).astype(F32), NEG)
        accumulate(jnp.maximum(ta - 1, 0), pb_ref, alpha_b)
        m, alpha_a = probs(sa_ref, pa_ref, mx_a, off_a, m)
        mx_b = scores(jnp.minimum(tb, nk - 1), sb_ref)
        accumulate(ta, pa_ref, alpha_a)
        m, alpha_b = probs(sb_ref, pb_ref, mx_b, off_b, m)
        mx_a = scores(jnp.minimum(ta + 2, nk - 1), sa_ref)
        return m, mx_a, alpha_b

    n_pairs = (n_tiles - kt_min + 1) // 2
    init = (jnp.full((1, w), NEG, F32), mx_a, jnp.ones((1, w), F32))
    _, _, alpha_b = lax.fori_loop(0, n_pairs, pair, init)
    accumulate(jnp.minimum(kt_min + 2 * n_pairs - 1, nk - 1), pb_ref, alpha_b)

    lam = (jnp.exp(jnp.sum(lq1_ref[...] * lk1_ref[...], axis=-1, keepdims=True))
           - jnp.exp(jnp.sum(lq2_ref[...] * lk2_ref[...], axis=-1, keepdims=True)) + lam_init)
    on = acc_ref[0:DA_VDIM, :] * (1.0 / acc_ref[DA_VDIM:DA_VDIM + 1, :])
    o = on[:, :tq] - lam * on[:, tq:]
    r = lax.rsqrt(jnp.mean(o * o, axis=0, keepdims=True) + DA_SUBLN_EPS)
    o_ref[...] = (o * r * subw_ref[...] * (1.0 - lam_init)).T.astype(BF16)


EXP_UNDERFLOW = 104.0


def _first_live_tile(slopes_np, slab3, tq, tk):
    B, S, _ = slab3.shape
    H = DA_HEADS
    width = 2 * DA_DIM * H
    grp = jnp.asarray((np.arange(width)[:, None] // DA_DIM == np.arange(2 * H)[None, :]).astype(np.float32), BF16)

    def norms(c0):
        v = slab3[:, :, c0:c0 + width]
        return jnp.sqrt(jnp.einsum('bsc,cg->bsg', v * v, grp, preferred_element_type=F32))

    qn = norms(COL_DAQ) * DA_DIM ** -0.5
    qn = jnp.max(qn.reshape(B, S // tq, tq, H, 2), axis=2).transpose(0, 2, 3, 1)
    kn = jnp.max(norms(COL_DAK), axis=1).reshape(B, H, 2)
    bound = jnp.max(qn * kn[..., None], axis=2) * 1.01 + 0.01
    q0 = (jnp.arange(S // tq, dtype=F32) * tq)[None, None, :]
    slope = jnp.asarray(slopes_np, F32)[None, :, None]
    first_key = q0 - (EXP_UNDERFLOW + 2.0 * bound) / slope
    kt = jnp.floor(first_key / tk).astype(I32)
    n_tiles = (jnp.arange(S // tq, dtype=I32) * tq + tq - 1) // tk + 1
    return jnp.clip(kt, 0, n_tiles[None, None, :] - 1).reshape(-1)


def _diff_attention(slopes, kmin, slab3, vT, srows, lq1, lk1, lq2, lk2, subw, lam_init, tq, tk):
    B, S, _ = slab3.shape
    nk = S // tk
    assert tk % tq == 0 and S % tk == 0
    qcol = COL_DAQ // 128
    tile = tq
    kcol = COL_DAK // 128
    v_rows = vT.shape[3]
    vec = lambda a: a.reshape(1, DA_DIM).astype(F32)
    grid_spec = pltpu.PrefetchScalarGridSpec(
        num_scalar_prefetch=2,
        grid=(B, DA_HEADS, S // tile),
        in_specs=[pl.BlockSpec((None, tile, 128), lambda b, h, i, s, km: (b, i, qcol + h)),
                  pl.BlockSpec((None, S, 128), lambda b, h, i, s, km: (b, 0, kcol + h)),
                  pl.BlockSpec((None, None, nk, v_rows, tk), lambda b, h, i, s, km: (b, h, 0, 0, 0)),
                  pl.BlockSpec((tk, 128), lambda b, h, i, s, km: (0, 0)),
                  pl.BlockSpec((tk, 2 * tq), lambda b, h, i, s, km: (0, 0)),
                  pl.BlockSpec((None, POS_ROWS, 128), lambda b, h, i, s, km: (h, 0, 0)),
                  pl.BlockSpec((1, DA_DIM), lambda b, h, i, s, km: (0, 0)),
                  pl.BlockSpec((1, DA_DIM), lambda b, h, i, s, km: (0, 0)),
                  pl.BlockSpec((1, DA_DIM), lambda b, h, i, s, km: (0, 0)),
                  pl.BlockSpec((1, DA_DIM), lambda b, h, i, s, km: (0, 0)),
                  pl.BlockSpec((DA_VDIM, 1), lambda b, h, i, s, km: (0, 0))],
        out_specs=pl.BlockSpec((None, tile, 128), lambda b, h, i, s, km: (b, i, h)),
        scratch_shapes=[pltpu.VMEM((v_rows, 2 * tq), F32), pltpu.VMEM((4 * DA_DIM, 2 * tq), BF16),
                        pltpu.VMEM((tk, 2 * tq), F32), pltpu.VMEM((tk, 2 * tq), F32),
                        pltpu.VMEM((tk, 2 * tq), BF16), pltpu.VMEM((tk, 2 * tq), BF16)],
    )
    rel = jnp.asarray(np.arange(tk)[:, None] - (np.arange(2 * tq) % tq)[None, :], I32)
    return pl.pallas_call(
        functools.partial(_da_kernel, tq=tq, tk=tk, lam_init=lam_init),
        grid_spec=grid_spec,
        out_shape=jax.ShapeDtypeStruct((B, S, DA_HEADS * DA_VDIM), BF16),
        compiler_params=_cparams(("parallel", "parallel", "arbitrary")),
        name="diffattn",
    )(slopes, kmin, slab3, slab3, vT, _pos_features(tk), rel, srows, vec(lq1), vec(lk1), vec(lq2), vec(lk2),
      subw.reshape(DA_VDIM, 1).astype(F32))


def _nsa_kernel(slopes_ref, qT_ref, kc_ref, vcT_ref, ksa_ref, vsT_ref, kwa_ref, vwT_ref, wb_ref, gT_ref, mT_ref,
                grp_ref, ltri_ref, oT_ref, acc_ref, res_ref, base_ref, qa_ref, sa_ref, sb_ref, pa_ref, pb_ref,
                list_ref, slist_ref, lsem, imp_ref, *, t, tk, n_top):
    g = pl.program_id(1)
    qi = pl.program_id(2)
    q0 = qi * t
    w = NSA_HPG * t
    n_cp = kc_ref.shape[0]
    n_sel = mT_ref.shape[0]
    lane1 = lax.broadcasted_iota(I32, (1, w), 1)
    slope_row = jnp.zeros((1, w), F32)
    for hh in range(NSA_HPG):
        slope_row = jnp.where((lane1 >= hh * t) & (lane1 < (hh + 1) * t), slopes_ref[g * NSA_HPG + hh], slope_row)
    q_all = qT_ref[...]
    gates = jax.nn.sigmoid(gT_ref[...])
    key_off = lax.broadcasted_iota(I32, (tk, w), 0)
    t_loc = jnp.bitwise_and(lax.broadcasted_iota(I32, (tk, w), 1), t - 1)
    base_ref[...] = slope_row * key_off.astype(F32)

    def compressed(rows):
        c_idx = lax.broadcasted_iota(I32, (rows, w), 0)
        t_c = q0 + jnp.bitwise_and(lax.broadcasted_iota(I32, (rows, w), 1), t - 1)
        d_c = t_c - (c_idx * CMP_STRIDE + (CMP_BLOCK - 1))
        ok_c = d_c >= 0
        s = jnp.dot(kc_ref[0:rows, :], q_all, preferred_element_type=F32) - slope_row * d_c.astype(F32)
        s = jnp.where(ok_c, s, NEG)
        e = jnp.exp(s - jnp.max(s, axis=0, keepdims=True))
        inv = 1.0 / jnp.sum(e, axis=0, keepdims=True)
        p = jnp.where(ok_c, e * inv, 0.0)
        res_ref[...] = gates[0:1, :] * jnp.dot(vcT_ref[:, 0:rows], p.astype(BF16), preferred_element_type=F32)
        p_sum = p[:, 0:t]
        for hh in range(1, NSA_HPG):
            p_sum = p_sum + p[:, hh * t:(hh + 1) * t]
        p_hi = p_sum.astype(BF16)
        p_lo = (p_sum - p_hi.astype(F32)).astype(BF16)
        mT = mT_ref[:, 0:rows]
        imp_ref[...] = jnp.dot(mT, p_hi, preferred_element_type=F32) + jnp.dot(mT, p_lo, preferred_element_type=F32)

    chunk = min(n_cp, 128)
    n_chunk = jnp.minimum((q0 + t - CMP_BLOCK) // (CMP_STRIDE * chunk) + 1, n_cp // chunk)
    for c in range(1, n_cp // chunk + 1):
        pl.when(n_chunk == c)(functools.partial(compressed, c * chunk))

    imp = imp_ref[...]
    blk = lax.broadcasted_iota(I32, (n_sel, t), 0)
    t_s = q0 + lax.broadcasted_iota(I32, (n_sel, t), 1)
    cur = jnp.right_shift(t_s, int(math.log2(SEL_BLOCK)))
    forced = (blk == 0) | (blk == cur) | (blk == cur - 1)
    score = jnp.where(blk * SEL_BLOCK <= t_s, imp + jnp.where(forced, FORCE_BONUS, 0.0), NEG)
    blk_f = blk.astype(F32)
    sel = jnp.zeros((n_sel, t), F32)
    for _ in range(n_top):
        _, idx = _first_max(score, blk_f)
        hit = blk_f == idx
        sel = jnp.where(hit, 1.0, sel)
        score = jnp.where(hit, -jnp.inf, score)
    n_full = q0 // tk
    nt_pad = grp_ref.shape[0]
    cnt = jnp.broadcast_to(jnp.sum(sel, axis=1, keepdims=True), (n_sel, 128)).astype(BF16)
    tile_cnt = jnp.dot(grp_ref[...], cnt, preferred_element_type=F32)
    kt_i = lax.broadcasted_iota(I32, (nt_pad, 128), 0)
    active = (tile_cnt > 0.5) & (kt_i < n_full)
    act_f = jnp.where(active, 1.0, 0.0)
    before = jnp.dot(ltri_ref[...], act_f.astype(BF16), preferred_element_type=F32)
    lane_j = lax.broadcasted_iota(I32, (nt_pad, 128), 1)
    slot_hit = active & (before == lane_j.astype(F32))
    tiles_row = jnp.sum(jnp.where(slot_hit, kt_i.astype(F32), 0.0), axis=0, keepdims=True)
    n_act_row = jnp.sum(act_f, axis=0, keepdims=True)
    lane8 = lax.broadcasted_iota(I32, (8, 128), 1)
    list_ref[...] = jnp.where(lane8 == 127, n_act_row, tiles_row).astype(I32)
    list_copy = pltpu.make_async_copy(list_ref, slist_ref, lsem)
    list_copy.start()

    selb = jnp.where(sel > 0.5, 0.0, NEG)
    row64 = lax.broadcasted_iota(I32, (NSA_DIM, w), 0)
    qa_ref[0:NSA_DIM, :] = q_all
    qa_ref[NSA_DIM:2 * NSA_DIM, :] = jnp.where(row64 == 0, NEG, 0.0).astype(BF16)
    qa_ref[2 * NSA_DIM:2 * NSA_DIM + n_sel, :] = jnp.concatenate([selb] * NSA_HPG, axis=1).astype(BF16)
    if n_sel < SEL_COLS:
        qa_ref[2 * NSA_DIM + n_sel:, :] = jnp.zeros((SEL_COLS - n_sel, w), BF16)

    n_wt = (WINDOW + t) // t
    kw_t = kwa_ref[pl.ds(pl.multiple_of(q0, t), WINDOW + t), :]
    s = jnp.dot(kw_t, qa_ref[0:2 * NSA_DIM, :], preferred_element_type=F32) + wb_ref[...]
    p = jnp.exp(s - jnp.max(s, axis=0, keepdims=True))
    inv = 1.0 / jnp.sum(p, axis=0, keepdims=True)
    p = p.astype(BF16)
    o_w = jnp.dot(vwT_ref[qi], p[0:t, :], preferred_element_type=F32)
    for c in range(1, n_wt):
        o_w = o_w + jnp.dot(vwT_ref[qi + c], p[c * t:(c + 1) * t, :], preferred_element_type=F32)
    res_ref[...] = res_ref[...] + gates[2:3, :] * (o_w * inv)

    def scores(kt, s_ref):
        k_t = ksa_ref[pl.ds(pl.multiple_of(kt * tk, tk), tk), :]
        s = jnp.dot(k_t, qa_ref[...], preferred_element_type=F32) + base_ref[...]
        s_ref[...] = s
        return jnp.max(s, axis=0, keepdims=True)

    def probs(s_ref, p_ref, mx, off, m_old, l_old):
        m_new = jnp.maximum(m_old, mx + off)
        p = jnp.exp(s_ref[...] - (m_new - off))
        alpha = jnp.exp(m_old - m_new)
        p_ref[...] = p.astype(BF16)
        return m_new, alpha * l_old + jnp.sum(p, axis=0, keepdims=True), alpha

    def accumulate(kt, p_ref, alpha):
        acc_ref[...] = alpha * acc_ref[...] + jnp.dot(vsT_ref[kt], p_ref[...], preferred_element_type=F32)

    list_copy.wait()
    n_act = slist_ref[0, 127]
    mx_a = scores(slist_ref[0, 0], sa_ref)

    kd = n_full * tk
    s = jnp.dot(ksa_ref[pl.ds(pl.multiple_of(kd, tk), tk), :], qa_ref[...], preferred_element_type=F32) + base_ref[...]
    s = jnp.where(key_off - t_loc <= q0 - kd, s, NEG)
    off_d = slope_row * kd.astype(F32)
    m = jnp.max(s, axis=0, keepdims=True) + off_d
    p = jnp.exp(s - (m - off_d))
    l = jnp.sum(p, axis=0, keepdims=True)
    acc_ref[...] = jnp.dot(vsT_ref[n_full], p.astype(BF16), preferred_element_type=F32)
    pb_ref[...] = jnp.zeros_like(pb_ref)

    def pair(j, carry):
        m, l, mx_a, alpha_b, tb_prev = carry
        ta = slist_ref[0, 2 * j]
        tb = slist_ref[0, 2 * j + 1]
        off_a = slope_row * (ta * tk).astype(F32)
        off_b = jnp.where(2 * j + 1 < n_act, slope_row * (tb * tk).astype(F32), NEG)
        accumulate(tb_prev, pb_ref, alpha_b)
        m, l, alpha_a = probs(sa_ref, pa_ref, mx_a, off_a, m, l)
        mx_b = scores(tb, sb_ref)
        accumulate(ta, pa_ref, alpha_a)
        m, l, alpha_b = probs(sb_ref, pb_ref, mx_b, off_b, m, l)
        mx_a = scores(slist_ref[0, 2 * j + 2], sa_ref)
        return m, l, mx_a, alpha_b, tb

    n_pairs = (n_act + 1) // 2
    m, l, _, alpha_b, tb_last = lax.fori_loop(0, n_pairs, pair, (m, l, mx_a, jnp.ones((1, w), F32), jnp.int32(0)))
    accumulate(tb_last, pb_ref, alpha_b)
    oT_ref[...] = (res_ref[...] + gates[1:2, :] * (acc_ref[...] * (1.0 / l))).astype(BF16)


def _nsa_attention(slopes, qT, kc, vcT, ksa, vsT, kwa, vwT, wb, gT, mT, t, tk):
    B, G, nq, _, w = qT.shape
    S = nq * t
    n_cp = kc.shape[2]
    n_sel = S // SEL_BLOCK
    n_top = min(SEL_TOPK, n_sel)
    assert tk % t == 0 and S % tk == 0 and t & (t - 1) == 0 and WINDOW % t == 0 and n_sel <= SEL_COLS
    n_tiles = S // tk
    nt_pad = -(-n_tiles // 16) * 16
    assert nt_pad + 2 < 127
    grp = jnp.asarray((np.arange(n_sel)[None, :] // (tk // SEL_BLOCK) == np.arange(nt_pad)[:, None])
                      .astype(np.float32), BF16)
    ltri = jnp.asarray((np.arange(nt_pad)[None, :] < np.arange(nt_pad)[:, None]).astype(np.float32), BF16)
    fixed = lambda b, g, i, s: (b, g, 0, 0)
    fixed5 = lambda b, g, i, s: (b, g, 0, 0, 0)
    tile5 = lambda b, g, i, s: (b, g, i, 0, 0)
    grid_spec = pltpu.PrefetchScalarGridSpec(
        num_scalar_prefetch=1,
        grid=(B, G, nq),
        in_specs=[pl.BlockSpec((None, None, None, NSA_DIM, w), tile5),
                  pl.BlockSpec((None, None, n_cp, NSA_DIM), fixed),
                  pl.BlockSpec((None, None, NSA_DIM, n_cp), fixed),
                  pl.BlockSpec((None, None, S, ksa.shape[-1]), fixed),
                  pl.BlockSpec((None, None, S // tk, NSA_DIM, tk), fixed5),
                  pl.BlockSpec((None, None, S + WINDOW, 2 * NSA_DIM), fixed),
                  pl.BlockSpec((None, None, (S + WINDOW) // t, NSA_DIM, t), fixed5),
                  pl.BlockSpec((None, WINDOW + t, w), lambda b, g, i, s: (g, 0, 0)),
                  pl.BlockSpec((None, None, None, 3, w), tile5),
                  pl.BlockSpec((n_sel, n_cp), lambda b, g, i, s: (0, 0)),
                  pl.BlockSpec((nt_pad, n_sel), lambda b, g, i, s: (0, 0)),
                  pl.BlockSpec((nt_pad, nt_pad), lambda b, g, i, s: (0, 0))],
        out_specs=pl.BlockSpec((None, None, None, NSA_DIM, w), tile5),
        scratch_shapes=[pltpu.VMEM((NSA_DIM, w), F32),
                        pltpu.VMEM((NSA_DIM, w), F32),
                        pltpu.VMEM((tk, w), F32),
                        pltpu.VMEM((2 * NSA_DIM + SEL_COLS, w), BF16),
                        pltpu.VMEM((tk, w), F32), pltpu.VMEM((tk, w), F32),
                        pltpu.VMEM((tk, w), BF16), pltpu.VMEM((tk, w), BF16),
                        pltpu.VMEM((8, 128), I32), pltpu.SMEM((8, 128), I32), pltpu.SemaphoreType.DMA(()),
                        pltpu.VMEM((n_sel, t), F32)],
    )
    return pl.pallas_call(
        functools.partial(_nsa_kernel, t=t, tk=tk, n_top=n_top),
        grid_spec=grid_spec,
        out_shape=jax.ShapeDtypeStruct((B, G, nq, NSA_DIM, w), BF16),
        compiler_params=_cparams(("parallel", "parallel", "arbitrary")),
        name="nsa",
    )(slopes, qT, kc, vcT, ksa, vsT, kwa, vwT, wb, gT, mT, grp, ltri)


def _pack_halves(v):
    c = v.shape[1] // 2
    lo = lax.bitcast_convert_type(v[:, :c].astype(BF16).astype(F32), U32)
    hi = lax.bitcast_convert_type(v[:, c:].astype(BF16).astype(F32), U32)
    return hi | (lo >> 16)


def _unpack_halves(u):
    lo = lax.bitcast_convert_type(u << 16, F32)
    hi = lax.bitcast_convert_type(u & jnp.uint32(0xFFFF0000), F32)
    return lo, hi


def _store_slabs(ref, v):
    packed = _pack_halves(v)
    for s in range(SC_SPLIT):
        ref[s] = packed[:, s * SC_PIECE:(s + 1) * SC_PIECE]


def _load_slabs_bf16(ref):
    parts = [_unpack_halves(ref[s]) for s in range(SC_SPLIT)]
    return jnp.concatenate([p[0].astype(BF16) for p in parts] + [p[1].astype(BF16) for p in parts], axis=1)


def _merge_kernel(oa_ref, ob_ref, ga_ref, gb_ref, x_ref, g1_ref, nw_ref, sh_ref, sc_ref,
                  wa_ref, wb_ref, wo_ref, wr_ref, x1_ref, h2_ref, lg_ref):
    ya = jnp.dot(oa_ref[...], wa_ref[...], preferred_element_type=F32)
    yb = jnp.dot(ob_ref[...], wb_ref[...], preferred_element_type=F32)
    merged = (jax.nn.sigmoid(ga_ref[...].astype(F32)) * ya
              + jax.nn.sigmoid(gb_ref[...].astype(F32)) * yb).astype(BF16)
    mix = jnp.dot(merged, wo_ref[...], preferred_element_type=F32)
    x1 = x_ref[...] + g1_ref[...] * mix
    x1_ref[...] = x1
    h2 = _modulate(x1, nw_ref[...], sh_ref[...], sc_ref[...])
    _store_slabs(h2_ref, h2)
    h_hi = h2.astype(BF16)
    h_lo = (h2 - h_hi.astype(F32)).astype(BF16)
    lg_ref[...] = (jnp.dot(h_hi, wr_ref[0], preferred_element_type=F32)
                   + jnp.dot(h_lo, wr_ref[0], preferred_element_type=F32)
                   + jnp.dot(h_hi, wr_ref[1], preferred_element_type=F32))


def _merge(oa, ob, slab, x2, g1, nw, sh, sc, wa, wb, wo, wr, S):
    N, D = x2.shape
    tm = 256
    per_b = S // tm
    mcol = COL_MERGE // D
    row = lambda i: (i, 0)
    full = lambda i: (0, 0)
    perb = lambda i: (i // per_b, 0, 0)
    n_lg = wr.shape[-1]
    return pl.pallas_call(
        _merge_kernel,
        grid=(N // tm,),
        in_specs=[pl.BlockSpec((tm, oa.shape[1]), row),
                  pl.BlockSpec((tm, ob.shape[1]), row),
                  pl.BlockSpec((tm, D), lambda i: (i, mcol)),
                  pl.BlockSpec((tm, D), lambda i: (i, mcol + 1)),
                  pl.BlockSpec((tm, D), row),
                  pl.BlockSpec((None, 1, D), perb),
                  pl.BlockSpec((1, D), full),
                  pl.BlockSpec((None, 1, D), perb),
                  pl.BlockSpec((None, 1, D), perb),
                  pl.BlockSpec(wa.shape, full),
                  pl.BlockSpec(wb.shape, full),
                  pl.BlockSpec(wo.shape, full),
                  pl.BlockSpec(wr.shape, lambda i: (0, 0, 0))],
        out_specs=[pl.BlockSpec((tm, D), row), pl.BlockSpec((SC_SPLIT, tm, SC_PIECE), lambda i: (0, i, 0)),
                   pl.BlockSpec((tm, n_lg), row)],
        out_shape=[jax.ShapeDtypeStruct((N, D), F32), jax.ShapeDtypeStruct((SC_SPLIT, N, SC_PIECE), U32),
                   jax.ShapeDtypeStruct((N, n_lg), F32)],
        compiler_params=_cparams(("parallel",)),
        name="merge",
    )(oa, ob, slab, slab, x2, g1.reshape(-1, 1, D), nw.reshape(1, D), sh.reshape(-1, 1, D),
      sc.reshape(-1, 1, D), wa, wb, wo, wr)


def _first_max(v, idx):
    mx = jnp.max(v, axis=0, keepdims=True)
    first = jnp.min(jnp.where(v == mx, idx, float(v.shape[0])), axis=0, keepdims=True)
    return mx, first


def _route_kernel(lg_ref, b_ref, e_ref, w_ref):
    lg = lg_ref[...]
    tn = lg.shape[1]
    per_g = N_EXPERTS // N_EXPERT_GROUPS
    scores = jax.nn.sigmoid(lg)
    biased = scores + b_ref[...]
    gi = lax.broadcasted_iota(I32, (N_EXPERT_GROUPS, tn), 0).astype(F32)
    gscore = jnp.zeros((N_EXPERT_GROUPS, tn), F32)
    for g in range(N_EXPERT_GROUPS):
        grp = biased[g * per_g:(g + 1) * per_g, :]
        m1, f1 = _first_max(grp, gi)
        m2 = jnp.max(jnp.where(gi == f1, -jnp.inf, grp), axis=0, keepdims=True)
        gscore = jnp.where(gi == g, m1 + m2, gscore)
    gsel = jnp.zeros((N_EXPERT_GROUPS, tn), F32)
    for _ in range(TOPK_EXPERT_GROUPS):
        _, f = _first_max(gscore, gi)
        hit = gi == f
        gsel = jnp.where(hit, 1.0, gsel)
        gscore = jnp.where(hit, -jnp.inf, gscore)
    ei = lax.broadcasted_iota(I32, (N_EXPERTS, tn), 0).astype(F32)
    emask = jnp.zeros((N_EXPERTS, tn), F32)
    for g in range(N_EXPERT_GROUPS):
        in_g = (ei >= g * per_g) & (ei < (g + 1) * per_g)
        emask = jnp.where(in_g, gsel[g:g + 1, :], emask)
    masked = jnp.where(emask > 0.5, biased, NEG)
    e_out = jnp.zeros((MOE_TOPK, tn), F32)
    w_out = jnp.zeros((MOE_TOPK, tn), F32)
    for r in range(MOE_TOPK):
        _, f = _first_max(masked, ei)
        hit = ei == f
        wv = jnp.sum(jnp.where(hit, scores, 0.0), axis=0, keepdims=True)
        e_out = jnp.where(gi == r, f, e_out)
        w_out = jnp.where(gi == r, wv, w_out)
        masked = jnp.where(hit, -jnp.inf, masked)
    e_ref[...] = e_out.astype(I32)
    w_ref[...] = w_out / jnp.sum(w_out, axis=0, keepdims=True) * ROUTED_SCALE


def _route(lgT, rb):
    E, N = lgT.shape
    tn = 512
    return pl.pallas_call(
        _route_kernel,
        grid=(N // tn,),
        in_specs=[pl.BlockSpec((E, tn), lambda i: (0, i)), pl.BlockSpec((E, 1), lambda i: (0, 0))],
        out_specs=[pl.BlockSpec((MOE_TOPK, tn), lambda i: (0, i)), pl.BlockSpec((MOE_TOPK, tn), lambda i: (0, i))],
        out_shape=[jax.ShapeDtypeStruct((MOE_TOPK, N), I32), jax.ShapeDtypeStruct((MOE_TOPK, N), F32)],
        compiler_params=_cparams(("parallel",)),
        name="route",
    )(lgT, rb.reshape(E, 1).astype(F32))


def _slots_kernel(e_ref, tri_ref, lt_ref, dest_ref, cnt_ref, carry_ref, *, rb):
    phase = pl.program_id(0)
    i = pl.program_id(1)
    e = e_ref[...]
    tn = e.shape[1]
    ei = lax.broadcasted_iota(I32, (N_EXPERTS, tn), 0)

    @pl.when((phase == 0) & (i == 0))
    def _():
        carry_ref[...] = jnp.zeros_like(carry_ref)

    @pl.when(phase == 0)
    def _():
        tot = jnp.zeros((N_EXPERTS, 1), F32)
        for k in range(MOE_TOPK):
            oh = jnp.where(ei == e[k:k + 1, :], 1.0, 0.0)
            tot = tot + jnp.sum(oh, axis=1, keepdims=True)
        carry_ref[...] = carry_ref[...] + tot
        dest_ref[...] = jnp.zeros_like(dest_ref)

    @pl.when((phase == 1) & (i == 0))
    def _():
        cnt = carry_ref[...]
        cnt_ref[...] = cnt
        padded = jnp.broadcast_to(jnp.ceil(cnt / rb) * rb, (N_EXPERTS, 128))
        first = jnp.dot(lt_ref[...], padded, preferred_element_type=F32, precision=HIGHEST)
        carry_ref[...] = first[:, :1]

    @pl.when(phase == 1)
    def _():
        carry = carry_ref[...]
        ki = lax.broadcasted_iota(I32, (MOE_TOPK, tn), 0)
        dest = jnp.zeros((MOE_TOPK, tn), F32)
        for k in range(MOE_TOPK):
            hit = ei == e[k:k + 1, :]
            oh = jnp.where(hit, 1.0, 0.0)
            before = jnp.dot(oh.astype(BF16), tri_ref[...], preferred_element_type=F32)
            row = jnp.sum(jnp.where(hit, before + carry, 0.0), axis=0, keepdims=True)
            dest = jnp.where(ki == k, row, dest)
            carry = carry + jnp.sum(oh, axis=1, keepdims=True)
        carry_ref[...] = carry
        dest_ref[...] = dest.astype(I32)


def _slots(top_eT, rb):
    K, N = top_eT.shape
    tn = 512
    idx = np.arange(tn)
    tri = jnp.asarray((idx[:, None] < idx[None, :]).astype(np.float32), BF16)
    e_idx = np.arange(N_EXPERTS)
    lt = jnp.asarray((e_idx[None, :] < e_idx[:, None]).astype(np.float32), F32)
    return pl.pallas_call(
        functools.partial(_slots_kernel, rb=rb),
        grid=(2, N // tn),
        in_specs=[pl.BlockSpec((K, tn), lambda p, i: (0, i)),
                  pl.BlockSpec((tn, tn), lambda p, i: (0, 0)),
                  pl.BlockSpec((N_EXPERTS, N_EXPERTS), lambda p, i: (0, 0))],
        out_specs=[pl.BlockSpec((K, tn), lambda p, i: (0, i * p)),
                   pl.BlockSpec((N_EXPERTS, 1), lambda p, i: (0, 0))],
        out_shape=[jax.ShapeDtypeStruct((K, N), I32), jax.ShapeDtypeStruct((N_EXPERTS, 1), F32)],
        scratch_shapes=[pltpu.VMEM((N_EXPERTS, 1), F32)],
        compiler_params=_cparams(("arbitrary", "arbitrary")),
        name="slots",
    )(top_eT, tri, lt)


def _sc_scatter_rows(src, idx, n_out, repeat):
    width = src.shape[1]
    n_tok = src.shape[0] // SC_SPLIT
    w = n_tok // SC_WINDOW
    mesh = plsc.VectorSubcoreMesh(core_axis_name="c", subcore_axis_name="s")

    @pl.kernel(out_type=jax.ShapeDtypeStruct((n_out, width), src.dtype), mesh=mesh)
    def scatter_kernel(x_hbm, i_hbm, o_hbm):
        def body(x_vmem, i_vmem):
            for k in range(repeat):
                pltpu.sync_copy(x_vmem, o_hbm.at[i_vmem.at[k]])

        pltpu.emit_pipeline(
            body,
            grid=(SC_SPLIT * w,),
            in_specs=[pl.BlockSpec((SC_WINDOW, width), lambda g: (g, 0)),
                      pl.BlockSpec((repeat, SC_WINDOW), lambda g: (g // w, g % w))],
            out_specs=[],
            core_axis_name=("c", "s"),
            dimension_semantics=(pltpu.PARALLEL,),
        )(x_hbm, i_hbm)

    return scatter_kernel(src, idx.reshape(SC_SPLIT * repeat, n_tok))


def _zero_pads_kernel(start_ref, len_ref, xs_in_ref, xs_ref, zeros_ref, sem, *, rb, n_rows):
    del xs_in_ref
    zeros_ref[...] = jnp.zeros_like(zeros_ref)
    n_seg = start_ref.shape[0]
    bits = rb.bit_length() - 1
    assert rb == 1 << bits

    def for_each_copy(act):
        def rows(row0, size):
            for s in range(SC_SPLIT):
                act(pltpu.make_async_copy(zeros_ref.at[pl.ds(0, size), :],
                                          xs_ref.at[pl.ds(s * n_rows + row0, size), :], sem))

        def segment(e, c):
            start = start_ref[e]
            length = len_ref[e]
            end = start + length
            n_whole = length // rb

            def whole(j, c2):
                rows(pl.multiple_of(end - (j + 1) * rb, SUBLANES), rb)
                return c2

            lax.fori_loop(0, n_whole, whole, 0)
            off = end - n_whole * rb
            rest = length - n_whole * rb
            for b in reversed(range(SUBLANES.bit_length() - 1, bits)):
                size = 1 << b
                take = jnp.bitwise_and(jnp.right_shift(rest, b), 1)
                off = off - take * size

                @pl.when(take == 1)
                def _(off=off, size=size):
                    rows(pl.multiple_of(off, SUBLANES), size)

            lead = jnp.bitwise_and(rest, SUBLANES - 1)
            for i in range(SUBLANES - 1):
                @pl.when(i < lead)
                def _(i=i):
                    rows(start + i, 1)

            return c

        lax.fori_loop(0, n_seg, segment, 0)

    for_each_copy(lambda cp: cp.start())
    for_each_copy(lambda cp: cp.wait())


def _zero_pads(xs2d, start, length, rb, n_rows):
    grid_spec = pltpu.PrefetchScalarGridSpec(
        num_scalar_prefetch=2,
        grid=(1,),
        in_specs=[pl.BlockSpec(memory_space=pl.ANY)],
        out_specs=pl.BlockSpec(memory_space=pl.ANY),
        scratch_shapes=[pltpu.VMEM((rb, xs2d.shape[1]), xs2d.dtype), pltpu.SemaphoreType.DMA(())],
    )
    return pl.pallas_call(
        functools.partial(_zero_pads_kernel, rb=rb, n_rows=n_rows),
        grid_spec=grid_spec,
        out_shape=jax.ShapeDtypeStruct(xs2d.shape, xs2d.dtype),
        input_output_aliases={2: 0},
        compiler_params=_cparams(("arbitrary",)),
        name="zeropads",
    )(start, length, xs2d)


def _experts_kernel(be_ref, x_ref, wg_ref, wu_ref, wd_ref, y_ref, wgu_sc, wd_sc):
    i = pl.program_id(0)
    hdim = wg_ref.shape[1]

    @pl.when((i == 0) | (be_ref[i] != be_ref[jnp.maximum(i - 1, 0)]))
    def _():
        wgu_sc[:, 0:hdim] = wg_ref[...].astype(BF16)
        wgu_sc[:, hdim:] = wu_ref[...].astype(BF16)
        wd_sc[...] = wd_ref[...].astype(BF16)

    gu = jnp.dot(_load_slabs_bf16(x_ref), wgu_sc[...], preferred_element_type=F32)
    g = gu[:, :hdim]
    act = (g * jax.nn.sigmoid(g) * gu[:, hdim:]).astype(BF16)
    _store_slabs(y_ref, jnp.dot(act, wd_sc[...], preferred_element_type=F32))


def _experts(blk_e, xs, wg, wu, wd, rb):
    n_rows = xs.shape[1]
    d_model, hdim = wg.shape[1:]
    per_expert = lambda i, be: (be[i], 0, 0)
    grid_spec = pltpu.PrefetchScalarGridSpec(
        num_scalar_prefetch=1,
        grid=(n_rows // rb,),
        in_specs=[pl.BlockSpec((SC_SPLIT, rb, SC_PIECE), lambda i, be: (0, i, 0)),
                  pl.BlockSpec((None, d_model, hdim), per_expert),
                  pl.BlockSpec((None, d_model, hdim), per_expert),
                  pl.BlockSpec((None, hdim, d_model), per_expert)],
        out_specs=pl.BlockSpec((SC_SPLIT, rb, SC_PIECE), lambda i, be: (0, i, 0)),
        scratch_shapes=[pltpu.VMEM((d_model, 2 * hdim), BF16), pltpu.VMEM((hdim, d_model), BF16)],
    )
    return pl.pallas_call(
        _experts_kernel,
        grid_spec=grid_spec,
        out_shape=jax.ShapeDtypeStruct((SC_SPLIT, n_rows, SC_PIECE), U32),
        compiler_params=_cparams(("arbitrary",)),
        name="experts",
    )(blk_e, xs, wg, wu, wd)


def _sc_gather_rows(table, idx):
    n = idx.shape[0]
    width = table.shape[1]
    mesh = plsc.VectorSubcoreMesh(core_axis_name="c", subcore_axis_name="s")

    @pl.kernel(out_type=jax.ShapeDtypeStruct((n, width), table.dtype), mesh=mesh)
    def gather_kernel(t_hbm, i_hbm, o_hbm):
        def body(i_vmem, o_vmem):
            pltpu.sync_copy(t_hbm.at[i_vmem.at[0]], o_vmem)

        pltpu.emit_pipeline(
            body,
            grid=(n // SC_WINDOW,),
            in_specs=[pl.BlockSpec((1, SC_WINDOW), lambda i: (0, i))],
            out_specs=[pl.BlockSpec((SC_WINDOW, width), lambda i: (i, 0))],
            core_axis_name=("c", "s"),
            dimension_semantics=(pltpu.PARALLEL,),
        )(i_hbm, o_hbm)

    return gather_kernel(table, idx.reshape(1, n))


def _combine_kernel(yg_ref, w_ref, h_ref, x1_ref, g2_ref, fw_ref, sgu_ref, sd_ref, o_ref):
    gu = jnp.dot(_load_slabs_bf16(h_ref), sgu_ref[...], preferred_element_type=F32)
    hdim = gu.shape[1] // 2
    g = gu[:, :hdim]
    ffn = jnp.dot((g * jax.nn.sigmoid(g) * gu[:, hdim:]).astype(BF16), sd_ref[...], preferred_element_type=F32)

    w = w_ref[...]
    tt = w.shape[0]
    lo = [jnp.zeros((tt, SC_PIECE), F32) for _ in range(SC_SPLIT)]
    hi = [jnp.zeros((tt, SC_PIECE), F32) for _ in range(SC_SPLIT)]
    for k in range(MOE_TOPK):
        wk = w[:, k:k + 1]
        for s in range(SC_SPLIT):
            y_lo, y_hi = _unpack_halves(yg_ref[s, k])
            lo[s] = lo[s] + wk * y_lo
            hi[s] = hi[s] + wk * y_hi
    ffn = ffn + jnp.concatenate(lo + hi, axis=1)
    x2 = x1_ref[...] + g2_ref[...] * ffn
    r = lax.rsqrt(jnp.mean(x2 * x2, axis=-1, keepdims=True) + NORM_EPS)
    o_ref[...] = x2 * r * fw_ref[...]


def _combine(yg, top_w, h2p, x1, g2, fw, sgu, sd, S, tt):
    N, D = x1.shape
    per_b = S // tt
    row = lambda i: (i, 0)
    full = lambda i: (0, 0)
    return pl.pallas_call(
        _combine_kernel,
        grid=(N // tt,),
        in_specs=[pl.BlockSpec((SC_SPLIT, MOE_TOPK, tt, SC_PIECE), lambda i: (0, 0, i, 0)),
                  pl.BlockSpec((tt, MOE_TOPK), row),
                  pl.BlockSpec((SC_SPLIT, tt, SC_PIECE), lambda i: (0, i, 0)),
                  pl.BlockSpec((tt, D), row),
                  pl.BlockSpec((None, 1, D), lambda i: (i // per_b, 0, 0)),
                  pl.BlockSpec((1, D), full),
                  pl.BlockSpec(sgu.shape, full),
                  pl.BlockSpec(sd.shape, full)],
        out_specs=pl.BlockSpec((tt, D), row),
        out_shape=jax.ShapeDtypeStruct((N, D), F32),
        compiler_params=_cparams(("parallel",)),
        name="combine",
    )(yg, top_w, h2p, x1, g2.reshape(-1, 1, D), fw.reshape(1, D), sgu, sd)


def _cmp_to_sel_T(n_cp, n_sel):
    c0 = np.arange(n_cp)[None, :] * CMP_STRIDE
    s0 = np.arange(n_sel)[:, None] * SEL_BLOCK
    ov = np.minimum(c0 + CMP_BLOCK, s0 + SEL_BLOCK) - np.maximum(c0, s0)
    m = np.clip(ov, 0, None).astype(np.float32) / CMP_BLOCK
    m[:, n_cp - 1] = 0.0
    return jnp.asarray(m, BF16)


def _slab_weights(w_in):
    D = w_in.shape[0]
    sizes = [1024, 1024, 1024, 512, 128, 128, 128, 128, 128, 128, 24, 2048]
    offs = np.concatenate([[0], np.cumsum(sizes)])
    part = lambda i: w_in[:, offs[i]:offs[i + 1]]
    pieces = [part(0), part(1), part(2), part(11), part(3)] + [part(i) for i in range(4, 10)] + [part(10)]
    w = jnp.concatenate(pieces, axis=1)
    return jnp.pad(w, ((0, 0), (0, SLAB_COLS - w.shape[1]))).astype(BF16)


def _layer(x, c, ada_w, ada_b, norm1_w, w_in, da_lq1, da_lk1, da_lq2, da_lk2, da_subln_w,
           cmp_k_pe, cmp_k_w1, cmp_k_w2, cmp_v_pe, cmp_v_w1, cmp_v_w2, w_da_out, w_nsa_out, w_o,
           norm2_w, router_w, router_b, exp_w_gate, exp_w_up, exp_w_down,
           sh_w_gate, sh_w_up, sh_w_down, final_norm_w, lam_init):
    B, S, D = x.shape
    N = B * S
    G, HPG = NSA_GROUPS, NSA_HPG
    x2 = x.reshape(N, D)

    mod = _ada(c, ada_w, ada_b)
    sh1, sc1, g1, sh2, sc2, g2 = jnp.split(mod, 6, axis=-1)

    slab = _inproj(x2, norm1_w, sh1, sc1, _slab_weights(w_in), S)
    slab3 = slab.reshape(B, S, SLAB_COLS)

    i_all = np.arange(1, DA_HEADS + NSA_HEADS + 1, dtype=np.float32)
    slopes = (2.0 ** (-8.0 * i_all / (DA_HEADS + NSA_HEADS))).astype(np.float32)
    slopes_a = jnp.asarray(slopes[0::2])
    slopes_b = jnp.asarray(slopes[1::2])

    ta_q, ta_k = min(512, S), min(512, S)
    vT = slab3[:, :, COL_DAV:COL_DAV + 1024].reshape(B, S // ta_k, ta_k, DA_HEADS, DA_VDIM).transpose(0, 3, 1, 4, 2)
    kmin = _first_live_tile(slopes[0::2], slab3, ta_q, ta_k)
    oa = _diff_attention(slopes_a, kmin, slab3, _with_ones_row(vT), _slope_rows(slopes[0::2]), da_lq1, da_lk1,
                         da_lq2, da_lk2, da_subln_w, lam_init, ta_q, ta_k).reshape(N, DA_HEADS * DA_VDIM)

    tb, tb_k = 256, min(256, S)
    n_cp = S // CMP_STRIDE
    n_sel = S // SEL_BLOCK

    def kv_groups(j):
        c0 = COL_KV6 + 128 * j
        return slab3[:, :, c0:c0 + 128].reshape(B, S, G, NSA_DIM).transpose(0, 2, 1, 3)

    def kv_tiles_T(a, rows):
        return a.reshape(B, G, S // rows, rows, NSA_DIM).transpose(0, 1, 2, 4, 3)

    ck_rows = kv_groups(0).reshape(B * G, n_cp, CMP_STRIDE * NSA_DIM)
    cv_rows = kv_groups(1).reshape(B * G, n_cp, CMP_STRIDE * NSA_DIM)
    kc = _compress(ck_rows, cmp_k_pe, cmp_k_w1, cmp_k_w2).reshape(B, G, n_cp, NSA_DIM).astype(BF16)
    vc = _compress(cv_rows, cmp_v_pe, cmp_v_w1, cmp_v_w2).reshape(B, G, n_cp, NSA_DIM).astype(BF16)
    nq = (slab3[:, :, COL_NSAQ:COL_NSAQ + 512] * jnp.asarray(NSA_DIM ** -0.5, BF16))
    nqT = nq.reshape(B, S // tb, tb, G, HPG, NSA_DIM).transpose(0, 3, 1, 5, 4, 2).reshape(B, G, S // tb, NSA_DIM, HPG * tb)
    gT = (slab3[:, :, COL_NSAG:COL_NSAG + 3 * NSA_HEADS].astype(F32).reshape(B, S // tb, tb, G, HPG, 3)
          .transpose(0, 3, 1, 5, 4, 2).reshape(B, G, S // tb, 3, HPG * tb))
    onehot = jnp.asarray((np.arange(S)[:, None] // SEL_BLOCK == np.arange(SEL_COLS)[None, :]).astype(np.float32), BF16)
    zeros64 = jnp.zeros((B, G, S, NSA_DIM), BF16)
    ksa = jnp.concatenate([kv_groups(2), zeros64, jnp.broadcast_to(onehot, (B, G, S, SEL_COLS))], axis=-1)
    pad_rows = jnp.zeros((B, G, WINDOW, 2 * NSA_DIM), BF16).at[..., NSA_DIM].set(1.0)
    kwa = jnp.concatenate([pad_rows, jnp.concatenate([kv_groups(4), zeros64], axis=-1)], axis=2)
    vw_pad = jnp.pad(kv_groups(5), ((0, 0), (0, 0), (WINDOW, 0), (0, 0)))
    vwT = vw_pad.reshape(B, G, (S + WINDOW) // tb, tb, NSA_DIM).transpose(0, 1, 2, 4, 3)
    r_w = np.arange(WINDOW + tb)[:, None]
    lane_w = np.arange(HPG * tb)[None, :]
    d_w = (WINDOW + lane_w % tb - r_w).astype(np.float32)
    slope_w = slopes[1::2].reshape(G, 1, HPG)[:, :, lane_w[0] // tb]
    wb = jnp.asarray(np.where((d_w >= 0) & (d_w < WINDOW), -slope_w * d_w[None], np.float32(NEG)).astype(np.float32))
    obT = _nsa_attention(slopes_b, nqT, kc, vc.transpose(0, 1, 3, 2), ksa, kv_tiles_T(kv_groups(3), tb_k),
                         kwa, vwT, wb, gT, _cmp_to_sel_T(n_cp, n_sel), tb, tb_k)
    ob = (obT.reshape(B, G, S // tb, NSA_DIM, HPG, tb).transpose(0, 2, 5, 1, 4, 3)
          .reshape(N, NSA_HEADS * NSA_DIM))

    wr_hi = router_w.astype(BF16)
    wr_lo = (router_w.astype(F32) - wr_hi.astype(F32)).astype(BF16)
    x1, h2, logits = _merge(oa, ob, slab, x2, g1, norm2_w, sh2, sc2, w_da_out.astype(BF16),
                            w_nsa_out.astype(BF16), w_o.astype(BF16), jnp.stack([wr_hi, wr_lo]), S)

    rb = 512
    tc = 256
    top_eT, top_wT = _route(logits.T, router_b)
    destT, counts = _slots(top_eT, rb)
    n_rows = ((N * MOE_TOPK + N_EXPERTS * (rb - 1) + rb - 1) // rb) * rb
    padded = (jnp.ceil(counts[:, 0] / rb) * rb).astype(I32)
    pend = jnp.cumsum(padded)
    blk_start = jnp.arange(n_rows // rb, dtype=I32) * rb
    blk_e = jnp.minimum(jnp.sum((pend[None, :] <= blk_start[:, None]).astype(I32), axis=1), N_EXPERTS - 1)
    d_flat = destT.reshape(-1)
    piece_idx = jnp.concatenate([d_flat + s * n_rows for s in range(SC_SPLIT)])
    xs2d = _sc_scatter_rows(h2.reshape(SC_SPLIT * N, SC_PIECE), piece_idx, SC_SPLIT * n_rows, MOE_TOPK)
    cnt = counts[:, 0].astype(I32)
    pad_start = jnp.concatenate([pend - padded + cnt, pend[-1:]])
    pad_len = jnp.concatenate([padded - cnt, n_rows - pend[-1:]])
    xs = _zero_pads(xs2d, pad_start, pad_len, rb, n_rows).reshape(SC_SPLIT, n_rows, SC_PIECE)
    ys = _experts(blk_e, xs, exp_w_gate, exp_w_up, exp_w_down, rb)
    sgu = jnp.concatenate([sh_w_gate, sh_w_up], axis=-1).astype(BF16)
    yg = _sc_gather_rows(ys.reshape(SC_SPLIT * n_rows, SC_PIECE), piece_idx).reshape(SC_SPLIT, MOE_TOPK, N, SC_PIECE)
    out = _combine(yg, top_wT.T, h2, x1, g2, final_norm_w, sgu, sh_w_down.astype(BF16), S, tc)
    return out.reshape(B, S, D)


def kernel(x, c, ada_w, ada_b, norm1_w, w_in, da_lq1, da_lk1, da_lq2, da_lk2, da_subln_w, cmp_k_pe, cmp_k_w1, cmp_k_w2, cmp_v_pe, cmp_v_w1, cmp_v_w2, w_da_out, w_nsa_out, w_o, norm2_w, router_w, router_b, exp_w_gate, exp_w_up, exp_w_down, sh_w_gate, sh_w_up, sh_w_down, final_norm_w):
    depth = ada_w.shape[0]
    assert depth == 1, "one decoder layer"
    lam_init = 0.8 - 0.6 * math.exp(-0.3 * 0)
    return _layer(x, c, ada_w[0], ada_b[0], norm1_w[0], w_in[0], da_lq1[0], da_lk1[0], da_lq2[0], da_lk2[0],
                  da_subln_w[0], cmp_k_pe[0], cmp_k_w1[0], cmp_k_w2[0], cmp_v_pe[0], cmp_v_w1[0], cmp_v_w2[0],
                  w_da_out[0], w_nsa_out[0], w_o[0], norm2_w[0], router_w[0], router_b[0],
                  exp_w_gate[0], exp_w_up[0], exp_w_down[0], sh_w_gate[0], sh_w_up[0], sh_w_down[0],
                  final_norm_w, lam_init)
```

```python
import functools
import math

import numpy as np
import jax
import jax.numpy as jnp
from jax import lax
from jax.experimental import pallas as pl
from jax.experimental.pallas import tpu as pltpu
from jax.experimental.pallas import tpu_sc as plsc

F32 = jnp.float32
BF16 = jnp.bfloat16
I32 = jnp.int32
U32 = jnp.uint32
HIGHEST = lax.Precision.HIGHEST

NORM_EPS = 1e-6
NEG = -1e30

DA_HEADS = 8
DA_DIM = 64
DA_VDIM = 128
DA_SUBLN_EPS = 1e-5

NSA_HEADS = 8
NSA_GROUPS = 2
NSA_HPG = 4
NSA_DIM = 64
CMP_BLOCK = 32
CMP_STRIDE = 16
SEL_BLOCK = 64
SEL_TOPK = 16
WINDOW = 512
FORCE_BONUS = 1e4
SEL_COLS = 128

N_EXPERTS = 64
MOE_TOPK = 8
N_EXPERT_GROUPS = 8
TOPK_EXPERT_GROUPS = 4
ROUTED_SCALE = 2.5

VMEM_LIMIT_V7X = 56 * 1024 * 1024

COL_DAQ, COL_DAK, COL_DAV = 0, 1024, 2048
COL_MERGE = 3072
COL_NSAQ = 5120
COL_KV6 = 5632
COL_NSAG = 6400
SLAB_COLS = 6528


SUBLANES = 8
SC_WINDOW = 128
SC_SPLIT = 2
SC_PIECE = 256

POS_SPLIT = 16
POS_ROWS = 16
ONES_ROWS = 16


def _pos_features(rows):
    r = np.arange(rows)
    f = np.zeros((rows, 128), np.float32)
    f[:, 0:3] = (r // POS_SPLIT)[:, None]
    f[:, 3:6] = (r % POS_SPLIT)[:, None]
    return jnp.asarray(f, BF16)


def _slope_rows(slopes):
    s = np.asarray(slopes, np.float32)
    bf = lambda x: x.astype(BF16).astype(np.float32)
    p1 = bf(s)
    p2 = bf(s - p1)
    p3 = bf(s - p1 - p2)
    out = np.zeros((s.shape[0], POS_ROWS, 128), np.float32)
    for i, piece in enumerate((p1, p2, p3)):
        out[:, i, :] = POS_SPLIT * piece[:, None]
        out[:, 3 + i, :] = piece[:, None]
    return jnp.asarray(out)


def _with_ones_row(vT):
    shape = vT.shape[:-2] + (ONES_ROWS, vT.shape[-1])
    extra = jnp.zeros(shape, vT.dtype).at[..., 0, :].set(1.0)
    return jnp.concatenate([vT, extra], axis=-2)


def _cparams(sem, vmem=VMEM_LIMIT_V7X):
    return pltpu.CompilerParams(dimension_semantics=sem, vmem_limit_bytes=vmem)


def _ada_kernel(c_ref, w_ref, b_ref, o_ref):
    c = c_ref[...]
    ca = c * jax.nn.sigmoid(c)
    o_ref[...] = jnp.dot(ca, w_ref[...], preferred_element_type=F32, precision=HIGHEST) + b_ref[...]


def _ada(c, w, b):
    B, D = c.shape
    n_out = w.shape[1]
    rows = 8
    cp = jnp.zeros((rows, D), F32).at[:B].set(c)
    tn = 1024
    out = pl.pallas_call(
        _ada_kernel,
        grid=(n_out // tn,),
        in_specs=[pl.BlockSpec((rows, D), lambda j: (0, 0)),
                  pl.BlockSpec((D, tn), lambda j: (0, j)),
                  pl.BlockSpec((1, tn), lambda j: (0, j))],
        out_specs=pl.BlockSpec((rows, tn), lambda j: (0, j)),
        out_shape=jax.ShapeDtypeStruct((rows, n_out), F32),
        compiler_params=_cparams(("arbitrary",)),
        name="ada",
    )(cp, w, b.reshape(1, n_out))
    return out[:B]


def _modulate(x, nw, sh, sc):
    r = lax.rsqrt(jnp.mean(x * x, axis=-1, keepdims=True) + NORM_EPS)
    return (x * r) * nw * (1.0 + sc) + sh


def _inproj_kernel(x_ref, nw_ref, sh_ref, sc_ref, w_ref, o_ref, h_ref):
    @pl.when(pl.program_id(1) == 0)
    def _():
        h_ref[...] = _modulate(x_ref[...], nw_ref[...], sh_ref[...], sc_ref[...]).astype(BF16)

    o_ref[...] = jnp.dot(h_ref[...], w_ref[...], preferred_element_type=F32).astype(BF16)


def _inproj(x2, nw, sh, sc, w_slab, S):
    N, D = x2.shape
    tm = min(1024, S)
    tn = SLAB_COLS // 3
    per_b = S // tm
    return pl.pallas_call(
        _inproj_kernel,
        grid=(N // tm, SLAB_COLS // tn),
        in_specs=[pl.BlockSpec((tm, D), lambda i, j: (i, 0)),
                  pl.BlockSpec((1, D), lambda i, j: (0, 0)),
                  pl.BlockSpec((None, 1, D), lambda i, j: (i // per_b, 0, 0)),
                  pl.BlockSpec((None, 1, D), lambda i, j: (i // per_b, 0, 0)),
                  pl.BlockSpec((D, tn), lambda i, j: (0, j))],
        out_specs=pl.BlockSpec((tm, tn), lambda i, j: (i, j)),
        out_shape=jax.ShapeDtypeStruct((N, SLAB_COLS), BF16),
        scratch_shapes=[pltpu.VMEM((tm, D), BF16)],
        compiler_params=_cparams(("parallel", "arbitrary")),
        name="inproj",
    )(x2, nw.reshape(1, D), sh.reshape(-1, 1, D), sc.reshape(-1, 1, D), w_slab)


def _cmp_kernel(r_ref, pe_ref, w1a_ref, w1b_ref, w2_ref, o_ref):
    r = r_ref[...]
    n_rows = r.shape[0]
    a = jnp.dot(r, w1a_ref[...], preferred_element_type=F32)
    b = jnp.dot(r, w1b_ref[...], preferred_element_type=F32)
    pe = pe_ref[...]
    half = pe.shape[1] // 2
    bias = (jnp.dot(pe[:, :half], w1a_ref[...].astype(F32), preferred_element_type=F32, precision=HIGHEST)
            + jnp.dot(pe[:, half:], w1b_ref[...].astype(F32), preferred_element_type=F32, precision=HIGHEST))
    hid = a + pltpu.roll(b, n_rows - 1, 0) + bias
    act = 0.5 * hid * (1.0 + jnp.tanh(0.7978845608028654 * (hid + 0.044715 * hid * hid * hid)))
    o_ref[...] = jnp.dot(act, w2_ref[...], preferred_element_type=F32, precision=HIGHEST)


def _compress(rows, pe, w1, w2):
    BG, n_rows, width = rows.shape
    hidden = w1.shape[1]
    w1a = w1[:width].astype(BF16)
    w1b = w1[width:].astype(BF16)
    return pl.pallas_call(
        _cmp_kernel,
        grid=(BG,),
        in_specs=[pl.BlockSpec((None, n_rows, width), lambda i: (i, 0, 0)),
                  pl.BlockSpec((1, 2 * width), lambda i: (0, 0)),
                  pl.BlockSpec((width, hidden), lambda i: (0, 0)),
                  pl.BlockSpec((width, hidden), lambda i: (0, 0)),
                  pl.BlockSpec((hidden, NSA_DIM), lambda i: (0, 0))],
        out_specs=pl.BlockSpec((None, n_rows, NSA_DIM), lambda i: (i, 0, 0)),
        out_shape=jax.ShapeDtypeStruct((BG, n_rows, NSA_DIM), F32),
        compiler_params=_cparams(("parallel",)),
        name="cmp",
    )(rows, pe.reshape(1, -1).astype(F32), w1a, w1b, w2.astype(F32))


def _da_kernel(slopes_ref, kmin_ref, q_ref, k_ref, vT_ref, pos_ref, rel_ref, srow_ref, lq1_ref, lk1_ref, lq2_ref,
               lk2_ref, subw_ref, o_ref, acc_ref, qa_ref, sa_ref, sb_ref, pa_ref, pb_ref, *, tq, tk, lam_init):
    h = pl.program_id(1)
    qi = pl.program_id(2)
    slope = slopes_ref[h]
    q0 = qi * tq
    w = 2 * tq
    nk = k_ref.shape[0] // tk
    qT = (q_ref[...].astype(F32) * DA_DIM ** -0.5).T.astype(BF16)
    row = lax.broadcasted_iota(I32, qT.shape, 0)
    zero = jnp.zeros_like(qT)
    qa_ref[0:2 * DA_DIM, :] = jnp.concatenate(
        [jnp.where(row < DA_DIM, qT, zero), jnp.where(row >= DA_DIM, qT, zero)], axis=1)
    qa_ref[2 * DA_DIM:2 * DA_DIM + POS_ROWS, :] = jnp.concatenate([srow_ref[...]] * (w // 128), axis=1).astype(BF16)
    qa_ref[2 * DA_DIM + POS_ROWS:, :] = jnp.zeros((2 * DA_DIM - POS_ROWS, w), BF16)

    def scores(kt, s_ref):
        k0 = kt * tk
        k_t = k_ref[pl.ds(pl.multiple_of(k0, tk), tk), :]
        s = jnp.dot(jnp.concatenate([k_t, pos_ref[...]], axis=1), qa_ref[...], preferred_element_type=F32)
        s = jnp.where(rel_ref[...] <= q0 - k0, s, NEG)
        s_ref[...] = s
        return jnp.max(s, axis=0, keepdims=True)

    def probs(s_ref, p_ref, mx, off, m_old):
        m_new = jnp.maximum(m_old, mx + off)
        p_ref[...] = jnp.exp(s_ref[...] - (m_new - off)).astype(BF16)
        return m_new, jnp.exp(m_old - m_new)

    def accumulate(kt, p_ref, alpha):
        acc_ref[...] = alpha * acc_ref[...] + jnp.dot(vT_ref[kt], p_ref[...], preferred_element_type=F32)

    n_tiles = (q0 + tq - 1) // tk + 1
    kt_min = kmin_ref[(pl.program_id(0) * pl.num_programs(1) + h) * pl.num_programs(2) + qi]
    acc_ref[...] = jnp.zeros_like(acc_ref)
    pb_ref[...] = jnp.zeros_like(pb_ref)
    mx_a = scores(kt_min, sa_ref)

    def pair(j, carry):
        m, mx_a, alpha_b = carry
        ta = kt_min + 2 * j
        tb = ta + 1
        off_a = slope * (ta * tk).astype(F32)
        off_b = jnp.where(tb < n_tiles, slope * (tb * tk).astype(F32), NEG)
        accumulate(jnp.maximum(ta - 1, 0), pb_ref, alpha_b)
        m, alpha_a = probs(sa_ref, pa_ref, mx_a, off_a, m)
        mx_b = scores(jnp.minimum(tb, nk - 1), sb_ref)
        accumulate(ta, pa_ref, alpha_a)
        m, alpha_b = probs(sb_ref, pb_ref, mx_b, off_b, m)
        mx_a = scores(jnp.minimum(ta + 2, nk - 1), sa_ref)
        return m, mx_a, alpha_b

    n_pairs = (n_tiles - kt_min + 1) // 2
    init = (jnp.full((1, w), NEG, F32), mx_a, jnp.ones((1, w), F32))
    _, _, alpha_b = lax.fori_loop(0, n_pairs, pair, init)
    accumulate(jnp.minimum(kt_min + 2 * n_pairs - 1, nk - 1), pb_ref, alpha_b)

    lam = (jnp.exp(jnp.sum(lq1_ref[...] * lk1_ref[...], axis=-1, keepdims=True))
           - jnp.exp(jnp.sum(lq2_ref[...] * lk2_ref[...], axis=-1, keepdims=True)) + lam_init)
    on = acc_ref[0:DA_VDIM, :] * (1.0 / acc_ref[DA_VDIM:DA_VDIM + 1, :])
    o = on[:, :tq] - lam * on[:, tq:]
    r = lax.rsqrt(jnp.mean(o * o, axis=0, keepdims=True) + DA_SUBLN_EPS)
    o_ref[...] = (o * r * subw_ref[...] * (1.0 - lam_init)).T.astype(BF16)


EXP_UNDERFLOW = 104.0


def _first_live_tile(slopes_np, slab3, tq, tk):
    B, S, _ = slab3.shape
    H = DA_HEADS
    width = 2 * DA_DIM * H
    grp = jnp.asarray((np.arange(width)[:, None] // DA_DIM == np.arange(2 * H)[None, :]).astype(np.float32), BF16)

    def norms(c0):
        v = slab3[:, :, c0:c0 + width]
        return jnp.sqrt(jnp.einsum('bsc,cg->bsg', v * v, grp, preferred_element_type=F32))

    qn = norms(COL_DAQ) * DA_DIM ** -0.5
    qn = jnp.max(qn.reshape(B, S // tq, tq, H, 2), axis=2).transpose(0, 2, 3, 1)
    kn = jnp.max(norms(COL_DAK), axis=1).reshape(B, H, 2)
    bound = jnp.max(qn * kn[..., None], axis=2) * 1.01 + 0.01
    q0 = (jnp.arange(S // tq, dtype=F32) * tq)[None, None, :]
    slope = jnp.asarray(slopes_np, F32)[None, :, None]
    first_key = q0 - (EXP_UNDERFLOW + 2.0 * bound) / slope
    kt = jnp.floor(first_key / tk).astype(I32)
    n_tiles = (jnp.arange(S // tq, dtype=I32) * tq + tq - 1) // tk + 1
    return jnp.clip(kt, 0, n_tiles[None, None, :] - 1).reshape(-1)


def _diff_attention(slopes, kmin, slab3, vT, srows, lq1, lk1, lq2, lk2, subw, lam_init, tq, tk):
    B, S, _ = slab3.shape
    nk = S // tk
    assert tk % tq == 0 and S % tk == 0
    qcol = COL_DAQ // 128
    tile = tq
    kcol = COL_DAK // 128
    v_rows = vT.shape[3]
    vec = lambda a: a.reshape(1, DA_DIM).astype(F32)
    grid_spec = pltpu.PrefetchScalarGridSpec(
        num_scalar_prefetch=2,
        grid=(B, DA_HEADS, S // tile),
        in_specs=[pl.BlockSpec((None, tile, 128), lambda b, h, i, s, km: (b, i, qcol + h)),
                  pl.BlockSpec((None, S, 128), lambda b, h, i, s, km: (b, 0, kcol + h)),
                  pl.BlockSpec((None, None, nk, v_rows, tk), lambda b, h, i, s, km: (b, h, 0, 0, 0)),
                  pl.BlockSpec((tk, 128), lambda b, h, i, s, km: (0, 0)),
                  pl.BlockSpec((tk, 2 * tq), lambda b, h, i, s, km: (0, 0)),
                  pl.BlockSpec((None, POS_ROWS, 128), lambda b, h, i, s, km: (h, 0, 0)),
                  pl.BlockSpec((1, DA_DIM), lambda b, h, i, s, km: (0, 0)),
                  pl.BlockSpec((1, DA_DIM), lambda b, h, i, s, km: (0, 0)),
                  pl.BlockSpec((1, DA_DIM), lambda b, h, i, s, km: (0, 0)),
                  pl.BlockSpec((1, DA_DIM), lambda b, h, i, s, km: (0, 0)),
                  pl.BlockSpec((DA_VDIM, 1), lambda b, h, i, s, km: (0, 0))],
        out_specs=pl.BlockSpec((None, tile, 128), lambda b, h, i, s, km: (b, i, h)),
        scratch_shapes=[pltpu.VMEM((v_rows, 2 * tq), F32), pltpu.VMEM((4 * DA_DIM, 2 * tq), BF16),
                        pltpu.VMEM((tk, 2 * tq), F32), pltpu.VMEM((tk, 2 * tq), F32),
                        pltpu.VMEM((tk, 2 * tq), BF16), pltpu.VMEM((tk, 2 * tq), BF16)],
    )
    rel = jnp.asarray(np.arange(tk)[:, None] - (np.arange(2 * tq) % tq)[None, :], I32)
    return pl.pallas_call(
        functools.partial(_da_kernel, tq=tq, tk=tk, lam_init=lam_init),
        grid_spec=grid_spec,
        out_shape=jax.ShapeDtypeStruct((B, S, DA_HEADS * DA_VDIM), BF16),
        compiler_params=_cparams(("parallel", "parallel", "arbitrary")),
        name="diffattn",
    )(slopes, kmin, slab3, slab3, vT, _pos_features(tk), rel, srows, vec(lq1), vec(lk1), vec(lq2), vec(lk2),
      subw.reshape(DA_VDIM, 1).astype(F32))


def _nsa_kernel(slopes_ref, qT_ref, kc_ref, vcT_ref, ksa_ref, vsT_ref, kwa_ref, vwT_ref, wb_ref, gT_ref, mT_ref,
                grp_ref, ltri_ref, oT_ref, acc_ref, res_ref, base_ref, qa_ref, sa_ref, sb_ref, pa_ref, pb_ref,
                list_ref, slist_ref, lsem, imp_ref, *, t, tk, n_top):
    g = pl.program_id(1)
    qi = pl.program_id(2)
    q0 = qi * t
    w = NSA_HPG * t
    n_cp = kc_ref.shape[0]
    n_sel = mT_ref.shape[0]
    lane1 = lax.broadcasted_iota(I32, (1, w), 1)
    slope_row = jnp.zeros((1, w), F32)
    for hh in range(NSA_HPG):
        slope_row = jnp.where((lane1 >= hh * t) & (lane1 < (hh + 1) * t), slopes_ref[g * NSA_HPG + hh], slope_row)
    q_all = qT_ref[...]
    gates = jax.nn.sigmoid(gT_ref[...])
    key_off = lax.broadcasted_iota(I32, (tk, w), 0)
    t_loc = jnp.bitwise_and(lax.broadcasted_iota(I32, (tk, w), 1), t - 1)
    base_ref[...] = slope_row * key_off.astype(F32)

    def compressed(rows):
        c_idx = lax.broadcasted_iota(I32, (rows, w), 0)
        t_c = q0 + jnp.bitwise_and(lax.broadcasted_iota(I32, (rows, w), 1), t - 1)
        d_c = t_c - (c_idx * CMP_STRIDE + (CMP_BLOCK - 1))
        ok_c = d_c >= 0
        s = jnp.dot(kc_ref[0:rows, :], q_all, preferred_element_type=F32) - slope_row * d_c.astype(F32)
        s = jnp.where(ok_c, s, NEG)
        e = jnp.exp(s - jnp.max(s, axis=0, keepdims=True))
        inv = 1.0 / jnp.sum(e, axis=0, keepdims=True)
        p = jnp.where(ok_c, e * inv, 0.0)
        res_ref[...] = gates[0:1, :] * jnp.dot(vcT_ref[:, 0:rows], p.astype(BF16), preferred_element_type=F32)
        p_sum = p[:, 0:t]
        for hh in range(1, NSA_HPG):
            p_sum = p_sum + p[:, hh * t:(hh + 1) * t]
        p_hi = p_sum.astype(BF16)
        p_lo = (p_sum - p_hi.astype(F32)).astype(BF16)
        mT = mT_ref[:, 0:rows]
        imp_ref[...] = jnp.dot(mT, p_hi, preferred_element_type=F32) + jnp.dot(mT, p_lo, preferred_element_type=F32)

    chunk = min(n_cp, 128)
    n_chunk = jnp.minimum((q0 + t - CMP_BLOCK) // (CMP_STRIDE * chunk) + 1, n_cp // chunk)
    for c in range(1, n_cp // chunk + 1):
        pl.when(n_chunk == c)(functools.partial(compressed, c * chunk))

    imp = imp_ref[...]
    blk = lax.broadcasted_iota(I32, (n_sel, t), 0)
    t_s = q0 + lax.broadcasted_iota(I32, (n_sel, t), 1)
    cur = jnp.right_shift(t_s, int(math.log2(SEL_BLOCK)))
    forced = (blk == 0) | (blk == cur) | (blk == cur - 1)
    score = jnp.where(blk * SEL_BLOCK <= t_s, imp + jnp.where(forced, FORCE_BONUS, 0.0), NEG)
    blk_f = blk.astype(F32)
    sel = jnp.zeros((n_sel, t), F32)
    for _ in range(n_top):
        _, idx = _first_max(score, blk_f)
        hit = blk_f == idx
        sel = jnp.where(hit, 1.0, sel)
        score = jnp.where(hit, -jnp.inf, score)
    n_full = q0 // tk
    nt_pad = grp_ref.shape[0]
    cnt = jnp.broadcast_to(jnp.sum(sel, axis=1, keepdims=True), (n_sel, 128)).astype(BF16)
    tile_cnt = jnp.dot(grp_ref[...], cnt, preferred_element_type=F32)
    kt_i = lax.broadcasted_iota(I32, (nt_pad, 128), 0)
    active = (tile_cnt > 0.5) & (kt_i < n_full)
    act_f = jnp.where(active, 1.0, 0.0)
    before = jnp.dot(ltri_ref[...], act_f.astype(BF16), preferred_element_type=F32)
    lane_j = lax.broadcasted_iota(I32, (nt_pad, 128), 1)
    slot_hit = active & (before == lane_j.astype(F32))
    tiles_row = jnp.sum(jnp.where(slot_hit, kt_i.astype(F32), 0.0), axis=0, keepdims=True)
    n_act_row = jnp.sum(act_f, axis=0, keepdims=True)
    lane8 = lax.broadcasted_iota(I32, (8, 128), 1)
    list_ref[...] = jnp.where(lane8 == 127, n_act_row, tiles_row).astype(I32)
    list_copy = pltpu.make_async_copy(list_ref, slist_ref, lsem)
    list_copy.start()

    selb = jnp.where(sel > 0.5, 0.0, NEG)
    row64 = lax.broadcasted_iota(I32, (NSA_DIM, w), 0)
    qa_ref[0:NSA_DIM, :] = q_all
    qa_ref[NSA_DIM:2 * NSA_DIM, :] = jnp.where(row64 == 0, NEG, 0.0).astype(BF16)
    qa_ref[2 * NSA_DIM:2 * NSA_DIM + n_sel, :] = jnp.concatenate([selb] * NSA_HPG, axis=1).astype(BF16)
    if n_sel < SEL_COLS:
        qa_ref[2 * NSA_DIM + n_sel:, :] = jnp.zeros((SEL_COLS - n_sel, w), BF16)

    n_wt = (WINDOW + t) // t
    kw_t = kwa_ref[pl.ds(pl.multiple_of(q0, t), WINDOW + t), :]
    s = jnp.dot(kw_t, qa_ref[0:2 * NSA_DIM, :], preferred_element_type=F32) + wb_ref[...]
    p = jnp.exp(s - jnp.max(s, axis=0, keepdims=True))
    inv = 1.0 / jnp.sum(p, axis=0, keepdims=True)
    p = p.astype(BF16)
    o_w = jnp.dot(vwT_ref[qi], p[0:t, :], preferred_element_type=F32)
    for c in range(1, n_wt):
        o_w = o_w + jnp.dot(vwT_ref[qi + c], p[c * t:(c + 1) * t, :], preferred_element_type=F32)
    res_ref[...] = res_ref[...] + gates[2:3, :] * (o_w * inv)

    def scores(kt, s_ref):
        k_t = ksa_ref[pl.ds(pl.multiple_of(kt * tk, tk), tk), :]
        s = jnp.dot(k_t, qa_ref[...], preferred_element_type=F32) + base_ref[...]
        s_ref[...] = s
        return jnp.max(s, axis=0, keepdims=True)

    def probs(s_ref, p_ref, mx, off, m_old, l_old):
        m_new = jnp.maximum(m_old, mx + off)
        p = jnp.exp(s_ref[...] - (m_new - off))
        alpha = jnp.exp(m_old - m_new)
        p_ref[...] = p.astype(BF16)
        return m_new, alpha * l_old + jnp.sum(p, axis=0, keepdims=True), alpha

    def accumulate(kt, p_ref, alpha):
        acc_ref[...] = alpha * acc_ref[...] + jnp.dot(vsT_ref[kt], p_ref[...], preferred_element_type=F32)

    list_copy.wait()
    n_act = slist_ref[0, 127]
    mx_a = scores(slist_ref[0, 0], sa_ref)

    kd = n_full * tk
    s = jnp.dot(ksa_ref[pl.ds(pl.multiple_of(kd, tk), tk), :], qa_ref[...], preferred_element_type=F32) + base_ref[...]
    s = jnp.where(key_off - t_loc <= q0 - kd, s, NEG)
    off_d = slope_row * kd.astype(F32)
    m = jnp.max(s, axis=0, keepdims=True) + off_d
    p = jnp.exp(s - (m - off_d))
    l = jnp.sum(p, axis=0, keepdims=True)
    acc_ref[...] = jnp.dot(vsT_ref[n_full], p.astype(BF16), preferred_element_type=F32)
    pb_ref[...] = jnp.zeros_like(pb_ref)

    def pair(j, carry):
        m, l, mx_a, alpha_b, tb_prev = carry
        ta = slist_ref[0, 2 * j]
        tb = slist_ref[0, 2 * j + 1]
        off_a = slope_row * (ta * tk).astype(F32)
        off_b = jnp.where(2 * j + 1 < n_act, slope_row * (tb * tk).astype(F32), NEG)
        accumulate(tb_prev, pb_ref, alpha_b)
        m, l, alpha_a = probs(sa_ref, pa_ref, mx_a, off_a, m, l)
        mx_b = scores(tb, sb_ref)
        accumulate(ta, pa_ref, alpha_a)
        m, l, alpha_b = probs(sb_ref, pb_ref, mx_b, off_b, m, l)
        mx_a = scores(slist_ref[0, 2 * j + 2], sa_ref)
        return m, l, mx_a, alpha_b, tb

    n_pairs = (n_act + 1) // 2
    m, l, _, alpha_b, tb_last = lax.fori_loop(0, n_pairs, pair, (m, l, mx_a, jnp.ones((1, w), F32), jnp.int32(0)))
    accumulate(tb_last, pb_ref, alpha_b)
    oT_ref[...] = (res_ref[...] + gates[1:2, :] * (acc_ref[...] * (1.0 / l))).astype(BF16)


def _nsa_attention(slopes, qT, kc, vcT, ksa, vsT, kwa, vwT, wb, gT, mT, t, tk):
    B, G, nq, _, w = qT.shape
    S = nq * t
    n_cp = kc.shape[2]
    n_sel = S // SEL_BLOCK
    n_top = min(SEL_TOPK, n_sel)
    assert tk % t == 0 and S % tk == 0 and t & (t - 1) == 0 and WINDOW % t == 0 and n_sel <= SEL_COLS
    n_tiles = S // tk
    nt_pad = -(-n_tiles // 16) * 16
    assert nt_pad + 2 < 127
    grp = jnp.asarray((np.arange(n_sel)[None, :] // (tk // SEL_BLOCK) == np.arange(nt_pad)[:, None])
                      .astype(np.float32), BF16)
    ltri = jnp.asarray((np.arange(nt_pad)[None, :] < np.arange(nt_pad)[:, None]).astype(np.float32), BF16)
    fixed = lambda b, g, i, s: (b, g, 0, 0)
    fixed5 = lambda b, g, i, s: (b, g, 0, 0, 0)
    tile5 = lambda b, g, i, s: (b, g, i, 0, 0)
    grid_spec = pltpu.PrefetchScalarGridSpec(
        num_scalar_prefetch=1,
        grid=(B, G, nq),
        in_specs=[pl.BlockSpec((None, None, None, NSA_DIM, w), tile5),
                  pl.BlockSpec((None, None, n_cp, NSA_DIM), fixed),
                  pl.BlockSpec((None, None, NSA_DIM, n_cp), fixed),
                  pl.BlockSpec((None, None, S, ksa.shape[-1]), fixed),
                  pl.BlockSpec((None, None, S // tk, NSA_DIM, tk), fixed5),
                  pl.BlockSpec((None, None, S + WINDOW, 2 * NSA_DIM), fixed),
                  pl.BlockSpec((None, None, (S + WINDOW) // t, NSA_DIM, t), fixed5),
                  pl.BlockSpec((None, WINDOW + t, w), lambda b, g, i, s: (g, 0, 0)),
                  pl.BlockSpec((None, None, None, 3, w), tile5),
                  pl.BlockSpec((n_sel, n_cp), lambda b, g, i, s: (0, 0)),
                  pl.BlockSpec((nt_pad, n_sel), lambda b, g, i, s: (0, 0)),
                  pl.BlockSpec((nt_pad, nt_pad), lambda b, g, i, s: (0, 0))],
        out_specs=pl.BlockSpec((None, None, None, NSA_DIM, w), tile5),
        scratch_shapes=[pltpu.VMEM((NSA_DIM, w), F32),
                        pltpu.VMEM((NSA_DIM, w), F32),
                        pltpu.VMEM((tk, w), F32),
                        pltpu.VMEM((2 * NSA_DIM + SEL_COLS, w), BF16),
                        pltpu.VMEM((tk, w), F32), pltpu.VMEM((tk, w), F32),
                        pltpu.VMEM((tk, w), BF16), pltpu.VMEM((tk, w), BF16),
                        pltpu.VMEM((8, 128), I32), pltpu.SMEM((8, 128), I32), pltpu.SemaphoreType.DMA(()),
                        pltpu.VMEM((n_sel, t), F32)],
    )
    return pl.pallas_call(
        functools.partial(_nsa_kernel, t=t, tk=tk, n_top=n_top),
        grid_spec=grid_spec,
        out_shape=jax.ShapeDtypeStruct((B, G, nq, NSA_DIM, w), BF16),
        compiler_params=_cparams(("parallel", "parallel", "arbitrary")),
        name="nsa",
    )(slopes, qT, kc, vcT, ksa, vsT, kwa, vwT, wb, gT, mT, grp, ltri)


def _pack_halves(v):
    c = v.shape[1] // 2
    lo = lax.bitcast_convert_type(v[:, :c].astype(BF16).astype(F32), U32)
    hi = lax.bitcast_convert_type(v[:, c:].astype(BF16).astype(F32), U32)
    return hi | (lo >> 16)


def _unpack_halves(u):
    lo = lax.bitcast_convert_type(u << 16, F32)
    hi = lax.bitcast_convert_type(u & jnp.uint32(0xFFFF0000), F32)
    return lo, hi


def _store_slabs(ref, v):
    packed = _pack_halves(v)
    for s in range(SC_SPLIT):
        ref[s] = packed[:, s * SC_PIECE:(s + 1) * SC_PIECE]


def _load_slabs_bf16(ref):
    parts = [_unpack_halves(ref[s]) for s in range(SC_SPLIT)]
    return jnp.concatenate([p[0].astype(BF16) for p in parts] + [p[1].astype(BF16) for p in parts], axis=1)


def _merge_kernel(oa_ref, ob_ref, ga_ref, gb_ref, x_ref, g1_ref, nw_ref, sh_ref, sc_ref,
                  wa_ref, wb_ref, wo_ref, wr_ref, x1_ref, h2_ref, lg_ref):
    ya = jnp.dot(oa_ref[...], wa_ref[...], preferred_element_type=F32)
    yb = jnp.dot(ob_ref[...], wb_ref[...], preferred_element_type=F32)
    merged = (jax.nn.sigmoid(ga_ref[...].astype(F32)) * ya
              + jax.nn.sigmoid(gb_ref[...].astype(F32)) * yb).astype(BF16)
    mix = jnp.dot(merged, wo_ref[...], preferred_element_type=F32)
    x1 = x_ref[...] + g1_ref[...] * mix
    x1_ref[...] = x1
    h2 = _modulate(x1, nw_ref[...], sh_ref[...], sc_ref[...])
    _store_slabs(h2_ref, h2)
    h_hi = h2.astype(BF16)
    h_lo = (h2 - h_hi.astype(F32)).astype(BF16)
    lg_ref[...] = (jnp.dot(h_hi, wr_ref[0], preferred_element_type=F32)
                   + jnp.dot(h_lo, wr_ref[0], preferred_element_type=F32)
                   + jnp.dot(h_hi, wr_ref[1], preferred_element_type=F32))


def _merge(oa, ob, slab, x2, g1, nw, sh, sc, wa, wb, wo, wr, S):
    N, D = x2.shape
    tm = 256
    per_b = S // tm
    mcol = COL_MERGE // D
    row = lambda i: (i, 0)
    full = lambda i: (0, 0)
    perb = lambda i: (i // per_b, 0, 0)
    n_lg = wr.shape[-1]
    return pl.pallas_call(
        _merge_kernel,
        grid=(N // tm,),
        in_specs=[pl.BlockSpec((tm, oa.shape[1]), row),
                  pl.BlockSpec((tm, ob.shape[1]), row),
                  pl.BlockSpec((tm, D), lambda i: (i, mcol)),
                  pl.BlockSpec((tm, D), lambda i: (i, mcol + 1)),
                  pl.BlockSpec((tm, D), row),
                  pl.BlockSpec((None, 1, D), perb),
                  pl.BlockSpec((1, D), full),
                  pl.BlockSpec((None, 1, D), perb),
                  pl.BlockSpec((None, 1, D), perb),
                  pl.BlockSpec(wa.shape, full),
                  pl.BlockSpec(wb.shape, full),
                  pl.BlockSpec(wo.shape, full),
                  pl.BlockSpec(wr.shape, lambda i: (0, 0, 0))],
        out_specs=[pl.BlockSpec((tm, D), row), pl.BlockSpec((SC_SPLIT, tm, SC_PIECE), lambda i: (0, i, 0)),
                   pl.BlockSpec((tm, n_lg), row)],
        out_shape=[jax.ShapeDtypeStruct((N, D), F32), jax.ShapeDtypeStruct((SC_SPLIT, N, SC_PIECE), U32),
                   jax.ShapeDtypeStruct((N, n_lg), F32)],
        compiler_params=_cparams(("parallel",)),
        name="merge",
    )(oa, ob, slab, slab, x2, g1.reshape(-1, 1, D), nw.reshape(1, D), sh.reshape(-1, 1, D),
      sc.reshape(-1, 1, D), wa, wb, wo, wr)


def _first_max(v, idx):
    mx = jnp.max(v, axis=0, keepdims=True)
    first = jnp.min(jnp.where(v == mx, idx, float(v.shape[0])), axis=0, keepdims=True)
    return mx, first


def _route_kernel(lg_ref, b_ref, e_ref, w_ref):
    lg = lg_ref[...]
    tn = lg.shape[1]
    per_g = N_EXPERTS // N_EXPERT_GROUPS
    scores = jax.nn.sigmoid(lg)
    biased = scores + b_ref[...]
    gi = lax.broadcasted_iota(I32, (N_EXPERT_GROUPS, tn), 0).astype(F32)
    gscore = jnp.zeros((N_EXPERT_GROUPS, tn), F32)
    for g in range(N_EXPERT_GROUPS):
        grp = biased[g * per_g:(g + 1) * per_g, :]
        m1, f1 = _first_max(grp, gi)
        m2 = jnp.max(jnp.where(gi == f1, -jnp.inf, grp), axis=0, keepdims=True)
        gscore = jnp.where(gi == g, m1 + m2, gscore)
    gsel = jnp.zeros((N_EXPERT_GROUPS, tn), F32)
    for _ in range(TOPK_EXPERT_GROUPS):
        _, f = _first_max(gscore, gi)
        hit = gi == f
        gsel = jnp.where(hit, 1.0, gsel)
        gscore = jnp.where(hit, -jnp.inf, gscore)
    ei = lax.broadcasted_iota(I32, (N_EXPERTS, tn), 0).astype(F32)
    emask = jnp.zeros((N_EXPERTS, tn), F32)
    for g in range(N_EXPERT_GROUPS):
        in_g = (ei >= g * per_g) & (ei < (g + 1) * per_g)
        emask = jnp.where(in_g, gsel[g:g + 1, :], emask)
    masked = jnp.where(emask > 0.5, biased, NEG)
    e_out = jnp.zeros((MOE_TOPK, tn), F32)
    w_out = jnp.zeros((MOE_TOPK, tn), F32)
    for r in range(MOE_TOPK):
        _, f = _first_max(masked, ei)
        hit = ei == f
        wv = jnp.sum(jnp.where(hit, scores, 0.0), axis=0, keepdims=True)
        e_out = jnp.where(gi == r, f, e_out)
        w_out = jnp.where(gi == r, wv, w_out)
        masked = jnp.where(hit, -jnp.inf, masked)
    e_ref[...] = e_out.astype(I32)
    w_ref[...] = w_out / jnp.sum(w_out, axis=0, keepdims=True) * ROUTED_SCALE


def _route(lgT, rb):
    E, N = lgT.shape
    tn = 512
    return pl.pallas_call(
        _route_kernel,
        grid=(N // tn,),
        in_specs=[pl.BlockSpec((E, tn), lambda i: (0, i)), pl.BlockSpec((E, 1), lambda i: (0, 0))],
        out_specs=[pl.BlockSpec((MOE_TOPK, tn), lambda i: (0, i)), pl.BlockSpec((MOE_TOPK, tn), lambda i: (0, i))],
        out_shape=[jax.ShapeDtypeStruct((MOE_TOPK, N), I32), jax.ShapeDtypeStruct((MOE_TOPK, N), F32)],
        compiler_params=_cparams(("parallel",)),
        name="route",
    )(lgT, rb.reshape(E, 1).astype(F32))


def _slots_kernel(e_ref, tri_ref, lt_ref, dest_ref, cnt_ref, carry_ref, *, rb):
    phase = pl.program_id(0)
    i = pl.program_id(1)
    e = e_ref[...]
    tn = e.shape[1]
    ei = lax.broadcasted_iota(I32, (N_EXPERTS, tn), 0)

    @pl.when((phase == 0) & (i == 0))
    def _():
        carry_ref[...] = jnp.zeros_like(carry_ref)

    @pl.when(phase == 0)
    def _():
        tot = jnp.zeros((N_EXPERTS, 1), F32)
        for k in range(MOE_TOPK):
            oh = jnp.where(ei == e[k:k + 1, :], 1.0, 0.0)
            tot = tot + jnp.sum(oh, axis=1, keepdims=True)
        carry_ref[...] = carry_ref[...] + tot
        dest_ref[...] = jnp.zeros_like(dest_ref)

    @pl.when((phase == 1) & (i == 0))
    def _():
        cnt = carry_ref[...]
        cnt_ref[...] = cnt
        padded = jnp.broadcast_to(jnp.ceil(cnt / rb) * rb, (N_EXPERTS, 128))
        first = jnp.dot(lt_ref[...], padded, preferred_element_type=F32, precision=HIGHEST)
        carry_ref[...] = first[:, :1]

    @pl.when(phase == 1)
    def _():
        carry = carry_ref[...]
        ki = lax.broadcasted_iota(I32, (MOE_TOPK, tn), 0)
        dest = jnp.zeros((MOE_TOPK, tn), F32)
        for k in range(MOE_TOPK):
            hit = ei == e[k:k + 1, :]
            oh = jnp.where(hit, 1.0, 0.0)
            before = jnp.dot(oh.astype(BF16), tri_ref[...], preferred_element_type=F32)
            row = jnp.sum(jnp.where(hit, before + carry, 0.0), axis=0, keepdims=True)
            dest = jnp.where(ki == k, row, dest)
            carry = carry + jnp.sum(oh, axis=1, keepdims=True)
        carry_ref[...] = carry
        dest_ref[...] = dest.astype(I32)


def _slots(top_eT, rb):
    K, N = top_eT.shape
    tn = 512
    idx = np.arange(tn)
    tri = jnp.asarray((idx[:, None] < idx[None, :]).astype(np.float32), BF16)
    e_idx = np.arange(N_EXPERTS)
    lt = jnp.asarray((e_idx[None, :] < e_idx[:, None]).astype(np.float32), F32)
    return pl.pallas_call(
        functools.partial(_slots_kernel, rb=rb),
        grid=(2, N // tn),
        in_specs=[pl.BlockSpec((K, tn), lambda p, i: (0, i)),
                  pl.BlockSpec((tn, tn), lambda p, i: (0, 0)),
                  pl.BlockSpec((N_EXPERTS, N_EXPERTS), lambda p, i: (0, 0))],
        out_specs=[pl.BlockSpec((K, tn), lambda p, i: (0, i * p)),
                   pl.BlockSpec((N_EXPERTS, 1), lambda p, i: (0, 0))],
        out_shape=[jax.ShapeDtypeStruct((K, N), I32), jax.ShapeDtypeStruct((N_EXPERTS, 1), F32)],
        scratch_shapes=[pltpu.VMEM((N_EXPERTS, 1), F32)],
        compiler_params=_cparams(("arbitrary", "arbitrary")),
        name="slots",
    )(top_eT, tri, lt)


def _sc_scatter_rows(src, idx, n_out, repeat):
    width = src.shape[1]
    n_tok = src.shape[0] // SC_SPLIT
    w = n_tok // SC_WINDOW
    mesh = plsc.VectorSubcoreMesh(core_axis_name="c", subcore_axis_name="s")

    @pl.kernel(out_type=jax.ShapeDtypeStruct((n_out, width), src.dtype), mesh=mesh)
    def scatter_kernel(x_hbm, i_hbm, o_hbm):
        def body(x_vmem, i_vmem):
            for k in range(repeat):
                pltpu.sync_copy(x_vmem, o_hbm.at[i_vmem.at[k]])

        pltpu.emit_pipeline(
            body,
            grid=(SC_SPLIT * w,),
            in_specs=[pl.BlockSpec((SC_WINDOW, width), lambda g: (g, 0)),
                      pl.BlockSpec((repeat, SC_WINDOW), lambda g: (g // w, g % w))],
            out_specs=[],
            core_axis_name=("c", "s"),
            dimension_semantics=(pltpu.PARALLEL,),
        )(x_hbm, i_hbm)

    return scatter_kernel(src, idx.reshape(SC_SPLIT * repeat, n_tok))


def _zero_pads_kernel(start_ref, len_ref, xs_in_ref, xs_ref, zeros_ref, sem, *, rb, n_rows):
    del xs_in_ref
    zeros_ref[...] = jnp.zeros_like(zeros_ref)
    n_seg = start_ref.shape[0]
    bits = rb.bit_length() - 1
    assert rb == 1 << bits

    def for_each_copy(act):
        def rows(row0, size):
            for s in range(SC_SPLIT):
                act(pltpu.make_async_copy(zeros_ref.at[pl.ds(0, size), :],
                                          xs_ref.at[pl.ds(s * n_rows + row0, size), :], sem))

        def segment(e, c):
            start = start_ref[e]
            length = len_ref[e]
            end = start + length
            n_whole = length // rb

            def whole(j, c2):
                rows(pl.multiple_of(end - (j + 1) * rb, SUBLANES), rb)
                return c2

            lax.fori_loop(0, n_whole, whole, 0)
            off = end - n_whole * rb
            rest = length - n_whole * rb
            for b in reversed(range(SUBLANES.bit_length() - 1, bits)):
                size = 1 << b
                take = jnp.bitwise_and(jnp.right_shift(rest, b), 1)
                off = off - take * size

                @pl.when(take == 1)
                def _(off=off, size=size):
                    rows(pl.multiple_of(off, SUBLANES), size)

            lead = jnp.bitwise_and(rest, SUBLANES - 1)
            for i in range(SUBLANES - 1):
                @pl.when(i < lead)
                def _(i=i):
                    rows(start + i, 1)

            return c

        lax.fori_loop(0, n_seg, segment, 0)

    for_each_copy(lambda cp: cp.start())
    for_each_copy(lambda cp: cp.wait())


def _zero_pads(xs2d, start, length, rb, n_rows):
    grid_spec = pltpu.PrefetchScalarGridSpec(
        num_scalar_prefetch=2,
        grid=(1,),
        in_specs=[pl.BlockSpec(memory_space=pl.ANY)],
        out_specs=pl.BlockSpec(memory_space=pl.ANY),
        scratch_shapes=[pltpu.VMEM((rb, xs2d.shape[1]), xs2d.dtype), pltpu.SemaphoreType.DMA(())],
    )
    return pl.pallas_call(
        functools.partial(_zero_pads_kernel, rb=rb, n_rows=n_rows),
        grid_spec=grid_spec,
        out_shape=jax.ShapeDtypeStruct(xs2d.shape, xs2d.dtype),
        input_output_aliases={2: 0},
        compiler_params=_cparams(("arbitrary",)),
        name="zeropads",
    )(start, length, xs2d)


def _experts_kernel(be_ref, x_ref, wg_ref, wu_ref, wd_ref, y_ref, wgu_sc, wd_sc):
    i = pl.program_id(0)
    hdim = wg_ref.shape[1]

    @pl.when((i == 0) | (be_ref[i] != be_ref[jnp.maximum(i - 1, 0)]))
    def _():
        wgu_sc[:, 0:hdim] = wg_ref[...].astype(BF16)
        wgu_sc[:, hdim:] = wu_ref[...].astype(BF16)
        wd_sc[...] = wd_ref[...].astype(BF16)

    @pl.when(i < be_ref[pl.num_programs(0)])
    def _():
        gu = jnp.dot(_load_slabs_bf16(x_ref), wgu_sc[...], preferred_element_type=F32)
        g = gu[:, :hdim]
        act = (g * jax.nn.sigmoid(g) * gu[:, hdim:]).astype(BF16)
        _store_slabs(y_ref, jnp.dot(act, wd_sc[...], preferred_element_type=F32))


def _experts(blk_e, xs, wg, wu, wd, rb):
    n_rows = xs.shape[1]
    d_model, hdim = wg.shape[1:]
    per_expert = lambda i, be: (be[i], 0, 0)
    grid_spec = pltpu.PrefetchScalarGridSpec(
        num_scalar_prefetch=1,
        grid=(n_rows // rb,),
        in_specs=[pl.BlockSpec((SC_SPLIT, rb, SC_PIECE), lambda i, be: (0, i, 0)),
                  pl.BlockSpec((None, d_model, hdim), per_expert),
                  pl.BlockSpec((None, d_model, hdim), per_expert),
                  pl.BlockSpec((None, hdim, d_model), per_expert)],
        out_specs=pl.BlockSpec((SC_SPLIT, rb, SC_PIECE), lambda i, be: (0, i, 0)),
        scratch_shapes=[pltpu.VMEM((d_model, 2 * hdim), BF16), pltpu.VMEM((hdim, d_model), BF16)],
    )
    return pl.pallas_call(
        _experts_kernel,
        grid_spec=grid_spec,
        out_shape=jax.ShapeDtypeStruct((SC_SPLIT, n_rows, SC_PIECE), U32),
        compiler_params=_cparams(("arbitrary",)),
        name="experts",
    )(blk_e, xs, wg, wu, wd)


def _sc_gather_rows(table, idx):
    n = idx.shape[0]
    width = table.shape[1]
    mesh = plsc.VectorSubcoreMesh(core_axis_name="c", subcore_axis_name="s")

    @pl.kernel(out_type=jax.ShapeDtypeStruct((n, width), table.dtype), mesh=mesh)
    def gather_kernel(t_hbm, i_hbm, o_hbm):
        def body(i_vmem, o_vmem):
            pltpu.sync_copy(t_hbm.at[i_vmem.at[0]], o_vmem)

        pltpu.emit_pipeline(
            body,
            grid=(n // SC_WINDOW,),
            in_specs=[pl.BlockSpec((1, SC_WINDOW), lambda i: (0, i))],
            out_specs=[pl.BlockSpec((SC_WINDOW, width), lambda i: (i, 0))],
            core_axis_name=("c", "s"),
            dimension_semantics=(pltpu.PARALLEL,),
        )(i_hbm, o_hbm)

    return gather_kernel(table, idx.reshape(1, n))


def _combine_kernel(yg_ref, w_ref, h_ref, x1_ref, g2_ref, fw_ref, sgu_ref, sd_ref, o_ref):
    gu = jnp.dot(_load_slabs_bf16(h_ref), sgu_ref[...], preferred_element_type=F32)
    hdim = gu.shape[1] // 2
    g = gu[:, :hdim]
    ffn = jnp.dot((g * jax.nn.sigmoid(g) * gu[:, hdim:]).astype(BF16), sd_ref[...], preferred_element_type=F32)

    w = w_ref[...]
    tt = w.shape[0]
    lo = [jnp.zeros((tt, SC_PIECE), F32) for _ in range(SC_SPLIT)]
    hi = [jnp.zeros((tt, SC_PIECE), F32) for _ in range(SC_SPLIT)]
    for k in range(MOE_TOPK):
        wk = w[:, k:k + 1]
        for s in range(SC_SPLIT):
            y_lo, y_hi = _unpack_halves(yg_ref[s, k])
            lo[s] = lo[s] + wk * y_lo
            hi[s] = hi[s] + wk * y_hi
    ffn = ffn + jnp.concatenate(lo + hi, axis=1)
    x2 = x1_ref[...] + g2_ref[...] * ffn
    r = lax.rsqrt(jnp.mean(x2 * x2, axis=-1, keepdims=True) + NORM_EPS)
    o_ref[...] = x2 * r * fw_ref[...]


def _combine(yg, top_w, h2p, x1, g2, fw, sgu, sd, S, tt):
    N, D = x1.shape
    per_b = S // tt
    row = lambda i: (i, 0)
    full = lambda i: (0, 0)
    return pl.pallas_call(
        _combine_kernel,
        grid=(N // tt,),
        in_specs=[pl.BlockSpec((SC_SPLIT, MOE_TOPK, tt, SC_PIECE), lambda i: (0, 0, i, 0)),
                  pl.BlockSpec((tt, MOE_TOPK), row),
                  pl.BlockSpec((SC_SPLIT, tt, SC_PIECE), lambda i: (0, i, 0)),
                  pl.BlockSpec((tt, D), row),
                  pl.BlockSpec((None, 1, D), lambda i: (i // per_b, 0, 0)),
                  pl.BlockSpec((1, D), full),
                  pl.BlockSpec(sgu.shape, full),
                  pl.BlockSpec(sd.shape, full)],
        out_specs=pl.BlockSpec((tt, D), row),
        out_shape=jax.ShapeDtypeStruct((N, D), F32),
        compiler_params=_cparams(("parallel",)),
        name="combine",
    )(yg, top_w, h2p, x1, g2.reshape(-1, 1, D), fw.reshape(1, D), sgu, sd)


def _cmp_to_sel_T(n_cp, n_sel):
    c0 = np.arange(n_cp)[None, :] * CMP_STRIDE
    s0 = np.arange(n_sel)[:, None] * SEL_BLOCK
    ov = np.minimum(c0 + CMP_BLOCK, s0 + SEL_BLOCK) - np.maximum(c0, s0)
    m = np.clip(ov, 0, None).astype(np.float32) / CMP_BLOCK
    m[:, n_cp - 1] = 0.0
    return jnp.asarray(m, BF16)


def _slab_weights(w_in):
    D = w_in.shape[0]
    sizes = [1024, 1024, 1024, 512, 128, 128, 128, 128, 128, 128, 24, 2048]
    offs = np.concatenate([[0], np.cumsum(sizes)])
    part = lambda i: w_in[:, offs[i]:offs[i + 1]]
    pieces = [part(0), part(1), part(2), part(11), part(3)] + [part(i) for i in range(4, 10)] + [part(10)]
    w = jnp.concatenate(pieces, axis=1)
    return jnp.pad(w, ((0, 0), (0, SLAB_COLS - w.shape[1]))).astype(BF16)


def _layer(x, c, ada_w, ada_b, norm1_w, w_in, da_lq1, da_lk1, da_lq2, da_lk2, da_subln_w,
           cmp_k_pe, cmp_k_w1, cmp_k_w2, cmp_v_pe, cmp_v_w1, cmp_v_w2, w_da_out, w_nsa_out, w_o,
           norm2_w, router_w, router_b, exp_w_gate, exp_w_up, exp_w_down,
           sh_w_gate, sh_w_up, sh_w_down, final_norm_w, lam_init):
    B, S, D = x.shape
    N = B * S
    G, HPG = NSA_GROUPS, NSA_HPG
    x2 = x.reshape(N, D)

    mod = _ada(c, ada_w, ada_b)
    sh1, sc1, g1, sh2, sc2, g2 = jnp.split(mod, 6, axis=-1)

    slab = _inproj(x2, norm1_w, sh1, sc1, _slab_weights(w_in), S)
    slab3 = slab.reshape(B, S, SLAB_COLS)

    i_all = np.arange(1, DA_HEADS + NSA_HEADS + 1, dtype=np.float32)
    slopes = (2.0 ** (-8.0 * i_all / (DA_HEADS + NSA_HEADS))).astype(np.float32)
    slopes_a = jnp.asarray(slopes[0::2])
    slopes_b = jnp.asarray(slopes[1::2])

    ta_q, ta_k = min(512, S), min(512, S)
    vT = slab3[:, :, COL_DAV:COL_DAV + 1024].reshape(B, S // ta_k, ta_k, DA_HEADS, DA_VDIM).transpose(0, 3, 1, 4, 2)
    kmin = _first_live_tile(slopes[0::2], slab3, ta_q, ta_k)
    oa = _diff_attention(slopes_a, kmin, slab3, _with_ones_row(vT), _slope_rows(slopes[0::2]), da_lq1, da_lk1,
                         da_lq2, da_lk2, da_subln_w, lam_init, ta_q, ta_k).reshape(N, DA_HEADS * DA_VDIM)

    tb, tb_k = 256, min(256, S)
    n_cp = S // CMP_STRIDE
    n_sel = S // SEL_BLOCK

    def kv_groups(j):
        c0 = COL_KV6 + 128 * j
        return slab3[:, :, c0:c0 + 128].reshape(B, S, G, NSA_DIM).transpose(0, 2, 1, 3)

    def kv_tiles_T(a, rows):
        return a.reshape(B, G, S // rows, rows, NSA_DIM).transpose(0, 1, 2, 4, 3)

    ck_rows = kv_groups(0).reshape(B * G, n_cp, CMP_STRIDE * NSA_DIM)
    cv_rows = kv_groups(1).reshape(B * G, n_cp, CMP_STRIDE * NSA_DIM)
    kc = _compress(ck_rows, cmp_k_pe, cmp_k_w1, cmp_k_w2).reshape(B, G, n_cp, NSA_DIM).astype(BF16)
    vc = _compress(cv_rows, cmp_v_pe, cmp_v_w1, cmp_v_w2).reshape(B, G, n_cp, NSA_DIM).astype(BF16)
    nq = (slab3[:, :, COL_NSAQ:COL_NSAQ + 512] * jnp.asarray(NSA_DIM ** -0.5, BF16))
    nqT = nq.reshape(B, S // tb, tb, G, HPG, NSA_DIM).transpose(0, 3, 1, 5, 4, 2).reshape(B, G, S // tb, NSA_DIM, HPG * tb)
    gT = (slab3[:, :, COL_NSAG:COL_NSAG + 3 * NSA_HEADS].astype(F32).reshape(B, S // tb, tb, G, HPG, 3)
          .transpose(0, 3, 1, 5, 4, 2).reshape(B, G, S // tb, 3, HPG * tb))
    onehot = jnp.asarray((np.arange(S)[:, None] // SEL_BLOCK == np.arange(SEL_COLS)[None, :]).astype(np.float32), BF16)
    zeros64 = jnp.zeros((B, G, S, NSA_DIM), BF16)
    ksa = jnp.concatenate([kv_groups(2), zeros64, jnp.broadcast_to(onehot, (B, G, S, SEL_COLS))], axis=-1)
    pad_rows = jnp.zeros((B, G, WINDOW, 2 * NSA_DIM), BF16).at[..., NSA_DIM].set(1.0)
    kwa = jnp.concatenate([pad_rows, jnp.concatenate([kv_groups(4), zeros64], axis=-1)], axis=2)
    vw_pad = jnp.pad(kv_groups(5), ((0, 0), (0, 0), (WINDOW, 0), (0, 0)))
    vwT = vw_pad.reshape(B, G, (S + WINDOW) // tb, tb, NSA_DIM).transpose(0, 1, 2, 4, 3)
    r_w = np.arange(WINDOW + tb)[:, None]
    lane_w = np.arange(HPG * tb)[None, :]
    d_w = (WINDOW + lane_w % tb - r_w).astype(np.float32)
    slope_w = slopes[1::2].reshape(G, 1, HPG)[:, :, lane_w[0] // tb]
    wb = jnp.asarray(np.where((d_w >= 0) & (d_w < WINDOW), -slope_w * d_w[None], np.float32(NEG)).astype(np.float32))
    obT = _nsa_attention(slopes_b, nqT, kc, vc.transpose(0, 1, 3, 2), ksa, kv_tiles_T(kv_groups(3), tb_k),
                         kwa, vwT, wb, gT, _cmp_to_sel_T(n_cp, n_sel), tb, tb_k)
    ob = (obT.reshape(B, G, S // tb, NSA_DIM, HPG, tb).transpose(0, 2, 5, 1, 4, 3)
          .reshape(N, NSA_HEADS * NSA_DIM))

    wr_hi = router_w.astype(BF16)
    wr_lo = (router_w.astype(F32) - wr_hi.astype(F32)).astype(BF16)
    x1, h2, logits = _merge(oa, ob, slab, x2, g1, norm2_w, sh2, sc2, w_da_out.astype(BF16),
                            w_nsa_out.astype(BF16), w_o.astype(BF16), jnp.stack([wr_hi, wr_lo]), S)

    rb = 512
    tc = 256
    top_eT, top_wT = _route(logits.T, router_b)
    destT, counts = _slots(top_eT, rb)
    n_rows = ((N * MOE_TOPK + N_EXPERTS * (rb - 1) + rb - 1) // rb) * rb
    padded = (jnp.ceil(counts[:, 0] / rb) * rb).astype(I32)
    pend = jnp.cumsum(padded)
    blk_start = jnp.arange(n_rows // rb, dtype=I32) * rb
    blk_e = jnp.minimum(jnp.sum((pend[None, :] <= blk_start[:, None]).astype(I32), axis=1), N_EXPERTS - 1)
    d_flat = destT.reshape(-1)
    piece_idx = jnp.concatenate([d_flat + s * n_rows for s in range(SC_SPLIT)])
    xs2d = _sc_scatter_rows(h2.reshape(SC_SPLIT * N, SC_PIECE), piece_idx, SC_SPLIT * n_rows, MOE_TOPK)
    cnt = counts[:, 0].astype(I32)
    pad_start = jnp.concatenate([pend - padded + cnt, pend[-1:]])
    pad_len = jnp.concatenate([padded - cnt, n_rows - pend[-1:]])
    xs = _zero_pads(xs2d, pad_start, pad_len, rb, n_rows).reshape(SC_SPLIT, n_rows, SC_PIECE)
    blk_info = jnp.concatenate([blk_e, pend[-1:] // rb])
    ys = _experts(blk_info, xs, exp_w_gate, exp_w_up, exp_w_down, rb)
    sgu = jnp.concatenate([sh_w_gate, sh_w_up], axis=-1).astype(BF16)
    yg = _sc_gather_rows(ys.reshape(SC_SPLIT * n_rows, SC_PIECE), piece_idx).reshape(SC_SPLIT, MOE_TOPK, N, SC_PIECE)
    out = _combine(yg, top_wT.T, h2, x1, g2, final_norm_w, sgu, sh_w_down.astype(BF16), S, tc)
    return out.reshape(B, S, D)


def kernel(x, c, ada_w, ada_b, norm1_w, w_in, da_lq1, da_lk1, da_lq2, da_lk2, da_subln_w, cmp_k_pe, cmp_k_w1, cmp_k_w2, cmp_v_pe, cmp_v_w1, cmp_v_w2, w_da_out, w_nsa_out, w_o, norm2_w, router_w, router_b, exp_w_gate, exp_w_up, exp_w_down, sh_w_gate, sh_w_up, sh_w_down, final_norm_w):
    depth = ada_w.shape[0]
    assert depth == 1, "one decoder layer"
    lam_init = 0.8 - 0.6 * math.exp(-0.3 * 0)
    return _layer(x, c, ada_w[0], ada_b[0], norm1_w[0], w_in[0], da_lq1[0], da_lk1[0], da_lq2[0], da_lk2[0],
                  da_subln_w[0], cmp_k_pe[0], cmp_k_w1[0], cmp_k_w2[0], cmp_v_pe[0], cmp_v_w1[0], cmp_v_w2[0],
                  w_da_out[0], w_nsa_out[0], w_o[0], norm2_w[0], router_w[0], router_b[0],
                  exp_w_gate[0], exp_w_up[0], exp_w_down[0], sh_w_gate[0], sh_w_up[0], sh_w_down[0],
                  final_norm_w, lam_init)
```

```python
import functools
import math

import numpy as np
import jax
import jax.numpy as jnp
from jax import lax
from jax.experimental import pallas as pl
from jax.experimental.pallas import tpu as pltpu
from jax.experimental.pallas import tpu_sc as plsc

F32 = jnp.float32
BF16 = jnp.bfloat16
I32 = jnp.int32
U32 = jnp.uint32
HIGHEST = lax.Precision.HIGHEST

NORM_EPS = 1e-6
NEG = -1e30

DA_HEADS = 8
DA_DIM = 64
DA_VDIM = 128
DA_SUBLN_EPS = 1e-5

NSA_HEADS = 8
NSA_GROUPS = 2
NSA_HPG = 4
NSA_DIM = 64
CMP_BLOCK = 32
CMP_STRIDE = 16
SEL_BLOCK = 64
SEL_TOPK = 16
WINDOW = 512
FORCE_BONUS = 1e4
SEL_COLS = 128

N_EXPERTS = 64
MOE_TOPK = 8
N_EXPERT_GROUPS = 8
TOPK_EXPERT_GROUPS = 4
ROUTED_SCALE = 2.5

VMEM_LIMIT_V7X = 56 * 1024 * 1024

COL_DAQ, COL_DAK, COL_DAV = 0, 1024, 2048
COL_MERGE = 3072
COL_NSAQ = 5120
COL_KV6 = 5632
COL_NSAG = 6400
SLAB_COLS = 6528


SUBLANES = 8
SC_WINDOW = 128
SC_SPLIT = 2
SC_PIECE = 256

POS_SPLIT = 16
POS_ROWS = 16
ONES_ROWS = 16


def _pos_features(rows):
    r = np.arange(rows)
    f = np.zeros((rows, 128), np.float32)
    f[:, 0:3] = (r // POS_SPLIT)[:, None]
    f[:, 3:6] = (r % POS_SPLIT)[:, None]
    return jnp.asarray(f, BF16)


def _slope_rows(slopes):
    s = np.asarray(slopes, np.float32)
    bf = lambda x: x.astype(BF16).astype(np.float32)
    p1 = bf(s)
    p2 = bf(s - p1)
    p3 = bf(s - p1 - p2)
    out = np.zeros((s.shape[0], POS_ROWS, 128), np.float32)
    for i, piece in enumerate((p1, p2, p3)):
        out[:, i, :] = POS_SPLIT * piece[:, None]
        out[:, 3 + i, :] = piece[:, None]
    return jnp.asarray(out)


def _with_ones_row(vT):
    shape = vT.shape[:-2] + (ONES_ROWS, vT.shape[-1])
    extra = jnp.zeros(shape, vT.dtype).at[..., 0, :].set(1.0)
    return jnp.concatenate([vT, extra], axis=-2)


def _cparams(sem, vmem=VMEM_LIMIT_V7X):
    return pltpu.CompilerParams(dimension_semantics=sem, vmem_limit_bytes=vmem)


def _ada_kernel(c_ref, w_ref, b_ref, o_ref):
    c = c_ref[...]
    ca = c * jax.nn.sigmoid(c)
    o_ref[...] = jnp.dot(ca, w_ref[...], preferred_element_type=F32, precision=HIGHEST) + b_ref[...]


def _ada(c, w, b):
    B, D = c.shape
    n_out = w.shape[1]
    rows = 8
    cp = jnp.zeros((rows, D), F32).at[:B].set(c)
    tn = 1024
    out = pl.pallas_call(
        _ada_kernel,
        grid=(n_out // tn,),
        in_specs=[pl.BlockSpec((rows, D), lambda j: (0, 0)),
                  pl.BlockSpec((D, tn), lambda j: (0, j)),
                  pl.BlockSpec((1, tn), lambda j: (0, j))],
        out_specs=pl.BlockSpec((rows, tn), lambda j: (0, j)),
        out_shape=jax.ShapeDtypeStruct((rows, n_out), F32),
        compiler_params=_cparams(("arbitrary",)),
        name="ada",
    )(cp, w, b.reshape(1, n_out))
    return out[:B]


def _modulate(x, nw, sh, sc):
    r = lax.rsqrt(jnp.mean(x * x, axis=-1, keepdims=True) + NORM_EPS)
    return (x * r) * nw * (1.0 + sc) + sh


def _inproj_kernel(x_ref, nw_ref, sh_ref, sc_ref, w_ref, o_ref, h_ref):
    @pl.when(pl.program_id(1) == 0)
    def _():
        h_ref[...] = _modulate(x_ref[...], nw_ref[...], sh_ref[...], sc_ref[...]).astype(BF16)

    o_ref[...] = jnp.dot(h_ref[...], w_ref[...], preferred_element_type=F32).astype(BF16)


def _inproj(x2, nw, sh, sc, w_slab, S):
    N, D = x2.shape
    tm = min(1024, S)
    tn = SLAB_COLS // 3
    per_b = S // tm
    return pl.pallas_call(
        _inproj_kernel,
        grid=(N // tm, SLAB_COLS // tn),
        in_specs=[pl.BlockSpec((tm, D), lambda i, j: (i, 0)),
                  pl.BlockSpec((1, D), lambda i, j: (0, 0)),
                  pl.BlockSpec((None, 1, D), lambda i, j: (i // per_b, 0, 0)),
                  pl.BlockSpec((None, 1, D), lambda i, j: (i // per_b, 0, 0)),
                  pl.BlockSpec((D, tn), lambda i, j: (0, j))],
        out_specs=pl.BlockSpec((tm, tn), lambda i, j: (i, j)),
        out_shape=jax.ShapeDtypeStruct((N, SLAB_COLS), BF16),
        scratch_shapes=[pltpu.VMEM((tm, D), BF16)],
        compiler_params=_cparams(("parallel", "arbitrary")),
        name="inproj",
    )(x2, nw.reshape(1, D), sh.reshape(-1, 1, D), sc.reshape(-1, 1, D), w_slab)


def _cmp_kernel(r_ref, pe_ref, w1a_ref, w1b_ref, w2_ref, o_ref):
    r = r_ref[...]
    n_rows = r.shape[0]
    a = jnp.dot(r, w1a_ref[...], preferred_element_type=F32)
    b = jnp.dot(r, w1b_ref[...], preferred_element_type=F32)
    pe = pe_ref[...]
    half = pe.shape[1] // 2
    bias = (jnp.dot(pe[:, :half], w1a_ref[...].astype(F32), preferred_element_type=F32, precision=HIGHEST)
            + jnp.dot(pe[:, half:], w1b_ref[...].astype(F32), preferred_element_type=F32, precision=HIGHEST))
    hid = a + pltpu.roll(b, n_rows - 1, 0) + bias
    act = 0.5 * hid * (1.0 + jnp.tanh(0.7978845608028654 * (hid + 0.044715 * hid * hid * hid)))
    o_ref[...] = jnp.dot(act, w2_ref[...], preferred_element_type=F32, precision=HIGHEST)


def _compress(rows, pe, w1, w2):
    BG, n_rows, width = rows.shape
    hidden = w1.shape[1]
    w1a = w1[:width].astype(BF16)
    w1b = w1[width:].astype(BF16)
    return pl.pallas_call(
        _cmp_kernel,
        grid=(BG,),
        in_specs=[pl.BlockSpec((None, n_rows, width), lambda i: (i, 0, 0)),
                  pl.BlockSpec((1, 2 * width), lambda i: (0, 0)),
                  pl.BlockSpec((width, hidden), lambda i: (0, 0)),
                  pl.BlockSpec((width, hidden), lambda i: (0, 0)),
                  pl.BlockSpec((hidden, NSA_DIM), lambda i: (0, 0))],
        out_specs=pl.BlockSpec((None, n_rows, NSA_DIM), lambda i: (i, 0, 0)),
        out_shape=jax.ShapeDtypeStruct((BG, n_rows, NSA_DIM), F32),
        compiler_params=_cparams(("parallel",)),
        name="cmp",
    )(rows, pe.reshape(1, -1).astype(F32), w1a, w1b, w2.astype(F32))


def _da_kernel(slopes_ref, kmin_ref, q_ref, k_ref, vT_ref, pos_ref, rel_ref, srow_ref, lq1_ref, lk1_ref, lq2_ref,
               lk2_ref, subw_ref, o_ref, acc_ref, qa_ref, sa_ref, sb_ref, pa_ref, pb_ref, *, tq, tk, lam_init):
    h = pl.program_id(1)
    qi = pl.program_id(2)
    slope = slopes_ref[h]
    q0 = qi * tq
    w = 2 * tq
    nk = k_ref.shape[0] // tk
    qT = (q_ref[...].astype(F32) * DA_DIM ** -0.5).T.astype(BF16)
    row = lax.broadcasted_iota(I32, qT.shape, 0)
    zero = jnp.zeros_like(qT)
    qa_ref[0:2 * DA_DIM, :] = jnp.concatenate(
        [jnp.where(row < DA_DIM, qT, zero), jnp.where(row >= DA_DIM, qT, zero)], axis=1)
    qa_ref[2 * DA_DIM:2 * DA_DIM + POS_ROWS, :] = jnp.concatenate([srow_ref[...]] * (w // 128), axis=1).astype(BF16)
    qa_ref[2 * DA_DIM + POS_ROWS:, :] = jnp.zeros((2 * DA_DIM - POS_ROWS, w), BF16)

    def scores(kt, s_ref):
        k0 = kt * tk
        k_t = k_ref[pl.ds(pl.multiple_of(k0, tk), tk), :]
        s = jnp.dot(jnp.concatenate([k_t, pos_ref[...]], axis=1), qa_ref[...], preferred_element_type=F32)
        s = jnp.where(rel_ref[...] <= q0 - k0, s, NEG)
        s_ref[...] = s
        return jnp.max(s, axis=0, keepdims=True)

    def probs(s_ref, p_ref, mx, off, m_old):
        m_new = jnp.maximum(m_old, mx + off)
        p_ref[...] = jnp.exp(s_ref[...] - (m_new - off)).astype(BF16)
        return m_new, jnp.exp(m_old - m_new)

    def accumulate(kt, p_ref, alpha):
        acc_ref[...] = alpha * acc_ref[...] + jnp.dot(vT_ref[kt], p_ref[...], preferred_element_type=F32)

    n_tiles = (q0 + tq - 1) // tk + 1
    kt_min = kmin_ref[(pl.program_id(0) * pl.num_programs(1) + h) * pl.num_programs(2) + qi]
    acc_ref[...] = jnp.zeros_like(acc_ref)
    pb_ref[...] = jnp.zeros_like(pb_ref)
    mx_a = scores(kt_min, sa_ref)

    def pair(j, carry):
        m, mx_a, alpha_b = carry
        ta = kt_min + 2 * j
        tb = ta + 1
        accumulate(jnp.maximum(ta - 1, 0), pb_ref, alpha_b)
        m, alpha_a = probs(sa_ref, pa_ref, mx_a, slope * (ta * tk).astype(F32), m)
        mx_b = scores(tb, sb_ref)
        accumulate(ta, pa_ref, alpha_a)
        m, alpha_b = probs(sb_ref, pb_ref, mx_b, slope * (tb * tk).astype(F32), m)
        mx_a = scores(jnp.minimum(ta + 2, nk - 1), sa_ref)
        return m, mx_a, alpha_b

    n_live = n_tiles - kt_min
    n_pairs = n_live // 2
    init = (jnp.full((1, w), NEG, F32), mx_a, jnp.ones((1, w), F32))
    m, mx_a, alpha_b = lax.fori_loop(0, n_pairs, pair, init)
    last = n_tiles - 1

    @pl.when(n_live % 2 == 0)
    def _():
        accumulate(last, pb_ref, alpha_b)

    @pl.when(n_live % 2 == 1)
    def _():
        accumulate(jnp.maximum(last - 1, 0), pb_ref, alpha_b)
        _, alpha_a = probs(sa_ref, pa_ref, mx_a, slope * (last * tk).astype(F32), m)
        accumulate(last, pa_ref, alpha_a)

    lam = (jnp.exp(jnp.sum(lq1_ref[...] * lk1_ref[...], axis=-1, keepdims=True))
           - jnp.exp(jnp.sum(lq2_ref[...] * lk2_ref[...], axis=-1, keepdims=True)) + lam_init)
    on = acc_ref[0:DA_VDIM, :] * (1.0 / acc_ref[DA_VDIM:DA_VDIM + 1, :])
    o = on[:, :tq] - lam * on[:, tq:]
    r = lax.rsqrt(jnp.mean(o * o, axis=0, keepdims=True) + DA_SUBLN_EPS)
    o_ref[...] = (o * r * subw_ref[...] * (1.0 - lam_init)).T.astype(BF16)


EXP_UNDERFLOW = 104.0


def _first_live_tile(slopes_np, slab3, tq, tk):
    B, S, _ = slab3.shape
    H = DA_HEADS
    width = 2 * DA_DIM * H
    grp = jnp.asarray((np.arange(width)[:, None] // DA_DIM == np.arange(2 * H)[None, :]).astype(np.float32), BF16)

    def norms(c0):
        v = slab3[:, :, c0:c0 + width]
        return jnp.sqrt(jnp.einsum('bsc,cg->bsg', v * v, grp, preferred_element_type=F32))

    qn = norms(COL_DAQ) * DA_DIM ** -0.5
    qn = jnp.max(qn.reshape(B, S // tq, tq, H, 2), axis=2).transpose(0, 2, 3, 1)
    kn = jnp.max(norms(COL_DAK), axis=1).reshape(B, H, 2)
    bound = jnp.max(qn * kn[..., None], axis=2) * 1.01 + 0.01
    q0 = (jnp.arange(S // tq, dtype=F32) * tq)[None, None, :]
    slope = jnp.asarray(slopes_np, F32)[None, :, None]
    first_key = q0 - (EXP_UNDERFLOW + 2.0 * bound) / slope
    kt = jnp.floor(first_key / tk).astype(I32)
    n_tiles = (jnp.arange(S // tq, dtype=I32) * tq + tq - 1) // tk + 1
    return jnp.clip(kt, 0, n_tiles[None, None, :] - 1).reshape(-1)


def _diff_attention(slopes, kmin, slab3, vT, srows, lq1, lk1, lq2, lk2, subw, lam_init, tq, tk):
    B, S, _ = slab3.shape
    nk = S // tk
    assert tk % tq == 0 and S % tk == 0
    qcol = COL_DAQ // 128
    tile = tq
    kcol = COL_DAK // 128
    v_rows = vT.shape[3]
    vec = lambda a: a.reshape(1, DA_DIM).astype(F32)
    grid_spec = pltpu.PrefetchScalarGridSpec(
        num_scalar_prefetch=2,
        grid=(B, DA_HEADS, S // tile),
        in_specs=[pl.BlockSpec((None, tile, 128), lambda b, h, i, s, km: (b, i, qcol + h)),
                  pl.BlockSpec((None, S, 128), lambda b, h, i, s, km: (b, 0, kcol + h)),
                  pl.BlockSpec((None, None, nk, v_rows, tk), lambda b, h, i, s, km: (b, h, 0, 0, 0)),
                  pl.BlockSpec((tk, 128), lambda b, h, i, s, km: (0, 0)),
                  pl.BlockSpec((tk, 2 * tq), lambda b, h, i, s, km: (0, 0)),
                  pl.BlockSpec((None, POS_ROWS, 128), lambda b, h, i, s, km: (h, 0, 0)),
                  pl.BlockSpec((1, DA_DIM), lambda b, h, i, s, km: (0, 0)),
                  pl.BlockSpec((1, DA_DIM), lambda b, h, i, s, km: (0, 0)),
                  pl.BlockSpec((1, DA_DIM), lambda b, h, i, s, km: (0, 0)),
                  pl.BlockSpec((1, DA_DIM), lambda b, h, i, s, km: (0, 0)),
                  pl.BlockSpec((DA_VDIM, 1), lambda b, h, i, s, km: (0, 0))],
        out_specs=pl.BlockSpec((None, tile, 128), lambda b, h, i, s, km: (b, i, h)),
        scratch_shapes=[pltpu.VMEM((v_rows, 2 * tq), F32), pltpu.VMEM((4 * DA_DIM, 2 * tq), BF16),
                        pltpu.VMEM((tk, 2 * tq), F32), pltpu.VMEM((tk, 2 * tq), F32),
                        pltpu.VMEM((tk, 2 * tq), BF16), pltpu.VMEM((tk, 2 * tq), BF16)],
    )
    rel = jnp.asarray(np.arange(tk)[:, None] - (np.arange(2 * tq) % tq)[None, :], I32)
    return pl.pallas_call(
        functools.partial(_da_kernel, tq=tq, tk=tk, lam_init=lam_init),
        grid_spec=grid_spec,
        out_shape=jax.ShapeDtypeStruct((B, S, DA_HEADS * DA_VDIM), BF16),
        compiler_params=_cparams(("parallel", "parallel", "arbitrary")),
        name="diffattn",
    )(slopes, kmin, slab3, slab3, vT, _pos_features(tk), rel, srows, vec(lq1), vec(lk1), vec(lq2), vec(lk2),
      subw.reshape(DA_VDIM, 1).astype(F32))


def _nsa_kernel(slopes_ref, qT_ref, kc_ref, vcT_ref, ksa_ref, vsT_ref, kwa_ref, vwT_ref, wb_ref, gT_ref, mT_ref,
                grp_ref, ltri_ref, oT_ref, acc_ref, res_ref, base_ref, qa_ref, sa_ref, sb_ref, pa_ref, pb_ref,
                list_ref, slist_ref, lsem, imp_ref, *, t, tk, n_top):
    g = pl.program_id(1)
    qi = pl.program_id(2)
    q0 = qi * t
    w = NSA_HPG * t
    n_cp = kc_ref.shape[0]
    n_sel = mT_ref.shape[0]
    lane1 = lax.broadcasted_iota(I32, (1, w), 1)
    slope_row = jnp.zeros((1, w), F32)
    for hh in range(NSA_HPG):
        slope_row = jnp.where((lane1 >= hh * t) & (lane1 < (hh + 1) * t), slopes_ref[g * NSA_HPG + hh], slope_row)
    q_all = qT_ref[...]
    gates = jax.nn.sigmoid(gT_ref[...])
    key_off = lax.broadcasted_iota(I32, (tk, w), 0)
    t_loc = jnp.bitwise_and(lax.broadcasted_iota(I32, (tk, w), 1), t - 1)
    base_ref[...] = slope_row * key_off.astype(F32)

    def compressed(rows):
        c_idx = lax.broadcasted_iota(I32, (rows, w), 0)
        t_c = q0 + jnp.bitwise_and(lax.broadcasted_iota(I32, (rows, w), 1), t - 1)
        d_c = t_c - (c_idx * CMP_STRIDE + (CMP_BLOCK - 1))
        ok_c = d_c >= 0
        s = jnp.dot(kc_ref[0:rows, :], q_all, preferred_element_type=F32) - slope_row * d_c.astype(F32)
        s = jnp.where(ok_c, s, NEG)
        e = jnp.exp(s - jnp.max(s, axis=0, keepdims=True))
        inv = 1.0 / jnp.sum(e, axis=0, keepdims=True)
        p = jnp.where(ok_c, e * inv, 0.0)
        res_ref[...] = gates[0:1, :] * jnp.dot(vcT_ref[:, 0:rows], p.astype(BF16), preferred_element_type=F32)
        p_sum = p[:, 0:t]
        for hh in range(1, NSA_HPG):
            p_sum = p_sum + p[:, hh * t:(hh + 1) * t]
        p_hi = p_sum.astype(BF16)
        p_lo = (p_sum - p_hi.astype(F32)).astype(BF16)
        mT = mT_ref[:, 0:rows]
        imp_ref[...] = jnp.dot(mT, p_hi, preferred_element_type=F32) + jnp.dot(mT, p_lo, preferred_element_type=F32)

    chunk = min(n_cp, 128)
    n_chunk = jnp.minimum((q0 + t - CMP_BLOCK) // (CMP_STRIDE * chunk) + 1, n_cp // chunk)
    for c in range(1, n_cp // chunk + 1):
        pl.when(n_chunk == c)(functools.partial(compressed, c * chunk))

    imp = imp_ref[...]
    blk = lax.broadcasted_iota(I32, (n_sel, t), 0)
    t_s = q0 + lax.broadcasted_iota(I32, (n_sel, t), 1)
    cur = jnp.right_shift(t_s, int(math.log2(SEL_BLOCK)))
    forced = (blk == 0) | (blk == cur) | (blk == cur - 1)
    score = jnp.where(blk * SEL_BLOCK <= t_s, imp + jnp.where(forced, FORCE_BONUS, 0.0), NEG)
    blk_f = blk.astype(F32)
    sel = jnp.zeros((n_sel, t), F32)
    for _ in range(n_top):
        _, idx = _first_max(score, blk_f)
        hit = blk_f == idx
        sel = jnp.where(hit, 1.0, sel)
        score = jnp.where(hit, -jnp.inf, score)
    n_full = q0 // tk
    nt_pad = grp_ref.shape[0]
    cnt = jnp.broadcast_to(jnp.sum(sel, axis=1, keepdims=True), (n_sel, 128)).astype(BF16)
    tile_cnt = jnp.dot(grp_ref[...], cnt, preferred_element_type=F32)
    kt_i = lax.broadcasted_iota(I32, (nt_pad, 128), 0)
    active = (tile_cnt > 0.5) & (kt_i < n_full)
    act_f = jnp.where(active, 1.0, 0.0)
    before = jnp.dot(ltri_ref[...], act_f.astype(BF16), preferred_element_type=F32)
    lane_j = lax.broadcasted_iota(I32, (nt_pad, 128), 1)
    slot_hit = active & (before == lane_j.astype(F32))
    tiles_row = jnp.sum(jnp.where(slot_hit, kt_i.astype(F32), 0.0), axis=0, keepdims=True)
    n_act_row = jnp.sum(act_f, axis=0, keepdims=True)
    lane8 = lax.broadcasted_iota(I32, (8, 128), 1)
    list_ref[...] = jnp.where(lane8 == 127, n_act_row, tiles_row).astype(I32)
    list_copy = pltpu.make_async_copy(list_ref, slist_ref, lsem)
    list_copy.start()

    selb = jnp.where(sel > 0.5, 0.0, NEG)
    row64 = lax.broadcasted_iota(I32, (NSA_DIM, w), 0)
    qa_ref[0:NSA_DIM, :] = q_all
    qa_ref[NSA_DIM:2 * NSA_DIM, :] = jnp.where(row64 == 0, NEG, 0.0).astype(BF16)
    qa_ref[2 * NSA_DIM:2 * NSA_DIM + n_sel, :] = jnp.concatenate([selb] * NSA_HPG, axis=1).astype(BF16)
    if n_sel < SEL_COLS:
        qa_ref[2 * NSA_DIM + n_sel:, :] = jnp.zeros((SEL_COLS - n_sel, w), BF16)

    n_wt = (WINDOW + t) // t
    kw_t = kwa_ref[pl.ds(pl.multiple_of(q0, t), WINDOW + t), :]
    s = jnp.dot(kw_t, qa_ref[0:2 * NSA_DIM, :], preferred_element_type=F32) + wb_ref[...]
    p = jnp.exp(s - jnp.max(s, axis=0, keepdims=True))
    inv = 1.0 / jnp.sum(p, axis=0, keepdims=True)
    p = p.astype(BF16)
    o_w = jnp.dot(vwT_ref[qi], p[0:t, :], preferred_element_type=F32)
    for c in range(1, n_wt):
        o_w = o_w + jnp.dot(vwT_ref[qi + c], p[c * t:(c + 1) * t, :], preferred_element_type=F32)
    res_ref[...] = res_ref[...] + gates[2:3, :] * (o_w * inv)

    def scores(kt, s_ref):
        k_t = ksa_ref[pl.ds(pl.multiple_of(kt * tk, tk), tk), :]
        s = jnp.dot(k_t, qa_ref[...], preferred_element_type=F32) + base_ref[...]
        s_ref[...] = s
        return jnp.max(s, axis=0, keepdims=True)

    def probs(s_ref, p_ref, mx, off, m_old, l_old):
        m_new = jnp.maximum(m_old, mx + off)
        p = jnp.exp(s_ref[...] - (m_new - off))
        alpha = jnp.exp(m_old - m_new)
        p_ref[...] = p.astype(BF16)
        return m_new, alpha * l_old + jnp.sum(p, axis=0, keepdims=True), alpha

    def accumulate(kt, p_ref, alpha):
        acc_ref[...] = alpha * acc_ref[...] + jnp.dot(vsT_ref[kt], p_ref[...], preferred_element_type=F32)

    list_copy.wait()
    n_act = slist_ref[0, 127]
    mx_a = scores(slist_ref[0, 0], sa_ref)

    kd = n_full * tk
    s = jnp.dot(ksa_ref[pl.ds(pl.multiple_of(kd, tk), tk), :], qa_ref[...], preferred_element_type=F32) + base_ref[...]
    s = jnp.where(key_off - t_loc <= q0 - kd, s, NEG)
    off_d = slope_row * kd.astype(F32)
    m = jnp.max(s, axis=0, keepdims=True) + off_d
    p = jnp.exp(s - (m - off_d))
    l = jnp.sum(p, axis=0, keepdims=True)
    acc_ref[...] = jnp.dot(vsT_ref[n_full], p.astype(BF16), preferred_element_type=F32)
    pb_ref[...] = jnp.zeros_like(pb_ref)

    def pair(j, carry):
        m, l, mx_a, alpha_b, tb_prev = carry
        ta = slist_ref[0, 2 * j]
        tb = slist_ref[0, 2 * j + 1]
        off_a = slope_row * (ta * tk).astype(F32)
        off_b = jnp.where(2 * j + 1 < n_act, slope_row * (tb * tk).astype(F32), NEG)
        accumulate(tb_prev, pb_ref, alpha_b)
        m, l, alpha_a = probs(sa_ref, pa_ref, mx_a, off_a, m, l)
        mx_b = scores(tb, sb_ref)
        accumulate(ta, pa_ref, alpha_a)
        m, l, alpha_b = probs(sb_ref, pb_ref, mx_b, off_b, m, l)
        mx_a = scores(slist_ref[0, 2 * j + 2], sa_ref)
        return m, l, mx_a, alpha_b, tb

    n_pairs = (n_act + 1) // 2
    m, l, _, alpha_b, tb_last = lax.fori_loop(0, n_pairs, pair, (m, l, mx_a, jnp.ones((1, w), F32), jnp.int32(0)))
    accumulate(tb_last, pb_ref, alpha_b)
    oT_ref[...] = (res_ref[...] + gates[1:2, :] * (acc_ref[...] * (1.0 / l))).astype(BF16)


def _nsa_attention(slopes, qT, kc, vcT, ksa, vsT, kwa, vwT, wb, gT, mT, t, tk):
    B, G, nq, _, w = qT.shape
    S = nq * t
    n_cp = kc.shape[2]
    n_sel = S // SEL_BLOCK
    n_top = min(SEL_TOPK, n_sel)
    assert tk % t == 0 and S % tk == 0 and t & (t - 1) == 0 and WINDOW % t == 0 and n_sel <= SEL_COLS
    n_tiles = S // tk
    nt_pad = -(-n_tiles // 16) * 16
    assert nt_pad + 2 < 127
    grp = jnp.asarray((np.arange(n_sel)[None, :] // (tk // SEL_BLOCK) == np.arange(nt_pad)[:, None])
                      .astype(np.float32), BF16)
    ltri = jnp.asarray((np.arange(nt_pad)[None, :] < np.arange(nt_pad)[:, None]).astype(np.float32), BF16)
    fixed = lambda b, g, i, s: (b, g, 0, 0)
    fixed5 = lambda b, g, i, s: (b, g, 0, 0, 0)
    tile5 = lambda b, g, i, s: (b, g, i, 0, 0)
    grid_spec = pltpu.PrefetchScalarGridSpec(
        num_scalar_prefetch=1,
        grid=(B, G, nq),
        in_specs=[pl.BlockSpec((None, None, None, NSA_DIM, w), tile5),
                  pl.BlockSpec((None, None, n_cp, NSA_DIM), fixed),
                  pl.BlockSpec((None, None, NSA_DIM, n_cp), fixed),
                  pl.BlockSpec((None, None, S, ksa.shape[-1]), fixed),
                  pl.BlockSpec((None, None, S // tk, NSA_DIM, tk), fixed5),
                  pl.BlockSpec((None, None, S + WINDOW, 2 * NSA_DIM), fixed),
                  pl.BlockSpec((None, None, (S + WINDOW) // t, NSA_DIM, t), fixed5),
                  pl.BlockSpec((None, WINDOW + t, w), lambda b, g, i, s: (g, 0, 0)),
                  pl.BlockSpec((None, None, None, 3, w), tile5),
                  pl.BlockSpec((n_sel, n_cp), lambda b, g, i, s: (0, 0)),
                  pl.BlockSpec((nt_pad, n_sel), lambda b, g, i, s: (0, 0)),
                  pl.BlockSpec((nt_pad, nt_pad), lambda b, g, i, s: (0, 0))],
        out_specs=pl.BlockSpec((None, None, None, NSA_DIM, w), tile5),
        scratch_shapes=[pltpu.VMEM((NSA_DIM, w), F32),
                        pltpu.VMEM((NSA_DIM, w), F32),
                        pltpu.VMEM((tk, w), F32),
                        pltpu.VMEM((2 * NSA_DIM + SEL_COLS, w), BF16),
                        pltpu.VMEM((tk, w), F32), pltpu.VMEM((tk, w), F32),
                        pltpu.VMEM((tk, w), BF16), pltpu.VMEM((tk, w), BF16),
                        pltpu.VMEM((8, 128), I32), pltpu.SMEM((8, 128), I32), pltpu.SemaphoreType.DMA(()),
                        pltpu.VMEM((n_sel, t), F32)],
    )
    return pl.pallas_call(
        functools.partial(_nsa_kernel, t=t, tk=tk, n_top=n_top),
        grid_spec=grid_spec,
        out_shape=jax.ShapeDtypeStruct((B, G, nq, NSA_DIM, w), BF16),
        compiler_params=_cparams(("parallel", "parallel", "arbitrary")),
        name="nsa",
    )(slopes, qT, kc, vcT, ksa, vsT, kwa, vwT, wb, gT, mT, grp, ltri)


def _pack_halves(v):
    c = v.shape[1] // 2
    lo = lax.bitcast_convert_type(v[:, :c].astype(BF16).astype(F32), U32)
    hi = lax.bitcast_convert_type(v[:, c:].astype(BF16).astype(F32), U32)
    return hi | (lo >> 16)


def _unpack_halves(u):
    lo = lax.bitcast_convert_type(u << 16, F32)
    hi = lax.bitcast_convert_type(u & jnp.uint32(0xFFFF0000), F32)
    return lo, hi


def _store_slabs(ref, v):
    packed = _pack_halves(v)
    for s in range(SC_SPLIT):
        ref[s] = packed[:, s * SC_PIECE:(s + 1) * SC_PIECE]


def _load_slabs_bf16(ref):
    parts = [_unpack_halves(ref[s]) for s in range(SC_SPLIT)]
    return jnp.concatenate([p[0].astype(BF16) for p in parts] + [p[1].astype(BF16) for p in parts], axis=1)


def _merge_kernel(oa_ref, ob_ref, ga_ref, gb_ref, x_ref, g1_ref, nw_ref, sh_ref, sc_ref,
                  wa_ref, wb_ref, wo_ref, wr_ref, x1_ref, h2_ref, lg_ref):
    ya = jnp.dot(oa_ref[...], wa_ref[...], preferred_element_type=F32)
    yb = jnp.dot(ob_ref[...], wb_ref[...], preferred_element_type=F32)
    merged = (jax.nn.sigmoid(ga_ref[...].astype(F32)) * ya
              + jax.nn.sigmoid(gb_ref[...].astype(F32)) * yb).astype(BF16)
    mix = jnp.dot(merged, wo_ref[...], preferred_element_type=F32)
    x1 = x_ref[...] + g1_ref[...] * mix
    x1_ref[...] = x1
    h2 = _modulate(x1, nw_ref[...], sh_ref[...], sc_ref[...])
    _store_slabs(h2_ref, h2)
    h_hi = h2.astype(BF16)
    h_lo = (h2 - h_hi.astype(F32)).astype(BF16)
    lg_ref[...] = (jnp.dot(h_hi, wr_ref[0], preferred_element_type=F32)
                   + jnp.dot(h_lo, wr_ref[0], preferred_element_type=F32)
                   + jnp.dot(h_hi, wr_ref[1], preferred_element_type=F32))


def _merge(oa, ob, slab, x2, g1, nw, sh, sc, wa, wb, wo, wr, S):
    N, D = x2.shape
    tm = 256
    per_b = S // tm
    mcol = COL_MERGE // D
    row = lambda i: (i, 0)
    full = lambda i: (0, 0)
    perb = lambda i: (i // per_b, 0, 0)
    n_lg = wr.shape[-1]
    return pl.pallas_call(
        _merge_kernel,
        grid=(N // tm,),
        in_specs=[pl.BlockSpec((tm, oa.shape[1]), row),
                  pl.BlockSpec((tm, ob.shape[1]), row),
                  pl.BlockSpec((tm, D), lambda i: (i, mcol)),
                  pl.BlockSpec((tm, D), lambda i: (i, mcol + 1)),
                  pl.BlockSpec((tm, D), row),
                  pl.BlockSpec((None, 1, D), perb),
                  pl.BlockSpec((1, D), full),
                  pl.BlockSpec((None, 1, D), perb),
                  pl.BlockSpec((None, 1, D), perb),
                  pl.BlockSpec(wa.shape, full),
                  pl.BlockSpec(wb.shape, full),
                  pl.BlockSpec(wo.shape, full),
                  pl.BlockSpec(wr.shape, lambda i: (0, 0, 0))],
        out_specs=[pl.BlockSpec((tm, D), row), pl.BlockSpec((SC_SPLIT, tm, SC_PIECE), lambda i: (0, i, 0)),
                   pl.BlockSpec((tm, n_lg), row)],
        out_shape=[jax.ShapeDtypeStruct((N, D), F32), jax.ShapeDtypeStruct((SC_SPLIT, N, SC_PIECE), U32),
                   jax.ShapeDtypeStruct((N, n_lg), F32)],
        compiler_params=_cparams(("parallel",)),
        name="merge",
    )(oa, ob, slab, slab, x2, g1.reshape(-1, 1, D), nw.reshape(1, D), sh.reshape(-1, 1, D),
      sc.reshape(-1, 1, D), wa, wb, wo, wr)


def _first_max(v, idx):
    mx = jnp.max(v, axis=0, keepdims=True)
    first = jnp.min(jnp.where(v == mx, idx, float(v.shape[0])), axis=0, keepdims=True)
    return mx, first


def _route_kernel(lg_ref, b_ref, e_ref, w_ref):
    lg = lg_ref[...]
    tn = lg.shape[1]
    per_g = N_EXPERTS // N_EXPERT_GROUPS
    scores = jax.nn.sigmoid(lg)
    biased = scores + b_ref[...]
    gi = lax.broadcasted_iota(I32, (N_EXPERT_GROUPS, tn), 0).astype(F32)
    gscore = jnp.zeros((N_EXPERT_GROUPS, tn), F32)
    for g in range(N_EXPERT_GROUPS):
        grp = biased[g * per_g:(g + 1) * per_g, :]
        m1, f1 = _first_max(grp, gi)
        m2 = jnp.max(jnp.where(gi == f1, -jnp.inf, grp), axis=0, keepdims=True)
        gscore = jnp.where(gi == g, m1 + m2, gscore)
    gsel = jnp.zeros((N_EXPERT_GROUPS, tn), F32)
    for _ in range(TOPK_EXPERT_GROUPS):
        _, f = _first_max(gscore, gi)
        hit = gi == f
        gsel = jnp.where(hit, 1.0, gsel)
        gscore = jnp.where(hit, -jnp.inf, gscore)
    ei = lax.broadcasted_iota(I32, (N_EXPERTS, tn), 0).astype(F32)
    emask = jnp.zeros((N_EXPERTS, tn), F32)
    for g in range(N_EXPERT_GROUPS):
        in_g = (ei >= g * per_g) & (ei < (g + 1) * per_g)
        emask = jnp.where(in_g, gsel[g:g + 1, :], emask)
    masked = jnp.where(emask > 0.5, biased, NEG)
    e_out = jnp.zeros((MOE_TOPK, tn), F32)
    w_out = jnp.zeros((MOE_TOPK, tn), F32)
    for r in range(MOE_TOPK):
        _, f = _first_max(masked, ei)
        hit = ei == f
        wv = jnp.sum(jnp.where(hit, scores, 0.0), axis=0, keepdims=True)
        e_out = jnp.where(gi == r, f, e_out)
        w_out = jnp.where(gi == r, wv, w_out)
        masked = jnp.where(hit, -jnp.inf, masked)
    e_ref[...] = e_out.astype(I32)
    w_ref[...] = w_out / jnp.sum(w_out, axis=0, keepdims=True) * ROUTED_SCALE


def _route(lgT, rb):
    E, N = lgT.shape
    tn = 512
    return pl.pallas_call(
        _route_kernel,
        grid=(N // tn,),
        in_specs=[pl.BlockSpec((E, tn), lambda i: (0, i)), pl.BlockSpec((E, 1), lambda i: (0, 0))],
        out_specs=[pl.BlockSpec((MOE_TOPK, tn), lambda i: (0, i)), pl.BlockSpec((MOE_TOPK, tn), lambda i: (0, i))],
        out_shape=[jax.ShapeDtypeStruct((MOE_TOPK, N), I32), jax.ShapeDtypeStruct((MOE_TOPK, N), F32)],
        compiler_params=_cparams(("parallel",)),
        name="route",
    )(lgT, rb.reshape(E, 1).astype(F32))


def _slots_kernel(e_ref, tri_ref, lt_ref, dest_ref, cnt_ref, carry_ref, *, rb):
    phase = pl.program_id(0)
    i = pl.program_id(1)
    e = e_ref[...]
    tn = e.shape[1]
    ei = lax.broadcasted_iota(I32, (N_EXPERTS, tn), 0)

    @pl.when((phase == 0) & (i == 0))
    def _():
        carry_ref[...] = jnp.zeros_like(carry_ref)

    @pl.when(phase == 0)
    def _():
        tot = jnp.zeros((N_EXPERTS, 1), F32)
        for k in range(MOE_TOPK):
            oh = jnp.where(ei == e[k:k + 1, :], 1.0, 0.0)
            tot = tot + jnp.sum(oh, axis=1, keepdims=True)
        carry_ref[...] = carry_ref[...] + tot
        dest_ref[...] = jnp.zeros_like(dest_ref)

    @pl.when((phase == 1) & (i == 0))
    def _():
        cnt = carry_ref[...]
        cnt_ref[...] = cnt
        padded = jnp.broadcast_to(jnp.ceil(cnt / rb) * rb, (N_EXPERTS, 128))
        first = jnp.dot(lt_ref[...], padded, preferred_element_type=F32, precision=HIGHEST)
        carry_ref[...] = first[:, :1]

    @pl.when(phase == 1)
    def _():
        carry = carry_ref[...]
        ki = lax.broadcasted_iota(I32, (MOE_TOPK, tn), 0)
        dest = jnp.zeros((MOE_TOPK, tn), F32)
        for k in range(MOE_TOPK):
            hit = ei == e[k:k + 1, :]
            oh = jnp.where(hit, 1.0, 0.0)
            before = jnp.dot(oh.astype(BF16), tri_ref[...], preferred_element_type=F32)
            row = jnp.sum(jnp.where(hit, before + carry, 0.0), axis=0, keepdims=True)
            dest = jnp.where(ki == k, row, dest)
            carry = carry + jnp.sum(oh, axis=1, keepdims=True)
        carry_ref[...] = carry
        dest_ref[...] = dest.astype(I32)


def _slots(top_eT, rb):
    K, N = top_eT.shape
    tn = 512
    idx = np.arange(tn)
    tri = jnp.asarray((idx[:, None] < idx[None, :]).astype(np.float32), BF16)
    e_idx = np.arange(N_EXPERTS)
    lt = jnp.asarray((e_idx[None, :] < e_idx[:, None]).astype(np.float32), F32)
    return pl.pallas_call(
        functools.partial(_slots_kernel, rb=rb),
        grid=(2, N // tn),
        in_specs=[pl.BlockSpec((K, tn), lambda p, i: (0, i)),
                  pl.BlockSpec((tn, tn), lambda p, i: (0, 0)),
                  pl.BlockSpec((N_EXPERTS, N_EXPERTS), lambda p, i: (0, 0))],
        out_specs=[pl.BlockSpec((K, tn), lambda p, i: (0, i * p)),
                   pl.BlockSpec((N_EXPERTS, 1), lambda p, i: (0, 0))],
        out_shape=[jax.ShapeDtypeStruct((K, N), I32), jax.ShapeDtypeStruct((N_EXPERTS, 1), F32)],
        scratch_shapes=[pltpu.VMEM((N_EXPERTS, 1), F32)],
        compiler_params=_cparams(("arbitrary", "arbitrary")),
        name="slots",
    )(top_eT, tri, lt)


def _sc_scatter_rows(src, idx, n_out, repeat):
    width = src.shape[1]
    n_tok = src.shape[0] // SC_SPLIT
    w = n_tok // SC_WINDOW
    mesh = plsc.VectorSubcoreMesh(core_axis_name="c", subcore_axis_name="s")

    @pl.kernel(out_type=jax.ShapeDtypeStruct((n_out, width), src.dtype), mesh=mesh)
    def scatter_kernel(x_hbm, i_hbm, o_hbm):
        def body(x_vmem, i_vmem):
            for k in range(repeat):
                pltpu.sync_copy(x_vmem, o_hbm.at[i_vmem.at[k]])

        pltpu.emit_pipeline(
            body,
            grid=(SC_SPLIT * w,),
            in_specs=[pl.BlockSpec((SC_WINDOW, width), lambda g: (g, 0)),
                      pl.BlockSpec((repeat, SC_WINDOW), lambda g: (g // w, g % w))],
            out_specs=[],
            core_axis_name=("c", "s"),
            dimension_semantics=(pltpu.PARALLEL,),
        )(x_hbm, i_hbm)

    return scatter_kernel(src, idx.reshape(SC_SPLIT * repeat, n_tok))


def _zero_pads_kernel(start_ref, len_ref, xs_in_ref, xs_ref, zeros_ref, sem, *, rb, n_rows):
    del xs_in_ref
    zeros_ref[...] = jnp.zeros_like(zeros_ref)
    n_seg = start_ref.shape[0]
    bits = rb.bit_length() - 1
    assert rb == 1 << bits

    def for_each_copy(act):
        def rows(row0, size):
            for s in range(SC_SPLIT):
                act(pltpu.make_async_copy(zeros_ref.at[pl.ds(0, size), :],
                                          xs_ref.at[pl.ds(s * n_rows + row0, size), :], sem))

        def segment(e, c):
            start = start_ref[e]
            length = len_ref[e]
            end = start + length
            n_whole = length // rb

            def whole(j, c2):
                rows(pl.multiple_of(end - (j + 1) * rb, SUBLANES), rb)
                return c2

            lax.fori_loop(0, n_whole, whole, 0)
            off = end - n_whole * rb
            rest = length - n_whole * rb
            for b in reversed(range(SUBLANES.bit_length() - 1, bits)):
                size = 1 << b
                take = jnp.bitwise_and(jnp.right_shift(rest, b), 1)
                off = off - take * size

                @pl.when(take == 1)
                def _(off=off, size=size):
                    rows(pl.multiple_of(off, SUBLANES), size)

            lead = jnp.bitwise_and(rest, SUBLANES - 1)
            for i in range(SUBLANES - 1):
                @pl.when(i < lead)
                def _(i=i):
                    rows(start + i, 1)

            return c

        lax.fori_loop(0, n_seg, segment, 0)

    for_each_copy(lambda cp: cp.start())
    for_each_copy(lambda cp: cp.wait())


def _zero_pads(xs2d, start, length, rb, n_rows):
    grid_spec = pltpu.PrefetchScalarGridSpec(
        num_scalar_prefetch=2,
        grid=(1,),
        in_specs=[pl.BlockSpec(memory_space=pl.ANY)],
        out_specs=pl.BlockSpec(memory_space=pl.ANY),
        scratch_shapes=[pltpu.VMEM((rb, xs2d.shape[1]), xs2d.dtype), pltpu.SemaphoreType.DMA(())],
    )
    return pl.pallas_call(
        functools.partial(_zero_pads_kernel, rb=rb, n_rows=n_rows),
        grid_spec=grid_spec,
        out_shape=jax.ShapeDtypeStruct(xs2d.shape, xs2d.dtype),
        input_output_aliases={2: 0},
        compiler_params=_cparams(("arbitrary",)),
        name="zeropads",
    )(start, length, xs2d)


def _experts_kernel(be_ref, x_ref, wg_ref, wu_ref, wd_ref, y_ref, wgu_sc, wd_sc):
    i = pl.program_id(0)
    hdim = wg_ref.shape[1]

    @pl.when((i == 0) | (be_ref[i] != be_ref[jnp.maximum(i - 1, 0)]))
    def _():
        wgu_sc[:, 0:hdim] = wg_ref[...].astype(BF16)
        wgu_sc[:, hdim:] = wu_ref[...].astype(BF16)
        wd_sc[...] = wd_ref[...].astype(BF16)

    @pl.when(i < be_ref[pl.num_programs(0)])
    def _():
        gu = jnp.dot(_load_slabs_bf16(x_ref), wgu_sc[...], preferred_element_type=F32)
        g = gu[:, :hdim]
        act = (g * jax.nn.sigmoid(g) * gu[:, hdim:]).astype(BF16)
        _store_slabs(y_ref, jnp.dot(act, wd_sc[...], preferred_element_type=F32))


def _experts(blk_e, xs, wg, wu, wd, rb):
    n_rows = xs.shape[1]
    d_model, hdim = wg.shape[1:]
    per_expert = lambda i, be: (be[i], 0, 0)
    grid_spec = pltpu.PrefetchScalarGridSpec(
        num_scalar_prefetch=1,
        grid=(n_rows // rb,),
        in_specs=[pl.BlockSpec((SC_SPLIT, rb, SC_PIECE), lambda i, be: (0, i, 0)),
                  pl.BlockSpec((None, d_model, hdim), per_expert),
                  pl.BlockSpec((None, d_model, hdim), per_expert),
                  pl.BlockSpec((None, hdim, d_model), per_expert)],
        out_specs=pl.BlockSpec((SC_SPLIT, rb, SC_PIECE), lambda i, be: (0, i, 0)),
        scratch_shapes=[pltpu.VMEM((d_model, 2 * hdim), BF16), pltpu.VMEM((hdim, d_model), BF16)],
    )
    return pl.pallas_call(
        _experts_kernel,
        grid_spec=grid_spec,
        out_shape=jax.ShapeDtypeStruct((SC_SPLIT, n_rows, SC_PIECE), U32),
        compiler_params=_cparams(("arbitrary",)),
        name="experts",
    )(blk_e, xs, wg, wu, wd)


def _sc_gather_rows(table, idx):
    n = idx.shape[0]
    width = table.shape[1]
    mesh = plsc.VectorSubcoreMesh(core_axis_name="c", subcore_axis_name="s")

    @pl.kernel(out_type=jax.ShapeDtypeStruct((n, width), table.dtype), mesh=mesh)
    def gather_kernel(t_hbm, i_hbm, o_hbm):
        def body(i_vmem, o_vmem):
            pltpu.sync_copy(t_hbm.at[i_vmem.at[0]], o_vmem)

        pltpu.emit_pipeline(
            body,
            grid=(n // SC_WINDOW,),
            in_specs=[pl.BlockSpec((1, SC_WINDOW), lambda i: (0, i))],
            out_specs=[pl.BlockSpec((SC_WINDOW, width), lambda i: (i, 0))],
            core_axis_name=("c", "s"),
            dimension_semantics=(pltpu.PARALLEL,),
        )(i_hbm, o_hbm)

    return gather_kernel(table, idx.reshape(1, n))


def _combine_kernel(yg_ref, w_ref, h_ref, x1_ref, g2_ref, fw_ref, sgu_ref, sd_ref, o_ref):
    gu = jnp.dot(_load_slabs_bf16(h_ref), sgu_ref[...], preferred_element_type=F32)
    hdim = gu.shape[1] // 2
    g = gu[:, :hdim]
    ffn = jnp.dot((g * jax.nn.sigmoid(g) * gu[:, hdim:]).astype(BF16), sd_ref[...], preferred_element_type=F32)

    w = w_ref[...]
    tt = w.shape[0]
    lo = [jnp.zeros((tt, SC_PIECE), F32) for _ in range(SC_SPLIT)]
    hi = [jnp.zeros((tt, SC_PIECE), F32) for _ in range(SC_SPLIT)]
    for k in range(MOE_TOPK):
        wk = w[:, k:k + 1]
        for s in range(SC_SPLIT):
            y_lo, y_hi = _unpack_halves(yg_ref[s, k])
            lo[s] = lo[s] + wk * y_lo
            hi[s] = hi[s] + wk * y_hi
    ffn = ffn + jnp.concatenate(lo + hi, axis=1)
    x2 = x1_ref[...] + g2_ref[...] * ffn
    r = lax.rsqrt(jnp.mean(x2 * x2, axis=-1, keepdims=True) + NORM_EPS)
    o_ref[...] = x2 * r * fw_ref[...]


def _combine(yg, top_w, h2p, x1, g2, fw, sgu, sd, S, tt):
    N, D = x1.shape
    per_b = S // tt
    row = lambda i: (i, 0)
    full = lambda i: (0, 0)
    return pl.pallas_call(
        _combine_kernel,
        grid=(N // tt,),
        in_specs=[pl.BlockSpec((SC_SPLIT, MOE_TOPK, tt, SC_PIECE), lambda i: (0, 0, i, 0)),
                  pl.BlockSpec((tt, MOE_TOPK), row),
                  pl.BlockSpec((SC_SPLIT, tt, SC_PIECE), lambda i: (0, i, 0)),
                  pl.BlockSpec((tt, D), row),
                  pl.BlockSpec((None, 1, D), lambda i: (i // per_b, 0, 0)),
                  pl.BlockSpec((1, D), full),
                  pl.BlockSpec(sgu.shape, full),
                  pl.BlockSpec(sd.shape, full)],
        out_specs=pl.BlockSpec((tt, D), row),
        out_shape=jax.ShapeDtypeStruct((N, D), F32),
        compiler_params=_cparams(("parallel",)),
        name="combine",
    )(yg, top_w, h2p, x1, g2.reshape(-1, 1, D), fw.reshape(1, D), sgu, sd)


def _cmp_to_sel_T(n_cp, n_sel):
    c0 = np.arange(n_cp)[None, :] * CMP_STRIDE
    s0 = np.arange(n_sel)[:, None] * SEL_BLOCK
    ov = np.minimum(c0 + CMP_BLOCK, s0 + SEL_BLOCK) - np.maximum(c0, s0)
    m = np.clip(ov, 0, None).astype(np.float32) / CMP_BLOCK
    m[:, n_cp - 1] = 0.0
    return jnp.asarray(m, BF16)


def _slab_weights(w_in):
    D = w_in.shape[0]
    sizes = [1024, 1024, 1024, 512, 128, 128, 128, 128, 128, 128, 24, 2048]
    offs = np.concatenate([[0], np.cumsum(sizes)])
    part = lambda i: w_in[:, offs[i]:offs[i + 1]]
    pieces = [part(0), part(1), part(2), part(11), part(3)] + [part(i) for i in range(4, 10)] + [part(10)]
    w = jnp.concatenate(pieces, axis=1)
    return jnp.pad(w, ((0, 0), (0, SLAB_COLS - w.shape[1]))).astype(BF16)


def _layer(x, c, ada_w, ada_b, norm1_w, w_in, da_lq1, da_lk1, da_lq2, da_lk2, da_subln_w,
           cmp_k_pe, cmp_k_w1, cmp_k_w2, cmp_v_pe, cmp_v_w1, cmp_v_w2, w_da_out, w_nsa_out, w_o,
           norm2_w, router_w, router_b, exp_w_gate, exp_w_up, exp_w_down,
           sh_w_gate, sh_w_up, sh_w_down, final_norm_w, lam_init):
    B, S, D = x.shape
    N = B * S
    G, HPG = NSA_GROUPS, NSA_HPG
    x2 = x.reshape(N, D)

    mod = _ada(c, ada_w, ada_b)
    sh1, sc1, g1, sh2, sc2, g2 = jnp.split(mod, 6, axis=-1)

    slab = _inproj(x2, norm1_w, sh1, sc1, _slab_weights(w_in), S)
    slab3 = slab.reshape(B, S, SLAB_COLS)

    i_all = np.arange(1, DA_HEADS + NSA_HEADS + 1, dtype=np.float32)
    slopes = (2.0 ** (-8.0 * i_all / (DA_HEADS + NSA_HEADS))).astype(np.float32)
    slopes_a = jnp.asarray(slopes[0::2])
    slopes_b = jnp.asarray(slopes[1::2])

    ta_q, ta_k = min(512, S), min(512, S)
    vT = slab3[:, :, COL_DAV:COL_DAV + 1024].reshape(B, S // ta_k, ta_k, DA_HEADS, DA_VDIM).transpose(0, 3, 1, 4, 2)
    kmin = _first_live_tile(slopes[0::2], slab3, ta_q, ta_k)
    oa = _diff_attention(slopes_a, kmin, slab3, _with_ones_row(vT), _slope_rows(slopes[0::2]), da_lq1, da_lk1,
                         da_lq2, da_lk2, da_subln_w, lam_init, ta_q, ta_k).reshape(N, DA_HEADS * DA_VDIM)

    tb, tb_k = 256, min(256, S)
    n_cp = S // CMP_STRIDE
    n_sel = S // SEL_BLOCK

    def kv_groups(j):
        c0 = COL_KV6 + 128 * j
        return slab3[:, :, c0:c0 + 128].reshape(B, S, G, NSA_DIM).transpose(0, 2, 1, 3)

    def kv_tiles_T(a, rows):
        return a.reshape(B, G, S // rows, rows, NSA_DIM).transpose(0, 1, 2, 4, 3)

    ck_rows = kv_groups(0).reshape(B * G, n_cp, CMP_STRIDE * NSA_DIM)
    cv_rows = kv_groups(1).reshape(B * G, n_cp, CMP_STRIDE * NSA_DIM)
    kc = _compress(ck_rows, cmp_k_pe, cmp_k_w1, cmp_k_w2).reshape(B, G, n_cp, NSA_DIM).astype(BF16)
    vc = _compress(cv_rows, cmp_v_pe, cmp_v_w1, cmp_v_w2).reshape(B, G, n_cp, NSA_DIM).astype(BF16)
    nq = (slab3[:, :, COL_NSAQ:COL_NSAQ + 512] * jnp.asarray(NSA_DIM ** -0.5, BF16))
    nqT = nq.reshape(B, S // tb, tb, G, HPG, NSA_DIM).transpose(0, 3, 1, 5, 4, 2).reshape(B, G, S // tb, NSA_DIM, HPG * tb)
    gT = (slab3[:, :, COL_NSAG:COL_NSAG + 3 * NSA_HEADS].astype(F32).reshape(B, S // tb, tb, G, HPG, 3)
          .transpose(0, 3, 1, 5, 4, 2).reshape(B, G, S // tb, 3, HPG * tb))
    onehot = jnp.asarray((np.arange(S)[:, None] // SEL_BLOCK == np.arange(SEL_COLS)[None, :]).astype(np.float32), BF16)
    zeros64 = jnp.zeros((B, G, S, NSA_DIM), BF16)
    ksa = jnp.concatenate([kv_groups(2), zeros64, jnp.broadcast_to(onehot, (B, G, S, SEL_COLS))], axis=-1)
    pad_rows = jnp.zeros((B, G, WINDOW, 2 * NSA_DIM), BF16).at[..., NSA_DIM].set(1.0)
    kwa = jnp.concatenate([pad_rows, jnp.concatenate([kv_groups(4), zeros64], axis=-1)], axis=2)
    vw_pad = jnp.pad(kv_groups(5), ((0, 0), (0, 0), (WINDOW, 0), (0, 0)))
    vwT = vw_pad.reshape(B, G, (S + WINDOW) // tb, tb, NSA_DIM).transpose(0, 1, 2, 4, 3)
    r_w = np.arange(WINDOW + tb)[:, None]
    lane_w = np.arange(HPG * tb)[None, :]
    d_w = (WINDOW + lane_w % tb - r_w).astype(np.float32)
    slope_w = slopes[1::2].reshape(G, 1, HPG)[:, :, lane_w[0] // tb]
    wb = jnp.asarray(np.where((d_w >= 0) & (d_w < WINDOW), -slope_w * d_w[None], np.float32(NEG)).astype(np.float32))
    obT = _nsa_attention(slopes_b, nqT, kc, vc.transpose(0, 1, 3, 2), ksa, kv_tiles_T(kv_groups(3), tb_k),
                         kwa, vwT, wb, gT, _cmp_to_sel_T(n_cp, n_sel), tb, tb_k)
    ob = (obT.reshape(B, G, S // tb, NSA_DIM, HPG, tb).transpose(0, 2, 5, 1, 4, 3)
          .reshape(N, NSA_HEADS * NSA_DIM))

    wr_hi = router_w.astype(BF16)
    wr_lo = (router_w.astype(F32) - wr_hi.astype(F32)).astype(BF16)
    x1, h2, logits = _merge(oa, ob, slab, x2, g1, norm2_w, sh2, sc2, w_da_out.astype(BF16),
                            w_nsa_out.astype(BF16), w_o.astype(BF16), jnp.stack([wr_hi, wr_lo]), S)

    rb = 512
    tc = 256
    top_eT, top_wT = _route(logits.T, router_b)
    destT, counts = _slots(top_eT, rb)
    n_rows = ((N * MOE_TOPK + N_EXPERTS * (rb - 1) + rb - 1) // rb) * rb
    padded = (jnp.ceil(counts[:, 0] / rb) * rb).astype(I32)
    pend = jnp.cumsum(padded)
    blk_start = jnp.arange(n_rows // rb, dtype=I32) * rb
    blk_e = jnp.minimum(jnp.sum((pend[None, :] <= blk_start[:, None]).astype(I32), axis=1), N_EXPERTS - 1)
    d_flat = destT.reshape(-1)
    piece_idx = jnp.concatenate([d_flat + s * n_rows for s in range(SC_SPLIT)])
    xs2d = _sc_scatter_rows(h2.reshape(SC_SPLIT * N, SC_PIECE), piece_idx, SC_SPLIT * n_rows, MOE_TOPK)
    cnt = counts[:, 0].astype(I32)
    pad_start = jnp.concatenate([pend - padded + cnt, pend[-1:]])
    pad_len = jnp.concatenate([padded - cnt, n_rows - pend[-1:]])
    xs = _zero_pads(xs2d, pad_start, pad_len, rb, n_rows).reshape(SC_SPLIT, n_rows, SC_PIECE)
    blk_info = jnp.concatenate([blk_e, pend[-1:] // rb])
    ys = _experts(blk_info, xs, exp_w_gate, exp_w_up, exp_w_down, rb)
    sgu = jnp.concatenate([sh_w_gate, sh_w_up], axis=-1).astype(BF16)
    yg = _sc_gather_rows(ys.reshape(SC_SPLIT * n_rows, SC_PIECE), piece_idx).reshape(SC_SPLIT, MOE_TOPK, N, SC_PIECE)
    out = _combine(yg, top_wT.T, h2, x1, g2, final_norm_w, sgu, sh_w_down.astype(BF16), S, tc)
    return out.reshape(B, S, D)


def kernel(x, c, ada_w, ada_b, norm1_w, w_in, da_lq1, da_lk1, da_lq2, da_lk2, da_subln_w, cmp_k_pe, cmp_k_w1, cmp_k_w2, cmp_v_pe, cmp_v_w1, cmp_v_w2, w_da_out, w_nsa_out, w_o, norm2_w, router_w, router_b, exp_w_gate, exp_w_up, exp_w_down, sh_w_gate, sh_w_up, sh_w_down, final_norm_w):
    depth = ada_w.shape[0]
    assert depth == 1, "one decoder layer"
    lam_init = 0.8 - 0.6 * math.exp(-0.3 * 0)
    return _layer(x, c, ada_w[0], ada_b[0], norm1_w[0], w_in[0], da_lq1[0], da_lk1[0], da_lq2[0], da_lk2[0],
                  da_subln_w[0], cmp_k_pe[0], cmp_k_w1[0], cmp_k_w2[0], cmp_v_pe[0], cmp_v_w1[0], cmp_v_w2[0],
                  w_da_out[0], w_nsa_out[0], w_o[0], norm2_w[0], router_w[0], router_b[0],
                  exp_w_gate[0], exp_w_up[0], exp_w_down[0], sh_w_gate[0], sh_w_up[0], sh_w_down[0],
                  final_norm_w, lam_init)
```

```python
import functools
import math

import numpy as np
import jax
import jax.numpy as jnp
from jax import lax
from jax.experimental import pallas as pl
from jax.experimental.pallas import tpu as pltpu
from jax.experimental.pallas import tpu_sc as plsc

F32 = jnp.float32
BF16 = jnp.bfloat16
I32 = jnp.int32
U32 = jnp.uint32
HIGHEST = lax.Precision.HIGHEST

NORM_EPS = 1e-6
NEG = -1e30

DA_HEADS = 8
DA_DIM = 64
DA_VDIM = 128
DA_SUBLN_EPS = 1e-5

NSA_HEADS = 8
NSA_GROUPS = 2
NSA_HPG = 4
NSA_DIM = 64
CMP_BLOCK = 32
CMP_STRIDE = 16
SEL_BLOCK = 64
SEL_TOPK = 16
WINDOW = 512
FORCE_BONUS = 1e4
SEL_COLS = 128

N_EXPERTS = 64
MOE_TOPK = 8
N_EXPERT_GROUPS = 8
TOPK_EXPERT_GROUPS = 4
ROUTED_SCALE = 2.5

VMEM_LIMIT_V7X = 56 * 1024 * 1024

COL_DAQ, COL_DAK, COL_DAV = 0, 1024, 2048
COL_MERGE = 3072
COL_NSAQ = 5120
COL_KV6 = 5632
COL_NSAG = 6400
SLAB_COLS = 6528


SUBLANES = 8
SC_WINDOW = 128
SC_SPLIT = 2
SC_PIECE = 256

POS_SPLIT = 16
POS_ROWS = 16
ONES_ROWS = 16


def _pos_features(rows):
    r = np.arange(rows)
    f = np.zeros((rows, 128), np.float32)
    f[:, 0:3] = (r // POS_SPLIT)[:, None]
    f[:, 3:6] = (r % POS_SPLIT)[:, None]
    return jnp.asarray(f, BF16)


def _slope_rows(slopes):
    s = np.asarray(slopes, np.float32)
    bf = lambda x: x.astype(BF16).astype(np.float32)
    p1 = bf(s)
    p2 = bf(s - p1)
    p3 = bf(s - p1 - p2)
    out = np.zeros((s.shape[0], POS_ROWS, 128), np.float32)
    for i, piece in enumerate((p1, p2, p3)):
        out[:, i, :] = POS_SPLIT * piece[:, None]
        out[:, 3 + i, :] = piece[:, None]
    return jnp.asarray(out)


def _with_ones_row(vT):
    shape = vT.shape[:-2] + (ONES_ROWS, vT.shape[-1])
    extra = jnp.zeros(shape, vT.dtype).at[..., 0, :].set(1.0)
    return jnp.concatenate([vT, extra], axis=-2)


def _cparams(sem, vmem=VMEM_LIMIT_V7X):
    return pltpu.CompilerParams(dimension_semantics=sem, vmem_limit_bytes=vmem)


def _ada_kernel(c_ref, w_ref, b_ref, o_ref):
    c = c_ref[...]
    ca = c * jax.nn.sigmoid(c)
    o_ref[...] = jnp.dot(ca, w_ref[...], preferred_element_type=F32, precision=HIGHEST) + b_ref[...]


def _ada(c, w, b):
    B, D = c.shape
    n_out = w.shape[1]
    rows = 8
    cp = jnp.zeros((rows, D), F32).at[:B].set(c)
    tn = 1024
    out = pl.pallas_call(
        _ada_kernel,
        grid=(n_out // tn,),
        in_specs=[pl.BlockSpec((rows, D), lambda j: (0, 0)),
                  pl.BlockSpec((D, tn), lambda j: (0, j)),
                  pl.BlockSpec((1, tn), lambda j: (0, j))],
        out_specs=pl.BlockSpec((rows, tn), lambda j: (0, j)),
        out_shape=jax.ShapeDtypeStruct((rows, n_out), F32),
        compiler_params=_cparams(("arbitrary",)),
        name="ada",
    )(cp, w, b.reshape(1, n_out))
    return out[:B]


def _modulate(x, nw, sh, sc):
    r = lax.rsqrt(jnp.mean(x * x, axis=-1, keepdims=True) + NORM_EPS)
    return (x * r) * nw * (1.0 + sc) + sh


def _inproj_kernel(x_ref, nw_ref, sh_ref, sc_ref, w_ref, o_ref, h_ref):
    @pl.when(pl.program_id(1) == 0)
    def _():
        h_ref[...] = _modulate(x_ref[...], nw_ref[...], sh_ref[...], sc_ref[...]).astype(BF16)

    o_ref[...] = jnp.dot(h_ref[...], w_ref[...], preferred_element_type=F32).astype(BF16)


def _inproj(x2, nw, sh, sc, w_slab, S):
    N, D = x2.shape
    tm = min(1024, S)
    tn = SLAB_COLS // 3
    per_b = S // tm
    return pl.pallas_call(
        _inproj_kernel,
        grid=(N // tm, SLAB_COLS // tn),
        in_specs=[pl.BlockSpec((tm, D), lambda i, j: (i, 0)),
                  pl.BlockSpec((1, D), lambda i, j: (0, 0)),
                  pl.BlockSpec((None, 1, D), lambda i, j: (i // per_b, 0, 0)),
                  pl.BlockSpec((None, 1, D), lambda i, j: (i // per_b, 0, 0)),
                  pl.BlockSpec((D, tn), lambda i, j: (0, j))],
        out_specs=pl.BlockSpec((tm, tn), lambda i, j: (i, j)),
        out_shape=jax.ShapeDtypeStruct((N, SLAB_COLS), BF16),
        scratch_shapes=[pltpu.VMEM((tm, D), BF16)],
        compiler_params=_cparams(("parallel", "arbitrary")),
        name="inproj",
    )(x2, nw.reshape(1, D), sh.reshape(-1, 1, D), sc.reshape(-1, 1, D), w_slab)


def _cmp_kernel(r_ref, pe_ref, w1a_ref, w1b_ref, w2_ref, o_ref):
    r = r_ref[...]
    n_rows = r.shape[0]
    a = jnp.dot(r, w1a_ref[...], preferred_element_type=F32)
    b = jnp.dot(r, w1b_ref[...], preferred_element_type=F32)
    pe = pe_ref[...]
    half = pe.shape[1] // 2
    bias = (jnp.dot(pe[:, :half], w1a_ref[...].astype(F32), preferred_element_type=F32, precision=HIGHEST)
            + jnp.dot(pe[:, half:], w1b_ref[...].astype(F32), preferred_element_type=F32, precision=HIGHEST))
    hid = a + pltpu.roll(b, n_rows - 1, 0) + bias
    act = 0.5 * hid * (1.0 + jnp.tanh(0.7978845608028654 * (hid + 0.044715 * hid * hid * hid)))
    o_ref[...] = jnp.dot(act, w2_ref[...], preferred_element_type=F32, precision=HIGHEST)


def _compress(rows, pe, w1, w2):
    BG, n_rows, width = rows.shape
    hidden = w1.shape[1]
    w1a = w1[:width].astype(BF16)
    w1b = w1[width:].astype(BF16)
    return pl.pallas_call(
        _cmp_kernel,
        grid=(BG,),
        in_specs=[pl.BlockSpec((None, n_rows, width), lambda i: (i, 0, 0)),
                  pl.BlockSpec((1, 2 * width), lambda i: (0, 0)),
                  pl.BlockSpec((width, hidden), lambda i: (0, 0)),
                  pl.BlockSpec((width, hidden), lambda i: (0, 0)),
                  pl.BlockSpec((hidden, NSA_DIM), lambda i: (0, 0))],
        out_specs=pl.BlockSpec((None, n_rows, NSA_DIM), lambda i: (i, 0, 0)),
        out_shape=jax.ShapeDtypeStruct((BG, n_rows, NSA_DIM), F32),
        compiler_params=_cparams(("parallel",)),
        name="cmp",
    )(rows, pe.reshape(1, -1).astype(F32), w1a, w1b, w2.astype(F32))


def _da_kernel(slopes_ref, kmin_ref, q_ref, k_ref, vT_ref, pos_ref, rel_ref, srow_ref, lq1_ref, lk1_ref, lq2_ref,
               lk2_ref, subw_ref, o_ref, acc_ref, qa_ref, sa_ref, sb_ref, pa_ref, pb_ref, *, tq, tk, lam_init):
    h = pl.program_id(1)
    qi = pl.program_id(2)
    slope = slopes_ref[h]
    q0 = qi * tq
    w = 2 * tq
    nk = k_ref.shape[0] // tk
    qT = (q_ref[...].astype(F32) * DA_DIM ** -0.5).T.astype(BF16)
    row = lax.broadcasted_iota(I32, qT.shape, 0)
    zero = jnp.zeros_like(qT)
    qa_ref[0:2 * DA_DIM, :] = jnp.concatenate(
        [jnp.where(row < DA_DIM, qT, zero), jnp.where(row >= DA_DIM, qT, zero)], axis=1)
    qa_ref[2 * DA_DIM:2 * DA_DIM + POS_ROWS, :] = jnp.concatenate([srow_ref[...]] * (w // 128), axis=1).astype(BF16)
    qa_ref[2 * DA_DIM + POS_ROWS:, :] = jnp.zeros((2 * DA_DIM - POS_ROWS, w), BF16)

    def scores(kt, s_ref):
        k0 = kt * tk
        k_t = k_ref[pl.ds(pl.multiple_of(k0, tk), tk), :]
        s = jnp.dot(jnp.concatenate([k_t, pos_ref[...]], axis=1), qa_ref[...], preferred_element_type=F32)
        s = jnp.where(rel_ref[...] <= q0 - k0, s, NEG)
        s_ref[...] = s
        return jnp.max(s, axis=0, keepdims=True)

    def probs(s_ref, p_ref, mx, off, m_old):
        m_new = jnp.maximum(m_old, mx + off)
        p_ref[...] = jnp.exp(s_ref[...] - (m_new - off)).astype(BF16)
        return m_new, jnp.exp(m_old - m_new)

    def accumulate(kt, p_ref, alpha):
        acc_ref[...] = alpha * acc_ref[...] + jnp.dot(vT_ref[kt], p_ref[...], preferred_element_type=F32)

    n_tiles = (q0 + tq - 1) // tk + 1
    kt_min = kmin_ref[(pl.program_id(0) * pl.num_programs(1) + h) * pl.num_programs(2) + qi]
    acc_ref[...] = jnp.zeros_like(acc_ref)
    pb_ref[...] = jnp.zeros_like(pb_ref)
    mx_a = scores(kt_min, sa_ref)

    def pair(j, carry):
        m, mx_a, alpha_b = carry
        ta = kt_min + 2 * j
        tb = ta + 1
        accumulate(jnp.maximum(ta - 1, 0), pb_ref, alpha_b)
        m, alpha_a = probs(sa_ref, pa_ref, mx_a, slope * (ta * tk).astype(F32), m)
        mx_b = scores(tb, sb_ref)
        accumulate(ta, pa_ref, alpha_a)
        m, alpha_b = probs(sb_ref, pb_ref, mx_b, slope * (tb * tk).astype(F32), m)
        mx_a = scores(jnp.minimum(ta + 2, nk - 1), sa_ref)
        return m, mx_a, alpha_b

    n_live = n_tiles - kt_min
    n_pairs = n_live // 2
    init = (jnp.full((1, w), NEG, F32), mx_a, jnp.ones((1, w), F32))
    m, mx_a, alpha_b = lax.fori_loop(0, n_pairs, pair, init)
    last = n_tiles - 1

    @pl.when(n_live % 2 == 0)
    def _():
        accumulate(last, pb_ref, alpha_b)

    @pl.when(n_live % 2 == 1)
    def _():
        accumulate(jnp.maximum(last - 1, 0), pb_ref, alpha_b)
        _, alpha_a = probs(sa_ref, pa_ref, mx_a, slope * (last * tk).astype(F32), m)
        accumulate(last, pa_ref, alpha_a)

    lam = (jnp.exp(jnp.sum(lq1_ref[...] * lk1_ref[...], axis=-1, keepdims=True))
           - jnp.exp(jnp.sum(lq2_ref[...] * lk2_ref[...], axis=-1, keepdims=True)) + lam_init)
    on = acc_ref[0:DA_VDIM, :] * (1.0 / acc_ref[DA_VDIM:DA_VDIM + 1, :])
    o = on[:, :tq] - lam * on[:, tq:]
    r = lax.rsqrt(jnp.mean(o * o, axis=0, keepdims=True) + DA_SUBLN_EPS)
    o_ref[...] = (o * r * subw_ref[...] * (1.0 - lam_init)).T.astype(BF16)


EXP_UNDERFLOW = 104.0


def _first_live_tile(slopes_np, slab3, tq, tk):
    B, S, _ = slab3.shape
    H = DA_HEADS
    width = 2 * DA_DIM * H
    grp = jnp.asarray((np.arange(width)[:, None] // DA_DIM == np.arange(2 * H)[None, :]).astype(np.float32), BF16)

    def norms(c0):
        v = slab3[:, :, c0:c0 + width]
        return jnp.sqrt(jnp.einsum('bsc,cg->bsg', v * v, grp, preferred_element_type=F32))

    qn = norms(COL_DAQ) * DA_DIM ** -0.5
    qn = jnp.max(qn.reshape(B, S // tq, tq, H, 2), axis=2).transpose(0, 2, 3, 1)
    kn = jnp.max(norms(COL_DAK), axis=1).reshape(B, H, 2)
    bound = jnp.max(qn * kn[..., None], axis=2) * 1.01 + 0.01
    q0 = (jnp.arange(S // tq, dtype=F32) * tq)[None, None, :]
    slope = jnp.asarray(slopes_np, F32)[None, :, None]
    first_key = q0 - (EXP_UNDERFLOW + 2.0 * bound) / slope
    kt = jnp.floor(first_key / tk).astype(I32)
    n_tiles = (jnp.arange(S // tq, dtype=I32) * tq + tq - 1) // tk + 1
    return jnp.clip(kt, 0, n_tiles[None, None, :] - 1).reshape(-1)


def _diff_attention(slopes, kmin, slab3, vT, srows, lq1, lk1, lq2, lk2, subw, lam_init, tq, tk):
    B, S, _ = slab3.shape
    nk = S // tk
    assert tk % tq == 0 and S % tk == 0
    qcol = COL_DAQ // 128
    tile = tq
    kcol = COL_DAK // 128
    v_rows = vT.shape[3]
    vec = lambda a: a.reshape(1, DA_DIM).astype(F32)
    grid_spec = pltpu.PrefetchScalarGridSpec(
        num_scalar_prefetch=2,
        grid=(B, DA_HEADS, S // tile),
        in_specs=[pl.BlockSpec((None, tile, 128), lambda b, h, i, s, km: (b, i, qcol + h)),
                  pl.BlockSpec((None, S, 128), lambda b, h, i, s, km: (b, 0, kcol + h)),
                  pl.BlockSpec((None, None, nk, v_rows, tk), lambda b, h, i, s, km: (b, h, 0, 0, 0)),
                  pl.BlockSpec((tk, 128), lambda b, h, i, s, km: (0, 0)),
                  pl.BlockSpec((tk, 2 * tq), lambda b, h, i, s, km: (0, 0)),
                  pl.BlockSpec((None, POS_ROWS, 128), lambda b, h, i, s, km: (h, 0, 0)),
                  pl.BlockSpec((1, DA_DIM), lambda b, h, i, s, km: (0, 0)),
                  pl.BlockSpec((1, DA_DIM), lambda b, h, i, s, km: (0, 0)),
                  pl.BlockSpec((1, DA_DIM), lambda b, h, i, s, km: (0, 0)),
                  pl.BlockSpec((1, DA_DIM), lambda b, h, i, s, km: (0, 0)),
                  pl.BlockSpec((DA_VDIM, 1), lambda b, h, i, s, km: (0, 0))],
        out_specs=pl.BlockSpec((None, tile, 128), lambda b, h, i, s, km: (b, i, h)),
        scratch_shapes=[pltpu.VMEM((v_rows, 2 * tq), F32), pltpu.VMEM((4 * DA_DIM, 2 * tq), BF16),
                        pltpu.VMEM((tk, 2 * tq), F32), pltpu.VMEM((tk, 2 * tq), F32),
                        pltpu.VMEM((tk, 2 * tq), BF16), pltpu.VMEM((tk, 2 * tq), BF16)],
    )
    rel = jnp.asarray(np.arange(tk)[:, None] - (np.arange(2 * tq) % tq)[None, :], I32)
    return pl.pallas_call(
        functools.partial(_da_kernel, tq=tq, tk=tk, lam_init=lam_init),
        grid_spec=grid_spec,
        out_shape=jax.ShapeDtypeStruct((B, S, DA_HEADS * DA_VDIM), BF16),
        compiler_params=_cparams(("parallel", "parallel", "arbitrary")),
        name="diffattn",
    )(slopes, kmin, slab3, slab3, vT, _pos_features(tk), rel, srows, vec(lq1), vec(lk1), vec(lq2), vec(lk2),
      subw.reshape(DA_VDIM, 1).astype(F32))


def _nsa_kernel(slopes_ref, qT_ref, kc_ref, vcT_ref, ksa_ref, vsT_ref, kwa_ref, vwT_ref, wb_ref, gT_ref, mT_ref,
                grp_ref, ltri_ref, oT_ref, acc_ref, res_ref, base_ref, qa_ref, sa_ref, sb_ref, pa_ref, pb_ref,
                list_ref, slist_ref, lsem, imp_ref, *, t, tk, n_top):
    g = pl.program_id(1)
    qi = pl.program_id(2)
    q0 = qi * t
    w = NSA_HPG * t
    n_cp = kc_ref.shape[0]
    n_sel = mT_ref.shape[0]
    lane1 = lax.broadcasted_iota(I32, (1, w), 1)
    slope_row = jnp.zeros((1, w), F32)
    for hh in range(NSA_HPG):
        slope_row = jnp.where((lane1 >= hh * t) & (lane1 < (hh + 1) * t), slopes_ref[g * NSA_HPG + hh], slope_row)
    q_all = qT_ref[...]
    gates = jax.nn.sigmoid(gT_ref[...])
    key_off = lax.broadcasted_iota(I32, (tk, w), 0)
    t_loc = jnp.bitwise_and(lax.broadcasted_iota(I32, (tk, w), 1), t - 1)
    base_ref[...] = slope_row * key_off.astype(F32)

    def compressed(rows):
        c_idx = lax.broadcasted_iota(I32, (rows, w), 0)
        t_c = q0 + jnp.bitwise_and(lax.broadcasted_iota(I32, (rows, w), 1), t - 1)
        d_c = t_c - (c_idx * CMP_STRIDE + (CMP_BLOCK - 1))
        ok_c = d_c >= 0
        s = jnp.dot(kc_ref[0:rows, :], q_all, preferred_element_type=F32) - slope_row * d_c.astype(F32)
        s = jnp.where(ok_c, s, NEG)
        e = jnp.exp(s - jnp.max(s, axis=0, keepdims=True))
        inv = 1.0 / jnp.sum(e, axis=0, keepdims=True)
        p = jnp.where(ok_c, e * inv, 0.0)
        res_ref[...] = gates[0:1, :] * jnp.dot(vcT_ref[:, 0:rows], p.astype(BF16), preferred_element_type=F32)
        p_sum = p[:, 0:t]
        for hh in range(1, NSA_HPG):
            p_sum = p_sum + p[:, hh * t:(hh + 1) * t]
        p_hi = p_sum.astype(BF16)
        p_lo = (p_sum - p_hi.astype(F32)).astype(BF16)
        mT = mT_ref[:, 0:rows]
        imp_ref[...] = jnp.dot(mT, p_hi, preferred_element_type=F32) + jnp.dot(mT, p_lo, preferred_element_type=F32)

    chunk = min(n_cp, 128)
    n_chunk = jnp.minimum((q0 + t - CMP_BLOCK) // (CMP_STRIDE * chunk) + 1, n_cp // chunk)
    for c in range(1, n_cp // chunk + 1):
        pl.when(n_chunk == c)(functools.partial(compressed, c * chunk))

    imp = imp_ref[...]
    blk = lax.broadcasted_iota(I32, (n_sel, t), 0)
    t_s = q0 + lax.broadcasted_iota(I32, (n_sel, t), 1)
    cur = jnp.right_shift(t_s, int(math.log2(SEL_BLOCK)))
    forced = (blk == 0) | (blk == cur) | (blk == cur - 1)
    score = jnp.where(blk * SEL_BLOCK <= t_s, imp + jnp.where(forced, FORCE_BONUS, 0.0), NEG)
    blk_f = blk.astype(F32)
    sel = jnp.zeros((n_sel, t), F32)
    for _ in range(n_top):
        _, idx = _first_max(score, blk_f)
        hit = blk_f == idx
        sel = jnp.where(hit, 1.0, sel)
        score = jnp.where(hit, -jnp.inf, score)
    n_full = q0 // tk
    nt_pad = grp_ref.shape[0]
    cnt = jnp.broadcast_to(jnp.sum(sel, axis=1, keepdims=True), (n_sel, 128)).astype(BF16)
    tile_cnt = jnp.dot(grp_ref[...], cnt, preferred_element_type=F32)
    kt_i = lax.broadcasted_iota(I32, (nt_pad, 128), 0)
    active = (tile_cnt > 0.5) & (kt_i < n_full)
    act_f = jnp.where(active, 1.0, 0.0)
    before = jnp.dot(ltri_ref[...], act_f.astype(BF16), preferred_element_type=F32)
    lane_j = lax.broadcasted_iota(I32, (nt_pad, 128), 1)
    slot_hit = active & (before == lane_j.astype(F32))
    tiles_row = jnp.sum(jnp.where(slot_hit, kt_i.astype(F32), 0.0), axis=0, keepdims=True)
    n_act_row = jnp.sum(act_f, axis=0, keepdims=True)
    lane8 = lax.broadcasted_iota(I32, (8, 128), 1)
    list_ref[...] = jnp.where(lane8 == 127, n_act_row, tiles_row).astype(I32)
    list_copy = pltpu.make_async_copy(list_ref, slist_ref, lsem)
    list_copy.start()

    selb = jnp.where(sel > 0.5, 0.0, NEG)
    row64 = lax.broadcasted_iota(I32, (NSA_DIM, w), 0)
    qa_ref[0:NSA_DIM, :] = q_all
    qa_ref[NSA_DIM:2 * NSA_DIM, :] = jnp.where(row64 == 0, NEG, 0.0).astype(BF16)
    qa_ref[2 * NSA_DIM:2 * NSA_DIM + n_sel, :] = jnp.concatenate([selb] * NSA_HPG, axis=1).astype(BF16)
    if n_sel < SEL_COLS:
        qa_ref[2 * NSA_DIM + n_sel:, :] = jnp.zeros((SEL_COLS - n_sel, w), BF16)

    n_wt = (WINDOW + t) // t
    kw_t = kwa_ref[pl.ds(pl.multiple_of(q0, t), WINDOW + t), :]
    s = jnp.dot(kw_t, qa_ref[0:2 * NSA_DIM, :], preferred_element_type=F32) + wb_ref[...]
    p = jnp.exp(s - jnp.max(s, axis=0, keepdims=True))
    inv = 1.0 / jnp.sum(p, axis=0, keepdims=True)
    p = p.astype(BF16)
    o_w = jnp.dot(vwT_ref[qi], p[0:t, :], preferred_element_type=F32)
    for c in range(1, n_wt):
        o_w = o_w + jnp.dot(vwT_ref[qi + c], p[c * t:(c + 1) * t, :], preferred_element_type=F32)
    res_ref[...] = res_ref[...] + gates[2:3, :] * (o_w * inv)

    def scores(kt, s_ref):
        k_t = ksa_ref[pl.ds(pl.multiple_of(kt * tk, tk), tk), :]
        s = jnp.dot(k_t, qa_ref[...], preferred_element_type=F32) + base_ref[...]
        s_ref[...] = s
        return jnp.max(s, axis=0, keepdims=True)

    def probs(s_ref, p_ref, mx, off, m_old, l_old):
        m_new = jnp.maximum(m_old, mx + off)
        p = jnp.exp(s_ref[...] - (m_new - off))
        alpha = jnp.exp(m_old - m_new)
        p_ref[...] = p.astype(BF16)
        return m_new, alpha * l_old + jnp.sum(p, axis=0, keepdims=True), alpha

    def accumulate(kt, p_ref, alpha):
        acc_ref[...] = alpha * acc_ref[...] + jnp.dot(vsT_ref[kt], p_ref[...], preferred_element_type=F32)

    list_copy.wait()
    n_act = slist_ref[0, 127]
    mx_a = scores(slist_ref[0, 0], sa_ref)

    kd = n_full * tk
    s = jnp.dot(ksa_ref[pl.ds(pl.multiple_of(kd, tk), tk), :], qa_ref[...], preferred_element_type=F32) + base_ref[...]
    s = jnp.where(key_off - t_loc <= q0 - kd, s, NEG)
    off_d = slope_row * kd.astype(F32)
    m = jnp.max(s, axis=0, keepdims=True) + off_d
    p = jnp.exp(s - (m - off_d))
    l = jnp.sum(p, axis=0, keepdims=True)
    acc_ref[...] = jnp.dot(vsT_ref[n_full], p.astype(BF16), preferred_element_type=F32)
    pb_ref[...] = jnp.zeros_like(pb_ref)

    def pair(j, carry):
        m, l, mx_a, alpha_b, tb_prev = carry
        ta = slist_ref[0, 2 * j]
        tb = slist_ref[0, 2 * j + 1]
        off_a = slope_row * (ta * tk).astype(F32)
        off_b = jnp.where(2 * j + 1 < n_act, slope_row * (tb * tk).astype(F32), NEG)
        accumulate(tb_prev, pb_ref, alpha_b)
        m, l, alpha_a = probs(sa_ref, pa_ref, mx_a, off_a, m, l)
        mx_b = scores(tb, sb_ref)
        accumulate(ta, pa_ref, alpha_a)
        m, l, alpha_b = probs(sb_ref, pb_ref, mx_b, off_b, m, l)
        mx_a = scores(slist_ref[0, 2 * j + 2], sa_ref)
        return m, l, mx_a, alpha_b, tb

    n_pairs = (n_act + 1) // 2
    m, l, _, alpha_b, tb_last = lax.fori_loop(0, n_pairs, pair, (m, l, mx_a, jnp.ones((1, w), F32), jnp.int32(0)))
    accumulate(tb_last, pb_ref, alpha_b)
    oT_ref[...] = (res_ref[...] + gates[1:2, :] * (acc_ref[...] * (1.0 / l))).astype(BF16)


def _nsa_attention(slopes, qT, kc, vcT, ksa, vsT, kwa, vwT, wb, gT, mT, t, tk):
    B, G, nq, _, w = qT.shape
    S = nq * t
    n_cp = kc.shape[2]
    n_sel = S // SEL_BLOCK
    n_top = min(SEL_TOPK, n_sel)
    assert tk % t == 0 and S % tk == 0 and t & (t - 1) == 0 and WINDOW % t == 0 and n_sel <= SEL_COLS
    n_tiles = S // tk
    nt_pad = -(-n_tiles // 16) * 16
    assert nt_pad + 2 < 127
    grp = jnp.asarray((np.arange(n_sel)[None, :] // (tk // SEL_BLOCK) == np.arange(nt_pad)[:, None])
                      .astype(np.float32), BF16)
    ltri = jnp.asarray((np.arange(nt_pad)[None, :] < np.arange(nt_pad)[:, None]).astype(np.float32), BF16)
    fixed = lambda b, g, i, s: (b, g, 0, 0)
    fixed5 = lambda b, g, i, s: (b, g, 0, 0, 0)
    tile5 = lambda b, g, i, s: (b, g, i, 0, 0)
    grid_spec = pltpu.PrefetchScalarGridSpec(
        num_scalar_prefetch=1,
        grid=(B, G, nq),
        in_specs=[pl.BlockSpec((None, None, None, NSA_DIM, w), tile5),
                  pl.BlockSpec((None, None, n_cp, NSA_DIM), fixed),
                  pl.BlockSpec((None, None, NSA_DIM, n_cp), fixed),
                  pl.BlockSpec((None, None, S, ksa.shape[-1]), fixed),
                  pl.BlockSpec((None, None, S // tk, NSA_DIM, tk), fixed5),
                  pl.BlockSpec((None, None, S + WINDOW, 2 * NSA_DIM), fixed),
                  pl.BlockSpec((None, None, (S + WINDOW) // t, NSA_DIM, t), fixed5),
                  pl.BlockSpec((None, WINDOW + t, w), lambda b, g, i, s: (g, 0, 0)),
                  pl.BlockSpec((None, None, None, 3, w), tile5),
                  pl.BlockSpec((n_sel, n_cp), lambda b, g, i, s: (0, 0)),
                  pl.BlockSpec((nt_pad, n_sel), lambda b, g, i, s: (0, 0)),
                  pl.BlockSpec((nt_pad, nt_pad), lambda b, g, i, s: (0, 0))],
        out_specs=pl.BlockSpec((None, None, None, NSA_DIM, w), tile5),
        scratch_shapes=[pltpu.VMEM((NSA_DIM, w), F32),
                        pltpu.VMEM((NSA_DIM, w), F32),
                        pltpu.VMEM((tk, w), F32),
                        pltpu.VMEM((2 * NSA_DIM + SEL_COLS, w), BF16),
                        pltpu.VMEM((tk, w), F32), pltpu.VMEM((tk, w), F32),
                        pltpu.VMEM((tk, w), BF16), pltpu.VMEM((tk, w), BF16),
                        pltpu.VMEM((8, 128), I32), pltpu.SMEM((8, 128), I32), pltpu.SemaphoreType.DMA(()),
                        pltpu.VMEM((n_sel, t), F32)],
    )
    return pl.pallas_call(
        functools.partial(_nsa_kernel, t=t, tk=tk, n_top=n_top),
        grid_spec=grid_spec,
        out_shape=jax.ShapeDtypeStruct((B, G, nq, NSA_DIM, w), BF16),
        compiler_params=_cparams(("parallel", "parallel", "arbitrary")),
        name="nsa",
    )(slopes, qT, kc, vcT, ksa, vsT, kwa, vwT, wb, gT, mT, grp, ltri)


def _pack_halves(v):
    c = v.shape[1] // 2
    lo = lax.bitcast_convert_type(v[:, :c].astype(BF16).astype(F32), U32)
    hi = lax.bitcast_convert_type(v[:, c:].astype(BF16).astype(F32), U32)
    return hi | (lo >> 16)


def _unpack_halves(u):
    lo = lax.bitcast_convert_type(u << 16, F32)
    hi = lax.bitcast_convert_type(u & jnp.uint32(0xFFFF0000), F32)
    return lo, hi


def _store_slabs(ref, v):
    packed = _pack_halves(v)
    for s in range(SC_SPLIT):
        ref[s] = packed[:, s * SC_PIECE:(s + 1) * SC_PIECE]


def _load_slabs_bf16(ref):
    parts = [_unpack_halves(ref[s]) for s in range(SC_SPLIT)]
    return jnp.concatenate([p[0].astype(BF16) for p in parts] + [p[1].astype(BF16) for p in parts], axis=1)


def _merge_kernel(oa_ref, ob_ref, ga_ref, gb_ref, x_ref, g1_ref, nw_ref, sh_ref, sc_ref,
                  wa_ref, wb_ref, wo_ref, wr_ref, x1_ref, h2_ref, lg_ref):
    ya = jnp.dot(oa_ref[...], wa_ref[...], preferred_element_type=F32)
    yb = jnp.dot(ob_ref[...], wb_ref[...], preferred_element_type=F32)
    merged = (jax.nn.sigmoid(ga_ref[...].astype(F32)) * ya
              + jax.nn.sigmoid(gb_ref[...].astype(F32)) * yb).astype(BF16)
    mix = jnp.dot(merged, wo_ref[...], preferred_element_type=F32)
    x1 = x_ref[...] + g1_ref[...] * mix
    x1_ref[...] = x1
    h2 = _modulate(x1, nw_ref[...], sh_ref[...], sc_ref[...])
    _store_slabs(h2_ref, h2)
    h_hi = h2.astype(BF16)
    h_lo = (h2 - h_hi.astype(F32)).astype(BF16)
    lg_ref[...] = (jnp.dot(h_hi, wr_ref[0], preferred_element_type=F32)
                   + jnp.dot(h_lo, wr_ref[0], preferred_element_type=F32)
                   + jnp.dot(h_hi, wr_ref[1], preferred_element_type=F32))


def _merge(oa, ob, slab, x2, g1, nw, sh, sc, wa, wb, wo, wr, S):
    N, D = x2.shape
    tm = 256
    per_b = S // tm
    mcol = COL_MERGE // D
    row = lambda i: (i, 0)
    full = lambda i: (0, 0)
    perb = lambda i: (i // per_b, 0, 0)
    n_lg = wr.shape[-1]
    return pl.pallas_call(
        _merge_kernel,
        grid=(N // tm,),
        in_specs=[pl.BlockSpec((tm, oa.shape[1]), row),
                  pl.BlockSpec((tm, ob.shape[1]), row),
                  pl.BlockSpec((tm, D), lambda i: (i, mcol)),
                  pl.BlockSpec((tm, D), lambda i: (i, mcol + 1)),
                  pl.BlockSpec((tm, D), row),
                  pl.BlockSpec((None, 1, D), perb),
                  pl.BlockSpec((1, D), full),
                  pl.BlockSpec((None, 1, D), perb),
                  pl.BlockSpec((None, 1, D), perb),
                  pl.BlockSpec(wa.shape, full),
                  pl.BlockSpec(wb.shape, full),
                  pl.BlockSpec(wo.shape, full),
                  pl.BlockSpec(wr.shape, lambda i: (0, 0, 0))],
        out_specs=[pl.BlockSpec((tm, D), row), pl.BlockSpec((SC_SPLIT, tm, SC_PIECE), lambda i: (0, i, 0)),
                   pl.BlockSpec((tm, n_lg), row)],
        out_shape=[jax.ShapeDtypeStruct((N, D), F32), jax.ShapeDtypeStruct((SC_SPLIT, N, SC_PIECE), U32),
                   jax.ShapeDtypeStruct((N, n_lg), F32)],
        compiler_params=_cparams(("parallel",)),
        name="merge",
    )(oa, ob, slab, slab, x2, g1.reshape(-1, 1, D), nw.reshape(1, D), sh.reshape(-1, 1, D),
      sc.reshape(-1, 1, D), wa, wb, wo, wr)


def _first_max(v, idx):
    mx = jnp.max(v, axis=0, keepdims=True)
    first = jnp.min(jnp.where(v == mx, idx, float(v.shape[0])), axis=0, keepdims=True)
    return mx, first


def _route_kernel(lg_ref, b_ref, e_ref, w_ref):
    lg = lg_ref[...]
    tn = lg.shape[1]
    per_g = N_EXPERTS // N_EXPERT_GROUPS
    scores = jax.nn.sigmoid(lg)
    biased = scores + b_ref[...]
    gi = lax.broadcasted_iota(I32, (N_EXPERT_GROUPS, tn), 0).astype(F32)
    gscore = jnp.zeros((N_EXPERT_GROUPS, tn), F32)
    for g in range(N_EXPERT_GROUPS):
        grp = biased[g * per_g:(g + 1) * per_g, :]
        m1, f1 = _first_max(grp, gi)
        m2 = jnp.max(jnp.where(gi == f1, -jnp.inf, grp), axis=0, keepdims=True)
        gscore = jnp.where(gi == g, m1 + m2, gscore)
    gsel = jnp.zeros((N_EXPERT_GROUPS, tn), F32)
    for _ in range(TOPK_EXPERT_GROUPS):
        _, f = _first_max(gscore, gi)
        hit = gi == f
        gsel = jnp.where(hit, 1.0, gsel)
        gscore = jnp.where(hit, -jnp.inf, gscore)
    ei = lax.broadcasted_iota(I32, (N_EXPERTS, tn), 0).astype(F32)
    emask = jnp.zeros((N_EXPERTS, tn), F32)
    for g in range(N_EXPERT_GROUPS):
        in_g = (ei >= g * per_g) & (ei < (g + 1) * per_g)
        emask = jnp.where(in_g, gsel[g:g + 1, :], emask)
    masked = jnp.where(emask > 0.5, biased, NEG)
    e_out = jnp.zeros((MOE_TOPK, tn), F32)
    w_out = jnp.zeros((MOE_TOPK, tn), F32)
    for r in range(MOE_TOPK):
        _, f = _first_max(masked, ei)
        hit = ei == f
        wv = jnp.sum(jnp.where(hit, scores, 0.0), axis=0, keepdims=True)
        e_out = jnp.where(gi == r, f, e_out)
        w_out = jnp.where(gi == r, wv, w_out)
        masked = jnp.where(hit, -jnp.inf, masked)
    e_ref[...] = e_out.astype(I32)
    w_ref[...] = w_out / jnp.sum(w_out, axis=0, keepdims=True) * ROUTED_SCALE


def _route(lgT, rb):
    E, N = lgT.shape
    tn = 512
    return pl.pallas_call(
        _route_kernel,
        grid=(N // tn,),
        in_specs=[pl.BlockSpec((E, tn), lambda i: (0, i)), pl.BlockSpec((E, 1), lambda i: (0, 0))],
        out_specs=[pl.BlockSpec((MOE_TOPK, tn), lambda i: (0, i)), pl.BlockSpec((MOE_TOPK, tn), lambda i: (0, i))],
        out_shape=[jax.ShapeDtypeStruct((MOE_TOPK, N), I32), jax.ShapeDtypeStruct((MOE_TOPK, N), F32)],
        compiler_params=_cparams(("parallel",)),
        name="route",
    )(lgT, rb.reshape(E, 1).astype(F32))


def _slots_kernel(e_ref, tri_ref, lt_ref, dest_ref, cnt_ref, carry_ref, *, rb):
    phase = pl.program_id(0)
    i = pl.program_id(1)
    e = e_ref[...]
    tn = e.shape[1]
    ei = lax.broadcasted_iota(I32, (N_EXPERTS, tn), 0)

    @pl.when((phase == 0) & (i == 0))
    def _():
        carry_ref[...] = jnp.zeros_like(carry_ref)

    @pl.when(phase == 0)
    def _():
        tot = jnp.zeros((N_EXPERTS, 1), F32)
        for k in range(MOE_TOPK):
            oh = jnp.where(ei == e[k:k + 1, :], 1.0, 0.0)
            tot = tot + jnp.sum(oh, axis=1, keepdims=True)
        carry_ref[...] = carry_ref[...] + tot
        dest_ref[...] = jnp.zeros_like(dest_ref)

    @pl.when((phase == 1) & (i == 0))
    def _():
        cnt = carry_ref[...]
        cnt_ref[...] = cnt
        padded = jnp.broadcast_to(jnp.ceil(cnt / rb) * rb, (N_EXPERTS, 128))
        first = jnp.dot(lt_ref[...], padded, preferred_element_type=F32, precision=HIGHEST)
        carry_ref[...] = first[:, :1]

    @pl.when(phase == 1)
    def _():
        carry = carry_ref[...]
        ki = lax.broadcasted_iota(I32, (MOE_TOPK, tn), 0)
        dest = jnp.zeros((MOE_TOPK, tn), F32)
        for k in range(MOE_TOPK):
            hit = ei == e[k:k + 1, :]
            oh = jnp.where(hit, 1.0, 0.0)
            before = jnp.dot(oh.astype(BF16), tri_ref[...], preferred_element_type=F32)
            row = jnp.sum(jnp.where(hit, before + carry, 0.0), axis=0, keepdims=True)
            dest = jnp.where(ki == k, row, dest)
            carry = carry + jnp.sum(oh, axis=1, keepdims=True)
        carry_ref[...] = carry
        dest_ref[...] = dest.astype(I32)


def _slots(top_eT, rb):
    K, N = top_eT.shape
    tn = 512
    idx = np.arange(tn)
    tri = jnp.asarray((idx[:, None] < idx[None, :]).astype(np.float32), BF16)
    e_idx = np.arange(N_EXPERTS)
    lt = jnp.asarray((e_idx[None, :] < e_idx[:, None]).astype(np.float32), F32)
    return pl.pallas_call(
        functools.partial(_slots_kernel, rb=rb),
        grid=(2, N // tn),
        in_specs=[pl.BlockSpec((K, tn), lambda p, i: (0, i)),
                  pl.BlockSpec((tn, tn), lambda p, i: (0, 0)),
                  pl.BlockSpec((N_EXPERTS, N_EXPERTS), lambda p, i: (0, 0))],
        out_specs=[pl.BlockSpec((K, tn), lambda p, i: (0, i * p)),
                   pl.BlockSpec((N_EXPERTS, 1), lambda p, i: (0, 0))],
        out_shape=[jax.ShapeDtypeStruct((K, N), I32), jax.ShapeDtypeStruct((N_EXPERTS, 1), F32)],
        scratch_shapes=[pltpu.VMEM((N_EXPERTS, 1), F32)],
        compiler_params=_cparams(("arbitrary", "arbitrary")),
        name="slots",
    )(top_eT, tri, lt)


def _sc_scatter_rows(src, idx, n_out, repeat):
    width = src.shape[1]
    n_tok = src.shape[0] // SC_SPLIT
    w = n_tok // SC_WINDOW
    mesh = plsc.VectorSubcoreMesh(core_axis_name="c", subcore_axis_name="s")

    @pl.kernel(out_type=jax.ShapeDtypeStruct((n_out, width), src.dtype), mesh=mesh)
    def scatter_kernel(x_hbm, i_hbm, o_hbm):
        def body(x_vmem, i_vmem):
            for k in range(repeat):
                pltpu.sync_copy(x_vmem, o_hbm.at[i_vmem.at[k]])

        pltpu.emit_pipeline(
            body,
            grid=(SC_SPLIT * w,),
            in_specs=[pl.BlockSpec((SC_WINDOW, width), lambda g: (g, 0)),
                      pl.BlockSpec((repeat, SC_WINDOW), lambda g: (g // w, g % w))],
            out_specs=[],
            core_axis_name=("c", "s"),
            dimension_semantics=(pltpu.PARALLEL,),
        )(x_hbm, i_hbm)

    return scatter_kernel(src, idx.reshape(SC_SPLIT * repeat, n_tok))


def _zero_pads_kernel(start_ref, len_ref, xs_in_ref, xs_ref, zeros_ref, sem, *, rb, n_rows):
    del xs_in_ref
    zeros_ref[...] = jnp.zeros_like(zeros_ref)
    n_seg = start_ref.shape[0]
    bits = rb.bit_length() - 1
    assert rb == 1 << bits

    def for_each_copy(act):
        def rows(row0, size):
            for s in range(SC_SPLIT):
                act(pltpu.make_async_copy(zeros_ref.at[pl.ds(0, size), :],
                                          xs_ref.at[pl.ds(s * n_rows + row0, size), :], sem))

        def segment(e, c):
            start = start_ref[e]
            length = len_ref[e]
            end = start + length
            n_whole = length // rb

            def whole(j, c2):
                rows(pl.multiple_of(end - (j + 1) * rb, SUBLANES), rb)
                return c2

            lax.fori_loop(0, n_whole, whole, 0)
            off = end - n_whole * rb
            rest = length - n_whole * rb
            for b in reversed(range(SUBLANES.bit_length() - 1, bits)):
                size = 1 << b
                take = jnp.bitwise_and(jnp.right_shift(rest, b), 1)
                off = off - take * size

                @pl.when(take == 1)
                def _(off=off, size=size):
                    rows(pl.multiple_of(off, SUBLANES), size)

            lead = jnp.bitwise_and(rest, SUBLANES - 1)
            for i in range(SUBLANES - 1):
                @pl.when(i < lead)
                def _(i=i):
                    rows(start + i, 1)

            return c

        lax.fori_loop(0, n_seg, segment, 0)

    for_each_copy(lambda cp: cp.start())
    for_each_copy(lambda cp: cp.wait())


def _zero_pads(xs2d, start, length, rb, n_rows):
    grid_spec = pltpu.PrefetchScalarGridSpec(
        num_scalar_prefetch=2,
        grid=(1,),
        in_specs=[pl.BlockSpec(memory_space=pl.ANY)],
        out_specs=pl.BlockSpec(memory_space=pl.ANY),
        scratch_shapes=[pltpu.VMEM((rb, xs2d.shape[1]), xs2d.dtype), pltpu.SemaphoreType.DMA(())],
    )
    return pl.pallas_call(
        functools.partial(_zero_pads_kernel, rb=rb, n_rows=n_rows),
        grid_spec=grid_spec,
        out_shape=jax.ShapeDtypeStruct(xs2d.shape, xs2d.dtype),
        input_output_aliases={2: 0},
        compiler_params=_cparams(("arbitrary",)),
        name="zeropads",
    )(start, length, xs2d)


def _experts_kernel(be_ref, x_ref, wg_ref, wu_ref, wd_ref, y_ref, wgu_sc, wd_sc):
    i = pl.program_id(0)
    hdim = wg_ref.shape[1]

    @pl.when((i == 0) | (be_ref[i] != be_ref[jnp.maximum(i - 1, 0)]))
    def _():
        wgu_sc[:, 0:hdim] = wg_ref[...].astype(BF16)
        wgu_sc[:, hdim:] = wu_ref[...].astype(BF16)
        wd_sc[...] = wd_ref[...].astype(BF16)

    @pl.when(i < be_ref[pl.num_programs(0)])
    def _():
        gu = jnp.dot(_load_slabs_bf16(x_ref), wgu_sc[...], preferred_element_type=F32)
        g = gu[:, :hdim]
        act = (g * jax.nn.sigmoid(g) * gu[:, hdim:]).astype(BF16)
        _store_slabs(y_ref, jnp.dot(act, wd_sc[...], preferred_element_type=F32))


def _experts(blk_e, xs, wg, wu, wd, rb):
    n_rows = xs.shape[1]
    d_model, hdim = wg.shape[1:]
    per_expert = lambda i, be: (be[i], 0, 0)
    grid_spec = pltpu.PrefetchScalarGridSpec(
        num_scalar_prefetch=1,
        grid=(n_rows // rb,),
        in_specs=[pl.BlockSpec((SC_SPLIT, rb, SC_PIECE), lambda i, be: (0, i, 0)),
                  pl.BlockSpec((None, d_model, hdim), per_expert),
                  pl.BlockSpec((None, d_model, hdim), per_expert),
                  pl.BlockSpec((None, hdim, d_model), per_expert)],
        out_specs=pl.BlockSpec((SC_SPLIT, rb, SC_PIECE), lambda i, be: (0, i, 0)),
        scratch_shapes=[pltpu.VMEM((d_model, 2 * hdim), BF16), pltpu.VMEM((hdim, d_model), BF16)],
    )
    return pl.pallas_call(
        _experts_kernel,
        grid_spec=grid_spec,
        out_shape=jax.ShapeDtypeStruct((SC_SPLIT, n_rows, SC_PIECE), U32),
        compiler_params=_cparams(("arbitrary",)),
        name="experts",
    )(blk_e, xs, wg, wu, wd)


def _sc_gather_rows(table, idx):
    n = idx.shape[0]
    width = table.shape[1]
    mesh = plsc.VectorSubcoreMesh(core_axis_name="c", subcore_axis_name="s")

    @pl.kernel(out_type=jax.ShapeDtypeStruct((n, width), table.dtype), mesh=mesh)
    def gather_kernel(t_hbm, i_hbm, o_hbm):
        def body(i_vmem, o_vmem):
            pltpu.sync_copy(t_hbm.at[i_vmem.at[0]], o_vmem)

        pltpu.emit_pipeline(
            body,
            grid=(n // SC_WINDOW,),
            in_specs=[pl.BlockSpec((1, SC_WINDOW), lambda i: (0, i))],
            out_specs=[pl.BlockSpec((SC_WINDOW, width), lambda i: (i, 0))],
            core_axis_name=("c", "s"),
            dimension_semantics=(pltpu.PARALLEL,),
        )(i_hbm, o_hbm)

    return gather_kernel(table, idx.reshape(1, n))


def _shared_kernel(h_ref, sgu_ref, sd_ref, o_ref):
    gu = jnp.dot(_load_slabs_bf16(h_ref), sgu_ref[...], preferred_element_type=F32)
    hdim = gu.shape[1] // 2
    g = gu[:, :hdim]
    o_ref[...] = jnp.dot((g * jax.nn.sigmoid(g) * gu[:, hdim:]).astype(BF16), sd_ref[...], preferred_element_type=F32)


def _shared_expert(h2p, sgu, sd, tt):
    N = h2p.shape[1]
    D = sd.shape[1]
    full = lambda i: (0, 0)
    return pl.pallas_call(
        _shared_kernel,
        grid=(N // tt,),
        in_specs=[pl.BlockSpec((SC_SPLIT, tt, SC_PIECE), lambda i: (0, i, 0)),
                  pl.BlockSpec(sgu.shape, full),
                  pl.BlockSpec(sd.shape, full)],
        out_specs=pl.BlockSpec((tt, D), lambda i: (i, 0)),
        out_shape=jax.ShapeDtypeStruct((N, D), F32),
        compiler_params=_cparams(("parallel",)),
        name="shared",
    )(h2p, sgu, sd)


def _combine_kernel(yg_ref, w_ref, sh_ref, x1_ref, g2_ref, fw_ref, o_ref):
    ffn = sh_ref[...]

    w = w_ref[...]
    tt = w.shape[0]
    lo = [jnp.zeros((tt, SC_PIECE), F32) for _ in range(SC_SPLIT)]
    hi = [jnp.zeros((tt, SC_PIECE), F32) for _ in range(SC_SPLIT)]
    for k in range(MOE_TOPK):
        wk = w[:, k:k + 1]
        for s in range(SC_SPLIT):
            y_lo, y_hi = _unpack_halves(yg_ref[s, k])
            lo[s] = lo[s] + wk * y_lo
            hi[s] = hi[s] + wk * y_hi
    ffn = ffn + jnp.concatenate(lo + hi, axis=1)
    x2 = x1_ref[...] + g2_ref[...] * ffn
    r = lax.rsqrt(jnp.mean(x2 * x2, axis=-1, keepdims=True) + NORM_EPS)
    o_ref[...] = x2 * r * fw_ref[...]


def _combine(yg, top_w, shared, x1, g2, fw, S, tt):
    N, D = x1.shape
    per_b = S // tt
    row = lambda i: (i, 0)
    full = lambda i: (0, 0)
    return pl.pallas_call(
        _combine_kernel,
        grid=(N // tt,),
        in_specs=[pl.BlockSpec((SC_SPLIT, MOE_TOPK, tt, SC_PIECE), lambda i: (0, 0, i, 0)),
                  pl.BlockSpec((tt, MOE_TOPK), row),
                  pl.BlockSpec((tt, D), row),
                  pl.BlockSpec((tt, D), row),
                  pl.BlockSpec((None, 1, D), lambda i: (i // per_b, 0, 0)),
                  pl.BlockSpec((1, D), full)],
        out_specs=pl.BlockSpec((tt, D), row),
        out_shape=jax.ShapeDtypeStruct((N, D), F32),
        compiler_params=_cparams(("parallel",)),
        name="combine",
    )(yg, top_w, shared, x1, g2.reshape(-1, 1, D), fw.reshape(1, D))


def _cmp_to_sel_T(n_cp, n_sel):
    c0 = np.arange(n_cp)[None, :] * CMP_STRIDE
    s0 = np.arange(n_sel)[:, None] * SEL_BLOCK
    ov = np.minimum(c0 + CMP_BLOCK, s0 + SEL_BLOCK) - np.maximum(c0, s0)
    m = np.clip(ov, 0, None).astype(np.float32) / CMP_BLOCK
    m[:, n_cp - 1] = 0.0
    return jnp.asarray(m, BF16)


def _slab_weights(w_in):
    D = w_in.shape[0]
    sizes = [1024, 1024, 1024, 512, 128, 128, 128, 128, 128, 128, 24, 2048]
    offs = np.concatenate([[0], np.cumsum(sizes)])
    part = lambda i: w_in[:, offs[i]:offs[i + 1]]
    pieces = [part(0), part(1), part(2), part(11), part(3)] + [part(i) for i in range(4, 10)] + [part(10)]
    w = jnp.concatenate(pieces, axis=1)
    return jnp.pad(w, ((0, 0), (0, SLAB_COLS - w.shape[1]))).astype(BF16)


def _layer(x, c, ada_w, ada_b, norm1_w, w_in, da_lq1, da_lk1, da_lq2, da_lk2, da_subln_w,
           cmp_k_pe, cmp_k_w1, cmp_k_w2, cmp_v_pe, cmp_v_w1, cmp_v_w2, w_da_out, w_nsa_out, w_o,
           norm2_w, router_w, router_b, exp_w_gate, exp_w_up, exp_w_down,
           sh_w_gate, sh_w_up, sh_w_down, final_norm_w, lam_init):
    B, S, D = x.shape
    N = B * S
    G, HPG = NSA_GROUPS, NSA_HPG
    x2 = x.reshape(N, D)

    mod = _ada(c, ada_w, ada_b)
    sh1, sc1, g1, sh2, sc2, g2 = jnp.split(mod, 6, axis=-1)

    slab = _inproj(x2, norm1_w, sh1, sc1, _slab_weights(w_in), S)
    slab3 = slab.reshape(B, S, SLAB_COLS)

    i_all = np.arange(1, DA_HEADS + NSA_HEADS + 1, dtype=np.float32)
    slopes = (2.0 ** (-8.0 * i_all / (DA_HEADS + NSA_HEADS))).astype(np.float32)
    slopes_a = jnp.asarray(slopes[0::2])
    slopes_b = jnp.asarray(slopes[1::2])

    ta_q, ta_k = min(512, S), min(512, S)
    vT = slab3[:, :, COL_DAV:COL_DAV + 1024].reshape(B, S // ta_k, ta_k, DA_HEADS, DA_VDIM).transpose(0, 3, 1, 4, 2)
    kmin = _first_live_tile(slopes[0::2], slab3, ta_q, ta_k)
    oa = _diff_attention(slopes_a, kmin, slab3, _with_ones_row(vT), _slope_rows(slopes[0::2]), da_lq1, da_lk1,
                         da_lq2, da_lk2, da_subln_w, lam_init, ta_q, ta_k).reshape(N, DA_HEADS * DA_VDIM)

    tb, tb_k = 256, min(256, S)
    n_cp = S // CMP_STRIDE
    n_sel = S // SEL_BLOCK

    def kv_groups(j):
        c0 = COL_KV6 + 128 * j
        return slab3[:, :, c0:c0 + 128].reshape(B, S, G, NSA_DIM).transpose(0, 2, 1, 3)

    def kv_tiles_T(a, rows):
        return a.reshape(B, G, S // rows, rows, NSA_DIM).transpose(0, 1, 2, 4, 3)

    ck_rows = kv_groups(0).reshape(B * G, n_cp, CMP_STRIDE * NSA_DIM)
    cv_rows = kv_groups(1).reshape(B * G, n_cp, CMP_STRIDE * NSA_DIM)
    kc = _compress(ck_rows, cmp_k_pe, cmp_k_w1, cmp_k_w2).reshape(B, G, n_cp, NSA_DIM).astype(BF16)
    vc = _compress(cv_rows, cmp_v_pe, cmp_v_w1, cmp_v_w2).reshape(B, G, n_cp, NSA_DIM).astype(BF16)
    nq = (slab3[:, :, COL_NSAQ:COL_NSAQ + 512] * jnp.asarray(NSA_DIM ** -0.5, BF16))
    nqT = nq.reshape(B, S // tb, tb, G, HPG, NSA_DIM).transpose(0, 3, 1, 5, 4, 2).reshape(B, G, S // tb, NSA_DIM, HPG * tb)
    gT = (slab3[:, :, COL_NSAG:COL_NSAG + 3 * NSA_HEADS].astype(F32).reshape(B, S // tb, tb, G, HPG, 3)
          .transpose(0, 3, 1, 5, 4, 2).reshape(B, G, S // tb, 3, HPG * tb))
    onehot = jnp.asarray((np.arange(S)[:, None] // SEL_BLOCK == np.arange(SEL_COLS)[None, :]).astype(np.float32), BF16)
    zeros64 = jnp.zeros((B, G, S, NSA_DIM), BF16)
    ksa = jnp.concatenate([kv_groups(2), zeros64, jnp.broadcast_to(onehot, (B, G, S, SEL_COLS))], axis=-1)
    pad_rows = jnp.zeros((B, G, WINDOW, 2 * NSA_DIM), BF16).at[..., NSA_DIM].set(1.0)
    kwa = jnp.concatenate([pad_rows, jnp.concatenate([kv_groups(4), zeros64], axis=-1)], axis=2)
    vw_pad = jnp.pad(kv_groups(5), ((0, 0), (0, 0), (WINDOW, 0), (0, 0)))
    vwT = vw_pad.reshape(B, G, (S + WINDOW) // tb, tb, NSA_DIM).transpose(0, 1, 2, 4, 3)
    r_w = np.arange(WINDOW + tb)[:, None]
    lane_w = np.arange(HPG * tb)[None, :]
    d_w = (WINDOW + lane_w % tb - r_w).astype(np.float32)
    slope_w = slopes[1::2].reshape(G, 1, HPG)[:, :, lane_w[0] // tb]
    wb = jnp.asarray(np.where((d_w >= 0) & (d_w < WINDOW), -slope_w * d_w[None], np.float32(NEG)).astype(np.float32))
    obT = _nsa_attention(slopes_b, nqT, kc, vc.transpose(0, 1, 3, 2), ksa, kv_tiles_T(kv_groups(3), tb_k),
                         kwa, vwT, wb, gT, _cmp_to_sel_T(n_cp, n_sel), tb, tb_k)
    ob = (obT.reshape(B, G, S // tb, NSA_DIM, HPG, tb).transpose(0, 2, 5, 1, 4, 3)
          .reshape(N, NSA_HEADS * NSA_DIM))

    wr_hi = router_w.astype(BF16)
    wr_lo = (router_w.astype(F32) - wr_hi.astype(F32)).astype(BF16)
    x1, h2, logits = _merge(oa, ob, slab, x2, g1, norm2_w, sh2, sc2, w_da_out.astype(BF16),
                            w_nsa_out.astype(BF16), w_o.astype(BF16), jnp.stack([wr_hi, wr_lo]), S)

    rb = 512
    tc = 256
    top_eT, top_wT = _route(logits.T, router_b)
    destT, counts = _slots(top_eT, rb)
    n_rows = ((N * MOE_TOPK + N_EXPERTS * (rb - 1) + rb - 1) // rb) * rb
    padded = (jnp.ceil(counts[:, 0] / rb) * rb).astype(I32)
    pend = jnp.cumsum(padded)
    blk_start = jnp.arange(n_rows // rb, dtype=I32) * rb
    blk_e = jnp.minimum(jnp.sum((pend[None, :] <= blk_start[:, None]).astype(I32), axis=1), N_EXPERTS - 1)
    d_flat = destT.reshape(-1)
    piece_idx = jnp.concatenate([d_flat + s * n_rows for s in range(SC_SPLIT)])
    xs2d = _sc_scatter_rows(h2.reshape(SC_SPLIT * N, SC_PIECE), piece_idx, SC_SPLIT * n_rows, MOE_TOPK)
    cnt = counts[:, 0].astype(I32)
    pad_start = jnp.concatenate([pend - padded + cnt, pend[-1:]])
    pad_len = jnp.concatenate([padded - cnt, n_rows - pend[-1:]])
    xs = _zero_pads(xs2d, pad_start, pad_len, rb, n_rows).reshape(SC_SPLIT, n_rows, SC_PIECE)
    blk_info = jnp.concatenate([blk_e, pend[-1:] // rb])
    ys = _experts(blk_info, xs, exp_w_gate, exp_w_up, exp_w_down, rb)
    sgu = jnp.concatenate([sh_w_gate, sh_w_up], axis=-1).astype(BF16)
    shared = _shared_expert(h2, sgu, sh_w_down.astype(BF16), 2 * tc)
    yg = _sc_gather_rows(ys.reshape(SC_SPLIT * n_rows, SC_PIECE), piece_idx).reshape(SC_SPLIT, MOE_TOPK, N, SC_PIECE)
    out = _combine(yg, top_wT.T, shared, x1, g2, final_norm_w, S, tc)
    return out.reshape(B, S, D)


def kernel(x, c, ada_w, ada_b, norm1_w, w_in, da_lq1, da_lk1, da_lq2, da_lk2, da_subln_w, cmp_k_pe, cmp_k_w1, cmp_k_w2, cmp_v_pe, cmp_v_w1, cmp_v_w2, w_da_out, w_nsa_out, w_o, norm2_w, router_w, router_b, exp_w_gate, exp_w_up, exp_w_down, sh_w_gate, sh_w_up, sh_w_down, final_norm_w):
    depth = ada_w.shape[0]
    assert depth == 1, "one decoder layer"
    lam_init = 0.8 - 0.6 * math.exp(-0.3 * 0)
    return _layer(x, c, ada_w[0], ada_b[0], norm1_w[0], w_in[0], da_lq1[0], da_lk1[0], da_lq2[0], da_lk2[0],
                  da_subln_w[0], cmp_k_pe[0], cmp_k_w1[0], cmp_k_w2[0], cmp_v_pe[0], cmp_v_w1[0], cmp_v_w2[0],
                  w_da_out[0], w_nsa_out[0], w_o[0], norm2_w[0], router_w[0], router_b[0],
                  exp_w_gate[0], exp_w_up[0], exp_w_down[0], sh_w_gate[0], sh_w_up[0], sh_w_down[0],
                  final_norm_w, lam_init)
```
